```python
import math
import jax, jax.numpy as jnp
from jax import lax
import numpy as np

D_MODEL = 1024
BATCH = 32
SEQ = 256
DEPTH = 4
DEC_BATCH = 2
DEC_SEQ = 1024
PAST_LEN = 256

GRID_W = 64
SGU_GROUPS = 4
SGU_GROUP_W = 64
SGU_W = SGU_GROUPS * SGU_GROUP_W
SGU_CHUNK = 128
DIFF_HEADS = 4
DIFF_QK = 64
DIFF_V = 2 * DIFF_QK
Q_BLOCK = 128
ROPE_BASE = 10000.0
GLA_HEADS = 4
GLA_DK = 64
GLA_DV = 64
GLA_RANK = 16
GLA_GATE_NORM = 16.0
GLA_CHUNK = 64
PROJ_SIZES = (SGU_W, SGU_W, DIFF_HEADS * 2 * DIFF_QK, DIFF_HEADS * 2 * DIFF_QK, DIFF_HEADS * DIFF_V,
              GLA_HEADS * GLA_DK, GLA_HEADS * GLA_DK, GLA_HEADS * GLA_DV, GLA_HEADS * GLA_DV, GLA_RANK, GLA_RANK)
D_PROJ = sum(PROJ_SIZES)
D_MIX = SGU_W + DIFF_HEADS * DIFF_V + GLA_HEADS * GLA_DV
N_EXPERTS = 16
N_GROUPS = 4
EXPERTS_PER_GROUP = N_EXPERTS // N_GROUPS
TOP_K = 2
D_EXPERT = 512
EPS = 1e-6

kernel_name = 'hybrid_diffusion_prefix_step'


def _rms(x):
    xf = x.astype(jnp.float32)
    return (xf * lax.rsqrt(jnp.mean(xf * xf, axis=-1, keepdims=True) + EPS)).astype(x.dtype)


def _modulate(h, shift, scale):
    return h * (1.0 + scale) + shift


def _adaln(cond, w, b):
    mod = (cond @ w + b)[:, None, :]
    return jnp.split(mod, 6, axis=-1)


def _split_points():
    return np.cumsum(PROJ_SIZES)[:-1].tolist()


def _spatial_gating(u, v, w_s, b_s):
    B, N, _ = u.shape
    u = jax.nn.gelu(u)
    v = _rms(jax.nn.gelu(v).reshape(B, N // SGU_CHUNK, SGU_CHUNK, SGU_GROUPS, SGU_GROUP_W))
    s = jnp.einsum('gpq,bnqgc->bnpgc', w_s, v) + b_s.T[None, None, :, :, None]
    return u * s.reshape(B, N, SGU_W)


def _axial_rope(x):
    n = x.shape[-2]
    n_rows = n // GRID_W
    pos_r = jnp.repeat(jnp.arange(n_rows), GRID_W)
    pos_c = jnp.tile(jnp.arange(GRID_W), n_rows)
    half = DIFF_QK // 2
    nf = half // 2
    freqs = ROPE_BASE ** (-jnp.arange(nf, dtype=jnp.float32) / nf)

    def rot(a, pos):
        ang = pos.astype(jnp.float32)[:, None] * freqs
        cos, sin = jnp.cos(ang), jnp.sin(ang)
        a1, a2 = a[..., :nf].astype(jnp.float32), a[..., nf:].astype(jnp.float32)
        return jnp.concatenate([a1 * cos - a2 * sin, a2 * cos + a1 * sin], axis=-1)

    out = jnp.concatenate([rot(x[..., :half], pos_r), rot(x[..., half:], pos_c)], axis=-1)
    return out.astype(x.dtype)


def _diff_attention(q, k, v, lam):
    B, H, _, Nq, dh = q.shape
    nb = Nq // Q_BLOCK
    qb = jnp.moveaxis(q.reshape(B, H, 2, nb, Q_BLOCK, dh), 3, 0)
    scale = dh ** -0.5

    def block(q_blk):
        s = jnp.einsum('bhiqd,bhikd->bhiqk', q_blk, k).astype(jnp.float32) * scale
        p = jax.nn.softmax(s, axis=-1)
        w = p[:, :, 0] - lam * p[:, :, 1]
        return jnp.einsum('bhqk,bhkv->bhqv', w.astype(v.dtype), v)

    o = lax.map(block, qb)
    return jnp.moveaxis(o, 0, 2).reshape(B, H, Nq, v.shape[-1])


def _gla_chunked(q, k, v, g, s0):
    B, H, N, _ = q.shape
    DV = v.shape[-1]
    nc = N // GLA_CHUNK
    q, k, v, g = (t.astype(jnp.float32).reshape(B, H, nc, GLA_CHUNK, t.shape[-1]) for t in (q, k, v, g))
    b = jnp.cumsum(g, axis=3)
    b_last = b[:, :, :, -1:, :]
    q_dec = q * jnp.exp(b)
    k_inv = k * jnp.exp(-b)
    k_end = k * jnp.exp(b_last - b)
    causal = jnp.tril(jnp.ones((GLA_CHUNK, GLA_CHUNK), dtype=bool))
    attn = jnp.where(causal, jnp.einsum('bhcid,bhcjd->bhcij', q_dec, k_inv), 0.0)
    o_intra = jnp.einsum('bhcij,bhcjv->bhciv', attn, v)

    def step(s, inp):
        q_c, k_c, v_c, dec_c = inp
        o_c = jnp.einsum('bhld,bhdv->bhlv', q_c, s)
        s = dec_c[..., None] * s + jnp.einsum('bhld,bhlv->bhdv', k_c, v_c)
        return s, o_c

    xs = (jnp.moveaxis(q_dec, 2, 0), jnp.moveaxis(k_end, 2, 0), jnp.moveaxis(v, 2, 0),
          jnp.moveaxis(jnp.exp(b_last[:, :, :, 0, :]), 2, 0))
    s_fin, o_inter = lax.scan(step, s0.astype(jnp.float32), xs)
    o = o_intra + jnp.moveaxis(o_inter, 0, 2)
    return o.reshape(B, H, N, DV), s_fin


def _token_mixers(h, w_in, w_out, sgu_w, sgu_b, q_g, k_g, lam, lam_init, diff_g, gla_w2, gla_b, gla_g,
                  ctx_k=None, ctx_v=None, ctx_state=None):
    B, N, _ = h.shape
    proj = h @ w_in
    a_u, a_v, bq, bk, bv, cq, ck, cv, cr, lr_f, lr_b = jnp.split(proj, _split_points(), axis=-1)

    o_a = _spatial_gating(a_u, a_v, sgu_w, sgu_b)

    q = (_rms(bq.reshape(B, N, DIFF_HEADS, 2, DIFF_QK)) * q_g).transpose(0, 2, 3, 1, 4)
    k = (_rms(bk.reshape(B, N, DIFF_HEADS, 2, DIFF_QK)) * k_g).transpose(0, 2, 3, 1, 4)
    v = bv.reshape(B, N, DIFF_HEADS, DIFF_V).transpose(0, 2, 1, 3)

    to_heads = lambda t, d: t.reshape(B, N, GLA_HEADS, d).transpose(0, 2, 1, 3)
    gq = to_heads(cq, GLA_DK) * (GLA_DK ** -0.5)
    gk = to_heads(ck, GLA_DK)
    gv = to_heads(cv, GLA_DV)
    gr = to_heads(cr, GLA_DV)
    g_f = to_heads(jax.nn.log_sigmoid((lr_f @ gla_w2[0] + gla_b[0]).astype(jnp.float32)) / GLA_GATE_NORM, GLA_DK)
    g_b = to_heads(jax.nn.log_sigmoid((lr_b @ gla_w2[1] + gla_b[1]).astype(jnp.float32)) / GLA_GATE_NORM, GLA_DK)

    if ctx_k is None:
        o_b = _diff_attention(q, k, v, lam)
        s0_f = jnp.zeros((B, GLA_HEADS, GLA_DK, GLA_DV), jnp.float32)
        s0_b = s0_f
    else:
        q_r, k_r = _axial_rope(q), _axial_rope(k)
        o_b = _diff_attention(q_r, jnp.concatenate([ctx_k, k_r], axis=3),
                              jnp.concatenate([ctx_v, v], axis=2), lam)
        s0_f, s0_b = ctx_state[:, 0], ctx_state[:, 1]
    o_b = (_rms(o_b) * diff_g * (1.0 - lam_init)).transpose(0, 2, 1, 3).reshape(B, N, DIFF_HEADS * DIFF_V)

    flip = lambda t: jnp.flip(t, axis=2)
    o_f, s_f = _gla_chunked(gq, gk, gv, g_f, s0_f)
    o_r, s_b = _gla_chunked(flip(gq), flip(gk), flip(gv), flip(g_b), s0_b)
    o_c = _rms(o_f + flip(o_r)) * gla_g * jax.nn.silu(gr.astype(jnp.float32))
    o_c = o_c.transpose(0, 2, 1, 3).reshape(B, N, GLA_HEADS * GLA_DV)

    out = (jnp.concatenate([o_a, o_b.astype(o_a.dtype), o_c.astype(o_a.dtype)], axis=-1) @ w_out).astype(h.dtype)
    if ctx_k is None:
        return out, k, v, jnp.stack([s_f, s_b], axis=1)
    return out


def _moe(h, router_w, router_bias, w1, w3, w2):
    B, N, D = h.shape
    t = h.reshape(B * N, D)
    aff = jax.nn.sigmoid((t @ router_w).astype(jnp.float32))
    sel = aff + router_bias.astype(jnp.float32)
    grp_score = lax.top_k(sel.reshape(-1, N_GROUPS, EXPERTS_PER_GROUP), 2)[0].sum(-1)
    best = jnp.argmax(grp_score, axis=-1)
    in_grp = (jnp.arange(N_EXPERTS) // EXPERTS_PER_GROUP)[None, :] == best[:, None]
    _, idx = lax.top_k(jnp.where(in_grp, sel, -jnp.inf), TOP_K)
    w = jnp.take_along_axis(aff, idx, axis=-1)
    w = w / jnp.sum(w, axis=-1, keepdims=True)
    gates = jnp.sum(jax.nn.one_hot(idx, N_EXPERTS, dtype=jnp.float32) * w[..., None], axis=1)
    hid = jax.nn.silu(jnp.einsum('td,edf->tef', t, w1)) * jnp.einsum('td,edf->tef', t, w3)
    y = jnp.einsum('tef,efd->td', hid * gates[:, :, None].astype(hid.dtype), w2)
    return y.reshape(B, N, D).astype(h.dtype)


def _normal(key, shape, scale):
    return jax.random.normal(key, shape, jnp.float32) * scale


def setup_inputs(seed: int = 0) -> dict:
    key = jax.random.key(seed)
    ks = jax.random.split(key, 32)
    D, E, F = D_MODEL, N_EXPERTS, D_EXPERT
    return {
        'x_prompt': _normal(ks[0], (BATCH, SEQ, D), 1.0),
        'x_sample': _normal(ks[1], (DEC_BATCH, DEC_SEQ, D), 1.0),
        'cache_k': _normal(ks[2], (DEC_BATCH, DEPTH, DIFF_HEADS, 2, PAST_LEN, DIFF_QK), 1.0),
        'cache_v': _normal(ks[3], (DEC_BATCH, DEPTH, DIFF_HEADS, PAST_LEN, DIFF_V), 1.0),
        'state_gla': _normal(ks[4], (DEC_BATCH, DEPTH, 2, GLA_HEADS, GLA_DK, GLA_DV), 1.0),
        'c': _normal(ks[5], (DEC_BATCH, D), 1.0),
        'c_ctx': _normal(ks[6], (D,), 1.0),
        'w_in': _normal(ks[7], (DEPTH, D, D_PROJ), D ** -0.5),
        'w_out': _normal(ks[8], (DEPTH, D_MIX, D), D_MIX ** -0.5),
        'sgu_w': _normal(ks[9], (DEPTH, SGU_GROUPS, SGU_CHUNK, SGU_CHUNK), SGU_CHUNK ** -0.5),
        'sgu_b': 1.0 + _normal(ks[10], (DEPTH, SGU_GROUPS, SGU_CHUNK), 0.1),
        'q_norm_g': 1.0 + _normal(ks[11], (DEPTH, DIFF_QK), 0.1),
        'k_norm_g': 1.0 + _normal(ks[12], (DEPTH, DIFF_QK), 0.1),
        'diff_lambda': _normal(ks[13], (DEPTH, 4, DIFF_QK), 0.1),
        'diff_norm_g': 1.0 + _normal(ks[14], (DEPTH, DIFF_V), 0.1),
        'gla_w2': _normal(ks[15], (DEPTH, 2, GLA_RANK, GLA_HEADS * GLA_DK), GLA_RANK ** -0.5),
        'gla_b': _normal(ks[16], (DEPTH, 2, GLA_HEADS * GLA_DK), 0.1),
        'gla_norm_g': 1.0 + _normal(ks[17], (DEPTH, GLA_DV), 0.1),
        'norm1_g': 1.0 + _normal(ks[18], (DEPTH, D), 0.1),
        'norm2_g': 1.0 + _normal(ks[19], (DEPTH, D), 0.1),
        'ada_w': _normal(ks[20], (DEPTH, D, 6 * D), 0.5 * D ** -0.5),
        'ada_b': _normal(ks[21], (DEPTH, 6 * D), 0.02),
        'router_w': _normal(ks[22], (D, E), D ** -0.5),
        'router_bias': _normal(ks[23], (E,), 0.01),
        'moe_w1': _normal(ks[24], (DEPTH, E, D, F), D ** -0.5),
        'moe_w3': _normal(ks[25], (DEPTH, E, D, F), D ** -0.5),
        'moe_w2': _normal(ks[26], (DEPTH, E, F, D), F ** -0.5),
    }


def reference(x_prompt, x_sample, cache_k, cache_v, state_gla, c, c_ctx, w_in, w_out, sgu_w, sgu_b,
              q_norm_g, k_norm_g, diff_lambda, diff_norm_g, gla_w2, gla_b, gla_norm_g, norm1_g, norm2_g,
              ada_w, ada_b, router_w, router_bias, moe_w1, moe_w3, moe_w2):
    cond_ctx = jax.nn.silu(c_ctx)[None, :]
    cond_lat = jax.nn.silu(c)
    xp, xs = x_prompt, x_sample
    new_k, new_v, new_s = [], [], []
    for l in range(DEPTH):
        lam_init = 0.8 - 0.6 * math.exp(-0.3 * l)
        lq1, lk1, lq2, lk2 = diff_lambda[l].astype(jnp.float32)
        lam = jnp.exp(jnp.sum(lq1 * lk1)) - jnp.exp(jnp.sum(lq2 * lk2)) + lam_init
        mix_w = (w_in[l], w_out[l], sgu_w[l], sgu_b[l], q_norm_g[l], k_norm_g[l], lam, lam_init,
                 diff_norm_g[l], gla_w2[l], gla_b[l], gla_norm_g[l])
        moe_w = (router_w, router_bias, moe_w1[l], moe_w3[l], moe_w2[l])

        sh1, sc1, g1, sh2, sc2, g2 = _adaln(cond_ctx, ada_w[l], ada_b[l])
        o, k_l, v_l, s_l = _token_mixers(_modulate(_rms(xp) * norm1_g[l], sh1, sc1), *mix_w)
        xp = xp + g1 * o
        xp = xp + g2 * _moe(_modulate(_rms(xp) * norm2_g[l], sh2, sc2), *moe_w)
        new_k.append(k_l)
        new_v.append(v_l)
        new_s.append(s_l)

        sh1, sc1, g1, sh2, sc2, g2 = _adaln(cond_lat, ada_w[l], ada_b[l])
        o = _token_mixers(_modulate(_rms(xs) * norm1_g[l], sh1, sc1), *mix_w,
                          ctx_k=cache_k[:, l], ctx_v=cache_v[:, l], ctx_state=state_gla[:, l])
        xs = xs + g1 * o
        xs = xs + g2 * _moe(_modulate(_rms(xs) * norm2_g[l], sh2, sc2), *moe_w)

    return (xp, xs, jnp.stack(new_k, axis=1), jnp.stack(new_v, axis=1), jnp.stack(new_s, axis=1))
```

```python
import functools
import math

import jax
import jax.numpy as jnp
import numpy as np
from jax import lax
from jax.experimental import pallas as pl
from jax.experimental.pallas import tpu as pltpu

F32 = jnp.float32
BF16 = jnp.bfloat16

D_MODEL = 1024
DEPTH = 4
GRID_W = 64
SGU_GROUPS = 4
SGU_GROUP_W = 64
SGU_W = SGU_GROUPS * SGU_GROUP_W
SGU_CHUNK = 128
DIFF_HEADS = 4
DIFF_QK = 64
DIFF_V = 2 * DIFF_QK
ROPE_BASE = 10000.0
GLA_HEADS = 4
GLA_DK = 64
GLA_DV = 64
GLA_RANK = 16
GLA_GATE_NORM = 16.0
GLA_CHUNK = 64
N_EXPERTS = 16
N_GROUPS = 4
EXPERTS_PER_GROUP = N_EXPERTS // N_GROUPS
D_EXPERT = 512
EPS = 1e-6

LANES = 128
SUBLANES = 8
MXU_DIM = 256
VMEM_PHYSICAL_BYTES = 64 * 1024 * 1024

ROW_BLOCK = MXU_DIM
ROW_SLABS = D_MODEL // LANES

C_AU, C_AV, C_BQ, C_BK, C_BV = 0, 256, 512, 1024, 1536
C_CQ, C_CK, C_CV, C_CR, C_LR = 2048, 2304, 2560, 2816, 3072
D_PROJ_MAIN = 3072
D_PROJ_PAD = D_PROJ_MAIN + LANES
W_QK = DIFF_HEADS * 2 * DIFF_QK
W_GLA = GLA_HEADS * GLA_DK
M_A, M_B, M_C = 0, SGU_W, SGU_W + DIFF_HEADS * DIFF_V


def _split2(x):
    hi = x.astype(BF16)
    lo = (x - hi.astype(F32)).astype(BF16)
    return hi, lo


def _split3(x):
    hi = x.astype(BF16)
    r = x - hi.astype(F32)
    mid = r.astype(BF16)
    lo = (r - mid.astype(F32)).astype(BF16)
    return hi, mid, lo


def _dot(a, b):
    return jnp.dot(a, b, preferred_element_type=F32)


def _dot_nt(a, b):
    return lax.dot_general(a, b, (((1,), (1,)), ((), ())), preferred_element_type=F32)


def _dot_tn(a, b):
    return lax.dot_general(a, b, (((0,), (0,)), ((), ())), preferred_element_type=F32)


def _iota(shape, dim):
    return lax.broadcasted_iota(jnp.int32, shape, dim)


def _block_ones(width, block):
    r = _iota((width, width), 0) // block
    c = _iota((width, width), 1) // block
    return (r == c)


def _group_sum(z, block):
    width = z.shape[-1]
    outs = []
    for s in range(0, width, MXU_DIM):
        w = min(MXU_DIM, width - s)
        ones = _block_ones(w, block).astype(BF16)
        hi, lo = _split2(z[:, s:s + w])
        outs.append(_dot(hi, ones) + _dot(lo, ones))
    return outs[0] if len(outs) == 1 else jnp.concatenate(outs, axis=-1)


def _group_rms(z, block):
    ms = _group_sum(z * z, block) * (1.0 / block)
    return z * lax.rsqrt(ms + EPS)


def _row_rms(z):
    return z * lax.rsqrt(jnp.mean(z * z, axis=-1, keepdims=True) + EPS)


def _log_sigmoid(x):
    return jnp.minimum(x, 0.0) - jnp.log1p(jnp.exp(-jnp.abs(x)))


ADA_COLS = 1536


def _adaln_kernel(cond_ref, w_ref, b_ref, o_ref):
    sc = jax.nn.silu(cond_ref[...])
    c_hi, c_lo = _split2(sc)
    w = w_ref[0]
    w_hi = w.astype(BF16)
    w_lo = (w - w_hi.astype(F32)).astype(BF16)
    o_ref[0] = _dot(c_hi, w_hi) + _dot(c_lo, w_hi) + _dot(c_hi, w_lo) + b_ref[0]


def _adaln_call(cond, ada_w, ada_b):
    n_col = 6 * D_MODEL // ADA_COLS
    return pl.pallas_call(
        _adaln_kernel,
        grid=(DEPTH, n_col),
        in_specs=[
            pl.BlockSpec((SUBLANES, D_MODEL), lambda l, j: (0, 0)),
            pl.BlockSpec((1, D_MODEL, ADA_COLS), lambda l, j: (l, 0, j)),
            pl.BlockSpec((1, 1, ADA_COLS), lambda l, j: (l, 0, j)),
        ],
        out_specs=pl.BlockSpec((1, SUBLANES, ADA_COLS), lambda l, j: (l, 0, j)),
        out_shape=jax.ShapeDtypeStruct((DEPTH, SUBLANES, 6 * D_MODEL), F32),
        compiler_params=pltpu.CompilerParams(
            dimension_semantics=("arbitrary", "arbitrary"), vmem_limit_bytes=40 * 1024 * 1024),
        name="adaln",
    )(cond, ada_w, ada_b.reshape(DEPTH, 1, 6 * D_MODEL))


def _route(hn, rwt_ref, rb_ref):
    h_hi, h_lo = _split2(hn)
    rw = rwt_ref[...]
    rw_hi = rw.astype(BF16)
    rw_lo = (rw - rw_hi.astype(F32)).astype(BF16)
    logits = _dot_nt(rw_hi, h_hi) + _dot_nt(rw_hi, h_lo) + _dot_nt(rw_lo, h_hi)
    aff = jax.nn.sigmoid(logits)
    sel = aff + rb_ref[...]
    n_tok = sel.shape[1]

    def top2_sum(a, b, c, d):
        hi1, lo1 = jnp.maximum(a, b), jnp.minimum(a, b)
        hi2, lo2 = jnp.maximum(c, d), jnp.minimum(c, d)
        return jnp.maximum(hi1, hi2) + jnp.maximum(jnp.minimum(hi1, hi2), jnp.maximum(lo1, lo2))

    scores = []
    for g in range(N_GROUPS):
        rows = [sel[EXPERTS_PER_GROUP * g + j:EXPERTS_PER_GROUP * g + j + 1, :] for j in range(EXPERTS_PER_GROUP)]
        scores.append(top2_sum(*rows))
    best = jnp.zeros((1, n_tok), jnp.int32)
    best_score = scores[0]
    for g in range(1, N_GROUPS):
        upd = scores[g] > best_score
        best = jnp.where(upd, g, best)
        best_score = jnp.where(upd, scores[g], best_score)

    eid_i = _iota((N_EXPERTS, n_tok), 0)
    eid = eid_i.astype(F32)
    neg = jnp.float32(-jnp.inf)
    msel = jnp.where(eid_i // EXPERTS_PER_GROUP == best, sel, neg)
    m1 = jnp.max(msel, axis=0, keepdims=True)
    idx1 = jnp.min(jnp.where(msel == m1, eid, float(N_EXPERTS)), axis=0, keepdims=True)
    msel2 = jnp.where(eid == idx1, neg, msel)
    m2 = jnp.max(msel2, axis=0, keepdims=True)
    idx2 = jnp.min(jnp.where(msel2 == m2, eid, float(N_EXPERTS)), axis=0, keepdims=True)
    w1 = jnp.sum(jnp.where(eid == idx1, aff, 0.0), axis=0, keepdims=True)
    w2 = jnp.sum(jnp.where(eid == idx2, aff, 0.0), axis=0, keepdims=True)
    wsum = w1 + w2
    return idx1.astype(jnp.int32), idx2.astype(jnp.int32), w1 / wsum, w2 / wsum


def _mixer_kernel(n_tok, latent, lam_init, *refs):
    it = iter(refs)
    x_ref, mod_ref, n1_ref, n2_ref, win_ref, wout_ref = (next(it) for _ in range(6))
    sw_ref, sb_ref, qg_ref, kg_ref, dl_ref, dg_ref = (next(it) for _ in range(6))
    w2c_ref, gb_ref, gg_ref, rwt_ref, rb_ref = (next(it) for _ in range(5))
    if latent:
        ck_ref, cv_ref, st0_ref, cos_ref, sin_ref = (next(it) for _ in range(5))
    xo_ref, hn_ref, ridx_ref, rwt_out_ref, wcol_ref = (next(it) for _ in range(5))
    if not latent:
        ko_ref, vo_ref, so_ref = (next(it) for _ in range(3))
    proj_ref, mix_ref, q_ref, k_ref, v_ref = (next(it) for _ in range(5))
    gq_ref, gki_ref, gke_ref, gv_ref, dec_ref, go_ref, st_ref = (next(it) for _ in range(7))

    n_blk = n_tok // ROW_BLOCK
    n_ctx = k_ref.shape[0] - n_tok
    mod = mod_ref[0]

    for r in range(n_blk):
        rows = pl.ds(r * ROW_BLOCK, ROW_BLOCK)
        h = _row_rms(x_ref[0, rows, :]) * n1_ref[...]
        h = h * (1.0 + mod[1:2, :]) + mod[0:1, :]
        proj_ref[rows, :] = _dot(h.astype(BF16), win_ref[...])

    lane_group = _iota((SGU_CHUNK, SGU_W), 1) // SGU_GROUP_W

    def sgu_chunk(c, carry):
        rows = pl.ds(pl.multiple_of(c * SGU_CHUNK, SGU_CHUNK), SGU_CHUNK)
        u = jax.nn.gelu(proj_ref[rows, C_AU:C_AU + SGU_W])
        v = _group_rms(jax.nn.gelu(proj_ref[rows, C_AV:C_AV + SGU_W]), SGU_GROUP_W).astype(BF16)
        s = sb_ref[...]
        for g in range(SGU_GROUPS):
            s = s + jnp.where(lane_group == g, _dot(sw_ref[g], v), 0.0)
        mix_ref[rows, M_A:M_A + SGU_W] = (u * s).astype(BF16)
        return carry

    lax.fori_loop(0, n_tok // SGU_CHUNK, sgu_chunk, 0)

    dl = dl_ref[...]
    lam = (jnp.exp(jnp.sum(dl[0:1] * dl[1:2], axis=-1, keepdims=True))
           - jnp.exp(jnp.sum(dl[2:3] * dl[3:4], axis=-1, keepdims=True)) + lam_init)

    if latent:
        for h in range(DIFF_HEADS):
            k_ref[0:n_ctx, h * DIFF_V:(h + 1) * DIFF_V] = ck_ref[0, 0, h].astype(BF16)
            v_ref[0:n_ctx, h * DIFF_V:(h + 1) * DIFF_V] = cv_ref[0, 0, h].astype(BF16)
        pair_lo = (_iota((ROW_BLOCK, W_QK), 1) % (DIFF_QK // 2)) < (DIFF_QK // 4)

        def rope(z, rows):
            cos = jnp.concatenate([cos_ref[rows, :]] * DIFF_HEADS, axis=-1)
            sin = jnp.concatenate([sin_ref[rows, :]] * DIFF_HEADS, axis=-1)
            shift = DIFF_QK // 4
            swapped = jnp.where(pair_lo, pltpu.roll(z, W_QK - shift, 1), pltpu.roll(z, shift, 1))
            return z * cos + swapped * sin

    for r in range(n_blk):
        rows = pl.ds(r * ROW_BLOCK, ROW_BLOCK)
        qn = _group_rms(proj_ref[rows, C_BQ:C_BQ + W_QK], DIFF_QK) * qg_ref[...]
        kn = _group_rms(proj_ref[rows, C_BK:C_BK + W_QK], DIFF_QK) * kg_ref[...]
        vv = proj_ref[rows, C_BV:C_BV + W_QK]
        if latent:
            qn, kn = rope(qn, rows), rope(kn, rows)
        else:
            for h in range(DIFF_HEADS):
                for i in range(2):
                    lo = h * DIFF_V + i * DIFF_QK
                    ko_ref[0, h, i, rows, :] = kn[:, lo:lo + DIFF_QK]
                vo_ref[0, h, rows, :] = vv[:, h * DIFF_V:(h + 1) * DIFF_V]
        q_ref[rows, :] = (qn * (DIFF_QK ** -0.5)).astype(BF16)
        k_ref[pl.ds(n_ctx + r * ROW_BLOCK, ROW_BLOCK), :] = kn.astype(BF16)
        v_ref[pl.ds(n_ctx + r * ROW_BLOCK, ROW_BLOCK), :] = vv.astype(BF16)

    sub0 = (_iota((ROW_BLOCK, DIFF_V), 1) < DIFF_QK)

    def softmax(s):
        e = jnp.exp(s - jnp.max(s, axis=-1, keepdims=True))
        return e, jnp.sum(e, axis=-1, keepdims=True)

    def attn_block(r, carry):
        rows = pl.ds(pl.multiple_of(r * ROW_BLOCK, ROW_BLOCK), ROW_BLOCK)
        for h in range(DIFF_HEADS):
            cols = slice(h * DIFF_V, (h + 1) * DIFF_V)
            qh = q_ref[rows, cols]
            kh = k_ref[:, cols]
            e0, z0 = softmax(_dot_nt(jnp.where(sub0, qh, jnp.zeros_like(qh)), kh))
            e1, z1 = softmax(_dot_nt(jnp.where(sub0, jnp.zeros_like(qh), qh), kh))
            w = e0 / z0 - lam * (e1 / z1)
            o = _dot(w.astype(BF16), v_ref[:, cols])
            o = _row_rms(o) * dg_ref[...] * (1.0 - lam_init)
            mix_ref[rows, M_B + h * DIFF_V:M_B + (h + 1) * DIFF_V] = o.astype(BF16)
        return carry

    lax.fori_loop(0, n_blk, attn_block, 0)

    blk_r = _iota((ROW_BLOCK, ROW_BLOCK), 0)
    blk_c = _iota((ROW_BLOCK, ROW_BLOCK), 1)
    same_chunk = (blk_r // GLA_CHUNK) == (blk_c // GLA_CHUNK)
    tri = (jnp.where(same_chunk & (blk_c <= blk_r), 1.0, 0.0).astype(BF16),
           jnp.where(same_chunk & (blk_c >= blk_r), 1.0, 0.0).astype(BF16))
    chunks_per_blk = ROW_BLOCK // GLA_CHUNK
    for r in range(n_blk):
        rows = pl.ds(r * ROW_BLOCK, ROW_BLOCK)
        gpre = _dot(proj_ref[rows, C_LR:C_LR + LANES].astype(BF16), w2c_ref[...]) + gb_ref[...]
        gate = _log_sigmoid(gpre) * (1.0 / GLA_GATE_NORM)
        gq = proj_ref[rows, C_CQ:C_CQ + W_GLA] * (GLA_DK ** -0.5)
        gk = proj_ref[rows, C_CK:C_CK + W_GLA]
        gv_ref[rows, :] = proj_ref[rows, C_CV:C_CV + W_GLA].astype(BF16)
        for d in range(2):
            g = gate[:, d * W_GLA:(d + 1) * W_GLA]
            b = sum(_dot(tri[d], p) for p in _split3(g))
            last = GLA_CHUNK - 1 if d == 0 else 0
            b_last = jnp.concatenate(
                [jnp.broadcast_to(b[c * GLA_CHUNK + last:c * GLA_CHUNK + last + 1, :], (GLA_CHUNK, W_GLA))
                 for c in range(chunks_per_blk)], axis=0)
            gq_ref[d, rows, :] = (gq * jnp.exp(b)).astype(BF16)
            gki_ref[d, rows, :] = (gk * jnp.exp(-b)).astype(BF16)
            gke_ref[d, rows, :] = (gk * jnp.exp(b_last - b)).astype(BF16)
            for c in range(chunks_per_blk):
                row = c * GLA_CHUNK + last
                dec_ref[d, r * chunks_per_blk + c] = jnp.exp(b[row:row + 1, :])

    if latent:
        st_ref[...] = st0_ref[0, 0]
    else:
        st_ref[...] = jnp.zeros(st_ref.shape, F32)

    n_chunk = n_tok // GLA_CHUNK
    head_of_lane = _iota((GLA_CHUNK, W_GLA), 1) // GLA_DK
    stack_r = _iota((GLA_HEADS * GLA_CHUNK, GLA_CHUNK), 0) % GLA_CHUNK
    stack_c = _iota((GLA_HEADS * GLA_CHUNK, GLA_CHUNK), 1)
    causal = (stack_c <= stack_r, stack_c >= stack_r)
    st_diag = (_iota((W_GLA, W_GLA), 0) // GLA_DV) == (_iota((W_GLA, W_GLA), 1) // GLA_DK)

    def gla_step(c, carry):
        for d in range(2):
            cc = c if d == 0 else n_chunk - 1 - c
            rows = pl.ds(pl.multiple_of(cc * GLA_CHUNK, GLA_CHUNK), GLA_CHUNK)
            qd = gq_ref[d, rows, :]
            vv = gv_ref[rows, :]
            q_stack = jnp.concatenate(
                [jnp.where(head_of_lane == h, qd, jnp.zeros_like(qd)) for h in range(GLA_HEADS)], axis=0)
            attn = jnp.where(causal[d], _dot_nt(q_stack, gki_ref[d, rows, :]), 0.0)
            spread = _dot(attn.astype(BF16), vv)
            o = jnp.zeros((GLA_CHUNK, W_GLA), F32)
            for h in range(GLA_HEADS):
                o = o + jnp.where(head_of_lane == h, spread[h * GLA_CHUNK:(h + 1) * GLA_CHUNK, :], 0.0)
            st = st_ref[d]
            o = o + _dot_nt(qd, st.astype(BF16))
            go_ref[d, rows, :] = o
            upd = _dot_tn(vv, gke_ref[d, rows, :])
            st_ref[d] = dec_ref[d, cc] * st + jnp.where(st_diag, upd, 0.0)
        return carry

    lax.fori_loop(0, n_chunk, gla_step, 0)

    if not latent:
        for d in range(2):
            s_full = st_ref[d].T
            for h in range(GLA_HEADS):
                so_ref[0, d, h] = s_full[h * GLA_DK:(h + 1) * GLA_DK, h * GLA_DV:(h + 1) * GLA_DV]

    for r in range(n_blk):
        rows = pl.ds(r * ROW_BLOCK, ROW_BLOCK)
        oc = _group_rms(go_ref[0, rows, :] + go_ref[1, rows, :], GLA_DV) * gg_ref[...]
        oc = oc * jax.nn.silu(proj_ref[rows, C_CR:C_CR + W_GLA])
        mix_ref[rows, M_C:M_C + W_GLA] = oc.astype(BF16)
        x1 = x_ref[0, rows, :] + mod[2:3, :] * _dot(mix_ref[rows, :], wout_ref[...])
        xo_ref[0, rows, :] = x1
        hn = _row_rms(x1) * n2_ref[...]
        hn = hn * (1.0 + mod[4:5, :]) + mod[3:4, :]
        hn_ref[0, rows, :] = hn
        idx1, idx2, w1, w2 = _route(hn, rwt_ref, rb_ref)
        ridx_ref[0, :, rows] = jnp.concatenate([idx1, idx2], axis=0)
        rwt_out_ref[0, :, rows] = jnp.concatenate([w1, w2], axis=0)
        wpad = jnp.concatenate([w1, w2, jnp.zeros((LANES - 2, ROW_BLOCK), F32)], axis=0)
        wcol_ref[0, rows, :] = wpad.T


def _mixer_call(n_tok, latent, lam_init, x, mods, cond_of_seq, weights, extras):
    n_seq = x.shape[0]
    n_keys = n_tok + (extras[0].shape[3] if latent else 0)
    n_chunk = n_tok // GLA_CHUNK

    single = pl.Buffered(1)
    seq_mode = single if latent else None

    def const(shape):
        return pl.BlockSpec(shape, lambda s, _n=len(shape): (0,) * _n, pipeline_mode=single)

    in_specs = [
        pl.BlockSpec((1, n_tok, D_MODEL), lambda s: (s, 0, 0), pipeline_mode=seq_mode),
        pl.BlockSpec((1, 6, D_MODEL), lambda s: (cond_of_seq(s), 0, 0)),
        const((1, D_MODEL)), const((1, D_MODEL)),
        const((D_MODEL, D_PROJ_PAD)), const((D_MODEL, D_MODEL)),
        const((SGU_GROUPS, SGU_CHUNK, SGU_CHUNK)), const((SGU_CHUNK, SGU_W)),
        const((1, W_QK)), const((1, W_QK)), const((4, DIFF_QK)), const((1, DIFF_V)),
        const((LANES, 2 * W_GLA)), const((1, 2 * W_GLA)), const((1, W_GLA)),
        const((N_EXPERTS, D_MODEL)), const((N_EXPERTS, 1)),
    ]
    operands = [x, mods] + list(weights)
    if latent:
        ck, cv, st0, cos, sin = extras
        in_specs += [
            pl.BlockSpec((1, 1) + ck.shape[2:], lambda s: (s, 0, 0, 0, 0)),
            pl.BlockSpec((1, 1) + cv.shape[2:], lambda s: (s, 0, 0, 0, 0)),
            pl.BlockSpec((1, 1) + st0.shape[2:], lambda s: (s, 0, 0, 0, 0)),
            const(cos.shape), const(sin.shape),
        ]
        operands += [ck, cv, st0, cos, sin]

    out_shape = [
        jax.ShapeDtypeStruct((n_seq, n_tok, D_MODEL), F32),
        jax.ShapeDtypeStruct((n_seq, n_tok, D_MODEL), F32),
        jax.ShapeDtypeStruct((n_seq, 2, n_tok), jnp.int32),
        jax.ShapeDtypeStruct((n_seq, 2, n_tok), F32),
        jax.ShapeDtypeStruct((n_seq, n_tok, LANES), F32),
    ]
    out_specs = [
        pl.BlockSpec((1, n_tok, D_MODEL), lambda s: (s, 0, 0), pipeline_mode=seq_mode),
        pl.BlockSpec((1, n_tok, D_MODEL), lambda s: (s, 0, 0), pipeline_mode=seq_mode),
        pl.BlockSpec((1, 2, n_tok), lambda s: (s, 0, 0)),
        pl.BlockSpec((1, 2, n_tok), lambda s: (s, 0, 0)),
        pl.BlockSpec((1, n_tok, LANES), lambda s: (s, 0, 0)),
    ]
    if not latent:
        out_shape += [
            jax.ShapeDtypeStruct((n_seq, DIFF_HEADS, 2, n_tok, DIFF_QK), F32),
            jax.ShapeDtypeStruct((n_seq, DIFF_HEADS, n_tok, DIFF_V), F32),
            jax.ShapeDtypeStruct((n_seq, 2, GLA_HEADS, GLA_DK, GLA_DV), F32),
        ]
        out_specs += [
            pl.BlockSpec((1, DIFF_HEADS, 2, n_tok, DIFF_QK), lambda s: (s, 0, 0, 0, 0)),
            pl.BlockSpec((1, DIFF_HEADS, n_tok, DIFF_V), lambda s: (s, 0, 0, 0)),
            pl.BlockSpec((1, 2, GLA_HEADS, GLA_DK, GLA_DV), lambda s: (s, 0, 0, 0, 0)),
        ]
    scratch = [
        pltpu.VMEM((n_tok, D_PROJ_PAD), F32),
        pltpu.VMEM((n_tok, D_MODEL), BF16),
        pltpu.VMEM((n_tok, W_QK), BF16),
        pltpu.VMEM((n_keys, W_QK), BF16),
        pltpu.VMEM((n_keys, W_QK), BF16),
        pltpu.VMEM((2, n_tok, W_GLA), BF16),
        pltpu.VMEM((2, n_tok, W_GLA), BF16),
        pltpu.VMEM((2, n_tok, W_GLA), BF16),
        pltpu.VMEM((n_tok, W_GLA), BF16),
        pltpu.VMEM((2, n_chunk, 1, W_GLA), F32),
        pltpu.VMEM((2, n_tok, W_GLA), F32),
        pltpu.VMEM((2, W_GLA, W_GLA), F32),
    ]
    return pl.pallas_call(
        functools.partial(_mixer_kernel, n_tok, latent, lam_init),
        grid=(n_seq,),
        in_specs=in_specs,
        out_specs=out_specs,
        out_shape=out_shape,
        scratch_shapes=scratch,
        compiler_params=pltpu.CompilerParams(
            dimension_semantics=("arbitrary",), vmem_limit_bytes=56 * 1024 * 1024),
        name="mixer_latent" if latent else "mixer_context",
    )(*operands)


RANK_TILE = 512


def _rank_kernel(idx_ref, rank_ref, count_ref, carry_ref):
    @pl.when(pl.program_id(0) == 0)
    def _():
        carry_ref[...] = jnp.zeros(carry_ref.shape, F32)

    idx = idx_ref[...]
    eid = _iota((N_EXPERTS, RANK_TILE), 0)
    hot0 = eid == idx[0:1, :]
    hot1 = eid == idx[1:2, :]
    hot = jnp.where(hot0 | hot1, 1.0, 0.0).astype(BF16)
    before = (_iota((RANK_TILE, RANK_TILE), 0) < _iota((RANK_TILE, RANK_TILE), 1))
    prefix = _dot(hot, jnp.where(before, 1.0, 0.0).astype(BF16)) + carry_ref[:, 0:1]
    r0 = jnp.sum(jnp.where(hot0, prefix, 0.0), axis=0, keepdims=True)
    r1 = jnp.sum(jnp.where(hot1, prefix, 0.0), axis=0, keepdims=True)
    rank_ref[...] = jnp.concatenate([r0, r1], axis=0).astype(jnp.int32)
    total = carry_ref[...] + jnp.sum(hot.astype(F32), axis=1, keepdims=True)
    carry_ref[...] = total
    count_ref[...] = total.astype(jnp.int32)


def _rank_call(idx):
    n = idx.shape[1]
    return pl.pallas_call(
        _rank_kernel,
        grid=(n // RANK_TILE,),
        in_specs=[pl.BlockSpec((2, RANK_TILE), lambda i: (0, i))],
        out_specs=[pl.BlockSpec((2, RANK_TILE), lambda i: (0, i)),
                   pl.BlockSpec((N_EXPERTS, LANES), lambda i: (0, 0))],
        out_shape=[jax.ShapeDtypeStruct((2, n), jnp.int32),
                   jax.ShapeDtypeStruct((N_EXPERTS, LANES), jnp.int32)],
        scratch_shapes=[pltpu.VMEM((N_EXPERTS, LANES), F32)],
        compiler_params=pltpu.CompilerParams(dimension_semantics=("arbitrary",)),
        name="moe_rank",
    )(idx)


def _to_row_tiles(ref, value):
    for s in range(ROW_SLABS):
        ref[:, s, :] = value[:, s * LANES:(s + 1) * LANES]


def _from_row_tiles(ref):
    return jnp.concatenate([ref[:, s, :] for s in range(ROW_SLABS)], axis=-1)


def _dispatch_kernel(n_tok, dst_ref, hn_ref, xs_in_ref, xs_ref, buf_ref, sem):
    del xs_in_ref
    base = pl.program_id(0) * ROW_BLOCK
    _to_row_tiles(buf_ref, hn_ref[...])

    def copy(r, k):
        return pltpu.make_async_copy(buf_ref.at[r], xs_ref.at[dst_ref[k * n_tok + base + r]], sem)

    def start(r, carry):
        copy(r, 0).start()
        copy(r, 1).start()
        return carry

    def wait(r, carry):
        copy(r, 0).wait()
        copy(r, 1).wait()
        return carry

    lax.fori_loop(0, ROW_BLOCK, start, 0)
    lax.fori_loop(0, ROW_BLOCK, wait, 0)


def _dispatch_call(hn, dst, n_slots):
    n_tok = hn.shape[0]
    xs_init = jnp.zeros((n_slots, ROW_SLABS, LANES), F32)
    return pl.pallas_call(
        functools.partial(_dispatch_kernel, n_tok),
        grid_spec=pltpu.PrefetchScalarGridSpec(
            num_scalar_prefetch=1,
            grid=(n_tok // ROW_BLOCK,),
            in_specs=[pl.BlockSpec((ROW_BLOCK, D_MODEL), lambda i, dst: (i, 0)),
                      pl.BlockSpec(memory_space=pl.ANY)],
            out_specs=pl.BlockSpec(memory_space=pl.ANY),
            scratch_shapes=[pltpu.VMEM((ROW_BLOCK, ROW_SLABS, LANES), F32), pltpu.SemaphoreType.DMA],
        ),
        out_shape=jax.ShapeDtypeStruct((n_slots, ROW_SLABS, LANES), F32),
        input_output_aliases={2: 0},
        compiler_params=pltpu.CompilerParams(dimension_semantics=("arbitrary",)),
        name="moe_dispatch",
    )(dst.reshape(-1), hn, xs_init)


def _expert_kernel(te_ref, tv_ref, xs_ref, w1_ref, w3_ref, w2_ref, ys_ref, w1b_ref, w3b_ref, w2b_ref):
    i = pl.program_id(0)
    prev = te_ref[jnp.maximum(i - 1, 0)]

    @pl.when((i == 0) | (te_ref[i] != prev))
    def _():
        w1b_ref[...] = w1_ref[0].astype(BF16)
        w3b_ref[...] = w3_ref[0].astype(BF16)
        w2b_ref[...] = w2_ref[0].astype(BF16)

    @pl.when(tv_ref[i] != 0)
    def _():
        x = _from_row_tiles(xs_ref).astype(BF16)
        hid = jax.nn.silu(_dot(x, w1b_ref[...])) * _dot(x, w3b_ref[...])
        _to_row_tiles(ys_ref, _dot(hid.astype(BF16), w2b_ref[...]))

    @pl.when(tv_ref[i] == 0)
    def _():
        ys_ref[...] = jnp.zeros(ys_ref.shape, F32)


def _expert_call(xs, tile_expert, tile_valid, w1, w3, w2):
    n_tiles = tile_expert.shape[0]
    return pl.pallas_call(
        _expert_kernel,
        grid_spec=pltpu.PrefetchScalarGridSpec(
            num_scalar_prefetch=2,
            grid=(n_tiles,),
            in_specs=[
                pl.BlockSpec((ROW_BLOCK, ROW_SLABS, LANES), lambda i, te, tv: (i, 0, 0)),
                pl.BlockSpec((1, D_MODEL, D_EXPERT), lambda i, te, tv: (te[i], 0, 0)),
                pl.BlockSpec((1, D_MODEL, D_EXPERT), lambda i, te, tv: (te[i], 0, 0)),
                pl.BlockSpec((1, D_EXPERT, D_MODEL), lambda i, te, tv: (te[i], 0, 0)),
            ],
            out_specs=pl.BlockSpec((ROW_BLOCK, ROW_SLABS, LANES), lambda i, te, tv: (i, 0, 0)),
            scratch_shapes=[pltpu.VMEM((D_MODEL, D_EXPERT), BF16), pltpu.VMEM((D_MODEL, D_EXPERT), BF16),
                            pltpu.VMEM((D_EXPERT, D_MODEL), BF16)],
        ),
        out_shape=jax.ShapeDtypeStruct(xs.shape, F32),
        compiler_params=pltpu.CompilerParams(
            dimension_semantics=("arbitrary",), vmem_limit_bytes=40 * 1024 * 1024),
        name="moe_experts",
    )(tile_expert, tile_valid, xs, w1, w3, w2)


def _combine_kernel(n_tok, dst_ref, x_ref, wcol_ref, mod_ref, ys_ref, xo_ref, buf_ref, sem):
    base = pl.program_id(0) * ROW_BLOCK

    def copy(r, k):
        return pltpu.make_async_copy(ys_ref.at[dst_ref[k * n_tok + base + r]], buf_ref.at[k, r], sem)

    def start(r, carry):
        copy(r, 0).start()
        copy(r, 1).start()
        return carry

    def wait(r, carry):
        copy(r, 0).wait()
        copy(r, 1).wait()
        return carry

    lax.fori_loop(0, ROW_BLOCK, start, 0)
    lax.fori_loop(0, ROW_BLOCK, wait, 0)
    wcol = wcol_ref[...]
    y = wcol[:, 0:1] * _from_row_tiles(buf_ref.at[0]) + wcol[:, 1:2] * _from_row_tiles(buf_ref.at[1])
    xo_ref[...] = x_ref[...] + mod_ref[0, 5:6, :] * y


def _combine_call(x, wcol, mods, cond_of_tile, ys, dst):
    n_tok = x.shape[0]
    return pl.pallas_call(
        functools.partial(_combine_kernel, n_tok),
        grid_spec=pltpu.PrefetchScalarGridSpec(
            num_scalar_prefetch=1,
            grid=(n_tok // ROW_BLOCK,),
            in_specs=[pl.BlockSpec((ROW_BLOCK, D_MODEL), lambda i, dst: (i, 0)),
                      pl.BlockSpec((ROW_BLOCK, LANES), lambda i, dst: (i, 0)),
                      pl.BlockSpec((1, 6, D_MODEL), lambda i, dst: (cond_of_tile(i), 0, 0)),
                      pl.BlockSpec(memory_space=pl.ANY)],
            out_specs=pl.BlockSpec((ROW_BLOCK, D_MODEL), lambda i, dst: (i, 0)),
            scratch_shapes=[pltpu.VMEM((2, ROW_BLOCK, ROW_SLABS, LANES), F32), pltpu.SemaphoreType.DMA],
        ),
        out_shape=jax.ShapeDtypeStruct(x.shape, F32),
        compiler_params=pltpu.CompilerParams(dimension_semantics=("arbitrary",)),
        name="moe_combine",
    )(dst.reshape(-1), x, wcol, mods, ys)


def _moe_plan(idx, rank, counts):
    n_tok = idx.shape[1]
    n_tiles = 2 * n_tok // ROW_BLOCK + N_EXPERTS
    padded = (counts + ROW_BLOCK - 1) // ROW_BLOCK * ROW_BLOCK
    ends = jnp.cumsum(padded)
    dst = jnp.take(ends - padded, idx) + rank
    tile_start = jnp.arange(n_tiles, dtype=jnp.int32) * ROW_BLOCK
    tile_valid = (tile_start < ends[-1]).astype(jnp.int32)
    tile_expert = jnp.sum((tile_start[:, None] >= ends[None, :]).astype(jnp.int32), axis=1)
    last_valid = jnp.maximum(ends[-1] // ROW_BLOCK - 1, 0)
    tile_expert = jnp.where(tile_valid != 0, tile_expert, tile_expert[last_valid])
    return dst, tile_expert.astype(jnp.int32), tile_valid, n_tiles * ROW_BLOCK


def _rope_tables(n_tok):
    n_rows = n_tok // GRID_W
    pos_r = jnp.repeat(jnp.arange(n_rows), GRID_W)
    pos_c = jnp.tile(jnp.arange(GRID_W), n_rows)
    half = DIFF_QK // 2
    nf = half // 2
    freqs = ROPE_BASE ** (-jnp.arange(nf, dtype=F32) / nf)

    def tables(pos):
        ang = pos.astype(F32)[:, None] * freqs
        cos, sin = jnp.cos(ang), jnp.sin(ang)
        return jnp.concatenate([cos, cos], axis=-1), jnp.concatenate([-sin, sin], axis=-1)

    cos_r, sin_r = tables(pos_r)
    cos_c, sin_c = tables(pos_c)
    cos = jnp.concatenate([cos_r, cos_c], axis=-1)
    sin = jnp.concatenate([sin_r, sin_c], axis=-1)
    return jnp.concatenate([cos, cos], axis=-1), jnp.concatenate([sin, sin], axis=-1)


def _layer_weights(l, w_in, w_out, sgu_w, sgu_b, q_norm_g, k_norm_g, diff_lambda, diff_norm_g, gla_w2, gla_b,
                   gla_norm_g, norm1_g, norm2_g, router_w, router_bias):
    w_in_pad = jnp.pad(w_in[l], ((0, 0), (0, D_PROJ_PAD - w_in.shape[2]))).astype(BF16)
    w2cat = jnp.zeros((LANES, 2 * W_GLA), F32)
    w2cat = w2cat.at[0:GLA_RANK, 0:W_GLA].set(gla_w2[l, 0]).at[GLA_RANK:2 * GLA_RANK, W_GLA:].set(gla_w2[l, 1])
    return (
        norm1_g[l][None, :], norm2_g[l][None, :], w_in_pad, w_out[l].astype(BF16),
        sgu_w[l].astype(BF16), jnp.repeat(sgu_b[l].T, SGU_GROUP_W, axis=1),
        jnp.tile(q_norm_g[l], W_QK // DIFF_QK)[None, :], jnp.tile(k_norm_g[l], W_QK // DIFF_QK)[None, :],
        diff_lambda[l], diff_norm_g[l][None, :],
        w2cat.astype(BF16), gla_b[l].reshape(1, 2 * W_GLA), jnp.tile(gla_norm_g[l], GLA_HEADS)[None, :],
        router_w.T, router_bias[:, None],
    )


def kernel(x_prompt, x_sample, cache_k, cache_v, state_gla, c, c_ctx, w_in, w_out, sgu_w, sgu_b, q_norm_g, k_norm_g,
           diff_lambda, diff_norm_g, gla_w2, gla_b, gla_norm_g, norm1_g, norm2_g, ada_w, ada_b, router_w, router_bias,
           moe_w1, moe_w3, moe_w2):
    n_ctx_seq, ctx_len, _ = x_prompt.shape
    n_lat_seq, lat_len, _ = x_sample.shape
    n_ctx_tok = n_ctx_seq * ctx_len
    n_lat_tok = n_lat_seq * lat_len
    ctx_tiles = n_ctx_tok // ROW_BLOCK
    lat_tiles_per_seq = lat_len // ROW_BLOCK

    cond = jnp.zeros((SUBLANES, D_MODEL), F32).at[0].set(c_ctx).at[1:1 + n_lat_seq].set(c)
    mods_all = _adaln_call(cond, ada_w, ada_b)[:, :1 + n_lat_seq].reshape(DEPTH, 1 + n_lat_seq, 6, D_MODEL)

    ck_all = cache_k.transpose(0, 1, 2, 4, 3, 5).reshape(cache_k.shape[:3] + (cache_k.shape[4], DIFF_V))
    st_all = jnp.einsum('bldhkv,hg->bldhvgk', state_gla, jnp.eye(GLA_HEADS, dtype=F32)).reshape(
        n_lat_seq, DEPTH, 2, W_GLA, W_GLA)
    cos, sin = _rope_tables(lat_len)

    def cond_of_tile(i):
        return jnp.where(i < ctx_tiles, 0, 1 + (i - ctx_tiles) // lat_tiles_per_seq)

    xp, xl = x_prompt, x_sample
    new_k, new_v, new_s = [], [], []
    for l in range(DEPTH):
        lam_init = 0.8 - 0.6 * math.exp(-0.3 * l)
        weights = _layer_weights(l, w_in, w_out, sgu_w, sgu_b, q_norm_g, k_norm_g, diff_lambda, diff_norm_g,
                                 gla_w2, gla_b, gla_norm_g, norm1_g, norm2_g, router_w, router_bias)
        mods = mods_all[l]
        xp1, hn_c, idx_c, _, wcol_c, k_l, v_l, s_l = _mixer_call(
            ctx_len, False, lam_init, xp, mods, lambda s: 0, weights, None)
        extras = (ck_all[:, l:l + 1], cache_v[:, l:l + 1], st_all[:, l:l + 1], cos, sin)
        xl1, hn_l, idx_l, _, wcol_l = _mixer_call(
            lat_len, True, lam_init, xl, mods, lambda s: 1 + s, weights, extras)
        new_k.append(k_l)
        new_v.append(v_l)
        new_s.append(s_l)

        x1 = jnp.concatenate([xp1.reshape(n_ctx_tok, D_MODEL), xl1.reshape(n_lat_tok, D_MODEL)], axis=0)
        hn = jnp.concatenate([hn_c.reshape(n_ctx_tok, D_MODEL), hn_l.reshape(n_lat_tok, D_MODEL)], axis=0)
        wcol = jnp.concatenate([wcol_c.reshape(n_ctx_tok, LANES), wcol_l.reshape(n_lat_tok, LANES)], axis=0)
        idx = jnp.concatenate([idx_c.transpose(1, 0, 2).reshape(2, n_ctx_tok),
                               idx_l.transpose(1, 0, 2).reshape(2, n_lat_tok)], axis=1)

        rank, counts = _rank_call(idx)
        dst, tile_expert, tile_valid, n_slots = _moe_plan(idx, rank, counts[:, 0])
        xs = _dispatch_call(hn, dst, n_slots)
        ys = _expert_call(xs, tile_expert, tile_valid, moe_w1[l], moe_w3[l], moe_w2[l])
        x2 = _combine_call(x1, wcol, mods, cond_of_tile, ys, dst)
        xp = x2[:n_ctx_tok].reshape(x_prompt.shape)
        xl = x2[n_ctx_tok:].reshape(x_sample.shape)

    return (xp, xl, jnp.stack(new_k, axis=1), jnp.stack(new_v, axis=1), jnp.stack(new_s, axis=1))
```

```python
import functools
import math

import jax
import jax.numpy as jnp
from jax import lax
from jax.experimental import pallas as pl
from jax.experimental.pallas import tpu as pltpu

F32 = jnp.float32
BF16 = jnp.bfloat16

D_MODEL = 1024
DEPTH = 4
GRID_W = 64
SGU_GROUPS = 4
SGU_GROUP_W = 64
SGU_W = SGU_GROUPS * SGU_GROUP_W
SGU_CHUNK = 128
DIFF_HEADS = 4
DIFF_QK = 64
DIFF_V = 2 * DIFF_QK
ROPE_BASE = 10000.0
GLA_HEADS = 4
GLA_DK = 64
GLA_DV = 64
GLA_RANK = 16
GLA_GATE_NORM = 16.0
GLA_CHUNK = 64
N_EXPERTS = 16
N_GROUPS = 4
EXPERTS_PER_GROUP = N_EXPERTS // N_GROUPS
D_EXPERT = 512
EPS = 1e-6

LANES = 128
SUBLANES = 8
MXU_DIM = 256

ROW_BLOCK = MXU_DIM
ROW_SLABS = D_MODEL // LANES

C_AU, C_AV, C_BQ, C_BK, C_BV = 0, 256, 512, 1024, 1536
C_CQ, C_CK, C_CV, C_CR, C_LR = 2048, 2304, 2560, 2816, 3072
D_PROJ_MAIN = 3072
D_PROJ_PAD = D_PROJ_MAIN + LANES
W_QK = DIFF_HEADS * 2 * DIFF_QK
W_GLA = GLA_HEADS * GLA_DK
M_A, M_B, M_C = 0, SGU_W, SGU_W + DIFF_HEADS * DIFF_V


def _split2(x):
    hi = x.astype(BF16)
    lo = (x - hi.astype(F32)).astype(BF16)
    return hi, lo


def _split3(x):
    hi = x.astype(BF16)
    r = x - hi.astype(F32)
    mid = r.astype(BF16)
    lo = (r - mid.astype(F32)).astype(BF16)
    return hi, mid, lo


def _dot(a, b):
    return jnp.dot(a, b, preferred_element_type=F32)


def _dot_nt(a, b):
    return lax.dot_general(a, b, (((1,), (1,)), ((), ())), preferred_element_type=F32)


def _dot_tn(a, b):
    return lax.dot_general(a, b, (((0,), (0,)), ((), ())), preferred_element_type=F32)


def _iota(shape, dim):
    return lax.broadcasted_iota(jnp.int32, shape, dim)


def _block_ones(width, block):
    r = _iota((width, width), 0) // block
    c = _iota((width, width), 1) // block
    return (r == c)


def _group_sum(z, block):
    width = z.shape[-1]
    outs = []
    for s in range(0, width, MXU_DIM):
        w = min(MXU_DIM, width - s)
        ones = _block_ones(w, block).astype(BF16)
        hi, lo = _split2(z[:, s:s + w])
        outs.append(_dot(hi, ones) + _dot(lo, ones))
    return outs[0] if len(outs) == 1 else jnp.concatenate(outs, axis=-1)


def _group_rms(z, block):
    ms = _group_sum(z * z, block) * (1.0 / block)
    return z * lax.rsqrt(ms + EPS)


def _row_rms(z):
    return z * lax.rsqrt(jnp.mean(z * z, axis=-1, keepdims=True) + EPS)


def _log_sigmoid(x):
    return jnp.minimum(x, 0.0) - jnp.log1p(jnp.exp(-jnp.abs(x)))


ADA_COLS = 1536


def _adaln_kernel(cond_ref, w_ref, b_ref, o_ref):
    sc = jax.nn.silu(cond_ref[...])
    c_hi, c_lo = _split2(sc)
    w = w_ref[0]
    w_hi = w.astype(BF16)
    w_lo = (w - w_hi.astype(F32)).astype(BF16)
    o_ref[0] = _dot(c_hi, w_hi) + _dot(c_lo, w_hi) + _dot(c_hi, w_lo) + b_ref[0]


def _adaln_call(cond, ada_w, ada_b):
    n_col = 6 * D_MODEL // ADA_COLS
    return pl.pallas_call(
        _adaln_kernel,
        grid=(DEPTH, n_col),
        in_specs=[
            pl.BlockSpec((SUBLANES, D_MODEL), lambda l, j: (0, 0)),
            pl.BlockSpec((1, D_MODEL, ADA_COLS), lambda l, j: (l, 0, j)),
            pl.BlockSpec((1, 1, ADA_COLS), lambda l, j: (l, 0, j)),
        ],
        out_specs=pl.BlockSpec((1, SUBLANES, ADA_COLS), lambda l, j: (l, 0, j)),
        out_shape=jax.ShapeDtypeStruct((DEPTH, SUBLANES, 6 * D_MODEL), F32),
        compiler_params=pltpu.CompilerParams(
            dimension_semantics=("arbitrary", "arbitrary"), vmem_limit_bytes=40 * 1024 * 1024),
        name="adaln",
    )(cond, ada_w, ada_b.reshape(DEPTH, 1, 6 * D_MODEL))


def _route(hn, rwt_ref, rb_ref):
    h_hi, h_lo = _split2(hn)
    rw = rwt_ref[...]
    rw_hi = rw.astype(BF16)
    rw_lo = (rw - rw_hi.astype(F32)).astype(BF16)
    logits = _dot_nt(rw_hi, h_hi) + _dot_nt(rw_hi, h_lo) + _dot_nt(rw_lo, h_hi)
    aff = jax.nn.sigmoid(logits)
    sel = aff + rb_ref[...]
    n_tok = sel.shape[1]

    def top2_sum(a, b, c, d):
        hi1, lo1 = jnp.maximum(a, b), jnp.minimum(a, b)
        hi2, lo2 = jnp.maximum(c, d), jnp.minimum(c, d)
        return jnp.maximum(hi1, hi2) + jnp.maximum(jnp.minimum(hi1, hi2), jnp.maximum(lo1, lo2))

    scores = []
    for g in range(N_GROUPS):
        rows = [sel[EXPERTS_PER_GROUP * g + j:EXPERTS_PER_GROUP * g + j + 1, :] for j in range(EXPERTS_PER_GROUP)]
        scores.append(top2_sum(*rows))
    best = jnp.zeros((1, n_tok), jnp.int32)
    best_score = scores[0]
    for g in range(1, N_GROUPS):
        upd = scores[g] > best_score
        best = jnp.where(upd, g, best)
        best_score = jnp.where(upd, scores[g], best_score)

    eid_i = _iota((N_EXPERTS, n_tok), 0)
    eid = eid_i.astype(F32)
    neg = jnp.float32(-jnp.inf)
    msel = jnp.where(eid_i // EXPERTS_PER_GROUP == best, sel, neg)
    m1 = jnp.max(msel, axis=0, keepdims=True)
    idx1 = jnp.min(jnp.where(msel == m1, eid, float(N_EXPERTS)), axis=0, keepdims=True)
    msel2 = jnp.where(eid == idx1, neg, msel)
    m2 = jnp.max(msel2, axis=0, keepdims=True)
    idx2 = jnp.min(jnp.where(msel2 == m2, eid, float(N_EXPERTS)), axis=0, keepdims=True)
    w1 = jnp.sum(jnp.where(eid == idx1, aff, 0.0), axis=0, keepdims=True)
    w2 = jnp.sum(jnp.where(eid == idx2, aff, 0.0), axis=0, keepdims=True)
    wsum = w1 + w2
    return idx1.astype(jnp.int32), idx2.astype(jnp.int32), w1 / wsum, w2 / wsum


N_MIXER_WEIGHTS = 15


def _mixer_kernel(n_tok, latent, n_alias, lam_init, *refs):
    it = iter(refs)
    x_ref, mod_ref = next(it), next(it)
    (n1_ref, n2_ref, win_ref, wout_ref, sw_ref, sb_ref, qg_ref, kg_ref, dl_ref, dg_ref,
     w2c_ref, gb_ref, gg_ref, rwt_ref, rb_ref) = (next(it) for _ in range(N_MIXER_WEIGHTS))
    if latent:
        ck_ref, cv_ref, st0_ref, cos_ref, sin_ref = (next(it) for _ in range(5))
    for _ in range(n_alias):
        next(it)
    xo_ref, hn_ref, ridx_ref, wcol_ref = (next(it) for _ in range(4))
    if not latent:
        ko_ref, vo_ref, so_ref = (next(it) for _ in range(3))
    proj_ref, mix_ref, q_ref, k_ref, v_ref = (next(it) for _ in range(5))
    gq_ref, gki_ref, gke_ref, gv_ref, dec_ref, go_ref, st_ref = (next(it) for _ in range(7))

    n_blk = n_tok // ROW_BLOCK
    n_ctx = k_ref.shape[0] - n_tok
    mod = mod_ref[0, 0]

    if not latent:
        for ref in (ko_ref, vo_ref, so_ref):
            for other in range(1, ref.shape[1]):
                ref[0, other] = jnp.zeros(ref.shape[2:], F32)

    for r in range(n_blk):
        rows = pl.ds(r * ROW_BLOCK, ROW_BLOCK)
        h = _row_rms(x_ref[rows, :]) * n1_ref[0]
        h = h * (1.0 + mod[1:2, :]) + mod[0:1, :]
        proj_ref[rows, :] = _dot(h.astype(BF16), win_ref[0])

    lane_group = _iota((SGU_CHUNK, SGU_W), 1) // SGU_GROUP_W

    def sgu_chunk(c, carry):
        rows = pl.ds(pl.multiple_of(c * SGU_CHUNK, SGU_CHUNK), SGU_CHUNK)
        u = jax.nn.gelu(proj_ref[rows, C_AU:C_AU + SGU_W])
        v = _group_rms(jax.nn.gelu(proj_ref[rows, C_AV:C_AV + SGU_W]), SGU_GROUP_W).astype(BF16)
        s = sb_ref[0]
        for g in range(SGU_GROUPS):
            s = s + jnp.where(lane_group == g, _dot(sw_ref[0, g], v), 0.0)
        mix_ref[rows, M_A:M_A + SGU_W] = (u * s).astype(BF16)
        return carry

    lax.fori_loop(0, n_tok // SGU_CHUNK, sgu_chunk, 0)

    dl = dl_ref[0]
    lam = (jnp.exp(jnp.sum(dl[0:1] * dl[1:2], axis=-1, keepdims=True))
           - jnp.exp(jnp.sum(dl[2:3] * dl[3:4], axis=-1, keepdims=True)) + lam_init)

    if latent:
        for h in range(DIFF_HEADS):
            k_ref[0:n_ctx, h * DIFF_V:(h + 1) * DIFF_V] = ck_ref[0, 0, h].astype(BF16)
            v_ref[0:n_ctx, h * DIFF_V:(h + 1) * DIFF_V] = cv_ref[0, 0, h].astype(BF16)
        pair_lo = (_iota((ROW_BLOCK, W_QK), 1) % (DIFF_QK // 2)) < (DIFF_QK // 4)

        def rope(z, rows):
            cos = jnp.concatenate([cos_ref[rows, :]] * DIFF_HEADS, axis=-1)
            sin = jnp.concatenate([sin_ref[rows, :]] * DIFF_HEADS, axis=-1)
            shift = DIFF_QK // 4
            swapped = jnp.where(pair_lo, pltpu.roll(z, W_QK - shift, 1), pltpu.roll(z, shift, 1))
            return z * cos + swapped * sin

    for r in range(n_blk):
        rows = pl.ds(r * ROW_BLOCK, ROW_BLOCK)
        qn = _group_rms(proj_ref[rows, C_BQ:C_BQ + W_QK], DIFF_QK) * qg_ref[0]
        kn = _group_rms(proj_ref[rows, C_BK:C_BK + W_QK], DIFF_QK) * kg_ref[0]
        vv = proj_ref[rows, C_BV:C_BV + W_QK]
        if latent:
            qn, kn = rope(qn, rows), rope(kn, rows)
        else:
            for h in range(DIFF_HEADS):
                for i in range(2):
                    lo = h * DIFF_V + i * DIFF_QK
                    ko_ref[0, 0, h, i, rows, :] = kn[:, lo:lo + DIFF_QK]
                vo_ref[0, 0, h, rows, :] = vv[:, h * DIFF_V:(h + 1) * DIFF_V]
        q_ref[rows, :] = (qn * (DIFF_QK ** -0.5)).astype(BF16)
        k_ref[pl.ds(n_ctx + r * ROW_BLOCK, ROW_BLOCK), :] = kn.astype(BF16)
        v_ref[pl.ds(n_ctx + r * ROW_BLOCK, ROW_BLOCK), :] = vv.astype(BF16)

    sub0 = (_iota((ROW_BLOCK, DIFF_V), 1) < DIFF_QK)

    def softmax(s):
        e = jnp.exp(s - jnp.max(s, axis=-1, keepdims=True))
        return e, jnp.sum(e, axis=-1, keepdims=True)

    def attn_block(r, carry):
        rows = pl.ds(pl.multiple_of(r * ROW_BLOCK, ROW_BLOCK), ROW_BLOCK)
        for h in range(DIFF_HEADS):
            cols = slice(h * DIFF_V, (h + 1) * DIFF_V)
            qh = q_ref[rows, cols]
            kh = k_ref[:, cols]
            e0, z0 = softmax(_dot_nt(jnp.where(sub0, qh, jnp.zeros_like(qh)), kh))
            e1, z1 = softmax(_dot_nt(jnp.where(sub0, jnp.zeros_like(qh), qh), kh))
            w = e0 / z0 - lam * (e1 / z1)
            o = _dot(w.astype(BF16), v_ref[:, cols])
            o = _row_rms(o) * dg_ref[0] * (1.0 - lam_init)
            mix_ref[rows, M_B + h * DIFF_V:M_B + (h + 1) * DIFF_V] = o.astype(BF16)
        return carry

    lax.fori_loop(0, n_blk, attn_block, 0)

    blk_r = _iota((ROW_BLOCK, ROW_BLOCK), 0)
    blk_c = _iota((ROW_BLOCK, ROW_BLOCK), 1)
    same_chunk = (blk_r // GLA_CHUNK) == (blk_c // GLA_CHUNK)
    tri = (jnp.where(same_chunk & (blk_c <= blk_r), 1.0, 0.0).astype(BF16),
           jnp.where(same_chunk & (blk_c >= blk_r), 1.0, 0.0).astype(BF16))
    chunks_per_blk = ROW_BLOCK // GLA_CHUNK
    for r in range(n_blk):
        rows = pl.ds(r * ROW_BLOCK, ROW_BLOCK)
        gpre = _dot(proj_ref[rows, C_LR:C_LR + LANES].astype(BF16), w2c_ref[0]) + gb_ref[0]
        gate = _log_sigmoid(gpre) * (1.0 / GLA_GATE_NORM)
        gq = proj_ref[rows, C_CQ:C_CQ + W_GLA] * (GLA_DK ** -0.5)
        gk = proj_ref[rows, C_CK:C_CK + W_GLA]
        gv_ref[rows, :] = proj_ref[rows, C_CV:C_CV + W_GLA].astype(BF16)
        for d in range(2):
            g = gate[:, d * W_GLA:(d + 1) * W_GLA]
            b = sum(_dot(tri[d], p) for p in _split3(g))
            last = GLA_CHUNK - 1 if d == 0 else 0
            b_last = jnp.concatenate(
                [jnp.broadcast_to(b[c * GLA_CHUNK + last:c * GLA_CHUNK + last + 1, :], (GLA_CHUNK, W_GLA))
                 for c in range(chunks_per_blk)], axis=0)
            gq_ref[d, rows, :] = (gq * jnp.exp(b)).astype(BF16)
            gki_ref[d, rows, :] = (gk * jnp.exp(-b)).astype(BF16)
            gke_ref[d, rows, :] = (gk * jnp.exp(b_last - b)).astype(BF16)
            for c in range(chunks_per_blk):
                row = c * GLA_CHUNK + last
                dec_ref[d, r * chunks_per_blk + c] = jnp.exp(b[row:row + 1, :])

    if latent:
        st_ref[...] = st0_ref[0, 0]
    else:
        st_ref[...] = jnp.zeros(st_ref.shape, F32)

    n_chunk = n_tok // GLA_CHUNK
    head_of_lane = _iota((GLA_CHUNK, W_GLA), 1) // GLA_DK
    stack_r = _iota((GLA_HEADS * GLA_CHUNK, GLA_CHUNK), 0) % GLA_CHUNK
    stack_c = _iota((GLA_HEADS * GLA_CHUNK, GLA_CHUNK), 1)
    causal = (stack_c <= stack_r, stack_c >= stack_r)
    st_diag = (_iota((W_GLA, W_GLA), 0) // GLA_DV) == (_iota((W_GLA, W_GLA), 1) // GLA_DK)

    def gla_step(c, carry):
        for d in range(2):
            cc = c if d == 0 else n_chunk - 1 - c
            rows = pl.ds(pl.multiple_of(cc * GLA_CHUNK, GLA_CHUNK), GLA_CHUNK)
            qd = gq_ref[d, rows, :]
            vv = gv_ref[rows, :]
            q_stack = jnp.concatenate(
                [jnp.where(head_of_lane == h, qd, jnp.zeros_like(qd)) for h in range(GLA_HEADS)], axis=0)
            attn = jnp.where(causal[d], _dot_nt(q_stack, gki_ref[d, rows, :]), 0.0)
            spread = _dot(attn.astype(BF16), vv)
            o = jnp.zeros((GLA_CHUNK, W_GLA), F32)
            for h in range(GLA_HEADS):
                o = o + jnp.where(head_of_lane == h, spread[h * GLA_CHUNK:(h + 1) * GLA_CHUNK, :], 0.0)
            st = st_ref[d]
            o = o + _dot_nt(qd, st.astype(BF16))
            go_ref[d, rows, :] = o
            upd = _dot_tn(vv, gke_ref[d, rows, :])
            st_ref[d] = dec_ref[d, cc] * st + jnp.where(st_diag, upd, 0.0)
        return carry

    lax.fori_loop(0, n_chunk, gla_step, 0)

    if not latent:
        for d in range(2):
            s_full = st_ref[d].T
            for h in range(GLA_HEADS):
                so_ref[0, 0, d, h] = s_full[h * GLA_DK:(h + 1) * GLA_DK, h * GLA_DV:(h + 1) * GLA_DV]

    for r in range(n_blk):
        rows = pl.ds(r * ROW_BLOCK, ROW_BLOCK)
        oc = _group_rms(go_ref[0, rows, :] + go_ref[1, rows, :], GLA_DV) * gg_ref[0]
        oc = oc * jax.nn.silu(proj_ref[rows, C_CR:C_CR + W_GLA])
        mix_ref[rows, M_C:M_C + W_GLA] = oc.astype(BF16)
        x1 = x_ref[rows, :] + mod[2:3, :] * _dot(mix_ref[rows, :], wout_ref[0])
        xo_ref[rows, :] = x1
        hn = _row_rms(x1) * n2_ref[0]
        hn = hn * (1.0 + mod[4:5, :]) + mod[3:4, :]
        hn_ref[rows, :] = hn
        idx1, idx2, w1, w2 = _route(hn, rwt_ref, rb_ref)
        ridx_ref[:, rows] = jnp.concatenate([idx1, idx2], axis=0)
        wpad = jnp.concatenate([w1, w2, jnp.zeros((LANES - 2, ROW_BLOCK), F32)], axis=0)
        wcol_ref[rows, :] = wpad.T


def _mixer_call(l, n_tok, latent, x, mods_all, weights, extras, cache_bufs):
    n_seq = x.shape[0] // n_tok
    n_all = x.shape[0]
    n_keys = n_tok + (extras[0].shape[3] if latent else 0)
    n_chunk = n_tok // GLA_CHUNK
    lam_init = 0.8 - 0.6 * math.exp(-0.3 * l)

    single = pl.Buffered(1)
    seq_mode = single if latent else None

    def layer(arr):
        tail = arr.shape[1:]
        return pl.BlockSpec((1,) + tail, lambda s, _n=len(tail): (l,) + (0,) * _n, pipeline_mode=single)

    def const(arr):
        return pl.BlockSpec(arr.shape, lambda s, _n=arr.ndim: (0,) * _n, pipeline_mode=single)

    def tok_spec(width):
        return pl.BlockSpec((n_tok, width), lambda s: (s, 0), pipeline_mode=seq_mode)

    mod_row = (lambda s: 1 + s) if latent else (lambda s: 0)
    in_specs = [tok_spec(D_MODEL),
                pl.BlockSpec((1, 1, 6, D_MODEL), lambda s: (l, mod_row(s), 0, 0))]
    in_specs += [layer(w) for w in weights[:N_MIXER_WEIGHTS - 2]] + [const(w) for w in weights[-2:]]
    operands = [x, mods_all] + list(weights)
    if latent:
        ck, cv, st0, cos, sin = extras
        in_specs += [
            pl.BlockSpec((1, 1) + ck.shape[2:], lambda s: (s, l, 0, 0, 0)),
            pl.BlockSpec((1, 1) + cv.shape[2:], lambda s: (s, l, 0, 0, 0)),
            pl.BlockSpec((1, 1) + st0.shape[2:], lambda s: (s, l, 0, 0, 0)),
            const(cos), const(sin),
        ]
        operands += [ck, cv, st0, cos, sin]
    n_in = len(operands)
    in_specs += [pl.BlockSpec(memory_space=pl.ANY)] * len(cache_bufs)
    operands += list(cache_bufs)

    out_shape = [
        jax.ShapeDtypeStruct((n_all, D_MODEL), F32),
        jax.ShapeDtypeStruct((n_all, D_MODEL), F32),
        jax.ShapeDtypeStruct((2, n_all), jnp.int32),
        jax.ShapeDtypeStruct((n_all, LANES), F32),
    ]
    out_specs = [
        tok_spec(D_MODEL), tok_spec(D_MODEL),
        pl.BlockSpec((2, n_tok), lambda s: (0, s)),
        pl.BlockSpec((n_tok, LANES), lambda s: (s, 0)),
    ]
    aliases = {}
    if not latent:
        out_shape += [
            jax.ShapeDtypeStruct((n_seq, DEPTH, DIFF_HEADS, 2, n_tok, DIFF_QK), F32),
            jax.ShapeDtypeStruct((n_seq, DEPTH, DIFF_HEADS, n_tok, DIFF_V), F32),
            jax.ShapeDtypeStruct((n_seq, DEPTH, 2, GLA_HEADS, GLA_DK, GLA_DV), F32),
        ]
        n_lay, at = (1, l) if cache_bufs else (DEPTH, 0)
        out_specs += [
            pl.BlockSpec((1, n_lay, DIFF_HEADS, 2, n_tok, DIFF_QK), lambda s: (s, at, 0, 0, 0, 0)),
            pl.BlockSpec((1, n_lay, DIFF_HEADS, n_tok, DIFF_V), lambda s: (s, at, 0, 0, 0)),
            pl.BlockSpec((1, n_lay, 2, GLA_HEADS, GLA_DK, GLA_DV), lambda s: (s, at, 0, 0, 0, 0)),
        ]
        aliases = {n_in + j: 4 + j for j in range(len(cache_bufs))}
    scratch = [
        pltpu.VMEM((n_tok, D_PROJ_PAD), F32),
        pltpu.VMEM((n_tok, D_MODEL), BF16),
        pltpu.VMEM((n_tok, W_QK), BF16),
        pltpu.VMEM((n_keys, W_QK), BF16),
        pltpu.VMEM((n_keys, W_QK), BF16),
        pltpu.VMEM((2, n_tok, W_GLA), BF16),
        pltpu.VMEM((2, n_tok, W_GLA), BF16),
        pltpu.VMEM((2, n_tok, W_GLA), BF16),
        pltpu.VMEM((n_tok, W_GLA), BF16),
        pltpu.VMEM((2, n_chunk, 1, W_GLA), F32),
        pltpu.VMEM((2, n_tok, W_GLA), F32),
        pltpu.VMEM((2, W_GLA, W_GLA), F32),
    ]
    return pl.pallas_call(
        functools.partial(_mixer_kernel, n_tok, latent, len(cache_bufs), lam_init),
        grid=(n_seq,),
        in_specs=in_specs,
        out_specs=out_specs,
        out_shape=out_shape,
        scratch_shapes=scratch,
        input_output_aliases=aliases,
        compiler_params=pltpu.CompilerParams(
            dimension_semantics=("arbitrary",), vmem_limit_bytes=56 * 1024 * 1024),
        name="mixer_latent" if latent else "mixer_context",
    )(*operands)


RANK_TILE = 512


def _rank_kernel(idx_ref, rank_ref, count_ref, carry_ref):
    @pl.when(pl.program_id(0) == 0)
    def _():
        carry_ref[...] = jnp.zeros(carry_ref.shape, F32)

    idx = idx_ref[...]
    eid = _iota((N_EXPERTS, RANK_TILE), 0)
    hot0 = eid == idx[0:1, :]
    hot1 = eid == idx[1:2, :]
    hot = jnp.where(hot0 | hot1, 1.0, 0.0).astype(BF16)
    before = (_iota((RANK_TILE, RANK_TILE), 0) < _iota((RANK_TILE, RANK_TILE), 1))
    prefix = _dot(hot, jnp.where(before, 1.0, 0.0).astype(BF16)) + carry_ref[:, 0:1]
    r0 = jnp.sum(jnp.where(hot0, prefix, 0.0), axis=0, keepdims=True)
    r1 = jnp.sum(jnp.where(hot1, prefix, 0.0), axis=0, keepdims=True)
    rank_ref[...] = jnp.concatenate([r0, r1], axis=0).astype(jnp.int32)
    total = carry_ref[...] + jnp.sum(hot.astype(F32), axis=1, keepdims=True)
    carry_ref[...] = total
    count_ref[...] = total.astype(jnp.int32)


def _rank_call(idx):
    n = idx.shape[1]
    return pl.pallas_call(
        _rank_kernel,
        grid=(n // RANK_TILE,),
        in_specs=[pl.BlockSpec((2, RANK_TILE), lambda i: (0, i))],
        out_specs=[pl.BlockSpec((2, RANK_TILE), lambda i: (0, i)),
                   pl.BlockSpec((N_EXPERTS, LANES), lambda i: (0, 0))],
        out_shape=[jax.ShapeDtypeStruct((2, n), jnp.int32),
                   jax.ShapeDtypeStruct((N_EXPERTS, LANES), jnp.int32)],
        scratch_shapes=[pltpu.VMEM((N_EXPERTS, LANES), F32)],
        compiler_params=pltpu.CompilerParams(dimension_semantics=("arbitrary",)),
        name="moe_rank",
    )(idx)


def _to_row_tiles(ref, value):
    for s in range(ROW_SLABS):
        ref[:, s, :] = value[:, s * LANES:(s + 1) * LANES]


def _from_row_tiles(ref):
    return jnp.concatenate([ref[:, s, :] for s in range(ROW_SLABS)], axis=-1)


def _two_streams(n_first_tiles):
    def first(i, *_):
        return (jnp.minimum(i, n_first_tiles - 1), 0)

    def second(i, *_):
        return (jnp.maximum(i - n_first_tiles, 0), 0)

    return first, second


def _dispatch_kernel(n_tok, n_ctx_tiles, dst_ref, hn_c_ref, hn_l_ref, xs_in_ref, xs_ref, buf_ref, sem):
    del xs_in_ref
    i = pl.program_id(0)
    base = i * ROW_BLOCK

    @pl.when(i < n_ctx_tiles)
    def _():
        _to_row_tiles(buf_ref, hn_c_ref[...])

    @pl.when(i >= n_ctx_tiles)
    def _():
        _to_row_tiles(buf_ref, hn_l_ref[...])

    def copy(r, k):
        return pltpu.make_async_copy(buf_ref.at[r], xs_ref.at[dst_ref[k * n_tok + base + r]], sem)

    def start(r, carry):
        copy(r, 0).start()
        copy(r, 1).start()
        return carry

    def wait(r, carry):
        copy(r, 0).wait()
        copy(r, 1).wait()
        return carry

    lax.fori_loop(0, ROW_BLOCK, start, 0)
    lax.fori_loop(0, ROW_BLOCK, wait, 0)


def _dispatch_call(hn_c, hn_l, dst, n_slots):
    n_tok = hn_c.shape[0] + hn_l.shape[0]
    n_ctx_tiles = hn_c.shape[0] // ROW_BLOCK
    first, second = _two_streams(n_ctx_tiles)
    xs_init = jnp.zeros((n_slots, ROW_SLABS, LANES), F32)
    return pl.pallas_call(
        functools.partial(_dispatch_kernel, n_tok, n_ctx_tiles),
        grid_spec=pltpu.PrefetchScalarGridSpec(
            num_scalar_prefetch=1,
            grid=(n_tok // ROW_BLOCK,),
            in_specs=[pl.BlockSpec((ROW_BLOCK, D_MODEL), first),
                      pl.BlockSpec((ROW_BLOCK, D_MODEL), second),
                      pl.BlockSpec(memory_space=pl.ANY)],
            out_specs=pl.BlockSpec(memory_space=pl.ANY),
            scratch_shapes=[pltpu.VMEM((ROW_BLOCK, ROW_SLABS, LANES), F32), pltpu.SemaphoreType.DMA],
        ),
        out_shape=jax.ShapeDtypeStruct((n_slots, ROW_SLABS, LANES), F32),
        input_output_aliases={3: 0},
        compiler_params=pltpu.CompilerParams(dimension_semantics=("arbitrary",)),
        name="moe_dispatch",
    )(dst.reshape(-1), hn_c, hn_l, xs_init)


def _expert_kernel(te_ref, tr_ref, xs_ref, w1_ref, w3_ref, w2_ref, ys_ref, w1b_ref, w3b_ref, w2b_ref):
    i = pl.program_id(0)
    prev = te_ref[jnp.maximum(i - 1, 0)]
    n_rows = tr_ref[i]

    @pl.when((i == 0) | (te_ref[i] != prev))
    def _():
        w1b_ref[...] = w1_ref[0, 0].astype(BF16)
        w3b_ref[...] = w3_ref[0, 0].astype(BF16)
        w2b_ref[...] = w2_ref[0, 0].astype(BF16)

    @pl.when(n_rows > 0)
    def _():
        live = _iota((ROW_BLOCK, D_MODEL), 0) < n_rows
        x = jnp.where(live, _from_row_tiles(xs_ref), 0.0).astype(BF16)
        hid = jax.nn.silu(_dot(x, w1b_ref[...])) * _dot(x, w3b_ref[...])
        _to_row_tiles(ys_ref, _dot(hid.astype(BF16), w2b_ref[...]))

    @pl.when(n_rows == 0)
    def _():
        ys_ref[...] = jnp.zeros(ys_ref.shape, F32)


def _expert_call(l, xs, tile_expert, tile_rows, w1, w3, w2):
    n_tiles = tile_expert.shape[0]
    return pl.pallas_call(
        _expert_kernel,
        grid_spec=pltpu.PrefetchScalarGridSpec(
            num_scalar_prefetch=2,
            grid=(n_tiles,),
            in_specs=[
                pl.BlockSpec((ROW_BLOCK, ROW_SLABS, LANES), lambda i, te, tr: (i, 0, 0)),
                pl.BlockSpec((1, 1, D_MODEL, D_EXPERT), lambda i, te, tr: (l, te[i], 0, 0)),
                pl.BlockSpec((1, 1, D_MODEL, D_EXPERT), lambda i, te, tr: (l, te[i], 0, 0)),
                pl.BlockSpec((1, 1, D_EXPERT, D_MODEL), lambda i, te, tr: (l, te[i], 0, 0)),
            ],
            out_specs=pl.BlockSpec((ROW_BLOCK, ROW_SLABS, LANES), lambda i, te, tr: (i, 0, 0)),
            scratch_shapes=[pltpu.VMEM((D_MODEL, D_EXPERT), BF16), pltpu.VMEM((D_MODEL, D_EXPERT), BF16),
                            pltpu.VMEM((D_EXPERT, D_MODEL), BF16)],
        ),
        out_shape=jax.ShapeDtypeStruct(xs.shape, F32),
        compiler_params=pltpu.CompilerParams(
            dimension_semantics=("arbitrary",), vmem_limit_bytes=40 * 1024 * 1024),
        name="moe_experts",
    )(tile_expert, tile_rows, xs, w1, w3, w2)


def _combine_kernel(n_tok, n_ctx_tiles, dst_ref, x_c_ref, x_l_ref, wcol_c_ref, wcol_l_ref, mod_ref, ys_ref,
                    xo_c_ref, xo_l_ref, buf_ref, sem):
    i = pl.program_id(0)
    base = i * ROW_BLOCK

    def copy(r, k):
        return pltpu.make_async_copy(ys_ref.at[dst_ref[k * n_tok + base + r]], buf_ref.at[k, r], sem)

    def start(r, carry):
        copy(r, 0).start()
        copy(r, 1).start()
        return carry

    def wait(r, carry):
        copy(r, 0).wait()
        copy(r, 1).wait()
        return carry

    lax.fori_loop(0, ROW_BLOCK, start, 0)
    lax.fori_loop(0, ROW_BLOCK, wait, 0)
    y0, y1 = _from_row_tiles(buf_ref.at[0]), _from_row_tiles(buf_ref.at[1])
    gate = mod_ref[0, 0, 5:6, :]

    def finish(x_ref, wcol_ref, xo_ref):
        wcol = wcol_ref[...]
        xo_ref[...] = x_ref[...] + gate * (wcol[:, 0:1] * y0 + wcol[:, 1:2] * y1)

    @pl.when(i < n_ctx_tiles)
    def _():
        finish(x_c_ref, wcol_c_ref, xo_c_ref)

    @pl.when(i >= n_ctx_tiles)
    def _():
        finish(x_l_ref, wcol_l_ref, xo_l_ref)


def _combine_call(l, x_c, x_l, wcol_c, wcol_l, mods_all, mod_row_of_tile, ys, dst):
    n_tok = x_c.shape[0] + x_l.shape[0]
    n_ctx_tiles = x_c.shape[0] // ROW_BLOCK
    first, second = _two_streams(n_ctx_tiles)
    return pl.pallas_call(
        functools.partial(_combine_kernel, n_tok, n_ctx_tiles),
        grid_spec=pltpu.PrefetchScalarGridSpec(
            num_scalar_prefetch=1,
            grid=(n_tok // ROW_BLOCK,),
            in_specs=[pl.BlockSpec((ROW_BLOCK, D_MODEL), first),
                      pl.BlockSpec((ROW_BLOCK, D_MODEL), second),
                      pl.BlockSpec((ROW_BLOCK, LANES), first),
                      pl.BlockSpec((ROW_BLOCK, LANES), second),
                      pl.BlockSpec((1, 1, 6, D_MODEL), lambda i, dst: (l, mod_row_of_tile(i), 0, 0)),
                      pl.BlockSpec(memory_space=pl.ANY)],
            out_specs=[pl.BlockSpec((ROW_BLOCK, D_MODEL), first),
                       pl.BlockSpec((ROW_BLOCK, D_MODEL), second)],
            scratch_shapes=[pltpu.VMEM((2, ROW_BLOCK, ROW_SLABS, LANES), F32), pltpu.SemaphoreType.DMA],
        ),
        out_shape=[jax.ShapeDtypeStruct(x_c.shape, F32), jax.ShapeDtypeStruct(x_l.shape, F32)],
        compiler_params=pltpu.CompilerParams(dimension_semantics=("arbitrary",)),
        name="moe_combine",
    )(dst.reshape(-1), x_c, x_l, wcol_c, wcol_l, mods_all, ys)


def _moe_plan(idx, rank, counts):
    n_tok = idx.shape[1]
    n_tiles = 2 * n_tok // ROW_BLOCK + N_EXPERTS
    padded = (counts + ROW_BLOCK - 1) // ROW_BLOCK * ROW_BLOCK
    ends = jnp.cumsum(padded)
    starts = ends - padded
    experts = jnp.arange(N_EXPERTS, dtype=jnp.int32)[:, None, None]
    dst = jnp.sum(jnp.where(idx[None] == experts, starts[:, None, None], 0), axis=0) + rank
    tile_start = jnp.arange(n_tiles, dtype=jnp.int32) * ROW_BLOCK
    tile_expert = jnp.minimum(
        jnp.sum((tile_start[:, None] >= ends[None, :]).astype(jnp.int32), axis=1), N_EXPERTS - 1)
    hot = tile_expert[:, None] == jnp.arange(N_EXPERTS, dtype=jnp.int32)[None, :]
    live_end = jnp.sum(jnp.where(hot, (starts + counts)[None, :], 0), axis=1)
    tile_rows = jnp.clip(live_end - tile_start, 0, ROW_BLOCK)
    return dst, tile_expert.astype(jnp.int32), tile_rows.astype(jnp.int32), n_tiles * ROW_BLOCK


def _rope_tables(n_tok):
    n_rows = n_tok // GRID_W
    pos_r = jnp.repeat(jnp.arange(n_rows), GRID_W)
    pos_c = jnp.tile(jnp.arange(GRID_W), n_rows)
    half = DIFF_QK // 2
    nf = half // 2
    freqs = ROPE_BASE ** (-jnp.arange(nf, dtype=F32) / nf)

    def tables(pos):
        ang = pos.astype(F32)[:, None] * freqs
        cos, sin = jnp.cos(ang), jnp.sin(ang)
        return jnp.concatenate([cos, cos], axis=-1), jnp.concatenate([-sin, sin], axis=-1)

    cos_r, sin_r = tables(pos_r)
    cos_c, sin_c = tables(pos_c)
    cos = jnp.concatenate([cos_r, cos_c], axis=-1)
    sin = jnp.concatenate([sin_r, sin_c], axis=-1)
    return jnp.concatenate([cos, cos], axis=-1), jnp.concatenate([sin, sin], axis=-1)


def _mixer_weights(w_in, w_out, sgu_w, sgu_b, q_norm_g, k_norm_g, diff_lambda, diff_norm_g, gla_w2, gla_b,
                   gla_norm_g, norm1_g, norm2_g, router_w, router_bias):
    w_in_pad = jnp.pad(w_in, ((0, 0), (0, 0), (0, D_PROJ_PAD - w_in.shape[2]))).astype(BF16)
    w2cat = jnp.zeros((DEPTH, LANES, 2 * W_GLA), F32)
    w2cat = w2cat.at[:, 0:GLA_RANK, 0:W_GLA].set(gla_w2[:, 0]).at[:, GLA_RANK:2 * GLA_RANK, W_GLA:].set(gla_w2[:, 1])
    return (
        norm1_g[:, None, :], norm2_g[:, None, :], w_in_pad, w_out.astype(BF16),
        sgu_w.astype(BF16), jnp.repeat(sgu_b.transpose(0, 2, 1), SGU_GROUP_W, axis=2),
        jnp.tile(q_norm_g, (1, W_QK // DIFF_QK))[:, None, :], jnp.tile(k_norm_g, (1, W_QK // DIFF_QK))[:, None, :],
        diff_lambda, diff_norm_g[:, None, :],
        w2cat.astype(BF16), gla_b.reshape(DEPTH, 1, 2 * W_GLA), jnp.tile(gla_norm_g, (1, GLA_HEADS))[:, None, :],
        router_w.T, router_bias[:, None],
    )


def kernel(x_prompt, x_sample, cache_k, cache_v, state_gla, c, c_ctx, w_in, w_out, sgu_w, sgu_b, q_norm_g, k_norm_g,
           diff_lambda, diff_norm_g, gla_w2, gla_b, gla_norm_g, norm1_g, norm2_g, ada_w, ada_b, router_w, router_bias,
           moe_w1, moe_w3, moe_w2):
    n_ctx_seq, ctx_len, _ = x_prompt.shape
    n_lat_seq, lat_len, _ = x_sample.shape
    n_ctx_tok = n_ctx_seq * ctx_len
    n_lat_tok = n_lat_seq * lat_len
    ctx_tiles = n_ctx_tok // ROW_BLOCK
    lat_tiles_per_seq = lat_len // ROW_BLOCK

    cond = jnp.zeros((SUBLANES, D_MODEL), F32).at[0].set(c_ctx).at[1:1 + n_lat_seq].set(c)
    mods_all = _adaln_call(cond, ada_w, ada_b)[:, :1 + n_lat_seq].reshape(DEPTH, 1 + n_lat_seq, 6, D_MODEL)
    weights = _mixer_weights(w_in, w_out, sgu_w, sgu_b, q_norm_g, k_norm_g, diff_lambda, diff_norm_g, gla_w2, gla_b,
                             gla_norm_g, norm1_g, norm2_g, router_w, router_bias)

    ck_all = cache_k.transpose(0, 1, 2, 4, 3, 5).reshape(cache_k.shape[:3] + (cache_k.shape[4], DIFF_V))
    st_all = jnp.einsum('bldhkv,hg->bldhvgk', state_gla, jnp.eye(GLA_HEADS, dtype=F32)).reshape(
        n_lat_seq, DEPTH, 2, W_GLA, W_GLA)
    cos, sin = _rope_tables(lat_len)
    extras = (ck_all, cache_v, st_all, cos, sin)

    def mod_row_of_tile(i):
        return jnp.where(i < ctx_tiles, 0, 1 + (i - ctx_tiles) // lat_tiles_per_seq)

    x_c = x_prompt.reshape(n_ctx_tok, D_MODEL)
    x_l = x_sample.reshape(n_lat_tok, D_MODEL)
    cache_bufs = ()
    for l in range(DEPTH):
        x1_c, hn_c, idx_c, wcol_c, *cache_bufs = _mixer_call(
            l, ctx_len, False, x_c, mods_all, weights, None, tuple(cache_bufs))
        x1_l, hn_l, idx_l, wcol_l = _mixer_call(l, lat_len, True, x_l, mods_all, weights, extras, ())
        idx = jnp.concatenate([idx_c, idx_l], axis=1)
        rank, counts = _rank_call(idx)
        dst, tile_expert, tile_rows, n_slots = _moe_plan(idx, rank, counts[:, 0])
        xs = _dispatch_call(hn_c, hn_l, dst, n_slots)
        ys = _expert_call(l, xs, tile_expert, tile_rows, moe_w1, moe_w3, moe_w2)
        x_c, x_l = _combine_call(l, x1_c, x1_l, wcol_c, wcol_l, mods_all, mod_row_of_tile, ys, dst)

    new_k, new_v, new_s = cache_bufs
    return (x_c.reshape(x_prompt.shape), x_l.reshape(x_sample.shape), new_k, new_v, new_s)
```

```python
import functools
import math

import jax
import jax.numpy as jnp
from jax import lax
from jax.experimental import pallas as pl
from jax.experimental.pallas import tpu as pltpu

F32 = jnp.float32
BF16 = jnp.bfloat16

D_MODEL = 1024
DEPTH = 4
GRID_W = 64
SGU_GROUPS = 4
SGU_GROUP_W = 64
SGU_W = SGU_GROUPS * SGU_GROUP_W
SGU_CHUNK = 128
DIFF_HEADS = 4
DIFF_QK = 64
DIFF_V = 2 * DIFF_QK
ROPE_BASE = 10000.0
GLA_HEADS = 4
GLA_DK = 64
GLA_DV = 64
GLA_RANK = 16
GLA_GATE_NORM = 16.0
GLA_CHUNK = 64
N_EXPERTS = 16
N_GROUPS = 4
EXPERTS_PER_GROUP = N_EXPERTS // N_GROUPS
D_EXPERT = 512
EPS = 1e-6

LANES = 128
SUBLANES = 8
MXU_DIM = 256

ROW_BLOCK = MXU_DIM
ROW_SLABS = D_MODEL // LANES

C_AU, C_AV, C_BQ, C_BK, C_BV = 0, 256, 512, 1024, 1536
C_CQ, C_CK, C_CV, C_CR, C_LR = 2048, 2304, 2560, 2816, 3072
D_PROJ_MAIN = 3072
D_PROJ_PAD = D_PROJ_MAIN + LANES
W_QK = DIFF_HEADS * 2 * DIFF_QK
W_GLA = GLA_HEADS * GLA_DK
M_A, M_B, M_C = 0, SGU_W, SGU_W + DIFF_HEADS * DIFF_V


def _split2(x):
    hi = x.astype(BF16)
    lo = (x - hi.astype(F32)).astype(BF16)
    return hi, lo


def _split3(x):
    hi = x.astype(BF16)
    r = x - hi.astype(F32)
    mid = r.astype(BF16)
    lo = (r - mid.astype(F32)).astype(BF16)
    return hi, mid, lo


def _dot(a, b):
    return jnp.dot(a, b, preferred_element_type=F32)


def _dot_nt(a, b):
    return lax.dot_general(a, b, (((1,), (1,)), ((), ())), preferred_element_type=F32)


def _dot_tn(a, b):
    return lax.dot_general(a, b, (((0,), (0,)), ((), ())), preferred_element_type=F32)


def _iota(shape, dim):
    return lax.broadcasted_iota(jnp.int32, shape, dim)


def _block_ones(width, block):
    r = _iota((width, width), 0) // block
    c = _iota((width, width), 1) // block
    return (r == c)


def _group_sum(z, block):
    width = z.shape[-1]
    outs = []
    for s in range(0, width, MXU_DIM):
        w = min(MXU_DIM, width - s)
        ones = _block_ones(w, block).astype(BF16)
        hi, lo = _split2(z[:, s:s + w])
        outs.append(_dot(hi, ones) + _dot(lo, ones))
    return outs[0] if len(outs) == 1 else jnp.concatenate(outs, axis=-1)


def _group_rms(z, block):
    ms = _group_sum(z * z, block) * (1.0 / block)
    return z * lax.rsqrt(ms + EPS)


def _row_rms(z):
    return z * lax.rsqrt(jnp.mean(z * z, axis=-1, keepdims=True) + EPS)


def _log_sigmoid(x):
    return jnp.minimum(x, 0.0) - jnp.log1p(jnp.exp(-jnp.abs(x)))


ADA_COLS = 1536


def _adaln_kernel(cond_ref, w_ref, b_ref, o_ref):
    sc = jax.nn.silu(cond_ref[...])
    c_hi, c_lo = _split2(sc)
    w = w_ref[0]
    w_hi = w.astype(BF16)
    w_lo = (w - w_hi.astype(F32)).astype(BF16)
    o_ref[0] = _dot(c_hi, w_hi) + _dot(c_lo, w_hi) + _dot(c_hi, w_lo) + b_ref[0]


def _adaln_call(cond, ada_w, ada_b):
    n_col = 6 * D_MODEL // ADA_COLS
    return pl.pallas_call(
        _adaln_kernel,
        grid=(DEPTH, n_col),
        in_specs=[
            pl.BlockSpec((SUBLANES, D_MODEL), lambda l, j: (0, 0)),
            pl.BlockSpec((1, D_MODEL, ADA_COLS), lambda l, j: (l, 0, j)),
            pl.BlockSpec((1, 1, ADA_COLS), lambda l, j: (l, 0, j)),
        ],
        out_specs=pl.BlockSpec((1, SUBLANES, ADA_COLS), lambda l, j: (l, 0, j)),
        out_shape=jax.ShapeDtypeStruct((DEPTH, SUBLANES, 6 * D_MODEL), F32),
        compiler_params=pltpu.CompilerParams(
            dimension_semantics=("arbitrary", "arbitrary"), vmem_limit_bytes=40 * 1024 * 1024),
        name="adaln",
    )(cond, ada_w, ada_b.reshape(DEPTH, 1, 6 * D_MODEL))


def _route(hn, rwt_ref, rb_ref):
    h_hi, h_lo = _split2(hn)
    rw = rwt_ref[...]
    rw_hi = rw.astype(BF16)
    rw_lo = (rw - rw_hi.astype(F32)).astype(BF16)
    logits = _dot_nt(rw_hi, h_hi) + _dot_nt(rw_hi, h_lo) + _dot_nt(rw_lo, h_hi)
    aff = jax.nn.sigmoid(logits)
    sel = aff + rb_ref[...]
    n_tok = sel.shape[1]

    def top2_sum(a, b, c, d):
        hi1, lo1 = jnp.maximum(a, b), jnp.minimum(a, b)
        hi2, lo2 = jnp.maximum(c, d), jnp.minimum(c, d)
        return jnp.maximum(hi1, hi2) + jnp.maximum(jnp.minimum(hi1, hi2), jnp.maximum(lo1, lo2))

    scores = []
    for g in range(N_GROUPS):
        rows = [sel[EXPERTS_PER_GROUP * g + j:EXPERTS_PER_GROUP * g + j + 1, :] for j in range(EXPERTS_PER_GROUP)]
        scores.append(top2_sum(*rows))
    best = jnp.zeros((1, n_tok), jnp.int32)
    best_score = scores[0]
    for g in range(1, N_GROUPS):
        upd = scores[g] > best_score
        best = jnp.where(upd, g, best)
        best_score = jnp.where(upd, scores[g], best_score)

    eid_i = _iota((N_EXPERTS, n_tok), 0)
    eid = eid_i.astype(F32)
    neg = jnp.float32(-jnp.inf)
    msel = jnp.where(eid_i // EXPERTS_PER_GROUP == best, sel, neg)
    m1 = jnp.max(msel, axis=0, keepdims=True)
    idx1 = jnp.min(jnp.where(msel == m1, eid, float(N_EXPERTS)), axis=0, keepdims=True)
    msel2 = jnp.where(eid == idx1, neg, msel)
    m2 = jnp.max(msel2, axis=0, keepdims=True)
    idx2 = jnp.min(jnp.where(msel2 == m2, eid, float(N_EXPERTS)), axis=0, keepdims=True)
    w1 = jnp.sum(jnp.where(eid == idx1, aff, 0.0), axis=0, keepdims=True)
    w2 = jnp.sum(jnp.where(eid == idx2, aff, 0.0), axis=0, keepdims=True)
    wsum = w1 + w2
    return idx1.astype(jnp.int32), idx2.astype(jnp.int32), w1 / wsum, w2 / wsum


def _local_slots(idx1, idx2):
    n_tok = idx1.shape[1]
    eid = _iota((N_EXPERTS, n_tok), 0)
    hot1, hot2 = eid == idx1, eid == idx2
    hot = jnp.where(hot1, 1.0, jnp.where(hot2, 1.0, 0.0))
    earlier = jnp.where(_iota((n_tok, n_tok), 0) < _iota((n_tok, n_tok), 1), 1.0, 0.0).astype(BF16)
    before_in_expert = _dot(hot.astype(BF16), earlier)
    counts = jnp.sum(hot, axis=1, keepdims=True)
    lower = jnp.where(_iota((N_EXPERTS, N_EXPERTS), 1) < _iota((N_EXPERTS, N_EXPERTS), 0), 1.0, 0.0).astype(BF16)
    first_slot = _dot(lower, jnp.broadcast_to(counts, (N_EXPERTS, LANES)).astype(BF16))[:, 0:1]
    slot = before_in_expert + first_slot
    slot1 = jnp.sum(jnp.where(hot1, slot, 0.0), axis=0, keepdims=True)
    slot2 = jnp.sum(jnp.where(hot2, slot, 0.0), axis=0, keepdims=True)
    return slot1, slot2, counts


def _slot_one_hot(slot1, slot2, v1, v2):
    n_tok = slot1.shape[1]
    row = _iota((2 * n_tok, n_tok), 0).astype(F32)
    return jnp.where(row == slot1, v1, jnp.where(row == slot2, v2, 0.0))


N_MIXER_WEIGHTS = 15


def _mixer_kernel(n_tok, latent, n_alias, lam_init, *refs):
    it = iter(refs)
    x_ref, mod_ref = next(it), next(it)
    (n1_ref, n2_ref, win_ref, wout_ref, sw_ref, sb_ref, qg_ref, kg_ref, dl_ref, dg_ref,
     w2c_ref, gb_ref, gg_ref, rwt_ref, rb_ref) = (next(it) for _ in range(N_MIXER_WEIGHTS))
    if latent:
        ck_ref, cv_ref, st0_ref, cos_ref, sin_ref = (next(it) for _ in range(5))
    for _ in range(n_alias):
        next(it)
    xo_ref, hs_ref, slot_ref, wt_ref, cnt_ref = (next(it) for _ in range(5))
    if not latent:
        ko_ref, vo_ref, so_ref = (next(it) for _ in range(3))
    proj_ref, mix_ref, q_ref, k_ref, v_ref = (next(it) for _ in range(5))
    gq_ref, gki_ref, gke_ref, gv_ref, gr_ref, dec_ref, go_ref, st_ref = (next(it) for _ in range(8))

    n_blk = n_tok // ROW_BLOCK
    n_ctx = k_ref.shape[0] - n_tok
    mod = mod_ref[0, 0]

    if not latent:
        for ref in (ko_ref, vo_ref, so_ref):
            for other in range(1, ref.shape[1]):
                ref[0, other] = jnp.zeros(ref.shape[2:], F32)

    lane_group = _iota((SGU_CHUNK, SGU_W), 1) // SGU_GROUP_W
    blk_r = _iota((ROW_BLOCK, ROW_BLOCK), 0)
    blk_c = _iota((ROW_BLOCK, ROW_BLOCK), 1)
    same_chunk = (blk_r // GLA_CHUNK) == (blk_c // GLA_CHUNK)
    tri = (jnp.where(same_chunk & (blk_c <= blk_r), 1.0, 0.0).astype(BF16),
           jnp.where(same_chunk & (blk_c >= blk_r), 1.0, 0.0).astype(BF16))
    chunks_per_blk = ROW_BLOCK // GLA_CHUNK

    if latent:
        for h in range(DIFF_HEADS):
            k_ref[0:n_ctx, h * DIFF_V:(h + 1) * DIFF_V] = ck_ref[0, 0, h].astype(BF16)
            v_ref[0:n_ctx, h * DIFF_V:(h + 1) * DIFF_V] = cv_ref[0, 0, h].astype(BF16)
        pair_lo = (_iota((ROW_BLOCK, W_QK), 1) % (DIFF_QK // 2)) < (DIFF_QK // 4)

        def rope(z, rows):
            cos = jnp.concatenate([cos_ref[rows, :]] * DIFF_HEADS, axis=-1)
            sin = jnp.concatenate([sin_ref[rows, :]] * DIFF_HEADS, axis=-1)
            shift = DIFF_QK // 4
            swapped = jnp.where(pair_lo, pltpu.roll(z, W_QK - shift, 1), pltpu.roll(z, shift, 1))
            return z * cos + swapped * sin

    def project_block(r, carry):
        rows = pl.ds(pl.multiple_of(r * ROW_BLOCK, ROW_BLOCK), ROW_BLOCK)
        key_rows = pl.ds(pl.multiple_of(n_ctx + r * ROW_BLOCK, ROW_BLOCK), ROW_BLOCK)

        h = _row_rms(x_ref[rows, :]) * n1_ref[0]
        h = h * (1.0 + mod[1:2, :]) + mod[0:1, :]
        proj_ref[...] = _dot(h.astype(BF16), win_ref[0])

        for c in range(ROW_BLOCK // SGU_CHUNK):
            local = slice(c * SGU_CHUNK, (c + 1) * SGU_CHUNK)
            u = jax.nn.gelu(proj_ref[local, C_AU:C_AU + SGU_W])
            v = _group_rms(jax.nn.gelu(proj_ref[local, C_AV:C_AV + SGU_W]), SGU_GROUP_W).astype(BF16)
            s = sb_ref[0]
            for g in range(SGU_GROUPS):
                s = s + jnp.where(lane_group == g, _dot(sw_ref[0, g], v), 0.0)
            mix_ref[pl.ds(pl.multiple_of(r * ROW_BLOCK + c * SGU_CHUNK, SGU_CHUNK), SGU_CHUNK),
                    M_A:M_A + SGU_W] = (u * s).astype(BF16)

        qn = _group_rms(proj_ref[:, C_BQ:C_BQ + W_QK], DIFF_QK) * qg_ref[0]
        kn = _group_rms(proj_ref[:, C_BK:C_BK + W_QK], DIFF_QK) * kg_ref[0]
        vv = proj_ref[:, C_BV:C_BV + W_QK]
        if latent:
            qn, kn = rope(qn, rows), rope(kn, rows)
        else:
            for h in range(DIFF_HEADS):
                for i in range(2):
                    lo = h * DIFF_V + i * DIFF_QK
                    ko_ref[0, 0, h, i, rows, :] = kn[:, lo:lo + DIFF_QK]
                vo_ref[0, 0, h, rows, :] = vv[:, h * DIFF_V:(h + 1) * DIFF_V]
        q_ref[rows, :] = (qn * (DIFF_QK ** -0.5)).astype(BF16)
        k_ref[key_rows, :] = kn.astype(BF16)
        v_ref[key_rows, :] = vv.astype(BF16)

        gpre = _dot(proj_ref[:, C_LR:C_LR + LANES].astype(BF16), w2c_ref[0]) + gb_ref[0]
        gate = _log_sigmoid(gpre) * (1.0 / GLA_GATE_NORM)
        gq = proj_ref[:, C_CQ:C_CQ + W_GLA] * (GLA_DK ** -0.5)
        gk = proj_ref[:, C_CK:C_CK + W_GLA]
        gv_ref[rows, :] = proj_ref[:, C_CV:C_CV + W_GLA].astype(BF16)
        gr_ref[rows, :] = proj_ref[:, C_CR:C_CR + W_GLA]
        for d in range(2):
            g = gate[:, d * W_GLA:(d + 1) * W_GLA]
            b = sum(_dot(tri[d], p) for p in _split3(g))
            last = GLA_CHUNK - 1 if d == 0 else 0
            b_last = jnp.concatenate(
                [jnp.broadcast_to(b[c * GLA_CHUNK + last:c * GLA_CHUNK + last + 1, :], (GLA_CHUNK, W_GLA))
                 for c in range(chunks_per_blk)], axis=0)
            gq_ref[d, rows, :] = (gq * jnp.exp(b)).astype(BF16)
            gki_ref[d, rows, :] = (gk * jnp.exp(-b)).astype(BF16)
            gke_ref[d, rows, :] = (gk * jnp.exp(b_last - b)).astype(BF16)
            for c in range(chunks_per_blk):
                row = c * GLA_CHUNK + last
                dec_ref[d, r * chunks_per_blk + c] = jnp.exp(b[row:row + 1, :])
        return carry

    lax.fori_loop(0, n_blk, project_block, 0)

    dl = dl_ref[0]
    lam = (jnp.exp(jnp.sum(dl[0:1] * dl[1:2], axis=-1, keepdims=True))
           - jnp.exp(jnp.sum(dl[2:3] * dl[3:4], axis=-1, keepdims=True)) + lam_init)
    sub0 = (_iota((ROW_BLOCK, DIFF_V), 1) < DIFF_QK)

    def softmax(s):
        e = jnp.exp(s - jnp.max(s, axis=-1, keepdims=True))
        return e, jnp.sum(e, axis=-1, keepdims=True)

    def attn_block(r, carry):
        rows = pl.ds(pl.multiple_of(r * ROW_BLOCK, ROW_BLOCK), ROW_BLOCK)
        for h in range(DIFF_HEADS):
            cols = slice(h * DIFF_V, (h + 1) * DIFF_V)
            qh = q_ref[rows, cols]
            kh = k_ref[:, cols]
            e0, z0 = softmax(_dot_nt(jnp.where(sub0, qh, jnp.zeros_like(qh)), kh))
            e1, z1 = softmax(_dot_nt(jnp.where(sub0, jnp.zeros_like(qh), qh), kh))
            w = e0 / z0 - lam * (e1 / z1)
            o = _dot(w.astype(BF16), v_ref[:, cols])
            o = _row_rms(o) * dg_ref[0] * (1.0 - lam_init)
            mix_ref[rows, M_B + h * DIFF_V:M_B + (h + 1) * DIFF_V] = o.astype(BF16)
        return carry

    lax.fori_loop(0, n_blk, attn_block, 0)

    if latent:
        st_ref[...] = st0_ref[0, 0]
    else:
        st_ref[...] = jnp.zeros(st_ref.shape, F32)

    n_chunk = n_tok // GLA_CHUNK
    head_of_lane = _iota((GLA_CHUNK, W_GLA), 1) // GLA_DK
    stack_r = _iota((GLA_HEADS * GLA_CHUNK, GLA_CHUNK), 0) % GLA_CHUNK
    stack_c = _iota((GLA_HEADS * GLA_CHUNK, GLA_CHUNK), 1)
    causal = (stack_c <= stack_r, stack_c >= stack_r)
    st_diag = (_iota((W_GLA, W_GLA), 0) // GLA_DV) == (_iota((W_GLA, W_GLA), 1) // GLA_DK)

    def gla_step(c, carry):
        for d in range(2):
            cc = c if d == 0 else n_chunk - 1 - c
            rows = pl.ds(pl.multiple_of(cc * GLA_CHUNK, GLA_CHUNK), GLA_CHUNK)
            qd = gq_ref[d, rows, :]
            vv = gv_ref[rows, :]
            q_stack = jnp.concatenate(
                [jnp.where(head_of_lane == h, qd, jnp.zeros_like(qd)) for h in range(GLA_HEADS)], axis=0)
            attn = jnp.where(causal[d], _dot_nt(q_stack, gki_ref[d, rows, :]), 0.0)
            spread = _dot(attn.astype(BF16), vv)
            o = jnp.zeros((GLA_CHUNK, W_GLA), F32)
            for h in range(GLA_HEADS):
                o = o + jnp.where(head_of_lane == h, spread[h * GLA_CHUNK:(h + 1) * GLA_CHUNK, :], 0.0)
            st = st_ref[d]
            o = o + _dot_nt(qd, st.astype(BF16))
            go_ref[d, rows, :] = o
            upd = _dot_tn(vv, gke_ref[d, rows, :])
            st_ref[d] = dec_ref[d, cc] * st + jnp.where(st_diag, upd, 0.0)
        return carry

    lax.fori_loop(0, n_chunk, gla_step, 0)

    if not latent:
        for d in range(2):
            s_full = st_ref[d].T
            for h in range(GLA_HEADS):
                so_ref[0, 0, d, h] = s_full[h * GLA_DK:(h + 1) * GLA_DK, h * GLA_DV:(h + 1) * GLA_DV]

    for r in range(n_blk):
        rows = pl.ds(r * ROW_BLOCK, ROW_BLOCK)
        oc = _group_rms(go_ref[0, rows, :] + go_ref[1, rows, :], GLA_DV) * gg_ref[0]
        oc = oc * jax.nn.silu(gr_ref[rows, :])
        mix_ref[rows, M_C:M_C + W_GLA] = oc.astype(BF16)
        x1 = x_ref[rows, :] + mod[2:3, :] * _dot(mix_ref[rows, :], wout_ref[0])
        xo_ref[rows, :] = x1
        hn = _row_rms(x1) * n2_ref[0]
        hn = hn * (1.0 + mod[4:5, :]) + mod[3:4, :]
        idx1, idx2, w1, w2 = _route(hn, rwt_ref, rb_ref)
        slot1, slot2, counts = _local_slots(idx1, idx2)
        perm = _slot_one_hot(slot1, slot2, 1.0, 1.0).astype(BF16)
        _to_row_tiles(hs_ref.at[pl.ds(2 * r * ROW_BLOCK, 2 * ROW_BLOCK)], _dot(perm, hn.astype(BF16)))
        slot_ref[:, rows] = jnp.concatenate([slot1, slot2], axis=0).astype(jnp.int32)
        wt_ref[:, rows] = jnp.concatenate([w1, w2], axis=0)
        cnt_ref[r] = jnp.broadcast_to(counts, (N_EXPERTS, LANES)).astype(jnp.int32)


def _mixer_call(l, n_tok, latent, x, mods_all, weights, extras, cache_bufs):
    n_seq = x.shape[0] // n_tok
    n_all = x.shape[0]
    n_keys = n_tok + (extras[0].shape[3] if latent else 0)
    n_chunk = n_tok // GLA_CHUNK
    lam_init = 0.8 - 0.6 * math.exp(-0.3 * l)

    single = pl.Buffered(1)
    seq_mode = single if latent else None

    def layer(arr):
        tail = arr.shape[1:]
        return pl.BlockSpec((1,) + tail, lambda s, _n=len(tail): (l,) + (0,) * _n, pipeline_mode=single)

    def const(arr):
        return pl.BlockSpec(arr.shape, lambda s, _n=arr.ndim: (0,) * _n, pipeline_mode=single)

    def tok_spec(width):
        return pl.BlockSpec((n_tok, width), lambda s: (s, 0), pipeline_mode=seq_mode)

    mod_row = (lambda s: 1 + s) if latent else (lambda s: 0)
    in_specs = [tok_spec(D_MODEL),
                pl.BlockSpec((1, 1, 6, D_MODEL), lambda s: (l, mod_row(s), 0, 0))]
    in_specs += [layer(w) for w in weights[:N_MIXER_WEIGHTS - 2]] + [const(w) for w in weights[-2:]]
    operands = [x, mods_all] + list(weights)
    if latent:
        ck, cv, st0, cos, sin = extras
        in_specs += [
            pl.BlockSpec((1, 1) + ck.shape[2:], lambda s: (s, l, 0, 0, 0)),
            pl.BlockSpec((1, 1) + cv.shape[2:], lambda s: (s, l, 0, 0, 0)),
            pl.BlockSpec((1, 1) + st0.shape[2:], lambda s: (s, l, 0, 0, 0)),
            const(cos), const(sin),
        ]
        operands += [ck, cv, st0, cos, sin]
    n_in = len(operands)
    in_specs += [pl.BlockSpec(memory_space=pl.ANY)] * len(cache_bufs)
    operands += list(cache_bufs)

    tiles_per_seq = n_tok // ROW_BLOCK
    out_shape = [
        jax.ShapeDtypeStruct((n_all, D_MODEL), F32),
        jax.ShapeDtypeStruct((2 * n_all, ROW_SLABS, LANES), F32),
        jax.ShapeDtypeStruct((2, n_all), jnp.int32),
        jax.ShapeDtypeStruct((2, n_all), F32),
        jax.ShapeDtypeStruct((n_all // ROW_BLOCK, N_EXPERTS, LANES), jnp.int32),
    ]
    out_specs = [
        tok_spec(D_MODEL),
        pl.BlockSpec((2 * n_tok, ROW_SLABS, LANES), lambda s: (s, 0, 0), pipeline_mode=seq_mode),
        pl.BlockSpec((2, n_tok), lambda s: (0, s)),
        pl.BlockSpec((2, n_tok), lambda s: (0, s)),
        pl.BlockSpec((tiles_per_seq, N_EXPERTS, LANES), lambda s: (s, 0, 0)),
    ]
    n_shared_out = len(out_shape)
    aliases = {}
    if not latent:
        out_shape += [
            jax.ShapeDtypeStruct((n_seq, DEPTH, DIFF_HEADS, 2, n_tok, DIFF_QK), F32),
            jax.ShapeDtypeStruct((n_seq, DEPTH, DIFF_HEADS, n_tok, DIFF_V), F32),
            jax.ShapeDtypeStruct((n_seq, DEPTH, 2, GLA_HEADS, GLA_DK, GLA_DV), F32),
        ]
        n_lay, at = (1, l) if cache_bufs else (DEPTH, 0)
        out_specs += [
            pl.BlockSpec((1, n_lay, DIFF_HEADS, 2, n_tok, DIFF_QK), lambda s: (s, at, 0, 0, 0, 0)),
            pl.BlockSpec((1, n_lay, DIFF_HEADS, n_tok, DIFF_V), lambda s: (s, at, 0, 0, 0)),
            pl.BlockSpec((1, n_lay, 2, GLA_HEADS, GLA_DK, GLA_DV), lambda s: (s, at, 0, 0, 0, 0)),
        ]
        aliases = {n_in + j: n_shared_out + j for j in range(len(cache_bufs))}
    scratch = [
        pltpu.VMEM((ROW_BLOCK, D_PROJ_PAD), F32),
        pltpu.VMEM((n_tok, D_MODEL), BF16),
        pltpu.VMEM((n_tok, W_QK), BF16),
        pltpu.VMEM((n_keys, W_QK), BF16),
        pltpu.VMEM((n_keys, W_QK), BF16),
        pltpu.VMEM((2, n_tok, W_GLA), BF16),
        pltpu.VMEM((2, n_tok, W_GLA), BF16),
        pltpu.VMEM((2, n_tok, W_GLA), BF16),
        pltpu.VMEM((n_tok, W_GLA), BF16),
        pltpu.VMEM((n_tok, W_GLA), F32),
        pltpu.VMEM((2, n_chunk, 1, W_GLA), F32),
        pltpu.VMEM((2, n_tok, W_GLA), F32),
        pltpu.VMEM((2, W_GLA, W_GLA), F32),
    ]
    return pl.pallas_call(
        functools.partial(_mixer_kernel, n_tok, latent, len(cache_bufs), lam_init),
        grid=(n_seq,),
        in_specs=in_specs,
        out_specs=out_specs,
        out_shape=out_shape,
        scratch_shapes=scratch,
        input_output_aliases=aliases,
        compiler_params=pltpu.CompilerParams(
            dimension_semantics=("arbitrary",), vmem_limit_bytes=56 * 1024 * 1024),
        name="mixer_latent" if latent else "mixer_context",
    )(*operands)


PAIR_BLOCK = 2 * ROW_BLOCK
COPY_SIZES = tuple(ROW_BLOCK >> k for k in range(ROW_BLOCK.bit_length()))


def _segment_copies(n_rows, make_copy, act):
    for size in COPY_SIZES:
        @pl.when((n_rows & size) != 0)
        def _():
            act(make_copy(n_rows & (-2 * size), size))


def _start(copy):
    copy.start()


def _wait(copy):
    copy.wait()


def _to_row_tiles(ref, value):
    for s in range(ROW_SLABS):
        ref[:, s, :] = value[:, s * LANES:(s + 1) * LANES]


def _from_row_tiles(ref):
    return jnp.concatenate([ref[:, s, :] for s in range(ROW_SLABS)], axis=-1)


def _two_streams(n_first_tiles):
    def first(i, *_):
        return (jnp.minimum(i, n_first_tiles - 1), 0)

    def second(i, *_):
        return (jnp.maximum(i - n_first_tiles, 0), 0)

    return first, second


N_EXPERT_TABLES = 8


def _expert_kernel(n_ctx_tiles, te_ref, first_ref, rows_ref, jlo_ref, jhi_ref, cpre_ref, cnt_ref, lofs_ref,
                   hs_c_ref, hs_l_ref, w1_ref, w3_ref, w2_ref, ys_ref, xbuf_ref, w1b_ref, w3b_ref, w2b_ref, sem):
    i = pl.program_id(0)
    n_tiles = pl.num_programs(0)

    def gather(t, act):
        slot = t % 2
        e, first = te_ref[t], first_ref[t]
        last = first + rows_ref[t]

        def segment_of(hs_ref, first_tile):
            def body(j, carry):
                k = j * N_EXPERTS + e
                seg_first = cpre_ref[k]
                lo = jnp.maximum(seg_first, first)
                n = jnp.maximum(jnp.minimum(seg_first + cnt_ref[k], last) - lo, 0)
                src = (j - first_tile) * PAIR_BLOCK + lofs_ref[k] + (lo - seg_first)
                dst = lo - first
                _segment_copies(n, lambda done, size: pltpu.make_async_copy(
                    hs_ref.at[pl.ds(src + done, size)], xbuf_ref.at[slot, pl.ds(dst + done, size)],
                    sem.at[slot]), act)
                return carry
            return body

        jlo, jhi = jlo_ref[t], jhi_ref[t]
        lax.fori_loop(jnp.minimum(jlo, n_ctx_tiles), jnp.minimum(jhi, n_ctx_tiles), segment_of(hs_c_ref, 0), 0)
        lax.fori_loop(jnp.maximum(jlo, n_ctx_tiles), jnp.maximum(jhi, n_ctx_tiles),
                      segment_of(hs_l_ref, n_ctx_tiles), 0)

    @pl.when(i == 0)
    def _():
        xbuf_ref[...] = jnp.zeros(xbuf_ref.shape, F32)
        gather(0, _start)

    @pl.when(i + 1 < n_tiles)
    def _():
        gather(i + 1, _start)

    @pl.when((i == 0) | (te_ref[i] != te_ref[jnp.maximum(i - 1, 0)]))
    def _():
        w1b_ref[...] = w1_ref[0, 0].astype(BF16)
        w3b_ref[...] = w3_ref[0, 0].astype(BF16)
        w2b_ref[...] = w2_ref[0, 0].astype(BF16)

    gather(i, _wait)
    n_rows = rows_ref[i]

    @pl.when(n_rows > 0)
    def _():
        live = _iota((ROW_BLOCK, D_MODEL), 0) < n_rows
        x = jnp.where(live, _from_row_tiles(xbuf_ref.at[i % 2]), 0.0).astype(BF16)
        hid = jax.nn.silu(_dot(x, w1b_ref[...])) * _dot(x, w3b_ref[...])
        _to_row_tiles(ys_ref, _dot(hid.astype(BF16), w2b_ref[...]))

    @pl.when(n_rows == 0)
    def _():
        ys_ref[...] = jnp.zeros(ys_ref.shape, F32)


def _expert_call(l, plan, hs_c, hs_l, w1, w3, w2):
    tables = plan["expert_tables"]
    n_tiles = tables[0].shape[0]
    n_ctx_tiles = hs_c.shape[0] // PAIR_BLOCK

    def weight(shape):
        return pl.BlockSpec((1, 1) + shape, lambda i, te, *_: (l, te[i], 0, 0))

    return pl.pallas_call(
        functools.partial(_expert_kernel, n_ctx_tiles),
        grid_spec=pltpu.PrefetchScalarGridSpec(
            num_scalar_prefetch=N_EXPERT_TABLES,
            grid=(n_tiles,),
            in_specs=[pl.BlockSpec(memory_space=pl.ANY), pl.BlockSpec(memory_space=pl.ANY),
                      weight((D_MODEL, D_EXPERT)), weight((D_MODEL, D_EXPERT)), weight((D_EXPERT, D_MODEL))],
            out_specs=pl.BlockSpec((ROW_BLOCK, ROW_SLABS, LANES), lambda i, *_: (i, 0, 0)),
            scratch_shapes=[pltpu.VMEM((2, ROW_BLOCK, ROW_SLABS, LANES), F32),
                            pltpu.VMEM((D_MODEL, D_EXPERT), BF16), pltpu.VMEM((D_MODEL, D_EXPERT), BF16),
                            pltpu.VMEM((D_EXPERT, D_MODEL), BF16), pltpu.SemaphoreType.DMA((2,))],
        ),
        out_shape=jax.ShapeDtypeStruct((n_tiles * ROW_BLOCK, ROW_SLABS, LANES), F32),
        compiler_params=pltpu.CompilerParams(
            dimension_semantics=("arbitrary",), vmem_limit_bytes=40 * 1024 * 1024),
        name="moe_experts",
    )(*tables, hs_c, hs_l, w1, w3, w2)


N_COMBINE_TABLES = 4


def _combine_kernel(n_ctx_tiles, cnt_ref, cpre_ref, lofs_ref, starts_ref, x_c_ref, x_l_ref, slot_c_ref, slot_l_ref,
                    wt_c_ref, wt_l_ref, mod_ref, ys_ref, xo_c_ref, xo_l_ref, buf_ref, sem):
    j = pl.program_id(0)
    n_tiles = pl.num_programs(0)

    def collect(t, act):
        slot = t % 2

        def body(e, carry):
            k = t * N_EXPERTS + e
            src, dst = starts_ref[e] + cpre_ref[k], lofs_ref[k]
            _segment_copies(cnt_ref[k], lambda done, size: pltpu.make_async_copy(
                ys_ref.at[pl.ds(src + done, size)], buf_ref.at[slot, pl.ds(dst + done, size)], sem.at[slot]), act)
            return carry

        lax.fori_loop(0, N_EXPERTS, body, 0)

    @pl.when(j == 0)
    def _():
        collect(0, _start)

    @pl.when(j + 1 < n_tiles)
    def _():
        collect(j + 1, _start)

    collect(j, _wait)
    rows = _from_row_tiles(buf_ref.at[j % 2])
    gate = mod_ref[0, 0, 5:6, :]

    def finish(x_ref, slot_ref, wt_ref, xo_ref):
        slots, wts = slot_ref[...].astype(F32), wt_ref[...]
        slot1, slot2 = slots[0:1], slots[1:2]
        weight_of_row = jnp.sum(_slot_one_hot(slot1, slot2, wts[0:1], wts[1:2]), axis=1, keepdims=True)
        hi, lo = _split2(rows * weight_of_row)
        perm = _slot_one_hot(slot1, slot2, 1.0, 1.0).astype(BF16)
        y = _dot_tn(perm, hi) + _dot_tn(perm, lo)
        xo_ref[...] = x_ref[...] + gate * y

    @pl.when(j < n_ctx_tiles)
    def _():
        finish(x_c_ref, slot_c_ref, wt_c_ref, xo_c_ref)

    @pl.when(j >= n_ctx_tiles)
    def _():
        finish(x_l_ref, slot_l_ref, wt_l_ref, xo_l_ref)


def _combine_call(l, plan, x_c, x_l, slot_c, slot_l, wt_c, wt_l, mods_all, mod_row_of_tile, ys):
    n_tiles = (x_c.shape[0] + x_l.shape[0]) // ROW_BLOCK
    n_ctx_tiles = x_c.shape[0] // ROW_BLOCK
    first, second = _two_streams(n_ctx_tiles)

    def lanes(index_map):
        return lambda i, *_: index_map(i)[::-1]

    return pl.pallas_call(
        functools.partial(_combine_kernel, n_ctx_tiles),
        grid_spec=pltpu.PrefetchScalarGridSpec(
            num_scalar_prefetch=N_COMBINE_TABLES,
            grid=(n_tiles,),
            in_specs=[pl.BlockSpec((ROW_BLOCK, D_MODEL), first),
                      pl.BlockSpec((ROW_BLOCK, D_MODEL), second),
                      pl.BlockSpec((2, ROW_BLOCK), lanes(first)),
                      pl.BlockSpec((2, ROW_BLOCK), lanes(second)),
                      pl.BlockSpec((2, ROW_BLOCK), lanes(first)),
                      pl.BlockSpec((2, ROW_BLOCK), lanes(second)),
                      pl.BlockSpec((1, 1, 6, D_MODEL), lambda i, *_: (l, mod_row_of_tile(i), 0, 0)),
                      pl.BlockSpec(memory_space=pl.ANY)],
            out_specs=[pl.BlockSpec((ROW_BLOCK, D_MODEL), first),
                       pl.BlockSpec((ROW_BLOCK, D_MODEL), second)],
            scratch_shapes=[pltpu.VMEM((2, PAIR_BLOCK, ROW_SLABS, LANES), F32), pltpu.SemaphoreType.DMA((2,))],
        ),
        out_shape=[jax.ShapeDtypeStruct(x_c.shape, F32), jax.ShapeDtypeStruct(x_l.shape, F32)],
        compiler_params=pltpu.CompilerParams(
            dimension_semantics=("arbitrary",), vmem_limit_bytes=40 * 1024 * 1024),
        name="moe_combine",
    )(*plan["combine_tables"], x_c, x_l, slot_c, slot_l, wt_c, wt_l, mods_all, ys)


def _moe_plan(cnt):
    n_tok_tiles = cnt.shape[0]
    n_tiles = n_tok_tiles * PAIR_BLOCK // ROW_BLOCK + N_EXPERTS
    lofs = jnp.cumsum(cnt, axis=1) - cnt
    cpre = jnp.cumsum(cnt, axis=0) - cnt
    counts = jnp.sum(cnt, axis=0)
    padded = (counts + ROW_BLOCK - 1) // ROW_BLOCK * ROW_BLOCK
    ends = jnp.cumsum(padded)
    starts = ends - padded
    tile_start = jnp.arange(n_tiles, dtype=jnp.int32) * ROW_BLOCK
    tile_expert = jnp.minimum(
        jnp.sum((tile_start[:, None] >= ends[None, :]).astype(jnp.int32), axis=1), N_EXPERTS - 1)
    hot = tile_expert[:, None] == jnp.arange(N_EXPERTS, dtype=jnp.int32)[None, :]
    first = tile_start - jnp.sum(jnp.where(hot, starts[None, :], 0), axis=1)
    rows = jnp.clip(jnp.sum(jnp.where(hot, counts[None, :], 0), axis=1) - first, 0, ROW_BLOCK)
    seg_first = jnp.sum(jnp.where(hot[:, None, :], cpre[None, :, :], 0), axis=2)
    seg_rows = jnp.sum(jnp.where(hot[:, None, :], cnt[None, :, :], 0), axis=2)
    overlap = (seg_first < (first + rows)[:, None]) & (seg_first + seg_rows > first[:, None])
    j = jnp.arange(n_tok_tiles, dtype=jnp.int32)[None, :]
    jlo = jnp.min(jnp.where(overlap, j, n_tok_tiles), axis=1)
    jhi = jnp.max(jnp.where(overlap, j + 1, 0), axis=1)
    i32 = lambda a: a.astype(jnp.int32).reshape(-1)
    return {
        "expert_tables": tuple(i32(a) for a in (tile_expert, first, rows, jlo, jhi, cpre, cnt, lofs)),
        "combine_tables": tuple(i32(a) for a in (cnt, cpre, lofs, starts)),
    }


def _rope_tables(n_tok):
    n_rows = n_tok // GRID_W
    pos_r = jnp.repeat(jnp.arange(n_rows), GRID_W)
    pos_c = jnp.tile(jnp.arange(GRID_W), n_rows)
    half = DIFF_QK // 2
    nf = half // 2
    freqs = ROPE_BASE ** (-jnp.arange(nf, dtype=F32) / nf)

    def tables(pos):
        ang = pos.astype(F32)[:, None] * freqs
        cos, sin = jnp.cos(ang), jnp.sin(ang)
        return jnp.concatenate([cos, cos], axis=-1), jnp.concatenate([-sin, sin], axis=-1)

    cos_r, sin_r = tables(pos_r)
    cos_c, sin_c = tables(pos_c)
    cos = jnp.concatenate([cos_r, cos_c], axis=-1)
    sin = jnp.concatenate([sin_r, sin_c], axis=-1)
    return jnp.concatenate([cos, cos], axis=-1), jnp.concatenate([sin, sin], axis=-1)


def _mixer_weights(w_in, w_out, sgu_w, sgu_b, q_norm_g, k_norm_g, diff_lambda, diff_norm_g, gla_w2, gla_b,
                   gla_norm_g, norm1_g, norm2_g, router_w, router_bias):
    w_in_pad = jnp.pad(w_in, ((0, 0), (0, 0), (0, D_PROJ_PAD - w_in.shape[2]))).astype(BF16)
    w2cat = jnp.zeros((DEPTH, LANES, 2 * W_GLA), F32)
    w2cat = w2cat.at[:, 0:GLA_RANK, 0:W_GLA].set(gla_w2[:, 0]).at[:, GLA_RANK:2 * GLA_RANK, W_GLA:].set(gla_w2[:, 1])
    return (
        norm1_g[:, None, :], norm2_g[:, None, :], w_in_pad, w_out.astype(BF16),
        sgu_w.astype(BF16), jnp.repeat(sgu_b.transpose(0, 2, 1), SGU_GROUP_W, axis=2),
        jnp.tile(q_norm_g, (1, W_QK // DIFF_QK))[:, None, :], jnp.tile(k_norm_g, (1, W_QK // DIFF_QK))[:, None, :],
        diff_lambda, diff_norm_g[:, None, :],
        w2cat.astype(BF16), gla_b.reshape(DEPTH, 1, 2 * W_GLA), jnp.tile(gla_norm_g, (1, GLA_HEADS))[:, None, :],
        router_w.T, router_bias[:, None],
    )


def kernel(x_prompt, x_sample, cache_k, cache_v, state_gla, c, c_ctx, w_in, w_out, sgu_w, sgu_b, q_norm_g, k_norm_g,
           diff_lambda, diff_norm_g, gla_w2, gla_b, gla_norm_g, norm1_g, norm2_g, ada_w, ada_b, router_w, router_bias,
           moe_w1, moe_w3, moe_w2):
    n_ctx_seq, ctx_len, _ = x_prompt.shape
    n_lat_seq, lat_len, _ = x_sample.shape
    n_ctx_tok = n_ctx_seq * ctx_len
    n_lat_tok = n_lat_seq * lat_len
    ctx_tiles = n_ctx_tok // ROW_BLOCK
    lat_tiles_per_seq = lat_len // ROW_BLOCK

    cond = jnp.zeros((SUBLANES, D_MODEL), F32).at[0].set(c_ctx).at[1:1 + n_lat_seq].set(c)
    mods_all = _adaln_call(cond, ada_w, ada_b)[:, :1 + n_lat_seq].reshape(DEPTH, 1 + n_lat_seq, 6, D_MODEL)
    weights = _mixer_weights(w_in, w_out, sgu_w, sgu_b, q_norm_g, k_norm_g, diff_lambda, diff_norm_g, gla_w2, gla_b,
                             gla_norm_g, norm1_g, norm2_g, router_w, router_bias)

    ck_all = cache_k.transpose(0, 1, 2, 4, 3, 5).reshape(cache_k.shape[:3] + (cache_k.shape[4], DIFF_V))
    st_all = jnp.einsum('bldhkv,hg->bldhvgk', state_gla, jnp.eye(GLA_HEADS, dtype=F32)).reshape(
        n_lat_seq, DEPTH, 2, W_GLA, W_GLA)
    cos, sin = _rope_tables(lat_len)
    extras = (ck_all, cache_v, st_all, cos, sin)

    def mod_row_of_tile(i):
        return jnp.where(i < ctx_tiles, 0, 1 + (i - ctx_tiles) // lat_tiles_per_seq)

    x_c = x_prompt.reshape(n_ctx_tok, D_MODEL)
    x_l = x_sample.reshape(n_lat_tok, D_MODEL)
    cache_bufs = ()
    for l in range(DEPTH):
        x1_c, hs_c, slot_c, wt_c, cnt_c, *cache_bufs = _mixer_call(
            l, ctx_len, False, x_c, mods_all, weights, None, tuple(cache_bufs))
        x1_l, hs_l, slot_l, wt_l, cnt_l = _mixer_call(l, lat_len, True, x_l, mods_all, weights, extras, ())
        plan = _moe_plan(jnp.concatenate([cnt_c[:, :, 0], cnt_l[:, :, 0]], axis=0))
        ys = _expert_call(l, plan, hs_c, hs_l, moe_w1, moe_w3, moe_w2)
        x_c, x_l = _combine_call(l, plan, x1_c, x1_l, slot_c, slot_l, wt_c, wt_l, mods_all, mod_row_of_tile, ys)

    new_k, new_v, new_s = cache_bufs
    return (x_c.reshape(x_prompt.shape), x_l.reshape(x_sample.shape), new_k, new_v, new_s)
```

```python
import functools
import math

import jax
import jax.numpy as jnp
from jax import lax
from jax.experimental import pallas as pl
from jax.experimental.pallas import tpu as pltpu

F32 = jnp.float32
BF16 = jnp.bfloat16

D_MODEL = 1024
DEPTH = 4
GRID_W = 64
SGU_GROUPS = 4
SGU_GROUP_W = 64
SGU_W = SGU_GROUPS * SGU_GROUP_W
SGU_CHUNK = 128
DIFF_HEADS = 4
DIFF_QK = 64
DIFF_V = 2 * DIFF_QK
ROPE_BASE = 10000.0
GLA_HEADS = 4
GLA_DK = 64
GLA_DV = 64
GLA_RANK = 16
GLA_GATE_NORM = 16.0
GLA_CHUNK = 64
N_EXPERTS = 16
N_GROUPS = 4
EXPERTS_PER_GROUP = N_EXPERTS // N_GROUPS
D_EXPERT = 512
EPS = 1e-6

LANES = 128
SUBLANES = 8
MXU_DIM = 256

ROW_BLOCK = MXU_DIM
ROW_SLABS = D_MODEL // LANES

C_AU, C_AV, C_BQ, C_BK, C_BV = 0, 256, 512, 1024, 1536
C_CQ, C_CK, C_CV, C_CR, C_LR = 2048, 2304, 2560, 2816, 3072
D_PROJ_MAIN = 3072
D_PROJ_PAD = D_PROJ_MAIN + LANES
W_QK = DIFF_HEADS * 2 * DIFF_QK
W_GLA = GLA_HEADS * GLA_DK
M_A, M_B, M_C = 0, SGU_W, SGU_W + DIFF_HEADS * DIFF_V


def _split2(x):
    hi = x.astype(BF16)
    lo = (x - hi.astype(F32)).astype(BF16)
    return hi, lo


def _split3(x):
    hi = x.astype(BF16)
    r = x - hi.astype(F32)
    mid = r.astype(BF16)
    lo = (r - mid.astype(F32)).astype(BF16)
    return hi, mid, lo


def _dot(a, b):
    return jnp.dot(a, b, preferred_element_type=F32)


def _dot_nt(a, b):
    return lax.dot_general(a, b, (((1,), (1,)), ((), ())), preferred_element_type=F32)


def _dot_tn(a, b):
    return lax.dot_general(a, b, (((0,), (0,)), ((), ())), preferred_element_type=F32)


def _iota(shape, dim):
    return lax.broadcasted_iota(jnp.int32, shape, dim)


def _block_ones(width, block):
    r = _iota((width, width), 0) // block
    c = _iota((width, width), 1) // block
    return (r == c)


def _group_sum(z, block):
    width = z.shape[-1]
    outs = []
    for s in range(0, width, MXU_DIM):
        w = min(MXU_DIM, width - s)
        ones = _block_ones(w, block).astype(BF16)
        hi, lo = _split2(z[:, s:s + w])
        outs.append(_dot(hi, ones) + _dot(lo, ones))
    return outs[0] if len(outs) == 1 else jnp.concatenate(outs, axis=-1)


def _group_rms(z, block):
    ms = _group_sum(z * z, block) * (1.0 / block)
    return z * lax.rsqrt(ms + EPS)


def _row_rms(z):
    return z * lax.rsqrt(jnp.mean(z * z, axis=-1, keepdims=True) + EPS)


def _log_sigmoid(x):
    return jnp.minimum(x, 0.0) - jnp.log1p(jnp.exp(-jnp.abs(x)))


ADA_COLS = 1536


def _adaln_kernel(cond_ref, w_ref, b_ref, o_ref):
    sc = jax.nn.silu(cond_ref[...])
    c_hi, c_lo = _split2(sc)
    w = w_ref[0]
    w_hi = w.astype(BF16)
    w_lo = (w - w_hi.astype(F32)).astype(BF16)
    o_ref[0] = _dot(c_hi, w_hi) + _dot(c_lo, w_hi) + _dot(c_hi, w_lo) + b_ref[0]


def _adaln_call(cond, ada_w, ada_b):
    n_col = 6 * D_MODEL // ADA_COLS
    return pl.pallas_call(
        _adaln_kernel,
        grid=(DEPTH, n_col),
        in_specs=[
            pl.BlockSpec((SUBLANES, D_MODEL), lambda l, j: (0, 0)),
            pl.BlockSpec((1, D_MODEL, ADA_COLS), lambda l, j: (l, 0, j)),
            pl.BlockSpec((1, 1, ADA_COLS), lambda l, j: (l, 0, j)),
        ],
        out_specs=pl.BlockSpec((1, SUBLANES, ADA_COLS), lambda l, j: (l, 0, j)),
        out_shape=jax.ShapeDtypeStruct((DEPTH, SUBLANES, 6 * D_MODEL), F32),
        compiler_params=pltpu.CompilerParams(
            dimension_semantics=("arbitrary", "arbitrary"), vmem_limit_bytes=40 * 1024 * 1024),
        name="adaln",
    )(cond, ada_w, ada_b.reshape(DEPTH, 1, 6 * D_MODEL))


def _route(hn, rwt_ref, rb_ref):
    h_hi, h_lo = _split2(hn)
    rw = rwt_ref[...]
    rw_hi = rw.astype(BF16)
    rw_lo = (rw - rw_hi.astype(F32)).astype(BF16)
    logits = _dot_nt(rw_hi, h_hi) + _dot_nt(rw_hi, h_lo) + _dot_nt(rw_lo, h_hi)
    aff = jax.nn.sigmoid(logits)
    sel = aff + rb_ref[...]
    n_tok = sel.shape[1]

    def top2_sum(a, b, c, d):
        hi1, lo1 = jnp.maximum(a, b), jnp.minimum(a, b)
        hi2, lo2 = jnp.maximum(c, d), jnp.minimum(c, d)
        return jnp.maximum(hi1, hi2) + jnp.maximum(jnp.minimum(hi1, hi2), jnp.maximum(lo1, lo2))

    scores = []
    for g in range(N_GROUPS):
        rows = [sel[EXPERTS_PER_GROUP * g + j:EXPERTS_PER_GROUP * g + j + 1, :] for j in range(EXPERTS_PER_GROUP)]
        scores.append(top2_sum(*rows))
    best = jnp.zeros((1, n_tok), jnp.int32)
    best_score = scores[0]
    for g in range(1, N_GROUPS):
        upd = scores[g] > best_score
        best = jnp.where(upd, g, best)
        best_score = jnp.where(upd, scores[g], best_score)

    eid_i = _iota((N_EXPERTS, n_tok), 0)
    eid = eid_i.astype(F32)
    neg = jnp.float32(-jnp.inf)
    msel = jnp.where(eid_i // EXPERTS_PER_GROUP == best, sel, neg)
    m1 = jnp.max(msel, axis=0, keepdims=True)
    idx1 = jnp.min(jnp.where(msel == m1, eid, float(N_EXPERTS)), axis=0, keepdims=True)
    msel2 = jnp.where(eid == idx1, neg, msel)
    m2 = jnp.max(msel2, axis=0, keepdims=True)
    idx2 = jnp.min(jnp.where(msel2 == m2, eid, float(N_EXPERTS)), axis=0, keepdims=True)
    w1 = jnp.sum(jnp.where(eid == idx1, aff, 0.0), axis=0, keepdims=True)
    w2 = jnp.sum(jnp.where(eid == idx2, aff, 0.0), axis=0, keepdims=True)
    wsum = w1 + w2
    return idx1.astype(jnp.int32), idx2.astype(jnp.int32), w1 / wsum, w2 / wsum


def _local_slots(idx1, idx2):
    n_tok = idx1.shape[1]
    eid = _iota((N_EXPERTS, n_tok), 0)
    hot1, hot2 = eid == idx1, eid == idx2
    hot = jnp.where(hot1, 1.0, jnp.where(hot2, 1.0, 0.0))
    earlier = jnp.where(_iota((n_tok, n_tok), 0) < _iota((n_tok, n_tok), 1), 1.0, 0.0).astype(BF16)
    before_in_expert = _dot(hot.astype(BF16), earlier)
    counts = jnp.sum(hot, axis=1, keepdims=True)
    lower = jnp.where(_iota((N_EXPERTS, N_EXPERTS), 1) < _iota((N_EXPERTS, N_EXPERTS), 0), 1.0, 0.0).astype(BF16)
    first_slot = _dot(lower, jnp.broadcast_to(counts, (N_EXPERTS, LANES)).astype(BF16))[:, 0:1]
    slot = before_in_expert + first_slot
    slot1 = jnp.sum(jnp.where(hot1, slot, 0.0), axis=0, keepdims=True)
    slot2 = jnp.sum(jnp.where(hot2, slot, 0.0), axis=0, keepdims=True)
    return slot1, slot2, counts


def _slot_one_hot(slot1, slot2, v1, v2):
    n_tok = slot1.shape[1]
    row = _iota((2 * n_tok, n_tok), 0).astype(F32)
    return jnp.where(row == slot1, v1, jnp.where(row == slot2, v2, 0.0))


N_MIXER_WEIGHTS = 15


def _mixer_kernel(n_tok, latent, n_alias, lam_init, *refs):
    it = iter(refs)
    x_ref, mod_ref = next(it), next(it)
    (n1_ref, n2_ref, win_ref, wout_ref, sw_ref, sb_ref, qg_ref, kg_ref, dl_ref, dg_ref,
     w2c_ref, gb_ref, gg_ref, rwt_ref, rb_ref) = (next(it) for _ in range(N_MIXER_WEIGHTS))
    if latent:
        ck_ref, cv_ref, st0_ref, cos_ref, sin_ref = (next(it) for _ in range(5))
    for _ in range(n_alias):
        next(it)
    xo_ref, hs_ref, slot_ref, wt_ref, cnt_ref = (next(it) for _ in range(5))
    if not latent:
        ko_ref, vo_ref, so_ref = (next(it) for _ in range(3))
    proj_ref, mix_ref, q_ref, k_ref, v_ref = (next(it) for _ in range(5))
    gq_ref, gki_ref, gke_ref, gv_ref, gr_ref, dec_ref, go_ref, st_ref = (next(it) for _ in range(8))

    n_blk = n_tok // ROW_BLOCK
    n_ctx = k_ref.shape[0] - n_tok
    mod = mod_ref[0, 0]

    if not latent:
        for ref in (ko_ref, vo_ref, so_ref):
            for other in range(1, ref.shape[1]):
                ref[0, other] = jnp.zeros(ref.shape[2:], F32)

    lane_group = _iota((SGU_CHUNK, SGU_W), 1) // SGU_GROUP_W
    blk_r = _iota((ROW_BLOCK, ROW_BLOCK), 0)
    blk_c = _iota((ROW_BLOCK, ROW_BLOCK), 1)
    same_chunk = (blk_r // GLA_CHUNK) == (blk_c // GLA_CHUNK)
    tri = (jnp.where(same_chunk & (blk_c <= blk_r), 1.0, 0.0).astype(BF16),
           jnp.where(same_chunk & (blk_c >= blk_r), 1.0, 0.0).astype(BF16))
    chunks_per_blk = ROW_BLOCK // GLA_CHUNK

    if latent:
        for h in range(DIFF_HEADS):
            k_ref[0:n_ctx, h * DIFF_V:(h + 1) * DIFF_V] = ck_ref[0, 0, h].astype(BF16)
            v_ref[0:n_ctx, h * DIFF_V:(h + 1) * DIFF_V] = cv_ref[0, 0, h].astype(BF16)
        pair_lo = (_iota((ROW_BLOCK, W_QK), 1) % (DIFF_QK // 2)) < (DIFF_QK // 4)

        def rope(z, rows):
            cos = jnp.concatenate([cos_ref[rows, :]] * DIFF_HEADS, axis=-1)
            sin = jnp.concatenate([sin_ref[rows, :]] * DIFF_HEADS, axis=-1)
            shift = DIFF_QK // 4
            swapped = jnp.where(pair_lo, pltpu.roll(z, W_QK - shift, 1), pltpu.roll(z, shift, 1))
            return z * cos + swapped * sin

    def project_block(r, carry):
        rows = pl.ds(pl.multiple_of(r * ROW_BLOCK, ROW_BLOCK), ROW_BLOCK)
        key_rows = pl.ds(pl.multiple_of(n_ctx + r * ROW_BLOCK, ROW_BLOCK), ROW_BLOCK)

        h = _row_rms(x_ref[rows, :]) * n1_ref[0]
        h = h * (1.0 + mod[1:2, :]) + mod[0:1, :]
        proj_ref[...] = _dot(h.astype(BF16), win_ref[0])

        for c in range(ROW_BLOCK // SGU_CHUNK):
            local = slice(c * SGU_CHUNK, (c + 1) * SGU_CHUNK)
            u = jax.nn.gelu(proj_ref[local, C_AU:C_AU + SGU_W])
            v = _group_rms(jax.nn.gelu(proj_ref[local, C_AV:C_AV + SGU_W]), SGU_GROUP_W).astype(BF16)
            s = sb_ref[0]
            for g in range(SGU_GROUPS):
                s = s + jnp.where(lane_group == g, _dot(sw_ref[0, g], v), 0.0)
            mix_ref[pl.ds(pl.multiple_of(r * ROW_BLOCK + c * SGU_CHUNK, SGU_CHUNK), SGU_CHUNK),
                    M_A:M_A + SGU_W] = (u * s).astype(BF16)

        qn = _group_rms(proj_ref[:, C_BQ:C_BQ + W_QK], DIFF_QK) * qg_ref[0]
        kn = _group_rms(proj_ref[:, C_BK:C_BK + W_QK], DIFF_QK) * kg_ref[0]
        vv = proj_ref[:, C_BV:C_BV + W_QK]
        if latent:
            qn, kn = rope(qn, rows), rope(kn, rows)
        else:
            for h in range(DIFF_HEADS):
                for i in range(2):
                    lo = h * DIFF_V + i * DIFF_QK
                    ko_ref[0, 0, h, i, rows, :] = kn[:, lo:lo + DIFF_QK]
                vo_ref[0, 0, h, rows, :] = vv[:, h * DIFF_V:(h + 1) * DIFF_V]
        q_ref[rows, :] = (qn * (DIFF_QK ** -0.5)).astype(BF16)
        k_ref[key_rows, :] = kn.astype(BF16)
        v_ref[key_rows, :] = vv.astype(BF16)

        gpre = _dot(proj_ref[:, C_LR:C_LR + LANES].astype(BF16), w2c_ref[0]) + gb_ref[0]
        gate = _log_sigmoid(gpre) * (1.0 / GLA_GATE_NORM)
        gq = proj_ref[:, C_CQ:C_CQ + W_GLA] * (GLA_DK ** -0.5)
        gk = proj_ref[:, C_CK:C_CK + W_GLA]
        gv_ref[rows, :] = proj_ref[:, C_CV:C_CV + W_GLA].astype(BF16)
        gr_ref[rows, :] = proj_ref[:, C_CR:C_CR + W_GLA]
        for d in range(2):
            g = gate[:, d * W_GLA:(d + 1) * W_GLA]
            b = sum(_dot(tri[d], p) for p in _split3(g))
            last = GLA_CHUNK - 1 if d == 0 else 0
            b_last = jnp.concatenate(
                [jnp.broadcast_to(b[c * GLA_CHUNK + last:c * GLA_CHUNK + last + 1, :], (GLA_CHUNK, W_GLA))
                 for c in range(chunks_per_blk)], axis=0)
            gq_ref[d, rows, :] = (gq * jnp.exp(b)).astype(BF16)
            gki_ref[d, rows, :] = (gk * jnp.exp(-b)).astype(BF16)
            gke_ref[d, rows, :] = (gk * jnp.exp(b_last - b)).astype(BF16)
            for c in range(chunks_per_blk):
                row = c * GLA_CHUNK + last
                dec_ref[d, r * chunks_per_blk + c] = jnp.exp(b[row:row + 1, :])
        return carry

    lax.fori_loop(0, n_blk, project_block, 0)

    dl = dl_ref[0]
    lam = (jnp.exp(jnp.sum(dl[0:1] * dl[1:2], axis=-1, keepdims=True))
           - jnp.exp(jnp.sum(dl[2:3] * dl[3:4], axis=-1, keepdims=True)) + lam_init)
    sub0 = (_iota((ROW_BLOCK, DIFF_V), 1) < DIFF_QK)

    def softmax(s):
        e = jnp.exp(s - jnp.max(s, axis=-1, keepdims=True))
        return e, jnp.sum(e, axis=-1, keepdims=True)

    def attn_block(r, carry):
        rows = pl.ds(pl.multiple_of(r * ROW_BLOCK, ROW_BLOCK), ROW_BLOCK)
        for h in range(DIFF_HEADS):
            cols = slice(h * DIFF_V, (h + 1) * DIFF_V)
            qh = q_ref[rows, cols]
            kh = k_ref[:, cols]
            e0, z0 = softmax(_dot_nt(jnp.where(sub0, qh, jnp.zeros_like(qh)), kh))
            e1, z1 = softmax(_dot_nt(jnp.where(sub0, jnp.zeros_like(qh), qh), kh))
            w = e0 / z0 - lam * (e1 / z1)
            o = _dot(w.astype(BF16), v_ref[:, cols])
            o = _row_rms(o) * dg_ref[0] * (1.0 - lam_init)
            mix_ref[rows, M_B + h * DIFF_V:M_B + (h + 1) * DIFF_V] = o.astype(BF16)
        return carry

    lax.fori_loop(0, n_blk, attn_block, 0)

    if latent:
        st_ref[...] = st0_ref[0, 0]
    else:
        st_ref[...] = jnp.zeros(st_ref.shape, F32)

    n_chunk = n_tok // GLA_CHUNK
    head_of_lane = _iota((GLA_CHUNK, W_GLA), 1) // GLA_DK
    stack_r = _iota((GLA_HEADS * GLA_CHUNK, GLA_CHUNK), 0) % GLA_CHUNK
    stack_c = _iota((GLA_HEADS * GLA_CHUNK, GLA_CHUNK), 1)
    causal = (stack_c <= stack_r, stack_c >= stack_r)
    st_diag = (_iota((W_GLA, W_GLA), 0) // GLA_DV) == (_iota((W_GLA, W_GLA), 1) // GLA_DK)

    def gla_step(c, carry):
        for d in range(2):
            cc = c if d == 0 else n_chunk - 1 - c
            rows = pl.ds(pl.multiple_of(cc * GLA_CHUNK, GLA_CHUNK), GLA_CHUNK)
            qd = gq_ref[d, rows, :]
            vv = gv_ref[rows, :]
            q_stack = jnp.concatenate(
                [jnp.where(head_of_lane == h, qd, jnp.zeros_like(qd)) for h in range(GLA_HEADS)], axis=0)
            attn = jnp.where(causal[d], _dot_nt(q_stack, gki_ref[d, rows, :]), 0.0)
            spread = _dot(attn.astype(BF16), vv)
            o = jnp.zeros((GLA_CHUNK, W_GLA), F32)
            for h in range(GLA_HEADS):
                o = o + jnp.where(head_of_lane == h, spread[h * GLA_CHUNK:(h + 1) * GLA_CHUNK, :], 0.0)
            st = st_ref[d]
            o = o + _dot_nt(qd, st.astype(BF16))
            go_ref[d, rows, :] = o
            upd = _dot_tn(vv, gke_ref[d, rows, :])
            st_ref[d] = dec_ref[d, cc] * st + jnp.where(st_diag, upd, 0.0)
        return carry

    lax.fori_loop(0, n_chunk, gla_step, 0)

    if not latent:
        for d in range(2):
            s_full = st_ref[d].T
            for h in range(GLA_HEADS):
                so_ref[0, 0, d, h] = s_full[h * GLA_DK:(h + 1) * GLA_DK, h * GLA_DV:(h + 1) * GLA_DV]

    for r in range(n_blk):
        rows = pl.ds(r * ROW_BLOCK, ROW_BLOCK)
        oc = _group_rms(go_ref[0, rows, :] + go_ref[1, rows, :], GLA_DV) * gg_ref[0]
        oc = oc * jax.nn.silu(gr_ref[rows, :])
        mix_ref[rows, M_C:M_C + W_GLA] = oc.astype(BF16)
        x1 = x_ref[rows, :] + mod[2:3, :] * _dot(mix_ref[rows, :], wout_ref[0])
        xo_ref[rows, :] = x1
        hn = _row_rms(x1) * n2_ref[0]
        hn = hn * (1.0 + mod[4:5, :]) + mod[3:4, :]
        idx1, idx2, w1, w2 = _route(hn, rwt_ref, rb_ref)
        slot1, slot2, counts = _local_slots(idx1, idx2)
        perm = _slot_one_hot(slot1, slot2, 1.0, 1.0).astype(BF16)
        _to_row_slabs(hs_ref, 2 * r * ROW_BLOCK, _dot(perm, hn.astype(BF16)))
        slot_ref[:, rows] = jnp.concatenate([slot1, slot2], axis=0).astype(jnp.int32)
        wt_ref[:, rows] = jnp.concatenate([w1, w2], axis=0)
        cnt_ref[r] = jnp.broadcast_to(counts, (N_EXPERTS, LANES)).astype(jnp.int32)


def _mixer_call(l, n_tok, latent, x, mods_all, weights, extras, cache_bufs):
    n_seq = x.shape[0] // n_tok
    n_all = x.shape[0]
    n_keys = n_tok + (extras[0].shape[3] if latent else 0)
    n_chunk = n_tok // GLA_CHUNK
    lam_init = 0.8 - 0.6 * math.exp(-0.3 * l)

    single = pl.Buffered(1)
    seq_mode = single if latent else None

    def layer(arr):
        tail = arr.shape[1:]
        return pl.BlockSpec((1,) + tail, lambda s, _n=len(tail): (l,) + (0,) * _n, pipeline_mode=single)

    def const(arr):
        return pl.BlockSpec(arr.shape, lambda s, _n=arr.ndim: (0,) * _n, pipeline_mode=single)

    def tok_spec(width):
        return pl.BlockSpec((n_tok, width), lambda s: (s, 0), pipeline_mode=seq_mode)

    mod_row = (lambda s: 1 + s) if latent else (lambda s: 0)
    in_specs = [tok_spec(D_MODEL),
                pl.BlockSpec((1, 1, 6, D_MODEL), lambda s: (l, mod_row(s), 0, 0))]
    in_specs += [layer(w) for w in weights[:N_MIXER_WEIGHTS - 2]] + [const(w) for w in weights[-2:]]
    operands = [x, mods_all] + list(weights)
    if latent:
        ck, cv, st0, cos, sin = extras
        in_specs += [
            pl.BlockSpec((1, 1) + ck.shape[2:], lambda s: (s, l, 0, 0, 0)),
            pl.BlockSpec((1, 1) + cv.shape[2:], lambda s: (s, l, 0, 0, 0)),
            pl.BlockSpec((1, 1) + st0.shape[2:], lambda s: (s, l, 0, 0, 0)),
            const(cos), const(sin),
        ]
        operands += [ck, cv, st0, cos, sin]
    n_in = len(operands)
    in_specs += [pl.BlockSpec(memory_space=pl.ANY)] * len(cache_bufs)
    operands += list(cache_bufs)

    tiles_per_seq = n_tok // ROW_BLOCK
    out_shape = [
        jax.ShapeDtypeStruct((n_all, D_MODEL), F32),
        jax.ShapeDtypeStruct((2 * n_all * ROW_SLABS, LANES), F32),
        jax.ShapeDtypeStruct((2, n_all), jnp.int32),
        jax.ShapeDtypeStruct((2, n_all), F32),
        jax.ShapeDtypeStruct((n_all // ROW_BLOCK, N_EXPERTS, LANES), jnp.int32),
    ]
    out_specs = [
        tok_spec(D_MODEL),
        pl.BlockSpec((2 * n_tok * ROW_SLABS, LANES), lambda s: (s, 0), pipeline_mode=seq_mode),
        pl.BlockSpec((2, n_tok), lambda s: (0, s)),
        pl.BlockSpec((2, n_tok), lambda s: (0, s)),
        pl.BlockSpec((tiles_per_seq, N_EXPERTS, LANES), lambda s: (s, 0, 0)),
    ]
    n_shared_out = len(out_shape)
    aliases = {}
    if not latent:
        out_shape += [
            jax.ShapeDtypeStruct((n_seq, DEPTH, DIFF_HEADS, 2, n_tok, DIFF_QK), F32),
            jax.ShapeDtypeStruct((n_seq, DEPTH, DIFF_HEADS, n_tok, DIFF_V), F32),
            jax.ShapeDtypeStruct((n_seq, DEPTH, 2, GLA_HEADS, GLA_DK, GLA_DV), F32),
        ]
        n_lay, at = (1, l) if cache_bufs else (DEPTH, 0)
        out_specs += [
            pl.BlockSpec((1, n_lay, DIFF_HEADS, 2, n_tok, DIFF_QK), lambda s: (s, at, 0, 0, 0, 0)),
            pl.BlockSpec((1, n_lay, DIFF_HEADS, n_tok, DIFF_V), lambda s: (s, at, 0, 0, 0)),
            pl.BlockSpec((1, n_lay, 2, GLA_HEADS, GLA_DK, GLA_DV), lambda s: (s, at, 0, 0, 0, 0)),
        ]
        aliases = {n_in + j: n_shared_out + j for j in range(len(cache_bufs))}
    scratch = [
        pltpu.VMEM((ROW_BLOCK, D_PROJ_PAD), F32),
        pltpu.VMEM((n_tok, D_MODEL), BF16),
        pltpu.VMEM((n_tok, W_QK), BF16),
        pltpu.VMEM((n_keys, W_QK), BF16),
        pltpu.VMEM((n_keys, W_QK), BF16),
        pltpu.VMEM((2, n_tok, W_GLA), BF16),
        pltpu.VMEM((2, n_tok, W_GLA), BF16),
        pltpu.VMEM((2, n_tok, W_GLA), BF16),
        pltpu.VMEM((n_tok, W_GLA), BF16),
        pltpu.VMEM((n_tok, W_GLA), F32),
        pltpu.VMEM((2, n_chunk, 1, W_GLA), F32),
        pltpu.VMEM((2, n_tok, W_GLA), F32),
        pltpu.VMEM((2, W_GLA, W_GLA), F32),
    ]
    return pl.pallas_call(
        functools.partial(_mixer_kernel, n_tok, latent, len(cache_bufs), lam_init),
        grid=(n_seq,),
        in_specs=in_specs,
        out_specs=out_specs,
        out_shape=out_shape,
        scratch_shapes=scratch,
        input_output_aliases=aliases,
        compiler_params=pltpu.CompilerParams(
            dimension_semantics=("arbitrary",), vmem_limit_bytes=56 * 1024 * 1024),
        name="mixer_latent" if latent else "mixer_context",
    )(*operands)


PAIR_BLOCK = 2 * ROW_BLOCK
COPY_SIZES = tuple(ROW_BLOCK >> k for k in range(ROW_BLOCK.bit_length()))


def _segment_copies(n_rows, make_copy, act):
    for size in COPY_SIZES:
        @pl.when((n_rows & size) != 0)
        def _():
            act(make_copy(n_rows & (-2 * size), size))


def _start(copy):
    copy.start()


def _wait(copy):
    copy.wait()


def _slab_rows(first_row, n_rows, slab):
    return pl.ds(first_row * ROW_SLABS + slab, n_rows, stride=ROW_SLABS)


def _to_row_slabs(ref, first_row, value):
    for s in range(ROW_SLABS):
        ref[_slab_rows(first_row, value.shape[0], s), :] = value[:, s * LANES:(s + 1) * LANES]


def _from_row_slabs(ref, first_row, n_rows):
    return jnp.concatenate([ref[_slab_rows(first_row, n_rows, s), :] for s in range(ROW_SLABS)], axis=-1)


def _row_span(ref, first_row, n_rows):
    return ref.at[pl.ds(pl.multiple_of(first_row * ROW_SLABS, ROW_SLABS), n_rows * ROW_SLABS)]


def _two_streams(n_first_tiles):
    def first(i, *_):
        return (jnp.minimum(i, n_first_tiles - 1), 0)

    def second(i, *_):
        return (jnp.maximum(i - n_first_tiles, 0), 0)

    return first, second


N_EXPERT_TABLES = 8


def _expert_kernel(n_ctx_tiles, te_ref, first_ref, rows_ref, jlo_ref, jhi_ref, cpre_ref, cnt_ref, lofs_ref,
                   hs_c_ref, hs_l_ref, w1_ref, w3_ref, w2_ref, ys_ref, xbuf_ref, w1b_ref, w3b_ref, w2b_ref, sem):
    i = pl.program_id(0)
    n_tiles = pl.num_programs(0)

    def gather(t, act):
        slot = t % 2
        e, first = te_ref[t], first_ref[t]
        last = first + rows_ref[t]

        def segment_of(hs_ref, first_tile):
            def body(j, carry):
                k = j * N_EXPERTS + e
                seg_first = cpre_ref[k]
                lo = jnp.maximum(seg_first, first)
                n = jnp.maximum(jnp.minimum(seg_first + cnt_ref[k], last) - lo, 0)
                src = (j - first_tile) * PAIR_BLOCK + lofs_ref[k] + (lo - seg_first)
                dst = slot * ROW_BLOCK + lo - first
                _segment_copies(n, lambda done, size: pltpu.make_async_copy(
                    _row_span(hs_ref, src + done, size), _row_span(xbuf_ref, dst + done, size), sem.at[slot]), act)
                return carry
            return body

        jlo, jhi = jlo_ref[t], jhi_ref[t]
        lax.fori_loop(jnp.minimum(jlo, n_ctx_tiles), jnp.minimum(jhi, n_ctx_tiles), segment_of(hs_c_ref, 0), 0)
        lax.fori_loop(jnp.maximum(jlo, n_ctx_tiles), jnp.maximum(jhi, n_ctx_tiles),
                      segment_of(hs_l_ref, n_ctx_tiles), 0)

    @pl.when(i == 0)
    def _():
        xbuf_ref[...] = jnp.zeros(xbuf_ref.shape, F32)
        gather(0, _start)

    @pl.when(i + 1 < n_tiles)
    def _():
        gather(i + 1, _start)

    @pl.when((i == 0) | (te_ref[i] != te_ref[jnp.maximum(i - 1, 0)]))
    def _():
        w1b_ref[...] = w1_ref[0, 0].astype(BF16)
        w3b_ref[...] = w3_ref[0, 0].astype(BF16)
        w2b_ref[...] = w2_ref[0, 0].astype(BF16)

    gather(i, _wait)
    n_rows = rows_ref[i]

    @pl.when(n_rows > 0)
    def _():
        live = _iota((ROW_BLOCK, D_MODEL), 0) < n_rows
        x = jnp.where(live, _from_row_slabs(xbuf_ref, (i % 2) * ROW_BLOCK, ROW_BLOCK), 0.0).astype(BF16)
        hid = jax.nn.silu(_dot(x, w1b_ref[...])) * _dot(x, w3b_ref[...])
        _to_row_slabs(ys_ref, 0, _dot(hid.astype(BF16), w2b_ref[...]))

    @pl.when(n_rows == 0)
    def _():
        ys_ref[...] = jnp.zeros(ys_ref.shape, F32)


def _expert_call(l, plan, hs_c, hs_l, w1, w3, w2):
    tables = plan["expert_tables"]
    n_tiles = tables[0].shape[0]
    n_ctx_tiles = hs_c.shape[0] // (PAIR_BLOCK * ROW_SLABS)

    def weight(shape):
        return pl.BlockSpec((1, 1) + shape, lambda i, te, *_: (l, te[i], 0, 0))

    return pl.pallas_call(
        functools.partial(_expert_kernel, n_ctx_tiles),
        grid_spec=pltpu.PrefetchScalarGridSpec(
            num_scalar_prefetch=N_EXPERT_TABLES,
            grid=(n_tiles,),
            in_specs=[pl.BlockSpec(memory_space=pl.ANY), pl.BlockSpec(memory_space=pl.ANY),
                      weight((D_MODEL, D_EXPERT)), weight((D_MODEL, D_EXPERT)), weight((D_EXPERT, D_MODEL))],
            out_specs=pl.BlockSpec((ROW_BLOCK * ROW_SLABS, LANES), lambda i, *_: (i, 0)),
            scratch_shapes=[pltpu.VMEM((2 * ROW_BLOCK * ROW_SLABS, LANES), F32),
                            pltpu.VMEM((D_MODEL, D_EXPERT), BF16), pltpu.VMEM((D_MODEL, D_EXPERT), BF16),
                            pltpu.VMEM((D_EXPERT, D_MODEL), BF16), pltpu.SemaphoreType.DMA((2,))],
        ),
        out_shape=jax.ShapeDtypeStruct((n_tiles * ROW_BLOCK * ROW_SLABS, LANES), F32),
        compiler_params=pltpu.CompilerParams(
            dimension_semantics=("arbitrary",), vmem_limit_bytes=40 * 1024 * 1024),
        name="moe_experts",
    )(*tables, hs_c, hs_l, w1, w3, w2)


N_COMBINE_TABLES = 4


def _combine_kernel(n_ctx_tiles, cnt_ref, cpre_ref, lofs_ref, starts_ref, x_c_ref, x_l_ref, slot_c_ref, slot_l_ref,
                    wt_c_ref, wt_l_ref, mod_ref, ys_ref, xo_c_ref, xo_l_ref, buf_ref, sem):
    j = pl.program_id(0)
    n_tiles = pl.num_programs(0)

    def collect(t, act):
        slot = t % 2

        def body(e, carry):
            k = t * N_EXPERTS + e
            src, dst = starts_ref[e] + cpre_ref[k], slot * PAIR_BLOCK + lofs_ref[k]
            _segment_copies(cnt_ref[k], lambda done, size: pltpu.make_async_copy(
                _row_span(ys_ref, src + done, size), _row_span(buf_ref, dst + done, size), sem.at[slot]), act)
            return carry

        lax.fori_loop(0, N_EXPERTS, body, 0)

    @pl.when(j == 0)
    def _():
        collect(0, _start)

    @pl.when(j + 1 < n_tiles)
    def _():
        collect(j + 1, _start)

    collect(j, _wait)
    rows = _from_row_slabs(buf_ref, (j % 2) * PAIR_BLOCK, PAIR_BLOCK)
    gate = mod_ref[0, 0, 5:6, :]

    def finish(x_ref, slot_ref, wt_ref, xo_ref):
        slots, wts = slot_ref[...].astype(F32), wt_ref[...]
        slot1, slot2 = slots[0:1], slots[1:2]
        weight_of_row = jnp.sum(_slot_one_hot(slot1, slot2, wts[0:1], wts[1:2]), axis=1, keepdims=True)
        hi, lo = _split2(rows * weight_of_row)
        perm = _slot_one_hot(slot1, slot2, 1.0, 1.0).astype(BF16)
        y = _dot_tn(perm, hi) + _dot_tn(perm, lo)
        xo_ref[...] = x_ref[...] + gate * y

    @pl.when(j < n_ctx_tiles)
    def _():
        finish(x_c_ref, slot_c_ref, wt_c_ref, xo_c_ref)

    @pl.when(j >= n_ctx_tiles)
    def _():
        finish(x_l_ref, slot_l_ref, wt_l_ref, xo_l_ref)


def _combine_call(l, plan, x_c, x_l, slot_c, slot_l, wt_c, wt_l, mods_all, mod_row_of_tile, ys):
    n_tiles = (x_c.shape[0] + x_l.shape[0]) // ROW_BLOCK
    n_ctx_tiles = x_c.shape[0] // ROW_BLOCK
    first, second = _two_streams(n_ctx_tiles)

    def lanes(index_map):
        return lambda i, *_: index_map(i)[::-1]

    return pl.pallas_call(
        functools.partial(_combine_kernel, n_ctx_tiles),
        grid_spec=pltpu.PrefetchScalarGridSpec(
            num_scalar_prefetch=N_COMBINE_TABLES,
            grid=(n_tiles,),
            in_specs=[pl.BlockSpec((ROW_BLOCK, D_MODEL), first),
                      pl.BlockSpec((ROW_BLOCK, D_MODEL), second),
                      pl.BlockSpec((2, ROW_BLOCK), lanes(first)),
                      pl.BlockSpec((2, ROW_BLOCK), lanes(second)),
                      pl.BlockSpec((2, ROW_BLOCK), lanes(first)),
                      pl.BlockSpec((2, ROW_BLOCK), lanes(second)),
                      pl.BlockSpec((1, 1, 6, D_MODEL), lambda i, *_: (l, mod_row_of_tile(i), 0, 0)),
                      pl.BlockSpec(memory_space=pl.ANY)],
            out_specs=[pl.BlockSpec((ROW_BLOCK, D_MODEL), first),
                       pl.BlockSpec((ROW_BLOCK, D_MODEL), second)],
            scratch_shapes=[pltpu.VMEM((2 * PAIR_BLOCK * ROW_SLABS, LANES), F32), pltpu.SemaphoreType.DMA((2,))],
        ),
        out_shape=[jax.ShapeDtypeStruct(x_c.shape, F32), jax.ShapeDtypeStruct(x_l.shape, F32)],
        compiler_params=pltpu.CompilerParams(
            dimension_semantics=("arbitrary",), vmem_limit_bytes=40 * 1024 * 1024),
        name="moe_combine",
    )(*plan["combine_tables"], x_c, x_l, slot_c, slot_l, wt_c, wt_l, mods_all, ys)


def _moe_plan(cnt):
    n_tok_tiles = cnt.shape[0]
    n_tiles = n_tok_tiles * PAIR_BLOCK // ROW_BLOCK + N_EXPERTS
    lofs = jnp.cumsum(cnt, axis=1) - cnt
    cpre = jnp.cumsum(cnt, axis=0) - cnt
    counts = jnp.sum(cnt, axis=0)
    padded = (counts + ROW_BLOCK - 1) // ROW_BLOCK * ROW_BLOCK
    ends = jnp.cumsum(padded)
    starts = ends - padded
    tile_start = jnp.arange(n_tiles, dtype=jnp.int32) * ROW_BLOCK
    tile_expert = jnp.minimum(
        jnp.sum((tile_start[:, None] >= ends[None, :]).astype(jnp.int32), axis=1), N_EXPERTS - 1)
    hot = tile_expert[:, None] == jnp.arange(N_EXPERTS, dtype=jnp.int32)[None, :]
    first = tile_start - jnp.sum(jnp.where(hot, starts[None, :], 0), axis=1)
    rows = jnp.clip(jnp.sum(jnp.where(hot, counts[None, :], 0), axis=1) - first, 0, ROW_BLOCK)
    seg_first = jnp.sum(jnp.where(hot[:, None, :], cpre[None, :, :], 0), axis=2)
    seg_rows = jnp.sum(jnp.where(hot[:, None, :], cnt[None, :, :], 0), axis=2)
    overlap = (seg_first < (first + rows)[:, None]) & (seg_first + seg_rows > first[:, None])
    j = jnp.arange(n_tok_tiles, dtype=jnp.int32)[None, :]
    jlo = jnp.min(jnp.where(overlap, j, n_tok_tiles), axis=1)
    jhi = jnp.max(jnp.where(overlap, j + 1, 0), axis=1)
    i32 = lambda a: a.astype(jnp.int32).reshape(-1)
    return {
        "expert_tables": tuple(i32(a) for a in (tile_expert, first, rows, jlo, jhi, cpre, cnt, lofs)),
        "combine_tables": tuple(i32(a) for a in (cnt, cpre, lofs, starts)),
    }


def _rope_tables(n_tok):
    n_rows = n_tok // GRID_W
    pos_r = jnp.repeat(jnp.arange(n_rows), GRID_W)
    pos_c = jnp.tile(jnp.arange(GRID_W), n_rows)
    half = DIFF_QK // 2
    nf = half // 2
    freqs = ROPE_BASE ** (-jnp.arange(nf, dtype=F32) / nf)

    def tables(pos):
        ang = pos.astype(F32)[:, None] * freqs
        cos, sin = jnp.cos(ang), jnp.sin(ang)
        return jnp.concatenate([cos, cos], axis=-1), jnp.concatenate([-sin, sin], axis=-1)

    cos_r, sin_r = tables(pos_r)
    cos_c, sin_c = tables(pos_c)
    cos = jnp.concatenate([cos_r, cos_c], axis=-1)
    sin = jnp.concatenate([sin_r, sin_c], axis=-1)
    return jnp.concatenate([cos, cos], axis=-1), jnp.concatenate([sin, sin], axis=-1)


def _mixer_weights(w_in, w_out, sgu_w, sgu_b, q_norm_g, k_norm_g, diff_lambda, diff_norm_g, gla_w2, gla_b,
                   gla_norm_g, norm1_g, norm2_g, router_w, router_bias):
    w_in_pad = jnp.pad(w_in, ((0, 0), (0, 0), (0, D_PROJ_PAD - w_in.shape[2]))).astype(BF16)
    w2cat = jnp.zeros((DEPTH, LANES, 2 * W_GLA), F32)
    w2cat = w2cat.at[:, 0:GLA_RANK, 0:W_GLA].set(gla_w2[:, 0]).at[:, GLA_RANK:2 * GLA_RANK, W_GLA:].set(gla_w2[:, 1])
    return (
        norm1_g[:, None, :], norm2_g[:, None, :], w_in_pad, w_out.astype(BF16),
        sgu_w.astype(BF16), jnp.repeat(sgu_b.transpose(0, 2, 1), SGU_GROUP_W, axis=2),
        jnp.tile(q_norm_g, (1, W_QK // DIFF_QK))[:, None, :], jnp.tile(k_norm_g, (1, W_QK // DIFF_QK))[:, None, :],
        diff_lambda, diff_norm_g[:, None, :],
        w2cat.astype(BF16), gla_b.reshape(DEPTH, 1, 2 * W_GLA), jnp.tile(gla_norm_g, (1, GLA_HEADS))[:, None, :],
        router_w.T, router_bias[:, None],
    )


def kernel(x_prompt, x_sample, cache_k, cache_v, state_gla, c, c_ctx, w_in, w_out, sgu_w, sgu_b, q_norm_g, k_norm_g,
           diff_lambda, diff_norm_g, gla_w2, gla_b, gla_norm_g, norm1_g, norm2_g, ada_w, ada_b, router_w, router_bias,
           moe_w1, moe_w3, moe_w2):
    n_ctx_seq, ctx_len, _ = x_prompt.shape
    n_lat_seq, lat_len, _ = x_sample.shape
    n_ctx_tok = n_ctx_seq * ctx_len
    n_lat_tok = n_lat_seq * lat_len
    ctx_tiles = n_ctx_tok // ROW_BLOCK
    lat_tiles_per_seq = lat_len // ROW_BLOCK

    cond = jnp.zeros((SUBLANES, D_MODEL), F32).at[0].set(c_ctx).at[1:1 + n_lat_seq].set(c)
    mods_all = _adaln_call(cond, ada_w, ada_b)[:, :1 + n_lat_seq].reshape(DEPTH, 1 + n_lat_seq, 6, D_MODEL)
    weights = _mixer_weights(w_in, w_out, sgu_w, sgu_b, q_norm_g, k_norm_g, diff_lambda, diff_norm_g, gla_w2, gla_b,
                             gla_norm_g, norm1_g, norm2_g, router_w, router_bias)

    ck_all = cache_k.transpose(0, 1, 2, 4, 3, 5).reshape(cache_k.shape[:3] + (cache_k.shape[4], DIFF_V))
    st_all = jnp.einsum('bldhkv,hg->bldhvgk', state_gla, jnp.eye(GLA_HEADS, dtype=F32)).reshape(
        n_lat_seq, DEPTH, 2, W_GLA, W_GLA)
    cos, sin = _rope_tables(lat_len)
    extras = (ck_all, cache_v, st_all, cos, sin)

    def mod_row_of_tile(i):
        return jnp.where(i < ctx_tiles, 0, 1 + (i - ctx_tiles) // lat_tiles_per_seq)

    x_c = x_prompt.reshape(n_ctx_tok, D_MODEL)
    x_l = x_sample.reshape(n_lat_tok, D_MODEL)
    cache_bufs = ()
    for l in range(DEPTH):
        x1_c, hs_c, slot_c, wt_c, cnt_c, *cache_bufs = _mixer_call(
            l, ctx_len, False, x_c, mods_all, weights, None, tuple(cache_bufs))
        x1_l, hs_l, slot_l, wt_l, cnt_l = _mixer_call(l, lat_len, True, x_l, mods_all, weights, extras, ())
        plan = _moe_plan(jnp.concatenate([cnt_c[:, :, 0], cnt_l[:, :, 0]], axis=0))
        ys = _expert_call(l, plan, hs_c, hs_l, moe_w1, moe_w3, moe_w2)
        x_c, x_l = _combine_call(l, plan, x1_c, x1_l, slot_c, slot_l, wt_c, wt_l, mods_all, mod_row_of_tile, ys)

    new_k, new_v, new_s = cache_bufs
    return (x_c.reshape(x_prompt.shape), x_l.reshape(x_sample.shape), new_k, new_v, new_s)
```

```python
import functools
import math

import jax
import jax.numpy as jnp
from jax import lax
from jax.experimental import pallas as pl
from jax.experimental.pallas import tpu as pltpu

F32 = jnp.float32
BF16 = jnp.bfloat16

D_MODEL = 1024
DEPTH = 4
GRID_W = 64
SGU_GROUPS = 4
SGU_GROUP_W = 64
SGU_W = SGU_GROUPS * SGU_GROUP_W
SGU_CHUNK = 128
DIFF_HEADS = 4
DIFF_QK = 64
DIFF_V = 2 * DIFF_QK
ROPE_BASE = 10000.0
GLA_HEADS = 4
GLA_DK = 64
GLA_DV = 64
GLA_RANK = 16
GLA_GATE_NORM = 16.0
GLA_CHUNK = 64
N_EXPERTS = 16
N_GROUPS = 4
EXPERTS_PER_GROUP = N_EXPERTS // N_GROUPS
D_EXPERT = 512
EPS = 1e-6

LANES = 128
SUBLANES = 8
MXU_DIM = 256

ROW_BLOCK = MXU_DIM
ROW_SLABS = D_MODEL // LANES

C_AU, C_AV, C_BQ, C_BK, C_BV = 0, 256, 512, 1024, 1536
C_CQ, C_CK, C_CV, C_CR, C_LR = 2048, 2304, 2560, 2816, 3072
D_PROJ_MAIN = 3072
D_PROJ_PAD = D_PROJ_MAIN + LANES
W_QK = DIFF_HEADS * 2 * DIFF_QK
W_GLA = GLA_HEADS * GLA_DK
M_A, M_B, M_C = 0, SGU_W, SGU_W + DIFF_HEADS * DIFF_V


def _split2(x):
    hi = x.astype(BF16)
    lo = (x - hi.astype(F32)).astype(BF16)
    return hi, lo


def _split3(x):
    hi = x.astype(BF16)
    r = x - hi.astype(F32)
    mid = r.astype(BF16)
    lo = (r - mid.astype(F32)).astype(BF16)
    return hi, mid, lo


def _dot(a, b):
    return jnp.dot(a, b, preferred_element_type=F32)


def _dot_nt(a, b):
    return lax.dot_general(a, b, (((1,), (1,)), ((), ())), preferred_element_type=F32)


def _dot_tn(a, b):
    return lax.dot_general(a, b, (((0,), (0,)), ((), ())), preferred_element_type=F32)


def _iota(shape, dim):
    return lax.broadcasted_iota(jnp.int32, shape, dim)


def _block_ones(width, block):
    r = _iota((width, width), 0) // block
    c = _iota((width, width), 1) // block
    return (r == c)


def _group_sum(z, block):
    width = z.shape[-1]
    outs = []
    for s in range(0, width, MXU_DIM):
        w = min(MXU_DIM, width - s)
        ones = _block_ones(w, block).astype(BF16)
        hi, lo = _split2(z[:, s:s + w])
        outs.append(_dot(hi, ones) + _dot(lo, ones))
    return outs[0] if len(outs) == 1 else jnp.concatenate(outs, axis=-1)


def _group_rms(z, block):
    ms = _group_sum(z * z, block) * (1.0 / block)
    return z * lax.rsqrt(ms + EPS)


def _row_rms(z):
    return z * lax.rsqrt(jnp.mean(z * z, axis=-1, keepdims=True) + EPS)


def _log_sigmoid(x):
    return jnp.minimum(x, 0.0) - jnp.log1p(jnp.exp(-jnp.abs(x)))


ADA_COLS = 1536


def _adaln_kernel(cond_ref, w_ref, b_ref, o_ref):
    sc = jax.nn.silu(cond_ref[...])
    c_hi, c_lo = _split2(sc)
    w = w_ref[0]
    w_hi = w.astype(BF16)
    w_lo = (w - w_hi.astype(F32)).astype(BF16)
    o_ref[0] = _dot(c_hi, w_hi) + _dot(c_lo, w_hi) + _dot(c_hi, w_lo) + b_ref[0]


def _adaln_call(cond, ada_w, ada_b):
    n_col = 6 * D_MODEL // ADA_COLS
    return pl.pallas_call(
        _adaln_kernel,
        grid=(DEPTH, n_col),
        in_specs=[
            pl.BlockSpec((SUBLANES, D_MODEL), lambda l, j: (0, 0)),
            pl.BlockSpec((1, D_MODEL, ADA_COLS), lambda l, j: (l, 0, j)),
            pl.BlockSpec((1, 1, ADA_COLS), lambda l, j: (l, 0, j)),
        ],
        out_specs=pl.BlockSpec((1, SUBLANES, ADA_COLS), lambda l, j: (l, 0, j)),
        out_shape=jax.ShapeDtypeStruct((DEPTH, SUBLANES, 6 * D_MODEL), F32),
        compiler_params=pltpu.CompilerParams(
            dimension_semantics=("arbitrary", "arbitrary"), vmem_limit_bytes=40 * 1024 * 1024),
        name="adaln",
    )(cond, ada_w, ada_b.reshape(DEPTH, 1, 6 * D_MODEL))


def _route(hn, rwt_ref, rb_ref):
    h_hi, h_lo = _split2(hn)
    rw = rwt_ref[...]
    rw_hi = rw.astype(BF16)
    rw_lo = (rw - rw_hi.astype(F32)).astype(BF16)
    logits = _dot_nt(rw_hi, h_hi) + _dot_nt(rw_hi, h_lo) + _dot_nt(rw_lo, h_hi)
    aff = jax.nn.sigmoid(logits)
    sel = aff + rb_ref[...]
    n_tok = sel.shape[1]

    def top2_sum(a, b, c, d):
        hi1, lo1 = jnp.maximum(a, b), jnp.minimum(a, b)
        hi2, lo2 = jnp.maximum(c, d), jnp.minimum(c, d)
        return jnp.maximum(hi1, hi2) + jnp.maximum(jnp.minimum(hi1, hi2), jnp.maximum(lo1, lo2))

    scores = []
    for g in range(N_GROUPS):
        rows = [sel[EXPERTS_PER_GROUP * g + j:EXPERTS_PER_GROUP * g + j + 1, :] for j in range(EXPERTS_PER_GROUP)]
        scores.append(top2_sum(*rows))
    best = jnp.zeros((1, n_tok), jnp.int32)
    best_score = scores[0]
    for g in range(1, N_GROUPS):
        upd = scores[g] > best_score
        best = jnp.where(upd, g, best)
        best_score = jnp.where(upd, scores[g], best_score)

    eid_i = _iota((N_EXPERTS, n_tok), 0)
    eid = eid_i.astype(F32)
    neg = jnp.float32(-jnp.inf)
    msel = jnp.where(eid_i // EXPERTS_PER_GROUP == best, sel, neg)
    m1 = jnp.max(msel, axis=0, keepdims=True)
    idx1 = jnp.min(jnp.where(msel == m1, eid, float(N_EXPERTS)), axis=0, keepdims=True)
    msel2 = jnp.where(eid == idx1, neg, msel)
    m2 = jnp.max(msel2, axis=0, keepdims=True)
    idx2 = jnp.min(jnp.where(msel2 == m2, eid, float(N_EXPERTS)), axis=0, keepdims=True)
    w1 = jnp.sum(jnp.where(eid == idx1, aff, 0.0), axis=0, keepdims=True)
    w2 = jnp.sum(jnp.where(eid == idx2, aff, 0.0), axis=0, keepdims=True)
    wsum = w1 + w2
    return idx1.astype(jnp.int32), idx2.astype(jnp.int32), w1 / wsum, w2 / wsum


def _local_slots(idx1, idx2):
    n_tok = idx1.shape[1]
    eid = _iota((N_EXPERTS, n_tok), 0)
    hot1, hot2 = eid == idx1, eid == idx2
    hot = jnp.where(hot1, 1.0, jnp.where(hot2, 1.0, 0.0))
    earlier = jnp.where(_iota((n_tok, n_tok), 0) < _iota((n_tok, n_tok), 1), 1.0, 0.0).astype(BF16)
    before_in_expert = _dot(hot.astype(BF16), earlier)
    counts = jnp.sum(hot, axis=1, keepdims=True)
    lower = jnp.where(_iota((N_EXPERTS, N_EXPERTS), 1) < _iota((N_EXPERTS, N_EXPERTS), 0), 1.0, 0.0).astype(BF16)
    first_slot = _dot(lower, jnp.broadcast_to(counts, (N_EXPERTS, LANES)).astype(BF16))[:, 0:1]
    slot = before_in_expert + first_slot
    slot1 = jnp.sum(jnp.where(hot1, slot, 0.0), axis=0, keepdims=True)
    slot2 = jnp.sum(jnp.where(hot2, slot, 0.0), axis=0, keepdims=True)
    return slot1, slot2, counts


def _slot_one_hot(slot1, slot2, v1, v2):
    n_tok = slot1.shape[1]
    row = _iota((2 * n_tok, n_tok), 0).astype(F32)
    return jnp.where(row == slot1, v1, jnp.where(row == slot2, v2, 0.0))


N_MIXER_WEIGHTS = 15
CTX_SEQS_PER_STEP = 2
MAX_INLINE_BLOCKS = 2


def _mixer_kernel(n_tok, n_par, latent, n_alias, lam_init, *refs):
    it = iter(refs)
    x_ref, mod_ref = next(it), next(it)
    (n1_ref, n2_ref, win_ref, wout_ref, sw_ref, sb_ref, qg_ref, kg_ref, dl_ref, dg_ref,
     w2c_ref, gb_ref, gg_ref, rwt_ref, rb_ref) = (next(it) for _ in range(N_MIXER_WEIGHTS))
    if latent:
        ck_ref, cv_ref, st0_ref, cos_ref, sin_ref = (next(it) for _ in range(5))
    for _ in range(n_alias):
        next(it)
    xo_ref, hs_ref, slot_ref, wt_ref, cnt_ref = (next(it) for _ in range(5))
    if not latent:
        ko_ref, vo_ref, so_ref = (next(it) for _ in range(3))
    proj_ref, mix_ref, q_ref, k_ref, v_ref = (next(it) for _ in range(5))
    gq_ref, gki_ref, gke_ref, gv_ref, gr_ref, dec_ref, go_ref, st_ref = (next(it) for _ in range(8))

    n_blk = n_tok // ROW_BLOCK
    n_ctx = k_ref.shape[0] - n_par * n_tok
    n_keys = n_ctx + n_tok
    mod = mod_ref[0, 0]

    def blocks(body):
        if n_par * n_blk <= MAX_INLINE_BLOCKS:
            for r in range(n_par * n_blk):
                body(r)
        else:
            def step(r, carry):
                body(r)
                return carry
            lax.fori_loop(0, n_par * n_blk, step, 0)

    def aligned(start, size):
        return pl.ds(start if isinstance(start, int) else pl.multiple_of(start, size), size)

    def block_rows(r, offset=0):
        return aligned(offset + r * ROW_BLOCK, ROW_BLOCK)

    if not latent:
        for ref in (ko_ref, vo_ref, so_ref):
            for q in range(n_par):
                for other in range(1, ref.shape[1]):
                    ref[q, other] = jnp.zeros(ref.shape[2:], F32)

    lane_group = _iota((SGU_CHUNK, SGU_W), 1) // SGU_GROUP_W
    blk_r = _iota((ROW_BLOCK, ROW_BLOCK), 0)
    blk_c = _iota((ROW_BLOCK, ROW_BLOCK), 1)
    same_chunk = (blk_r // GLA_CHUNK) == (blk_c // GLA_CHUNK)
    tri = (jnp.where(same_chunk & (blk_c <= blk_r), 1.0, 0.0).astype(BF16),
           jnp.where(same_chunk & (blk_c >= blk_r), 1.0, 0.0).astype(BF16))
    chunks_per_blk = ROW_BLOCK // GLA_CHUNK

    if latent:
        for h in range(DIFF_HEADS):
            k_ref[0:n_ctx, h * DIFF_V:(h + 1) * DIFF_V] = ck_ref[0, 0, h].astype(BF16)
            v_ref[0:n_ctx, h * DIFF_V:(h + 1) * DIFF_V] = cv_ref[0, 0, h].astype(BF16)
        pair_lo = (_iota((ROW_BLOCK, W_QK), 1) % (DIFF_QK // 2)) < (DIFF_QK // 4)

        def rope(z, rows):
            cos = jnp.concatenate([cos_ref[rows, :]] * DIFF_HEADS, axis=-1)
            sin = jnp.concatenate([sin_ref[rows, :]] * DIFF_HEADS, axis=-1)
            shift = DIFF_QK // 4
            swapped = jnp.where(pair_lo, pltpu.roll(z, W_QK - shift, 1), pltpu.roll(z, shift, 1))
            return z * cos + swapped * sin

    def project_block(r):
        rows = block_rows(r)
        key_rows = block_rows(r, n_ctx)
        seq, seq_rows = r // n_blk, block_rows(r % n_blk)

        h = _row_rms(x_ref[rows, :]) * n1_ref[0]
        h = h * (1.0 + mod[1:2, :]) + mod[0:1, :]
        proj_ref[...] = _dot(h.astype(BF16), win_ref[0])

        for c in range(ROW_BLOCK // SGU_CHUNK):
            local = slice(c * SGU_CHUNK, (c + 1) * SGU_CHUNK)
            u = jax.nn.gelu(proj_ref[local, C_AU:C_AU + SGU_W])
            v = _group_rms(jax.nn.gelu(proj_ref[local, C_AV:C_AV + SGU_W]), SGU_GROUP_W).astype(BF16)
            s = sb_ref[0]
            for g in range(SGU_GROUPS):
                s = s + jnp.where(lane_group == g, _dot(sw_ref[0, g], v), 0.0)
            mix_ref[aligned(r * ROW_BLOCK + c * SGU_CHUNK, SGU_CHUNK), M_A:M_A + SGU_W] = (u * s).astype(BF16)

        qn = _group_rms(proj_ref[:, C_BQ:C_BQ + W_QK], DIFF_QK) * qg_ref[0]
        kn = _group_rms(proj_ref[:, C_BK:C_BK + W_QK], DIFF_QK) * kg_ref[0]
        vv = proj_ref[:, C_BV:C_BV + W_QK]
        if latent:
            qn, kn = rope(qn, rows), rope(kn, rows)
        else:
            for h in range(DIFF_HEADS):
                for i in range(2):
                    lo = h * DIFF_V + i * DIFF_QK
                    ko_ref[seq, 0, h, i, seq_rows, :] = kn[:, lo:lo + DIFF_QK]
                vo_ref[seq, 0, h, seq_rows, :] = vv[:, h * DIFF_V:(h + 1) * DIFF_V]
        q_ref[rows, :] = (qn * (DIFF_QK ** -0.5)).astype(BF16)
        k_ref[key_rows, :] = kn.astype(BF16)
        v_ref[key_rows, :] = vv.astype(BF16)

        gpre = _dot(proj_ref[:, C_LR:C_LR + LANES].astype(BF16), w2c_ref[0]) + gb_ref[0]
        gate = _log_sigmoid(gpre) * (1.0 / GLA_GATE_NORM)
        gq = proj_ref[:, C_CQ:C_CQ + W_GLA] * (GLA_DK ** -0.5)
        gk = proj_ref[:, C_CK:C_CK + W_GLA]
        gv_ref[rows, :] = proj_ref[:, C_CV:C_CV + W_GLA].astype(BF16)
        gr_ref[rows, :] = proj_ref[:, C_CR:C_CR + W_GLA]
        for d in range(2):
            g = gate[:, d * W_GLA:(d + 1) * W_GLA]
            b = sum(_dot(tri[d], p) for p in _split3(g))
            last = GLA_CHUNK - 1 if d == 0 else 0
            b_last = jnp.concatenate(
                [jnp.broadcast_to(b[c * GLA_CHUNK + last:c * GLA_CHUNK + last + 1, :], (GLA_CHUNK, W_GLA))
                 for c in range(chunks_per_blk)], axis=0)
            gq_ref[d, rows, :] = (gq * jnp.exp(b)).astype(BF16)
            gki_ref[d, rows, :] = (gk * jnp.exp(-b)).astype(BF16)
            gke_ref[d, rows, :] = (gk * jnp.exp(b_last - b)).astype(BF16)
            for c in range(chunks_per_blk):
                row = c * GLA_CHUNK + last
                dec_ref[d, r * chunks_per_blk + c] = jnp.exp(b[row:row + 1, :])

    blocks(project_block)

    dl = dl_ref[0]
    lam = (jnp.exp(jnp.sum(dl[0:1] * dl[1:2], axis=-1, keepdims=True))
           - jnp.exp(jnp.sum(dl[2:3] * dl[3:4], axis=-1, keepdims=True)) + lam_init)
    sub0 = (_iota((ROW_BLOCK, DIFF_V), 1) < DIFF_QK)

    def softmax(s):
        e = jnp.exp(s - jnp.max(s, axis=-1, keepdims=True))
        return e, jnp.sum(e, axis=-1, keepdims=True)

    def attn_block(r):
        rows = block_rows(r)
        keys = aligned((r // n_blk) * n_keys, n_keys)
        for h in range(DIFF_HEADS):
            cols = slice(h * DIFF_V, (h + 1) * DIFF_V)
            qh = q_ref[rows, cols]
            kh = k_ref[keys, cols]
            e0, z0 = softmax(_dot_nt(jnp.where(sub0, qh, jnp.zeros_like(qh)), kh))
            e1, z1 = softmax(_dot_nt(jnp.where(sub0, jnp.zeros_like(qh), qh), kh))
            w = e0 / z0 - lam * (e1 / z1)
            o = _dot(w.astype(BF16), v_ref[keys, cols])
            o = _row_rms(o) * dg_ref[0] * (1.0 - lam_init)
            mix_ref[rows, M_B + h * DIFF_V:M_B + (h + 1) * DIFF_V] = o.astype(BF16)

    blocks(attn_block)

    if latent:
        st_ref[0] = st0_ref[0, 0]
    else:
        st_ref[...] = jnp.zeros(st_ref.shape, F32)

    n_chunk = n_tok // GLA_CHUNK
    head_of_lane = _iota((GLA_CHUNK, W_GLA), 1) // GLA_DK
    stack_r = _iota((GLA_HEADS * GLA_CHUNK, GLA_CHUNK), 0) % GLA_CHUNK
    stack_c = _iota((GLA_HEADS * GLA_CHUNK, GLA_CHUNK), 1)
    causal = (stack_c <= stack_r, stack_c >= stack_r)
    st_diag = (_iota((W_GLA, W_GLA), 0) // GLA_DV) == (_iota((W_GLA, W_GLA), 1) // GLA_DK)

    def gla_step(c, carry):
        for seq in range(n_par):
            for d in range(2):
                cc = seq * n_chunk + (c if d == 0 else n_chunk - 1 - c)
                rows = pl.ds(pl.multiple_of(cc * GLA_CHUNK, GLA_CHUNK), GLA_CHUNK)
                qd = gq_ref[d, rows, :]
                vv = gv_ref[rows, :]
                q_stack = jnp.concatenate(
                    [jnp.where(head_of_lane == h, qd, jnp.zeros_like(qd)) for h in range(GLA_HEADS)], axis=0)
                attn = jnp.where(causal[d], _dot_nt(q_stack, gki_ref[d, rows, :]), 0.0)
                spread = _dot(attn.astype(BF16), vv)
                o = jnp.zeros((GLA_CHUNK, W_GLA), F32)
                for h in range(GLA_HEADS):
                    o = o + jnp.where(head_of_lane == h, spread[h * GLA_CHUNK:(h + 1) * GLA_CHUNK, :], 0.0)
                st = st_ref[seq, d]
                o = o + _dot_nt(qd, st.astype(BF16))
                go_ref[d, rows, :] = o
                upd = _dot_tn(vv, gke_ref[d, rows, :])
                st_ref[seq, d] = dec_ref[d, cc] * st + jnp.where(st_diag, upd, 0.0)
        return carry

    lax.fori_loop(0, n_chunk, gla_step, 0)

    if not latent:
        for seq in range(n_par):
            for d in range(2):
                s_full = st_ref[seq, d].T
                for h in range(GLA_HEADS):
                    so_ref[seq, 0, d, h] = s_full[h * GLA_DK:(h + 1) * GLA_DK, h * GLA_DV:(h + 1) * GLA_DV]

    for r in range(n_par * n_blk):
        rows = pl.ds(r * ROW_BLOCK, ROW_BLOCK)
        oc = _group_rms(go_ref[0, rows, :] + go_ref[1, rows, :], GLA_DV) * gg_ref[0]
        oc = oc * jax.nn.silu(gr_ref[rows, :])
        mix_ref[rows, M_C:M_C + W_GLA] = oc.astype(BF16)
        x1 = x_ref[rows, :] + mod[2:3, :] * _dot(mix_ref[rows, :], wout_ref[0])
        xo_ref[rows, :] = x1
        hn = _row_rms(x1) * n2_ref[0]
        hn = hn * (1.0 + mod[4:5, :]) + mod[3:4, :]
        idx1, idx2, w1, w2 = _route(hn, rwt_ref, rb_ref)
        slot1, slot2, counts = _local_slots(idx1, idx2)
        perm = _slot_one_hot(slot1, slot2, 1.0, 1.0).astype(BF16)
        _to_row_slabs(hs_ref, 2 * r * ROW_BLOCK, _dot(perm, hn.astype(BF16)))
        slot_ref[:, rows] = jnp.concatenate([slot1, slot2], axis=0).astype(jnp.int32)
        wt_ref[:, rows] = jnp.concatenate([w1, w2], axis=0)
        cnt_ref[r] = jnp.broadcast_to(counts, (N_EXPERTS, LANES)).astype(jnp.int32)


def _mixer_call(l, n_tok, n_par, latent, x, mods_all, weights, extras, cache_bufs):
    n_seq = x.shape[0] // n_tok
    n_all = x.shape[0]
    assert n_seq % n_par == 0 and not (latent and n_par > 1)
    n_step_tok = n_par * n_tok
    n_keys = n_step_tok + (extras[0].shape[3] if latent else 0)
    n_chunk = n_step_tok // GLA_CHUNK
    lam_init = 0.8 - 0.6 * math.exp(-0.3 * l)

    single = pl.Buffered(1)
    seq_mode = single if latent else None

    def layer(arr):
        tail = arr.shape[1:]
        return pl.BlockSpec((1,) + tail, lambda s, _n=len(tail): (l,) + (0,) * _n, pipeline_mode=single)

    def const(arr):
        return pl.BlockSpec(arr.shape, lambda s, _n=arr.ndim: (0,) * _n, pipeline_mode=single)

    def tok_spec(width):
        return pl.BlockSpec((n_step_tok, width), lambda s: (s, 0), pipeline_mode=seq_mode)

    mod_row = (lambda s: 1 + s) if latent else (lambda s: 0)
    in_specs = [tok_spec(D_MODEL),
                pl.BlockSpec((1, 1, 6, D_MODEL), lambda s: (l, mod_row(s), 0, 0))]
    in_specs += [layer(w) for w in weights[:N_MIXER_WEIGHTS - 2]] + [const(w) for w in weights[-2:]]
    operands = [x, mods_all] + list(weights)
    if latent:
        ck, cv, st0, cos, sin = extras
        in_specs += [
            pl.BlockSpec((1, 1) + ck.shape[2:], lambda s: (s, l, 0, 0, 0)),
            pl.BlockSpec((1, 1) + cv.shape[2:], lambda s: (s, l, 0, 0, 0)),
            pl.BlockSpec((1, 1) + st0.shape[2:], lambda s: (s, l, 0, 0, 0)),
            const(cos), const(sin),
        ]
        operands += [ck, cv, st0, cos, sin]
    n_in = len(operands)
    in_specs += [pl.BlockSpec(memory_space=pl.ANY)] * len(cache_bufs)
    operands += list(cache_bufs)

    tiles_per_step = n_step_tok // ROW_BLOCK
    out_shape = [
        jax.ShapeDtypeStruct((n_all, D_MODEL), F32),
        jax.ShapeDtypeStruct((2 * n_all * ROW_SLABS, LANES), F32),
        jax.ShapeDtypeStruct((2, n_all), jnp.int32),
        jax.ShapeDtypeStruct((2, n_all), F32),
        jax.ShapeDtypeStruct((n_all // ROW_BLOCK, N_EXPERTS, LANES), jnp.int32),
    ]
    out_specs = [
        tok_spec(D_MODEL),
        pl.BlockSpec((2 * n_step_tok * ROW_SLABS, LANES), lambda s: (s, 0), pipeline_mode=seq_mode),
        pl.BlockSpec((2, n_step_tok), lambda s: (0, s)),
        pl.BlockSpec((2, n_step_tok), lambda s: (0, s)),
        pl.BlockSpec((tiles_per_step, N_EXPERTS, LANES), lambda s: (s, 0, 0)),
    ]
    n_shared_out = len(out_shape)
    aliases = {}
    if not latent:
        out_shape += [
            jax.ShapeDtypeStruct((n_seq, DEPTH, DIFF_HEADS, 2, n_tok, DIFF_QK), F32),
            jax.ShapeDtypeStruct((n_seq, DEPTH, DIFF_HEADS, n_tok, DIFF_V), F32),
            jax.ShapeDtypeStruct((n_seq, DEPTH, 2, GLA_HEADS, GLA_DK, GLA_DV), F32),
        ]
        n_lay, at = (1, l) if cache_bufs else (DEPTH, 0)
        out_specs += [
            pl.BlockSpec((n_par, n_lay, DIFF_HEADS, 2, n_tok, DIFF_QK), lambda s: (s, at, 0, 0, 0, 0)),
            pl.BlockSpec((n_par, n_lay, DIFF_HEADS, n_tok, DIFF_V), lambda s: (s, at, 0, 0, 0)),
            pl.BlockSpec((n_par, n_lay, 2, GLA_HEADS, GLA_DK, GLA_DV), lambda s: (s, at, 0, 0, 0, 0)),
        ]
        aliases = {n_in + j: n_shared_out + j for j in range(len(cache_bufs))}
    scratch = [
        pltpu.VMEM((ROW_BLOCK, D_PROJ_PAD), F32),
        pltpu.VMEM((n_step_tok, D_MODEL), BF16),
        pltpu.VMEM((n_step_tok, W_QK), BF16),
        pltpu.VMEM((n_keys, W_QK), BF16),
        pltpu.VMEM((n_keys, W_QK), BF16),
        pltpu.VMEM((2, n_step_tok, W_GLA), BF16),
        pltpu.VMEM((2, n_step_tok, W_GLA), BF16),
        pltpu.VMEM((2, n_step_tok, W_GLA), BF16),
        pltpu.VMEM((n_step_tok, W_GLA), BF16),
        pltpu.VMEM((n_step_tok, W_GLA), F32),
        pltpu.VMEM((2, n_chunk, 1, W_GLA), F32),
        pltpu.VMEM((2, n_step_tok, W_GLA), F32),
        pltpu.VMEM((n_par, 2, W_GLA, W_GLA), F32),
    ]
    return pl.pallas_call(
        functools.partial(_mixer_kernel, n_tok, n_par, latent, len(cache_bufs), lam_init),
        grid=(n_seq // n_par,),
        in_specs=in_specs,
        out_specs=out_specs,
        out_shape=out_shape,
        scratch_shapes=scratch,
        input_output_aliases=aliases,
        compiler_params=pltpu.CompilerParams(
            dimension_semantics=("arbitrary",), vmem_limit_bytes=56 * 1024 * 1024),
        name="mixer_latent" if latent else "mixer_context",
    )(*operands)


PAIR_BLOCK = 2 * ROW_BLOCK
COPY_SIZES = tuple(ROW_BLOCK >> k for k in range(ROW_BLOCK.bit_length()))


def _segment_copies(n_rows, make_copy, act):
    for size in COPY_SIZES:
        @pl.when((n_rows & size) != 0)
        def _():
            act(make_copy(n_rows & (-2 * size), size))


def _start(copy):
    copy.start()


def _wait(copy):
    copy.wait()


def _slab_rows(first_row, n_rows, slab):
    return pl.ds(first_row * ROW_SLABS + slab, n_rows, stride=ROW_SLABS)


def _to_row_slabs(ref, first_row, value):
    for s in range(ROW_SLABS):
        ref[_slab_rows(first_row, value.shape[0], s), :] = value[:, s * LANES:(s + 1) * LANES]


def _from_row_slabs(ref, first_row, n_rows):
    return jnp.concatenate([ref[_slab_rows(first_row, n_rows, s), :] for s in range(ROW_SLABS)], axis=-1)


def _row_span(ref, first_row, n_rows):
    return ref.at[pl.ds(pl.multiple_of(first_row * ROW_SLABS, ROW_SLABS), n_rows * ROW_SLABS)]


def _two_streams(n_first_tiles):
    def first(i, *_):
        return (jnp.minimum(i, n_first_tiles - 1), 0)

    def second(i, *_):
        return (jnp.maximum(i - n_first_tiles, 0), 0)

    return first, second


N_EXPERT_TABLES = 8


def _expert_kernel(n_ctx_tiles, te_ref, first_ref, rows_ref, jlo_ref, jhi_ref, cpre_ref, cnt_ref, lofs_ref,
                   hs_c_ref, hs_l_ref, w1_ref, w3_ref, w2_ref, ys_ref, xbuf_ref, w1b_ref, w3b_ref, w2b_ref, sem):
    i = pl.program_id(0)
    n_tiles = pl.num_programs(0)

    def gather(t, act):
        slot = t % 2
        e, first = te_ref[t], first_ref[t]
        last = first + rows_ref[t]

        def segment_of(hs_ref, first_tile):
            def body(j, carry):
                k = j * N_EXPERTS + e
                seg_first = cpre_ref[k]
                lo = jnp.maximum(seg_first, first)
                n = jnp.maximum(jnp.minimum(seg_first + cnt_ref[k], last) - lo, 0)
                src = (j - first_tile) * PAIR_BLOCK + lofs_ref[k] + (lo - seg_first)
                dst = slot * ROW_BLOCK + lo - first
                _segment_copies(n, lambda done, size: pltpu.make_async_copy(
                    _row_span(hs_ref, src + done, size), _row_span(xbuf_ref, dst + done, size), sem.at[slot]), act)
                return carry
            return body

        jlo, jhi = jlo_ref[t], jhi_ref[t]
        lax.fori_loop(jnp.minimum(jlo, n_ctx_tiles), jnp.minimum(jhi, n_ctx_tiles), segment_of(hs_c_ref, 0), 0)
        lax.fori_loop(jnp.maximum(jlo, n_ctx_tiles), jnp.maximum(jhi, n_ctx_tiles),
                      segment_of(hs_l_ref, n_ctx_tiles), 0)

    @pl.when(i == 0)
    def _():
        xbuf_ref[...] = jnp.zeros(xbuf_ref.shape, F32)
        gather(0, _start)

    @pl.when(i + 1 < n_tiles)
    def _():
        gather(i + 1, _start)

    @pl.when((i == 0) | (te_ref[i] != te_ref[jnp.maximum(i - 1, 0)]))
    def _():
        w1b_ref[...] = w1_ref[0, 0].astype(BF16)
        w3b_ref[...] = w3_ref[0, 0].astype(BF16)
        w2b_ref[...] = w2_ref[0, 0].astype(BF16)

    n_rows = rows_ref[i]
    _segment_copies(n_rows, lambda done, size: pltpu.make_async_copy(
        _row_span(hs_c_ref, done, size), _row_span(xbuf_ref, (i % 2) * ROW_BLOCK + done, size), sem.at[i % 2]), _wait)

    @pl.when(n_rows > 0)
    def _():
        live = _iota((ROW_BLOCK, D_MODEL), 0) < n_rows
        x = jnp.where(live, _from_row_slabs(xbuf_ref, (i % 2) * ROW_BLOCK, ROW_BLOCK), 0.0).astype(BF16)
        hid = jax.nn.silu(_dot(x, w1b_ref[...])) * _dot(x, w3b_ref[...])
        _to_row_slabs(ys_ref, 0, _dot(hid.astype(BF16), w2b_ref[...]))

    @pl.when(n_rows == 0)
    def _():
        ys_ref[...] = jnp.zeros(ys_ref.shape, F32)


def _expert_call(l, plan, hs_c, hs_l, w1, w3, w2):
    tables = plan["expert_tables"]
    n_tiles = tables[0].shape[0]
    n_ctx_tiles = hs_c.shape[0] // (PAIR_BLOCK * ROW_SLABS)

    def weight(shape):
        return pl.BlockSpec((1, 1) + shape, lambda i, te, *_: (l, te[i], 0, 0))

    return pl.pallas_call(
        functools.partial(_expert_kernel, n_ctx_tiles),
        grid_spec=pltpu.PrefetchScalarGridSpec(
            num_scalar_prefetch=N_EXPERT_TABLES,
            grid=(n_tiles,),
            in_specs=[pl.BlockSpec(memory_space=pl.ANY), pl.BlockSpec(memory_space=pl.ANY),
                      weight((D_MODEL, D_EXPERT)), weight((D_MODEL, D_EXPERT)), weight((D_EXPERT, D_MODEL))],
            out_specs=pl.BlockSpec((ROW_BLOCK * ROW_SLABS, LANES), lambda i, *_: (i, 0)),
            scratch_shapes=[pltpu.VMEM((2 * ROW_BLOCK * ROW_SLABS, LANES), F32),
                            pltpu.VMEM((D_MODEL, D_EXPERT), BF16), pltpu.VMEM((D_MODEL, D_EXPERT), BF16),
                            pltpu.VMEM((D_EXPERT, D_MODEL), BF16), pltpu.SemaphoreType.DMA((2,))],
        ),
        out_shape=jax.ShapeDtypeStruct((n_tiles * ROW_BLOCK * ROW_SLABS, LANES), F32),
        compiler_params=pltpu.CompilerParams(
            dimension_semantics=("arbitrary",), vmem_limit_bytes=40 * 1024 * 1024),
        name="moe_experts",
    )(*tables, hs_c, hs_l, w1, w3, w2)


N_COMBINE_TABLES = 4


def _combine_kernel(n_ctx_tiles, cnt_ref, cpre_ref, lofs_ref, starts_ref, x_c_ref, x_l_ref, slot_c_ref, slot_l_ref,
                    wt_c_ref, wt_l_ref, mod_ref, ys_ref, xo_c_ref, xo_l_ref, buf_ref, sem):
    j = pl.program_id(0)
    n_tiles = pl.num_programs(0)

    def collect(t, act):
        slot = t % 2

        def body(e, carry):
            k = t * N_EXPERTS + e
            src, dst = starts_ref[e] + cpre_ref[k], slot * PAIR_BLOCK + lofs_ref[k]
            _segment_copies(cnt_ref[k], lambda done, size: pltpu.make_async_copy(
                _row_span(ys_ref, src + done, size), _row_span(buf_ref, dst + done, size), sem.at[slot]), act)
            return carry

        lax.fori_loop(0, N_EXPERTS, body, 0)

    @pl.when(j == 0)
    def _():
        collect(0, _start)

    @pl.when(j + 1 < n_tiles)
    def _():
        collect(j + 1, _start)

    for part in range(PAIR_BLOCK // ROW_BLOCK):
        pltpu.make_async_copy(
            _row_span(ys_ref, part * ROW_BLOCK, ROW_BLOCK),
            _row_span(buf_ref, (j % 2) * PAIR_BLOCK + part * ROW_BLOCK, ROW_BLOCK), sem.at[j % 2]).wait()
    rows = _from_row_slabs(buf_ref, (j % 2) * PAIR_BLOCK, PAIR_BLOCK)
    gate = mod_ref[0, 0, 5:6, :]

    def finish(x_ref, slot_ref, wt_ref, xo_ref):
        slots, wts = slot_ref[...].astype(F32), wt_ref[...]
        slot1, slot2 = slots[0:1], slots[1:2]
        weight_of_row = jnp.sum(_slot_one_hot(slot1, slot2, wts[0:1], wts[1:2]), axis=1, keepdims=True)
        hi, lo = _split2(rows * weight_of_row)
        perm = _slot_one_hot(slot1, slot2, 1.0, 1.0).astype(BF16)
        y = _dot_tn(perm, hi) + _dot_tn(perm, lo)
        xo_ref[...] = x_ref[...] + gate * y

    @pl.when(j < n_ctx_tiles)
    def _():
        finish(x_c_ref, slot_c_ref, wt_c_ref, xo_c_ref)

    @pl.when(j >= n_ctx_tiles)
    def _():
        finish(x_l_ref, slot_l_ref, wt_l_ref, xo_l_ref)


def _combine_call(l, plan, x_c, x_l, slot_c, slot_l, wt_c, wt_l, mods_all, mod_row_of_tile, ys):
    n_tiles = (x_c.shape[0] + x_l.shape[0]) // ROW_BLOCK
    n_ctx_tiles = x_c.shape[0] // ROW_BLOCK
    first, second = _two_streams(n_ctx_tiles)

    def lanes(index_map):
        return lambda i, *_: index_map(i)[::-1]

    return pl.pallas_call(
        functools.partial(_combine_kernel, n_ctx_tiles),
        grid_spec=pltpu.PrefetchScalarGridSpec(
            num_scalar_prefetch=N_COMBINE_TABLES,
            grid=(n_tiles,),
            in_specs=[pl.BlockSpec((ROW_BLOCK, D_MODEL), first),
                      pl.BlockSpec((ROW_BLOCK, D_MODEL), second),
                      pl.BlockSpec((2, ROW_BLOCK), lanes(first)),
                      pl.BlockSpec((2, ROW_BLOCK), lanes(second)),
                      pl.BlockSpec((2, ROW_BLOCK), lanes(first)),
                      pl.BlockSpec((2, ROW_BLOCK), lanes(second)),
                      pl.BlockSpec((1, 1, 6, D_MODEL), lambda i, *_: (l, mod_row_of_tile(i), 0, 0)),
                      pl.BlockSpec(memory_space=pl.ANY)],
            out_specs=[pl.BlockSpec((ROW_BLOCK, D_MODEL), first),
                       pl.BlockSpec((ROW_BLOCK, D_MODEL), second)],
            scratch_shapes=[pltpu.VMEM((2 * PAIR_BLOCK * ROW_SLABS, LANES), F32), pltpu.SemaphoreType.DMA((2,))],
        ),
        out_shape=[jax.ShapeDtypeStruct(x_c.shape, F32), jax.ShapeDtypeStruct(x_l.shape, F32)],
        compiler_params=pltpu.CompilerParams(
            dimension_semantics=("arbitrary",), vmem_limit_bytes=40 * 1024 * 1024),
        name="moe_combine",
    )(*plan["combine_tables"], x_c, x_l, slot_c, slot_l, wt_c, wt_l, mods_all, ys)


def _moe_plan(cnt):
    n_tok_tiles = cnt.shape[0]
    n_tiles = n_tok_tiles * PAIR_BLOCK // ROW_BLOCK + N_EXPERTS
    lofs = jnp.cumsum(cnt, axis=1) - cnt
    cpre = jnp.cumsum(cnt, axis=0) - cnt
    counts = jnp.sum(cnt, axis=0)
    padded = (counts + ROW_BLOCK - 1) // ROW_BLOCK * ROW_BLOCK
    ends = jnp.cumsum(padded)
    starts = ends - padded
    tile_start = jnp.arange(n_tiles, dtype=jnp.int32) * ROW_BLOCK
    tile_expert = jnp.minimum(
        jnp.sum((tile_start[:, None] >= ends[None, :]).astype(jnp.int32), axis=1), N_EXPERTS - 1)
    hot = tile_expert[:, None] == jnp.arange(N_EXPERTS, dtype=jnp.int32)[None, :]
    first = tile_start - jnp.sum(jnp.where(hot, starts[None, :], 0), axis=1)
    rows = jnp.clip(jnp.sum(jnp.where(hot, counts[None, :], 0), axis=1) - first, 0, ROW_BLOCK)
    seg_first = jnp.sum(jnp.where(hot[:, None, :], cpre[None, :, :], 0), axis=2)
    seg_rows = jnp.sum(jnp.where(hot[:, None, :], cnt[None, :, :], 0), axis=2)
    overlap = (seg_first < (first + rows)[:, None]) & (seg_first + seg_rows > first[:, None])
    j = jnp.arange(n_tok_tiles, dtype=jnp.int32)[None, :]
    jlo = jnp.min(jnp.where(overlap, j, n_tok_tiles), axis=1)
    jhi = jnp.max(jnp.where(overlap, j + 1, 0), axis=1)
    i32 = lambda a: a.astype(jnp.int32).reshape(-1)
    return {
        "expert_tables": tuple(i32(a) for a in (tile_expert, first, rows, jlo, jhi, cpre, cnt, lofs)),
        "combine_tables": tuple(i32(a) for a in (cnt, cpre, lofs, starts)),
    }


def _rope_tables(n_tok):
    n_rows = n_tok // GRID_W
    pos_r = jnp.repeat(jnp.arange(n_rows), GRID_W)
    pos_c = jnp.tile(jnp.arange(GRID_W), n_rows)
    half = DIFF_QK // 2
    nf = half // 2
    freqs = ROPE_BASE ** (-jnp.arange(nf, dtype=F32) / nf)

    def tables(pos):
        ang = pos.astype(F32)[:, None] * freqs
        cos, sin = jnp.cos(ang), jnp.sin(ang)
        return jnp.concatenate([cos, cos], axis=-1), jnp.concatenate([-sin, sin], axis=-1)

    cos_r, sin_r = tables(pos_r)
    cos_c, sin_c = tables(pos_c)
    cos = jnp.concatenate([cos_r, cos_c], axis=-1)
    sin = jnp.concatenate([sin_r, sin_c], axis=-1)
    return jnp.concatenate([cos, cos], axis=-1), jnp.concatenate([sin, sin], axis=-1)


def _mixer_weights(w_in, w_out, sgu_w, sgu_b, q_norm_g, k_norm_g, diff_lambda, diff_norm_g, gla_w2, gla_b,
                   gla_norm_g, norm1_g, norm2_g, router_w, router_bias):
    w_in_pad = jnp.pad(w_in, ((0, 0), (0, 0), (0, D_PROJ_PAD - w_in.shape[2]))).astype(BF16)
    w2cat = jnp.zeros((DEPTH, LANES, 2 * W_GLA), F32)
    w2cat = w2cat.at[:, 0:GLA_RANK, 0:W_GLA].set(gla_w2[:, 0]).at[:, GLA_RANK:2 * GLA_RANK, W_GLA:].set(gla_w2[:, 1])
    return (
        norm1_g[:, None, :], norm2_g[:, None, :], w_in_pad, w_out.astype(BF16),
        sgu_w.astype(BF16), jnp.repeat(sgu_b.transpose(0, 2, 1), SGU_GROUP_W, axis=2),
        jnp.tile(q_norm_g, (1, W_QK // DIFF_QK))[:, None, :], jnp.tile(k_norm_g, (1, W_QK // DIFF_QK))[:, None, :],
        diff_lambda, diff_norm_g[:, None, :],
        w2cat.astype(BF16), gla_b.reshape(DEPTH, 1, 2 * W_GLA), jnp.tile(gla_norm_g, (1, GLA_HEADS))[:, None, :],
        router_w.T, router_bias[:, None],
    )


def kernel(x_prompt, x_sample, cache_k, cache_v, state_gla, c, c_ctx, w_in, w_out, sgu_w, sgu_b, q_norm_g, k_norm_g,
           diff_lambda, diff_norm_g, gla_w2, gla_b, gla_norm_g, norm1_g, norm2_g, ada_w, ada_b, router_w, router_bias,
           moe_w1, moe_w3, moe_w2):
    n_ctx_seq, ctx_len, _ = x_prompt.shape
    n_lat_seq, lat_len, _ = x_sample.shape
    n_ctx_tok = n_ctx_seq * ctx_len
    n_lat_tok = n_lat_seq * lat_len
    ctx_tiles = n_ctx_tok // ROW_BLOCK
    lat_tiles_per_seq = lat_len // ROW_BLOCK

    cond = jnp.zeros((SUBLANES, D_MODEL), F32).at[0].set(c_ctx).at[1:1 + n_lat_seq].set(c)
    mods_all = _adaln_call(cond, ada_w, ada_b)[:, :1 + n_lat_seq].reshape(DEPTH, 1 + n_lat_seq, 6, D_MODEL)
    weights = _mixer_weights(w_in, w_out, sgu_w, sgu_b, q_norm_g, k_norm_g, diff_lambda, diff_norm_g, gla_w2, gla_b,
                             gla_norm_g, norm1_g, norm2_g, router_w, router_bias)

    ck_all = cache_k.transpose(0, 1, 2, 4, 3, 5).reshape(cache_k.shape[:3] + (cache_k.shape[4], DIFF_V))
    st_all = jnp.einsum('bldhkv,hg->bldhvgk', state_gla, jnp.eye(GLA_HEADS, dtype=F32)).reshape(
        n_lat_seq, DEPTH, 2, W_GLA, W_GLA)
    cos, sin = _rope_tables(lat_len)
    extras = (ck_all, cache_v, st_all, cos, sin)

    def mod_row_of_tile(i):
        return jnp.where(i < ctx_tiles, 0, 1 + (i - ctx_tiles) // lat_tiles_per_seq)

    x_c = x_prompt.reshape(n_ctx_tok, D_MODEL)
    x_l = x_sample.reshape(n_lat_tok, D_MODEL)
    cache_bufs = ()
    for l in range(DEPTH):
        ctx_par = 1 if l == 0 else CTX_SEQS_PER_STEP
        x1_c, hs_c, slot_c, wt_c, cnt_c, *cache_bufs = _mixer_call(
            l, ctx_len, ctx_par, False, x_c, mods_all, weights, None, tuple(cache_bufs))
        x1_l, hs_l, slot_l, wt_l, cnt_l = _mixer_call(l, lat_len, 1, True, x_l, mods_all, weights, extras, ())
        plan = _moe_plan(jnp.concatenate([cnt_c[:, :, 0], cnt_l[:, :, 0]], axis=0))
        ys = _expert_call(l, plan, hs_c, hs_l, moe_w1, moe_w3, moe_w2)
        x_c, x_l = _combine_call(l, plan, x1_c, x1_l, slot_c, slot_l, wt_c, wt_l, mods_all, mod_row_of_tile, ys)

    new_k, new_v, new_s = cache_bufs
    return (x_c.reshape(x_prompt.shape), x_l.reshape(x_sample.shape), new_k, new_v, new_s)
```

```python
import functools
import math

import jax
import jax.numpy as jnp
from jax import lax
from jax.experimental import pallas as pl
from jax.experimental.pallas import tpu as pltpu

F32 = jnp.float32
BF16 = jnp.bfloat16

D_MODEL = 1024
DEPTH = 4
GRID_W = 64
SGU_GROUPS = 4
SGU_GROUP_W = 64
SGU_W = SGU_GROUPS * SGU_GROUP_W
SGU_CHUNK = 128
DIFF_HEADS = 4
DIFF_QK = 64
DIFF_V = 2 * DIFF_QK
ROPE_BASE = 10000.0
GLA_HEADS = 4
GLA_DK = 64
GLA_DV = 64
GLA_RANK = 16
GLA_GATE_NORM = 16.0
GLA_CHUNK = 64
N_EXPERTS = 16
N_GROUPS = 4
EXPERTS_PER_GROUP = N_EXPERTS // N_GROUPS
D_EXPERT = 512
EPS = 1e-6

LANES = 128
SUBLANES = 8
MXU_DIM = 256

ROW_BLOCK = MXU_DIM
ROW_SLABS = D_MODEL // LANES

C_AU, C_AV, C_BQ, C_BK, C_BV = 0, 256, 512, 1024, 1536
C_CQ, C_CK, C_CV, C_CR, C_LR = 2048, 2304, 2560, 2816, 3072
D_PROJ_MAIN = 3072
D_PROJ_PAD = D_PROJ_MAIN + LANES
W_QK = DIFF_HEADS * 2 * DIFF_QK
W_GLA = GLA_HEADS * GLA_DK
M_A, M_B, M_C = 0, SGU_W, SGU_W + DIFF_HEADS * DIFF_V


def _split2(x):
    hi = x.astype(BF16)
    lo = (x - hi.astype(F32)).astype(BF16)
    return hi, lo


def _split3(x):
    hi = x.astype(BF16)
    r = x - hi.astype(F32)
    mid = r.astype(BF16)
    lo = (r - mid.astype(F32)).astype(BF16)
    return hi, mid, lo


def _dot(a, b):
    return jnp.dot(a, b, preferred_element_type=F32)


def _dot_nt(a, b):
    return lax.dot_general(a, b, (((1,), (1,)), ((), ())), preferred_element_type=F32)


def _dot_tn(a, b):
    return lax.dot_general(a, b, (((0,), (0,)), ((), ())), preferred_element_type=F32)


def _iota(shape, dim):
    return lax.broadcasted_iota(jnp.int32, shape, dim)


def _block_ones(width, block):
    r = _iota((width, width), 0) // block
    c = _iota((width, width), 1) // block
    return (r == c)


def _group_sum(z, block):
    width = z.shape[-1]
    outs = []
    for s in range(0, width, MXU_DIM):
        w = min(MXU_DIM, width - s)
        ones = _block_ones(w, block).astype(BF16)
        hi, lo = _split2(z[:, s:s + w])
        outs.append(_dot(hi, ones) + _dot(lo, ones))
    return outs[0] if len(outs) == 1 else jnp.concatenate(outs, axis=-1)


def _group_rms(z, block):
    ms = _group_sum(z * z, block) * (1.0 / block)
    return z * lax.rsqrt(ms + EPS)


def _row_rms(z):
    return z * lax.rsqrt(jnp.mean(z * z, axis=-1, keepdims=True) + EPS)


def _log_sigmoid(x):
    return jnp.minimum(x, 0.0) - jnp.log1p(jnp.exp(-jnp.abs(x)))


ADA_COLS = 1536


def _adaln_kernel(n_cond, cond_t_ref, w_ref, b_ref, o_ref):
    sc = jax.nn.silu(cond_t_ref[...])
    w = w_ref[0]
    rows = [jnp.sum(sc[:, r:r + 1] * w, axis=0, keepdims=True) + b_ref[0] for r in range(n_cond)]
    o_ref[0] = jnp.concatenate(rows + [jnp.zeros((SUBLANES - n_cond, w.shape[1]), F32)], axis=0)


def _adaln_call(cond_t, n_cond, ada_w, ada_b):
    n_col = 6 * D_MODEL // ADA_COLS
    return pl.pallas_call(
        functools.partial(_adaln_kernel, n_cond),
        grid=(DEPTH, n_col),
        in_specs=[
            pl.BlockSpec((D_MODEL, SUBLANES), lambda l, j: (0, 0)),
            pl.BlockSpec((1, D_MODEL, ADA_COLS), lambda l, j: (l, 0, j)),
            pl.BlockSpec((1, 1, ADA_COLS), lambda l, j: (l, 0, j)),
        ],
        out_specs=pl.BlockSpec((1, SUBLANES, ADA_COLS), lambda l, j: (l, 0, j)),
        out_shape=jax.ShapeDtypeStruct((DEPTH, SUBLANES, 6 * D_MODEL), F32),
        compiler_params=pltpu.CompilerParams(
            dimension_semantics=("arbitrary", "arbitrary"), vmem_limit_bytes=40 * 1024 * 1024),
        name="adaln",
    )(cond_t, ada_w, ada_b.reshape(DEPTH, 1, 6 * D_MODEL))


def _route(hn, rwt_ref, rb_ref):
    h_hi, h_lo = _split2(hn)
    rw = rwt_ref[...]
    rw_hi = rw.astype(BF16)
    rw_lo = (rw - rw_hi.astype(F32)).astype(BF16)
    logits = _dot_nt(rw_hi, h_hi) + _dot_nt(rw_hi, h_lo) + _dot_nt(rw_lo, h_hi)
    aff = jax.nn.sigmoid(logits)
    sel = aff + rb_ref[...]
    n_tok = sel.shape[1]

    def top2_sum(a, b, c, d):
        hi1, lo1 = jnp.maximum(a, b), jnp.minimum(a, b)
        hi2, lo2 = jnp.maximum(c, d), jnp.minimum(c, d)
        return jnp.maximum(hi1, hi2) + jnp.maximum(jnp.minimum(hi1, hi2), jnp.maximum(lo1, lo2))

    scores = []
    for g in range(N_GROUPS):
        rows = [sel[EXPERTS_PER_GROUP * g + j:EXPERTS_PER_GROUP * g + j + 1, :] for j in range(EXPERTS_PER_GROUP)]
        scores.append(top2_sum(*rows))
    best = jnp.zeros((1, n_tok), jnp.int32)
    best_score = scores[0]
    for g in range(1, N_GROUPS):
        upd = scores[g] > best_score
        best = jnp.where(upd, g, best)
        best_score = jnp.where(upd, scores[g], best_score)

    eid_i = _iota((N_EXPERTS, n_tok), 0)
    eid = eid_i.astype(F32)
    neg = jnp.float32(-jnp.inf)
    msel = jnp.where(eid_i // EXPERTS_PER_GROUP == best, sel, neg)
    m1 = jnp.max(msel, axis=0, keepdims=True)
    idx1 = jnp.min(jnp.where(msel == m1, eid, float(N_EXPERTS)), axis=0, keepdims=True)
    msel2 = jnp.where(eid == idx1, neg, msel)
    m2 = jnp.max(msel2, axis=0, keepdims=True)
    idx2 = jnp.min(jnp.where(msel2 == m2, eid, float(N_EXPERTS)), axis=0, keepdims=True)
    w1 = jnp.sum(jnp.where(eid == idx1, aff, 0.0), axis=0, keepdims=True)
    w2 = jnp.sum(jnp.where(eid == idx2, aff, 0.0), axis=0, keepdims=True)
    wsum = w1 + w2
    return idx1.astype(jnp.int32), idx2.astype(jnp.int32), w1 / wsum, w2 / wsum


def _local_slots(idx1, idx2):
    n_tok = idx1.shape[1]
    eid = _iota((N_EXPERTS, n_tok), 0)
    hot1, hot2 = eid == idx1, eid == idx2
    hot = jnp.where(hot1, 1.0, jnp.where(hot2, 1.0, 0.0))
    earlier = jnp.where(_iota((n_tok, n_tok), 0) < _iota((n_tok, n_tok), 1), 1.0, 0.0).astype(BF16)
    before_in_expert = _dot(hot.astype(BF16), earlier)
    counts = jnp.sum(hot, axis=1, keepdims=True)
    lower = jnp.where(_iota((N_EXPERTS, N_EXPERTS), 1) < _iota((N_EXPERTS, N_EXPERTS), 0), 1.0, 0.0).astype(BF16)
    first_slot = _dot(lower, jnp.broadcast_to(counts, (N_EXPERTS, LANES)).astype(BF16))[:, 0:1]
    slot = before_in_expert + first_slot
    slot1 = jnp.sum(jnp.where(hot1, slot, 0.0), axis=0, keepdims=True)
    slot2 = jnp.sum(jnp.where(hot2, slot, 0.0), axis=0, keepdims=True)
    return slot1, slot2, counts


def _slot_one_hot(slot1, slot2, v1, v2):
    n_tok = slot1.shape[1]
    row = _iota((2 * n_tok, n_tok), 0).astype(F32)
    return jnp.where(row == slot1, v1, jnp.where(row == slot2, v2, 0.0))


N_MIXER_WEIGHTS = 15
CTX_SEQS_PER_STEP = 2
MAX_INLINE_BLOCKS = 2


def _mixer_kernel(n_tok, n_par, latent, n_alias, lam_init, *refs):
    it = iter(refs)
    x_ref, mod_ref = next(it), next(it)
    (n1_ref, n2_ref, win_ref, wout_ref, sw_ref, sb_ref, qg_ref, kg_ref, dl_ref, dg_ref,
     w2c_ref, gb_ref, gg_ref, rwt_ref, rb_ref) = (next(it) for _ in range(N_MIXER_WEIGHTS))
    if latent:
        ck_ref, cv_ref, st0_ref, cos_ref, sin_ref = (next(it) for _ in range(5))
    for _ in range(n_alias):
        next(it)
    xo_ref, hs_ref, slot_ref, wt_ref, cnt_ref = (next(it) for _ in range(5))
    if not latent:
        ko_ref, vo_ref, so_ref = (next(it) for _ in range(3))
    proj_ref, mix_ref, q_ref, k_ref, v_ref = (next(it) for _ in range(5))
    gq_ref, gke_ref, gv_ref, gr_ref, dec_ref, go_ref, st_ref = (next(it) for _ in range(7))

    n_blk = n_tok // ROW_BLOCK
    n_ctx = k_ref.shape[0] - n_par * n_tok
    n_keys = n_ctx + n_tok
    mod = mod_ref[0, 0]

    def blocks(body):
        if n_par * n_blk <= MAX_INLINE_BLOCKS:
            for r in range(n_par * n_blk):
                body(r)
        else:
            def step(r, carry):
                body(r)
                return carry
            lax.fori_loop(0, n_par * n_blk, step, 0)

    def aligned(start, size):
        return pl.ds(start if isinstance(start, int) else pl.multiple_of(start, size), size)

    def block_rows(r, offset=0):
        return aligned(offset + r * ROW_BLOCK, ROW_BLOCK)

    if not latent:
        for ref in (ko_ref, vo_ref, so_ref):
            for q in range(n_par):
                for other in range(1, ref.shape[1]):
                    ref[q, other] = jnp.zeros(ref.shape[2:], F32)

    lane_group = _iota((SGU_CHUNK, SGU_W), 1) // SGU_GROUP_W
    blk_r = _iota((ROW_BLOCK, ROW_BLOCK), 0)
    blk_c = _iota((ROW_BLOCK, ROW_BLOCK), 1)
    same_chunk = (blk_r // GLA_CHUNK) == (blk_c // GLA_CHUNK)
    tri = (jnp.where(same_chunk & (blk_c <= blk_r), 1.0, 0.0).astype(BF16),
           jnp.where(same_chunk & (blk_c >= blk_r), 1.0, 0.0).astype(BF16))
    chunks_per_blk = ROW_BLOCK // GLA_CHUNK
    head_of_lane = _iota((GLA_CHUNK, W_GLA), 1) // GLA_DK
    stack_r = _iota((GLA_HEADS * GLA_CHUNK, GLA_CHUNK), 0) % GLA_CHUNK
    stack_c = _iota((GLA_HEADS * GLA_CHUNK, GLA_CHUNK), 1)
    causal = (stack_c <= stack_r, stack_c >= stack_r)

    if latent:
        for h in range(DIFF_HEADS):
            k_ref[0:n_ctx, h * DIFF_V:(h + 1) * DIFF_V] = ck_ref[0, 0, h].astype(BF16)
            v_ref[0:n_ctx, h * DIFF_V:(h + 1) * DIFF_V] = cv_ref[0, 0, h].astype(BF16)
        pair_lo = (_iota((ROW_BLOCK, W_QK), 1) % (DIFF_QK // 2)) < (DIFF_QK // 4)

        def rope(z, rows):
            cos = jnp.concatenate([cos_ref[rows, :]] * DIFF_HEADS, axis=-1)
            sin = jnp.concatenate([sin_ref[rows, :]] * DIFF_HEADS, axis=-1)
            shift = DIFF_QK // 4
            swapped = jnp.where(pair_lo, pltpu.roll(z, W_QK - shift, 1), pltpu.roll(z, shift, 1))
            return z * cos + swapped * sin

    def modulated_input(r):
        h = _row_rms(x_ref[block_rows(r), :]) * n1_ref[0]
        return (h * (1.0 + mod[1:2, :]) + mod[0:1, :]).astype(BF16)

    def spatial_gating(r):
        for c in range(ROW_BLOCK // SGU_CHUNK):
            local = slice(c * SGU_CHUNK, (c + 1) * SGU_CHUNK)
            u = jax.nn.gelu(proj_ref[local, C_AU:C_AU + SGU_W])
            v = _group_rms(jax.nn.gelu(proj_ref[local, C_AV:C_AV + SGU_W]), SGU_GROUP_W).astype(BF16)
            s = sb_ref[0]
            for g in range(SGU_GROUPS):
                s = s + jnp.where(lane_group == g, _dot(sw_ref[0, g], v), 0.0)
            mix_ref[aligned(r * ROW_BLOCK + c * SGU_CHUNK, SGU_CHUNK), M_A:M_A + SGU_W] = (u * s).astype(BF16)

    def attention_operands(r):
        rows = block_rows(r)
        key_rows = block_rows(r, n_ctx)
        seq, seq_rows = r // n_blk, block_rows(r % n_blk)
        qn = _group_rms(proj_ref[:, C_BQ:C_BQ + W_QK], DIFF_QK) * qg_ref[0]
        kn = _group_rms(proj_ref[:, C_BK:C_BK + W_QK], DIFF_QK) * kg_ref[0]
        vv = proj_ref[:, C_BV:C_BV + W_QK]
        if latent:
            qn, kn = rope(qn, rows), rope(kn, rows)
        else:
            for h in range(DIFF_HEADS):
                for i in range(2):
                    lo = h * DIFF_V + i * DIFF_QK
                    ko_ref[seq, 0, h, i, seq_rows, :] = kn[:, lo:lo + DIFF_QK]
                vo_ref[seq, 0, h, seq_rows, :] = vv[:, h * DIFF_V:(h + 1) * DIFF_V]
        q_ref[rows, :] = (qn * (DIFF_QK ** -0.5)).astype(BF16)
        k_ref[key_rows, :] = kn.astype(BF16)
        v_ref[key_rows, :] = vv.astype(BF16)

    def gla_operands(r):
        rows = block_rows(r)
        gpre = _dot(proj_ref[:, C_LR:C_LR + LANES].astype(BF16), w2c_ref[0]) + gb_ref[0]
        gate = _log_sigmoid(gpre) * (1.0 / GLA_GATE_NORM)
        gq = proj_ref[:, C_CQ:C_CQ + W_GLA] * (GLA_DK ** -0.5)
        gk = proj_ref[:, C_CK:C_CK + W_GLA]
        gv = proj_ref[:, C_CV:C_CV + W_GLA].astype(BF16)
        gv_ref[rows, :] = gv
        gr_ref[rows, :] = proj_ref[:, C_CR:C_CR + W_GLA]
        for d in range(2):
            g = gate[:, d * W_GLA:(d + 1) * W_GLA]
            b = sum(_dot(tri[d], p) for p in _split3(g))
            last = GLA_CHUNK - 1 if d == 0 else 0
            b_last = jnp.concatenate(
                [jnp.broadcast_to(b[c * GLA_CHUNK + last:c * GLA_CHUNK + last + 1, :], (GLA_CHUNK, W_GLA))
                 for c in range(chunks_per_blk)], axis=0)
            q_dec = (gq * jnp.exp(b)).astype(BF16)
            k_inv = (gk * jnp.exp(-b)).astype(BF16)
            gq_ref[d, rows, :] = q_dec
            gke_ref[d, rows, :] = (gk * jnp.exp(b_last - b)).astype(BF16)
            for c in range(chunks_per_blk):
                row = c * GLA_CHUNK + last
                dec_ref[d, r * chunks_per_blk + c] = jnp.exp(b[row:row + 1, :])
                chunk = slice(c * GLA_CHUNK, (c + 1) * GLA_CHUNK)
                qd = q_dec[chunk]
                q_stack = jnp.concatenate(
                    [jnp.where(head_of_lane == h, qd, jnp.zeros_like(qd)) for h in range(GLA_HEADS)], axis=0)
                attn = jnp.where(causal[d], _dot_nt(q_stack, k_inv[chunk]), 0.0)
                spread = _dot(attn.astype(BF16), gv[chunk])
                o = jnp.zeros((GLA_CHUNK, W_GLA), F32)
                for h in range(GLA_HEADS):
                    o = o + jnp.where(head_of_lane == h, spread[h * GLA_CHUNK:(h + 1) * GLA_CHUNK, :], 0.0)
                go_ref[d, aligned(r * ROW_BLOCK + c * GLA_CHUNK, GLA_CHUNK), :] = o

    def project_and_split(r):
        proj_ref[...] = _dot(modulated_input(r), win_ref[0])
        spatial_gating(r)
        attention_operands(r)
        gla_operands(r)

    blocks(project_and_split)

    dl = dl_ref[0]
    lam = (jnp.exp(jnp.sum(dl[0:1] * dl[1:2], axis=-1, keepdims=True))
           - jnp.exp(jnp.sum(dl[2:3] * dl[3:4], axis=-1, keepdims=True)) + lam_init)
    sub0 = (_iota((ROW_BLOCK, DIFF_V), 1) < DIFF_QK)

    def softmax(s):
        e = jnp.exp(s - jnp.max(s, axis=-1, keepdims=True))
        return e, jnp.sum(e, axis=-1, keepdims=True)

    def attn_block(r):
        rows = block_rows(r)
        keys = aligned((r // n_blk) * n_keys, n_keys)
        for h in range(DIFF_HEADS):
            cols = slice(h * DIFF_V, (h + 1) * DIFF_V)
            qh = q_ref[rows, cols]
            kh = k_ref[keys, cols]
            e0, z0 = softmax(_dot_nt(jnp.where(sub0, qh, jnp.zeros_like(qh)), kh))
            e1, z1 = softmax(_dot_nt(jnp.where(sub0, jnp.zeros_like(qh), qh), kh))
            w = e0 / z0 - lam * (e1 / z1)
            o = _dot(w.astype(BF16), v_ref[keys, cols])
            o = _row_rms(o) * dg_ref[0] * (1.0 - lam_init)
            mix_ref[rows, M_B + h * DIFF_V:M_B + (h + 1) * DIFF_V] = o.astype(BF16)

    blocks(attn_block)

    if latent:
        st_ref[0] = st0_ref[0, 0]
    else:
        st_ref[...] = jnp.zeros(st_ref.shape, F32)

    n_chunk = n_tok // GLA_CHUNK
    st_diag = (_iota((W_GLA, W_GLA), 0) // GLA_DV) == (_iota((W_GLA, W_GLA), 1) // GLA_DK)

    def gla_step(c, carry):
        for seq in range(n_par):
            for d in range(2):
                cc = seq * n_chunk + (c if d == 0 else n_chunk - 1 - c)
                rows = pl.ds(pl.multiple_of(cc * GLA_CHUNK, GLA_CHUNK), GLA_CHUNK)
                st = st_ref[seq, d]
                go_ref[d, rows, :] = go_ref[d, rows, :] + _dot_nt(gq_ref[d, rows, :], st.astype(BF16))
                upd = _dot_tn(gv_ref[rows, :], gke_ref[d, rows, :])
                st_ref[seq, d] = dec_ref[d, cc] * st + jnp.where(st_diag, upd, 0.0)
        return carry

    lax.fori_loop(0, n_chunk, gla_step, 0)

    if not latent:
        for seq in range(n_par):
            for d in range(2):
                s_full = st_ref[seq, d].T
                for h in range(GLA_HEADS):
                    so_ref[seq, 0, d, h] = s_full[h * GLA_DK:(h + 1) * GLA_DK, h * GLA_DV:(h + 1) * GLA_DV]

    for r in range(n_par * n_blk):
        rows = pl.ds(r * ROW_BLOCK, ROW_BLOCK)
        oc = _group_rms(go_ref[0, rows, :] + go_ref[1, rows, :], GLA_DV) * gg_ref[0]
        oc = oc * jax.nn.silu(gr_ref[rows, :])
        mix_ref[rows, M_C:M_C + W_GLA] = oc.astype(BF16)
        x1 = x_ref[rows, :] + mod[2:3, :] * _dot(mix_ref[rows, :], wout_ref[0])
        xo_ref[rows, :] = x1
        hn = _row_rms(x1) * n2_ref[0]
        hn = hn * (1.0 + mod[4:5, :]) + mod[3:4, :]
        idx1, idx2, w1, w2 = _route(hn, rwt_ref, rb_ref)
        slot1, slot2, counts = _local_slots(idx1, idx2)
        perm = _slot_one_hot(slot1, slot2, 1.0, 1.0).astype(BF16)
        _to_row_slabs(hs_ref, 2 * r * ROW_BLOCK, _dot(perm, hn.astype(BF16)))
        slot_ref[:, rows] = jnp.concatenate([slot1, slot2], axis=0).astype(jnp.int32)
        wt_ref[:, rows] = jnp.concatenate([w1, w2], axis=0)
        cnt_ref[r] = jnp.broadcast_to(counts, (N_EXPERTS, LANES)).astype(jnp.int32)


def _mixer_call(l, n_tok, n_par, latent, x, mods_all, weights, extras, cache_bufs):
    n_seq = x.shape[0] // n_tok
    n_all = x.shape[0]
    assert n_seq % n_par == 0 and not (latent and n_par > 1)
    n_step_tok = n_par * n_tok
    n_keys = n_step_tok + (extras[0].shape[3] if latent else 0)
    n_chunk = n_step_tok // GLA_CHUNK
    lam_init = 0.8 - 0.6 * math.exp(-0.3 * l)

    single = pl.Buffered(1)
    seq_mode = single if latent else None

    def layer(arr):
        tail = arr.shape[1:]
        return pl.BlockSpec((1,) + tail, lambda s, _n=len(tail): (l,) + (0,) * _n, pipeline_mode=single)

    def const(arr):
        return pl.BlockSpec(arr.shape, lambda s, _n=arr.ndim: (0,) * _n, pipeline_mode=single)

    def tok_spec(width):
        return pl.BlockSpec((n_step_tok, width), lambda s: (s, 0), pipeline_mode=seq_mode)

    mod_row = (lambda s: 1 + s) if latent else (lambda s: 0)
    in_specs = [tok_spec(D_MODEL),
                pl.BlockSpec((1, 1, 6, D_MODEL), lambda s: (l, mod_row(s), 0, 0))]
    in_specs += [layer(w) for w in weights[:N_MIXER_WEIGHTS - 2]] + [const(w) for w in weights[-2:]]
    operands = [x, mods_all] + list(weights)
    if latent:
        ck, cv, st0, cos, sin = extras
        in_specs += [
            pl.BlockSpec((1, 1) + ck.shape[2:], lambda s: (s, l, 0, 0, 0)),
            pl.BlockSpec((1, 1) + cv.shape[2:], lambda s: (s, l, 0, 0, 0)),
            pl.BlockSpec((1, 1) + st0.shape[2:], lambda s: (s, l, 0, 0, 0)),
            const(cos), const(sin),
        ]
        operands += [ck, cv, st0, cos, sin]
    n_in = len(operands)
    in_specs += [pl.BlockSpec(memory_space=pl.ANY)] * len(cache_bufs)
    operands += list(cache_bufs)

    tiles_per_step = n_step_tok // ROW_BLOCK
    out_shape = [
        jax.ShapeDtypeStruct((n_all, D_MODEL), F32),
        jax.ShapeDtypeStruct((2 * n_all * ROW_SLABS, LANES), F32),
        jax.ShapeDtypeStruct((2, n_all), jnp.int32),
        jax.ShapeDtypeStruct((2, n_all), F32),
        jax.ShapeDtypeStruct((n_all // ROW_BLOCK, N_EXPERTS, LANES), jnp.int32),
    ]
    out_specs = [
        tok_spec(D_MODEL),
        pl.BlockSpec((2 * n_step_tok * ROW_SLABS, LANES), lambda s: (s, 0), pipeline_mode=seq_mode),
        pl.BlockSpec((2, n_step_tok), lambda s: (0, s)),
        pl.BlockSpec((2, n_step_tok), lambda s: (0, s)),
        pl.BlockSpec((tiles_per_step, N_EXPERTS, LANES), lambda s: (s, 0, 0)),
    ]
    n_shared_out = len(out_shape)
    aliases = {}
    if not latent:
        out_shape += [
            jax.ShapeDtypeStruct((n_seq, DEPTH, DIFF_HEADS, 2, n_tok, DIFF_QK), F32),
            jax.ShapeDtypeStruct((n_seq, DEPTH, DIFF_HEADS, n_tok, DIFF_V), F32),
            jax.ShapeDtypeStruct((n_seq, DEPTH, 2, GLA_HEADS, GLA_DK, GLA_DV), F32),
        ]
        n_lay, at = (1, l) if cache_bufs else (DEPTH, 0)
        out_specs += [
            pl.BlockSpec((n_par, n_lay, DIFF_HEADS, 2, n_tok, DIFF_QK), lambda s: (s, at, 0, 0, 0, 0)),
            pl.BlockSpec((n_par, n_lay, DIFF_HEADS, n_tok, DIFF_V), lambda s: (s, at, 0, 0, 0)),
            pl.BlockSpec((n_par, n_lay, 2, GLA_HEADS, GLA_DK, GLA_DV), lambda s: (s, at, 0, 0, 0, 0)),
        ]
        aliases = {n_in + j: n_shared_out + j for j in range(len(cache_bufs))}
    scratch = [
        pltpu.VMEM((ROW_BLOCK, D_PROJ_PAD), F32),
        pltpu.VMEM((n_step_tok, D_MODEL), BF16),
        pltpu.VMEM((n_step_tok, W_QK), BF16),
        pltpu.VMEM((n_keys, W_QK), BF16),
        pltpu.VMEM((n_keys, W_QK), BF16),
        pltpu.VMEM((2, n_step_tok, W_GLA), BF16),
        pltpu.VMEM((2, n_step_tok, W_GLA), BF16),
        pltpu.VMEM((n_step_tok, W_GLA), BF16),
        pltpu.VMEM((n_step_tok, W_GLA), F32),
        pltpu.VMEM((2, n_chunk, 1, W_GLA), F32),
        pltpu.VMEM((2, n_step_tok, W_GLA), F32),
        pltpu.VMEM((n_par, 2, W_GLA, W_GLA), F32),
    ]
    return pl.pallas_call(
        functools.partial(_mixer_kernel, n_tok, n_par, latent, len(cache_bufs), lam_init),
        grid=(n_seq // n_par,),
        in_specs=in_specs,
        out_specs=out_specs,
        out_shape=out_shape,
        scratch_shapes=scratch,
        input_output_aliases=aliases,
        compiler_params=pltpu.CompilerParams(
            dimension_semantics=("arbitrary",), vmem_limit_bytes=56 * 1024 * 1024),
        name="mixer_latent" if latent else "mixer_context",
    )(*operands)


PAIR_BLOCK = 2 * ROW_BLOCK
COPY_SIZES = tuple(ROW_BLOCK >> k for k in range(ROW_BLOCK.bit_length()))
LARGE_COPY = 64


def _segment_copies(n_rows, make_copy, act):
    def copy_if_set(size):
        @pl.when((n_rows & size) != 0)
        def _():
            act(make_copy(n_rows & (-2 * size), size))

    n_large = COPY_SIZES.index(LARGE_COPY) + 1

    @pl.when(n_rows >= LARGE_COPY)
    def _():
        for size in COPY_SIZES[:n_large]:
            copy_if_set(size)

    for size in COPY_SIZES[n_large:]:
        copy_if_set(size)


def _start(copy):
    copy.start()


def _wait(copy):
    copy.wait()


def _slab_rows(first_row, n_rows, slab):
    return pl.ds(first_row * ROW_SLABS + slab, n_rows, stride=ROW_SLABS)


def _to_row_slabs(ref, first_row, value):
    for s in range(ROW_SLABS):
        ref[_slab_rows(first_row, value.shape[0], s), :] = value[:, s * LANES:(s + 1) * LANES]


def _from_row_slabs(ref, first_row, n_rows):
    return jnp.concatenate([ref[_slab_rows(first_row, n_rows, s), :] for s in range(ROW_SLABS)], axis=-1)


def _row_span(ref, first_row, n_rows):
    return ref.at[pl.ds(pl.multiple_of(first_row * ROW_SLABS, ROW_SLABS), n_rows * ROW_SLABS)]


def _two_streams(n_first_tiles):
    def first(i, *_):
        return (jnp.minimum(i, n_first_tiles - 1), 0)

    def second(i, *_):
        return (jnp.maximum(i - n_first_tiles, 0), 0)

    return first, second


N_EXPERT_TABLES = 8


def _expert_kernel(n_ctx_tiles, te_ref, first_ref, rows_ref, jlo_ref, jhi_ref, cpre_ref, cnt_ref, lofs_ref,
                   hs_c_ref, hs_l_ref, w1_ref, w3_ref, w2_ref, ys_ref, xbuf_ref, w1b_ref, w3b_ref, w2b_ref, sem):
    i = pl.program_id(0)
    n_tiles = pl.num_programs(0)

    def gather(t, act):
        slot = t % 2
        e, first = te_ref[t], first_ref[t]
        last = first + rows_ref[t]

        def segment_of(hs_ref, first_tile):
            def body(j, carry):
                k = j * N_EXPERTS + e
                seg_first = cpre_ref[k]
                lo = jnp.maximum(seg_first, first)
                n = jnp.maximum(jnp.minimum(seg_first + cnt_ref[k], last) - lo, 0)
                src = (j - first_tile) * PAIR_BLOCK + lofs_ref[k] + (lo - seg_first)
                dst = slot * ROW_BLOCK + lo - first
                _segment_copies(n, lambda done, size: pltpu.make_async_copy(
                    _row_span(hs_ref, src + done, size), _row_span(xbuf_ref, dst + done, size), sem.at[slot]), act)
                return carry
            return body

        jlo, jhi = jlo_ref[t], jhi_ref[t]
        lax.fori_loop(jnp.minimum(jlo, n_ctx_tiles), jnp.minimum(jhi, n_ctx_tiles), segment_of(hs_c_ref, 0), 0)
        lax.fori_loop(jnp.maximum(jlo, n_ctx_tiles), jnp.maximum(jhi, n_ctx_tiles),
                      segment_of(hs_l_ref, n_ctx_tiles), 0)

    @pl.when(i == 0)
    def _():
        xbuf_ref[...] = jnp.zeros(xbuf_ref.shape, F32)
        gather(0, _start)

    @pl.when(i + 1 < n_tiles)
    def _():
        gather(i + 1, _start)

    @pl.when((i == 0) | (te_ref[i] != te_ref[jnp.maximum(i - 1, 0)]))
    def _():
        w1b_ref[...] = w1_ref[0, 0].astype(BF16)
        w3b_ref[...] = w3_ref[0, 0].astype(BF16)
        w2b_ref[...] = w2_ref[0, 0].astype(BF16)

    n_rows = rows_ref[i]
    _segment_copies(n_rows, lambda done, size: pltpu.make_async_copy(
        _row_span(hs_c_ref, done, size), _row_span(xbuf_ref, (i % 2) * ROW_BLOCK + done, size), sem.at[i % 2]), _wait)

    @pl.when(n_rows > 0)
    def _():
        live = _iota((ROW_BLOCK, D_MODEL), 0) < n_rows
        x = jnp.where(live, _from_row_slabs(xbuf_ref, (i % 2) * ROW_BLOCK, ROW_BLOCK), 0.0).astype(BF16)
        hid = jax.nn.silu(_dot(x, w1b_ref[...])) * _dot(x, w3b_ref[...])
        _to_row_slabs(ys_ref, 0, _dot(hid.astype(BF16), w2b_ref[...]))

    @pl.when(n_rows == 0)
    def _():
        ys_ref[...] = jnp.zeros(ys_ref.shape, F32)


def _expert_call(l, plan, hs_c, hs_l, w1, w3, w2):
    tables = plan["expert_tables"]
    n_tiles = tables[0].shape[0]
    n_ctx_tiles = hs_c.shape[0] // (PAIR_BLOCK * ROW_SLABS)

    def weight(shape):
        return pl.BlockSpec((1, 1) + shape, lambda i, te, *_: (l, te[i], 0, 0))

    return pl.pallas_call(
        functools.partial(_expert_kernel, n_ctx_tiles),
        grid_spec=pltpu.PrefetchScalarGridSpec(
            num_scalar_prefetch=N_EXPERT_TABLES,
            grid=(n_tiles,),
            in_specs=[pl.BlockSpec(memory_space=pl.ANY), pl.BlockSpec(memory_space=pl.ANY),
                      weight((D_MODEL, D_EXPERT)), weight((D_MODEL, D_EXPERT)), weight((D_EXPERT, D_MODEL))],
            out_specs=pl.BlockSpec((ROW_BLOCK * ROW_SLABS, LANES), lambda i, *_: (i, 0)),
            scratch_shapes=[pltpu.VMEM((2 * ROW_BLOCK * ROW_SLABS, LANES), F32),
                            pltpu.VMEM((D_MODEL, D_EXPERT), BF16), pltpu.VMEM((D_MODEL, D_EXPERT), BF16),
                            pltpu.VMEM((D_EXPERT, D_MODEL), BF16), pltpu.SemaphoreType.DMA((2,))],
        ),
        out_shape=jax.ShapeDtypeStruct((n_tiles * ROW_BLOCK * ROW_SLABS, LANES), F32),
        compiler_params=pltpu.CompilerParams(
            dimension_semantics=("arbitrary",), vmem_limit_bytes=40 * 1024 * 1024),
        name="moe_experts",
    )(*tables, hs_c, hs_l, w1, w3, w2)


N_COMBINE_TABLES = 4


def _combine_kernel(n_ctx_tiles, cnt_ref, cpre_ref, lofs_ref, starts_ref, x_c_ref, x_l_ref, slot_c_ref, slot_l_ref,
                    wt_c_ref, wt_l_ref, mod_ref, ys_ref, xo_c_ref, xo_l_ref, buf_ref, sem):
    j = pl.program_id(0)
    n_tiles = pl.num_programs(0)

    def collect(t, act):
        slot = t % 2

        def body(e, carry):
            k = t * N_EXPERTS + e
            src, dst = starts_ref[e] + cpre_ref[k], slot * PAIR_BLOCK + lofs_ref[k]
            _segment_copies(cnt_ref[k], lambda done, size: pltpu.make_async_copy(
                _row_span(ys_ref, src + done, size), _row_span(buf_ref, dst + done, size), sem.at[slot]), act)
            return carry

        lax.fori_loop(0, N_EXPERTS, body, 0)

    @pl.when(j == 0)
    def _():
        collect(0, _start)

    @pl.when(j + 1 < n_tiles)
    def _():
        collect(j + 1, _start)

    for part in range(PAIR_BLOCK // ROW_BLOCK):
        pltpu.make_async_copy(
            _row_span(ys_ref, part * ROW_BLOCK, ROW_BLOCK),
            _row_span(buf_ref, (j % 2) * PAIR_BLOCK + part * ROW_BLOCK, ROW_BLOCK), sem.at[j % 2]).wait()
    rows = _from_row_slabs(buf_ref, (j % 2) * PAIR_BLOCK, PAIR_BLOCK)
    gate = mod_ref[0, 0, 5:6, :]

    def finish(x_ref, slot_ref, wt_ref, xo_ref):
        slots, wts = slot_ref[...].astype(F32), wt_ref[...]
        slot1, slot2 = slots[0:1], slots[1:2]
        weight_of_row = jnp.sum(_slot_one_hot(slot1, slot2, wts[0:1], wts[1:2]), axis=1, keepdims=True)
        hi, lo = _split2(rows * weight_of_row)
        perm = _slot_one_hot(slot1, slot2, 1.0, 1.0).astype(BF16)
        y = _dot_tn(perm, hi) + _dot_tn(perm, lo)
        xo_ref[...] = x_ref[...] + gate * y

    @pl.when(j < n_ctx_tiles)
    def _():
        finish(x_c_ref, slot_c_ref, wt_c_ref, xo_c_ref)

    @pl.when(j >= n_ctx_tiles)
    def _():
        finish(x_l_ref, slot_l_ref, wt_l_ref, xo_l_ref)


def _combine_call(l, plan, x_c, x_l, slot_c, slot_l, wt_c, wt_l, mods_all, mod_row_of_tile, ys):
    n_tiles = (x_c.shape[0] + x_l.shape[0]) // ROW_BLOCK
    n_ctx_tiles = x_c.shape[0] // ROW_BLOCK
    first, second = _two_streams(n_ctx_tiles)

    def lanes(index_map):
        return lambda i, *_: index_map(i)[::-1]

    return pl.pallas_call(
        functools.partial(_combine_kernel, n_ctx_tiles),
        grid_spec=pltpu.PrefetchScalarGridSpec(
            num_scalar_prefetch=N_COMBINE_TABLES,
            grid=(n_tiles,),
            in_specs=[pl.BlockSpec((ROW_BLOCK, D_MODEL), first),
                      pl.BlockSpec((ROW_BLOCK, D_MODEL), second),
                      pl.BlockSpec((2, ROW_BLOCK), lanes(first)),
                      pl.BlockSpec((2, ROW_BLOCK), lanes(second)),
                      pl.BlockSpec((2, ROW_BLOCK), lanes(first)),
                      pl.BlockSpec((2, ROW_BLOCK), lanes(second)),
                      pl.BlockSpec((1, 1, 6, D_MODEL), lambda i, *_: (l, mod_row_of_tile(i), 0, 0)),
                      pl.BlockSpec(memory_space=pl.ANY)],
            out_specs=[pl.BlockSpec((ROW_BLOCK, D_MODEL), first),
                       pl.BlockSpec((ROW_BLOCK, D_MODEL), second)],
            scratch_shapes=[pltpu.VMEM((2 * PAIR_BLOCK * ROW_SLABS, LANES), F32), pltpu.SemaphoreType.DMA((2,))],
        ),
        out_shape=[jax.ShapeDtypeStruct(x_c.shape, F32), jax.ShapeDtypeStruct(x_l.shape, F32)],
        compiler_params=pltpu.CompilerParams(
            dimension_semantics=("arbitrary",), vmem_limit_bytes=40 * 1024 * 1024),
        name="moe_combine",
    )(*plan["combine_tables"], x_c, x_l, slot_c, slot_l, wt_c, wt_l, mods_all, ys)


def _moe_plan(cnt):
    n_tok_tiles = cnt.shape[0]
    n_tiles = n_tok_tiles * PAIR_BLOCK // ROW_BLOCK + N_EXPERTS
    lofs = jnp.cumsum(cnt, axis=1) - cnt
    cpre = jnp.cumsum(cnt, axis=0) - cnt
    counts = jnp.sum(cnt, axis=0)
    padded = (counts + ROW_BLOCK - 1) // ROW_BLOCK * ROW_BLOCK
    ends = jnp.cumsum(padded)
    starts = ends - padded
    tile_start = jnp.arange(n_tiles, dtype=jnp.int32) * ROW_BLOCK
    tile_expert = jnp.minimum(
        jnp.sum((tile_start[:, None] >= ends[None, :]).astype(jnp.int32), axis=1), N_EXPERTS - 1)
    hot = tile_expert[:, None] == jnp.arange(N_EXPERTS, dtype=jnp.int32)[None, :]
    first = tile_start - jnp.sum(jnp.where(hot, starts[None, :], 0), axis=1)
    rows = jnp.clip(jnp.sum(jnp.where(hot, counts[None, :], 0), axis=1) - first, 0, ROW_BLOCK)
    seg_first = jnp.sum(jnp.where(hot[:, None, :], cpre[None, :, :], 0), axis=2)
    seg_rows = jnp.sum(jnp.where(hot[:, None, :], cnt[None, :, :], 0), axis=2)
    overlap = (seg_first < (first + rows)[:, None]) & (seg_first + seg_rows > first[:, None])
    j = jnp.arange(n_tok_tiles, dtype=jnp.int32)[None, :]
    jlo = jnp.min(jnp.where(overlap, j, n_tok_tiles), axis=1)
    jhi = jnp.max(jnp.where(overlap, j + 1, 0), axis=1)
    i32 = lambda a: a.astype(jnp.int32).reshape(-1)
    return {
        "expert_tables": tuple(i32(a) for a in (tile_expert, first, rows, jlo, jhi, cpre, cnt, lofs)),
        "combine_tables": tuple(i32(a) for a in (cnt, cpre, lofs, starts)),
    }


def _rope_tables(n_tok):
    n_rows = n_tok // GRID_W
    pos_r = jnp.repeat(jnp.arange(n_rows), GRID_W)
    pos_c = jnp.tile(jnp.arange(GRID_W), n_rows)
    half = DIFF_QK // 2
    nf = half // 2
    freqs = ROPE_BASE ** (-jnp.arange(nf, dtype=F32) / nf)

    def tables(pos):
        ang = pos.astype(F32)[:, None] * freqs
        cos, sin = jnp.cos(ang), jnp.sin(ang)
        return jnp.concatenate([cos, cos], axis=-1), jnp.concatenate([-sin, sin], axis=-1)

    cos_r, sin_r = tables(pos_r)
    cos_c, sin_c = tables(pos_c)
    cos = jnp.concatenate([cos_r, cos_c], axis=-1)
    sin = jnp.concatenate([sin_r, sin_c], axis=-1)
    return jnp.concatenate([cos, cos], axis=-1), jnp.concatenate([sin, sin], axis=-1)


def _mixer_weights(w_in, w_out, sgu_w, sgu_b, q_norm_g, k_norm_g, diff_lambda, diff_norm_g, gla_w2, gla_b,
                   gla_norm_g, norm1_g, norm2_g, router_w, router_bias):
    w_in_pad = jnp.pad(w_in.astype(BF16), ((0, 0), (0, 0), (0, D_PROJ_PAD - w_in.shape[2])))
    w2cat = jnp.zeros((DEPTH, LANES, 2 * W_GLA), F32)
    w2cat = w2cat.at[:, 0:GLA_RANK, 0:W_GLA].set(gla_w2[:, 0]).at[:, GLA_RANK:2 * GLA_RANK, W_GLA:].set(gla_w2[:, 1])
    return (
        norm1_g[:, None, :], norm2_g[:, None, :], w_in_pad, w_out.astype(BF16),
        sgu_w.astype(BF16), jnp.repeat(sgu_b.transpose(0, 2, 1), SGU_GROUP_W, axis=2),
        jnp.tile(q_norm_g, (1, W_QK // DIFF_QK))[:, None, :], jnp.tile(k_norm_g, (1, W_QK // DIFF_QK))[:, None, :],
        diff_lambda, diff_norm_g[:, None, :],
        w2cat.astype(BF16), gla_b.reshape(DEPTH, 1, 2 * W_GLA), jnp.tile(gla_norm_g, (1, GLA_HEADS))[:, None, :],
        router_w.T, router_bias[:, None],
    )


def kernel(x_prompt, x_sample, cache_k, cache_v, state_gla, c, c_ctx, w_in, w_out, sgu_w, sgu_b, q_norm_g, k_norm_g,
           diff_lambda, diff_norm_g, gla_w2, gla_b, gla_norm_g, norm1_g, norm2_g, ada_w, ada_b, router_w, router_bias,
           moe_w1, moe_w3, moe_w2):
    n_ctx_seq, ctx_len, _ = x_prompt.shape
    n_lat_seq, lat_len, _ = x_sample.shape
    n_ctx_tok = n_ctx_seq * ctx_len
    n_lat_tok = n_lat_seq * lat_len
    ctx_tiles = n_ctx_tok // ROW_BLOCK
    lat_tiles_per_seq = lat_len // ROW_BLOCK

    n_cond = 1 + n_lat_seq
    cond_t = jnp.zeros((D_MODEL, SUBLANES), F32).at[:, 0].set(c_ctx).at[:, 1:n_cond].set(c.T)
    mods_all = _adaln_call(cond_t, n_cond, ada_w, ada_b)[:, :n_cond].reshape(DEPTH, n_cond, 6, D_MODEL)
    weights = _mixer_weights(w_in, w_out, sgu_w, sgu_b, q_norm_g, k_norm_g, diff_lambda, diff_norm_g, gla_w2, gla_b,
                             gla_norm_g, norm1_g, norm2_g, router_w, router_bias)

    ck_all = cache_k.transpose(0, 1, 2, 4, 3, 5).reshape(cache_k.shape[:3] + (cache_k.shape[4], DIFF_V))
    st_all = jnp.einsum('bldhkv,hg->bldhvgk', state_gla, jnp.eye(GLA_HEADS, dtype=F32)).reshape(
        n_lat_seq, DEPTH, 2, W_GLA, W_GLA)
    cos, sin = _rope_tables(lat_len)
    extras = (ck_all, cache_v, st_all, cos, sin)

    def mod_row_of_tile(i):
        return jnp.where(i < ctx_tiles, 0, 1 + (i - ctx_tiles) // lat_tiles_per_seq)

    x_c = x_prompt.reshape(n_ctx_tok, D_MODEL)
    x_l = x_sample.reshape(n_lat_tok, D_MODEL)
    cache_bufs = ()
    for l in range(DEPTH):
        ctx_par = 1 if l == 0 else CTX_SEQS_PER_STEP
        x1_c, hs_c, slot_c, wt_c, cnt_c, *cache_bufs = _mixer_call(
            l, ctx_len, ctx_par, False, x_c, mods_all, weights, None, tuple(cache_bufs))
        x1_l, hs_l, slot_l, wt_l, cnt_l = _mixer_call(l, lat_len, 1, True, x_l, mods_all, weights, extras, ())
        plan = _moe_plan(jnp.concatenate([cnt_c[:, :, 0], cnt_l[:, :, 0]], axis=0))
        ys = _expert_call(l, plan, hs_c, hs_l, moe_w1, moe_w3, moe_w2)
        x_c, x_l = _combine_call(l, plan, x1_c, x1_l, slot_c, slot_l, wt_c, wt_l, mods_all, mod_row_of_tile, ys)

    new_k, new_v, new_s = cache_bufs
    return (x_c.reshape(x_prompt.shape), x_l.reshape(x_sample.shape), new_k, new_v, new_s)
```

```python
import functools
import math

import jax
import jax.numpy as jnp
from jax import lax
from jax.experimental import pallas as pl
from jax.experimental.pallas import tpu as pltpu

F32 = jnp.float32
BF16 = jnp.bfloat16

D_MODEL = 1024
DEPTH = 4
GRID_W = 64
SGU_GROUPS = 4
SGU_GROUP_W = 64
SGU_W = SGU_GROUPS * SGU_GROUP_W
SGU_CHUNK = 128
DIFF_HEADS = 4
DIFF_QK = 64
DIFF_V = 2 * DIFF_QK
ROPE_BASE = 10000.0
GLA_HEADS = 4
GLA_DK = 64
GLA_DV = 64
GLA_RANK = 16
GLA_GATE_NORM = 16.0
GLA_CHUNK = 64
N_EXPERTS = 16
N_GROUPS = 4
EXPERTS_PER_GROUP = N_EXPERTS // N_GROUPS
D_EXPERT = 512
EPS = 1e-6

LANES = 128
SUBLANES = 8
MXU_DIM = 256

ROW_BLOCK = MXU_DIM
ROW_SLABS = D_MODEL // LANES

C_AU, C_AV, C_BQ, C_BK, C_BV = 0, 256, 512, 1024, 1536
C_CQ, C_CK, C_CV, C_CR, C_LR = 2048, 2304, 2560, 2816, 3072
D_PROJ_MAIN = 3072
D_PROJ_PAD = D_PROJ_MAIN + LANES
W_QK = DIFF_HEADS * 2 * DIFF_QK
W_GLA = GLA_HEADS * GLA_DK
M_A, M_B, M_C = 0, SGU_W, SGU_W + DIFF_HEADS * DIFF_V


def _split2(x):
    hi = x.astype(BF16)
    lo = (x - hi.astype(F32)).astype(BF16)
    return hi, lo


def _split3(x):
    hi = x.astype(BF16)
    r = x - hi.astype(F32)
    mid = r.astype(BF16)
    lo = (r - mid.astype(F32)).astype(BF16)
    return hi, mid, lo


def _dot(a, b):
    return jnp.dot(a, b, preferred_element_type=F32)


def _dot_nt(a, b):
    return lax.dot_general(a, b, (((1,), (1,)), ((), ())), preferred_element_type=F32)


def _dot_tn(a, b):
    return lax.dot_general(a, b, (((0,), (0,)), ((), ())), preferred_element_type=F32)


def _iota(shape, dim):
    return lax.broadcasted_iota(jnp.int32, shape, dim)


def _block_ones(width, block):
    r = _iota((width, width), 0) // block
    c = _iota((width, width), 1) // block
    return (r == c)


def _group_sum(z, block):
    width = z.shape[-1]
    outs = []
    for s in range(0, width, MXU_DIM):
        w = min(MXU_DIM, width - s)
        ones = _block_ones(w, block).astype(BF16)
        hi, lo = _split2(z[:, s:s + w])
        outs.append(_dot(hi, ones) + _dot(lo, ones))
    return outs[0] if len(outs) == 1 else jnp.concatenate(outs, axis=-1)


def _group_rms(z, block):
    ms = _group_sum(z * z, block) * (1.0 / block)
    return z * lax.rsqrt(ms + EPS)


def _row_rms(z):
    return z * lax.rsqrt(jnp.mean(z * z, axis=-1, keepdims=True) + EPS)


def _log_sigmoid(x):
    return jnp.minimum(x, 0.0) - jnp.log1p(jnp.exp(-jnp.abs(x)))


ADA_COLS = 1536


def _adaln_kernel(n_cond, cond_t_ref, w_ref, b_ref, o_ref):
    sc = jax.nn.silu(cond_t_ref[...])
    w = w_ref[0]
    rows = [jnp.sum(sc[:, r:r + 1] * w, axis=0, keepdims=True) + b_ref[0] for r in range(n_cond)]
    o_ref[0] = jnp.concatenate(rows + [jnp.zeros((SUBLANES - n_cond, w.shape[1]), F32)], axis=0)


def _adaln_call(cond_t, n_cond, ada_w, ada_b):
    n_col = 6 * D_MODEL // ADA_COLS
    return pl.pallas_call(
        functools.partial(_adaln_kernel, n_cond),
        grid=(DEPTH, n_col),
        in_specs=[
            pl.BlockSpec((D_MODEL, SUBLANES), lambda l, j: (0, 0)),
            pl.BlockSpec((1, D_MODEL, ADA_COLS), lambda l, j: (l, 0, j)),
            pl.BlockSpec((1, 1, ADA_COLS), lambda l, j: (l, 0, j)),
        ],
        out_specs=pl.BlockSpec((1, SUBLANES, ADA_COLS), lambda l, j: (l, 0, j)),
        out_shape=jax.ShapeDtypeStruct((DEPTH, SUBLANES, 6 * D_MODEL), F32),
        compiler_params=pltpu.CompilerParams(
            dimension_semantics=("arbitrary", "arbitrary"), vmem_limit_bytes=40 * 1024 * 1024),
        name="adaln",
    )(cond_t, ada_w, ada_b.reshape(DEPTH, 1, 6 * D_MODEL))


def _route(hn, rwt_ref, rb_ref):
    h_hi, h_lo = _split2(hn)
    rw = rwt_ref[...]
    rw_hi = rw.astype(BF16)
    rw_lo = (rw - rw_hi.astype(F32)).astype(BF16)
    logits = _dot_nt(rw_hi, h_hi) + _dot_nt(rw_hi, h_lo) + _dot_nt(rw_lo, h_hi)
    aff = jax.nn.sigmoid(logits)
    sel = aff + rb_ref[...]
    n_tok = sel.shape[1]

    def top2_sum(a, b, c, d):
        hi1, lo1 = jnp.maximum(a, b), jnp.minimum(a, b)
        hi2, lo2 = jnp.maximum(c, d), jnp.minimum(c, d)
        return jnp.maximum(hi1, hi2) + jnp.maximum(jnp.minimum(hi1, hi2), jnp.maximum(lo1, lo2))

    scores = []
    for g in range(N_GROUPS):
        rows = [sel[EXPERTS_PER_GROUP * g + j:EXPERTS_PER_GROUP * g + j + 1, :] for j in range(EXPERTS_PER_GROUP)]
        scores.append(top2_sum(*rows))
    best = jnp.zeros((1, n_tok), jnp.int32)
    best_score = scores[0]
    for g in range(1, N_GROUPS):
        upd = scores[g] > best_score
        best = jnp.where(upd, g, best)
        best_score = jnp.where(upd, scores[g], best_score)

    eid_i = _iota((N_EXPERTS, n_tok), 0)
    eid = eid_i.astype(F32)
    neg = jnp.float32(-jnp.inf)
    msel = jnp.where(eid_i // EXPERTS_PER_GROUP == best, sel, neg)
    m1 = jnp.max(msel, axis=0, keepdims=True)
    idx1 = jnp.min(jnp.where(msel == m1, eid, float(N_EXPERTS)), axis=0, keepdims=True)
    msel2 = jnp.where(eid == idx1, neg, msel)
    m2 = jnp.max(msel2, axis=0, keepdims=True)
    idx2 = jnp.min(jnp.where(msel2 == m2, eid, float(N_EXPERTS)), axis=0, keepdims=True)
    w1 = jnp.sum(jnp.where(eid == idx1, aff, 0.0), axis=0, keepdims=True)
    w2 = jnp.sum(jnp.where(eid == idx2, aff, 0.0), axis=0, keepdims=True)
    wsum = w1 + w2
    return idx1.astype(jnp.int32), idx2.astype(jnp.int32), w1 / wsum, w2 / wsum


def _local_slots(idx1, idx2):
    n_tok = idx1.shape[1]
    eid = _iota((N_EXPERTS, n_tok), 0)
    hot1, hot2 = eid == idx1, eid == idx2
    hot = jnp.where(hot1, 1.0, jnp.where(hot2, 1.0, 0.0))
    earlier = jnp.where(_iota((n_tok, n_tok), 0) < _iota((n_tok, n_tok), 1), 1.0, 0.0).astype(BF16)
    before_in_expert = _dot(hot.astype(BF16), earlier)
    counts = jnp.sum(hot, axis=1, keepdims=True)
    lower = jnp.where(_iota((N_EXPERTS, N_EXPERTS), 1) < _iota((N_EXPERTS, N_EXPERTS), 0), 1.0, 0.0).astype(BF16)
    first_slot = _dot(lower, jnp.broadcast_to(counts, (N_EXPERTS, LANES)).astype(BF16))[:, 0:1]
    slot = before_in_expert + first_slot
    slot1 = jnp.sum(jnp.where(hot1, slot, 0.0), axis=0, keepdims=True)
    slot2 = jnp.sum(jnp.where(hot2, slot, 0.0), axis=0, keepdims=True)
    return slot1, slot2, counts


def _slot_one_hot(slot1, slot2, v1, v2):
    n_tok = slot1.shape[1]
    row = _iota((2 * n_tok, n_tok), 0).astype(F32)
    return jnp.where(row == slot1, v1, jnp.where(row == slot2, v2, 0.0))


N_MIXER_WEIGHTS = 15
CTX_SEQS_PER_STEP = 2
MAX_INLINE_BLOCKS = 2


def _mixer_kernel(n_tok, n_par, latent, n_alias, lam_init, *refs):
    it = iter(refs)
    x_ref, mod_ref = next(it), next(it)
    (n1_ref, n2_ref, win_ref, wout_ref, sw_ref, sb_ref, qg_ref, kg_ref, dl_ref, dg_ref,
     w2c_ref, gb_ref, gg_ref, rwt_ref, rb_ref) = (next(it) for _ in range(N_MIXER_WEIGHTS))
    if latent:
        ck_ref, cv_ref, st0_ref, cos_ref, sin_ref = (next(it) for _ in range(5))
    for _ in range(n_alias):
        next(it)
    xo_ref, hs_ref, slot_ref, wt_ref, cnt_ref = (next(it) for _ in range(5))
    if not latent:
        ko_ref, vo_ref, so_ref = (next(it) for _ in range(3))
    proj_ref, mix_ref, q_ref, k_ref, v_ref = (next(it) for _ in range(5))
    gq_ref, gke_ref, gv_ref, gr_ref, dec_ref, go_ref, st_ref = (next(it) for _ in range(7))

    n_blk = n_tok // ROW_BLOCK
    n_ctx = k_ref.shape[0] - n_par * n_tok
    n_keys = n_ctx + n_tok
    mod = mod_ref[0, 0]

    def blocks(body):
        if n_par * n_blk <= MAX_INLINE_BLOCKS:
            for r in range(n_par * n_blk):
                body(r)
        else:
            def step(r, carry):
                body(r)
                return carry
            lax.fori_loop(0, n_par * n_blk, step, 0)

    def aligned(start, size):
        return pl.ds(start if isinstance(start, int) else pl.multiple_of(start, size), size)

    def block_rows(r, offset=0):
        return aligned(offset + r * ROW_BLOCK, ROW_BLOCK)

    if not latent:
        for ref in (ko_ref, vo_ref, so_ref):
            for q in range(n_par):
                for other in range(1, ref.shape[1]):
                    ref[q, other] = jnp.zeros(ref.shape[2:], F32)

    lane_group = _iota((SGU_CHUNK, SGU_W), 1) // SGU_GROUP_W
    blk_r = _iota((ROW_BLOCK, ROW_BLOCK), 0)
    blk_c = _iota((ROW_BLOCK, ROW_BLOCK), 1)
    same_chunk = (blk_r // GLA_CHUNK) == (blk_c // GLA_CHUNK)
    tri = (jnp.where(same_chunk & (blk_c <= blk_r), 1.0, 0.0).astype(BF16),
           jnp.where(same_chunk & (blk_c >= blk_r), 1.0, 0.0).astype(BF16))
    chunks_per_blk = ROW_BLOCK // GLA_CHUNK
    head_of_lane = _iota((GLA_CHUNK, W_GLA), 1) // GLA_DK
    stack_r = _iota((GLA_HEADS * GLA_CHUNK, GLA_CHUNK), 0) % GLA_CHUNK
    stack_c = _iota((GLA_HEADS * GLA_CHUNK, GLA_CHUNK), 1)
    causal = (stack_c <= stack_r, stack_c >= stack_r)

    if latent:
        for h in range(DIFF_HEADS):
            k_ref[0:n_ctx, h * DIFF_V:(h + 1) * DIFF_V] = ck_ref[0, 0, h].astype(BF16)
            v_ref[0:n_ctx, h * DIFF_V:(h + 1) * DIFF_V] = cv_ref[0, 0, h].astype(BF16)
        pair_lo = (_iota((ROW_BLOCK, W_QK), 1) % (DIFF_QK // 2)) < (DIFF_QK // 4)

        def rope(z, rows):
            cos = jnp.concatenate([cos_ref[rows, :]] * DIFF_HEADS, axis=-1)
            sin = jnp.concatenate([sin_ref[rows, :]] * DIFF_HEADS, axis=-1)
            shift = DIFF_QK // 4
            swapped = jnp.where(pair_lo, pltpu.roll(z, W_QK - shift, 1), pltpu.roll(z, shift, 1))
            return z * cos + swapped * sin

    def modulated_input(r):
        h = _row_rms(x_ref[block_rows(r), :]) * n1_ref[0]
        return (h * (1.0 + mod[1:2, :]) + mod[0:1, :]).astype(BF16)

    def spatial_gating(r):
        for c in range(ROW_BLOCK // SGU_CHUNK):
            local = slice(c * SGU_CHUNK, (c + 1) * SGU_CHUNK)
            u = jax.nn.gelu(proj_ref[local, C_AU:C_AU + SGU_W])
            v = _group_rms(jax.nn.gelu(proj_ref[local, C_AV:C_AV + SGU_W]), SGU_GROUP_W).astype(BF16)
            s = sb_ref[0]
            for g in range(SGU_GROUPS):
                s = s + jnp.where(lane_group == g, _dot(sw_ref[0, g], v), 0.0)
            mix_ref[aligned(r * ROW_BLOCK + c * SGU_CHUNK, SGU_CHUNK), M_A:M_A + SGU_W] = (u * s).astype(BF16)

    def attention_operands(r):
        rows = block_rows(r)
        key_rows = block_rows(r, n_ctx)
        seq, seq_rows = r // n_blk, block_rows(r % n_blk)
        qn = _group_rms(proj_ref[:, C_BQ:C_BQ + W_QK], DIFF_QK) * qg_ref[0]
        kn = _group_rms(proj_ref[:, C_BK:C_BK + W_QK], DIFF_QK) * kg_ref[0]
        vv = proj_ref[:, C_BV:C_BV + W_QK]
        if latent:
            qn, kn = rope(qn, rows), rope(kn, rows)
        else:
            for h in range(DIFF_HEADS):
                for i in range(2):
                    lo = h * DIFF_V + i * DIFF_QK
                    ko_ref[seq, 0, h, i, seq_rows, :] = kn[:, lo:lo + DIFF_QK]
                vo_ref[seq, 0, h, seq_rows, :] = vv[:, h * DIFF_V:(h + 1) * DIFF_V]
        q_ref[rows, :] = (qn * (DIFF_QK ** -0.5)).astype(BF16)
        k_ref[key_rows, :] = kn.astype(BF16)
        v_ref[key_rows, :] = vv.astype(BF16)

    def gla_operands(r):
        rows = block_rows(r)
        gpre = _dot(proj_ref[:, C_LR:C_LR + LANES].astype(BF16), w2c_ref[0]) + gb_ref[0]
        gate = _log_sigmoid(gpre) * (1.0 / GLA_GATE_NORM)
        gq = proj_ref[:, C_CQ:C_CQ + W_GLA] * (GLA_DK ** -0.5)
        gk = proj_ref[:, C_CK:C_CK + W_GLA]
        gv = proj_ref[:, C_CV:C_CV + W_GLA].astype(BF16)
        gv_ref[rows, :] = gv
        gr_ref[rows, :] = proj_ref[:, C_CR:C_CR + W_GLA]
        for d in range(2):
            g = gate[:, d * W_GLA:(d + 1) * W_GLA]
            b = sum(_dot(tri[d], p) for p in _split3(g))
            last = GLA_CHUNK - 1 if d == 0 else 0
            b_last = jnp.concatenate(
                [jnp.broadcast_to(b[c * GLA_CHUNK + last:c * GLA_CHUNK + last + 1, :], (GLA_CHUNK, W_GLA))
                 for c in range(chunks_per_blk)], axis=0)
            q_dec = (gq * jnp.exp(b)).astype(BF16)
            k_inv = (gk * jnp.exp(-b)).astype(BF16)
            gq_ref[d, rows, :] = q_dec
            gke_ref[d, rows, :] = (gk * jnp.exp(b_last - b)).astype(BF16)
            for c in range(chunks_per_blk):
                row = c * GLA_CHUNK + last
                dec_ref[d, r * chunks_per_blk + c] = jnp.exp(b[row:row + 1, :])
                chunk = slice(c * GLA_CHUNK, (c + 1) * GLA_CHUNK)
                qd = q_dec[chunk]
                q_stack = jnp.concatenate(
                    [jnp.where(head_of_lane == h, qd, jnp.zeros_like(qd)) for h in range(GLA_HEADS)], axis=0)
                attn = jnp.where(causal[d], _dot_nt(q_stack, k_inv[chunk]), 0.0)
                spread = _dot(attn.astype(BF16), gv[chunk])
                o = jnp.zeros((GLA_CHUNK, W_GLA), F32)
                for h in range(GLA_HEADS):
                    o = o + jnp.where(head_of_lane == h, spread[h * GLA_CHUNK:(h + 1) * GLA_CHUNK, :], 0.0)
                go_ref[d, aligned(r * ROW_BLOCK + c * GLA_CHUNK, GLA_CHUNK), :] = o

    def project_and_split(r):
        proj_ref[...] = _dot(modulated_input(r), win_ref[0])
        spatial_gating(r)
        attention_operands(r)
        gla_operands(r)

    blocks(project_and_split)

    dl = dl_ref[0]
    lam = (jnp.exp(jnp.sum(dl[0:1] * dl[1:2], axis=-1, keepdims=True))
           - jnp.exp(jnp.sum(dl[2:3] * dl[3:4], axis=-1, keepdims=True)) + lam_init)
    sub0 = (_iota((ROW_BLOCK, DIFF_V), 1) < DIFF_QK)

    def softmax(s):
        e = jnp.exp(s - jnp.max(s, axis=-1, keepdims=True))
        return e, jnp.sum(e, axis=-1, keepdims=True)

    def attn_block(r):
        rows = block_rows(r)
        keys = aligned((r // n_blk) * n_keys, n_keys)
        for h in range(DIFF_HEADS):
            cols = slice(h * DIFF_V, (h + 1) * DIFF_V)
            qh = q_ref[rows, cols]
            kh = k_ref[keys, cols]
            e0, z0 = softmax(_dot_nt(jnp.where(sub0, qh, jnp.zeros_like(qh)), kh))
            e1, z1 = softmax(_dot_nt(jnp.where(sub0, jnp.zeros_like(qh), qh), kh))
            w = e0 / z0 - lam * (e1 / z1)
            o = _dot(w.astype(BF16), v_ref[keys, cols])
            o = _row_rms(o) * dg_ref[0] * (1.0 - lam_init)
            mix_ref[rows, M_B + h * DIFF_V:M_B + (h + 1) * DIFF_V] = o.astype(BF16)

    blocks(attn_block)

    if latent:
        st_ref[0] = st0_ref[0, 0]
    else:
        st_ref[...] = jnp.zeros(st_ref.shape, F32)

    n_chunk = n_tok // GLA_CHUNK
    st_diag = (_iota((W_GLA, W_GLA), 0) // GLA_DV) == (_iota((W_GLA, W_GLA), 1) // GLA_DK)

    def gla_step(c, carry):
        for seq in range(n_par):
            for d in range(2):
                cc = seq * n_chunk + (c if d == 0 else n_chunk - 1 - c)
                rows = pl.ds(pl.multiple_of(cc * GLA_CHUNK, GLA_CHUNK), GLA_CHUNK)
                st = st_ref[seq, d]
                go_ref[d, rows, :] = go_ref[d, rows, :] + _dot_nt(gq_ref[d, rows, :], st.astype(BF16))
                upd = _dot_tn(gv_ref[rows, :], gke_ref[d, rows, :])
                st_ref[seq, d] = dec_ref[d, cc] * st + jnp.where(st_diag, upd, 0.0)
        return carry

    lax.fori_loop(0, n_chunk, gla_step, 0)

    if not latent:
        for seq in range(n_par):
            for d in range(2):
                s_full = st_ref[seq, d].T
                for h in range(GLA_HEADS):
                    so_ref[seq, 0, d, h] = s_full[h * GLA_DK:(h + 1) * GLA_DK, h * GLA_DV:(h + 1) * GLA_DV]

    for r in range(n_par * n_blk):
        rows = pl.ds(r * ROW_BLOCK, ROW_BLOCK)
        oc = _group_rms(go_ref[0, rows, :] + go_ref[1, rows, :], GLA_DV) * gg_ref[0]
        oc = oc * jax.nn.silu(gr_ref[rows, :])
        mix_ref[rows, M_C:M_C + W_GLA] = oc.astype(BF16)
        x1 = x_ref[rows, :] + mod[2:3, :] * _dot(mix_ref[rows, :], wout_ref[0])
        xo_ref[rows, :] = x1
        hn = _row_rms(x1) * n2_ref[0]
        hn = hn * (1.0 + mod[4:5, :]) + mod[3:4, :]
        idx1, idx2, w1, w2 = _route(hn, rwt_ref, rb_ref)
        slot1, slot2, counts = _local_slots(idx1, idx2)
        perm = _slot_one_hot(slot1, slot2, 1.0, 1.0).astype(BF16)
        _to_row_slabs(hs_ref, 2 * r * ROW_BLOCK, _dot(perm, hn.astype(BF16)))
        slot_ref[:, rows] = jnp.concatenate([slot1, slot2], axis=0).astype(jnp.int32)
        wt_ref[:, rows] = jnp.concatenate([w1, w2], axis=0)
        cnt_ref[r] = jnp.broadcast_to(counts, (N_EXPERTS, LANES)).astype(jnp.int32)


def _mixer_call(l, n_tok, n_par, latent, x, mods_all, weights, extras, cache_bufs):
    n_seq = x.shape[0] // n_tok
    n_all = x.shape[0]
    assert n_seq % n_par == 0 and not (latent and n_par > 1)
    n_step_tok = n_par * n_tok
    n_keys = n_step_tok + (extras[0].shape[3] if latent else 0)
    n_chunk = n_step_tok // GLA_CHUNK
    lam_init = 0.8 - 0.6 * math.exp(-0.3 * l)

    single = pl.Buffered(1)
    seq_mode = single if latent else None

    def layer(arr):
        tail = arr.shape[1:]
        return pl.BlockSpec((1,) + tail, lambda s, _n=len(tail): (l,) + (0,) * _n, pipeline_mode=single)

    def const(arr):
        return pl.BlockSpec(arr.shape, lambda s, _n=arr.ndim: (0,) * _n, pipeline_mode=single)

    def tok_spec(width):
        return pl.BlockSpec((n_step_tok, width), lambda s: (s, 0), pipeline_mode=seq_mode)

    mod_row = (lambda s: 1 + s) if latent else (lambda s: 0)
    in_specs = [tok_spec(D_MODEL),
                pl.BlockSpec((1, 1, 6, D_MODEL), lambda s: (l, mod_row(s), 0, 0))]
    in_specs += [layer(w) for w in weights[:N_MIXER_WEIGHTS - 2]] + [const(w) for w in weights[-2:]]
    operands = [x, mods_all] + list(weights)
    if latent:
        ck, cv, st0, cos, sin = extras
        in_specs += [
            pl.BlockSpec((1, 1) + ck.shape[2:], lambda s: (s, l, 0, 0, 0)),
            pl.BlockSpec((1, 1) + cv.shape[2:], lambda s: (s, l, 0, 0, 0)),
            pl.BlockSpec((1, 1) + st0.shape[2:], lambda s: (s, l, 0, 0, 0)),
            const(cos), const(sin),
        ]
        operands += [ck, cv, st0, cos, sin]
    n_in = len(operands)
    in_specs += [pl.BlockSpec(memory_space=pl.ANY)] * len(cache_bufs)
    operands += list(cache_bufs)

    tiles_per_step = n_step_tok // ROW_BLOCK
    out_shape = [
        jax.ShapeDtypeStruct((n_all, D_MODEL), F32),
        jax.ShapeDtypeStruct((2 * n_all * ROW_SLABS, LANES), F32),
        jax.ShapeDtypeStruct((2, n_all), jnp.int32),
        jax.ShapeDtypeStruct((2, n_all), F32),
        jax.ShapeDtypeStruct((n_all // ROW_BLOCK, N_EXPERTS, LANES), jnp.int32),
    ]
    out_specs = [
        tok_spec(D_MODEL),
        pl.BlockSpec((2 * n_step_tok * ROW_SLABS, LANES), lambda s: (s, 0), pipeline_mode=seq_mode),
        pl.BlockSpec((2, n_step_tok), lambda s: (0, s)),
        pl.BlockSpec((2, n_step_tok), lambda s: (0, s)),
        pl.BlockSpec((tiles_per_step, N_EXPERTS, LANES), lambda s: (s, 0, 0)),
    ]
    n_shared_out = len(out_shape)
    aliases = {}
    if not latent:
        out_shape += [
            jax.ShapeDtypeStruct((n_seq, DEPTH, DIFF_HEADS, 2, n_tok, DIFF_QK), F32),
            jax.ShapeDtypeStruct((n_seq, DEPTH, DIFF_HEADS, n_tok, DIFF_V), F32),
            jax.ShapeDtypeStruct((n_seq, DEPTH, 2, GLA_HEADS, GLA_DK, GLA_DV), F32),
        ]
        n_lay, at = (1, l) if cache_bufs else (DEPTH, 0)
        out_specs += [
            pl.BlockSpec((n_par, n_lay, DIFF_HEADS, 2, n_tok, DIFF_QK), lambda s: (s, at, 0, 0, 0, 0)),
            pl.BlockSpec((n_par, n_lay, DIFF_HEADS, n_tok, DIFF_V), lambda s: (s, at, 0, 0, 0)),
            pl.BlockSpec((n_par, n_lay, 2, GLA_HEADS, GLA_DK, GLA_DV), lambda s: (s, at, 0, 0, 0, 0)),
        ]
        aliases = {n_in + j: n_shared_out + j for j in range(len(cache_bufs))}
    scratch = [
        pltpu.VMEM((ROW_BLOCK, D_PROJ_PAD), F32),
        pltpu.VMEM((n_step_tok, D_MODEL), BF16),
        pltpu.VMEM((n_step_tok, W_QK), BF16),
        pltpu.VMEM((n_keys, W_QK), BF16),
        pltpu.VMEM((n_keys, W_QK), BF16),
        pltpu.VMEM((2, n_step_tok, W_GLA), BF16),
        pltpu.VMEM((2, n_step_tok, W_GLA), BF16),
        pltpu.VMEM((n_step_tok, W_GLA), BF16),
        pltpu.VMEM((n_step_tok, W_GLA), F32),
        pltpu.VMEM((2, n_chunk, 1, W_GLA), F32),
        pltpu.VMEM((2, n_step_tok, W_GLA), F32),
        pltpu.VMEM((n_par, 2, W_GLA, W_GLA), F32),
    ]
    return pl.pallas_call(
        functools.partial(_mixer_kernel, n_tok, n_par, latent, len(cache_bufs), lam_init),
        grid=(n_seq // n_par,),
        in_specs=in_specs,
        out_specs=out_specs,
        out_shape=out_shape,
        scratch_shapes=scratch,
        input_output_aliases=aliases,
        compiler_params=pltpu.CompilerParams(
            dimension_semantics=("arbitrary",), vmem_limit_bytes=56 * 1024 * 1024),
        name="mixer_latent" if latent else "mixer_context",
    )(*operands)


PAIR_BLOCK = 2 * ROW_BLOCK
COPY_SIZES = tuple(ROW_BLOCK >> k for k in range(ROW_BLOCK.bit_length()))
LARGE_COPY = 64
GATHER_AHEAD = 2
GATHER_SLOTS = GATHER_AHEAD + 1


def _segment_copies(n_rows, make_copy, act):
    def copy_if_set(size):
        @pl.when((n_rows & size) != 0)
        def _():
            act(make_copy(n_rows & (-2 * size), size))

    n_large = COPY_SIZES.index(LARGE_COPY) + 1

    @pl.when(n_rows >= LARGE_COPY)
    def _():
        for size in COPY_SIZES[:n_large]:
            copy_if_set(size)

    for size in COPY_SIZES[n_large:]:
        copy_if_set(size)


def _start(copy):
    copy.start()


def _wait(copy):
    copy.wait()


def _slab_rows(first_row, n_rows, slab):
    return pl.ds(first_row * ROW_SLABS + slab, n_rows, stride=ROW_SLABS)


def _to_row_slabs(ref, first_row, value):
    for s in range(ROW_SLABS):
        ref[_slab_rows(first_row, value.shape[0], s), :] = value[:, s * LANES:(s + 1) * LANES]


def _from_row_slabs(ref, first_row, n_rows):
    return jnp.concatenate([ref[_slab_rows(first_row, n_rows, s), :] for s in range(ROW_SLABS)], axis=-1)


def _row_span(ref, first_row, n_rows):
    return ref.at[pl.ds(pl.multiple_of(first_row * ROW_SLABS, ROW_SLABS), n_rows * ROW_SLABS)]


def _two_streams(n_first_tiles):
    def first(i, *_):
        return (jnp.minimum(i, n_first_tiles - 1), 0)

    def second(i, *_):
        return (jnp.maximum(i - n_first_tiles, 0), 0)

    return first, second


N_EXPERT_TABLES = 8


def _expert_kernel(n_ctx_tiles, te_ref, first_ref, rows_ref, jlo_ref, jhi_ref, cpre_ref, cnt_ref, lofs_ref,
                   hs_c_ref, hs_l_ref, w1_ref, w3_ref, w2_ref, ys_ref, xbuf_ref, w1b_ref, w3b_ref, w2b_ref, sem):
    i = pl.program_id(0)
    n_tiles = pl.num_programs(0)

    def gather(t, act):
        slot = t % GATHER_SLOTS
        e, first = te_ref[t], first_ref[t]
        last = first + rows_ref[t]

        def segment_of(hs_ref, first_tile):
            def body(j, carry):
                k = j * N_EXPERTS + e
                seg_first = cpre_ref[k]
                lo = jnp.maximum(seg_first, first)
                n = jnp.maximum(jnp.minimum(seg_first + cnt_ref[k], last) - lo, 0)
                src = (j - first_tile) * PAIR_BLOCK + lofs_ref[k] + (lo - seg_first)
                dst = slot * ROW_BLOCK + lo - first
                _segment_copies(n, lambda done, size: pltpu.make_async_copy(
                    _row_span(hs_ref, src + done, size), _row_span(xbuf_ref, dst + done, size), sem.at[slot]), act)
                return carry
            return body

        jlo, jhi = jlo_ref[t], jhi_ref[t]
        lax.fori_loop(jnp.minimum(jlo, n_ctx_tiles), jnp.minimum(jhi, n_ctx_tiles), segment_of(hs_c_ref, 0), 0)
        lax.fori_loop(jnp.maximum(jlo, n_ctx_tiles), jnp.maximum(jhi, n_ctx_tiles),
                      segment_of(hs_l_ref, n_ctx_tiles), 0)

    @pl.when(i == 0)
    def _():
        xbuf_ref[...] = jnp.zeros(xbuf_ref.shape, F32)
        for t in range(GATHER_AHEAD):
            gather(t, _start)

    @pl.when(i + GATHER_AHEAD < n_tiles)
    def _():
        gather(i + GATHER_AHEAD, _start)

    @pl.when((i == 0) | (te_ref[i] != te_ref[jnp.maximum(i - 1, 0)]))
    def _():
        w1b_ref[...] = w1_ref[0, 0].astype(BF16)
        w3b_ref[...] = w3_ref[0, 0].astype(BF16)
        w2b_ref[...] = w2_ref[0, 0].astype(BF16)

    n_rows = rows_ref[i]
    slot = i % GATHER_SLOTS
    _segment_copies(n_rows, lambda done, size: pltpu.make_async_copy(
        _row_span(hs_c_ref, done, size), _row_span(xbuf_ref, slot * ROW_BLOCK + done, size), sem.at[slot]), _wait)

    @pl.when(n_rows > 0)
    def _():
        live = _iota((ROW_BLOCK, D_MODEL), 0) < n_rows
        x = jnp.where(live, _from_row_slabs(xbuf_ref, slot * ROW_BLOCK, ROW_BLOCK), 0.0).astype(BF16)
        hid = jax.nn.silu(_dot(x, w1b_ref[...])) * _dot(x, w3b_ref[...])
        _to_row_slabs(ys_ref, 0, _dot(hid.astype(BF16), w2b_ref[...]))

    @pl.when(n_rows == 0)
    def _():
        ys_ref[...] = jnp.zeros(ys_ref.shape, F32)


def _expert_call(l, plan, hs_c, hs_l, w1, w3, w2):
    tables = plan["expert_tables"]
    n_tiles = tables[0].shape[0]
    n_ctx_tiles = hs_c.shape[0] // (PAIR_BLOCK * ROW_SLABS)

    def weight(shape):
        return pl.BlockSpec((1, 1) + shape, lambda i, te, *_: (l, te[i], 0, 0))

    return pl.pallas_call(
        functools.partial(_expert_kernel, n_ctx_tiles),
        grid_spec=pltpu.PrefetchScalarGridSpec(
            num_scalar_prefetch=N_EXPERT_TABLES,
            grid=(n_tiles,),
            in_specs=[pl.BlockSpec(memory_space=pl.ANY), pl.BlockSpec(memory_space=pl.ANY),
                      weight((D_MODEL, D_EXPERT)), weight((D_MODEL, D_EXPERT)), weight((D_EXPERT, D_MODEL))],
            out_specs=pl.BlockSpec((ROW_BLOCK * ROW_SLABS, LANES), lambda i, *_: (i, 0)),
            scratch_shapes=[pltpu.VMEM((GATHER_SLOTS * ROW_BLOCK * ROW_SLABS, LANES), F32),
                            pltpu.VMEM((D_MODEL, D_EXPERT), BF16), pltpu.VMEM((D_MODEL, D_EXPERT), BF16),
                            pltpu.VMEM((D_EXPERT, D_MODEL), BF16), pltpu.SemaphoreType.DMA((GATHER_SLOTS,))],
        ),
        out_shape=jax.ShapeDtypeStruct((n_tiles * ROW_BLOCK * ROW_SLABS, LANES), F32),
        compiler_params=pltpu.CompilerParams(
            dimension_semantics=("arbitrary",), vmem_limit_bytes=40 * 1024 * 1024),
        name="moe_experts",
    )(*tables, hs_c, hs_l, w1, w3, w2)


N_COMBINE_TABLES = 4


def _combine_kernel(n_ctx_tiles, cnt_ref, cpre_ref, lofs_ref, starts_ref, x_c_ref, x_l_ref, slot_c_ref, slot_l_ref,
                    wt_c_ref, wt_l_ref, mod_ref, ys_ref, xo_c_ref, xo_l_ref, buf_ref, sem):
    j = pl.program_id(0)
    n_tiles = pl.num_programs(0)

    def collect(t, act):
        slot = t % GATHER_SLOTS

        def body(e, carry):
            k = t * N_EXPERTS + e
            src, dst = starts_ref[e] + cpre_ref[k], slot * PAIR_BLOCK + lofs_ref[k]
            _segment_copies(cnt_ref[k], lambda done, size: pltpu.make_async_copy(
                _row_span(ys_ref, src + done, size), _row_span(buf_ref, dst + done, size), sem.at[slot]), act)
            return carry

        lax.fori_loop(0, N_EXPERTS, body, 0)

    @pl.when(j == 0)
    def _():
        for t in range(GATHER_AHEAD):
            collect(t, _start)

    @pl.when(j + GATHER_AHEAD < n_tiles)
    def _():
        collect(j + GATHER_AHEAD, _start)

    slot = j % GATHER_SLOTS
    for part in range(PAIR_BLOCK // ROW_BLOCK):
        pltpu.make_async_copy(
            _row_span(ys_ref, part * ROW_BLOCK, ROW_BLOCK),
            _row_span(buf_ref, slot * PAIR_BLOCK + part * ROW_BLOCK, ROW_BLOCK), sem.at[slot]).wait()
    rows = _from_row_slabs(buf_ref, slot * PAIR_BLOCK, PAIR_BLOCK)
    gate = mod_ref[0, 0, 5:6, :]

    def finish(x_ref, slot_ref, wt_ref, xo_ref):
        slots, wts = slot_ref[...].astype(F32), wt_ref[...]
        slot1, slot2 = slots[0:1], slots[1:2]
        weight_of_row = jnp.sum(_slot_one_hot(slot1, slot2, wts[0:1], wts[1:2]), axis=1, keepdims=True)
        hi, lo = _split2(rows * weight_of_row)
        perm = _slot_one_hot(slot1, slot2, 1.0, 1.0).astype(BF16)
        y = _dot_tn(perm, hi) + _dot_tn(perm, lo)
        xo_ref[...] = x_ref[...] + gate * y

    @pl.when(j < n_ctx_tiles)
    def _():
        finish(x_c_ref, slot_c_ref, wt_c_ref, xo_c_ref)

    @pl.when(j >= n_ctx_tiles)
    def _():
        finish(x_l_ref, slot_l_ref, wt_l_ref, xo_l_ref)


def _combine_call(l, plan, x_c, x_l, slot_c, slot_l, wt_c, wt_l, mods_all, mod_row_of_tile, ys):
    n_tiles = (x_c.shape[0] + x_l.shape[0]) // ROW_BLOCK
    n_ctx_tiles = x_c.shape[0] // ROW_BLOCK
    first, second = _two_streams(n_ctx_tiles)

    def lanes(index_map):
        return lambda i, *_: index_map(i)[::-1]

    return pl.pallas_call(
        functools.partial(_combine_kernel, n_ctx_tiles),
        grid_spec=pltpu.PrefetchScalarGridSpec(
            num_scalar_prefetch=N_COMBINE_TABLES,
            grid=(n_tiles,),
            in_specs=[pl.BlockSpec((ROW_BLOCK, D_MODEL), first),
                      pl.BlockSpec((ROW_BLOCK, D_MODEL), second),
                      pl.BlockSpec((2, ROW_BLOCK), lanes(first)),
                      pl.BlockSpec((2, ROW_BLOCK), lanes(second)),
                      pl.BlockSpec((2, ROW_BLOCK), lanes(first)),
                      pl.BlockSpec((2, ROW_BLOCK), lanes(second)),
                      pl.BlockSpec((1, 1, 6, D_MODEL), lambda i, *_: (l, mod_row_of_tile(i), 0, 0)),
                      pl.BlockSpec(memory_space=pl.ANY)],
            out_specs=[pl.BlockSpec((ROW_BLOCK, D_MODEL), first),
                       pl.BlockSpec((ROW_BLOCK, D_MODEL), second)],
            scratch_shapes=[pltpu.VMEM((GATHER_SLOTS * PAIR_BLOCK * ROW_SLABS, LANES), F32),
                            pltpu.SemaphoreType.DMA((GATHER_SLOTS,))],
        ),
        out_shape=[jax.ShapeDtypeStruct(x_c.shape, F32), jax.ShapeDtypeStruct(x_l.shape, F32)],
        compiler_params=pltpu.CompilerParams(
            dimension_semantics=("arbitrary",), vmem_limit_bytes=40 * 1024 * 1024),
        name="moe_combine",
    )(*plan["combine_tables"], x_c, x_l, slot_c, slot_l, wt_c, wt_l, mods_all, ys)


def _moe_plan(cnt):
    n_tok_tiles = cnt.shape[0]
    n_tiles = n_tok_tiles * PAIR_BLOCK // ROW_BLOCK + N_EXPERTS
    lofs = jnp.cumsum(cnt, axis=1) - cnt
    cpre = jnp.cumsum(cnt, axis=0) - cnt
    counts = jnp.sum(cnt, axis=0)
    padded = (counts + ROW_BLOCK - 1) // ROW_BLOCK * ROW_BLOCK
    ends = jnp.cumsum(padded)
    starts = ends - padded
    tile_start = jnp.arange(n_tiles, dtype=jnp.int32) * ROW_BLOCK
    tile_expert = jnp.minimum(
        jnp.sum((tile_start[:, None] >= ends[None, :]).astype(jnp.int32), axis=1), N_EXPERTS - 1)
    hot = tile_expert[:, None] == jnp.arange(N_EXPERTS, dtype=jnp.int32)[None, :]
    first = tile_start - jnp.sum(jnp.where(hot, starts[None, :], 0), axis=1)
    rows = jnp.clip(jnp.sum(jnp.where(hot, counts[None, :], 0), axis=1) - first, 0, ROW_BLOCK)
    seg_first = jnp.sum(jnp.where(hot[:, None, :], cpre[None, :, :], 0), axis=2)
    seg_rows = jnp.sum(jnp.where(hot[:, None, :], cnt[None, :, :], 0), axis=2)
    overlap = (seg_first < (first + rows)[:, None]) & (seg_first + seg_rows > first[:, None])
    j = jnp.arange(n_tok_tiles, dtype=jnp.int32)[None, :]
    jlo = jnp.min(jnp.where(overlap, j, n_tok_tiles), axis=1)
    jhi = jnp.max(jnp.where(overlap, j + 1, 0), axis=1)
    i32 = lambda a: a.astype(jnp.int32).reshape(-1)
    return {
        "expert_tables": tuple(i32(a) for a in (tile_expert, first, rows, jlo, jhi, cpre, cnt, lofs)),
        "combine_tables": tuple(i32(a) for a in (cnt, cpre, lofs, starts)),
    }


def _rope_tables(n_tok):
    n_rows = n_tok // GRID_W
    pos_r = jnp.repeat(jnp.arange(n_rows), GRID_W)
    pos_c = jnp.tile(jnp.arange(GRID_W), n_rows)
    half = DIFF_QK // 2
    nf = half // 2
    freqs = ROPE_BASE ** (-jnp.arange(nf, dtype=F32) / nf)

    def tables(pos):
        ang = pos.astype(F32)[:, None] * freqs
        cos, sin = jnp.cos(ang), jnp.sin(ang)
        return jnp.concatenate([cos, cos], axis=-1), jnp.concatenate([-sin, sin], axis=-1)

    cos_r, sin_r = tables(pos_r)
    cos_c, sin_c = tables(pos_c)
    cos = jnp.concatenate([cos_r, cos_c], axis=-1)
    sin = jnp.concatenate([sin_r, sin_c], axis=-1)
    return jnp.concatenate([cos, cos], axis=-1), jnp.concatenate([sin, sin], axis=-1)


def _mixer_weights(w_in, w_out, sgu_w, sgu_b, q_norm_g, k_norm_g, diff_lambda, diff_norm_g, gla_w2, gla_b,
                   gla_norm_g, norm1_g, norm2_g, router_w, router_bias):
    w_in_pad = jnp.pad(w_in.astype(BF16), ((0, 0), (0, 0), (0, D_PROJ_PAD - w_in.shape[2])))
    w2cat = jnp.zeros((DEPTH, LANES, 2 * W_GLA), F32)
    w2cat = w2cat.at[:, 0:GLA_RANK, 0:W_GLA].set(gla_w2[:, 0]).at[:, GLA_RANK:2 * GLA_RANK, W_GLA:].set(gla_w2[:, 1])
    return (
        norm1_g[:, None, :], norm2_g[:, None, :], w_in_pad, w_out.astype(BF16),
        sgu_w.astype(BF16), jnp.repeat(sgu_b.transpose(0, 2, 1), SGU_GROUP_W, axis=2),
        jnp.tile(q_norm_g, (1, W_QK // DIFF_QK))[:, None, :], jnp.tile(k_norm_g, (1, W_QK // DIFF_QK))[:, None, :],
        diff_lambda, diff_norm_g[:, None, :],
        w2cat.astype(BF16), gla_b.reshape(DEPTH, 1, 2 * W_GLA), jnp.tile(gla_norm_g, (1, GLA_HEADS))[:, None, :],
        router_w.T, router_bias[:, None],
    )


def kernel(x_prompt, x_sample, cache_k, cache_v, state_gla, c, c_ctx, w_in, w_out, sgu_w, sgu_b, q_norm_g, k_norm_g,
           diff_lambda, diff_norm_g, gla_w2, gla_b, gla_norm_g, norm1_g, norm2_g, ada_w, ada_b, router_w, router_bias,
           moe_w1, moe_w3, moe_w2):
    n_ctx_seq, ctx_len, _ = x_prompt.shape
    n_lat_seq, lat_len, _ = x_sample.shape
    n_ctx_tok = n_ctx_seq * ctx_len
    n_lat_tok = n_lat_seq * lat_len
    ctx_tiles = n_ctx_tok // ROW_BLOCK
    lat_tiles_per_seq = lat_len // ROW_BLOCK

    n_cond = 1 + n_lat_seq
    cond_t = jnp.zeros((D_MODEL, SUBLANES), F32).at[:, 0].set(c_ctx).at[:, 1:n_cond].set(c.T)
    mods_all = _adaln_call(cond_t, n_cond, ada_w, ada_b)[:, :n_cond].reshape(DEPTH, n_cond, 6, D_MODEL)
    weights = _mixer_weights(w_in, w_out, sgu_w, sgu_b, q_norm_g, k_norm_g, diff_lambda, diff_norm_g, gla_w2, gla_b,
                             gla_norm_g, norm1_g, norm2_g, router_w, router_bias)

    ck_all = cache_k.transpose(0, 1, 2, 4, 3, 5).reshape(cache_k.shape[:3] + (cache_k.shape[4], DIFF_V))
    st_all = jnp.einsum('bldhkv,hg->bldhvgk', state_gla, jnp.eye(GLA_HEADS, dtype=F32)).reshape(
        n_lat_seq, DEPTH, 2, W_GLA, W_GLA)
    cos, sin = _rope_tables(lat_len)
    extras = (ck_all, cache_v, st_all, cos, sin)

    def mod_row_of_tile(i):
        return jnp.where(i < ctx_tiles, 0, 1 + (i - ctx_tiles) // lat_tiles_per_seq)

    x_c = x_prompt.reshape(n_ctx_tok, D_MODEL)
    x_l = x_sample.reshape(n_lat_tok, D_MODEL)
    cache_bufs = ()
    for l in range(DEPTH):
        ctx_par = 1 if l == 0 else CTX_SEQS_PER_STEP
        x1_c, hs_c, slot_c, wt_c, cnt_c, *cache_bufs = _mixer_call(
            l, ctx_len, ctx_par, False, x_c, mods_all, weights, None, tuple(cache_bufs))
        x1_l, hs_l, slot_l, wt_l, cnt_l = _mixer_call(l, lat_len, 1, True, x_l, mods_all, weights, extras, ())
        plan = _moe_plan(jnp.concatenate([cnt_c[:, :, 0], cnt_l[:, :, 0]], axis=0))
        ys = _expert_call(l, plan, hs_c, hs_l, moe_w1, moe_w3, moe_w2)
        x_c, x_l = _combine_call(l, plan, x1_c, x1_l, slot_c, slot_l, wt_c, wt_l, mods_all, mod_row_of_tile, ys)

    new_k, new_v, new_s = cache_bufs
    return (x_c.reshape(x_prompt.shape), x_l.reshape(x_sample.shape), new_k, new_v, new_s)
```

```python
import functools
import math

import jax
import jax.numpy as jnp
from jax import lax
from jax.experimental import pallas as pl
from jax.experimental.pallas import tpu as pltpu

F32 = jnp.float32
BF16 = jnp.bfloat16

D_MODEL = 1024
DEPTH = 4
GRID_W = 64
SGU_GROUPS = 4
SGU_GROUP_W = 64
SGU_W = SGU_GROUPS * SGU_GROUP_W
SGU_CHUNK = 128
DIFF_HEADS = 4
DIFF_QK = 64
DIFF_V = 2 * DIFF_QK
ROPE_BASE = 10000.0
GLA_HEADS = 4
GLA_DK = 64
GLA_DV = 64
GLA_RANK = 16
GLA_GATE_NORM = 16.0
GLA_CHUNK = 64
N_EXPERTS = 16
N_GROUPS = 4
EXPERTS_PER_GROUP = N_EXPERTS // N_GROUPS
D_EXPERT = 512
EPS = 1e-6

LANES = 128
SUBLANES = 8
MXU_DIM = 256

ROW_BLOCK = MXU_DIM
ROW_SLABS = D_MODEL // LANES

C_AU, C_AV, C_BQ, C_BK, C_BV = 0, 256, 512, 1024, 1536
C_CQ, C_CK, C_CV, C_CR, C_LR = 2048, 2304, 2560, 2816, 3072
D_PROJ_MAIN = 3072
D_PROJ_PAD = D_PROJ_MAIN + LANES
W_QK = DIFF_HEADS * 2 * DIFF_QK
W_GLA = GLA_HEADS * GLA_DK
M_A, M_B, M_C = 0, SGU_W, SGU_W + DIFF_HEADS * DIFF_V


def _split2(x):
    hi = x.astype(BF16)
    lo = (x - hi.astype(F32)).astype(BF16)
    return hi, lo


def _split3(x):
    hi = x.astype(BF16)
    r = x - hi.astype(F32)
    mid = r.astype(BF16)
    lo = (r - mid.astype(F32)).astype(BF16)
    return hi, mid, lo


def _dot(a, b):
    return jnp.dot(a, b, preferred_element_type=F32)


def _dot_nt(a, b):
    return lax.dot_general(a, b, (((1,), (1,)), ((), ())), preferred_element_type=F32)


def _dot_tn(a, b):
    return lax.dot_general(a, b, (((0,), (0,)), ((), ())), preferred_element_type=F32)


def _iota(shape, dim):
    return lax.broadcasted_iota(jnp.int32, shape, dim)


def _block_ones(width, block):
    r = _iota((width, width), 0) // block
    c = _iota((width, width), 1) // block
    return (r == c)


def _group_sum(z, block):
    width = z.shape[-1]
    outs = []
    for s in range(0, width, MXU_DIM):
        w = min(MXU_DIM, width - s)
        ones = _block_ones(w, block).astype(BF16)
        hi, lo = _split2(z[:, s:s + w])
        outs.append(_dot(hi, ones) + _dot(lo, ones))
    return outs[0] if len(outs) == 1 else jnp.concatenate(outs, axis=-1)


def _group_rms(z, block):
    ms = _group_sum(z * z, block) * (1.0 / block)
    return z * lax.rsqrt(ms + EPS)


def _row_rms(z):
    return z * lax.rsqrt(jnp.mean(z * z, axis=-1, keepdims=True) + EPS)


def _log_sigmoid(x):
    return jnp.minimum(x, 0.0) - jnp.log1p(jnp.exp(-jnp.abs(x)))


ADA_COLS = 1536


def _adaln_kernel(n_cond, cond_t_ref, w_ref, b_ref, o_ref):
    sc = jax.nn.silu(cond_t_ref[...])
    w = w_ref[0]
    rows = [jnp.sum(sc[:, r:r + 1] * w, axis=0, keepdims=True) + b_ref[0] for r in range(n_cond)]
    o_ref[0] = jnp.concatenate(rows + [jnp.zeros((SUBLANES - n_cond, w.shape[1]), F32)], axis=0)


def _adaln_call(cond_t, n_cond, ada_w, ada_b):
    n_col = 6 * D_MODEL // ADA_COLS
    return pl.pallas_call(
        functools.partial(_adaln_kernel, n_cond),
        grid=(DEPTH, n_col),
        in_specs=[
            pl.BlockSpec((D_MODEL, SUBLANES), lambda l, j: (0, 0)),
            pl.BlockSpec((1, D_MODEL, ADA_COLS), lambda l, j: (l, 0, j)),
            pl.BlockSpec((1, 1, ADA_COLS), lambda l, j: (l, 0, j)),
        ],
        out_specs=pl.BlockSpec((1, SUBLANES, ADA_COLS), lambda l, j: (l, 0, j)),
        out_shape=jax.ShapeDtypeStruct((DEPTH, SUBLANES, 6 * D_MODEL), F32),
        compiler_params=pltpu.CompilerParams(
            dimension_semantics=("arbitrary", "arbitrary"), vmem_limit_bytes=40 * 1024 * 1024),
        name="adaln",
    )(cond_t, ada_w, ada_b.reshape(DEPTH, 1, 6 * D_MODEL))


def _route(hn, rwt_ref, rb_ref):
    h_hi, h_lo = _split2(hn)
    rw = rwt_ref[...]
    rw_hi = rw.astype(BF16)
    rw_lo = (rw - rw_hi.astype(F32)).astype(BF16)
    logits = _dot_nt(rw_hi, h_hi) + _dot_nt(rw_hi, h_lo) + _dot_nt(rw_lo, h_hi)
    aff = jax.nn.sigmoid(logits)
    sel = aff + rb_ref[...]
    n_tok = sel.shape[1]

    def top2_sum(a, b, c, d):
        hi1, lo1 = jnp.maximum(a, b), jnp.minimum(a, b)
        hi2, lo2 = jnp.maximum(c, d), jnp.minimum(c, d)
        return jnp.maximum(hi1, hi2) + jnp.maximum(jnp.minimum(hi1, hi2), jnp.maximum(lo1, lo2))

    scores = []
    for g in range(N_GROUPS):
        rows = [sel[EXPERTS_PER_GROUP * g + j:EXPERTS_PER_GROUP * g + j + 1, :] for j in range(EXPERTS_PER_GROUP)]
        scores.append(top2_sum(*rows))
    best = jnp.zeros((1, n_tok), jnp.int32)
    best_score = scores[0]
    for g in range(1, N_GROUPS):
        upd = scores[g] > best_score
        best = jnp.where(upd, g, best)
        best_score = jnp.where(upd, scores[g], best_score)

    eid_i = _iota((N_EXPERTS, n_tok), 0)
    eid = eid_i.astype(F32)
    neg = jnp.float32(-jnp.inf)
    msel = jnp.where(eid_i // EXPERTS_PER_GROUP == best, sel, neg)
    m1 = jnp.max(msel, axis=0, keepdims=True)
    idx1 = jnp.min(jnp.where(msel == m1, eid, float(N_EXPERTS)), axis=0, keepdims=True)
    msel2 = jnp.where(eid == idx1, neg, msel)
    m2 = jnp.max(msel2, axis=0, keepdims=True)
    idx2 = jnp.min(jnp.where(msel2 == m2, eid, float(N_EXPERTS)), axis=0, keepdims=True)
    w1 = jnp.sum(jnp.where(eid == idx1, aff, 0.0), axis=0, keepdims=True)
    w2 = jnp.sum(jnp.where(eid == idx2, aff, 0.0), axis=0, keepdims=True)
    wsum = w1 + w2
    return idx1.astype(jnp.int32), idx2.astype(jnp.int32), w1 / wsum, w2 / wsum


def _local_slots(idx1, idx2):
    n_tok = idx1.shape[1]
    eid = _iota((N_EXPERTS, n_tok), 0)
    hot1, hot2 = eid == idx1, eid == idx2
    hot = jnp.where(hot1, 1.0, jnp.where(hot2, 1.0, 0.0))
    earlier = jnp.where(_iota((n_tok, n_tok), 0) < _iota((n_tok, n_tok), 1), 1.0, 0.0).astype(BF16)
    before_in_expert = _dot(hot.astype(BF16), earlier)
    counts = jnp.sum(hot, axis=1, keepdims=True)
    lower = jnp.where(_iota((N_EXPERTS, N_EXPERTS), 1) < _iota((N_EXPERTS, N_EXPERTS), 0), 1.0, 0.0).astype(BF16)
    first_slot = _dot(lower, jnp.broadcast_to(counts, (N_EXPERTS, LANES)).astype(BF16))[:, 0:1]
    slot = before_in_expert + first_slot
    slot1 = jnp.sum(jnp.where(hot1, slot, 0.0), axis=0, keepdims=True)
    slot2 = jnp.sum(jnp.where(hot2, slot, 0.0), axis=0, keepdims=True)
    return slot1, slot2, counts


def _slot_one_hot(slot1, slot2, v1, v2):
    n_tok = slot1.shape[1]
    row = _iota((2 * n_tok, n_tok), 0).astype(F32)
    return jnp.where(row == slot1, v1, jnp.where(row == slot2, v2, 0.0))


N_MIXER_WEIGHTS = 15
CTX_SEQS_PER_STEP = 2
MAX_INLINE_BLOCKS = 2


def _mixer_kernel(n_tok, n_par, latent, n_alias, lam_init, *refs):
    it = iter(refs)
    x_ref, mod_ref = next(it), next(it)
    (n1_ref, n2_ref, win_ref, wout_ref, sw_ref, sb_ref, qg_ref, kg_ref, dl_ref, dg_ref,
     w2c_ref, gb_ref, gg_ref, rwt_ref, rb_ref) = (next(it) for _ in range(N_MIXER_WEIGHTS))
    if latent:
        ck_ref, cv_ref, st0_ref, cos_ref, sin_ref = (next(it) for _ in range(5))
    for _ in range(n_alias):
        next(it)
    xo_ref, hs_ref, slot_ref, wt_ref, cnt_ref = (next(it) for _ in range(5))
    if not latent:
        ko_ref, vo_ref, so_ref = (next(it) for _ in range(3))
    proj_ref, mix_ref, q_ref, k_ref, v_ref = (next(it) for _ in range(5))
    gq_ref, gke_ref, gv_ref, gr_ref, dec_ref, go_ref, st_ref = (next(it) for _ in range(7))

    n_blk = n_tok // ROW_BLOCK
    n_ctx = k_ref.shape[0] - n_par * n_tok
    n_keys = n_ctx + n_tok
    mod = mod_ref[0, 0]

    def blocks(body):
        if n_par * n_blk <= MAX_INLINE_BLOCKS:
            for r in range(n_par * n_blk):
                body(r)
        else:
            def step(r, carry):
                body(r)
                return carry
            lax.fori_loop(0, n_par * n_blk, step, 0)

    def aligned(start, size):
        return pl.ds(start if isinstance(start, int) else pl.multiple_of(start, size), size)

    def block_rows(r, offset=0):
        return aligned(offset + r * ROW_BLOCK, ROW_BLOCK)

    if not latent:
        for ref in (ko_ref, vo_ref, so_ref):
            for q in range(n_par):
                for other in range(1, ref.shape[1]):
                    ref[q, other] = jnp.zeros(ref.shape[2:], F32)

    lane_group = _iota((SGU_CHUNK, SGU_W), 1) // SGU_GROUP_W
    blk_r = _iota((ROW_BLOCK, ROW_BLOCK), 0)
    blk_c = _iota((ROW_BLOCK, ROW_BLOCK), 1)
    same_chunk = (blk_r // GLA_CHUNK) == (blk_c // GLA_CHUNK)
    tri = (jnp.where(same_chunk & (blk_c <= blk_r), 1.0, 0.0).astype(BF16),
           jnp.where(same_chunk & (blk_c >= blk_r), 1.0, 0.0).astype(BF16))
    chunks_per_blk = ROW_BLOCK // GLA_CHUNK
    head_of_lane = _iota((GLA_CHUNK, W_GLA), 1) // GLA_DK
    stack_r = _iota((GLA_HEADS * GLA_CHUNK, GLA_CHUNK), 0) % GLA_CHUNK
    stack_c = _iota((GLA_HEADS * GLA_CHUNK, GLA_CHUNK), 1)
    causal = (stack_c <= stack_r, stack_c >= stack_r)

    if latent:
        for h in range(DIFF_HEADS):
            k_ref[0:n_ctx, h * DIFF_V:(h + 1) * DIFF_V] = ck_ref[0, 0, h].astype(BF16)
            v_ref[0:n_ctx, h * DIFF_V:(h + 1) * DIFF_V] = cv_ref[0, 0, h].astype(BF16)
        pair_lo = (_iota((ROW_BLOCK, W_QK), 1) % (DIFF_QK // 2)) < (DIFF_QK // 4)

        def rope(z, rows):
            cos = jnp.concatenate([cos_ref[rows, :]] * DIFF_HEADS, axis=-1)
            sin = jnp.concatenate([sin_ref[rows, :]] * DIFF_HEADS, axis=-1)
            shift = DIFF_QK // 4
            swapped = jnp.where(pair_lo, pltpu.roll(z, W_QK - shift, 1), pltpu.roll(z, shift, 1))
            return z * cos + swapped * sin

    def modulated_input(r):
        h = _row_rms(x_ref[block_rows(r), :]) * n1_ref[0]
        return (h * (1.0 + mod[1:2, :]) + mod[0:1, :]).astype(BF16)

    def spatial_gating(r):
        for c in range(ROW_BLOCK // SGU_CHUNK):
            local = slice(c * SGU_CHUNK, (c + 1) * SGU_CHUNK)
            u = jax.nn.gelu(proj_ref[local, C_AU:C_AU + SGU_W])
            v = _group_rms(jax.nn.gelu(proj_ref[local, C_AV:C_AV + SGU_W]), SGU_GROUP_W).astype(BF16)
            s = sb_ref[0]
            for g in range(SGU_GROUPS):
                s = s + jnp.where(lane_group == g, _dot(sw_ref[0, g], v), 0.0)
            mix_ref[aligned(r * ROW_BLOCK + c * SGU_CHUNK, SGU_CHUNK), M_A:M_A + SGU_W] = (u * s).astype(BF16)

    def attention_operands(r):
        rows = block_rows(r)
        key_rows = block_rows(r, n_ctx)
        seq, seq_rows = r // n_blk, block_rows(r % n_blk)
        qn = _group_rms(proj_ref[:, C_BQ:C_BQ + W_QK], DIFF_QK) * qg_ref[0]
        kn = _group_rms(proj_ref[:, C_BK:C_BK + W_QK], DIFF_QK) * kg_ref[0]
        vv = proj_ref[:, C_BV:C_BV + W_QK]
        if latent:
            qn, kn = rope(qn, rows), rope(kn, rows)
        else:
            for h in range(DIFF_HEADS):
                for i in range(2):
                    lo = h * DIFF_V + i * DIFF_QK
                    ko_ref[seq, 0, h, i, seq_rows, :] = kn[:, lo:lo + DIFF_QK]
                vo_ref[seq, 0, h, seq_rows, :] = vv[:, h * DIFF_V:(h + 1) * DIFF_V]
        q_ref[rows, :] = (qn * (DIFF_QK ** -0.5)).astype(BF16)
        k_ref[key_rows, :] = kn.astype(BF16)
        v_ref[key_rows, :] = vv.astype(BF16)

    def gla_operands(r):
        rows = block_rows(r)
        gpre = _dot(proj_ref[:, C_LR:C_LR + LANES].astype(BF16), w2c_ref[0]) + gb_ref[0]
        gate = _log_sigmoid(gpre) * (1.0 / GLA_GATE_NORM)
        gq = proj_ref[:, C_CQ:C_CQ + W_GLA] * (GLA_DK ** -0.5)
        gk = proj_ref[:, C_CK:C_CK + W_GLA]
        gv = proj_ref[:, C_CV:C_CV + W_GLA].astype(BF16)
        gv_ref[rows, :] = gv
        gr_ref[rows, :] = proj_ref[:, C_CR:C_CR + W_GLA]
        for d in range(2):
            g = gate[:, d * W_GLA:(d + 1) * W_GLA]
            b = sum(_dot(tri[d], p) for p in _split3(g))
            last = GLA_CHUNK - 1 if d == 0 else 0
            b_last = jnp.concatenate(
                [jnp.broadcast_to(b[c * GLA_CHUNK + last:c * GLA_CHUNK + last + 1, :], (GLA_CHUNK, W_GLA))
                 for c in range(chunks_per_blk)], axis=0)
            q_dec = (gq * jnp.exp(b)).astype(BF16)
            k_inv = (gk * jnp.exp(-b)).astype(BF16)
            gq_ref[d, rows, :] = q_dec
            gke_ref[d, rows, :] = (gk * jnp.exp(b_last - b)).astype(BF16)
            for c in range(chunks_per_blk):
                row = c * GLA_CHUNK + last
                dec_ref[d, r * chunks_per_blk + c] = jnp.exp(b[row:row + 1, :])
                chunk = slice(c * GLA_CHUNK, (c + 1) * GLA_CHUNK)
                qd = q_dec[chunk]
                q_stack = jnp.concatenate(
                    [jnp.where(head_of_lane == h, qd, jnp.zeros_like(qd)) for h in range(GLA_HEADS)], axis=0)
                attn = jnp.where(causal[d], _dot_nt(q_stack, k_inv[chunk]), 0.0)
                spread = _dot(attn.astype(BF16), gv[chunk])
                o = jnp.zeros((GLA_CHUNK, W_GLA), F32)
                for h in range(GLA_HEADS):
                    o = o + jnp.where(head_of_lane == h, spread[h * GLA_CHUNK:(h + 1) * GLA_CHUNK, :], 0.0)
                go_ref[d, aligned(r * ROW_BLOCK + c * GLA_CHUNK, GLA_CHUNK), :] = o

    def project_and_split(r):
        proj_ref[...] = _dot(modulated_input(r), win_ref[0])
        spatial_gating(r)
        attention_operands(r)
        gla_operands(r)

    blocks(project_and_split)

    dl = dl_ref[0]
    lam = (jnp.exp(jnp.sum(dl[0:1] * dl[1:2], axis=-1, keepdims=True))
           - jnp.exp(jnp.sum(dl[2:3] * dl[3:4], axis=-1, keepdims=True)) + lam_init)
    sub0 = (_iota((ROW_BLOCK, DIFF_V), 1) < DIFF_QK)

    def softmax(s):
        e = jnp.exp(s - jnp.max(s, axis=-1, keepdims=True))
        return e, jnp.sum(e, axis=-1, keepdims=True)

    def attn_block(r):
        rows = block_rows(r)
        keys = aligned((r // n_blk) * n_keys, n_keys)
        for h in range(DIFF_HEADS):
            cols = slice(h * DIFF_V, (h + 1) * DIFF_V)
            qh = q_ref[rows, cols]
            kh = k_ref[keys, cols]
            e0, z0 = softmax(_dot_nt(jnp.where(sub0, qh, jnp.zeros_like(qh)), kh))
            e1, z1 = softmax(_dot_nt(jnp.where(sub0, jnp.zeros_like(qh), qh), kh))
            w = e0 / z0 - lam * (e1 / z1)
            o = _dot(w.astype(BF16), v_ref[keys, cols])
            o = _row_rms(o) * dg_ref[0] * (1.0 - lam_init)
            mix_ref[rows, M_B + h * DIFF_V:M_B + (h + 1) * DIFF_V] = o.astype(BF16)

    blocks(attn_block)

    if latent:
        st_ref[0] = st0_ref[0, 0]
    else:
        st_ref[...] = jnp.zeros(st_ref.shape, F32)

    n_chunk = n_tok // GLA_CHUNK
    st_diag = (_iota((W_GLA, W_GLA), 0) // GLA_DV) == (_iota((W_GLA, W_GLA), 1) // GLA_DK)

    def gla_step(c, carry):
        for seq in range(n_par):
            for d in range(2):
                cc = seq * n_chunk + (c if d == 0 else n_chunk - 1 - c)
                rows = pl.ds(pl.multiple_of(cc * GLA_CHUNK, GLA_CHUNK), GLA_CHUNK)
                st = st_ref[seq, d]
                go_ref[d, rows, :] = go_ref[d, rows, :] + _dot_nt(gq_ref[d, rows, :], st.astype(BF16))
                upd = _dot_tn(gv_ref[rows, :], gke_ref[d, rows, :])
                st_ref[seq, d] = dec_ref[d, cc] * st + jnp.where(st_diag, upd, 0.0)
        return carry

    lax.fori_loop(0, n_chunk, gla_step, 0)

    if not latent:
        for seq in range(n_par):
            for d in range(2):
                s_full = st_ref[seq, d].T
                for h in range(GLA_HEADS):
                    so_ref[seq, 0, d, h] = s_full[h * GLA_DK:(h + 1) * GLA_DK, h * GLA_DV:(h + 1) * GLA_DV]

    for r in range(n_par * n_blk):
        rows = pl.ds(r * ROW_BLOCK, ROW_BLOCK)
        oc = _group_rms(go_ref[0, rows, :] + go_ref[1, rows, :], GLA_DV) * gg_ref[0]
        oc = oc * jax.nn.silu(gr_ref[rows, :])
        mix_ref[rows, M_C:M_C + W_GLA] = oc.astype(BF16)
        x1 = x_ref[rows, :] + mod[2:3, :] * _dot(mix_ref[rows, :], wout_ref[0])
        xo_ref[rows, :] = x1
        hn = _row_rms(x1) * n2_ref[0]
        hn = hn * (1.0 + mod[4:5, :]) + mod[3:4, :]
        idx1, idx2, w1, w2 = _route(hn, rwt_ref, rb_ref)
        slot1, slot2, counts = _local_slots(idx1, idx2)
        perm = _slot_one_hot(slot1, slot2, 1.0, 1.0).astype(BF16)
        _to_row_slabs(hs_ref, 2 * r * ROW_BLOCK, _dot(perm, hn.astype(BF16)))
        slot_ref[:, rows] = jnp.concatenate([slot1, slot2], axis=0).astype(jnp.int32)
        wt_ref[:, rows] = jnp.concatenate([w1, w2], axis=0)
        cnt_ref[r] = jnp.broadcast_to(counts, (N_EXPERTS, LANES)).astype(jnp.int32)


def _mixer_call(l, n_tok, n_par, latent, x, mods_all, weights, extras, cache_bufs):
    n_seq = x.shape[0] // n_tok
    n_all = x.shape[0]
    assert n_seq % n_par == 0 and not (latent and n_par > 1)
    n_step_tok = n_par * n_tok
    n_keys = n_step_tok + (extras[0].shape[3] if latent else 0)
    n_chunk = n_step_tok // GLA_CHUNK
    lam_init = 0.8 - 0.6 * math.exp(-0.3 * l)

    single = pl.Buffered(1)
    seq_mode = single if latent else None

    def layer(arr):
        tail = arr.shape[1:]
        return pl.BlockSpec((1,) + tail, lambda s, _n=len(tail): (l,) + (0,) * _n, pipeline_mode=single)

    def const(arr):
        return pl.BlockSpec(arr.shape, lambda s, _n=arr.ndim: (0,) * _n, pipeline_mode=single)

    def tok_spec(width):
        return pl.BlockSpec((n_step_tok, width), lambda s: (s, 0), pipeline_mode=seq_mode)

    mod_row = (lambda s: 1 + s) if latent else (lambda s: 0)
    in_specs = [tok_spec(D_MODEL),
                pl.BlockSpec((1, 1, 6, D_MODEL), lambda s: (l, mod_row(s), 0, 0))]
    in_specs += [layer(w) for w in weights[:N_MIXER_WEIGHTS - 2]] + [const(w) for w in weights[-2:]]
    operands = [x, mods_all] + list(weights)
    if latent:
        ck, cv, st0, cos, sin = extras
        in_specs += [
            pl.BlockSpec((1, 1) + ck.shape[2:], lambda s: (s, l, 0, 0, 0)),
            pl.BlockSpec((1, 1) + cv.shape[2:], lambda s: (s, l, 0, 0, 0)),
            pl.BlockSpec((1, 1) + st0.shape[2:], lambda s: (s, l, 0, 0, 0)),
            const(cos), const(sin),
        ]
        operands += [ck, cv, st0, cos, sin]
    n_in = len(operands)
    in_specs += [pl.BlockSpec(memory_space=pl.ANY)] * len(cache_bufs)
    operands += list(cache_bufs)

    tiles_per_step = n_step_tok // ROW_BLOCK
    out_shape = [
        jax.ShapeDtypeStruct((n_all, D_MODEL), F32),
        jax.ShapeDtypeStruct((2 * n_all * ROW_SLABS, LANES), F32),
        jax.ShapeDtypeStruct((2, n_all), jnp.int32),
        jax.ShapeDtypeStruct((2, n_all), F32),
        jax.ShapeDtypeStruct((n_all // ROW_BLOCK, N_EXPERTS, LANES), jnp.int32),
    ]
    out_specs = [
        tok_spec(D_MODEL),
        pl.BlockSpec((2 * n_step_tok * ROW_SLABS, LANES), lambda s: (s, 0), pipeline_mode=seq_mode),
        pl.BlockSpec((2, n_step_tok), lambda s: (0, s)),
        pl.BlockSpec((2, n_step_tok), lambda s: (0, s)),
        pl.BlockSpec((tiles_per_step, N_EXPERTS, LANES), lambda s: (s, 0, 0)),
    ]
    n_shared_out = len(out_shape)
    aliases = {}
    if not latent:
        out_shape += [
            jax.ShapeDtypeStruct((n_seq, DEPTH, DIFF_HEADS, 2, n_tok, DIFF_QK), F32),
            jax.ShapeDtypeStruct((n_seq, DEPTH, DIFF_HEADS, n_tok, DIFF_V), F32),
            jax.ShapeDtypeStruct((n_seq, DEPTH, 2, GLA_HEADS, GLA_DK, GLA_DV), F32),
        ]
        n_lay, at = (1, l) if cache_bufs else (DEPTH, 0)
        out_specs += [
            pl.BlockSpec((n_par, n_lay, DIFF_HEADS, 2, n_tok, DIFF_QK), lambda s: (s, at, 0, 0, 0, 0)),
            pl.BlockSpec((n_par, n_lay, DIFF_HEADS, n_tok, DIFF_V), lambda s: (s, at, 0, 0, 0)),
            pl.BlockSpec((n_par, n_lay, 2, GLA_HEADS, GLA_DK, GLA_DV), lambda s: (s, at, 0, 0, 0, 0)),
        ]
        aliases = {n_in + j: n_shared_out + j for j in range(len(cache_bufs))}
    scratch = [
        pltpu.VMEM((ROW_BLOCK, D_PROJ_PAD), F32),
        pltpu.VMEM((n_step_tok, D_MODEL), BF16),
        pltpu.VMEM((n_step_tok, W_QK), BF16),
        pltpu.VMEM((n_keys, W_QK), BF16),
        pltpu.VMEM((n_keys, W_QK), BF16),
        pltpu.VMEM((2, n_step_tok, W_GLA), BF16),
        pltpu.VMEM((2, n_step_tok, W_GLA), BF16),
        pltpu.VMEM((n_step_tok, W_GLA), BF16),
        pltpu.VMEM((n_step_tok, W_GLA), F32),
        pltpu.VMEM((2, n_chunk, 1, W_GLA), F32),
        pltpu.VMEM((2, n_step_tok, W_GLA), F32),
        pltpu.VMEM((n_par, 2, W_GLA, W_GLA), F32),
    ]
    return pl.pallas_call(
        functools.partial(_mixer_kernel, n_tok, n_par, latent, len(cache_bufs), lam_init),
        grid=(n_seq // n_par,),
        in_specs=in_specs,
        out_specs=out_specs,
        out_shape=out_shape,
        scratch_shapes=scratch,
        input_output_aliases=aliases,
        compiler_params=pltpu.CompilerParams(
            dimension_semantics=("arbitrary",), vmem_limit_bytes=56 * 1024 * 1024),
        name="mixer_latent" if latent else "mixer_context",
    )(*operands)


PAIR_BLOCK = 2 * ROW_BLOCK
COPY_SIZES = tuple(ROW_BLOCK >> k for k in range(ROW_BLOCK.bit_length()))
LARGE_COPY = 64
GATHER_AHEAD = 2
GATHER_SLOTS = GATHER_AHEAD + 1


def _segment_copies(n_rows, make_copy, act):
    def copy_if_set(size):
        @pl.when((n_rows & size) != 0)
        def _():
            act(make_copy(n_rows & (-2 * size), size))

    n_large = COPY_SIZES.index(LARGE_COPY) + 1

    @pl.when(n_rows >= LARGE_COPY)
    def _():
        for size in COPY_SIZES[:n_large]:
            copy_if_set(size)

    for size in COPY_SIZES[n_large:]:
        copy_if_set(size)


def _start(copy):
    copy.start()


def _wait(copy):
    copy.wait()


def _slab_rows(first_row, n_rows, slab):
    return pl.ds(first_row * ROW_SLABS + slab, n_rows, stride=ROW_SLABS)


def _to_row_slabs(ref, first_row, value):
    for s in range(ROW_SLABS):
        ref[_slab_rows(first_row, value.shape[0], s), :] = value[:, s * LANES:(s + 1) * LANES]


def _from_row_slabs(ref, first_row, n_rows):
    return jnp.concatenate([ref[_slab_rows(first_row, n_rows, s), :] for s in range(ROW_SLABS)], axis=-1)


def _row_span(ref, first_row, n_rows):
    return ref.at[pl.ds(pl.multiple_of(first_row * ROW_SLABS, ROW_SLABS), n_rows * ROW_SLABS)]


def _two_streams(n_first_tiles):
    def first(i, *_):
        return (jnp.minimum(i, n_first_tiles - 1), 0)

    def second(i, *_):
        return (jnp.maximum(i - n_first_tiles, 0), 0)

    return first, second


N_EXPERT_TABLES = 10
OUT_SLOTS = 2


def _expert_kernel(n_ctx_tiles, n_tiles_max, te_ref, first_ref, rows_ref, jlo_ref, jhi_ref, cpre_ref, cnt_ref,
                   lofs_ref, tile0_ref, ntile_ref, hs_c_ref, hs_l_ref, w1_ref, w3_ref, w2_ref, ys_ref,
                   xbuf_ref, obuf_ref, w1b_ref, w3b_ref, w2b_ref, sem, out_sem):
    expert = pl.program_id(0)
    n_tiles = tile0_ref[N_EXPERTS]

    def gather(t, act):
        slot = t % GATHER_SLOTS
        e, first = te_ref[t], first_ref[t]
        last = first + rows_ref[t]

        def segment_of(hs_ref, first_tile):
            def body(j, carry):
                k = j * N_EXPERTS + e
                seg_first = cpre_ref[k]
                lo = jnp.maximum(seg_first, first)
                n = jnp.maximum(jnp.minimum(seg_first + cnt_ref[k], last) - lo, 0)
                src = (j - first_tile) * PAIR_BLOCK + lofs_ref[k] + (lo - seg_first)
                dst = slot * ROW_BLOCK + lo - first
                _segment_copies(n, lambda done, size: pltpu.make_async_copy(
                    _row_span(hs_ref, src + done, size), _row_span(xbuf_ref, dst + done, size), sem.at[slot]), act)
                return carry
            return body

        jlo, jhi = jlo_ref[t], jhi_ref[t]
        lax.fori_loop(jnp.minimum(jlo, n_ctx_tiles), jnp.minimum(jhi, n_ctx_tiles), segment_of(hs_c_ref, 0), 0)
        lax.fori_loop(jnp.maximum(jlo, n_ctx_tiles), jnp.maximum(jhi, n_ctx_tiles),
                      segment_of(hs_l_ref, n_ctx_tiles), 0)

    def out_copy(t, oslot):
        return pltpu.make_async_copy(
            _row_span(obuf_ref, oslot * ROW_BLOCK, ROW_BLOCK), _row_span(ys_ref, t * ROW_BLOCK, ROW_BLOCK),
            out_sem.at[oslot])

    @pl.when(expert == 0)
    def _():
        xbuf_ref[...] = jnp.zeros(xbuf_ref.shape, F32)
        for t in range(GATHER_AHEAD):
            gather(t, _start)

    w1b_ref[...] = w1_ref[0, 0].astype(BF16)
    w3b_ref[...] = w3_ref[0, 0].astype(BF16)
    w2b_ref[...] = w2_ref[0, 0].astype(BF16)
    tile0, n_own = tile0_ref[expert], ntile_ref[expert]

    def tile_body(k, carry):
        t = tile0 + k
        slot, oslot = t % GATHER_SLOTS, k % OUT_SLOTS

        @pl.when(t + GATHER_AHEAD < n_tiles)
        def _():
            gather(t + GATHER_AHEAD, _start)

        n_rows = rows_ref[t]
        _segment_copies(n_rows, lambda done, size: pltpu.make_async_copy(
            _row_span(hs_c_ref, done, size), _row_span(xbuf_ref, slot * ROW_BLOCK + done, size), sem.at[slot]), _wait)

        @pl.when(k >= OUT_SLOTS)
        def _():
            out_copy(t, oslot).wait()

        live = _iota((ROW_BLOCK, D_MODEL), 0) < n_rows
        x = jnp.where(live, _from_row_slabs(xbuf_ref, slot * ROW_BLOCK, ROW_BLOCK), 0.0).astype(BF16)
        hid = jax.nn.silu(_dot(x, w1b_ref[...])) * _dot(x, w3b_ref[...])
        _to_row_slabs(obuf_ref, oslot * ROW_BLOCK, _dot(hid.astype(BF16), w2b_ref[...]))
        out_copy(t, oslot).start()
        return carry

    lax.fori_loop(0, n_own, tile_body, 0)
    for oslot in range(OUT_SLOTS):
        @pl.when(n_own > oslot)
        def _():
            out_copy(tile0, oslot).wait()

    @pl.when(expert == N_EXPERTS - 1)
    def _():
        obuf_ref[...] = jnp.zeros(obuf_ref.shape, F32)

        def fill(t, carry):
            out_copy(t, 0).start()
            out_copy(t, 0).wait()
            return carry

        lax.fori_loop(n_tiles, n_tiles_max, fill, 0)


def _expert_call(l, plan, hs_c, hs_l, w1, w3, w2):
    tables = plan["expert_tables"]
    n_tiles_max = tables[0].shape[0]
    n_ctx_tiles = hs_c.shape[0] // (PAIR_BLOCK * ROW_SLABS)

    def weight(shape):
        return pl.BlockSpec((1, 1) + shape, lambda e, *_: (l, e, 0, 0))

    return pl.pallas_call(
        functools.partial(_expert_kernel, n_ctx_tiles, n_tiles_max),
        grid_spec=pltpu.PrefetchScalarGridSpec(
            num_scalar_prefetch=N_EXPERT_TABLES,
            grid=(N_EXPERTS,),
            in_specs=[pl.BlockSpec(memory_space=pl.ANY), pl.BlockSpec(memory_space=pl.ANY),
                      weight((D_MODEL, D_EXPERT)), weight((D_MODEL, D_EXPERT)), weight((D_EXPERT, D_MODEL))],
            out_specs=pl.BlockSpec(memory_space=pl.ANY),
            scratch_shapes=[pltpu.VMEM((GATHER_SLOTS * ROW_BLOCK * ROW_SLABS, LANES), F32),
                            pltpu.VMEM((OUT_SLOTS * ROW_BLOCK * ROW_SLABS, LANES), F32),
                            pltpu.VMEM((D_MODEL, D_EXPERT), BF16), pltpu.VMEM((D_MODEL, D_EXPERT), BF16),
                            pltpu.VMEM((D_EXPERT, D_MODEL), BF16),
                            pltpu.SemaphoreType.DMA((GATHER_SLOTS,)), pltpu.SemaphoreType.DMA((OUT_SLOTS,))],
        ),
        out_shape=jax.ShapeDtypeStruct((n_tiles_max * ROW_BLOCK * ROW_SLABS, LANES), F32),
        compiler_params=pltpu.CompilerParams(
            dimension_semantics=("arbitrary",), vmem_limit_bytes=40 * 1024 * 1024),
        name="moe_experts",
    )(*tables, hs_c, hs_l, w1, w3, w2)


N_COMBINE_TABLES = 4


def _combine_kernel(n_ctx_tiles, cnt_ref, cpre_ref, lofs_ref, starts_ref, x_c_ref, x_l_ref, slot_c_ref, slot_l_ref,
                    wt_c_ref, wt_l_ref, mod_ref, ys_ref, xo_c_ref, xo_l_ref, buf_ref, sem):
    j = pl.program_id(0)
    n_tiles = pl.num_programs(0)

    def collect(t, act):
        slot = t % GATHER_SLOTS

        def body(e, carry):
            k = t * N_EXPERTS + e
            src, dst = starts_ref[e] + cpre_ref[k], slot * PAIR_BLOCK + lofs_ref[k]
            _segment_copies(cnt_ref[k], lambda done, size: pltpu.make_async_copy(
                _row_span(ys_ref, src + done, size), _row_span(buf_ref, dst + done, size), sem.at[slot]), act)
            return carry

        lax.fori_loop(0, N_EXPERTS, body, 0)

    @pl.when(j == 0)
    def _():
        for t in range(GATHER_AHEAD):
            collect(t, _start)

    @pl.when(j + GATHER_AHEAD < n_tiles)
    def _():
        collect(j + GATHER_AHEAD, _start)

    slot = j % GATHER_SLOTS
    for part in range(PAIR_BLOCK // ROW_BLOCK):
        pltpu.make_async_copy(
            _row_span(ys_ref, part * ROW_BLOCK, ROW_BLOCK),
            _row_span(buf_ref, slot * PAIR_BLOCK + part * ROW_BLOCK, ROW_BLOCK), sem.at[slot]).wait()
    rows = _from_row_slabs(buf_ref, slot * PAIR_BLOCK, PAIR_BLOCK)
    gate = mod_ref[0, 0, 5:6, :]

    def finish(x_ref, slot_ref, wt_ref, xo_ref):
        slots, wts = slot_ref[...].astype(F32), wt_ref[...]
        slot1, slot2 = slots[0:1], slots[1:2]
        weight_of_row = jnp.sum(_slot_one_hot(slot1, slot2, wts[0:1], wts[1:2]), axis=1, keepdims=True)
        hi, lo = _split2(rows * weight_of_row)
        perm = _slot_one_hot(slot1, slot2, 1.0, 1.0).astype(BF16)
        y = _dot_tn(perm, hi) + _dot_tn(perm, lo)
        xo_ref[...] = x_ref[...] + gate * y

    @pl.when(j < n_ctx_tiles)
    def _():
        finish(x_c_ref, slot_c_ref, wt_c_ref, xo_c_ref)

    @pl.when(j >= n_ctx_tiles)
    def _():
        finish(x_l_ref, slot_l_ref, wt_l_ref, xo_l_ref)


def _combine_call(l, plan, x_c, x_l, slot_c, slot_l, wt_c, wt_l, mods_all, mod_row_of_tile, ys):
    n_tiles = (x_c.shape[0] + x_l.shape[0]) // ROW_BLOCK
    n_ctx_tiles = x_c.shape[0] // ROW_BLOCK
    first, second = _two_streams(n_ctx_tiles)

    def lanes(index_map):
        return lambda i, *_: index_map(i)[::-1]

    return pl.pallas_call(
        functools.partial(_combine_kernel, n_ctx_tiles),
        grid_spec=pltpu.PrefetchScalarGridSpec(
            num_scalar_prefetch=N_COMBINE_TABLES,
            grid=(n_tiles,),
            in_specs=[pl.BlockSpec((ROW_BLOCK, D_MODEL), first),
                      pl.BlockSpec((ROW_BLOCK, D_MODEL), second),
                      pl.BlockSpec((2, ROW_BLOCK), lanes(first)),
                      pl.BlockSpec((2, ROW_BLOCK), lanes(second)),
                      pl.BlockSpec((2, ROW_BLOCK), lanes(first)),
                      pl.BlockSpec((2, ROW_BLOCK), lanes(second)),
                      pl.BlockSpec((1, 1, 6, D_MODEL), lambda i, *_: (l, mod_row_of_tile(i), 0, 0)),
                      pl.BlockSpec(memory_space=pl.ANY)],
            out_specs=[pl.BlockSpec((ROW_BLOCK, D_MODEL), first),
                       pl.BlockSpec((ROW_BLOCK, D_MODEL), second)],
            scratch_shapes=[pltpu.VMEM((GATHER_SLOTS * PAIR_BLOCK * ROW_SLABS, LANES), F32),
                            pltpu.SemaphoreType.DMA((GATHER_SLOTS,))],
        ),
        out_shape=[jax.ShapeDtypeStruct(x_c.shape, F32), jax.ShapeDtypeStruct(x_l.shape, F32)],
        compiler_params=pltpu.CompilerParams(
            dimension_semantics=("arbitrary",), vmem_limit_bytes=40 * 1024 * 1024),
        name="moe_combine",
    )(*plan["combine_tables"], x_c, x_l, slot_c, slot_l, wt_c, wt_l, mods_all, ys)


def _moe_plan(cnt):
    n_tok_tiles = cnt.shape[0]
    n_tiles = n_tok_tiles * PAIR_BLOCK // ROW_BLOCK + N_EXPERTS
    lofs = jnp.cumsum(cnt, axis=1) - cnt
    cpre = jnp.cumsum(cnt, axis=0) - cnt
    counts = jnp.sum(cnt, axis=0)
    padded = (counts + ROW_BLOCK - 1) // ROW_BLOCK * ROW_BLOCK
    ends = jnp.cumsum(padded)
    starts = ends - padded
    tile_start = jnp.arange(n_tiles, dtype=jnp.int32) * ROW_BLOCK
    tile_expert = jnp.minimum(
        jnp.sum((tile_start[:, None] >= ends[None, :]).astype(jnp.int32), axis=1), N_EXPERTS - 1)
    hot = tile_expert[:, None] == jnp.arange(N_EXPERTS, dtype=jnp.int32)[None, :]
    first = tile_start - jnp.sum(jnp.where(hot, starts[None, :], 0), axis=1)
    rows = jnp.clip(jnp.sum(jnp.where(hot, counts[None, :], 0), axis=1) - first, 0, ROW_BLOCK)
    seg_first = jnp.sum(jnp.where(hot[:, None, :], cpre[None, :, :], 0), axis=2)
    seg_rows = jnp.sum(jnp.where(hot[:, None, :], cnt[None, :, :], 0), axis=2)
    overlap = (seg_first < (first + rows)[:, None]) & (seg_first + seg_rows > first[:, None])
    j = jnp.arange(n_tok_tiles, dtype=jnp.int32)[None, :]
    jlo = jnp.min(jnp.where(overlap, j, n_tok_tiles), axis=1)
    jhi = jnp.max(jnp.where(overlap, j + 1, 0), axis=1)
    i32 = lambda a: a.astype(jnp.int32).reshape(-1)
    tile0 = jnp.concatenate([starts, ends[-1:]]) // ROW_BLOCK
    return {
        "expert_tables": tuple(i32(a) for a in (tile_expert, first, rows, jlo, jhi, cpre, cnt, lofs,
                                                tile0, padded // ROW_BLOCK)),
        "combine_tables": tuple(i32(a) for a in (cnt, cpre, lofs, starts)),
    }


def _rope_tables(n_tok):
    n_rows = n_tok // GRID_W
    pos_r = jnp.repeat(jnp.arange(n_rows), GRID_W)
    pos_c = jnp.tile(jnp.arange(GRID_W), n_rows)
    half = DIFF_QK // 2
    nf = half // 2
    freqs = ROPE_BASE ** (-jnp.arange(nf, dtype=F32) / nf)

    def tables(pos):
        ang = pos.astype(F32)[:, None] * freqs
        cos, sin = jnp.cos(ang), jnp.sin(ang)
        return jnp.concatenate([cos, cos], axis=-1), jnp.concatenate([-sin, sin], axis=-1)

    cos_r, sin_r = tables(pos_r)
    cos_c, sin_c = tables(pos_c)
    cos = jnp.concatenate([cos_r, cos_c], axis=-1)
    sin = jnp.concatenate([sin_r, sin_c], axis=-1)
    return jnp.concatenate([cos, cos], axis=-1), jnp.concatenate([sin, sin], axis=-1)


def _mixer_weights(w_in, w_out, sgu_w, sgu_b, q_norm_g, k_norm_g, diff_lambda, diff_norm_g, gla_w2, gla_b,
                   gla_norm_g, norm1_g, norm2_g, router_w, router_bias):
    w_in_pad = jnp.pad(w_in.astype(BF16), ((0, 0), (0, 0), (0, D_PROJ_PAD - w_in.shape[2])))
    w2cat = jnp.zeros((DEPTH, LANES, 2 * W_GLA), F32)
    w2cat = w2cat.at[:, 0:GLA_RANK, 0:W_GLA].set(gla_w2[:, 0]).at[:, GLA_RANK:2 * GLA_RANK, W_GLA:].set(gla_w2[:, 1])
    return (
        norm1_g[:, None, :], norm2_g[:, None, :], w_in_pad, w_out.astype(BF16),
        sgu_w.astype(BF16), jnp.repeat(sgu_b.transpose(0, 2, 1), SGU_GROUP_W, axis=2),
        jnp.tile(q_norm_g, (1, W_QK // DIFF_QK))[:, None, :], jnp.tile(k_norm_g, (1, W_QK // DIFF_QK))[:, None, :],
        diff_lambda, diff_norm_g[:, None, :],
        w2cat.astype(BF16), gla_b.reshape(DEPTH, 1, 2 * W_GLA), jnp.tile(gla_norm_g, (1, GLA_HEADS))[:, None, :],
        router_w.T, router_bias[:, None],
    )


def kernel(x_prompt, x_sample, cache_k, cache_v, state_gla, c, c_ctx, w_in, w_out, sgu_w, sgu_b, q_norm_g, k_norm_g,
           diff_lambda, diff_norm_g, gla_w2, gla_b, gla_norm_g, norm1_g, norm2_g, ada_w, ada_b, router_w, router_bias,
           moe_w1, moe_w3, moe_w2):
    n_ctx_seq, ctx_len, _ = x_prompt.shape
    n_lat_seq, lat_len, _ = x_sample.shape
    n_ctx_tok = n_ctx_seq * ctx_len
    n_lat_tok = n_lat_seq * lat_len
    ctx_tiles = n_ctx_tok // ROW_BLOCK
    lat_tiles_per_seq = lat_len // ROW_BLOCK

    n_cond = 1 + n_lat_seq
    cond_t = jnp.zeros((D_MODEL, SUBLANES), F32).at[:, 0].set(c_ctx).at[:, 1:n_cond].set(c.T)
    mods_all = _adaln_call(cond_t, n_cond, ada_w, ada_b)[:, :n_cond].reshape(DEPTH, n_cond, 6, D_MODEL)
    weights = _mixer_weights(w_in, w_out, sgu_w, sgu_b, q_norm_g, k_norm_g, diff_lambda, diff_norm_g, gla_w2, gla_b,
                             gla_norm_g, norm1_g, norm2_g, router_w, router_bias)

    ck_all = cache_k.transpose(0, 1, 2, 4, 3, 5).reshape(cache_k.shape[:3] + (cache_k.shape[4], DIFF_V))
    st_all = jnp.einsum('bldhkv,hg->bldhvgk', state_gla, jnp.eye(GLA_HEADS, dtype=F32)).reshape(
        n_lat_seq, DEPTH, 2, W_GLA, W_GLA)
    cos, sin = _rope_tables(lat_len)
    extras = (ck_all, cache_v, st_all, cos, sin)

    def mod_row_of_tile(i):
        return jnp.where(i < ctx_tiles, 0, 1 + (i - ctx_tiles) // lat_tiles_per_seq)

    x_c = x_prompt.reshape(n_ctx_tok, D_MODEL)
    x_l = x_sample.reshape(n_lat_tok, D_MODEL)
    cache_bufs = ()
    for l in range(DEPTH):
        ctx_par = 1 if l == 0 else CTX_SEQS_PER_STEP
        x1_c, hs_c, slot_c, wt_c, cnt_c, *cache_bufs = _mixer_call(
            l, ctx_len, ctx_par, False, x_c, mods_all, weights, None, tuple(cache_bufs))
        x1_l, hs_l, slot_l, wt_l, cnt_l = _mixer_call(l, lat_len, 1, True, x_l, mods_all, weights, extras, ())
        plan = _moe_plan(jnp.concatenate([cnt_c[:, :, 0], cnt_l[:, :, 0]], axis=0))
        ys = _expert_call(l, plan, hs_c, hs_l, moe_w1, moe_w3, moe_w2)
        x_c, x_l = _combine_call(l, plan, x1_c, x1_l, slot_c, slot_l, wt_c, wt_l, mods_all, mod_row_of_tile, ys)

    new_k, new_v, new_s = cache_bufs
    return (x_c.reshape(x_prompt.shape), x_l.reshape(x_sample.shape), new_k, new_v, new_s)
```

```python
import functools
import math

import jax
import jax.numpy as jnp
from jax import lax
from jax.experimental import pallas as pl
from jax.experimental.pallas import tpu as pltpu

F32 = jnp.float32
BF16 = jnp.bfloat16

D_MODEL = 1024
DEPTH = 4
GRID_W = 64
SGU_GROUPS = 4
SGU_GROUP_W = 64
SGU_W = SGU_GROUPS * SGU_GROUP_W
SGU_CHUNK = 128
DIFF_HEADS = 4
DIFF_QK = 64
DIFF_V = 2 * DIFF_QK
ROPE_BASE = 10000.0
GLA_HEADS = 4
GLA_DK = 64
GLA_DV = 64
GLA_RANK = 16
GLA_GATE_NORM = 16.0
GLA_CHUNK = 64
N_EXPERTS = 16
N_GROUPS = 4
EXPERTS_PER_GROUP = N_EXPERTS // N_GROUPS
D_EXPERT = 512
EPS = 1e-6

LANES = 128
SUBLANES = 8
MXU_DIM = 256
V7X_VMEM_BYTES = 64 * 1024 * 1024
MIXER_VMEM_LIMIT = V7X_VMEM_BYTES * 7 // 8
SMALL_KERNEL_VMEM_LIMIT = V7X_VMEM_BYTES * 5 // 8

ROW_BLOCK = MXU_DIM
ROW_SLABS = D_MODEL // LANES

W_QK = DIFF_HEADS * 2 * DIFF_QK
W_GLA = GLA_HEADS * GLA_DK
C_AU, C_AV = 0, SGU_W
C_BQ = C_AV + SGU_W
C_BK, C_BV = C_BQ + W_QK, C_BQ + 2 * W_QK
C_CQ = C_BV + DIFF_HEADS * DIFF_V
C_CK, C_CV, C_CR, C_LR = C_CQ + W_GLA, C_CQ + 2 * W_GLA, C_CQ + 3 * W_GLA, C_CQ + 4 * W_GLA
D_PROJ_MAIN = C_LR
D_PROJ_PAD = D_PROJ_MAIN + LANES
M_A, M_B, M_C = 0, SGU_W, SGU_W + DIFF_HEADS * DIFF_V


def _split2(x):
    hi = x.astype(BF16)
    lo = (x - hi.astype(F32)).astype(BF16)
    return hi, lo


def _split3(x):
    hi = x.astype(BF16)
    r = x - hi.astype(F32)
    mid = r.astype(BF16)
    lo = (r - mid.astype(F32)).astype(BF16)
    return hi, mid, lo


def _dot(a, b):
    return jnp.dot(a, b, preferred_element_type=F32)


def _dot_nt(a, b):
    return lax.dot_general(a, b, (((1,), (1,)), ((), ())), preferred_element_type=F32)


def _dot_tn(a, b):
    return lax.dot_general(a, b, (((0,), (0,)), ((), ())), preferred_element_type=F32)


def _iota(shape, dim):
    return lax.broadcasted_iota(jnp.int32, shape, dim)


def _block_ones(width, block):
    r = _iota((width, width), 0) // block
    c = _iota((width, width), 1) // block
    return (r == c)


def _group_sum(z, block):
    width = z.shape[-1]
    outs = []
    for s in range(0, width, MXU_DIM):
        w = min(MXU_DIM, width - s)
        ones = _block_ones(w, block).astype(BF16)
        hi, lo = _split2(z[:, s:s + w])
        outs.append(_dot(hi, ones) + _dot(lo, ones))
    return outs[0] if len(outs) == 1 else jnp.concatenate(outs, axis=-1)


def _group_rms(z, block):
    ms = _group_sum(z * z, block) * (1.0 / block)
    return z * lax.rsqrt(ms + EPS)


def _row_rms(z):
    return z * lax.rsqrt(jnp.mean(z * z, axis=-1, keepdims=True) + EPS)


def _log_sigmoid(x):
    return jnp.minimum(x, 0.0) - jnp.log1p(jnp.exp(-jnp.abs(x)))


ADA_COLS = 1536


def _adaln_kernel(n_cond, cond_t_ref, w_ref, b_ref, o_ref):
    sc = jax.nn.silu(cond_t_ref[...])
    w = w_ref[0]
    rows = [jnp.sum(sc[:, r:r + 1] * w, axis=0, keepdims=True) + b_ref[0] for r in range(n_cond)]
    o_ref[0] = jnp.concatenate(rows + [jnp.zeros((SUBLANES - n_cond, w.shape[1]), F32)], axis=0)


def _adaln_call(cond_t, n_cond, ada_w, ada_b):
    n_col = 6 * D_MODEL // ADA_COLS
    return pl.pallas_call(
        functools.partial(_adaln_kernel, n_cond),
        grid=(DEPTH, n_col),
        in_specs=[
            pl.BlockSpec((D_MODEL, SUBLANES), lambda l, j: (0, 0)),
            pl.BlockSpec((1, D_MODEL, ADA_COLS), lambda l, j: (l, 0, j)),
            pl.BlockSpec((1, 1, ADA_COLS), lambda l, j: (l, 0, j)),
        ],
        out_specs=pl.BlockSpec((1, SUBLANES, ADA_COLS), lambda l, j: (l, 0, j)),
        out_shape=jax.ShapeDtypeStruct((DEPTH, SUBLANES, 6 * D_MODEL), F32),
        compiler_params=pltpu.CompilerParams(
            dimension_semantics=("arbitrary", "arbitrary"), vmem_limit_bytes=SMALL_KERNEL_VMEM_LIMIT),
        name="adaln",
    )(cond_t, ada_w, ada_b.reshape(DEPTH, 1, 6 * D_MODEL))


def _route(hn, rwt_ref, rb_ref):
    h_hi, h_lo = _split2(hn)
    rw = rwt_ref[...]
    rw_hi = rw.astype(BF16)
    rw_lo = (rw - rw_hi.astype(F32)).astype(BF16)
    logits = _dot_nt(rw_hi, h_hi) + _dot_nt(rw_hi, h_lo) + _dot_nt(rw_lo, h_hi)
    aff = jax.nn.sigmoid(logits)
    sel = aff + rb_ref[...]
    n_tok = sel.shape[1]

    def top2_sum(a, b, c, d):
        hi1, lo1 = jnp.maximum(a, b), jnp.minimum(a, b)
        hi2, lo2 = jnp.maximum(c, d), jnp.minimum(c, d)
        return jnp.maximum(hi1, hi2) + jnp.maximum(jnp.minimum(hi1, hi2), jnp.maximum(lo1, lo2))

    scores = []
    for g in range(N_GROUPS):
        rows = [sel[EXPERTS_PER_GROUP * g + j:EXPERTS_PER_GROUP * g + j + 1, :] for j in range(EXPERTS_PER_GROUP)]
        scores.append(top2_sum(*rows))
    best = jnp.zeros((1, n_tok), jnp.int32)
    best_score = scores[0]
    for g in range(1, N_GROUPS):
        upd = scores[g] > best_score
        best = jnp.where(upd, g, best)
        best_score = jnp.where(upd, scores[g], best_score)

    eid_i = _iota((N_EXPERTS, n_tok), 0)
    eid = eid_i.astype(F32)
    neg = jnp.float32(-jnp.inf)
    msel = jnp.where(eid_i // EXPERTS_PER_GROUP == best, sel, neg)
    m1 = jnp.max(msel, axis=0, keepdims=True)
    idx1 = jnp.min(jnp.where(msel == m1, eid, float(N_EXPERTS)), axis=0, keepdims=True)
    msel2 = jnp.where(eid == idx1, neg, msel)
    m2 = jnp.max(msel2, axis=0, keepdims=True)
    idx2 = jnp.min(jnp.where(msel2 == m2, eid, float(N_EXPERTS)), axis=0, keepdims=True)
    w1 = jnp.sum(jnp.where(eid == idx1, aff, 0.0), axis=0, keepdims=True)
    w2 = jnp.sum(jnp.where(eid == idx2, aff, 0.0), axis=0, keepdims=True)
    wsum = w1 + w2
    return idx1.astype(jnp.int32), idx2.astype(jnp.int32), w1 / wsum, w2 / wsum


def _local_slots(idx1, idx2):
    n_tok = idx1.shape[1]
    eid = _iota((N_EXPERTS, n_tok), 0)
    hot1, hot2 = eid == idx1, eid == idx2
    hot = jnp.where(hot1, 1.0, jnp.where(hot2, 1.0, 0.0))
    earlier = jnp.where(_iota((n_tok, n_tok), 0) < _iota((n_tok, n_tok), 1), 1.0, 0.0).astype(BF16)
    before_in_expert = _dot(hot.astype(BF16), earlier)
    counts = jnp.sum(hot, axis=1, keepdims=True)
    lower = jnp.where(_iota((N_EXPERTS, N_EXPERTS), 1) < _iota((N_EXPERTS, N_EXPERTS), 0), 1.0, 0.0).astype(BF16)
    first_slot = _dot(lower, jnp.broadcast_to(counts, (N_EXPERTS, LANES)).astype(BF16))[:, 0:1]
    slot = before_in_expert + first_slot
    slot1 = jnp.sum(jnp.where(hot1, slot, 0.0), axis=0, keepdims=True)
    slot2 = jnp.sum(jnp.where(hot2, slot, 0.0), axis=0, keepdims=True)
    return slot1, slot2, counts


def _slot_one_hot(slot1, slot2, v1, v2):
    n_tok = slot1.shape[1]
    row = _iota((2 * n_tok, n_tok), 0).astype(F32)
    return jnp.where(row == slot1, v1, jnp.where(row == slot2, v2, 0.0))


N_MIXER_WEIGHTS = 15
CTX_SEQS_PER_STEP = 2
MAX_INLINE_BLOCKS = 2


def _mixer_kernel(n_tok, n_par, latent, n_alias, lam_init, *refs):
    it = iter(refs)
    x_ref, mod_ref = next(it), next(it)
    (n1_ref, n2_ref, win_ref, wout_ref, sw_ref, sb_ref, qg_ref, kg_ref, dl_ref, dg_ref,
     w2c_ref, gb_ref, gg_ref, rwt_ref, rb_ref) = (next(it) for _ in range(N_MIXER_WEIGHTS))
    if latent:
        ck_ref, cv_ref, st0_ref, cos_ref, sin_ref = (next(it) for _ in range(5))
    for _ in range(n_alias):
        next(it)
    xo_ref, hs_ref, slot_ref, wt_ref, cnt_ref = (next(it) for _ in range(5))
    if not latent:
        ko_ref, vo_ref, so_ref = (next(it) for _ in range(3))
    proj_ref, mix_ref, q_ref, k_ref, v_ref = (next(it) for _ in range(5))
    gq_ref, gke_ref, gv_ref, gr_ref, dec_ref, go_ref, st_ref = (next(it) for _ in range(7))

    n_blk = n_tok // ROW_BLOCK
    n_ctx = k_ref.shape[0] - n_par * n_tok
    n_keys = n_ctx + n_tok
    mod = mod_ref[0, 0]

    by_block = latent

    def whole_sequence(fn):
        def run():
            fn()

        if by_block:
            pl.when(pl.program_id(1) == 0)(run)
        else:
            run()

    def blocks(body):
        if n_par * n_blk <= MAX_INLINE_BLOCKS:
            for r in range(n_par * n_blk):
                body(r)
        else:
            def step(r, carry):
                body(r)
                return carry
            whole_sequence(lambda: lax.fori_loop(0, n_par * n_blk, step, 0))

    def aligned(start, size):
        return pl.ds(start if isinstance(start, int) else pl.multiple_of(start, size), size)

    def block_rows(r, offset=0):
        return aligned(offset + r * ROW_BLOCK, ROW_BLOCK)

    if not latent:
        for ref in (ko_ref, vo_ref, so_ref):
            for q in range(n_par):
                for other in range(1, ref.shape[1]):
                    ref[q, other] = jnp.zeros(ref.shape[2:], F32)

    lane_group = _iota((SGU_CHUNK, SGU_W), 1) // SGU_GROUP_W
    blk_r = _iota((ROW_BLOCK, ROW_BLOCK), 0)
    blk_c = _iota((ROW_BLOCK, ROW_BLOCK), 1)
    same_chunk = (blk_r // GLA_CHUNK) == (blk_c // GLA_CHUNK)
    tri = (jnp.where(same_chunk & (blk_c <= blk_r), 1.0, 0.0).astype(BF16),
           jnp.where(same_chunk & (blk_c >= blk_r), 1.0, 0.0).astype(BF16))
    chunks_per_blk = ROW_BLOCK // GLA_CHUNK
    head_of_lane = _iota((GLA_CHUNK, W_GLA), 1) // GLA_DK
    stack_r = _iota((GLA_HEADS * GLA_CHUNK, GLA_CHUNK), 0) % GLA_CHUNK
    stack_c = _iota((GLA_HEADS * GLA_CHUNK, GLA_CHUNK), 1)
    causal = (stack_c <= stack_r, stack_c >= stack_r)

    if latent:
        def cached_context():
            for h in range(DIFF_HEADS):
                k_ref[0:n_ctx, h * DIFF_V:(h + 1) * DIFF_V] = ck_ref[0, 0, h].astype(BF16)
                v_ref[0:n_ctx, h * DIFF_V:(h + 1) * DIFF_V] = cv_ref[0, 0, h].astype(BF16)
            st_ref[0] = st0_ref[0, 0]

        whole_sequence(cached_context)
        pair_lo = (_iota((ROW_BLOCK, W_QK), 1) % (DIFF_QK // 2)) < (DIFF_QK // 4)

        def rope(z, rows):
            cos = jnp.concatenate([cos_ref[rows, :]] * DIFF_HEADS, axis=-1)
            sin = jnp.concatenate([sin_ref[rows, :]] * DIFF_HEADS, axis=-1)
            shift = DIFF_QK // 4
            swapped = jnp.where(pair_lo, pltpu.roll(z, W_QK - shift, 1), pltpu.roll(z, shift, 1))
            return z * cos + swapped * sin

    def modulated_input(r):
        h = _row_rms(x_ref[block_rows(r), :]) * n1_ref[0]
        return (h * (1.0 + mod[1:2, :]) + mod[0:1, :]).astype(BF16)

    def spatial_gating(r):
        for c in range(ROW_BLOCK // SGU_CHUNK):
            local = slice(c * SGU_CHUNK, (c + 1) * SGU_CHUNK)
            u = jax.nn.gelu(proj_ref[local, C_AU:C_AU + SGU_W])
            v = _group_rms(jax.nn.gelu(proj_ref[local, C_AV:C_AV + SGU_W]), SGU_GROUP_W).astype(BF16)
            s = sb_ref[0]
            for g in range(SGU_GROUPS):
                s = s + jnp.where(lane_group == g, _dot(sw_ref[0, g], v), 0.0)
            mix_ref[aligned(r * ROW_BLOCK + c * SGU_CHUNK, SGU_CHUNK), M_A:M_A + SGU_W] = (u * s).astype(BF16)

    def attention_operands(r):
        rows = block_rows(r)
        key_rows = block_rows(r, n_ctx)
        seq, seq_rows = r // n_blk, block_rows(r % n_blk)
        qn = _group_rms(proj_ref[:, C_BQ:C_BQ + W_QK], DIFF_QK) * qg_ref[0]
        kn = _group_rms(proj_ref[:, C_BK:C_BK + W_QK], DIFF_QK) * kg_ref[0]
        vv = proj_ref[:, C_BV:C_BV + W_QK]
        if latent:
            qn, kn = rope(qn, rows), rope(kn, rows)
        else:
            for h in range(DIFF_HEADS):
                for i in range(2):
                    lo = h * DIFF_V + i * DIFF_QK
                    ko_ref[seq, 0, h, i, seq_rows, :] = kn[:, lo:lo + DIFF_QK]
                vo_ref[seq, 0, h, seq_rows, :] = vv[:, h * DIFF_V:(h + 1) * DIFF_V]
        q_ref[rows, :] = (qn * (DIFF_QK ** -0.5)).astype(BF16)
        k_ref[key_rows, :] = kn.astype(BF16)
        v_ref[key_rows, :] = vv.astype(BF16)

    def gla_operands(r):
        rows = block_rows(r)
        gpre = _dot(proj_ref[:, C_LR:C_LR + LANES].astype(BF16), w2c_ref[0]) + gb_ref[0]
        gate = _log_sigmoid(gpre) * (1.0 / GLA_GATE_NORM)
        gq = proj_ref[:, C_CQ:C_CQ + W_GLA] * (GLA_DK ** -0.5)
        gk = proj_ref[:, C_CK:C_CK + W_GLA]
        gv = proj_ref[:, C_CV:C_CV + W_GLA].astype(BF16)
        gv_ref[rows, :] = gv
        gr_ref[rows, :] = proj_ref[:, C_CR:C_CR + W_GLA]
        for d in range(2):
            g = gate[:, d * W_GLA:(d + 1) * W_GLA]
            b = sum(_dot(tri[d], p) for p in _split3(g))
            last = GLA_CHUNK - 1 if d == 0 else 0
            b_last = jnp.concatenate(
                [jnp.broadcast_to(b[c * GLA_CHUNK + last:c * GLA_CHUNK + last + 1, :], (GLA_CHUNK, W_GLA))
                 for c in range(chunks_per_blk)], axis=0)
            q_dec = (gq * jnp.exp(b)).astype(BF16)
            k_inv = (gk * jnp.exp(-b)).astype(BF16)
            gq_ref[d, rows, :] = q_dec
            gke_ref[d, rows, :] = (gk * jnp.exp(b_last - b)).astype(BF16)
            for c in range(chunks_per_blk):
                row = c * GLA_CHUNK + last
                dec_ref[d, r * chunks_per_blk + c] = jnp.exp(b[row:row + 1, :])
                chunk = slice(c * GLA_CHUNK, (c + 1) * GLA_CHUNK)
                qd = q_dec[chunk]
                q_stack = jnp.concatenate(
                    [jnp.where(head_of_lane == h, qd, jnp.zeros_like(qd)) for h in range(GLA_HEADS)], axis=0)
                attn = jnp.where(causal[d], _dot_nt(q_stack, k_inv[chunk]), 0.0)
                spread = _dot(attn.astype(BF16), gv[chunk])
                o = jnp.zeros((GLA_CHUNK, W_GLA), F32)
                for h in range(GLA_HEADS):
                    o = o + jnp.where(head_of_lane == h, spread[h * GLA_CHUNK:(h + 1) * GLA_CHUNK, :], 0.0)
                go_ref[d, aligned(r * ROW_BLOCK + c * GLA_CHUNK, GLA_CHUNK), :] = o

    def project_and_split(r):
        proj_ref[...] = _dot(modulated_input(r), win_ref[0])
        spatial_gating(r)
        attention_operands(r)
        gla_operands(r)

    blocks(project_and_split)

    dl = dl_ref[0]
    lam = (jnp.exp(jnp.sum(dl[0:1] * dl[1:2], axis=-1, keepdims=True))
           - jnp.exp(jnp.sum(dl[2:3] * dl[3:4], axis=-1, keepdims=True)) + lam_init)
    sub0 = (_iota((ROW_BLOCK, DIFF_V), 1) < DIFF_QK)

    def softmax(s):
        e = jnp.exp(s - jnp.max(s, axis=-1, keepdims=True))
        return e, jnp.sum(e, axis=-1, keepdims=True)

    def attn_block(r):
        rows = block_rows(r)
        keys = aligned((r // n_blk) * n_keys, n_keys)
        for h in range(DIFF_HEADS):
            cols = slice(h * DIFF_V, (h + 1) * DIFF_V)
            qh = q_ref[rows, cols]
            kh = k_ref[keys, cols]
            e0, z0 = softmax(_dot_nt(jnp.where(sub0, qh, jnp.zeros_like(qh)), kh))
            e1, z1 = softmax(_dot_nt(jnp.where(sub0, jnp.zeros_like(qh), qh), kh))
            w = e0 / z0 - lam * (e1 / z1)
            o = _dot(w.astype(BF16), v_ref[keys, cols])
            o = _row_rms(o) * dg_ref[0] * (1.0 - lam_init)
            mix_ref[rows, M_B + h * DIFF_V:M_B + (h + 1) * DIFF_V] = o.astype(BF16)

    blocks(attn_block)

    if not latent:
        st_ref[...] = jnp.zeros(st_ref.shape, F32)

    n_chunk = n_tok // GLA_CHUNK
    st_diag = (_iota((W_GLA, W_GLA), 0) // GLA_DV) == (_iota((W_GLA, W_GLA), 1) // GLA_DK)

    def gla_step(c, carry):
        for seq in range(n_par):
            for d in range(2):
                cc = seq * n_chunk + (c if d == 0 else n_chunk - 1 - c)
                rows = pl.ds(pl.multiple_of(cc * GLA_CHUNK, GLA_CHUNK), GLA_CHUNK)
                st = st_ref[seq, d]
                go_ref[d, rows, :] = go_ref[d, rows, :] + _dot_nt(gq_ref[d, rows, :], st.astype(BF16))
                upd = _dot_tn(gv_ref[rows, :], gke_ref[d, rows, :])
                st_ref[seq, d] = dec_ref[d, cc] * st + jnp.where(st_diag, upd, 0.0)
        return carry

    whole_sequence(lambda: lax.fori_loop(0, n_chunk, gla_step, 0))

    if not latent:
        for seq in range(n_par):
            for d in range(2):
                s_full = st_ref[seq, d].T
                for h in range(GLA_HEADS):
                    so_ref[seq, 0, d, h] = s_full[h * GLA_DK:(h + 1) * GLA_DK, h * GLA_DV:(h + 1) * GLA_DV]

    def finish_block(r, out_r):
        rows, out_rows = block_rows(r), block_rows(out_r)
        oc = _group_rms(go_ref[0, rows, :] + go_ref[1, rows, :], GLA_DV) * gg_ref[0]
        oc = oc * jax.nn.silu(gr_ref[rows, :])
        mix_ref[rows, M_C:M_C + W_GLA] = oc.astype(BF16)
        x1 = x_ref[rows, :] + mod[2:3, :] * _dot(mix_ref[rows, :], wout_ref[0])
        xo_ref[out_rows, :] = x1
        hn = _row_rms(x1) * n2_ref[0]
        hn = hn * (1.0 + mod[4:5, :]) + mod[3:4, :]
        idx1, idx2, w1, w2 = _route(hn, rwt_ref, rb_ref)
        slot1, slot2, counts = _local_slots(idx1, idx2)
        perm = _slot_one_hot(slot1, slot2, 1.0, 1.0).astype(BF16)
        _to_row_slabs(hs_ref, 2 * out_r * ROW_BLOCK, _dot(perm, hn.astype(BF16)))
        slot_ref[:, out_rows] = jnp.concatenate([slot1, slot2], axis=0).astype(jnp.int32)
        wt_ref[:, out_rows] = jnp.concatenate([w1, w2], axis=0)
        cnt_ref[out_r] = jnp.broadcast_to(counts, (N_EXPERTS, LANES)).astype(jnp.int32)

    if by_block:
        finish_block(pl.program_id(1), 0)
    else:
        for r in range(n_par * n_blk):
            finish_block(r, r)


def _mixer_call(l, n_tok, n_par, latent, x, mods_all, weights, extras, cache_bufs):
    n_seq = x.shape[0] // n_tok
    n_all = x.shape[0]
    assert n_seq % n_par == 0 and not (latent and n_par > 1)
    n_step_tok = n_par * n_tok
    n_keys = n_step_tok + (extras[0].shape[3] if latent else 0)
    n_chunk = n_step_tok // GLA_CHUNK
    lam_init = 0.8 - 0.6 * math.exp(-0.3 * l)

    single = pl.Buffered(1)
    seq_mode = single if latent else None

    def layer(arr):
        tail = arr.shape[1:]
        return pl.BlockSpec((1,) + tail, lambda s, *_r, _n=len(tail): (l,) + (0,) * _n, pipeline_mode=single)

    def const(arr):
        return pl.BlockSpec(arr.shape, lambda s, *_r, _n=arr.ndim: (0,) * _n, pipeline_mode=single)

    def tok_spec(width):
        return pl.BlockSpec((n_step_tok, width), lambda s, *_r: (s, 0), pipeline_mode=seq_mode)

    mod_row = (lambda s: 1 + s) if latent else (lambda s: 0)
    in_specs = [tok_spec(D_MODEL),
                pl.BlockSpec((1, 1, 6, D_MODEL), lambda s, *_r: (l, mod_row(s), 0, 0))]
    in_specs += [layer(w) for w in weights[:N_MIXER_WEIGHTS - 2]] + [const(w) for w in weights[-2:]]
    operands = [x, mods_all] + list(weights)
    if latent:
        ck, cv, st0, cos, sin = extras
        in_specs += [
            pl.BlockSpec((1, 1) + ck.shape[2:], lambda s, *_r: (s, l, 0, 0, 0)),
            pl.BlockSpec((1, 1) + cv.shape[2:], lambda s, *_r: (s, l, 0, 0, 0)),
            pl.BlockSpec((1, 1) + st0.shape[2:], lambda s, *_r: (s, l, 0, 0, 0)),
            const(cos), const(sin),
        ]
        operands += [ck, cv, st0, cos, sin]
    n_in = len(operands)
    in_specs += [pl.BlockSpec(memory_space=pl.ANY)] * len(cache_bufs)
    operands += list(cache_bufs)

    tiles_per_step = n_step_tok // ROW_BLOCK
    out_shape = [
        jax.ShapeDtypeStruct((n_all, D_MODEL), F32),
        jax.ShapeDtypeStruct((2 * n_all * ROW_SLABS, LANES), F32),
        jax.ShapeDtypeStruct((2, n_all), jnp.int32),
        jax.ShapeDtypeStruct((2, n_all), F32),
        jax.ShapeDtypeStruct((n_all // ROW_BLOCK, N_EXPERTS, LANES), jnp.int32),
    ]
    if latent:
        grid = (n_seq, tiles_per_step)
        out_tok, out_tiles = ROW_BLOCK, 1
        at = lambda s, r: s * tiles_per_step + r
    else:
        grid = (n_seq // n_par,)
        out_tok, out_tiles = n_step_tok, tiles_per_step
        at = lambda s: s
    out_specs = [
        pl.BlockSpec((out_tok, D_MODEL), lambda *g: (at(*g), 0)),
        pl.BlockSpec((2 * out_tok * ROW_SLABS, LANES), lambda *g: (at(*g), 0)),
        pl.BlockSpec((2, out_tok), lambda *g: (0, at(*g))),
        pl.BlockSpec((2, out_tok), lambda *g: (0, at(*g))),
        pl.BlockSpec((out_tiles, N_EXPERTS, LANES), lambda *g: (at(*g), 0, 0)),
    ]
    n_shared_out = len(out_shape)
    aliases = {}
    if not latent:
        out_shape += [
            jax.ShapeDtypeStruct((n_seq, DEPTH, DIFF_HEADS, 2, n_tok, DIFF_QK), F32),
            jax.ShapeDtypeStruct((n_seq, DEPTH, DIFF_HEADS, n_tok, DIFF_V), F32),
            jax.ShapeDtypeStruct((n_seq, DEPTH, 2, GLA_HEADS, GLA_DK, GLA_DV), F32),
        ]
        n_lay, lay = (1, l) if cache_bufs else (DEPTH, 0)
        out_specs += [
            pl.BlockSpec((n_par, n_lay, DIFF_HEADS, 2, n_tok, DIFF_QK), lambda s: (s, lay, 0, 0, 0, 0)),
            pl.BlockSpec((n_par, n_lay, DIFF_HEADS, n_tok, DIFF_V), lambda s: (s, lay, 0, 0, 0)),
            pl.BlockSpec((n_par, n_lay, 2, GLA_HEADS, GLA_DK, GLA_DV), lambda s: (s, lay, 0, 0, 0, 0)),
        ]
        aliases = {n_in + j: n_shared_out + j for j in range(len(cache_bufs))}
    scratch = [
        pltpu.VMEM((ROW_BLOCK, D_PROJ_PAD), F32),
        pltpu.VMEM((n_step_tok, D_MODEL), BF16),
        pltpu.VMEM((n_step_tok, W_QK), BF16),
        pltpu.VMEM((n_keys, W_QK), BF16),
        pltpu.VMEM((n_keys, W_QK), BF16),
        pltpu.VMEM((2, n_step_tok, W_GLA), BF16),
        pltpu.VMEM((2, n_step_tok, W_GLA), BF16),
        pltpu.VMEM((n_step_tok, W_GLA), BF16),
        pltpu.VMEM((n_step_tok, W_GLA), F32),
        pltpu.VMEM((2, n_chunk, 1, W_GLA), F32),
        pltpu.VMEM((2, n_step_tok, W_GLA), F32),
        pltpu.VMEM((n_par, 2, W_GLA, W_GLA), F32),
    ]
    return pl.pallas_call(
        functools.partial(_mixer_kernel, n_tok, n_par, latent, len(cache_bufs), lam_init),
        grid=grid,
        in_specs=in_specs,
        out_specs=out_specs,
        out_shape=out_shape,
        scratch_shapes=scratch,
        input_output_aliases=aliases,
        compiler_params=pltpu.CompilerParams(
            dimension_semantics=("arbitrary",) * len(grid), vmem_limit_bytes=MIXER_VMEM_LIMIT),
        name="mixer_latent" if latent else "mixer_context",
    )(*operands)


PAIR_BLOCK = 2 * ROW_BLOCK
COPY_SIZES = tuple(ROW_BLOCK >> k for k in range(ROW_BLOCK.bit_length()))
LARGE_COPY = 64
GATHER_AHEAD = 2
GATHER_SLOTS = GATHER_AHEAD + 1


def _segment_copies(n_rows, make_copy, act):
    def copy_if_set(size):
        @pl.when((n_rows & size) != 0)
        def _():
            act(make_copy(n_rows & (-2 * size), size))

    n_large = COPY_SIZES.index(LARGE_COPY) + 1

    @pl.when(n_rows >= LARGE_COPY)
    def _():
        for size in COPY_SIZES[:n_large]:
            copy_if_set(size)

    for size in COPY_SIZES[n_large:]:
        copy_if_set(size)


def _start(copy):
    copy.start()


def _wait(copy):
    copy.wait()


def _slab_rows(first_row, n_rows, slab):
    return pl.ds(first_row * ROW_SLABS + slab, n_rows, stride=ROW_SLABS)


def _to_row_slabs(ref, first_row, value):
    for s in range(ROW_SLABS):
        ref[_slab_rows(first_row, value.shape[0], s), :] = value[:, s * LANES:(s + 1) * LANES]


def _from_row_slabs(ref, first_row, n_rows):
    return jnp.concatenate([ref[_slab_rows(first_row, n_rows, s), :] for s in range(ROW_SLABS)], axis=-1)


def _row_span(ref, first_row, n_rows):
    return ref.at[pl.ds(pl.multiple_of(first_row * ROW_SLABS, ROW_SLABS), n_rows * ROW_SLABS)]


def _two_streams(n_first_tiles):
    def first(i, *_):
        return (jnp.minimum(i, n_first_tiles - 1), 0)

    def second(i, *_):
        return (jnp.maximum(i - n_first_tiles, 0), 0)

    return first, second


N_EXPERT_TABLES = 10
OUT_SLOTS = 2


def _expert_kernel(n_ctx_tiles, n_tiles_max, te_ref, first_ref, rows_ref, jlo_ref, jhi_ref, cpre_ref, cnt_ref,
                   lofs_ref, tile0_ref, ntile_ref, hs_c_ref, hs_l_ref, w1_ref, w3_ref, w2_ref, ys_ref,
                   xbuf_ref, obuf_ref, w1b_ref, w3b_ref, w2b_ref, sem, out_sem):
    expert = pl.program_id(0)
    n_tiles = tile0_ref[N_EXPERTS]

    def gather(t, act):
        slot = t % GATHER_SLOTS
        e, first = te_ref[t], first_ref[t]
        last = first + rows_ref[t]

        def segment_of(hs_ref, first_tile):
            def body(j, carry):
                k = j * N_EXPERTS + e
                seg_first = cpre_ref[k]
                lo = jnp.maximum(seg_first, first)
                n = jnp.maximum(jnp.minimum(seg_first + cnt_ref[k], last) - lo, 0)
                src = (j - first_tile) * PAIR_BLOCK + lofs_ref[k] + (lo - seg_first)
                dst = slot * ROW_BLOCK + lo - first
                _segment_copies(n, lambda done, size: pltpu.make_async_copy(
                    _row_span(hs_ref, src + done, size), _row_span(xbuf_ref, dst + done, size), sem.at[slot]), act)
                return carry
            return body

        jlo, jhi = jlo_ref[t], jhi_ref[t]
        lax.fori_loop(jnp.minimum(jlo, n_ctx_tiles), jnp.minimum(jhi, n_ctx_tiles), segment_of(hs_c_ref, 0), 0)
        lax.fori_loop(jnp.maximum(jlo, n_ctx_tiles), jnp.maximum(jhi, n_ctx_tiles),
                      segment_of(hs_l_ref, n_ctx_tiles), 0)

    def out_copy(t, oslot):
        return pltpu.make_async_copy(
            _row_span(obuf_ref, oslot * ROW_BLOCK, ROW_BLOCK), _row_span(ys_ref, t * ROW_BLOCK, ROW_BLOCK),
            out_sem.at[oslot])

    @pl.when(expert == 0)
    def _():
        xbuf_ref[...] = jnp.zeros(xbuf_ref.shape, F32)
        for t in range(GATHER_AHEAD):
            gather(t, _start)

    w1b_ref[...] = w1_ref[0, 0].astype(BF16)
    w3b_ref[...] = w3_ref[0, 0].astype(BF16)
    w2b_ref[...] = w2_ref[0, 0].astype(BF16)
    tile0, n_own = tile0_ref[expert], ntile_ref[expert]

    def tile_body(k, carry):
        t = tile0 + k
        slot, oslot = t % GATHER_SLOTS, k % OUT_SLOTS

        @pl.when(t + GATHER_AHEAD < n_tiles)
        def _():
            gather(t + GATHER_AHEAD, _start)

        n_rows = rows_ref[t]
        _segment_copies(n_rows, lambda done, size: pltpu.make_async_copy(
            _row_span(hs_c_ref, done, size), _row_span(xbuf_ref, slot * ROW_BLOCK + done, size), sem.at[slot]), _wait)

        @pl.when(k >= OUT_SLOTS)
        def _():
            out_copy(t, oslot).wait()

        live = _iota((ROW_BLOCK, D_MODEL), 0) < n_rows
        x = jnp.where(live, _from_row_slabs(xbuf_ref, slot * ROW_BLOCK, ROW_BLOCK), 0.0).astype(BF16)
        hid = jax.nn.silu(_dot(x, w1b_ref[...])) * _dot(x, w3b_ref[...])
        _to_row_slabs(obuf_ref, oslot * ROW_BLOCK, _dot(hid.astype(BF16), w2b_ref[...]))
        out_copy(t, oslot).start()
        return carry

    lax.fori_loop(0, n_own, tile_body, 0)
    for oslot in range(OUT_SLOTS):
        @pl.when(n_own > oslot)
        def _():
            out_copy(tile0, oslot).wait()

    @pl.when(expert == N_EXPERTS - 1)
    def _():
        obuf_ref[...] = jnp.zeros(obuf_ref.shape, F32)

        def fill(t, carry):
            out_copy(t, 0).start()
            out_copy(t, 0).wait()
            return carry

        lax.fori_loop(n_tiles, n_tiles_max, fill, 0)


def _expert_call(l, plan, hs_c, hs_l, w1, w3, w2):
    tables = plan["expert_tables"]
    n_tiles_max = tables[0].shape[0]
    n_ctx_tiles = hs_c.shape[0] // (PAIR_BLOCK * ROW_SLABS)

    def weight(shape):
        return pl.BlockSpec((1, 1) + shape, lambda e, *_: (l, e, 0, 0))

    return pl.pallas_call(
        functools.partial(_expert_kernel, n_ctx_tiles, n_tiles_max),
        grid_spec=pltpu.PrefetchScalarGridSpec(
            num_scalar_prefetch=N_EXPERT_TABLES,
            grid=(N_EXPERTS,),
            in_specs=[pl.BlockSpec(memory_space=pl.ANY), pl.BlockSpec(memory_space=pl.ANY),
                      weight((D_MODEL, D_EXPERT)), weight((D_MODEL, D_EXPERT)), weight((D_EXPERT, D_MODEL))],
            out_specs=pl.BlockSpec(memory_space=pl.ANY),
            scratch_shapes=[pltpu.VMEM((GATHER_SLOTS * ROW_BLOCK * ROW_SLABS, LANES), F32),
                            pltpu.VMEM((OUT_SLOTS * ROW_BLOCK * ROW_SLABS, LANES), F32),
                            pltpu.VMEM((D_MODEL, D_EXPERT), BF16), pltpu.VMEM((D_MODEL, D_EXPERT), BF16),
                            pltpu.VMEM((D_EXPERT, D_MODEL), BF16),
                            pltpu.SemaphoreType.DMA((GATHER_SLOTS,)), pltpu.SemaphoreType.DMA((OUT_SLOTS,))],
        ),
        out_shape=jax.ShapeDtypeStruct((n_tiles_max * ROW_BLOCK * ROW_SLABS, LANES), F32),
        compiler_params=pltpu.CompilerParams(
            dimension_semantics=("arbitrary",), vmem_limit_bytes=SMALL_KERNEL_VMEM_LIMIT),
        name="moe_experts",
    )(*tables, hs_c, hs_l, w1, w3, w2)


N_COMBINE_TABLES = 4


def _combine_kernel(n_ctx_tiles, cnt_ref, cpre_ref, lofs_ref, starts_ref, x_c_ref, x_l_ref, slot_c_ref, slot_l_ref,
                    wt_c_ref, wt_l_ref, mod_ref, ys_ref, xo_c_ref, xo_l_ref, buf_ref, sem):
    j = pl.program_id(0)
    n_tiles = pl.num_programs(0)

    def collect(t, act):
        slot = t % GATHER_SLOTS

        def body(e, carry):
            k = t * N_EXPERTS + e
            src, dst = starts_ref[e] + cpre_ref[k], slot * PAIR_BLOCK + lofs_ref[k]
            _segment_copies(cnt_ref[k], lambda done, size: pltpu.make_async_copy(
                _row_span(ys_ref, src + done, size), _row_span(buf_ref, dst + done, size), sem.at[slot]), act)
            return carry

        lax.fori_loop(0, N_EXPERTS, body, 0)

    @pl.when(j == 0)
    def _():
        for t in range(GATHER_AHEAD):
            collect(t, _start)

    @pl.when(j + GATHER_AHEAD < n_tiles)
    def _():
        collect(j + GATHER_AHEAD, _start)

    slot = j % GATHER_SLOTS
    for part in range(PAIR_BLOCK // ROW_BLOCK):
        pltpu.make_async_copy(
            _row_span(ys_ref, part * ROW_BLOCK, ROW_BLOCK),
            _row_span(buf_ref, slot * PAIR_BLOCK + part * ROW_BLOCK, ROW_BLOCK), sem.at[slot]).wait()
    rows = _from_row_slabs(buf_ref, slot * PAIR_BLOCK, PAIR_BLOCK)
    gate = mod_ref[0, 0, 5:6, :]

    def finish(x_ref, slot_ref, wt_ref, xo_ref):
        slots, wts = slot_ref[...].astype(F32), wt_ref[...]
        slot1, slot2 = slots[0:1], slots[1:2]
        weight_of_row = jnp.sum(_slot_one_hot(slot1, slot2, wts[0:1], wts[1:2]), axis=1, keepdims=True)
        hi, lo = _split2(rows * weight_of_row)
        perm = _slot_one_hot(slot1, slot2, 1.0, 1.0).astype(BF16)
        y = _dot_tn(perm, hi) + _dot_tn(perm, lo)
        xo_ref[...] = x_ref[...] + gate * y

    @pl.when(j < n_ctx_tiles)
    def _():
        finish(x_c_ref, slot_c_ref, wt_c_ref, xo_c_ref)

    @pl.when(j >= n_ctx_tiles)
    def _():
        finish(x_l_ref, slot_l_ref, wt_l_ref, xo_l_ref)


def _combine_call(l, plan, x_c, x_l, slot_c, slot_l, wt_c, wt_l, mods_all, mod_row_of_tile, ys):
    n_tiles = (x_c.shape[0] + x_l.shape[0]) // ROW_BLOCK
    n_ctx_tiles = x_c.shape[0] // ROW_BLOCK
    first, second = _two_streams(n_ctx_tiles)

    def lanes(index_map):
        return lambda i, *_: index_map(i)[::-1]

    return pl.pallas_call(
        functools.partial(_combine_kernel, n_ctx_tiles),
        grid_spec=pltpu.PrefetchScalarGridSpec(
            num_scalar_prefetch=N_COMBINE_TABLES,
            grid=(n_tiles,),
            in_specs=[pl.BlockSpec((ROW_BLOCK, D_MODEL), first),
                      pl.BlockSpec((ROW_BLOCK, D_MODEL), second),
                      pl.BlockSpec((2, ROW_BLOCK), lanes(first)),
                      pl.BlockSpec((2, ROW_BLOCK), lanes(second)),
                      pl.BlockSpec((2, ROW_BLOCK), lanes(first)),
                      pl.BlockSpec((2, ROW_BLOCK), lanes(second)),
                      pl.BlockSpec((1, 1, 6, D_MODEL), lambda i, *_: (l, mod_row_of_tile(i), 0, 0)),
                      pl.BlockSpec(memory_space=pl.ANY)],
            out_specs=[pl.BlockSpec((ROW_BLOCK, D_MODEL), first),
                       pl.BlockSpec((ROW_BLOCK, D_MODEL), second)],
            scratch_shapes=[pltpu.VMEM((GATHER_SLOTS * PAIR_BLOCK * ROW_SLABS, LANES), F32),
                            pltpu.SemaphoreType.DMA((GATHER_SLOTS,))],
        ),
        out_shape=[jax.ShapeDtypeStruct(x_c.shape, F32), jax.ShapeDtypeStruct(x_l.shape, F32)],
        compiler_params=pltpu.CompilerParams(
            dimension_semantics=("arbitrary",), vmem_limit_bytes=SMALL_KERNEL_VMEM_LIMIT),
        name="moe_combine",
    )(*plan["combine_tables"], x_c, x_l, slot_c, slot_l, wt_c, wt_l, mods_all, ys)


def _moe_plan(cnt):
    n_tok_tiles = cnt.shape[0]
    n_tiles = n_tok_tiles * PAIR_BLOCK // ROW_BLOCK + N_EXPERTS
    lofs = jnp.cumsum(cnt, axis=1) - cnt
    cpre = jnp.cumsum(cnt, axis=0) - cnt
    counts = jnp.sum(cnt, axis=0)
    padded = (counts + ROW_BLOCK - 1) // ROW_BLOCK * ROW_BLOCK
    ends = jnp.cumsum(padded)
    starts = ends - padded
    tile_start = jnp.arange(n_tiles, dtype=jnp.int32) * ROW_BLOCK
    tile_expert = jnp.minimum(
        jnp.sum((tile_start[:, None] >= ends[None, :]).astype(jnp.int32), axis=1), N_EXPERTS - 1)
    hot = tile_expert[:, None] == jnp.arange(N_EXPERTS, dtype=jnp.int32)[None, :]
    first = tile_start - jnp.sum(jnp.where(hot, starts[None, :], 0), axis=1)
    rows = jnp.clip(jnp.sum(jnp.where(hot, counts[None, :], 0), axis=1) - first, 0, ROW_BLOCK)
    seg_first = jnp.sum(jnp.where(hot[:, None, :], cpre[None, :, :], 0), axis=2)
    seg_rows = jnp.sum(jnp.where(hot[:, None, :], cnt[None, :, :], 0), axis=2)
    overlap = (seg_first < (first + rows)[:, None]) & (seg_first + seg_rows > first[:, None])
    j = jnp.arange(n_tok_tiles, dtype=jnp.int32)[None, :]
    jlo = jnp.min(jnp.where(overlap, j, n_tok_tiles), axis=1)
    jhi = jnp.max(jnp.where(overlap, j + 1, 0), axis=1)
    i32 = lambda a: a.astype(jnp.int32).reshape(-1)
    tile0 = jnp.concatenate([starts, ends[-1:]]) // ROW_BLOCK
    return {
        "expert_tables": tuple(i32(a) for a in (tile_expert, first, rows, jlo, jhi, cpre, cnt, lofs,
                                                tile0, padded // ROW_BLOCK)),
        "combine_tables": tuple(i32(a) for a in (cnt, cpre, lofs, starts)),
    }


def _rope_tables(n_tok):
    n_rows = n_tok // GRID_W
    pos_r = jnp.repeat(jnp.arange(n_rows), GRID_W)
    pos_c = jnp.tile(jnp.arange(GRID_W), n_rows)
    half = DIFF_QK // 2
    nf = half // 2
    freqs = ROPE_BASE ** (-jnp.arange(nf, dtype=F32) / nf)

    def tables(pos):
        ang = pos.astype(F32)[:, None] * freqs
        cos, sin = jnp.cos(ang), jnp.sin(ang)
        return jnp.concatenate([cos, cos], axis=-1), jnp.concatenate([-sin, sin], axis=-1)

    cos_r, sin_r = tables(pos_r)
    cos_c, sin_c = tables(pos_c)
    cos = jnp.concatenate([cos_r, cos_c], axis=-1)
    sin = jnp.concatenate([sin_r, sin_c], axis=-1)
    return jnp.concatenate([cos, cos], axis=-1), jnp.concatenate([sin, sin], axis=-1)


def _mixer_weights(w_in, w_out, sgu_w, sgu_b, q_norm_g, k_norm_g, diff_lambda, diff_norm_g, gla_w2, gla_b,
                   gla_norm_g, norm1_g, norm2_g, router_w, router_bias):
    w_in_pad = jnp.pad(w_in.astype(BF16), ((0, 0), (0, 0), (0, D_PROJ_PAD - w_in.shape[2])))
    w2cat = jnp.zeros((DEPTH, LANES, 2 * W_GLA), F32)
    w2cat = w2cat.at[:, 0:GLA_RANK, 0:W_GLA].set(gla_w2[:, 0]).at[:, GLA_RANK:2 * GLA_RANK, W_GLA:].set(gla_w2[:, 1])
    return (
        norm1_g[:, None, :], norm2_g[:, None, :], w_in_pad, w_out.astype(BF16),
        sgu_w.astype(BF16), jnp.repeat(sgu_b.transpose(0, 2, 1), SGU_GROUP_W, axis=2),
        jnp.tile(q_norm_g, (1, W_QK // DIFF_QK))[:, None, :], jnp.tile(k_norm_g, (1, W_QK // DIFF_QK))[:, None, :],
        diff_lambda, diff_norm_g[:, None, :],
        w2cat.astype(BF16), gla_b.reshape(DEPTH, 1, 2 * W_GLA), jnp.tile(gla_norm_g, (1, GLA_HEADS))[:, None, :],
        router_w.T, router_bias[:, None],
    )


def kernel(x_prompt, x_sample, cache_k, cache_v, state_gla, c, c_ctx, w_in, w_out, sgu_w, sgu_b, q_norm_g, k_norm_g,
           diff_lambda, diff_norm_g, gla_w2, gla_b, gla_norm_g, norm1_g, norm2_g, ada_w, ada_b, router_w, router_bias,
           moe_w1, moe_w3, moe_w2):
    n_ctx_seq, ctx_len, _ = x_prompt.shape
    n_lat_seq, lat_len, _ = x_sample.shape
    n_ctx_tok = n_ctx_seq * ctx_len
    n_lat_tok = n_lat_seq * lat_len
    ctx_tiles = n_ctx_tok // ROW_BLOCK
    lat_tiles_per_seq = lat_len // ROW_BLOCK

    n_cond = 1 + n_lat_seq
    cond_t = jnp.zeros((D_MODEL, SUBLANES), F32).at[:, 0].set(c_ctx).at[:, 1:n_cond].set(c.T)
    mods_all = _adaln_call(cond_t, n_cond, ada_w, ada_b)[:, :n_cond].reshape(DEPTH, n_cond, 6, D_MODEL)
    weights = _mixer_weights(w_in, w_out, sgu_w, sgu_b, q_norm_g, k_norm_g, diff_lambda, diff_norm_g, gla_w2, gla_b,
                             gla_norm_g, norm1_g, norm2_g, router_w, router_bias)

    ck_all = cache_k.transpose(0, 1, 2, 4, 3, 5).reshape(cache_k.shape[:3] + (cache_k.shape[4], DIFF_V))
    st_all = jnp.einsum('bldhkv,hg->bldhvgk', state_gla, jnp.eye(GLA_HEADS, dtype=F32)).reshape(
        n_lat_seq, DEPTH, 2, W_GLA, W_GLA)
    cos, sin = _rope_tables(lat_len)
    extras = (ck_all, cache_v, st_all, cos, sin)

    def mod_row_of_tile(i):
        return jnp.where(i < ctx_tiles, 0, 1 + (i - ctx_tiles) // lat_tiles_per_seq)

    x_c = x_prompt.reshape(n_ctx_tok, D_MODEL)
    x_l = x_sample.reshape(n_lat_tok, D_MODEL)
    cache_bufs = ()
    for l in range(DEPTH):
        ctx_par = 1 if l == 0 else CTX_SEQS_PER_STEP
        x1_c, hs_c, slot_c, wt_c, cnt_c, *cache_bufs = _mixer_call(
            l, ctx_len, ctx_par, False, x_c, mods_all, weights, None, tuple(cache_bufs))
        x1_l, hs_l, slot_l, wt_l, cnt_l = _mixer_call(l, lat_len, 1, True, x_l, mods_all, weights, extras, ())
        plan = _moe_plan(jnp.concatenate([cnt_c[:, :, 0], cnt_l[:, :, 0]], axis=0))
        ys = _expert_call(l, plan, hs_c, hs_l, moe_w1, moe_w3, moe_w2)
        x_c, x_l = _combine_call(l, plan, x1_c, x1_l, slot_c, slot_l, wt_c, wt_l, mods_all, mod_row_of_tile, ys)

    new_k, new_v, new_s = cache_bufs
    return (x_c.reshape(x_prompt.shape), x_l.reshape(x_sample.shape), new_k, new_v, new_s)
```

```python
import functools
import math

import jax
import jax.numpy as jnp
from jax import lax
from jax.experimental import pallas as pl
from jax.experimental.pallas import tpu as pltpu

F32 = jnp.float32
BF16 = jnp.bfloat16

D_MODEL = 1024
DEPTH = 4
GRID_W = 64
SGU_GROUPS = 4
SGU_GROUP_W = 64
SGU_W = SGU_GROUPS * SGU_GROUP_W
SGU_CHUNK = 128
DIFF_HEADS = 4
DIFF_QK = 64
DIFF_V = 2 * DIFF_QK
ROPE_BASE = 10000.0
GLA_HEADS = 4
GLA_DK = 64
GLA_DV = 64
GLA_RANK = 16
GLA_GATE_NORM = 16.0
GLA_CHUNK = 64
N_EXPERTS = 16
N_GROUPS = 4
EXPERTS_PER_GROUP = N_EXPERTS // N_GROUPS
D_EXPERT = 512
EPS = 1e-6

LANES = 128
SUBLANES = 8
MXU_DIM = 256
V7X_VMEM_BYTES = 64 * 1024 * 1024
MIXER_VMEM_LIMIT = V7X_VMEM_BYTES * 7 // 8
SMALL_KERNEL_VMEM_LIMIT = V7X_VMEM_BYTES * 5 // 8

ROW_BLOCK = MXU_DIM
ROW_SLABS = D_MODEL // LANES

W_QK = DIFF_HEADS * 2 * DIFF_QK
W_GLA = GLA_HEADS * GLA_DK
C_AU, C_AV = 0, SGU_W
C_BQ = C_AV + SGU_W
C_BK, C_BV = C_BQ + W_QK, C_BQ + 2 * W_QK
C_CQ = C_BV + DIFF_HEADS * DIFF_V
C_CK, C_CV, C_CR, C_LR = C_CQ + W_GLA, C_CQ + 2 * W_GLA, C_CQ + 3 * W_GLA, C_CQ + 4 * W_GLA
D_PROJ_MAIN = C_LR
D_PROJ_PAD = D_PROJ_MAIN + LANES
M_A, M_B, M_C = 0, SGU_W, SGU_W + DIFF_HEADS * DIFF_V


def _split2(x):
    hi = x.astype(BF16)
    lo = (x - hi.astype(F32)).astype(BF16)
    return hi, lo


def _split3(x):
    hi = x.astype(BF16)
    r = x - hi.astype(F32)
    mid = r.astype(BF16)
    lo = (r - mid.astype(F32)).astype(BF16)
    return hi, mid, lo


def _dot(a, b):
    return jnp.dot(a, b, preferred_element_type=F32)


def _dot_nt(a, b):
    return lax.dot_general(a, b, (((1,), (1,)), ((), ())), preferred_element_type=F32)


def _dot_tn(a, b):
    return lax.dot_general(a, b, (((0,), (0,)), ((), ())), preferred_element_type=F32)


def _iota(shape, dim):
    return lax.broadcasted_iota(jnp.int32, shape, dim)


def _block_ones(width, block):
    r = _iota((width, width), 0) // block
    c = _iota((width, width), 1) // block
    return (r == c)


def _group_sum(z, block):
    width = z.shape[-1]
    outs = []
    for s in range(0, width, MXU_DIM):
        w = min(MXU_DIM, width - s)
        ones = _block_ones(w, block).astype(BF16)
        hi, lo = _split2(z[:, s:s + w])
        outs.append(_dot(hi, ones) + _dot(lo, ones))
    return outs[0] if len(outs) == 1 else jnp.concatenate(outs, axis=-1)


def _group_rms(z, block):
    ms = _group_sum(z * z, block) * (1.0 / block)
    return z * lax.rsqrt(ms + EPS)


def _row_rms(z):
    return z * lax.rsqrt(jnp.mean(z * z, axis=-1, keepdims=True) + EPS)


def _log_sigmoid(x):
    return jnp.minimum(x, 0.0) - jnp.log1p(jnp.exp(-jnp.abs(x)))


ADA_COLS = 1536


def _adaln_kernel(n_cond, cond_t_ref, w_ref, b_ref, o_ref):
    sc = jax.nn.silu(cond_t_ref[...])
    w = w_ref[0]
    rows = [jnp.sum(sc[:, r:r + 1] * w, axis=0, keepdims=True) + b_ref[0] for r in range(n_cond)]
    o_ref[0] = jnp.concatenate(rows + [jnp.zeros((SUBLANES - n_cond, w.shape[1]), F32)], axis=0)


def _adaln_call(cond_t, n_cond, ada_w, ada_b):
    n_col = 6 * D_MODEL // ADA_COLS
    return pl.pallas_call(
        functools.partial(_adaln_kernel, n_cond),
        grid=(DEPTH, n_col),
        in_specs=[
            pl.BlockSpec((D_MODEL, SUBLANES), lambda l, j: (0, 0)),
            pl.BlockSpec((1, D_MODEL, ADA_COLS), lambda l, j: (l, 0, j)),
            pl.BlockSpec((1, 1, ADA_COLS), lambda l, j: (l, 0, j)),
        ],
        out_specs=pl.BlockSpec((1, SUBLANES, ADA_COLS), lambda l, j: (l, 0, j)),
        out_shape=jax.ShapeDtypeStruct((DEPTH, SUBLANES, 6 * D_MODEL), F32),
        compiler_params=pltpu.CompilerParams(
            dimension_semantics=("arbitrary", "arbitrary"), vmem_limit_bytes=SMALL_KERNEL_VMEM_LIMIT),
        name="adaln",
    )(cond_t, ada_w, ada_b.reshape(DEPTH, 1, 6 * D_MODEL))


def _route(hn, rwt_ref, rb_ref):
    h_hi, h_lo = _split2(hn)
    rw = rwt_ref[...]
    rw_hi = rw.astype(BF16)
    rw_lo = (rw - rw_hi.astype(F32)).astype(BF16)
    logits = _dot_nt(rw_hi, h_hi) + _dot_nt(rw_hi, h_lo) + _dot_nt(rw_lo, h_hi)
    aff = jax.nn.sigmoid(logits)
    sel = aff + rb_ref[...]
    n_tok = sel.shape[1]

    def top2_sum(a, b, c, d):
        hi1, lo1 = jnp.maximum(a, b), jnp.minimum(a, b)
        hi2, lo2 = jnp.maximum(c, d), jnp.minimum(c, d)
        return jnp.maximum(hi1, hi2) + jnp.maximum(jnp.minimum(hi1, hi2), jnp.maximum(lo1, lo2))

    scores = []
    for g in range(N_GROUPS):
        rows = [sel[EXPERTS_PER_GROUP * g + j:EXPERTS_PER_GROUP * g + j + 1, :] for j in range(EXPERTS_PER_GROUP)]
        scores.append(top2_sum(*rows))
    best = jnp.zeros((1, n_tok), jnp.int32)
    best_score = scores[0]
    for g in range(1, N_GROUPS):
        upd = scores[g] > best_score
        best = jnp.where(upd, g, best)
        best_score = jnp.where(upd, scores[g], best_score)

    eid_i = _iota((N_EXPERTS, n_tok), 0)
    eid = eid_i.astype(F32)
    neg = jnp.float32(-jnp.inf)
    msel = jnp.where(eid_i // EXPERTS_PER_GROUP == best, sel, neg)
    m1 = jnp.max(msel, axis=0, keepdims=True)
    idx1 = jnp.min(jnp.where(msel == m1, eid, float(N_EXPERTS)), axis=0, keepdims=True)
    msel2 = jnp.where(eid == idx1, neg, msel)
    m2 = jnp.max(msel2, axis=0, keepdims=True)
    idx2 = jnp.min(jnp.where(msel2 == m2, eid, float(N_EXPERTS)), axis=0, keepdims=True)
    w1 = jnp.sum(jnp.where(eid == idx1, aff, 0.0), axis=0, keepdims=True)
    w2 = jnp.sum(jnp.where(eid == idx2, aff, 0.0), axis=0, keepdims=True)
    wsum = w1 + w2
    return idx1.astype(jnp.int32), idx2.astype(jnp.int32), w1 / wsum, w2 / wsum


def _local_slots(idx1, idx2):
    n_tok = idx1.shape[1]
    eid = _iota((N_EXPERTS, n_tok), 0)
    hot1, hot2 = eid == idx1, eid == idx2
    hot = jnp.where(hot1, 1.0, jnp.where(hot2, 1.0, 0.0))
    earlier = jnp.where(_iota((n_tok, n_tok), 0) < _iota((n_tok, n_tok), 1), 1.0, 0.0).astype(BF16)
    before_in_expert = _dot(hot.astype(BF16), earlier)
    counts = jnp.sum(hot, axis=1, keepdims=True)
    lower = jnp.where(_iota((N_EXPERTS, N_EXPERTS), 1) < _iota((N_EXPERTS, N_EXPERTS), 0), 1.0, 0.0).astype(BF16)
    first_slot = _dot(lower, jnp.broadcast_to(counts, (N_EXPERTS, LANES)).astype(BF16))[:, 0:1]
    slot = before_in_expert + first_slot
    slot1 = jnp.sum(jnp.where(hot1, slot, 0.0), axis=0, keepdims=True)
    slot2 = jnp.sum(jnp.where(hot2, slot, 0.0), axis=0, keepdims=True)
    return slot1, slot2, counts


def _slot_one_hot(slot1, slot2, v1, v2):
    n_tok = slot1.shape[1]
    row = _iota((2 * n_tok, n_tok), 0).astype(F32)
    return jnp.where(row == slot1, v1, jnp.where(row == slot2, v2, 0.0))


N_MIXER_WEIGHTS = 15
CTX_SEQS_PER_STEP = 2
MAX_INLINE_BLOCKS = 2


def _mixer_kernel(n_tok, n_par, latent, n_alias, lam_init, *refs):
    it = iter(refs)
    x_ref, mod_ref = next(it), next(it)
    (n1_ref, n2_ref, win_ref, wout_ref, sw_ref, sb_ref, qg_ref, kg_ref, dl_ref, dg_ref,
     w2c_ref, gb_ref, gg_ref, rwt_ref, rb_ref) = (next(it) for _ in range(N_MIXER_WEIGHTS))
    if latent:
        ck_ref, cv_ref, st0_ref, cos_ref, sin_ref = (next(it) for _ in range(5))
    for _ in range(n_alias):
        next(it)
    xo_ref, hs_ref, slot_ref, wt_ref, cnt_ref = (next(it) for _ in range(5))
    if not latent:
        ko_ref, vo_ref, so_ref = (next(it) for _ in range(3))
    proj_ref, mix_ref, q_ref, k_ref, v_ref = (next(it) for _ in range(5))
    gq_ref, gke_ref, gv_ref, gr_ref, dec_ref, go_ref, st_ref = (next(it) for _ in range(7))

    n_blk = n_tok // ROW_BLOCK
    n_ctx = k_ref.shape[0] - n_par * n_tok
    n_keys = n_ctx + n_tok
    mod = mod_ref[0, 0]

    by_block = latent

    def whole_sequence(fn):
        def run():
            fn()

        if by_block:
            pl.when(pl.program_id(1) == 0)(run)
        else:
            run()

    def blocks(body):
        if n_par * n_blk <= MAX_INLINE_BLOCKS:
            for r in range(n_par * n_blk):
                body(r)
        else:
            def step(r, carry):
                body(r)
                return carry
            whole_sequence(lambda: lax.fori_loop(0, n_par * n_blk, step, 0))

    def aligned(start, size):
        return pl.ds(start if isinstance(start, int) else pl.multiple_of(start, size), size)

    def block_rows(r, offset=0):
        return aligned(offset + r * ROW_BLOCK, ROW_BLOCK)

    if not latent:
        for ref in (ko_ref, vo_ref, so_ref):
            for q in range(n_par):
                for other in range(1, ref.shape[1]):
                    ref[q, other] = jnp.zeros(ref.shape[2:], F32)

    lane_group = _iota((SGU_CHUNK, SGU_W), 1) // SGU_GROUP_W
    blk_r = _iota((ROW_BLOCK, ROW_BLOCK), 0)
    blk_c = _iota((ROW_BLOCK, ROW_BLOCK), 1)
    same_chunk = (blk_r // GLA_CHUNK) == (blk_c // GLA_CHUNK)
    tri = (jnp.where(same_chunk & (blk_c <= blk_r), 1.0, 0.0).astype(BF16),
           jnp.where(same_chunk & (blk_c >= blk_r), 1.0, 0.0).astype(BF16))
    chunks_per_blk = ROW_BLOCK // GLA_CHUNK
    head_of_lane = _iota((GLA_CHUNK, W_GLA), 1) // GLA_DK
    stack_r = _iota((GLA_HEADS * GLA_CHUNK, GLA_CHUNK), 0) % GLA_CHUNK
    stack_c = _iota((GLA_HEADS * GLA_CHUNK, GLA_CHUNK), 1)
    causal = (stack_c <= stack_r, stack_c >= stack_r)

    if latent:
        def cached_context():
            for h in range(DIFF_HEADS):
                k_ref[0:n_ctx, h * DIFF_V:(h + 1) * DIFF_V] = ck_ref[0, 0, h].astype(BF16)
                v_ref[0:n_ctx, h * DIFF_V:(h + 1) * DIFF_V] = cv_ref[0, 0, h].astype(BF16)
            st_ref[0] = st0_ref[0, 0]

        whole_sequence(cached_context)
        pair_lo = (_iota((ROW_BLOCK, W_QK), 1) % (DIFF_QK // 2)) < (DIFF_QK // 4)

        def rope(z, rows):
            cos = jnp.concatenate([cos_ref[rows, :]] * DIFF_HEADS, axis=-1)
            sin = jnp.concatenate([sin_ref[rows, :]] * DIFF_HEADS, axis=-1)
            shift = DIFF_QK // 4
            swapped = jnp.where(pair_lo, pltpu.roll(z, W_QK - shift, 1), pltpu.roll(z, shift, 1))
            return z * cos + swapped * sin

    def modulated_input(r):
        h = _row_rms(x_ref[block_rows(r), :]) * n1_ref[0]
        return (h * (1.0 + mod[1:2, :]) + mod[0:1, :]).astype(BF16)

    def spatial_gating(r):
        for c in range(ROW_BLOCK // SGU_CHUNK):
            local = slice(c * SGU_CHUNK, (c + 1) * SGU_CHUNK)
            u = jax.nn.gelu(proj_ref[local, C_AU:C_AU + SGU_W])
            v = _group_rms(jax.nn.gelu(proj_ref[local, C_AV:C_AV + SGU_W]), SGU_GROUP_W).astype(BF16)
            s = sb_ref[0]
            for g in range(SGU_GROUPS):
                s = s + jnp.where(lane_group == g, _dot(sw_ref[0, g], v), 0.0)
            mix_ref[aligned(r * ROW_BLOCK + c * SGU_CHUNK, SGU_CHUNK), M_A:M_A + SGU_W] = (u * s).astype(BF16)

    def attention_operands(r):
        rows = block_rows(r)
        key_rows = block_rows(r, n_ctx)
        seq, seq_rows = r // n_blk, block_rows(r % n_blk)
        qn = _group_rms(proj_ref[:, C_BQ:C_BQ + W_QK], DIFF_QK) * qg_ref[0]
        kn = _group_rms(proj_ref[:, C_BK:C_BK + W_QK], DIFF_QK) * kg_ref[0]
        vv = proj_ref[:, C_BV:C_BV + W_QK]
        if latent:
            qn, kn = rope(qn, rows), rope(kn, rows)
        else:
            for h in range(DIFF_HEADS):
                for i in range(2):
                    lo = h * DIFF_V + i * DIFF_QK
                    ko_ref[seq, 0, h, i, seq_rows, :] = kn[:, lo:lo + DIFF_QK]
                vo_ref[seq, 0, h, seq_rows, :] = vv[:, h * DIFF_V:(h + 1) * DIFF_V]
        q_ref[rows, :] = (qn * (DIFF_QK ** -0.5)).astype(BF16)
        k_ref[key_rows, :] = kn.astype(BF16)
        v_ref[key_rows, :] = vv.astype(BF16)

    def gla_operands(r):
        rows = block_rows(r)
        gpre = _dot(proj_ref[:, C_LR:C_LR + LANES].astype(BF16), w2c_ref[0]) + gb_ref[0]
        gate = _log_sigmoid(gpre) * (1.0 / GLA_GATE_NORM)
        gq = proj_ref[:, C_CQ:C_CQ + W_GLA] * (GLA_DK ** -0.5)
        gk = proj_ref[:, C_CK:C_CK + W_GLA]
        gv = proj_ref[:, C_CV:C_CV + W_GLA].astype(BF16)
        gv_ref[rows, :] = gv
        gr_ref[rows, :] = proj_ref[:, C_CR:C_CR + W_GLA]
        for d in range(2):
            g = gate[:, d * W_GLA:(d + 1) * W_GLA]
            b = sum(_dot(tri[d], p) for p in _split3(g))
            last = GLA_CHUNK - 1 if d == 0 else 0
            b_last = jnp.concatenate(
                [jnp.broadcast_to(b[c * GLA_CHUNK + last:c * GLA_CHUNK + last + 1, :], (GLA_CHUNK, W_GLA))
                 for c in range(chunks_per_blk)], axis=0)
            q_dec = (gq * jnp.exp(b)).astype(BF16)
            k_inv = (gk * jnp.exp(-b)).astype(BF16)
            gq_ref[d, rows, :] = q_dec
            gke_ref[d, rows, :] = (gk * jnp.exp(b_last - b)).astype(BF16)
            for c in range(chunks_per_blk):
                row = c * GLA_CHUNK + last
                dec_ref[d, r * chunks_per_blk + c] = jnp.exp(b[row:row + 1, :])
                chunk = slice(c * GLA_CHUNK, (c + 1) * GLA_CHUNK)
                qd = q_dec[chunk]
                q_stack = jnp.concatenate(
                    [jnp.where(head_of_lane == h, qd, jnp.zeros_like(qd)) for h in range(GLA_HEADS)], axis=0)
                attn = jnp.where(causal[d], _dot_nt(q_stack, k_inv[chunk]), 0.0)
                spread = _dot(attn.astype(BF16), gv[chunk])
                o = jnp.zeros((GLA_CHUNK, W_GLA), F32)
                for h in range(GLA_HEADS):
                    o = o + jnp.where(head_of_lane == h, spread[h * GLA_CHUNK:(h + 1) * GLA_CHUNK, :], 0.0)
                go_ref[d, aligned(r * ROW_BLOCK + c * GLA_CHUNK, GLA_CHUNK), :] = o

    def project_and_split(r):
        proj_ref[...] = _dot(modulated_input(r), win_ref[0])
        spatial_gating(r)
        attention_operands(r)
        gla_operands(r)

    blocks(project_and_split)

    dl = dl_ref[0]
    lam = (jnp.exp(jnp.sum(dl[0:1] * dl[1:2], axis=-1, keepdims=True))
           - jnp.exp(jnp.sum(dl[2:3] * dl[3:4], axis=-1, keepdims=True)) + lam_init)
    sub0 = (_iota((ROW_BLOCK, DIFF_V), 1) < DIFF_QK)

    def softmax(s):
        e = jnp.exp(s - jnp.max(s, axis=-1, keepdims=True))
        return e, jnp.sum(e, axis=-1, keepdims=True)

    def attn_block(r):
        rows = block_rows(r)
        keys = aligned((r // n_blk) * n_keys, n_keys)
        for h in range(DIFF_HEADS):
            cols = slice(h * DIFF_V, (h + 1) * DIFF_V)
            qh = q_ref[rows, cols]
            kh = k_ref[keys, cols]
            e0, z0 = softmax(_dot_nt(jnp.where(sub0, qh, jnp.zeros_like(qh)), kh))
            e1, z1 = softmax(_dot_nt(jnp.where(sub0, jnp.zeros_like(qh), qh), kh))
            w = e0 / z0 - lam * (e1 / z1)
            o = _dot(w.astype(BF16), v_ref[keys, cols])
            o = _row_rms(o) * dg_ref[0] * (1.0 - lam_init)
            mix_ref[rows, M_B + h * DIFF_V:M_B + (h + 1) * DIFF_V] = o.astype(BF16)

    blocks(attn_block)

    if not latent:
        st_ref[...] = jnp.zeros(st_ref.shape, F32)

    n_chunk = n_tok // GLA_CHUNK
    st_diag = (_iota((W_GLA, W_GLA), 0) // GLA_DV) == (_iota((W_GLA, W_GLA), 1) // GLA_DK)

    def gla_step(c, carry):
        for seq in range(n_par):
            for d in range(2):
                cc = seq * n_chunk + (c if d == 0 else n_chunk - 1 - c)
                rows = pl.ds(pl.multiple_of(cc * GLA_CHUNK, GLA_CHUNK), GLA_CHUNK)
                st = st_ref[seq, d]
                go_ref[d, rows, :] = go_ref[d, rows, :] + _dot_nt(gq_ref[d, rows, :], st.astype(BF16))
                upd = _dot_tn(gv_ref[rows, :], gke_ref[d, rows, :])
                st_ref[seq, d] = dec_ref[d, cc] * st + jnp.where(st_diag, upd, 0.0)
        return carry

    whole_sequence(lambda: lax.fori_loop(0, n_chunk, gla_step, 0))

    if not latent:
        for seq in range(n_par):
            for d in range(2):
                s_full = st_ref[seq, d].T
                for h in range(GLA_HEADS):
                    so_ref[seq, 0, d, h] = s_full[h * GLA_DK:(h + 1) * GLA_DK, h * GLA_DV:(h + 1) * GLA_DV]

    def finish_block(r, out_r):
        rows, out_rows = block_rows(r), block_rows(out_r)
        oc = _group_rms(go_ref[0, rows, :] + go_ref[1, rows, :], GLA_DV) * gg_ref[0]
        oc = oc * jax.nn.silu(gr_ref[rows, :])
        mix_ref[rows, M_C:M_C + W_GLA] = oc.astype(BF16)
        x1 = x_ref[rows, :] + mod[2:3, :] * _dot(mix_ref[rows, :], wout_ref[0])
        xo_ref[out_rows, :] = x1
        hn = _row_rms(x1) * n2_ref[0]
        hn = hn * (1.0 + mod[4:5, :]) + mod[3:4, :]
        idx1, idx2, w1, w2 = _route(hn, rwt_ref, rb_ref)
        slot1, slot2, counts = _local_slots(idx1, idx2)
        perm = _slot_one_hot(slot1, slot2, 1.0, 1.0).astype(BF16)
        _to_row_slabs(hs_ref, 2 * out_r * ROW_BLOCK, _dot(perm, hn.astype(BF16)))
        slot_ref[:, out_rows] = jnp.concatenate([slot1, slot2], axis=0).astype(jnp.int32)
        wt_ref[:, out_rows] = jnp.concatenate([w1, w2], axis=0)
        cnt_ref[out_r] = jnp.broadcast_to(counts, (N_EXPERTS, LANES)).astype(jnp.int32)

    if by_block:
        finish_block(pl.program_id(1), 0)
    else:
        for r in range(n_par * n_blk):
            finish_block(r, r)


def _mixer_call(l, n_tok, n_par, latent, x, mods_all, weights, extras, cache_bufs):
    n_seq = x.shape[0] // n_tok
    n_all = x.shape[0]
    assert n_seq % n_par == 0 and not (latent and n_par > 1)
    n_step_tok = n_par * n_tok
    n_keys = n_step_tok + (extras[0].shape[3] if latent else 0)
    n_chunk = n_step_tok // GLA_CHUNK
    lam_init = 0.8 - 0.6 * math.exp(-0.3 * l)

    single = pl.Buffered(1)
    seq_mode = single if latent else None

    def layer(arr):
        tail = arr.shape[1:]
        return pl.BlockSpec((1,) + tail, lambda s, *_r, _n=len(tail): (l,) + (0,) * _n, pipeline_mode=single)

    def const(arr):
        return pl.BlockSpec(arr.shape, lambda s, *_r, _n=arr.ndim: (0,) * _n, pipeline_mode=single)

    def tok_spec(width):
        return pl.BlockSpec((n_step_tok, width), lambda s, *_r: (s, 0), pipeline_mode=seq_mode)

    mod_row = (lambda s: 1 + s) if latent else (lambda s: 0)
    in_specs = [tok_spec(D_MODEL),
                pl.BlockSpec((1, 1, 6, D_MODEL), lambda s, *_r: (l, mod_row(s), 0, 0))]
    in_specs += [layer(w) for w in weights[:N_MIXER_WEIGHTS - 2]] + [const(w) for w in weights[-2:]]
    operands = [x, mods_all] + list(weights)
    if latent:
        ck, cv, st0, cos, sin = extras
        in_specs += [
            pl.BlockSpec((1, 1) + ck.shape[2:], lambda s, *_r: (s, l, 0, 0, 0)),
            pl.BlockSpec((1, 1) + cv.shape[2:], lambda s, *_r: (s, l, 0, 0, 0)),
            pl.BlockSpec((1, 1) + st0.shape[2:], lambda s, *_r: (s, l, 0, 0, 0)),
            const(cos), const(sin),
        ]
        operands += [ck, cv, st0, cos, sin]
    n_in = len(operands)
    in_specs += [pl.BlockSpec(memory_space=pl.ANY)] * len(cache_bufs)
    operands += list(cache_bufs)

    tiles_per_step = n_step_tok // ROW_BLOCK
    out_shape = [
        jax.ShapeDtypeStruct((n_all, D_MODEL), F32),
        jax.ShapeDtypeStruct((2 * n_all * ROW_SLABS, LANES), F32),
        jax.ShapeDtypeStruct((2, n_all), jnp.int32),
        jax.ShapeDtypeStruct((2, n_all), F32),
        jax.ShapeDtypeStruct((n_all // ROW_BLOCK, N_EXPERTS, LANES), jnp.int32),
    ]
    if latent:
        grid = (n_seq, tiles_per_step)
        out_tok, out_tiles = ROW_BLOCK, 1
        at = lambda s, r: s * tiles_per_step + r
    else:
        grid = (n_seq // n_par,)
        out_tok, out_tiles = n_step_tok, tiles_per_step
        at = lambda s: s
    out_specs = [
        pl.BlockSpec((out_tok, D_MODEL), lambda *g: (at(*g), 0)),
        pl.BlockSpec((2 * out_tok * ROW_SLABS, LANES), lambda *g: (at(*g), 0)),
        pl.BlockSpec((2, out_tok), lambda *g: (0, at(*g))),
        pl.BlockSpec((2, out_tok), lambda *g: (0, at(*g))),
        pl.BlockSpec((out_tiles, N_EXPERTS, LANES), lambda *g: (at(*g), 0, 0)),
    ]
    n_shared_out = len(out_shape)
    aliases = {}
    if not latent:
        out_shape += [
            jax.ShapeDtypeStruct((n_seq, DEPTH, DIFF_HEADS, 2, n_tok, DIFF_QK), F32),
            jax.ShapeDtypeStruct((n_seq, DEPTH, DIFF_HEADS, n_tok, DIFF_V), F32),
            jax.ShapeDtypeStruct((n_seq, DEPTH, 2, GLA_HEADS, GLA_DK, GLA_DV), F32),
        ]
        n_lay, lay = (1, l) if cache_bufs else (DEPTH, 0)
        out_specs += [
            pl.BlockSpec((n_par, n_lay, DIFF_HEADS, 2, n_tok, DIFF_QK), lambda s: (s, lay, 0, 0, 0, 0)),
            pl.BlockSpec((n_par, n_lay, DIFF_HEADS, n_tok, DIFF_V), lambda s: (s, lay, 0, 0, 0)),
            pl.BlockSpec((n_par, n_lay, 2, GLA_HEADS, GLA_DK, GLA_DV), lambda s: (s, lay, 0, 0, 0, 0)),
        ]
        aliases = {n_in + j: n_shared_out + j for j in range(len(cache_bufs))}
    scratch = [
        pltpu.VMEM((ROW_BLOCK, D_PROJ_PAD), F32),
        pltpu.VMEM((n_step_tok, D_MODEL), BF16),
        pltpu.VMEM((n_step_tok, W_QK), BF16),
        pltpu.VMEM((n_keys, W_QK), BF16),
        pltpu.VMEM((n_keys, W_QK), BF16),
        pltpu.VMEM((2, n_step_tok, W_GLA), BF16),
        pltpu.VMEM((2, n_step_tok, W_GLA), BF16),
        pltpu.VMEM((n_step_tok, W_GLA), BF16),
        pltpu.VMEM((n_step_tok, W_GLA), F32),
        pltpu.VMEM((2, n_chunk, 1, W_GLA), F32),
        pltpu.VMEM((2, n_step_tok, W_GLA), F32),
        pltpu.VMEM((n_par, 2, W_GLA, W_GLA), F32),
    ]
    return pl.pallas_call(
        functools.partial(_mixer_kernel, n_tok, n_par, latent, len(cache_bufs), lam_init),
        grid=grid,
        in_specs=in_specs,
        out_specs=out_specs,
        out_shape=out_shape,
        scratch_shapes=scratch,
        input_output_aliases=aliases,
        compiler_params=pltpu.CompilerParams(
            dimension_semantics=("arbitrary",) * len(grid), vmem_limit_bytes=MIXER_VMEM_LIMIT),
        name="mixer_latent" if latent else "mixer_context",
    )(*operands)


PAIR_BLOCK = 2 * ROW_BLOCK
COPY_SIZES = tuple(ROW_BLOCK >> k for k in range(ROW_BLOCK.bit_length()))
LARGE_COPY = 64
GATHER_AHEAD = 2
GATHER_SLOTS = GATHER_AHEAD + 1


def _segment_copies(n_rows, make_copy, act):
    def copy_if_set(size):
        @pl.when((n_rows & size) != 0)
        def _():
            act(make_copy(n_rows & (-2 * size), size))

    n_large = COPY_SIZES.index(LARGE_COPY) + 1

    @pl.when(n_rows >= LARGE_COPY)
    def _():
        for size in COPY_SIZES[:n_large]:
            copy_if_set(size)

    for size in COPY_SIZES[n_large:]:
        copy_if_set(size)


def _start(copy):
    copy.start()


def _wait(copy):
    copy.wait()


def _slab_rows(first_row, n_rows, slab):
    return pl.ds(first_row * ROW_SLABS + slab, n_rows, stride=ROW_SLABS)


def _to_row_slabs(ref, first_row, value):
    for s in range(ROW_SLABS):
        ref[_slab_rows(first_row, value.shape[0], s), :] = value[:, s * LANES:(s + 1) * LANES]


def _from_row_slabs(ref, first_row, n_rows):
    return jnp.concatenate([ref[_slab_rows(first_row, n_rows, s), :] for s in range(ROW_SLABS)], axis=-1)


def _row_span(ref, first_row, n_rows):
    return ref.at[pl.ds(pl.multiple_of(first_row * ROW_SLABS, ROW_SLABS), n_rows * ROW_SLABS)]


def _two_streams(n_first_tiles):
    def first(i, *_):
        return (jnp.minimum(i, n_first_tiles - 1), 0)

    def second(i, *_):
        return (jnp.maximum(i - n_first_tiles, 0), 0)

    return first, second


N_EXPERT_TABLES = 10
OUT_SLOTS = 2


def _expert_kernel(n_ctx_tiles, n_tiles_max, te_ref, first_ref, rows_ref, jlo_ref, jhi_ref, cpre_ref, cnt_ref,
                   lofs_ref, tile0_ref, ntile_ref, hs_c_ref, hs_l_ref, w1_ref, w3_ref, w2_ref, ys_ref,
                   xbuf_ref, obuf_ref, w1b_ref, w3b_ref, w2b_ref, sem, out_sem):
    expert = pl.program_id(0)
    n_tiles = tile0_ref[N_EXPERTS]

    def gather(t, act):
        slot = t % GATHER_SLOTS
        e, first = te_ref[t], first_ref[t]
        last = first + rows_ref[t]

        def segment_of(hs_ref, first_tile):
            def body(j, carry):
                k = j * N_EXPERTS + e
                seg_first = cpre_ref[k]
                lo = jnp.maximum(seg_first, first)
                n = jnp.maximum(jnp.minimum(seg_first + cnt_ref[k], last) - lo, 0)
                src = (j - first_tile) * PAIR_BLOCK + lofs_ref[k] + (lo - seg_first)
                dst = slot * ROW_BLOCK + lo - first
                _segment_copies(n, lambda done, size: pltpu.make_async_copy(
                    _row_span(hs_ref, src + done, size), _row_span(xbuf_ref, dst + done, size), sem.at[slot]), act)
                return carry
            return body

        jlo, jhi = jlo_ref[t], jhi_ref[t]
        lax.fori_loop(jnp.minimum(jlo, n_ctx_tiles), jnp.minimum(jhi, n_ctx_tiles), segment_of(hs_c_ref, 0), 0)
        lax.fori_loop(jnp.maximum(jlo, n_ctx_tiles), jnp.maximum(jhi, n_ctx_tiles),
                      segment_of(hs_l_ref, n_ctx_tiles), 0)

    def out_copy(t, oslot):
        return pltpu.make_async_copy(
            _row_span(obuf_ref, oslot * ROW_BLOCK, ROW_BLOCK), _row_span(ys_ref, t * ROW_BLOCK, ROW_BLOCK),
            out_sem.at[oslot])

    @pl.when(expert == 0)
    def _():
        xbuf_ref[...] = jnp.zeros(xbuf_ref.shape, F32)
        for t in range(GATHER_AHEAD):
            gather(t, _start)

    w1b_ref[...] = w1_ref[0, 0].astype(BF16)
    w3b_ref[...] = w3_ref[0, 0].astype(BF16)
    w2b_ref[...] = w2_ref[0, 0].astype(BF16)
    tile0, n_own = tile0_ref[expert], ntile_ref[expert]

    def tile_body(k, carry):
        t = tile0 + k
        slot, oslot = t % GATHER_SLOTS, k % OUT_SLOTS

        @pl.when(t + GATHER_AHEAD < n_tiles)
        def _():
            gather(t + GATHER_AHEAD, _start)

        n_rows = rows_ref[t]
        _segment_copies(n_rows, lambda done, size: pltpu.make_async_copy(
            _row_span(hs_c_ref, done, size), _row_span(xbuf_ref, slot * ROW_BLOCK + done, size), sem.at[slot]), _wait)

        @pl.when(k >= OUT_SLOTS)
        def _():
            out_copy(t, oslot).wait()

        live = _iota((ROW_BLOCK, D_MODEL), 0) < n_rows
        x = jnp.where(live, _from_row_slabs(xbuf_ref, slot * ROW_BLOCK, ROW_BLOCK), 0.0).astype(BF16)
        hid = jax.nn.silu(_dot(x, w1b_ref[...])) * _dot(x, w3b_ref[...])
        _to_row_slabs(obuf_ref, oslot * ROW_BLOCK, _dot(hid.astype(BF16), w2b_ref[...]))
        out_copy(t, oslot).start()
        return carry

    lax.fori_loop(0, n_own, tile_body, 0)
    for oslot in range(OUT_SLOTS):
        @pl.when(n_own > oslot)
        def _():
            out_copy(tile0, oslot).wait()

    @pl.when(expert == N_EXPERTS - 1)
    def _():
        obuf_ref[...] = jnp.zeros(obuf_ref.shape, F32)

        def fill(t, carry):
            out_copy(t, 0).start()
            out_copy(t, 0).wait()
            return carry

        lax.fori_loop(n_tiles, n_tiles_max, fill, 0)


def _expert_call(l, plan, hs_c, hs_l, w1, w3, w2):
    tables = plan["expert_tables"]
    n_tiles_max = tables[0].shape[0]
    n_ctx_tiles = hs_c.shape[0] // (PAIR_BLOCK * ROW_SLABS)

    def weight(shape):
        return pl.BlockSpec((1, 1) + shape, lambda e, *_: (l, e, 0, 0))

    return pl.pallas_call(
        functools.partial(_expert_kernel, n_ctx_tiles, n_tiles_max),
        grid_spec=pltpu.PrefetchScalarGridSpec(
            num_scalar_prefetch=N_EXPERT_TABLES,
            grid=(N_EXPERTS,),
            in_specs=[pl.BlockSpec(memory_space=pl.ANY), pl.BlockSpec(memory_space=pl.ANY),
                      weight((D_MODEL, D_EXPERT)), weight((D_MODEL, D_EXPERT)), weight((D_EXPERT, D_MODEL))],
            out_specs=pl.BlockSpec(memory_space=pl.ANY),
            scratch_shapes=[pltpu.VMEM((GATHER_SLOTS * ROW_BLOCK * ROW_SLABS, LANES), F32),
                            pltpu.VMEM((OUT_SLOTS * ROW_BLOCK * ROW_SLABS, LANES), F32),
                            pltpu.VMEM((D_MODEL, D_EXPERT), BF16), pltpu.VMEM((D_MODEL, D_EXPERT), BF16),
                            pltpu.VMEM((D_EXPERT, D_MODEL), BF16),
                            pltpu.SemaphoreType.DMA((GATHER_SLOTS,)), pltpu.SemaphoreType.DMA((OUT_SLOTS,))],
        ),
        out_shape=jax.ShapeDtypeStruct((n_tiles_max * ROW_BLOCK * ROW_SLABS, LANES), F32),
        compiler_params=pltpu.CompilerParams(
            dimension_semantics=("arbitrary",), vmem_limit_bytes=SMALL_KERNEL_VMEM_LIMIT),
        name="moe_experts",
    )(*tables, hs_c, hs_l, w1, w3, w2)


N_COMBINE_TABLES = 4
COMBINE_TILES_PER_STEP = 2


def _combine_kernel(n_ctx_steps, cnt_ref, cpre_ref, lofs_ref, starts_ref, x_c_ref, x_l_ref, slot_c_ref, slot_l_ref,
                    wt_c_ref, wt_l_ref, mod_ref, ys_ref, xo_c_ref, xo_l_ref, buf_ref, sem):
    step = pl.program_id(0)
    n_tiles = pl.num_programs(0) * COMBINE_TILES_PER_STEP

    def collect(t, act):
        slot = t % GATHER_SLOTS

        def body(e, carry):
            k = t * N_EXPERTS + e
            src, dst = starts_ref[e] + cpre_ref[k], slot * PAIR_BLOCK + lofs_ref[k]
            _segment_copies(cnt_ref[k], lambda done, size: pltpu.make_async_copy(
                _row_span(ys_ref, src + done, size), _row_span(buf_ref, dst + done, size), sem.at[slot]), act)
            return carry

        lax.fori_loop(0, N_EXPERTS, body, 0)

    @pl.when(step == 0)
    def _():
        for t in range(GATHER_AHEAD):
            collect(t, _start)

    gate = mod_ref[0, 0, 5:6, :]

    def finish(part, rows, x_ref, slot_ref, wt_ref, xo_ref):
        tokens = pl.ds(part * ROW_BLOCK, ROW_BLOCK)
        slots, wts = slot_ref[:, tokens].astype(F32), wt_ref[:, tokens]
        slot1, slot2 = slots[0:1], slots[1:2]
        weight_of_row = jnp.sum(_slot_one_hot(slot1, slot2, wts[0:1], wts[1:2]), axis=1, keepdims=True)
        hi, lo = _split2(rows * weight_of_row)
        perm = _slot_one_hot(slot1, slot2, 1.0, 1.0).astype(BF16)
        y = _dot_tn(perm, hi) + _dot_tn(perm, lo)
        xo_ref[tokens, :] = x_ref[tokens, :] + gate * y

    for part in range(COMBINE_TILES_PER_STEP):
        j = step * COMBINE_TILES_PER_STEP + part

        @pl.when(j + GATHER_AHEAD < n_tiles)
        def _():
            collect(j + GATHER_AHEAD, _start)

        slot = j % GATHER_SLOTS
        for piece in range(PAIR_BLOCK // ROW_BLOCK):
            pltpu.make_async_copy(
                _row_span(ys_ref, piece * ROW_BLOCK, ROW_BLOCK),
                _row_span(buf_ref, slot * PAIR_BLOCK + piece * ROW_BLOCK, ROW_BLOCK), sem.at[slot]).wait()
        rows = _from_row_slabs(buf_ref, slot * PAIR_BLOCK, PAIR_BLOCK)

        @pl.when(step < n_ctx_steps)
        def _():
            finish(part, rows, x_c_ref, slot_c_ref, wt_c_ref, xo_c_ref)

        @pl.when(step >= n_ctx_steps)
        def _():
            finish(part, rows, x_l_ref, slot_l_ref, wt_l_ref, xo_l_ref)


def _combine_call(l, plan, x_c, x_l, slot_c, slot_l, wt_c, wt_l, mods_all, mod_row_of_tile, ys):
    step_rows = COMBINE_TILES_PER_STEP * ROW_BLOCK
    n_steps = (x_c.shape[0] + x_l.shape[0]) // step_rows
    n_ctx_steps = x_c.shape[0] // step_rows
    first, second = _two_streams(n_ctx_steps)

    def lanes(index_map):
        return lambda i, *_: index_map(i)[::-1]

    return pl.pallas_call(
        functools.partial(_combine_kernel, n_ctx_steps),
        grid_spec=pltpu.PrefetchScalarGridSpec(
            num_scalar_prefetch=N_COMBINE_TABLES,
            grid=(n_steps,),
            in_specs=[pl.BlockSpec((step_rows, D_MODEL), first),
                      pl.BlockSpec((step_rows, D_MODEL), second),
                      pl.BlockSpec((2, step_rows), lanes(first)),
                      pl.BlockSpec((2, step_rows), lanes(second)),
                      pl.BlockSpec((2, step_rows), lanes(first)),
                      pl.BlockSpec((2, step_rows), lanes(second)),
                      pl.BlockSpec((1, 1, 6, D_MODEL),
                                   lambda i, *_: (l, mod_row_of_tile(i * COMBINE_TILES_PER_STEP), 0, 0)),
                      pl.BlockSpec(memory_space=pl.ANY)],
            out_specs=[pl.BlockSpec((step_rows, D_MODEL), first),
                       pl.BlockSpec((step_rows, D_MODEL), second)],
            scratch_shapes=[pltpu.VMEM((GATHER_SLOTS * PAIR_BLOCK * ROW_SLABS, LANES), F32),
                            pltpu.SemaphoreType.DMA((GATHER_SLOTS,))],
        ),
        out_shape=[jax.ShapeDtypeStruct(x_c.shape, F32), jax.ShapeDtypeStruct(x_l.shape, F32)],
        compiler_params=pltpu.CompilerParams(
            dimension_semantics=("arbitrary",), vmem_limit_bytes=SMALL_KERNEL_VMEM_LIMIT),
        name="moe_combine",
    )(*plan["combine_tables"], x_c, x_l, slot_c, slot_l, wt_c, wt_l, mods_all, ys)


def _moe_plan(cnt):
    n_tok_tiles = cnt.shape[0]
    n_tiles = n_tok_tiles * PAIR_BLOCK // ROW_BLOCK + N_EXPERTS
    lofs = jnp.cumsum(cnt, axis=1) - cnt
    cpre = jnp.cumsum(cnt, axis=0) - cnt
    counts = jnp.sum(cnt, axis=0)
    padded = (counts + ROW_BLOCK - 1) // ROW_BLOCK * ROW_BLOCK
    ends = jnp.cumsum(padded)
    starts = ends - padded
    tile_start = jnp.arange(n_tiles, dtype=jnp.int32) * ROW_BLOCK
    tile_expert = jnp.minimum(
        jnp.sum((tile_start[:, None] >= ends[None, :]).astype(jnp.int32), axis=1), N_EXPERTS - 1)
    hot = tile_expert[:, None] == jnp.arange(N_EXPERTS, dtype=jnp.int32)[None, :]
    first = tile_start - jnp.sum(jnp.where(hot, starts[None, :], 0), axis=1)
    rows = jnp.clip(jnp.sum(jnp.where(hot, counts[None, :], 0), axis=1) - first, 0, ROW_BLOCK)
    seg_first = jnp.sum(jnp.where(hot[:, None, :], cpre[None, :, :], 0), axis=2)
    seg_rows = jnp.sum(jnp.where(hot[:, None, :], cnt[None, :, :], 0), axis=2)
    overlap = (seg_first < (first + rows)[:, None]) & (seg_first + seg_rows > first[:, None])
    j = jnp.arange(n_tok_tiles, dtype=jnp.int32)[None, :]
    jlo = jnp.min(jnp.where(overlap, j, n_tok_tiles), axis=1)
    jhi = jnp.max(jnp.where(overlap, j + 1, 0), axis=1)
    i32 = lambda a: a.astype(jnp.int32).reshape(-1)
    tile0 = jnp.concatenate([starts, ends[-1:]]) // ROW_BLOCK
    return {
        "expert_tables": tuple(i32(a) for a in (tile_expert, first, rows, jlo, jhi, cpre, cnt, lofs,
                                                tile0, padded // ROW_BLOCK)),
        "combine_tables": tuple(i32(a) for a in (cnt, cpre, lofs, starts)),
    }


def _rope_tables(n_tok):
    n_rows = n_tok // GRID_W
    pos_r = jnp.repeat(jnp.arange(n_rows), GRID_W)
    pos_c = jnp.tile(jnp.arange(GRID_W), n_rows)
    half = DIFF_QK // 2
    nf = half // 2
    freqs = ROPE_BASE ** (-jnp.arange(nf, dtype=F32) / nf)

    def tables(pos):
        ang = pos.astype(F32)[:, None] * freqs
        cos, sin = jnp.cos(ang), jnp.sin(ang)
        return jnp.concatenate([cos, cos], axis=-1), jnp.concatenate([-sin, sin], axis=-1)

    cos_r, sin_r = tables(pos_r)
    cos_c, sin_c = tables(pos_c)
    cos = jnp.concatenate([cos_r, cos_c], axis=-1)
    sin = jnp.concatenate([sin_r, sin_c], axis=-1)
    return jnp.concatenate([cos, cos], axis=-1), jnp.concatenate([sin, sin], axis=-1)


def _mixer_weights(w_in, w_out, sgu_w, sgu_b, q_norm_g, k_norm_g, diff_lambda, diff_norm_g, gla_w2, gla_b,
                   gla_norm_g, norm1_g, norm2_g, router_w, router_bias):
    w_in_pad = jnp.pad(w_in.astype(BF16), ((0, 0), (0, 0), (0, D_PROJ_PAD - w_in.shape[2])))
    w2cat = jnp.zeros((DEPTH, LANES, 2 * W_GLA), F32)
    w2cat = w2cat.at[:, 0:GLA_RANK, 0:W_GLA].set(gla_w2[:, 0]).at[:, GLA_RANK:2 * GLA_RANK, W_GLA:].set(gla_w2[:, 1])
    return (
        norm1_g[:, None, :], norm2_g[:, None, :], w_in_pad, w_out.astype(BF16),
        sgu_w.astype(BF16), jnp.repeat(sgu_b.transpose(0, 2, 1), SGU_GROUP_W, axis=2),
        jnp.tile(q_norm_g, (1, W_QK // DIFF_QK))[:, None, :], jnp.tile(k_norm_g, (1, W_QK // DIFF_QK))[:, None, :],
        diff_lambda, diff_norm_g[:, None, :],
        w2cat.astype(BF16), gla_b.reshape(DEPTH, 1, 2 * W_GLA), jnp.tile(gla_norm_g, (1, GLA_HEADS))[:, None, :],
        router_w.T, router_bias[:, None],
    )


def kernel(x_prompt, x_sample, cache_k, cache_v, state_gla, c, c_ctx, w_in, w_out, sgu_w, sgu_b, q_norm_g, k_norm_g,
           diff_lambda, diff_norm_g, gla_w2, gla_b, gla_norm_g, norm1_g, norm2_g, ada_w, ada_b, router_w, router_bias,
           moe_w1, moe_w3, moe_w2):
    n_ctx_seq, ctx_len, _ = x_prompt.shape
    n_lat_seq, lat_len, _ = x_sample.shape
    n_ctx_tok = n_ctx_seq * ctx_len
    n_lat_tok = n_lat_seq * lat_len
    ctx_tiles = n_ctx_tok // ROW_BLOCK
    lat_tiles_per_seq = lat_len // ROW_BLOCK

    n_cond = 1 + n_lat_seq
    cond_t = jnp.zeros((D_MODEL, SUBLANES), F32).at[:, 0].set(c_ctx).at[:, 1:n_cond].set(c.T)
    mods_all = _adaln_call(cond_t, n_cond, ada_w, ada_b)[:, :n_cond].reshape(DEPTH, n_cond, 6, D_MODEL)
    weights = _mixer_weights(w_in, w_out, sgu_w, sgu_b, q_norm_g, k_norm_g, diff_lambda, diff_norm_g, gla_w2, gla_b,
                             gla_norm_g, norm1_g, norm2_g, router_w, router_bias)

    ck_all = cache_k.transpose(0, 1, 2, 4, 3, 5).reshape(cache_k.shape[:3] + (cache_k.shape[4], DIFF_V))
    st_all = jnp.einsum('bldhkv,hg->bldhvgk', state_gla, jnp.eye(GLA_HEADS, dtype=F32)).reshape(
        n_lat_seq, DEPTH, 2, W_GLA, W_GLA)
    cos, sin = _rope_tables(lat_len)
    extras = (ck_all, cache_v, st_all, cos, sin)

    def mod_row_of_tile(i):
        return jnp.where(i < ctx_tiles, 0, 1 + (i - ctx_tiles) // lat_tiles_per_seq)

    x_c = x_prompt.reshape(n_ctx_tok, D_MODEL)
    x_l = x_sample.reshape(n_lat_tok, D_MODEL)
    cache_bufs = ()
    for l in range(DEPTH):
        ctx_par = 1 if l == 0 else CTX_SEQS_PER_STEP
        x1_c, hs_c, slot_c, wt_c, cnt_c, *cache_bufs = _mixer_call(
            l, ctx_len, ctx_par, False, x_c, mods_all, weights, None, tuple(cache_bufs))
        x1_l, hs_l, slot_l, wt_l, cnt_l = _mixer_call(l, lat_len, 1, True, x_l, mods_all, weights, extras, ())
        plan = _moe_plan(jnp.concatenate([cnt_c[:, :, 0], cnt_l[:, :, 0]], axis=0))
        ys = _expert_call(l, plan, hs_c, hs_l, moe_w1, moe_w3, moe_w2)
        x_c, x_l = _combine_call(l, plan, x1_c, x1_l, slot_c, slot_l, wt_c, wt_l, mods_all, mod_row_of_tile, ys)

    new_k, new_v, new_s = cache_bufs
    return (x_c.reshape(x_prompt.shape), x_l.reshape(x_sample.shape), new_k, new_v, new_s)
```

```python
import functools
import math

import jax
import jax.numpy as jnp
from jax import lax
from jax.experimental import pallas as pl
from jax.experimental.pallas import tpu as pltpu

F32 = jnp.float32
BF16 = jnp.bfloat16

D_MODEL = 1024
DEPTH = 4
GRID_W = 64
SGU_GROUPS = 4
SGU_GROUP_W = 64
SGU_W = SGU_GROUPS * SGU_GROUP_W
SGU_CHUNK = 128
DIFF_HEADS = 4
DIFF_QK = 64
DIFF_V = 2 * DIFF_QK
ROPE_BASE = 10000.0
GLA_HEADS = 4
GLA_DK = 64
GLA_DV = 64
GLA_RANK = 16
GLA_GATE_NORM = 16.0
GLA_CHUNK = 64
N_EXPERTS = 16
N_GROUPS = 4
EXPERTS_PER_GROUP = N_EXPERTS // N_GROUPS
D_EXPERT = 512
EPS = 1e-6

LANES = 128
SUBLANES = 8
MXU_DIM = 256
V7X_VMEM_BYTES = 64 * 1024 * 1024
MIXER_VMEM_LIMIT = V7X_VMEM_BYTES * 7 // 8
SMALL_KERNEL_VMEM_LIMIT = V7X_VMEM_BYTES * 5 // 8

ROW_BLOCK = MXU_DIM
ROW_SLABS = D_MODEL // LANES

W_QK = DIFF_HEADS * 2 * DIFF_QK
W_GLA = GLA_HEADS * GLA_DK
C_AU, C_AV = 0, SGU_W
C_BQ = C_AV + SGU_W
C_BK, C_BV = C_BQ + W_QK, C_BQ + 2 * W_QK
C_CQ = C_BV + DIFF_HEADS * DIFF_V
C_CK, C_CV, C_CR, C_LR = C_CQ + W_GLA, C_CQ + 2 * W_GLA, C_CQ + 3 * W_GLA, C_CQ + 4 * W_GLA
D_PROJ_MAIN = C_LR
D_PROJ_PAD = D_PROJ_MAIN + LANES
M_A, M_B, M_C = 0, SGU_W, SGU_W + DIFF_HEADS * DIFF_V


def _split2(x):
    hi = x.astype(BF16)
    lo = (x - hi.astype(F32)).astype(BF16)
    return hi, lo


def _split3(x):
    hi = x.astype(BF16)
    r = x - hi.astype(F32)
    mid = r.astype(BF16)
    lo = (r - mid.astype(F32)).astype(BF16)
    return hi, mid, lo


def _dot(a, b):
    return jnp.dot(a, b, preferred_element_type=F32)


def _dot_nt(a, b):
    return lax.dot_general(a, b, (((1,), (1,)), ((), ())), preferred_element_type=F32)


def _dot_tn(a, b):
    return lax.dot_general(a, b, (((0,), (0,)), ((), ())), preferred_element_type=F32)


def _iota(shape, dim):
    return lax.broadcasted_iota(jnp.int32, shape, dim)


def _block_ones(width, block):
    r = _iota((width, width), 0) // block
    c = _iota((width, width), 1) // block
    return (r == c)


def _group_sum(z, block):
    width = z.shape[-1]
    outs = []
    for s in range(0, width, MXU_DIM):
        w = min(MXU_DIM, width - s)
        ones = _block_ones(w, block).astype(BF16)
        hi, lo = _split2(z[:, s:s + w])
        outs.append(_dot(hi, ones) + _dot(lo, ones))
    return outs[0] if len(outs) == 1 else jnp.concatenate(outs, axis=-1)


def _group_rms(z, block):
    ms = _group_sum(z * z, block) * (1.0 / block)
    return z * lax.rsqrt(ms + EPS)


def _row_rms(z):
    return z * lax.rsqrt(jnp.mean(z * z, axis=-1, keepdims=True) + EPS)


def _log_sigmoid(x):
    return jnp.minimum(x, 0.0) - jnp.log1p(jnp.exp(-jnp.abs(x)))


ADA_COLS = 1536


def _adaln_kernel(n_cond, cond_t_ref, w_ref, b_ref, o_ref):
    sc = jax.nn.silu(cond_t_ref[...])
    w = w_ref[0]
    rows = [jnp.sum(sc[:, r:r + 1] * w, axis=0, keepdims=True) + b_ref[0] for r in range(n_cond)]
    o_ref[0] = jnp.concatenate(rows + [jnp.zeros((SUBLANES - n_cond, w.shape[1]), F32)], axis=0)


def _adaln_call(cond_t, n_cond, ada_w, ada_b):
    n_col = 6 * D_MODEL // ADA_COLS
    return pl.pallas_call(
        functools.partial(_adaln_kernel, n_cond),
        grid=(DEPTH, n_col),
        in_specs=[
            pl.BlockSpec((D_MODEL, SUBLANES), lambda l, j: (0, 0)),
            pl.BlockSpec((1, D_MODEL, ADA_COLS), lambda l, j: (l, 0, j)),
            pl.BlockSpec((1, 1, ADA_COLS), lambda l, j: (l, 0, j)),
        ],
        out_specs=pl.BlockSpec((1, SUBLANES, ADA_COLS), lambda l, j: (l, 0, j)),
        out_shape=jax.ShapeDtypeStruct((DEPTH, SUBLANES, 6 * D_MODEL), F32),
        compiler_params=pltpu.CompilerParams(
            dimension_semantics=("arbitrary", "arbitrary"), vmem_limit_bytes=SMALL_KERNEL_VMEM_LIMIT),
        name="adaln",
    )(cond_t, ada_w, ada_b.reshape(DEPTH, 1, 6 * D_MODEL))


def _route(hn, rwt_ref, rb_ref):
    h_hi, h_lo = _split2(hn)
    rw = rwt_ref[...]
    rw_hi = rw.astype(BF16)
    rw_lo = (rw - rw_hi.astype(F32)).astype(BF16)
    logits = _dot_nt(rw_hi, h_hi) + _dot_nt(rw_hi, h_lo) + _dot_nt(rw_lo, h_hi)
    aff = jax.nn.sigmoid(logits)
    sel = aff + rb_ref[...]
    n_tok = sel.shape[1]

    def top2_sum(a, b, c, d):
        hi1, lo1 = jnp.maximum(a, b), jnp.minimum(a, b)
        hi2, lo2 = jnp.maximum(c, d), jnp.minimum(c, d)
        return jnp.maximum(hi1, hi2) + jnp.maximum(jnp.minimum(hi1, hi2), jnp.maximum(lo1, lo2))

    scores = []
    for g in range(N_GROUPS):
        rows = [sel[EXPERTS_PER_GROUP * g + j:EXPERTS_PER_GROUP * g + j + 1, :] for j in range(EXPERTS_PER_GROUP)]
        scores.append(top2_sum(*rows))
    best = jnp.zeros((1, n_tok), jnp.int32)
    best_score = scores[0]
    for g in range(1, N_GROUPS):
        upd = scores[g] > best_score
        best = jnp.where(upd, g, best)
        best_score = jnp.where(upd, scores[g], best_score)

    eid_i = _iota((N_EXPERTS, n_tok), 0)
    eid = eid_i.astype(F32)
    neg = jnp.float32(-jnp.inf)
    msel = jnp.where(eid_i // EXPERTS_PER_GROUP == best, sel, neg)
    m1 = jnp.max(msel, axis=0, keepdims=True)
    idx1 = jnp.min(jnp.where(msel == m1, eid, float(N_EXPERTS)), axis=0, keepdims=True)
    msel2 = jnp.where(eid == idx1, neg, msel)
    m2 = jnp.max(msel2, axis=0, keepdims=True)
    idx2 = jnp.min(jnp.where(msel2 == m2, eid, float(N_EXPERTS)), axis=0, keepdims=True)
    w1 = jnp.sum(jnp.where(eid == idx1, aff, 0.0), axis=0, keepdims=True)
    w2 = jnp.sum(jnp.where(eid == idx2, aff, 0.0), axis=0, keepdims=True)
    wsum = w1 + w2
    return idx1.astype(jnp.int32), idx2.astype(jnp.int32), w1 / wsum, w2 / wsum


def _local_slots(idx1, idx2):
    n_tok = idx1.shape[1]
    eid = _iota((N_EXPERTS, n_tok), 0)
    hot1, hot2 = eid == idx1, eid == idx2
    hot = jnp.where(hot1, 1.0, jnp.where(hot2, 1.0, 0.0))
    earlier = jnp.where(_iota((n_tok, n_tok), 0) < _iota((n_tok, n_tok), 1), 1.0, 0.0).astype(BF16)
    before_in_expert = _dot(hot.astype(BF16), earlier)
    counts = jnp.sum(hot, axis=1, keepdims=True)
    lower = jnp.where(_iota((N_EXPERTS, N_EXPERTS), 1) < _iota((N_EXPERTS, N_EXPERTS), 0), 1.0, 0.0).astype(BF16)
    first_slot = _dot(lower, jnp.broadcast_to(counts, (N_EXPERTS, LANES)).astype(BF16))[:, 0:1]
    slot = before_in_expert + first_slot
    slot1 = jnp.sum(jnp.where(hot1, slot, 0.0), axis=0, keepdims=True)
    slot2 = jnp.sum(jnp.where(hot2, slot, 0.0), axis=0, keepdims=True)
    return slot1, slot2, counts


def _slot_one_hot(slot1, slot2, v1, v2):
    n_tok = slot1.shape[1]
    row = _iota((2 * n_tok, n_tok), 0).astype(F32)
    return jnp.where(row == slot1, v1, jnp.where(row == slot2, v2, 0.0))


N_MIXER_WEIGHTS = 15
CTX_SEQS_PER_STEP = 2
MAX_INLINE_BLOCKS = 2


def _mixer_kernel(n_tok, n_par, latent, n_alias, lam_init, *refs):
    it = iter(refs)
    x_ref, mod_ref = next(it), next(it)
    (n1_ref, n2_ref, win_ref, wout_ref, sw_ref, sb_ref, qg_ref, kg_ref, dl_ref, dg_ref,
     w2c_ref, gb_ref, gg_ref, rwt_ref, rb_ref) = (next(it) for _ in range(N_MIXER_WEIGHTS))
    if latent:
        ck_ref, cv_ref, st0_ref, cos_ref, sin_ref = (next(it) for _ in range(5))
    for _ in range(n_alias):
        next(it)
    xo_ref, hs_ref, slot_ref, wt_ref, cnt_ref = (next(it) for _ in range(5))
    if not latent:
        ko_ref, vo_ref, so_ref = (next(it) for _ in range(3))
    proj_ref, mix_ref, q_ref, k_ref, v_ref = (next(it) for _ in range(5))
    gq_ref, gke_ref, gv_ref, gr_ref, dec_ref, go_ref, st_ref = (next(it) for _ in range(7))

    n_blk = n_tok // ROW_BLOCK
    n_ctx = k_ref.shape[0] - n_par * n_tok
    n_keys = n_ctx + n_tok
    mod = mod_ref[0, 0]

    by_block = latent

    def whole_sequence(fn):
        def run():
            fn()

        if by_block:
            pl.when(pl.program_id(1) == 0)(run)
        else:
            run()

    def blocks(body):
        if n_par * n_blk <= MAX_INLINE_BLOCKS:
            for r in range(n_par * n_blk):
                body(r)
        else:
            def step(r, carry):
                body(r)
                return carry
            whole_sequence(lambda: lax.fori_loop(0, n_par * n_blk, step, 0))

    def aligned(start, size):
        return pl.ds(start if isinstance(start, int) else pl.multiple_of(start, size), size)

    def block_rows(r, offset=0):
        return aligned(offset + r * ROW_BLOCK, ROW_BLOCK)

    if not latent:
        for ref in (ko_ref, vo_ref, so_ref):
            for q in range(n_par):
                for other in range(1, ref.shape[1]):
                    ref[q, other] = jnp.zeros(ref.shape[2:], F32)

    lane_group = _iota((SGU_CHUNK, SGU_W), 1) // SGU_GROUP_W
    blk_r = _iota((ROW_BLOCK, ROW_BLOCK), 0)
    blk_c = _iota((ROW_BLOCK, ROW_BLOCK), 1)
    same_chunk = (blk_r // GLA_CHUNK) == (blk_c // GLA_CHUNK)
    tri = (jnp.where(same_chunk & (blk_c <= blk_r), 1.0, 0.0).astype(BF16),
           jnp.where(same_chunk & (blk_c >= blk_r), 1.0, 0.0).astype(BF16))
    chunks_per_blk = ROW_BLOCK // GLA_CHUNK
    head_of_lane = _iota((GLA_CHUNK, W_GLA), 1) // GLA_DK
    stack_r = _iota((GLA_HEADS * GLA_CHUNK, GLA_CHUNK), 0) % GLA_CHUNK
    stack_c = _iota((GLA_HEADS * GLA_CHUNK, GLA_CHUNK), 1)
    causal = (stack_c <= stack_r, stack_c >= stack_r)

    if latent:
        def cached_context():
            for h in range(DIFF_HEADS):
                k_ref[0:n_ctx, h * DIFF_V:(h + 1) * DIFF_V] = ck_ref[0, 0, h].astype(BF16)
                v_ref[0:n_ctx, h * DIFF_V:(h + 1) * DIFF_V] = cv_ref[0, 0, h].astype(BF16)
            st_ref[0] = st0_ref[0, 0]

        whole_sequence(cached_context)
        pair_lo = (_iota((ROW_BLOCK, W_QK), 1) % (DIFF_QK // 2)) < (DIFF_QK // 4)

        def rope(z, rows):
            cos = jnp.concatenate([cos_ref[rows, :]] * DIFF_HEADS, axis=-1)
            sin = jnp.concatenate([sin_ref[rows, :]] * DIFF_HEADS, axis=-1)
            shift = DIFF_QK // 4
            swapped = jnp.where(pair_lo, pltpu.roll(z, W_QK - shift, 1), pltpu.roll(z, shift, 1))
            return z * cos + swapped * sin

    def modulated_input(r):
        h = _row_rms(x_ref[block_rows(r), :]) * n1_ref[0]
        return (h * (1.0 + mod[1:2, :]) + mod[0:1, :]).astype(BF16)

    def spatial_gating(r):
        for c in range(ROW_BLOCK // SGU_CHUNK):
            local = slice(c * SGU_CHUNK, (c + 1) * SGU_CHUNK)
            u = jax.nn.gelu(proj_ref[local, C_AU:C_AU + SGU_W])
            v = _group_rms(jax.nn.gelu(proj_ref[local, C_AV:C_AV + SGU_W]), SGU_GROUP_W).astype(BF16)
            s = sb_ref[0]
            for g in range(SGU_GROUPS):
                s = s + jnp.where(lane_group == g, _dot(sw_ref[0, g], v), 0.0)
            mix_ref[aligned(r * ROW_BLOCK + c * SGU_CHUNK, SGU_CHUNK), M_A:M_A + SGU_W] = (u * s).astype(BF16)

    def attention_operands(r):
        rows = block_rows(r)
        key_rows = block_rows(r, n_ctx)
        seq, seq_rows = r // n_blk, block_rows(r % n_blk)
        qn = _group_rms(proj_ref[:, C_BQ:C_BQ + W_QK], DIFF_QK) * qg_ref[0]
        kn = _group_rms(proj_ref[:, C_BK:C_BK + W_QK], DIFF_QK) * kg_ref[0]
        vv = proj_ref[:, C_BV:C_BV + W_QK]
        if latent:
            qn, kn = rope(qn, rows), rope(kn, rows)
        else:
            for h in range(DIFF_HEADS):
                for i in range(2):
                    lo = h * DIFF_V + i * DIFF_QK
                    ko_ref[seq, 0, h, i, seq_rows, :] = kn[:, lo:lo + DIFF_QK]
                vo_ref[seq, 0, h, seq_rows, :] = vv[:, h * DIFF_V:(h + 1) * DIFF_V]
        q_ref[rows, :] = (qn * (DIFF_QK ** -0.5)).astype(BF16)
        k_ref[key_rows, :] = kn.astype(BF16)
        v_ref[key_rows, :] = vv.astype(BF16)

    def gla_operands(r):
        rows = block_rows(r)
        gpre = _dot(proj_ref[:, C_LR:C_LR + LANES].astype(BF16), w2c_ref[0]) + gb_ref[0]
        gate = _log_sigmoid(gpre) * (1.0 / GLA_GATE_NORM)
        gq = proj_ref[:, C_CQ:C_CQ + W_GLA] * (GLA_DK ** -0.5)
        gk = proj_ref[:, C_CK:C_CK + W_GLA]
        gv = proj_ref[:, C_CV:C_CV + W_GLA].astype(BF16)
        gv_ref[rows, :] = gv
        gr_ref[rows, :] = proj_ref[:, C_CR:C_CR + W_GLA]
        for d in range(2):
            g = gate[:, d * W_GLA:(d + 1) * W_GLA]
            b = sum(_dot(tri[d], p) for p in _split3(g))
            last = GLA_CHUNK - 1 if d == 0 else 0
            b_last = jnp.concatenate(
                [jnp.broadcast_to(b[c * GLA_CHUNK + last:c * GLA_CHUNK + last + 1, :], (GLA_CHUNK, W_GLA))
                 for c in range(chunks_per_blk)], axis=0)
            q_dec = (gq * jnp.exp(b)).astype(BF16)
            k_inv = (gk * jnp.exp(-b)).astype(BF16)
            gq_ref[d, rows, :] = q_dec
            gke_ref[d, rows, :] = (gk * jnp.exp(b_last - b)).astype(BF16)
            for c in range(chunks_per_blk):
                row = c * GLA_CHUNK + last
                dec_ref[d, r * chunks_per_blk + c] = jnp.exp(b[row:row + 1, :])
                chunk = slice(c * GLA_CHUNK, (c + 1) * GLA_CHUNK)
                qd = q_dec[chunk]
                q_stack = jnp.concatenate(
                    [jnp.where(head_of_lane == h, qd, jnp.zeros_like(qd)) for h in range(GLA_HEADS)], axis=0)
                attn = jnp.where(causal[d], _dot_nt(q_stack, k_inv[chunk]), 0.0)
                spread = _dot(attn.astype(BF16), gv[chunk])
                o = jnp.zeros((GLA_CHUNK, W_GLA), F32)
                for h in range(GLA_HEADS):
                    o = o + jnp.where(head_of_lane == h, spread[h * GLA_CHUNK:(h + 1) * GLA_CHUNK, :], 0.0)
                go_ref[d, aligned(r * ROW_BLOCK + c * GLA_CHUNK, GLA_CHUNK), :] = o

    def project_and_split(r):
        proj_ref[...] = _dot(modulated_input(r), win_ref[0])
        spatial_gating(r)
        attention_operands(r)
        gla_operands(r)

    blocks(project_and_split)

    dl = dl_ref[0]
    lam = (jnp.exp(jnp.sum(dl[0:1] * dl[1:2], axis=-1, keepdims=True))
           - jnp.exp(jnp.sum(dl[2:3] * dl[3:4], axis=-1, keepdims=True)) + lam_init)
    sub0 = (_iota((ROW_BLOCK, DIFF_V), 1) < DIFF_QK)

    def softmax(s):
        e = jnp.exp(s - jnp.max(s, axis=-1, keepdims=True))
        return e, jnp.sum(e, axis=-1, keepdims=True)

    def attn_block(r):
        rows = block_rows(r)
        keys = aligned((r // n_blk) * n_keys, n_keys)
        for h in range(DIFF_HEADS):
            cols = slice(h * DIFF_V, (h + 1) * DIFF_V)
            qh = q_ref[rows, cols]
            kh = k_ref[keys, cols]
            e0, z0 = softmax(_dot_nt(jnp.where(sub0, qh, jnp.zeros_like(qh)), kh))
            e1, z1 = softmax(_dot_nt(jnp.where(sub0, jnp.zeros_like(qh), qh), kh))
            w = e0 / z0 - lam * (e1 / z1)
            o = _dot(w.astype(BF16), v_ref[keys, cols])
            o = _row_rms(o) * dg_ref[0] * (1.0 - lam_init)
            mix_ref[rows, M_B + h * DIFF_V:M_B + (h + 1) * DIFF_V] = o.astype(BF16)

    blocks(attn_block)

    if not latent:
        st_ref[...] = jnp.zeros(st_ref.shape, F32)

    n_chunk = n_tok // GLA_CHUNK
    st_diag = (_iota((W_GLA, W_GLA), 0) // GLA_DV) == (_iota((W_GLA, W_GLA), 1) // GLA_DK)

    def gla_step(c, carry):
        for seq in range(n_par):
            for d in range(2):
                cc = seq * n_chunk + (c if d == 0 else n_chunk - 1 - c)
                rows = pl.ds(pl.multiple_of(cc * GLA_CHUNK, GLA_CHUNK), GLA_CHUNK)
                st = st_ref[seq, d]
                go_ref[d, rows, :] = go_ref[d, rows, :] + _dot_nt(gq_ref[d, rows, :], st.astype(BF16))
                upd = _dot_tn(gv_ref[rows, :], gke_ref[d, rows, :])
                st_ref[seq, d] = dec_ref[d, cc] * st + jnp.where(st_diag, upd, 0.0)
        return carry

    whole_sequence(lambda: lax.fori_loop(0, n_chunk, gla_step, 0))

    if not latent:
        for seq in range(n_par):
            for d in range(2):
                s_full = st_ref[seq, d].T
                for h in range(GLA_HEADS):
                    so_ref[seq, 0, d, h] = s_full[h * GLA_DK:(h + 1) * GLA_DK, h * GLA_DV:(h + 1) * GLA_DV]

    def finish_block(r, out_r):
        rows, out_rows = block_rows(r), block_rows(out_r)
        oc = _group_rms(go_ref[0, rows, :] + go_ref[1, rows, :], GLA_DV) * gg_ref[0]
        oc = oc * jax.nn.silu(gr_ref[rows, :])
        mix_ref[rows, M_C:M_C + W_GLA] = oc.astype(BF16)
        x1 = x_ref[rows, :] + mod[2:3, :] * _dot(mix_ref[rows, :], wout_ref[0])
        xo_ref[out_rows, :] = x1
        hn = _row_rms(x1) * n2_ref[0]
        hn = hn * (1.0 + mod[4:5, :]) + mod[3:4, :]
        idx1, idx2, w1, w2 = _route(hn, rwt_ref, rb_ref)
        slot1, slot2, counts = _local_slots(idx1, idx2)
        perm = _slot_one_hot(slot1, slot2, 1.0, 1.0).astype(BF16)
        _to_row_slabs(hs_ref, 2 * out_r * ROW_BLOCK, _dot(perm, hn.astype(BF16)))
        slot_ref[:, out_rows] = jnp.concatenate([slot1, slot2], axis=0).astype(jnp.int32)
        wt_ref[:, out_rows] = jnp.concatenate([w1, w2], axis=0)
        cnt_ref[out_r] = jnp.broadcast_to(counts, (N_EXPERTS, LANES)).astype(jnp.int32)

    if by_block:
        finish_block(pl.program_id(1), 0)
    else:
        for r in range(n_par * n_blk):
            finish_block(r, r)


def _mixer_call(l, n_tok, n_par, latent, x, mods_all, weights, extras, cache_bufs):
    n_seq = x.shape[0] // n_tok
    n_all = x.shape[0]
    assert n_seq % n_par == 0 and not (latent and n_par > 1)
    n_step_tok = n_par * n_tok
    n_keys = n_step_tok + (extras[0].shape[3] if latent else 0)
    n_chunk = n_step_tok // GLA_CHUNK
    lam_init = 0.8 - 0.6 * math.exp(-0.3 * l)

    single = pl.Buffered(1)
    seq_mode = single if latent else None

    def layer(arr):
        tail = arr.shape[1:]
        return pl.BlockSpec((1,) + tail, lambda s, *_r, _n=len(tail): (l,) + (0,) * _n, pipeline_mode=single)

    def const(arr):
        return pl.BlockSpec(arr.shape, lambda s, *_r, _n=arr.ndim: (0,) * _n, pipeline_mode=single)

    def tok_spec(width):
        return pl.BlockSpec((n_step_tok, width), lambda s, *_r: (s, 0), pipeline_mode=seq_mode)

    mod_row = (lambda s: 1 + s) if latent else (lambda s: 0)
    in_specs = [tok_spec(D_MODEL),
                pl.BlockSpec((1, 1, 6, D_MODEL), lambda s, *_r: (l, mod_row(s), 0, 0))]
    in_specs += [layer(w) for w in weights[:N_MIXER_WEIGHTS - 2]] + [const(w) for w in weights[-2:]]
    operands = [x, mods_all] + list(weights)
    if latent:
        ck, cv, st0, cos, sin = extras
        in_specs += [
            pl.BlockSpec((1, 1) + ck.shape[2:], lambda s, *_r: (s, l, 0, 0, 0)),
            pl.BlockSpec((1, 1) + cv.shape[2:], lambda s, *_r: (s, l, 0, 0, 0)),
            pl.BlockSpec((1, 1) + st0.shape[2:], lambda s, *_r: (s, l, 0, 0, 0)),
            const(cos), const(sin),
        ]
        operands += [ck, cv, st0, cos, sin]
    n_in = len(operands)
    in_specs += [pl.BlockSpec(memory_space=pl.ANY)] * len(cache_bufs)
    operands += list(cache_bufs)

    tiles_per_step = n_step_tok // ROW_BLOCK
    out_shape = [
        jax.ShapeDtypeStruct((n_all, D_MODEL), F32),
        jax.ShapeDtypeStruct((2 * n_all * ROW_SLABS, LANES), F32),
        jax.ShapeDtypeStruct((2, n_all), jnp.int32),
        jax.ShapeDtypeStruct((2, n_all), F32),
        jax.ShapeDtypeStruct((n_all // ROW_BLOCK, N_EXPERTS, LANES), jnp.int32),
    ]
    if latent:
        grid = (n_seq, tiles_per_step)
        out_tok, out_tiles = ROW_BLOCK, 1
        at = lambda s, r: s * tiles_per_step + r
    else:
        grid = (n_seq // n_par,)
        out_tok, out_tiles = n_step_tok, tiles_per_step
        at = lambda s: s
    out_specs = [
        pl.BlockSpec((out_tok, D_MODEL), lambda *g: (at(*g), 0)),
        pl.BlockSpec((2 * out_tok * ROW_SLABS, LANES), lambda *g: (at(*g), 0)),
        pl.BlockSpec((2, out_tok), lambda *g: (0, at(*g))),
        pl.BlockSpec((2, out_tok), lambda *g: (0, at(*g))),
        pl.BlockSpec((out_tiles, N_EXPERTS, LANES), lambda *g: (at(*g), 0, 0)),
    ]
    n_shared_out = len(out_shape)
    aliases = {}
    if not latent:
        out_shape += [
            jax.ShapeDtypeStruct((n_seq, DEPTH, DIFF_HEADS, 2, n_tok, DIFF_QK), F32),
            jax.ShapeDtypeStruct((n_seq, DEPTH, DIFF_HEADS, n_tok, DIFF_V), F32),
            jax.ShapeDtypeStruct((n_seq, DEPTH, 2, GLA_HEADS, GLA_DK, GLA_DV), F32),
        ]
        n_lay, lay = (1, l) if cache_bufs else (DEPTH, 0)
        out_specs += [
            pl.BlockSpec((n_par, n_lay, DIFF_HEADS, 2, n_tok, DIFF_QK), lambda s: (s, lay, 0, 0, 0, 0)),
            pl.BlockSpec((n_par, n_lay, DIFF_HEADS, n_tok, DIFF_V), lambda s: (s, lay, 0, 0, 0)),
            pl.BlockSpec((n_par, n_lay, 2, GLA_HEADS, GLA_DK, GLA_DV), lambda s: (s, lay, 0, 0, 0, 0)),
        ]
        aliases = {n_in + j: n_shared_out + j for j in range(len(cache_bufs))}
    scratch = [
        pltpu.VMEM((ROW_BLOCK, D_PROJ_PAD), F32),
        pltpu.VMEM((n_step_tok, D_MODEL), BF16),
        pltpu.VMEM((n_step_tok, W_QK), BF16),
        pltpu.VMEM((n_keys, W_QK), BF16),
        pltpu.VMEM((n_keys, W_QK), BF16),
        pltpu.VMEM((2, n_step_tok, W_GLA), BF16),
        pltpu.VMEM((2, n_step_tok, W_GLA), BF16),
        pltpu.VMEM((n_step_tok, W_GLA), BF16),
        pltpu.VMEM((n_step_tok, W_GLA), F32),
        pltpu.VMEM((2, n_chunk, 1, W_GLA), F32),
        pltpu.VMEM((2, n_step_tok, W_GLA), F32),
        pltpu.VMEM((n_par, 2, W_GLA, W_GLA), F32),
    ]
    return pl.pallas_call(
        functools.partial(_mixer_kernel, n_tok, n_par, latent, len(cache_bufs), lam_init),
        grid=grid,
        in_specs=in_specs,
        out_specs=out_specs,
        out_shape=out_shape,
        scratch_shapes=scratch,
        input_output_aliases=aliases,
        compiler_params=pltpu.CompilerParams(
            dimension_semantics=("arbitrary",) * len(grid), vmem_limit_bytes=MIXER_VMEM_LIMIT),
        name="mixer_latent" if latent else "mixer_context",
    )(*operands)


PAIR_BLOCK = 2 * ROW_BLOCK
COPY_SIZES = tuple(ROW_BLOCK >> k for k in range(ROW_BLOCK.bit_length()))
LARGE_COPY = 64
GATHER_AHEAD = 2
GATHER_SLOTS = GATHER_AHEAD + 1


def _segment_copies(n_rows, make_copy, act):
    def copy_if_set(size):
        @pl.when((n_rows & size) != 0)
        def _():
            act(make_copy(n_rows & (-2 * size), size))

    n_large = COPY_SIZES.index(LARGE_COPY) + 1

    @pl.when(n_rows >= LARGE_COPY)
    def _():
        for size in COPY_SIZES[:n_large]:
            copy_if_set(size)

    for size in COPY_SIZES[n_large:]:
        copy_if_set(size)


def _start(copy):
    copy.start()


def _wait(copy):
    copy.wait()


def _slab_rows(first_row, n_rows, slab):
    return pl.ds(first_row * ROW_SLABS + slab, n_rows, stride=ROW_SLABS)


def _to_row_slabs(ref, first_row, value):
    for s in range(ROW_SLABS):
        ref[_slab_rows(first_row, value.shape[0], s), :] = value[:, s * LANES:(s + 1) * LANES]


def _from_row_slabs(ref, first_row, n_rows):
    return jnp.concatenate([ref[_slab_rows(first_row, n_rows, s), :] for s in range(ROW_SLABS)], axis=-1)


def _row_span(ref, first_row, n_rows):
    return ref.at[pl.ds(pl.multiple_of(first_row * ROW_SLABS, ROW_SLABS), n_rows * ROW_SLABS)]


def _two_streams(n_first_tiles):
    def first(i, *_):
        return (jnp.minimum(i, n_first_tiles - 1), 0)

    def second(i, *_):
        return (jnp.maximum(i - n_first_tiles, 0), 0)

    return first, second


N_EXPERT_TABLES = 10
OUT_SLOTS = 2


def _expert_kernel(n_ctx_tiles, n_tiles_max, te_ref, first_ref, rows_ref, jlo_ref, jhi_ref, cpre_ref, cnt_ref,
                   lofs_ref, tile0_ref, ntile_ref, hs_c_ref, hs_l_ref, w1_ref, w3_ref, w2_ref, ys_ref,
                   xbuf_ref, obuf_ref, w1b_ref, w3b_ref, w2b_ref, sem, out_sem):
    expert = pl.program_id(0)
    n_tiles = tile0_ref[N_EXPERTS]

    def gather(t, act):
        slot = t % GATHER_SLOTS
        e, first = te_ref[t], first_ref[t]
        last = first + rows_ref[t]

        def segment_of(hs_ref, first_tile):
            def body(j, carry):
                k = j * N_EXPERTS + e
                seg_first = cpre_ref[k]
                lo = jnp.maximum(seg_first, first)
                n = jnp.maximum(jnp.minimum(seg_first + cnt_ref[k], last) - lo, 0)
                src = (j - first_tile) * PAIR_BLOCK + lofs_ref[k] + (lo - seg_first)
                dst = slot * ROW_BLOCK + lo - first
                _segment_copies(n, lambda done, size: pltpu.make_async_copy(
                    _row_span(hs_ref, src + done, size), _row_span(xbuf_ref, dst + done, size), sem.at[slot]), act)
                return carry
            return body

        jlo, jhi = jlo_ref[t], jhi_ref[t]
        lax.fori_loop(jnp.minimum(jlo, n_ctx_tiles), jnp.minimum(jhi, n_ctx_tiles), segment_of(hs_c_ref, 0), 0)
        lax.fori_loop(jnp.maximum(jlo, n_ctx_tiles), jnp.maximum(jhi, n_ctx_tiles),
                      segment_of(hs_l_ref, n_ctx_tiles), 0)

    def out_copy(t, oslot):
        return pltpu.make_async_copy(
            _row_span(obuf_ref, oslot * ROW_BLOCK, ROW_BLOCK), _row_span(ys_ref, t * ROW_BLOCK, ROW_BLOCK),
            out_sem.at[oslot])

    @pl.when(expert == 0)
    def _():
        xbuf_ref[...] = jnp.zeros(xbuf_ref.shape, F32)
        for t in range(GATHER_AHEAD):
            gather(t, _start)

    w1b_ref[...] = w1_ref[0, 0].astype(BF16)
    w3b_ref[...] = w3_ref[0, 0].astype(BF16)
    w2b_ref[...] = w2_ref[0, 0].astype(BF16)
    tile0, n_own = tile0_ref[expert], ntile_ref[expert]

    def tile_body(k, carry):
        t = tile0 + k
        slot, oslot = t % GATHER_SLOTS, k % OUT_SLOTS

        @pl.when(t + GATHER_AHEAD < n_tiles)
        def _():
            gather(t + GATHER_AHEAD, _start)

        n_rows = rows_ref[t]
        _segment_copies(n_rows, lambda done, size: pltpu.make_async_copy(
            _row_span(hs_c_ref, done, size), _row_span(xbuf_ref, slot * ROW_BLOCK + done, size), sem.at[slot]), _wait)

        @pl.when(k >= OUT_SLOTS)
        def _():
            out_copy(t, oslot).wait()

        live = _iota((ROW_BLOCK, D_MODEL), 0) < n_rows
        x = jnp.where(live, _from_row_slabs(xbuf_ref, slot * ROW_BLOCK, ROW_BLOCK), 0.0).astype(BF16)
        hid = jax.nn.silu(_dot(x, w1b_ref[...])) * _dot(x, w3b_ref[...])
        _to_row_slabs(obuf_ref, oslot * ROW_BLOCK, _dot(hid.astype(BF16), w2b_ref[...]))
        out_copy(t, oslot).start()
        return carry

    lax.fori_loop(0, n_own, tile_body, 0)
    for oslot in range(OUT_SLOTS):
        @pl.when(n_own > oslot)
        def _():
            out_copy(tile0, oslot).wait()

    @pl.when(expert == N_EXPERTS - 1)
    def _():
        obuf_ref[...] = jnp.zeros(obuf_ref.shape, F32)

        def fill(t, carry):
            out_copy(t, 0).start()
            out_copy(t, 0).wait()
            return carry

        lax.fori_loop(n_tiles, n_tiles_max, fill, 0)


def _expert_call(l, plan, hs_c, hs_l, w1, w3, w2):
    tables = plan["expert_tables"]
    n_tiles_max = tables[0].shape[0]
    n_ctx_tiles = hs_c.shape[0] // (PAIR_BLOCK * ROW_SLABS)

    def weight(shape):
        return pl.BlockSpec((1, 1) + shape, lambda e, *_: (l, e, 0, 0))

    return pl.pallas_call(
        functools.partial(_expert_kernel, n_ctx_tiles, n_tiles_max),
        grid_spec=pltpu.PrefetchScalarGridSpec(
            num_scalar_prefetch=N_EXPERT_TABLES,
            grid=(N_EXPERTS,),
            in_specs=[pl.BlockSpec(memory_space=pl.ANY), pl.BlockSpec(memory_space=pl.ANY),
                      weight((D_MODEL, D_EXPERT)), weight((D_MODEL, D_EXPERT)), weight((D_EXPERT, D_MODEL))],
            out_specs=pl.BlockSpec(memory_space=pl.ANY),
            scratch_shapes=[pltpu.VMEM((GATHER_SLOTS * ROW_BLOCK * ROW_SLABS, LANES), F32),
                            pltpu.VMEM((OUT_SLOTS * ROW_BLOCK * ROW_SLABS, LANES), F32),
                            pltpu.VMEM((D_MODEL, D_EXPERT), BF16), pltpu.VMEM((D_MODEL, D_EXPERT), BF16),
                            pltpu.VMEM((D_EXPERT, D_MODEL), BF16),
                            pltpu.SemaphoreType.DMA((GATHER_SLOTS,)), pltpu.SemaphoreType.DMA((OUT_SLOTS,))],
        ),
        out_shape=jax.ShapeDtypeStruct((n_tiles_max * ROW_BLOCK * ROW_SLABS, LANES), F32),
        compiler_params=pltpu.CompilerParams(
            dimension_semantics=("arbitrary",), vmem_limit_bytes=SMALL_KERNEL_VMEM_LIMIT),
        name="moe_experts",
    )(*tables, hs_c, hs_l, w1, w3, w2)


N_COMBINE_TABLES = 4
COMBINE_TILES_PER_STEP = 2


def _combine_kernel(n_ctx_steps, cnt_ref, cpre_ref, lofs_ref, starts_ref, x_c_ref, x_l_ref, slot_c_ref, slot_l_ref,
                    wt_c_ref, wt_l_ref, mod_ref, ys_ref, xo_c_ref, xo_l_ref, buf_ref, sem):
    step = pl.program_id(0)
    n_tiles = pl.num_programs(0) * COMBINE_TILES_PER_STEP

    def collect(t, act):
        slot = t % GATHER_SLOTS

        def body(e, carry):
            k = t * N_EXPERTS + e
            src, dst = starts_ref[e] + cpre_ref[k], slot * PAIR_BLOCK + lofs_ref[k]
            _segment_copies(cnt_ref[k], lambda done, size: pltpu.make_async_copy(
                _row_span(ys_ref, src + done, size), _row_span(buf_ref, dst + done, size), sem.at[slot]), act)
            return carry

        lax.fori_loop(0, N_EXPERTS, body, 0)

    @pl.when(step == 0)
    def _():
        for t in range(GATHER_AHEAD):
            collect(t, _start)

    gate = mod_ref[0, 0, 5:6, :]

    def finish(part, first_row, x_ref, slot_ref, wt_ref, xo_ref):
        tokens = pl.ds(part * ROW_BLOCK, ROW_BLOCK)
        slots, wts = slot_ref[:, tokens].astype(F32), wt_ref[:, tokens]
        slot1, slot2 = slots[0:1], slots[1:2]
        weight_of_row = jnp.sum(_slot_one_hot(slot1, slot2, wts[0:1], wts[1:2]), axis=1, keepdims=True)
        gather_rows = _slot_one_hot(slot1, slot2, 1.0, 1.0).T.astype(BF16)
        for g in range(ROW_SLABS // 2):
            cols = slice(2 * g * LANES, 2 * (g + 1) * LANES)
            rows = jnp.concatenate(
                [buf_ref[_slab_rows(first_row, PAIR_BLOCK, s), :] for s in (2 * g, 2 * g + 1)], axis=-1)
            hi, lo = _split2(rows * weight_of_row)
            y = _dot(gather_rows, hi) + _dot(gather_rows, lo)
            xo_ref[tokens, cols] = x_ref[tokens, cols] + gate[:, cols] * y

    for part in range(COMBINE_TILES_PER_STEP):
        j = step * COMBINE_TILES_PER_STEP + part

        @pl.when(j + GATHER_AHEAD < n_tiles)
        def _():
            collect(j + GATHER_AHEAD, _start)

        slot = j % GATHER_SLOTS
        for piece in range(PAIR_BLOCK // ROW_BLOCK):
            pltpu.make_async_copy(
                _row_span(ys_ref, piece * ROW_BLOCK, ROW_BLOCK),
                _row_span(buf_ref, slot * PAIR_BLOCK + piece * ROW_BLOCK, ROW_BLOCK), sem.at[slot]).wait()
        @pl.when(step < n_ctx_steps)
        def _():
            finish(part, slot * PAIR_BLOCK, x_c_ref, slot_c_ref, wt_c_ref, xo_c_ref)

        @pl.when(step >= n_ctx_steps)
        def _():
            finish(part, slot * PAIR_BLOCK, x_l_ref, slot_l_ref, wt_l_ref, xo_l_ref)


def _combine_call(l, plan, x_c, x_l, slot_c, slot_l, wt_c, wt_l, mods_all, mod_row_of_tile, ys):
    step_rows = COMBINE_TILES_PER_STEP * ROW_BLOCK
    n_steps = (x_c.shape[0] + x_l.shape[0]) // step_rows
    n_ctx_steps = x_c.shape[0] // step_rows
    first, second = _two_streams(n_ctx_steps)

    def lanes(index_map):
        return lambda i, *_: index_map(i)[::-1]

    return pl.pallas_call(
        functools.partial(_combine_kernel, n_ctx_steps),
        grid_spec=pltpu.PrefetchScalarGridSpec(
            num_scalar_prefetch=N_COMBINE_TABLES,
            grid=(n_steps,),
            in_specs=[pl.BlockSpec((step_rows, D_MODEL), first),
                      pl.BlockSpec((step_rows, D_MODEL), second),
                      pl.BlockSpec((2, step_rows), lanes(first)),
                      pl.BlockSpec((2, step_rows), lanes(second)),
                      pl.BlockSpec((2, step_rows), lanes(first)),
                      pl.BlockSpec((2, step_rows), lanes(second)),
                      pl.BlockSpec((1, 1, 6, D_MODEL),
                                   lambda i, *_: (l, mod_row_of_tile(i * COMBINE_TILES_PER_STEP), 0, 0)),
                      pl.BlockSpec(memory_space=pl.ANY)],
            out_specs=[pl.BlockSpec((step_rows, D_MODEL), first),
                       pl.BlockSpec((step_rows, D_MODEL), second)],
            scratch_shapes=[pltpu.VMEM((GATHER_SLOTS * PAIR_BLOCK * ROW_SLABS, LANES), F32),
                            pltpu.SemaphoreType.DMA((GATHER_SLOTS,))],
        ),
        out_shape=[jax.ShapeDtypeStruct(x_c.shape, F32), jax.ShapeDtypeStruct(x_l.shape, F32)],
        compiler_params=pltpu.CompilerParams(
            dimension_semantics=("arbitrary",), vmem_limit_bytes=SMALL_KERNEL_VMEM_LIMIT),
        name="moe_combine",
    )(*plan["combine_tables"], x_c, x_l, slot_c, slot_l, wt_c, wt_l, mods_all, ys)


def _moe_plan(cnt):
    n_tok_tiles = cnt.shape[0]
    n_tiles = n_tok_tiles * PAIR_BLOCK // ROW_BLOCK + N_EXPERTS
    lofs = jnp.cumsum(cnt, axis=1) - cnt
    cpre = jnp.cumsum(cnt, axis=0) - cnt
    counts = jnp.sum(cnt, axis=0)
    padded = (counts + ROW_BLOCK - 1) // ROW_BLOCK * ROW_BLOCK
    ends = jnp.cumsum(padded)
    starts = ends - padded
    tile_start = jnp.arange(n_tiles, dtype=jnp.int32) * ROW_BLOCK
    tile_expert = jnp.minimum(
        jnp.sum((tile_start[:, None] >= ends[None, :]).astype(jnp.int32), axis=1), N_EXPERTS - 1)
    hot = tile_expert[:, None] == jnp.arange(N_EXPERTS, dtype=jnp.int32)[None, :]
    first = tile_start - jnp.sum(jnp.where(hot, starts[None, :], 0), axis=1)
    rows = jnp.clip(jnp.sum(jnp.where(hot, counts[None, :], 0), axis=1) - first, 0, ROW_BLOCK)
    seg_first = jnp.sum(jnp.where(hot[:, None, :], cpre[None, :, :], 0), axis=2)
    seg_rows = jnp.sum(jnp.where(hot[:, None, :], cnt[None, :, :], 0), axis=2)
    overlap = (seg_first < (first + rows)[:, None]) & (seg_first + seg_rows > first[:, None])
    j = jnp.arange(n_tok_tiles, dtype=jnp.int32)[None, :]
    jlo = jnp.min(jnp.where(overlap, j, n_tok_tiles), axis=1)
    jhi = jnp.max(jnp.where(overlap, j + 1, 0), axis=1)
    i32 = lambda a: a.astype(jnp.int32).reshape(-1)
    tile0 = jnp.concatenate([starts, ends[-1:]]) // ROW_BLOCK
    return {
        "expert_tables": tuple(i32(a) for a in (tile_expert, first, rows, jlo, jhi, cpre, cnt, lofs,
                                                tile0, padded // ROW_BLOCK)),
        "combine_tables": tuple(i32(a) for a in (cnt, cpre, lofs, starts)),
    }


def _rope_tables(n_tok):
    n_rows = n_tok // GRID_W
    pos_r = jnp.repeat(jnp.arange(n_rows), GRID_W)
    pos_c = jnp.tile(jnp.arange(GRID_W), n_rows)
    half = DIFF_QK // 2
    nf = half // 2
    freqs = ROPE_BASE ** (-jnp.arange(nf, dtype=F32) / nf)

    def tables(pos):
        ang = pos.astype(F32)[:, None] * freqs
        cos, sin = jnp.cos(ang), jnp.sin(ang)
        return jnp.concatenate([cos, cos], axis=-1), jnp.concatenate([-sin, sin], axis=-1)

    cos_r, sin_r = tables(pos_r)
    cos_c, sin_c = tables(pos_c)
    cos = jnp.concatenate([cos_r, cos_c], axis=-1)
    sin = jnp.concatenate([sin_r, sin_c], axis=-1)
    return jnp.concatenate([cos, cos], axis=-1), jnp.concatenate([sin, sin], axis=-1)


def _mixer_weights(w_in, w_out, sgu_w, sgu_b, q_norm_g, k_norm_g, diff_lambda, diff_norm_g, gla_w2, gla_b,
                   gla_norm_g, norm1_g, norm2_g, router_w, router_bias):
    w_in_pad = jnp.pad(w_in.astype(BF16), ((0, 0), (0, 0), (0, D_PROJ_PAD - w_in.shape[2])))
    w2cat = jnp.zeros((DEPTH, LANES, 2 * W_GLA), F32)
    w2cat = w2cat.at[:, 0:GLA_RANK, 0:W_GLA].set(gla_w2[:, 0]).at[:, GLA_RANK:2 * GLA_RANK, W_GLA:].set(gla_w2[:, 1])
    return (
        norm1_g[:, None, :], norm2_g[:, None, :], w_in_pad, w_out.astype(BF16),
        sgu_w.astype(BF16), jnp.repeat(sgu_b.transpose(0, 2, 1), SGU_GROUP_W, axis=2),
        jnp.tile(q_norm_g, (1, W_QK // DIFF_QK))[:, None, :], jnp.tile(k_norm_g, (1, W_QK // DIFF_QK))[:, None, :],
        diff_lambda, diff_norm_g[:, None, :],
        w2cat.astype(BF16), gla_b.reshape(DEPTH, 1, 2 * W_GLA), jnp.tile(gla_norm_g, (1, GLA_HEADS))[:, None, :],
        router_w.T, router_bias[:, None],
    )


def kernel(x_prompt, x_sample, cache_k, cache_v, state_gla, c, c_ctx, w_in, w_out, sgu_w, sgu_b, q_norm_g, k_norm_g,
           diff_lambda, diff_norm_g, gla_w2, gla_b, gla_norm_g, norm1_g, norm2_g, ada_w, ada_b, router_w, router_bias,
           moe_w1, moe_w3, moe_w2):
    n_ctx_seq, ctx_len, _ = x_prompt.shape
    n_lat_seq, lat_len, _ = x_sample.shape
    n_ctx_tok = n_ctx_seq * ctx_len
    n_lat_tok = n_lat_seq * lat_len
    ctx_tiles = n_ctx_tok // ROW_BLOCK
    lat_tiles_per_seq = lat_len // ROW_BLOCK

    n_cond = 1 + n_lat_seq
    cond_t = jnp.zeros((D_MODEL, SUBLANES), F32).at[:, 0].set(c_ctx).at[:, 1:n_cond].set(c.T)
    mods_all = _adaln_call(cond_t, n_cond, ada_w, ada_b)[:, :n_cond].reshape(DEPTH, n_cond, 6, D_MODEL)
    weights = _mixer_weights(w_in, w_out, sgu_w, sgu_b, q_norm_g, k_norm_g, diff_lambda, diff_norm_g, gla_w2, gla_b,
                             gla_norm_g, norm1_g, norm2_g, router_w, router_bias)

    ck_all = cache_k.transpose(0, 1, 2, 4, 3, 5).reshape(cache_k.shape[:3] + (cache_k.shape[4], DIFF_V))
    st_all = jnp.einsum('bldhkv,hg->bldhvgk', state_gla, jnp.eye(GLA_HEADS, dtype=F32)).reshape(
        n_lat_seq, DEPTH, 2, W_GLA, W_GLA)
    cos, sin = _rope_tables(lat_len)
    extras = (ck_all, cache_v, st_all, cos, sin)

    def mod_row_of_tile(i):
        return jnp.where(i < ctx_tiles, 0, 1 + (i - ctx_tiles) // lat_tiles_per_seq)

    x_c = x_prompt.reshape(n_ctx_tok, D_MODEL)
    x_l = x_sample.reshape(n_lat_tok, D_MODEL)
    cache_bufs = ()
    for l in range(DEPTH):
        ctx_par = 1 if l == 0 else CTX_SEQS_PER_STEP
        x1_c, hs_c, slot_c, wt_c, cnt_c, *cache_bufs = _mixer_call(
            l, ctx_len, ctx_par, False, x_c, mods_all, weights, None, tuple(cache_bufs))
        x1_l, hs_l, slot_l, wt_l, cnt_l = _mixer_call(l, lat_len, 1, True, x_l, mods_all, weights, extras, ())
        plan = _moe_plan(jnp.concatenate([cnt_c[:, :, 0], cnt_l[:, :, 0]], axis=0))
        ys = _expert_call(l, plan, hs_c, hs_l, moe_w1, moe_w3, moe_w2)
        x_c, x_l = _combine_call(l, plan, x1_c, x1_l, slot_c, slot_l, wt_c, wt_l, mods_all, mod_row_of_tile, ys)

    new_k, new_v, new_s = cache_bufs
    return (x_c.reshape(x_prompt.shape), x_l.reshape(x_sample.shape), new_k, new_v, new_s)
```

```python
import functools
import math

import jax
import jax.numpy as jnp
from jax import lax
from jax.experimental import pallas as pl
from jax.experimental.pallas import tpu as pltpu

F32 = jnp.float32
BF16 = jnp.bfloat16

D_MODEL = 1024
DEPTH = 4
GRID_W = 64
SGU_GROUPS = 4
SGU_GROUP_W = 64
SGU_W = SGU_GROUPS * SGU_GROUP_W
SGU_CHUNK = 128
DIFF_HEADS = 4
DIFF_QK = 64
DIFF_V = 2 * DIFF_QK
ROPE_BASE = 10000.0
GLA_HEADS = 4
GLA_DK = 64
GLA_DV = 64
GLA_RANK = 16
GLA_GATE_NORM = 16.0
GLA_CHUNK = 64
N_EXPERTS = 16
N_GROUPS = 4
EXPERTS_PER_GROUP = N_EXPERTS // N_GROUPS
D_EXPERT = 512
EPS = 1e-6

LANES = 128
SUBLANES = 8
MXU_DIM = 256
V7X_VMEM_BYTES = 64 * 1024 * 1024
MIXER_VMEM_LIMIT = V7X_VMEM_BYTES * 7 // 8
SMALL_KERNEL_VMEM_LIMIT = V7X_VMEM_BYTES * 5 // 8

ROW_BLOCK = MXU_DIM
ROW_SLABS = D_MODEL // LANES

W_QK = DIFF_HEADS * 2 * DIFF_QK
W_GLA = GLA_HEADS * GLA_DK
C_AU, C_AV = 0, SGU_W
C_BQ = C_AV + SGU_W
C_BK, C_BV = C_BQ + W_QK, C_BQ + 2 * W_QK
C_CQ = C_BV + DIFF_HEADS * DIFF_V
C_CK, C_CV, C_CR, C_LR = C_CQ + W_GLA, C_CQ + 2 * W_GLA, C_CQ + 3 * W_GLA, C_CQ + 4 * W_GLA
D_PROJ_MAIN = C_LR
D_PROJ_PAD = D_PROJ_MAIN + LANES
M_A, M_B, M_C = 0, SGU_W, SGU_W + DIFF_HEADS * DIFF_V


def _split2(x):
    hi = x.astype(BF16)
    lo = (x - hi.astype(F32)).astype(BF16)
    return hi, lo


def _split3(x):
    hi = x.astype(BF16)
    r = x - hi.astype(F32)
    mid = r.astype(BF16)
    lo = (r - mid.astype(F32)).astype(BF16)
    return hi, mid, lo


def _dot(a, b):
    return jnp.dot(a, b, preferred_element_type=F32)


def _dot_nt(a, b):
    return lax.dot_general(a, b, (((1,), (1,)), ((), ())), preferred_element_type=F32)


def _dot_tn(a, b):
    return lax.dot_general(a, b, (((0,), (0,)), ((), ())), preferred_element_type=F32)


def _iota(shape, dim):
    return lax.broadcasted_iota(jnp.int32, shape, dim)


def _block_ones(width, block):
    r = _iota((width, width), 0) // block
    c = _iota((width, width), 1) // block
    return (r == c)


def _group_sum(z, block):
    width = z.shape[-1]
    outs = []
    for s in range(0, width, MXU_DIM):
        w = min(MXU_DIM, width - s)
        ones = _block_ones(w, block).astype(BF16)
        hi, lo = _split2(z[:, s:s + w])
        outs.append(_dot(hi, ones) + _dot(lo, ones))
    return outs[0] if len(outs) == 1 else jnp.concatenate(outs, axis=-1)


def _group_rms(z, block):
    ms = _group_sum(z * z, block) * (1.0 / block)
    return z * lax.rsqrt(ms + EPS)


def _row_rms(z):
    return z * lax.rsqrt(jnp.mean(z * z, axis=-1, keepdims=True) + EPS)


def _log_sigmoid(x):
    return jnp.minimum(x, 0.0) - jnp.log1p(jnp.exp(-jnp.abs(x)))


ADA_COLS = 1536


def _adaln_kernel(n_cond, cond_t_ref, w_ref, b_ref, o_ref):
    sc = jax.nn.silu(cond_t_ref[...])
    w = w_ref[0]
    rows = [jnp.sum(sc[:, r:r + 1] * w, axis=0, keepdims=True) + b_ref[0] for r in range(n_cond)]
    o_ref[0] = jnp.concatenate(rows + [jnp.zeros((SUBLANES - n_cond, w.shape[1]), F32)], axis=0)


def _adaln_call(cond_t, n_cond, ada_w, ada_b):
    n_col = 6 * D_MODEL // ADA_COLS
    return pl.pallas_call(
        functools.partial(_adaln_kernel, n_cond),
        grid=(DEPTH, n_col),
        in_specs=[
            pl.BlockSpec((D_MODEL, SUBLANES), lambda l, j: (0, 0)),
            pl.BlockSpec((1, D_MODEL, ADA_COLS), lambda l, j: (l, 0, j)),
            pl.BlockSpec((1, 1, ADA_COLS), lambda l, j: (l, 0, j)),
        ],
        out_specs=pl.BlockSpec((1, SUBLANES, ADA_COLS), lambda l, j: (l, 0, j)),
        out_shape=jax.ShapeDtypeStruct((DEPTH, SUBLANES, 6 * D_MODEL), F32),
        compiler_params=pltpu.CompilerParams(
            dimension_semantics=("arbitrary", "arbitrary"), vmem_limit_bytes=SMALL_KERNEL_VMEM_LIMIT),
        name="adaln",
    )(cond_t, ada_w, ada_b.reshape(DEPTH, 1, 6 * D_MODEL))


def _route(hn, rwt_ref, rb_ref):
    h_hi, h_lo = _split2(hn)
    rw = rwt_ref[...]
    rw_hi = rw.astype(BF16)
    rw_lo = (rw - rw_hi.astype(F32)).astype(BF16)
    logits = _dot_nt(rw_hi, h_hi) + _dot_nt(rw_hi, h_lo) + _dot_nt(rw_lo, h_hi)
    aff = jax.nn.sigmoid(logits)
    sel = aff + rb_ref[...]
    n_tok = sel.shape[1]

    def top2_sum(a, b, c, d):
        hi1, lo1 = jnp.maximum(a, b), jnp.minimum(a, b)
        hi2, lo2 = jnp.maximum(c, d), jnp.minimum(c, d)
        return jnp.maximum(hi1, hi2) + jnp.maximum(jnp.minimum(hi1, hi2), jnp.maximum(lo1, lo2))

    scores = []
    for g in range(N_GROUPS):
        rows = [sel[EXPERTS_PER_GROUP * g + j:EXPERTS_PER_GROUP * g + j + 1, :] for j in range(EXPERTS_PER_GROUP)]
        scores.append(top2_sum(*rows))
    best = jnp.zeros((1, n_tok), jnp.int32)
    best_score = scores[0]
    for g in range(1, N_GROUPS):
        upd = scores[g] > best_score
        best = jnp.where(upd, g, best)
        best_score = jnp.where(upd, scores[g], best_score)

    eid_i = _iota((N_EXPERTS, n_tok), 0)
    eid = eid_i.astype(F32)
    neg = jnp.float32(-jnp.inf)
    msel = jnp.where(eid_i // EXPERTS_PER_GROUP == best, sel, neg)
    m1 = jnp.max(msel, axis=0, keepdims=True)
    idx1 = jnp.min(jnp.where(msel == m1, eid, float(N_EXPERTS)), axis=0, keepdims=True)
    msel2 = jnp.where(eid == idx1, neg, msel)
    m2 = jnp.max(msel2, axis=0, keepdims=True)
    idx2 = jnp.min(jnp.where(msel2 == m2, eid, float(N_EXPERTS)), axis=0, keepdims=True)
    w1 = jnp.sum(jnp.where(eid == idx1, aff, 0.0), axis=0, keepdims=True)
    w2 = jnp.sum(jnp.where(eid == idx2, aff, 0.0), axis=0, keepdims=True)
    wsum = w1 + w2
    return idx1.astype(jnp.int32), idx2.astype(jnp.int32), w1 / wsum, w2 / wsum


def _local_slots(idx1, idx2):
    n_tok = idx1.shape[1]
    eid = _iota((N_EXPERTS, n_tok), 0)
    hot1, hot2 = eid == idx1, eid == idx2
    hot = jnp.where(hot1, 1.0, jnp.where(hot2, 1.0, 0.0))
    earlier = jnp.where(_iota((n_tok, n_tok), 0) < _iota((n_tok, n_tok), 1), 1.0, 0.0).astype(BF16)
    before_in_expert = _dot(hot.astype(BF16), earlier)
    counts = jnp.sum(hot, axis=1, keepdims=True)
    lower = jnp.where(_iota((N_EXPERTS, N_EXPERTS), 1) < _iota((N_EXPERTS, N_EXPERTS), 0), 1.0, 0.0).astype(BF16)
    first_slot = _dot(lower, jnp.broadcast_to(counts, (N_EXPERTS, LANES)).astype(BF16))[:, 0:1]
    slot = before_in_expert + first_slot
    slot1 = jnp.sum(jnp.where(hot1, slot, 0.0), axis=0, keepdims=True)
    slot2 = jnp.sum(jnp.where(hot2, slot, 0.0), axis=0, keepdims=True)
    return slot1, slot2, counts


def _slot_one_hot(slot1, slot2, v1, v2):
    n_tok = slot1.shape[1]
    row = _iota((2 * n_tok, n_tok), 0).astype(F32)
    return jnp.where(row == slot1, v1, jnp.where(row == slot2, v2, 0.0))


N_MIXER_WEIGHTS = 15
CTX_SEQS_PER_STEP = 2
MAX_INLINE_BLOCKS = 2


def _mixer_kernel(n_tok, n_par, latent, n_alias, lam_init, *refs):
    it = iter(refs)
    x_ref, mod_ref = next(it), next(it)
    (n1_ref, n2_ref, win_ref, wout_ref, sw_ref, sb_ref, qg_ref, kg_ref, dl_ref, dg_ref,
     w2c_ref, gb_ref, gg_ref, rwt_ref, rb_ref) = (next(it) for _ in range(N_MIXER_WEIGHTS))
    if latent:
        ck_ref, cv_ref, st0_ref, cos_ref, sin_ref = (next(it) for _ in range(5))
    for _ in range(n_alias):
        next(it)
    xo_ref, hs_ref, slot_ref, wt_ref, cnt_ref = (next(it) for _ in range(5))
    if not latent:
        ko_ref, vo_ref, so_ref = (next(it) for _ in range(3))
    proj_ref, mix_ref, q_ref, k_ref, v_ref = (next(it) for _ in range(5))
    gq_ref, gke_ref, gv_ref, gr_ref, dec_ref, go_ref, st_ref = (next(it) for _ in range(7))

    n_blk = n_tok // ROW_BLOCK
    n_ctx = k_ref.shape[0] - n_par * n_tok
    n_keys = n_ctx + n_tok
    mod = mod_ref[0, 0]

    by_block = latent

    def whole_sequence(fn):
        def run():
            fn()

        if by_block:
            pl.when(pl.program_id(1) == 0)(run)
        else:
            run()

    def blocks(body):
        if n_par * n_blk <= MAX_INLINE_BLOCKS:
            for r in range(n_par * n_blk):
                body(r)
        else:
            def step(r, carry):
                body(r)
                return carry
            whole_sequence(lambda: lax.fori_loop(0, n_par * n_blk, step, 0))

    def aligned(start, size):
        return pl.ds(start if isinstance(start, int) else pl.multiple_of(start, size), size)

    def block_rows(r, offset=0):
        return aligned(offset + r * ROW_BLOCK, ROW_BLOCK)

    if not latent:
        for ref in (ko_ref, vo_ref, so_ref):
            for q in range(n_par):
                for other in range(1, ref.shape[1]):
                    ref[q, other] = jnp.zeros(ref.shape[2:], F32)

    lane_group = _iota((SGU_CHUNK, SGU_W), 1) // SGU_GROUP_W
    blk_r = _iota((ROW_BLOCK, ROW_BLOCK), 0)
    blk_c = _iota((ROW_BLOCK, ROW_BLOCK), 1)
    same_chunk = (blk_r // GLA_CHUNK) == (blk_c // GLA_CHUNK)
    tri = (jnp.where(same_chunk & (blk_c <= blk_r), 1.0, 0.0).astype(BF16),
           jnp.where(same_chunk & (blk_c >= blk_r), 1.0, 0.0).astype(BF16))
    chunks_per_blk = ROW_BLOCK // GLA_CHUNK
    head_of_lane = _iota((GLA_CHUNK, W_GLA), 1) // GLA_DK
    stack_r = _iota((GLA_HEADS * GLA_CHUNK, GLA_CHUNK), 0) % GLA_CHUNK
    stack_c = _iota((GLA_HEADS * GLA_CHUNK, GLA_CHUNK), 1)
    causal = (stack_c <= stack_r, stack_c >= stack_r)

    if latent:
        def cached_context():
            for h in range(DIFF_HEADS):
                k_ref[0:n_ctx, h * DIFF_V:(h + 1) * DIFF_V] = ck_ref[0, 0, h].astype(BF16)
                v_ref[0:n_ctx, h * DIFF_V:(h + 1) * DIFF_V] = cv_ref[0, 0, h].astype(BF16)
            st_ref[0] = st0_ref[0, 0]

        whole_sequence(cached_context)
        pair_lo = (_iota((ROW_BLOCK, W_QK), 1) % (DIFF_QK // 2)) < (DIFF_QK // 4)

        def rope(z, rows):
            cos = jnp.concatenate([cos_ref[rows, :]] * DIFF_HEADS, axis=-1)
            sin = jnp.concatenate([sin_ref[rows, :]] * DIFF_HEADS, axis=-1)
            shift = DIFF_QK // 4
            swapped = jnp.where(pair_lo, pltpu.roll(z, W_QK - shift, 1), pltpu.roll(z, shift, 1))
            return z * cos + swapped * sin

    def modulated_input(r):
        h = _row_rms(x_ref[block_rows(r), :]) * n1_ref[0]
        return (h * (1.0 + mod[1:2, :]) + mod[0:1, :]).astype(BF16)

    def spatial_gating(r):
        for c in range(ROW_BLOCK // SGU_CHUNK):
            local = slice(c * SGU_CHUNK, (c + 1) * SGU_CHUNK)
            u = jax.nn.gelu(proj_ref[local, C_AU:C_AU + SGU_W])
            v = _group_rms(jax.nn.gelu(proj_ref[local, C_AV:C_AV + SGU_W]), SGU_GROUP_W).astype(BF16)
            s = sb_ref[0]
            for g in range(SGU_GROUPS):
                s = s + jnp.where(lane_group == g, _dot(sw_ref[0, g], v), 0.0)
            mix_ref[aligned(r * ROW_BLOCK + c * SGU_CHUNK, SGU_CHUNK), M_A:M_A + SGU_W] = (u * s).astype(BF16)

    def attention_operands(r):
        rows = block_rows(r)
        key_rows = block_rows(r, n_ctx)
        seq, seq_rows = r // n_blk, block_rows(r % n_blk)
        qn = _group_rms(proj_ref[:, C_BQ:C_BQ + W_QK], DIFF_QK) * qg_ref[0]
        kn = _group_rms(proj_ref[:, C_BK:C_BK + W_QK], DIFF_QK) * kg_ref[0]
        vv = proj_ref[:, C_BV:C_BV + W_QK]
        if latent:
            qn, kn = rope(qn, rows), rope(kn, rows)
        else:
            for h in range(DIFF_HEADS):
                for i in range(2):
                    lo = h * DIFF_V + i * DIFF_QK
                    ko_ref[seq, 0, h, i, seq_rows, :] = kn[:, lo:lo + DIFF_QK]
                vo_ref[seq, 0, h, seq_rows, :] = vv[:, h * DIFF_V:(h + 1) * DIFF_V]
        q_ref[rows, :] = (qn * (DIFF_QK ** -0.5)).astype(BF16)
        k_ref[key_rows, :] = kn.astype(BF16)
        v_ref[key_rows, :] = vv.astype(BF16)

    def gla_operands(r):
        rows = block_rows(r)
        gpre = _dot(proj_ref[:, C_LR:C_LR + LANES].astype(BF16), w2c_ref[0]) + gb_ref[0]
        gate = _log_sigmoid(gpre) * (1.0 / GLA_GATE_NORM)
        gq = proj_ref[:, C_CQ:C_CQ + W_GLA] * (GLA_DK ** -0.5)
        gk = proj_ref[:, C_CK:C_CK + W_GLA]
        gv = proj_ref[:, C_CV:C_CV + W_GLA].astype(BF16)
        gv_ref[rows, :] = gv
        gr_ref[rows, :] = proj_ref[:, C_CR:C_CR + W_GLA]
        for d in range(2):
            g = gate[:, d * W_GLA:(d + 1) * W_GLA]
            b = sum(_dot(tri[d], p) for p in _split3(g))
            last = GLA_CHUNK - 1 if d == 0 else 0
            b_last = jnp.concatenate(
                [jnp.broadcast_to(b[c * GLA_CHUNK + last:c * GLA_CHUNK + last + 1, :], (GLA_CHUNK, W_GLA))
                 for c in range(chunks_per_blk)], axis=0)
            q_dec = (gq * jnp.exp(b)).astype(BF16)
            k_inv = (gk * jnp.exp(-b)).astype(BF16)
            gq_ref[d, rows, :] = q_dec
            gke_ref[d, rows, :] = (gk * jnp.exp(b_last - b)).astype(BF16)
            for c in range(chunks_per_blk):
                row = c * GLA_CHUNK + last
                dec_ref[d, r * chunks_per_blk + c] = jnp.exp(b[row:row + 1, :])
                chunk = slice(c * GLA_CHUNK, (c + 1) * GLA_CHUNK)
                qd = q_dec[chunk]
                q_stack = jnp.concatenate(
                    [jnp.where(head_of_lane == h, qd, jnp.zeros_like(qd)) for h in range(GLA_HEADS)], axis=0)
                attn = jnp.where(causal[d], _dot_nt(q_stack, k_inv[chunk]), 0.0)
                spread = _dot(attn.astype(BF16), gv[chunk])
                o = jnp.zeros((GLA_CHUNK, W_GLA), F32)
                for h in range(GLA_HEADS):
                    o = o + jnp.where(head_of_lane == h, spread[h * GLA_CHUNK:(h + 1) * GLA_CHUNK, :], 0.0)
                go_ref[d, aligned(r * ROW_BLOCK + c * GLA_CHUNK, GLA_CHUNK), :] = o

    def project_and_split(r):
        proj_ref[...] = _dot(modulated_input(r), win_ref[0])
        spatial_gating(r)
        attention_operands(r)
        gla_operands(r)

    blocks(project_and_split)

    dl = dl_ref[0]
    lam = (jnp.exp(jnp.sum(dl[0:1] * dl[1:2], axis=-1, keepdims=True))
           - jnp.exp(jnp.sum(dl[2:3] * dl[3:4], axis=-1, keepdims=True)) + lam_init)
    sub0 = (_iota((ROW_BLOCK, DIFF_V), 1) < DIFF_QK)

    def softmax(s):
        e = jnp.exp(s - jnp.max(s, axis=-1, keepdims=True))
        return e, jnp.sum(e, axis=-1, keepdims=True)

    def attn_block(r):
        rows = block_rows(r)
        keys = aligned((r // n_blk) * n_keys, n_keys)
        for h in range(DIFF_HEADS):
            cols = slice(h * DIFF_V, (h + 1) * DIFF_V)
            qh = q_ref[rows, cols]
            kh = k_ref[keys, cols]
            e0, z0 = softmax(_dot_nt(jnp.where(sub0, qh, jnp.zeros_like(qh)), kh))
            e1, z1 = softmax(_dot_nt(jnp.where(sub0, jnp.zeros_like(qh), qh), kh))
            w = e0 / z0 - lam * (e1 / z1)
            o = _dot(w.astype(BF16), v_ref[keys, cols])
            o = _row_rms(o) * dg_ref[0] * (1.0 - lam_init)
            mix_ref[rows, M_B + h * DIFF_V:M_B + (h + 1) * DIFF_V] = o.astype(BF16)

    blocks(attn_block)

    if not latent:
        st_ref[...] = jnp.zeros(st_ref.shape, F32)

    n_chunk = n_tok // GLA_CHUNK
    st_diag = (_iota((W_GLA, W_GLA), 0) // GLA_DV) == (_iota((W_GLA, W_GLA), 1) // GLA_DK)

    def gla_step(c, carry):
        for seq in range(n_par):
            for d in range(2):
                cc = seq * n_chunk + (c if d == 0 else n_chunk - 1 - c)
                rows = pl.ds(pl.multiple_of(cc * GLA_CHUNK, GLA_CHUNK), GLA_CHUNK)
                st = st_ref[seq, d]
                go_ref[d, rows, :] = go_ref[d, rows, :] + _dot_nt(gq_ref[d, rows, :], st.astype(BF16))
                upd = _dot_tn(gv_ref[rows, :], gke_ref[d, rows, :])
                st_ref[seq, d] = dec_ref[d, cc] * st + jnp.where(st_diag, upd, 0.0)
        return carry

    whole_sequence(lambda: lax.fori_loop(0, n_chunk, gla_step, 0))

    if not latent:
        for seq in range(n_par):
            for d in range(2):
                s_full = st_ref[seq, d].T
                for h in range(GLA_HEADS):
                    so_ref[seq, 0, d, h] = s_full[h * GLA_DK:(h + 1) * GLA_DK, h * GLA_DV:(h + 1) * GLA_DV]

    def finish_block(r, out_r):
        rows, out_rows = block_rows(r), block_rows(out_r)
        oc = _group_rms(go_ref[0, rows, :] + go_ref[1, rows, :], GLA_DV) * gg_ref[0]
        oc = oc * jax.nn.silu(gr_ref[rows, :])
        mix_ref[rows, M_C:M_C + W_GLA] = oc.astype(BF16)
        x1 = x_ref[rows, :] + mod[2:3, :] * _dot(mix_ref[rows, :], wout_ref[0])
        xo_ref[out_rows, :] = x1
        hn = _row_rms(x1) * n2_ref[0]
        hn = hn * (1.0 + mod[4:5, :]) + mod[3:4, :]
        idx1, idx2, w1, w2 = _route(hn, rwt_ref, rb_ref)
        slot1, slot2, counts = _local_slots(idx1, idx2)
        perm = _slot_one_hot(slot1, slot2, 1.0, 1.0).astype(BF16)
        _to_row_slabs(hs_ref, 2 * out_r * ROW_BLOCK, _dot(perm, hn.astype(BF16)))
        slot_ref[:, out_rows] = jnp.concatenate([slot1, slot2], axis=0).astype(jnp.int32)
        wt_ref[:, out_rows] = jnp.concatenate([w1, w2], axis=0)
        cnt_ref[out_r] = jnp.broadcast_to(counts, (N_EXPERTS, LANES)).astype(jnp.int32)

    if by_block:
        finish_block(pl.program_id(1), 0)
    else:
        for r in range(n_par * n_blk):
            finish_block(r, r)


def _mixer_call(l, n_tok, n_par, latent, x, mods_all, weights, extras, cache_bufs):
    n_seq = x.shape[0] // n_tok
    n_all = x.shape[0]
    assert n_seq % n_par == 0 and not (latent and n_par > 1)
    n_step_tok = n_par * n_tok
    n_keys = n_step_tok + (extras[0].shape[3] if latent else 0)
    n_chunk = n_step_tok // GLA_CHUNK
    lam_init = 0.8 - 0.6 * math.exp(-0.3 * l)

    single = pl.Buffered(1)
    seq_mode = single if latent else None

    def layer(arr):
        tail = arr.shape[1:]
        return pl.BlockSpec((1,) + tail, lambda s, *_r, _n=len(tail): (l,) + (0,) * _n, pipeline_mode=single)

    def const(arr):
        return pl.BlockSpec(arr.shape, lambda s, *_r, _n=arr.ndim: (0,) * _n, pipeline_mode=single)

    def tok_spec(width):
        return pl.BlockSpec((n_step_tok, width), lambda s, *_r: (s, 0), pipeline_mode=seq_mode)

    mod_row = (lambda s: 1 + s) if latent else (lambda s: 0)
    in_specs = [tok_spec(D_MODEL),
                pl.BlockSpec((1, 1, 6, D_MODEL), lambda s, *_r: (l, mod_row(s), 0, 0))]
    in_specs += [layer(w) for w in weights[:N_MIXER_WEIGHTS - 2]] + [const(w) for w in weights[-2:]]
    operands = [x, mods_all] + list(weights)
    if latent:
        ck, cv, st0, cos, sin = extras
        in_specs += [
            pl.BlockSpec((1, 1) + ck.shape[2:], lambda s, *_r: (s, l, 0, 0, 0)),
            pl.BlockSpec((1, 1) + cv.shape[2:], lambda s, *_r: (s, l, 0, 0, 0)),
            pl.BlockSpec((1, 1) + st0.shape[2:], lambda s, *_r: (s, l, 0, 0, 0)),
            const(cos), const(sin),
        ]
        operands += [ck, cv, st0, cos, sin]
    n_in = len(operands)
    in_specs += [pl.BlockSpec(memory_space=pl.ANY)] * len(cache_bufs)
    operands += list(cache_bufs)

    tiles_per_step = n_step_tok // ROW_BLOCK
    out_shape = [
        jax.ShapeDtypeStruct((n_all, D_MODEL), F32),
        jax.ShapeDtypeStruct((2 * n_all * ROW_SLABS, LANES), F32),
        jax.ShapeDtypeStruct((2, n_all), jnp.int32),
        jax.ShapeDtypeStruct((2, n_all), F32),
        jax.ShapeDtypeStruct((n_all // ROW_BLOCK, N_EXPERTS, LANES), jnp.int32),
    ]
    if latent:
        grid = (n_seq, tiles_per_step)
        out_tok, out_tiles = ROW_BLOCK, 1
        at = lambda s, r: s * tiles_per_step + r
    else:
        grid = (n_seq // n_par,)
        out_tok, out_tiles = n_step_tok, tiles_per_step
        at = lambda s: s
    out_specs = [
        pl.BlockSpec((out_tok, D_MODEL), lambda *g: (at(*g), 0)),
        pl.BlockSpec((2 * out_tok * ROW_SLABS, LANES), lambda *g: (at(*g), 0)),
        pl.BlockSpec((2, out_tok), lambda *g: (0, at(*g))),
        pl.BlockSpec((2, out_tok), lambda *g: (0, at(*g))),
        pl.BlockSpec((out_tiles, N_EXPERTS, LANES), lambda *g: (at(*g), 0, 0)),
    ]
    n_shared_out = len(out_shape)
    aliases = {}
    if not latent:
        out_shape += [
            jax.ShapeDtypeStruct((n_seq, DEPTH, DIFF_HEADS, 2, n_tok, DIFF_QK), F32),
            jax.ShapeDtypeStruct((n_seq, DEPTH, DIFF_HEADS, n_tok, DIFF_V), F32),
            jax.ShapeDtypeStruct((n_seq, DEPTH, 2, GLA_HEADS, GLA_DK, GLA_DV), F32),
        ]
        n_lay, lay = (1, l) if cache_bufs else (DEPTH, 0)
        out_specs += [
            pl.BlockSpec((n_par, n_lay, DIFF_HEADS, 2, n_tok, DIFF_QK), lambda s: (s, lay, 0, 0, 0, 0)),
            pl.BlockSpec((n_par, n_lay, DIFF_HEADS, n_tok, DIFF_V), lambda s: (s, lay, 0, 0, 0)),
            pl.BlockSpec((n_par, n_lay, 2, GLA_HEADS, GLA_DK, GLA_DV), lambda s: (s, lay, 0, 0, 0, 0)),
        ]
        aliases = {n_in + j: n_shared_out + j for j in range(len(cache_bufs))}
    scratch = [
        pltpu.VMEM((ROW_BLOCK, D_PROJ_PAD), F32),
        pltpu.VMEM((n_step_tok, D_MODEL), BF16),
        pltpu.VMEM((n_step_tok, W_QK), BF16),
        pltpu.VMEM((n_keys, W_QK), BF16),
        pltpu.VMEM((n_keys, W_QK), BF16),
        pltpu.VMEM((2, n_step_tok, W_GLA), BF16),
        pltpu.VMEM((2, n_step_tok, W_GLA), BF16),
        pltpu.VMEM((n_step_tok, W_GLA), BF16),
        pltpu.VMEM((n_step_tok, W_GLA), F32),
        pltpu.VMEM((2, n_chunk, 1, W_GLA), F32),
        pltpu.VMEM((2, n_step_tok, W_GLA), F32),
        pltpu.VMEM((n_par, 2, W_GLA, W_GLA), F32),
    ]
    return pl.pallas_call(
        functools.partial(_mixer_kernel, n_tok, n_par, latent, len(cache_bufs), lam_init),
        grid=grid,
        in_specs=in_specs,
        out_specs=out_specs,
        out_shape=out_shape,
        scratch_shapes=scratch,
        input_output_aliases=aliases,
        compiler_params=pltpu.CompilerParams(
            dimension_semantics=("arbitrary",) * len(grid), vmem_limit_bytes=MIXER_VMEM_LIMIT),
        name="mixer_latent" if latent else "mixer_context",
    )(*operands)


PAIR_BLOCK = 2 * ROW_BLOCK
COPY_SIZES = tuple(ROW_BLOCK >> k for k in range(ROW_BLOCK.bit_length()))
LARGE_COPY = 64
GATHER_AHEAD = 2
GATHER_SLOTS = GATHER_AHEAD + 1


def _segment_copies(n_rows, make_copy, act):
    def copy_if_set(size):
        @pl.when((n_rows & size) != 0)
        def _():
            act(make_copy(n_rows & (-2 * size), size))

    n_large = COPY_SIZES.index(LARGE_COPY) + 1

    @pl.when(n_rows >= LARGE_COPY)
    def _():
        for size in COPY_SIZES[:n_large]:
            copy_if_set(size)

    for size in COPY_SIZES[n_large:]:
        copy_if_set(size)


def _start(copy):
    copy.start()


def _wait(copy):
    copy.wait()


def _slab_rows(first_row, n_rows, slab):
    return pl.ds(first_row * ROW_SLABS + slab, n_rows, stride=ROW_SLABS)


def _to_row_slabs(ref, first_row, value):
    for s in range(ROW_SLABS):
        ref[_slab_rows(first_row, value.shape[0], s), :] = value[:, s * LANES:(s + 1) * LANES]


def _from_row_slabs(ref, first_row, n_rows):
    return jnp.concatenate([ref[_slab_rows(first_row, n_rows, s), :] for s in range(ROW_SLABS)], axis=-1)


SLAB_PAIR_W = 2 * LANES
N_SLAB_PAIRS = ROW_SLABS // 2


def _slab_pair_cols(g):
    return slice(g * SLAB_PAIR_W, (g + 1) * SLAB_PAIR_W)


def _load_slab_pair(ref, first_row, n_rows, g):
    return jnp.concatenate([ref[_slab_rows(first_row, n_rows, s), :] for s in (2 * g, 2 * g + 1)], axis=-1)


def _row_span(ref, first_row, n_rows):
    return ref.at[pl.ds(pl.multiple_of(first_row * ROW_SLABS, ROW_SLABS), n_rows * ROW_SLABS)]


def _two_streams(n_first_tiles):
    def first(i, *_):
        return (jnp.minimum(i, n_first_tiles - 1), 0)

    def second(i, *_):
        return (jnp.maximum(i - n_first_tiles, 0), 0)

    return first, second


N_EXPERT_TABLES = 10
OUT_SLOTS = 2


def _expert_kernel(n_ctx_tiles, n_tiles_max, te_ref, first_ref, rows_ref, jlo_ref, jhi_ref, cpre_ref, cnt_ref,
                   lofs_ref, tile0_ref, ntile_ref, hs_c_ref, hs_l_ref, w1_ref, w3_ref, w2_ref, ys_ref,
                   xbuf_ref, obuf_ref, w1b_ref, w3b_ref, w2b_ref, sem, out_sem):
    expert = pl.program_id(0)
    n_tiles = tile0_ref[N_EXPERTS]

    def gather(t, act):
        slot = t % GATHER_SLOTS
        e, first = te_ref[t], first_ref[t]
        last = first + rows_ref[t]

        def segment_of(hs_ref, first_tile):
            def body(j, carry):
                k = j * N_EXPERTS + e
                seg_first = cpre_ref[k]
                lo = jnp.maximum(seg_first, first)
                n = jnp.maximum(jnp.minimum(seg_first + cnt_ref[k], last) - lo, 0)
                src = (j - first_tile) * PAIR_BLOCK + lofs_ref[k] + (lo - seg_first)
                dst = slot * ROW_BLOCK + lo - first
                _segment_copies(n, lambda done, size: pltpu.make_async_copy(
                    _row_span(hs_ref, src + done, size), _row_span(xbuf_ref, dst + done, size), sem.at[slot]), act)
                return carry
            return body

        jlo, jhi = jlo_ref[t], jhi_ref[t]
        lax.fori_loop(jnp.minimum(jlo, n_ctx_tiles), jnp.minimum(jhi, n_ctx_tiles), segment_of(hs_c_ref, 0), 0)
        lax.fori_loop(jnp.maximum(jlo, n_ctx_tiles), jnp.maximum(jhi, n_ctx_tiles),
                      segment_of(hs_l_ref, n_ctx_tiles), 0)

    def out_copy(t, oslot):
        return pltpu.make_async_copy(
            _row_span(obuf_ref, oslot * ROW_BLOCK, ROW_BLOCK), _row_span(ys_ref, t * ROW_BLOCK, ROW_BLOCK),
            out_sem.at[oslot])

    @pl.when(expert == 0)
    def _():
        xbuf_ref[...] = jnp.zeros(xbuf_ref.shape, F32)
        for t in range(GATHER_AHEAD):
            gather(t, _start)

    w1b_ref[...] = w1_ref[0, 0].astype(BF16)
    w3b_ref[...] = w3_ref[0, 0].astype(BF16)
    w2b_ref[...] = w2_ref[0, 0].astype(BF16)
    tile0, n_own = tile0_ref[expert], ntile_ref[expert]

    def tile_body(k, carry):
        t = tile0 + k
        slot, oslot = t % GATHER_SLOTS, k % OUT_SLOTS

        @pl.when(t + GATHER_AHEAD < n_tiles)
        def _():
            gather(t + GATHER_AHEAD, _start)

        n_rows = rows_ref[t]
        _segment_copies(n_rows, lambda done, size: pltpu.make_async_copy(
            _row_span(hs_c_ref, done, size), _row_span(xbuf_ref, slot * ROW_BLOCK + done, size), sem.at[slot]), _wait)

        @pl.when(k >= OUT_SLOTS)
        def _():
            out_copy(t, oslot).wait()

        live = _iota((ROW_BLOCK, D_MODEL), 0) < n_rows
        x = jnp.where(live, _from_row_slabs(xbuf_ref, slot * ROW_BLOCK, ROW_BLOCK), 0.0).astype(BF16)
        hid = jax.nn.silu(_dot(x, w1b_ref[...])) * _dot(x, w3b_ref[...])
        _to_row_slabs(obuf_ref, oslot * ROW_BLOCK, _dot(hid.astype(BF16), w2b_ref[...]))
        out_copy(t, oslot).start()
        return carry

    lax.fori_loop(0, n_own, tile_body, 0)
    for oslot in range(OUT_SLOTS):
        @pl.when(n_own > oslot)
        def _():
            out_copy(tile0, oslot).wait()

    @pl.when(expert == N_EXPERTS - 1)
    def _():
        obuf_ref[...] = jnp.zeros(obuf_ref.shape, F32)

        def fill(t, carry):
            out_copy(t, 0).start()
            out_copy(t, 0).wait()
            return carry

        lax.fori_loop(n_tiles, n_tiles_max, fill, 0)


def _expert_call(l, plan, hs_c, hs_l, w1, w3, w2):
    tables = plan["expert_tables"]
    n_tiles_max = tables[0].shape[0]
    n_ctx_tiles = hs_c.shape[0] // (PAIR_BLOCK * ROW_SLABS)

    def weight(shape):
        return pl.BlockSpec((1, 1) + shape, lambda e, *_: (l, e, 0, 0))

    return pl.pallas_call(
        functools.partial(_expert_kernel, n_ctx_tiles, n_tiles_max),
        grid_spec=pltpu.PrefetchScalarGridSpec(
            num_scalar_prefetch=N_EXPERT_TABLES,
            grid=(N_EXPERTS,),
            in_specs=[pl.BlockSpec(memory_space=pl.ANY), pl.BlockSpec(memory_space=pl.ANY),
                      weight((D_MODEL, D_EXPERT)), weight((D_MODEL, D_EXPERT)), weight((D_EXPERT, D_MODEL))],
            out_specs=pl.BlockSpec(memory_space=pl.ANY),
            scratch_shapes=[pltpu.VMEM((GATHER_SLOTS * ROW_BLOCK * ROW_SLABS, LANES), F32),
                            pltpu.VMEM((OUT_SLOTS * ROW_BLOCK * ROW_SLABS, LANES), F32),
                            pltpu.VMEM((D_MODEL, D_EXPERT), BF16), pltpu.VMEM((D_MODEL, D_EXPERT), BF16),
                            pltpu.VMEM((D_EXPERT, D_MODEL), BF16),
                            pltpu.SemaphoreType.DMA((GATHER_SLOTS,)), pltpu.SemaphoreType.DMA((OUT_SLOTS,))],
        ),
        out_shape=jax.ShapeDtypeStruct((n_tiles_max * ROW_BLOCK * ROW_SLABS, LANES), F32),
        compiler_params=pltpu.CompilerParams(
            dimension_semantics=("arbitrary",), vmem_limit_bytes=SMALL_KERNEL_VMEM_LIMIT),
        name="moe_experts",
    )(*tables, hs_c, hs_l, w1, w3, w2)


N_COMBINE_TABLES = 4
COMBINE_TILES_PER_STEP = 2


def _combine_kernel(n_ctx_steps, cnt_ref, cpre_ref, lofs_ref, starts_ref, x_c_ref, x_l_ref, slot_c_ref, slot_l_ref,
                    wt_c_ref, wt_l_ref, mod_ref, ys_ref, xo_c_ref, xo_l_ref, buf_ref, sem):
    step = pl.program_id(0)
    n_tiles = pl.num_programs(0) * COMBINE_TILES_PER_STEP

    def collect(t, act):
        slot = t % GATHER_SLOTS

        def body(e, carry):
            k = t * N_EXPERTS + e
            src, dst = starts_ref[e] + cpre_ref[k], slot * PAIR_BLOCK + lofs_ref[k]
            _segment_copies(cnt_ref[k], lambda done, size: pltpu.make_async_copy(
                _row_span(ys_ref, src + done, size), _row_span(buf_ref, dst + done, size), sem.at[slot]), act)
            return carry

        lax.fori_loop(0, N_EXPERTS, body, 0)

    @pl.when(step == 0)
    def _():
        for t in range(GATHER_AHEAD):
            collect(t, _start)

    gate = mod_ref[0, 0, 5:6, :]

    def finish(part, first_row, x_ref, slot_ref, wt_ref, xo_ref):
        tokens = pl.ds(part * ROW_BLOCK, ROW_BLOCK)
        slots, wts = slot_ref[:, tokens].astype(F32), wt_ref[:, tokens]
        slot1, slot2 = slots[0:1], slots[1:2]
        weight_of_row = jnp.sum(_slot_one_hot(slot1, slot2, wts[0:1], wts[1:2]), axis=1, keepdims=True)
        gather_rows = _slot_one_hot(slot1, slot2, 1.0, 1.0).T.astype(BF16)
        for g in range(N_SLAB_PAIRS):
            cols = _slab_pair_cols(g)
            hi, lo = _split2(_load_slab_pair(buf_ref, first_row, PAIR_BLOCK, g) * weight_of_row)
            y = _dot(gather_rows, hi) + _dot(gather_rows, lo)
            xo_ref[tokens, cols] = x_ref[tokens, cols] + gate[:, cols] * y

    for part in range(COMBINE_TILES_PER_STEP):
        j = step * COMBINE_TILES_PER_STEP + part

        @pl.when(j + GATHER_AHEAD < n_tiles)
        def _():
            collect(j + GATHER_AHEAD, _start)

        slot = j % GATHER_SLOTS
        for piece in range(PAIR_BLOCK // ROW_BLOCK):
            pltpu.make_async_copy(
                _row_span(ys_ref, piece * ROW_BLOCK, ROW_BLOCK),
                _row_span(buf_ref, slot * PAIR_BLOCK + piece * ROW_BLOCK, ROW_BLOCK), sem.at[slot]).wait()
        @pl.when(step < n_ctx_steps)
        def _():
            finish(part, slot * PAIR_BLOCK, x_c_ref, slot_c_ref, wt_c_ref, xo_c_ref)

        @pl.when(step >= n_ctx_steps)
        def _():
            finish(part, slot * PAIR_BLOCK, x_l_ref, slot_l_ref, wt_l_ref, xo_l_ref)


def _combine_call(l, plan, x_c, x_l, slot_c, slot_l, wt_c, wt_l, mods_all, mod_row_of_tile, ys):
    step_rows = COMBINE_TILES_PER_STEP * ROW_BLOCK
    n_steps = (x_c.shape[0] + x_l.shape[0]) // step_rows
    n_ctx_steps = x_c.shape[0] // step_rows
    first, second = _two_streams(n_ctx_steps)

    def lanes(index_map):
        return lambda i, *_: index_map(i)[::-1]

    return pl.pallas_call(
        functools.partial(_combine_kernel, n_ctx_steps),
        grid_spec=pltpu.PrefetchScalarGridSpec(
            num_scalar_prefetch=N_COMBINE_TABLES,
            grid=(n_steps,),
            in_specs=[pl.BlockSpec((step_rows, D_MODEL), first),
                      pl.BlockSpec((step_rows, D_MODEL), second),
                      pl.BlockSpec((2, step_rows), lanes(first)),
                      pl.BlockSpec((2, step_rows), lanes(second)),
                      pl.BlockSpec((2, step_rows), lanes(first)),
                      pl.BlockSpec((2, step_rows), lanes(second)),
                      pl.BlockSpec((1, 1, 6, D_MODEL),
                                   lambda i, *_: (l, mod_row_of_tile(i * COMBINE_TILES_PER_STEP), 0, 0)),
                      pl.BlockSpec(memory_space=pl.ANY)],
            out_specs=[pl.BlockSpec((step_rows, D_MODEL), first),
                       pl.BlockSpec((step_rows, D_MODEL), second)],
            scratch_shapes=[pltpu.VMEM((GATHER_SLOTS * PAIR_BLOCK * ROW_SLABS, LANES), F32),
                            pltpu.SemaphoreType.DMA((GATHER_SLOTS,))],
        ),
        out_shape=[jax.ShapeDtypeStruct(x_c.shape, F32), jax.ShapeDtypeStruct(x_l.shape, F32)],
        compiler_params=pltpu.CompilerParams(
            dimension_semantics=("arbitrary",), vmem_limit_bytes=SMALL_KERNEL_VMEM_LIMIT),
        name="moe_combine",
    )(*plan["combine_tables"], x_c, x_l, slot_c, slot_l, wt_c, wt_l, mods_all, ys)


def _moe_plan(cnt):
    n_tok_tiles = cnt.shape[0]
    n_tiles = n_tok_tiles * PAIR_BLOCK // ROW_BLOCK + N_EXPERTS
    lofs = jnp.cumsum(cnt, axis=1) - cnt
    cpre = jnp.cumsum(cnt, axis=0) - cnt
    counts = jnp.sum(cnt, axis=0)
    padded = (counts + ROW_BLOCK - 1) // ROW_BLOCK * ROW_BLOCK
    ends = jnp.cumsum(padded)
    starts = ends - padded
    tile_start = jnp.arange(n_tiles, dtype=jnp.int32) * ROW_BLOCK
    tile_expert = jnp.minimum(
        jnp.sum((tile_start[:, None] >= ends[None, :]).astype(jnp.int32), axis=1), N_EXPERTS - 1)
    hot = tile_expert[:, None] == jnp.arange(N_EXPERTS, dtype=jnp.int32)[None, :]
    first = tile_start - jnp.sum(jnp.where(hot, starts[None, :], 0), axis=1)
    rows = jnp.clip(jnp.sum(jnp.where(hot, counts[None, :], 0), axis=1) - first, 0, ROW_BLOCK)
    seg_first = jnp.sum(jnp.where(hot[:, None, :], cpre[None, :, :], 0), axis=2)
    seg_rows = jnp.sum(jnp.where(hot[:, None, :], cnt[None, :, :], 0), axis=2)
    overlap = (seg_first < (first + rows)[:, None]) & (seg_first + seg_rows > first[:, None])
    j = jnp.arange(n_tok_tiles, dtype=jnp.int32)[None, :]
    jlo = jnp.min(jnp.where(overlap, j, n_tok_tiles), axis=1)
    jhi = jnp.max(jnp.where(overlap, j + 1, 0), axis=1)
    i32 = lambda a: a.astype(jnp.int32).reshape(-1)
    tile0 = jnp.concatenate([starts, ends[-1:]]) // ROW_BLOCK
    return {
        "expert_tables": tuple(i32(a) for a in (tile_expert, first, rows, jlo, jhi, cpre, cnt, lofs,
                                                tile0, padded // ROW_BLOCK)),
        "combine_tables": tuple(i32(a) for a in (cnt, cpre, lofs, starts)),
    }


def _rope_tables(n_tok):
    n_rows = n_tok // GRID_W
    pos_r = jnp.repeat(jnp.arange(n_rows), GRID_W)
    pos_c = jnp.tile(jnp.arange(GRID_W), n_rows)
    half = DIFF_QK // 2
    nf = half // 2
    freqs = ROPE_BASE ** (-jnp.arange(nf, dtype=F32) / nf)

    def tables(pos):
        ang = pos.astype(F32)[:, None] * freqs
        cos, sin = jnp.cos(ang), jnp.sin(ang)
        return jnp.concatenate([cos, cos], axis=-1), jnp.concatenate([-sin, sin], axis=-1)

    cos_r, sin_r = tables(pos_r)
    cos_c, sin_c = tables(pos_c)
    cos = jnp.concatenate([cos_r, cos_c], axis=-1)
    sin = jnp.concatenate([sin_r, sin_c], axis=-1)
    return jnp.concatenate([cos, cos], axis=-1), jnp.concatenate([sin, sin], axis=-1)


def _mixer_weights(w_in, w_out, sgu_w, sgu_b, q_norm_g, k_norm_g, diff_lambda, diff_norm_g, gla_w2, gla_b,
                   gla_norm_g, norm1_g, norm2_g, router_w, router_bias):
    w_in_pad = jnp.pad(w_in.astype(BF16), ((0, 0), (0, 0), (0, D_PROJ_PAD - w_in.shape[2])))
    w2cat = jnp.zeros((DEPTH, LANES, 2 * W_GLA), F32)
    w2cat = w2cat.at[:, 0:GLA_RANK, 0:W_GLA].set(gla_w2[:, 0]).at[:, GLA_RANK:2 * GLA_RANK, W_GLA:].set(gla_w2[:, 1])
    return (
        norm1_g[:, None, :], norm2_g[:, None, :], w_in_pad, w_out.astype(BF16),
        sgu_w.astype(BF16), jnp.repeat(sgu_b.transpose(0, 2, 1), SGU_GROUP_W, axis=2),
        jnp.tile(q_norm_g, (1, W_QK // DIFF_QK))[:, None, :], jnp.tile(k_norm_g, (1, W_QK // DIFF_QK))[:, None, :],
        diff_lambda, diff_norm_g[:, None, :],
        w2cat.astype(BF16), gla_b.reshape(DEPTH, 1, 2 * W_GLA), jnp.tile(gla_norm_g, (1, GLA_HEADS))[:, None, :],
        router_w.T, router_bias[:, None],
    )


def kernel(x_prompt, x_sample, cache_k, cache_v, state_gla, c, c_ctx, w_in, w_out, sgu_w, sgu_b, q_norm_g, k_norm_g,
           diff_lambda, diff_norm_g, gla_w2, gla_b, gla_norm_g, norm1_g, norm2_g, ada_w, ada_b, router_w, router_bias,
           moe_w1, moe_w3, moe_w2):
    n_ctx_seq, ctx_len, _ = x_prompt.shape
    n_lat_seq, lat_len, _ = x_sample.shape
    n_ctx_tok = n_ctx_seq * ctx_len
    n_lat_tok = n_lat_seq * lat_len
    ctx_tiles = n_ctx_tok // ROW_BLOCK
    lat_tiles_per_seq = lat_len // ROW_BLOCK

    n_cond = 1 + n_lat_seq
    cond_t = jnp.zeros((D_MODEL, SUBLANES), F32).at[:, 0].set(c_ctx).at[:, 1:n_cond].set(c.T)
    mods_all = _adaln_call(cond_t, n_cond, ada_w, ada_b)[:, :n_cond].reshape(DEPTH, n_cond, 6, D_MODEL)
    weights = _mixer_weights(w_in, w_out, sgu_w, sgu_b, q_norm_g, k_norm_g, diff_lambda, diff_norm_g, gla_w2, gla_b,
                             gla_norm_g, norm1_g, norm2_g, router_w, router_bias)

    ck_all = cache_k.transpose(0, 1, 2, 4, 3, 5).reshape(cache_k.shape[:3] + (cache_k.shape[4], DIFF_V))
    st_all = jnp.einsum('bldhkv,hg->bldhvgk', state_gla, jnp.eye(GLA_HEADS, dtype=F32)).reshape(
        n_lat_seq, DEPTH, 2, W_GLA, W_GLA)
    cos, sin = _rope_tables(lat_len)
    extras = (ck_all, cache_v, st_all, cos, sin)

    def mod_row_of_tile(i):
        return jnp.where(i < ctx_tiles, 0, 1 + (i - ctx_tiles) // lat_tiles_per_seq)

    x_c = x_prompt.reshape(n_ctx_tok, D_MODEL)
    x_l = x_sample.reshape(n_lat_tok, D_MODEL)
    cache_bufs = ()
    for l in range(DEPTH):
        ctx_par = 1 if l == 0 else CTX_SEQS_PER_STEP
        x1_c, hs_c, slot_c, wt_c, cnt_c, *cache_bufs = _mixer_call(
            l, ctx_len, ctx_par, False, x_c, mods_all, weights, None, tuple(cache_bufs))
        x1_l, hs_l, slot_l, wt_l, cnt_l = _mixer_call(l, lat_len, 1, True, x_l, mods_all, weights, extras, ())
        plan = _moe_plan(jnp.concatenate([cnt_c[:, :, 0], cnt_l[:, :, 0]], axis=0))
        ys = _expert_call(l, plan, hs_c, hs_l, moe_w1, moe_w3, moe_w2)
        x_c, x_l = _combine_call(l, plan, x1_c, x1_l, slot_c, slot_l, wt_c, wt_l, mods_all, mod_row_of_tile, ys)

    new_k, new_v, new_s = cache_bufs
    return (x_c.reshape(x_prompt.shape), x_l.reshape(x_sample.shape), new_k, new_v, new_s)
```

```python
import functools
import math

import jax
import jax.numpy as jnp
from jax import lax
from jax.experimental import pallas as pl
from jax.experimental.pallas import tpu as pltpu

F32 = jnp.float32
BF16 = jnp.bfloat16

D_MODEL = 1024
DEPTH = 4
GRID_W = 64
SGU_GROUPS = 4
SGU_GROUP_W = 64
SGU_W = SGU_GROUPS * SGU_GROUP_W
SGU_CHUNK = 128
DIFF_HEADS = 4
DIFF_QK = 64
DIFF_V = 2 * DIFF_QK
ROPE_BASE = 10000.0
GLA_HEADS = 4
GLA_DK = 64
GLA_DV = 64
GLA_RANK = 16
GLA_GATE_NORM = 16.0
GLA_CHUNK = 64
N_EXPERTS = 16
N_GROUPS = 4
EXPERTS_PER_GROUP = N_EXPERTS // N_GROUPS
D_EXPERT = 512
EPS = 1e-6

LANES = 128
SUBLANES = 8
MXU_DIM = 256
V7X_VMEM_BYTES = 64 * 1024 * 1024
MIXER_VMEM_LIMIT = V7X_VMEM_BYTES * 7 // 8
SMALL_KERNEL_VMEM_LIMIT = V7X_VMEM_BYTES * 5 // 8

ROW_BLOCK = MXU_DIM
ROW_SLABS = D_MODEL // LANES

W_QK = DIFF_HEADS * 2 * DIFF_QK
W_GLA = GLA_HEADS * GLA_DK
C_AU, C_AV = 0, SGU_W
C_BQ = C_AV + SGU_W
C_BK, C_BV = C_BQ + W_QK, C_BQ + 2 * W_QK
C_CQ = C_BV + DIFF_HEADS * DIFF_V
C_CK, C_CV, C_CR, C_LR = C_CQ + W_GLA, C_CQ + 2 * W_GLA, C_CQ + 3 * W_GLA, C_CQ + 4 * W_GLA
D_PROJ_MAIN = C_LR
D_PROJ_PAD = D_PROJ_MAIN + LANES
M_A, M_B, M_C = 0, SGU_W, SGU_W + DIFF_HEADS * DIFF_V


def _split2(x):
    hi = x.astype(BF16)
    lo = (x - hi.astype(F32)).astype(BF16)
    return hi, lo


def _split3(x):
    hi = x.astype(BF16)
    r = x - hi.astype(F32)
    mid = r.astype(BF16)
    lo = (r - mid.astype(F32)).astype(BF16)
    return hi, mid, lo


def _dot(a, b):
    return jnp.dot(a, b, preferred_element_type=F32)


def _dot_nt(a, b):
    return lax.dot_general(a, b, (((1,), (1,)), ((), ())), preferred_element_type=F32)


def _dot_tn(a, b):
    return lax.dot_general(a, b, (((0,), (0,)), ((), ())), preferred_element_type=F32)


def _iota(shape, dim):
    return lax.broadcasted_iota(jnp.int32, shape, dim)


def _block_ones(width, block):
    r = _iota((width, width), 0) // block
    c = _iota((width, width), 1) // block
    return (r == c)


def _group_sum(z, block):
    width = z.shape[-1]
    outs = []
    for s in range(0, width, MXU_DIM):
        w = min(MXU_DIM, width - s)
        ones = _block_ones(w, block).astype(BF16)
        hi, lo = _split2(z[:, s:s + w])
        outs.append(_dot(hi, ones) + _dot(lo, ones))
    return outs[0] if len(outs) == 1 else jnp.concatenate(outs, axis=-1)


def _group_rms(z, block):
    ms = _group_sum(z * z, block) * (1.0 / block)
    return z * lax.rsqrt(ms + EPS)


def _row_rms(z):
    return z * lax.rsqrt(jnp.mean(z * z, axis=-1, keepdims=True) + EPS)


def _log_sigmoid(x):
    return jnp.minimum(x, 0.0) - jnp.log(1.0 + jnp.exp(-jnp.abs(x)))


ADA_COLS = 1536


def _adaln_kernel(n_cond, cond_t_ref, w_ref, b_ref, o_ref):
    sc = jax.nn.silu(cond_t_ref[...])
    w = w_ref[0]
    rows = [jnp.sum(sc[:, r:r + 1] * w, axis=0, keepdims=True) + b_ref[0] for r in range(n_cond)]
    o_ref[0] = jnp.concatenate(rows + [jnp.zeros((SUBLANES - n_cond, w.shape[1]), F32)], axis=0)


def _adaln_call(cond_t, n_cond, ada_w, ada_b):
    n_col = 6 * D_MODEL // ADA_COLS
    return pl.pallas_call(
        functools.partial(_adaln_kernel, n_cond),
        grid=(DEPTH, n_col),
        in_specs=[
            pl.BlockSpec((D_MODEL, SUBLANES), lambda l, j: (0, 0)),
            pl.BlockSpec((1, D_MODEL, ADA_COLS), lambda l, j: (l, 0, j)),
            pl.BlockSpec((1, 1, ADA_COLS), lambda l, j: (l, 0, j)),
        ],
        out_specs=pl.BlockSpec((1, SUBLANES, ADA_COLS), lambda l, j: (l, 0, j)),
        out_shape=jax.ShapeDtypeStruct((DEPTH, SUBLANES, 6 * D_MODEL), F32),
        compiler_params=pltpu.CompilerParams(
            dimension_semantics=("arbitrary", "arbitrary"), vmem_limit_bytes=SMALL_KERNEL_VMEM_LIMIT),
        name="adaln",
    )(cond_t, ada_w, ada_b.reshape(DEPTH, 1, 6 * D_MODEL))


def _route(hn, rwt_ref, rb_ref):
    h_hi, h_lo = _split2(hn)
    rw = rwt_ref[...]
    rw_hi = rw.astype(BF16)
    rw_lo = (rw - rw_hi.astype(F32)).astype(BF16)
    logits = _dot_nt(rw_hi, h_hi) + _dot_nt(rw_hi, h_lo) + _dot_nt(rw_lo, h_hi)
    aff = jax.nn.sigmoid(logits)
    sel = aff + rb_ref[...]
    n_tok = sel.shape[1]

    def top2_sum(a, b, c, d):
        hi1, lo1 = jnp.maximum(a, b), jnp.minimum(a, b)
        hi2, lo2 = jnp.maximum(c, d), jnp.minimum(c, d)
        return jnp.maximum(hi1, hi2) + jnp.maximum(jnp.minimum(hi1, hi2), jnp.maximum(lo1, lo2))

    scores = []
    for g in range(N_GROUPS):
        rows = [sel[EXPERTS_PER_GROUP * g + j:EXPERTS_PER_GROUP * g + j + 1, :] for j in range(EXPERTS_PER_GROUP)]
        scores.append(top2_sum(*rows))
    best = jnp.zeros((1, n_tok), jnp.int32)
    best_score = scores[0]
    for g in range(1, N_GROUPS):
        upd = scores[g] > best_score
        best = jnp.where(upd, g, best)
        best_score = jnp.where(upd, scores[g], best_score)

    eid_i = _iota((N_EXPERTS, n_tok), 0)
    eid = eid_i.astype(F32)
    neg = jnp.float32(-jnp.inf)
    msel = jnp.where(eid_i // EXPERTS_PER_GROUP == best, sel, neg)
    m1 = jnp.max(msel, axis=0, keepdims=True)
    idx1 = jnp.min(jnp.where(msel == m1, eid, float(N_EXPERTS)), axis=0, keepdims=True)
    msel2 = jnp.where(eid == idx1, neg, msel)
    m2 = jnp.max(msel2, axis=0, keepdims=True)
    idx2 = jnp.min(jnp.where(msel2 == m2, eid, float(N_EXPERTS)), axis=0, keepdims=True)
    w1 = jnp.sum(jnp.where(eid == idx1, aff, 0.0), axis=0, keepdims=True)
    w2 = jnp.sum(jnp.where(eid == idx2, aff, 0.0), axis=0, keepdims=True)
    wsum = w1 + w2
    return idx1.astype(jnp.int32), idx2.astype(jnp.int32), w1 / wsum, w2 / wsum


def _local_slots(idx1, idx2):
    n_tok = idx1.shape[1]
    eid = _iota((N_EXPERTS, n_tok), 0)
    hot1, hot2 = eid == idx1, eid == idx2
    hot = jnp.where(hot1, 1.0, jnp.where(hot2, 1.0, 0.0))
    earlier = jnp.where(_iota((n_tok, n_tok), 0) < _iota((n_tok, n_tok), 1), 1.0, 0.0).astype(BF16)
    before_in_expert = _dot(hot.astype(BF16), earlier)
    counts = jnp.sum(hot, axis=1, keepdims=True)
    lower = jnp.where(_iota((N_EXPERTS, N_EXPERTS), 1) < _iota((N_EXPERTS, N_EXPERTS), 0), 1.0, 0.0).astype(BF16)
    first_slot = _dot(lower, jnp.broadcast_to(counts, (N_EXPERTS, LANES)).astype(BF16))[:, 0:1]
    slot = before_in_expert + first_slot
    slot1 = jnp.sum(jnp.where(hot1, slot, 0.0), axis=0, keepdims=True)
    slot2 = jnp.sum(jnp.where(hot2, slot, 0.0), axis=0, keepdims=True)
    return slot1, slot2, counts


def _slot_one_hot(slot1, slot2, v1, v2):
    n_tok = slot1.shape[1]
    row = _iota((2 * n_tok, n_tok), 0).astype(F32)
    return jnp.where(row == slot1, v1, jnp.where(row == slot2, v2, 0.0))


N_MIXER_WEIGHTS = 15
CTX_SEQS_PER_STEP = 2
MAX_INLINE_BLOCKS = 2


def _mixer_kernel(n_tok, n_par, latent, n_alias, lam_init, *refs):
    it = iter(refs)
    x_ref, mod_ref = next(it), next(it)
    (n1_ref, n2_ref, win_ref, wout_ref, sw_ref, sb_ref, qg_ref, kg_ref, dl_ref, dg_ref,
     w2c_ref, gb_ref, gg_ref, rwt_ref, rb_ref) = (next(it) for _ in range(N_MIXER_WEIGHTS))
    if latent:
        ck_ref, cv_ref, st0_ref, cos_ref, sin_ref = (next(it) for _ in range(5))
    for _ in range(n_alias):
        next(it)
    xo_ref, hs_ref, slot_ref, wt_ref, cnt_ref = (next(it) for _ in range(5))
    if not latent:
        ko_ref, vo_ref, so_ref = (next(it) for _ in range(3))
    proj_ref, mix_ref, q_ref, k_ref, v_ref = (next(it) for _ in range(5))
    gq_ref, gke_ref, gv_ref, gr_ref, dec_ref, go_ref, st_ref = (next(it) for _ in range(7))

    n_blk = n_tok // ROW_BLOCK
    n_ctx = k_ref.shape[0] - n_par * n_tok
    n_keys = n_ctx + n_tok
    mod = mod_ref[0, 0]

    by_block = latent

    def whole_sequence(fn):
        def run():
            fn()

        if by_block:
            pl.when(pl.program_id(1) == 0)(run)
        else:
            run()

    def blocks(body):
        if n_par * n_blk <= MAX_INLINE_BLOCKS:
            for r in range(n_par * n_blk):
                body(r)
        else:
            def step(r, carry):
                body(r)
                return carry
            whole_sequence(lambda: lax.fori_loop(0, n_par * n_blk, step, 0))

    def aligned(start, size):
        return pl.ds(start if isinstance(start, int) else pl.multiple_of(start, size), size)

    def block_rows(r, offset=0):
        return aligned(offset + r * ROW_BLOCK, ROW_BLOCK)

    if not latent:
        for ref in (ko_ref, vo_ref, so_ref):
            for q in range(n_par):
                for other in range(1, ref.shape[1]):
                    ref[q, other] = jnp.zeros(ref.shape[2:], F32)

    lane_group = _iota((SGU_CHUNK, SGU_W), 1) // SGU_GROUP_W
    blk_r = _iota((ROW_BLOCK, ROW_BLOCK), 0)
    blk_c = _iota((ROW_BLOCK, ROW_BLOCK), 1)
    same_chunk = (blk_r // GLA_CHUNK) == (blk_c // GLA_CHUNK)
    tri = (jnp.where(same_chunk & (blk_c <= blk_r), 1.0, 0.0).astype(BF16),
           jnp.where(same_chunk & (blk_c >= blk_r), 1.0, 0.0).astype(BF16))
    chunks_per_blk = ROW_BLOCK // GLA_CHUNK
    head_of_lane = _iota((GLA_CHUNK, W_GLA), 1) // GLA_DK
    stack_r = _iota((GLA_HEADS * GLA_CHUNK, GLA_CHUNK), 0) % GLA_CHUNK
    stack_c = _iota((GLA_HEADS * GLA_CHUNK, GLA_CHUNK), 1)
    causal = (stack_c <= stack_r, stack_c >= stack_r)

    if latent:
        def cached_context():
            for h in range(DIFF_HEADS):
                k_ref[0:n_ctx, h * DIFF_V:(h + 1) * DIFF_V] = ck_ref[0, 0, h].astype(BF16)
                v_ref[0:n_ctx, h * DIFF_V:(h + 1) * DIFF_V] = cv_ref[0, 0, h].astype(BF16)
            st_ref[0] = st0_ref[0, 0]

        whole_sequence(cached_context)
        pair_lo = (_iota((ROW_BLOCK, W_QK), 1) % (DIFF_QK // 2)) < (DIFF_QK // 4)

        def rope(z, rows):
            cos = jnp.concatenate([cos_ref[rows, :]] * DIFF_HEADS, axis=-1)
            sin = jnp.concatenate([sin_ref[rows, :]] * DIFF_HEADS, axis=-1)
            shift = DIFF_QK // 4
            swapped = jnp.where(pair_lo, pltpu.roll(z, W_QK - shift, 1), pltpu.roll(z, shift, 1))
            return z * cos + swapped * sin

    def modulated_input(r):
        h = _row_rms(x_ref[block_rows(r), :]) * n1_ref[0]
        return (h * (1.0 + mod[1:2, :]) + mod[0:1, :]).astype(BF16)

    def spatial_gating(r):
        for c in range(ROW_BLOCK // SGU_CHUNK):
            local = slice(c * SGU_CHUNK, (c + 1) * SGU_CHUNK)
            u = jax.nn.gelu(proj_ref[local, C_AU:C_AU + SGU_W])
            v = _group_rms(jax.nn.gelu(proj_ref[local, C_AV:C_AV + SGU_W]), SGU_GROUP_W).astype(BF16)
            s = sb_ref[0]
            for g in range(SGU_GROUPS):
                s = s + jnp.where(lane_group == g, _dot(sw_ref[0, g], v), 0.0)
            mix_ref[aligned(r * ROW_BLOCK + c * SGU_CHUNK, SGU_CHUNK), M_A:M_A + SGU_W] = (u * s).astype(BF16)

    def attention_operands(r):
        rows = block_rows(r)
        key_rows = block_rows(r, n_ctx)
        seq, seq_rows = r // n_blk, block_rows(r % n_blk)
        qn = _group_rms(proj_ref[:, C_BQ:C_BQ + W_QK], DIFF_QK) * qg_ref[0]
        kn = _group_rms(proj_ref[:, C_BK:C_BK + W_QK], DIFF_QK) * kg_ref[0]
        vv = proj_ref[:, C_BV:C_BV + W_QK]
        if latent:
            qn, kn = rope(qn, rows), rope(kn, rows)
        else:
            for h in range(DIFF_HEADS):
                for i in range(2):
                    lo = h * DIFF_V + i * DIFF_QK
                    ko_ref[seq, 0, h, i, seq_rows, :] = kn[:, lo:lo + DIFF_QK]
                vo_ref[seq, 0, h, seq_rows, :] = vv[:, h * DIFF_V:(h + 1) * DIFF_V]
        q_ref[rows, :] = (qn * (DIFF_QK ** -0.5)).astype(BF16)
        k_ref[key_rows, :] = kn.astype(BF16)
        v_ref[key_rows, :] = vv.astype(BF16)

    def gla_operands(r):
        rows = block_rows(r)
        gpre = _dot(proj_ref[:, C_LR:C_LR + LANES].astype(BF16), w2c_ref[0]) + gb_ref[0]
        gate = _log_sigmoid(gpre) * (1.0 / GLA_GATE_NORM)
        gq = proj_ref[:, C_CQ:C_CQ + W_GLA] * (GLA_DK ** -0.5)
        gk = proj_ref[:, C_CK:C_CK + W_GLA]
        gv = proj_ref[:, C_CV:C_CV + W_GLA].astype(BF16)
        gv_ref[rows, :] = gv
        gr_ref[rows, :] = proj_ref[:, C_CR:C_CR + W_GLA]
        for d in range(2):
            g = gate[:, d * W_GLA:(d + 1) * W_GLA]
            b = sum(_dot(tri[d], p) for p in _split3(g))
            last = GLA_CHUNK - 1 if d == 0 else 0
            b_last = jnp.concatenate(
                [jnp.broadcast_to(b[c * GLA_CHUNK + last:c * GLA_CHUNK + last + 1, :], (GLA_CHUNK, W_GLA))
                 for c in range(chunks_per_blk)], axis=0)
            q_dec = (gq * jnp.exp(b)).astype(BF16)
            k_inv = (gk * jnp.exp(-b)).astype(BF16)
            gq_ref[d, rows, :] = q_dec
            gke_ref[d, rows, :] = (gk * jnp.exp(b_last - b)).astype(BF16)
            for c in range(chunks_per_blk):
                row = c * GLA_CHUNK + last
                dec_ref[d, r * chunks_per_blk + c] = jnp.exp(b[row:row + 1, :])
                chunk = slice(c * GLA_CHUNK, (c + 1) * GLA_CHUNK)
                qd = q_dec[chunk]
                q_stack = jnp.concatenate(
                    [jnp.where(head_of_lane == h, qd, jnp.zeros_like(qd)) for h in range(GLA_HEADS)], axis=0)
                attn = jnp.where(causal[d], _dot_nt(q_stack, k_inv[chunk]), 0.0)
                spread = _dot(attn.astype(BF16), gv[chunk])
                o = jnp.zeros((GLA_CHUNK, W_GLA), F32)
                for h in range(GLA_HEADS):
                    o = o + jnp.where(head_of_lane == h, spread[h * GLA_CHUNK:(h + 1) * GLA_CHUNK, :], 0.0)
                go_ref[d, aligned(r * ROW_BLOCK + c * GLA_CHUNK, GLA_CHUNK), :] = o

    def project_and_split(r):
        proj_ref[...] = _dot(modulated_input(r), win_ref[0])
        spatial_gating(r)
        attention_operands(r)
        gla_operands(r)

    blocks(project_and_split)

    dl = dl_ref[0]
    lam = (jnp.exp(jnp.sum(dl[0:1] * dl[1:2], axis=-1, keepdims=True))
           - jnp.exp(jnp.sum(dl[2:3] * dl[3:4], axis=-1, keepdims=True)) + lam_init)
    sub0 = (_iota((ROW_BLOCK, DIFF_V), 1) < DIFF_QK)

    def softmax(s):
        e = jnp.exp(s - jnp.max(s, axis=-1, keepdims=True))
        return e, jnp.sum(e, axis=-1, keepdims=True)

    def attn_block(r):
        rows = block_rows(r)
        keys = aligned((r // n_blk) * n_keys, n_keys)
        for h in range(DIFF_HEADS):
            cols = slice(h * DIFF_V, (h + 1) * DIFF_V)
            qh = q_ref[rows, cols]
            kh = k_ref[keys, cols]
            e0, z0 = softmax(_dot_nt(jnp.where(sub0, qh, jnp.zeros_like(qh)), kh))
            e1, z1 = softmax(_dot_nt(jnp.where(sub0, jnp.zeros_like(qh), qh), kh))
            w = e0 / z0 - lam * (e1 / z1)
            o = _dot(w.astype(BF16), v_ref[keys, cols])
            o = _row_rms(o) * dg_ref[0] * (1.0 - lam_init)
            mix_ref[rows, M_B + h * DIFF_V:M_B + (h + 1) * DIFF_V] = o.astype(BF16)

    blocks(attn_block)

    if not latent:
        st_ref[...] = jnp.zeros(st_ref.shape, F32)

    n_chunk = n_tok // GLA_CHUNK
    st_diag = (_iota((W_GLA, W_GLA), 0) // GLA_DV) == (_iota((W_GLA, W_GLA), 1) // GLA_DK)

    def gla_step(c, carry):
        for seq in range(n_par):
            for d in range(2):
                cc = seq * n_chunk + (c if d == 0 else n_chunk - 1 - c)
                rows = pl.ds(pl.multiple_of(cc * GLA_CHUNK, GLA_CHUNK), GLA_CHUNK)
                st = st_ref[seq, d]
                go_ref[d, rows, :] = go_ref[d, rows, :] + _dot_nt(gq_ref[d, rows, :], st.astype(BF16))
                upd = _dot_tn(gv_ref[rows, :], gke_ref[d, rows, :])
                st_ref[seq, d] = dec_ref[d, cc] * st + jnp.where(st_diag, upd, 0.0)
        return carry

    whole_sequence(lambda: lax.fori_loop(0, n_chunk, gla_step, 0))

    if not latent:
        for seq in range(n_par):
            for d in range(2):
                s_full = st_ref[seq, d].T
                for h in range(GLA_HEADS):
                    so_ref[seq, 0, d, h] = s_full[h * GLA_DK:(h + 1) * GLA_DK, h * GLA_DV:(h + 1) * GLA_DV]

    def finish_block(r, out_r):
        rows, out_rows = block_rows(r), block_rows(out_r)
        oc = _group_rms(go_ref[0, rows, :] + go_ref[1, rows, :], GLA_DV) * gg_ref[0]
        oc = oc * jax.nn.silu(gr_ref[rows, :])
        mix_ref[rows, M_C:M_C + W_GLA] = oc.astype(BF16)
        x1 = x_ref[rows, :] + mod[2:3, :] * _dot(mix_ref[rows, :], wout_ref[0])
        xo_ref[out_rows, :] = x1
        hn = _row_rms(x1) * n2_ref[0]
        hn = hn * (1.0 + mod[4:5, :]) + mod[3:4, :]
        idx1, idx2, w1, w2 = _route(hn, rwt_ref, rb_ref)
        slot1, slot2, counts = _local_slots(idx1, idx2)
        perm = _slot_one_hot(slot1, slot2, 1.0, 1.0).astype(BF16)
        _to_row_slabs(hs_ref, 2 * out_r * ROW_BLOCK, _dot(perm, hn.astype(BF16)))
        slot_ref[:, out_rows] = jnp.concatenate([slot1, slot2], axis=0).astype(jnp.int32)
        wt_ref[:, out_rows] = jnp.concatenate([w1, w2], axis=0)
        cnt_ref[out_r] = jnp.broadcast_to(counts, (N_EXPERTS, LANES)).astype(jnp.int32)

    if by_block:
        finish_block(pl.program_id(1), 0)
    else:
        for r in range(n_par * n_blk):
            finish_block(r, r)


def _mixer_call(l, n_tok, n_par, latent, x, mods_all, weights, extras, cache_bufs):
    n_seq = x.shape[0] // n_tok
    n_all = x.shape[0]
    assert n_seq % n_par == 0 and not (latent and n_par > 1)
    n_step_tok = n_par * n_tok
    n_keys = n_step_tok + (extras[0].shape[3] if latent else 0)
    n_chunk = n_step_tok // GLA_CHUNK
    lam_init = 0.8 - 0.6 * math.exp(-0.3 * l)

    single = pl.Buffered(1)
    seq_mode = single if latent else None

    def layer(arr):
        tail = arr.shape[1:]
        return pl.BlockSpec((1,) + tail, lambda s, *_r, _n=len(tail): (l,) + (0,) * _n, pipeline_mode=single)

    def const(arr):
        return pl.BlockSpec(arr.shape, lambda s, *_r, _n=arr.ndim: (0,) * _n, pipeline_mode=single)

    def tok_spec(width):
        return pl.BlockSpec((n_step_tok, width), lambda s, *_r: (s, 0), pipeline_mode=seq_mode)

    mod_row = (lambda s: 1 + s) if latent else (lambda s: 0)
    in_specs = [tok_spec(D_MODEL),
                pl.BlockSpec((1, 1, 6, D_MODEL), lambda s, *_r: (l, mod_row(s), 0, 0))]
    in_specs += [layer(w) for w in weights[:N_MIXER_WEIGHTS - 2]] + [const(w) for w in weights[-2:]]
    operands = [x, mods_all] + list(weights)
    if latent:
        ck, cv, st0, cos, sin = extras
        in_specs += [
            pl.BlockSpec((1, 1) + ck.shape[2:], lambda s, *_r: (s, l, 0, 0, 0)),
            pl.BlockSpec((1, 1) + cv.shape[2:], lambda s, *_r: (s, l, 0, 0, 0)),
            pl.BlockSpec((1, 1) + st0.shape[2:], lambda s, *_r: (s, l, 0, 0, 0)),
            const(cos), const(sin),
        ]
        operands += [ck, cv, st0, cos, sin]
    n_in = len(operands)
    in_specs += [pl.BlockSpec(memory_space=pl.ANY)] * len(cache_bufs)
    operands += list(cache_bufs)

    tiles_per_step = n_step_tok // ROW_BLOCK
    out_shape = [
        jax.ShapeDtypeStruct((n_all, D_MODEL), F32),
        jax.ShapeDtypeStruct((2 * n_all * ROW_SLABS, LANES), F32),
        jax.ShapeDtypeStruct((2, n_all), jnp.int32),
        jax.ShapeDtypeStruct((2, n_all), F32),
        jax.ShapeDtypeStruct((n_all // ROW_BLOCK, N_EXPERTS, LANES), jnp.int32),
    ]
    if latent:
        grid = (n_seq, tiles_per_step)
        out_tok, out_tiles = ROW_BLOCK, 1
        at = lambda s, r: s * tiles_per_step + r
    else:
        grid = (n_seq // n_par,)
        out_tok, out_tiles = n_step_tok, tiles_per_step
        at = lambda s: s
    out_specs = [
        pl.BlockSpec((out_tok, D_MODEL), lambda *g: (at(*g), 0)),
        pl.BlockSpec((2 * out_tok * ROW_SLABS, LANES), lambda *g: (at(*g), 0)),
        pl.BlockSpec((2, out_tok), lambda *g: (0, at(*g))),
        pl.BlockSpec((2, out_tok), lambda *g: (0, at(*g))),
        pl.BlockSpec((out_tiles, N_EXPERTS, LANES), lambda *g: (at(*g), 0, 0)),
    ]
    n_shared_out = len(out_shape)
    aliases = {}
    if not latent:
        out_shape += [
            jax.ShapeDtypeStruct((n_seq, DEPTH, DIFF_HEADS, 2, n_tok, DIFF_QK), F32),
            jax.ShapeDtypeStruct((n_seq, DEPTH, DIFF_HEADS, n_tok, DIFF_V), F32),
            jax.ShapeDtypeStruct((n_seq, DEPTH, 2, GLA_HEADS, GLA_DK, GLA_DV), F32),
        ]
        n_lay, lay = (1, l) if cache_bufs else (DEPTH, 0)
        out_specs += [
            pl.BlockSpec((n_par, n_lay, DIFF_HEADS, 2, n_tok, DIFF_QK), lambda s: (s, lay, 0, 0, 0, 0)),
            pl.BlockSpec((n_par, n_lay, DIFF_HEADS, n_tok, DIFF_V), lambda s: (s, lay, 0, 0, 0)),
            pl.BlockSpec((n_par, n_lay, 2, GLA_HEADS, GLA_DK, GLA_DV), lambda s: (s, lay, 0, 0, 0, 0)),
        ]
        aliases = {n_in + j: n_shared_out + j for j in range(len(cache_bufs))}
    scratch = [
        pltpu.VMEM((ROW_BLOCK, D_PROJ_PAD), F32),
        pltpu.VMEM((n_step_tok, D_MODEL), BF16),
        pltpu.VMEM((n_step_tok, W_QK), BF16),
        pltpu.VMEM((n_keys, W_QK), BF16),
        pltpu.VMEM((n_keys, W_QK), BF16),
        pltpu.VMEM((2, n_step_tok, W_GLA), BF16),
        pltpu.VMEM((2, n_step_tok, W_GLA), BF16),
        pltpu.VMEM((n_step_tok, W_GLA), BF16),
        pltpu.VMEM((n_step_tok, W_GLA), F32),
        pltpu.VMEM((2, n_chunk, 1, W_GLA), F32),
        pltpu.VMEM((2, n_step_tok, W_GLA), F32),
        pltpu.VMEM((n_par, 2, W_GLA, W_GLA), F32),
    ]
    return pl.pallas_call(
        functools.partial(_mixer_kernel, n_tok, n_par, latent, len(cache_bufs), lam_init),
        grid=grid,
        in_specs=in_specs,
        out_specs=out_specs,
        out_shape=out_shape,
        scratch_shapes=scratch,
        input_output_aliases=aliases,
        compiler_params=pltpu.CompilerParams(
            dimension_semantics=("arbitrary",) * len(grid), vmem_limit_bytes=MIXER_VMEM_LIMIT),
        name="mixer_latent" if latent else "mixer_context",
    )(*operands)


PAIR_BLOCK = 2 * ROW_BLOCK
COPY_SIZES = tuple(ROW_BLOCK >> k for k in range(ROW_BLOCK.bit_length()))
LARGE_COPY = 64
GATHER_AHEAD = 2
GATHER_SLOTS = GATHER_AHEAD + 1


def _segment_copies(n_rows, make_copy, act):
    def copy_if_set(size):
        @pl.when((n_rows & size) != 0)
        def _():
            act(make_copy(n_rows & (-2 * size), size))

    n_large = COPY_SIZES.index(LARGE_COPY) + 1

    @pl.when(n_rows >= LARGE_COPY)
    def _():
        for size in COPY_SIZES[:n_large]:
            copy_if_set(size)

    for size in COPY_SIZES[n_large:]:
        copy_if_set(size)


def _start(copy):
    copy.start()


def _wait(copy):
    copy.wait()


def _slab_rows(first_row, n_rows, slab):
    return pl.ds(first_row * ROW_SLABS + slab, n_rows, stride=ROW_SLABS)


def _to_row_slabs(ref, first_row, value):
    for s in range(ROW_SLABS):
        ref[_slab_rows(first_row, value.shape[0], s), :] = value[:, s * LANES:(s + 1) * LANES]


def _from_row_slabs(ref, first_row, n_rows):
    return jnp.concatenate([ref[_slab_rows(first_row, n_rows, s), :] for s in range(ROW_SLABS)], axis=-1)


SLAB_PAIR_W = 2 * LANES
N_SLAB_PAIRS = ROW_SLABS // 2


def _slab_pair_cols(g):
    return slice(g * SLAB_PAIR_W, (g + 1) * SLAB_PAIR_W)


def _load_slab_pair(ref, first_row, n_rows, g):
    return jnp.concatenate([ref[_slab_rows(first_row, n_rows, s), :] for s in (2 * g, 2 * g + 1)], axis=-1)


def _row_span(ref, first_row, n_rows):
    return ref.at[pl.ds(pl.multiple_of(first_row * ROW_SLABS, ROW_SLABS), n_rows * ROW_SLABS)]


def _two_streams(n_first_tiles):
    def first(i, *_):
        return (jnp.minimum(i, n_first_tiles - 1), 0)

    def second(i, *_):
        return (jnp.maximum(i - n_first_tiles, 0), 0)

    return first, second


N_EXPERT_TABLES = 10
OUT_SLOTS = 2


def _expert_kernel(n_ctx_tiles, n_tiles_max, te_ref, first_ref, rows_ref, jlo_ref, jhi_ref, cpre_ref, cnt_ref,
                   lofs_ref, tile0_ref, ntile_ref, hs_c_ref, hs_l_ref, w1_ref, w3_ref, w2_ref, ys_ref,
                   xbuf_ref, obuf_ref, w1b_ref, w3b_ref, w2b_ref, sem, out_sem):
    expert = pl.program_id(0)
    n_tiles = tile0_ref[N_EXPERTS]

    def gather(t, act):
        slot = t % GATHER_SLOTS
        e, first = te_ref[t], first_ref[t]
        last = first + rows_ref[t]

        def segment_of(hs_ref, first_tile):
            def body(j, carry):
                k = j * N_EXPERTS + e
                seg_first = cpre_ref[k]
                lo = jnp.maximum(seg_first, first)
                n = jnp.maximum(jnp.minimum(seg_first + cnt_ref[k], last) - lo, 0)
                src = (j - first_tile) * PAIR_BLOCK + lofs_ref[k] + (lo - seg_first)
                dst = slot * ROW_BLOCK + lo - first
                _segment_copies(n, lambda done, size: pltpu.make_async_copy(
                    _row_span(hs_ref, src + done, size), _row_span(xbuf_ref, dst + done, size), sem.at[slot]), act)
                return carry
            return body

        jlo, jhi = jlo_ref[t], jhi_ref[t]
        lax.fori_loop(jnp.minimum(jlo, n_ctx_tiles), jnp.minimum(jhi, n_ctx_tiles), segment_of(hs_c_ref, 0), 0)
        lax.fori_loop(jnp.maximum(jlo, n_ctx_tiles), jnp.maximum(jhi, n_ctx_tiles),
                      segment_of(hs_l_ref, n_ctx_tiles), 0)

    def out_copy(t, oslot):
        return pltpu.make_async_copy(
            _row_span(obuf_ref, oslot * ROW_BLOCK, ROW_BLOCK), _row_span(ys_ref, t * ROW_BLOCK, ROW_BLOCK),
            out_sem.at[oslot])

    @pl.when(expert == 0)
    def _():
        xbuf_ref[...] = jnp.zeros(xbuf_ref.shape, F32)
        for t in range(GATHER_AHEAD):
            gather(t, _start)

    w1b_ref[...] = w1_ref[0, 0].astype(BF16)
    w3b_ref[...] = w3_ref[0, 0].astype(BF16)
    w2b_ref[...] = w2_ref[0, 0].astype(BF16)
    tile0, n_own = tile0_ref[expert], ntile_ref[expert]

    def tile_body(k, carry):
        t = tile0 + k
        slot, oslot = t % GATHER_SLOTS, k % OUT_SLOTS

        @pl.when(t + GATHER_AHEAD < n_tiles)
        def _():
            gather(t + GATHER_AHEAD, _start)

        n_rows = rows_ref[t]
        _segment_copies(n_rows, lambda done, size: pltpu.make_async_copy(
            _row_span(hs_c_ref, done, size), _row_span(xbuf_ref, slot * ROW_BLOCK + done, size), sem.at[slot]), _wait)

        @pl.when(k >= OUT_SLOTS)
        def _():
            out_copy(t, oslot).wait()

        live = _iota((ROW_BLOCK, D_MODEL), 0) < n_rows
        x = jnp.where(live, _from_row_slabs(xbuf_ref, slot * ROW_BLOCK, ROW_BLOCK), 0.0).astype(BF16)
        hid = jax.nn.silu(_dot(x, w1b_ref[...])) * _dot(x, w3b_ref[...])
        _to_row_slabs(obuf_ref, oslot * ROW_BLOCK, _dot(hid.astype(BF16), w2b_ref[...]))
        out_copy(t, oslot).start()
        return carry

    lax.fori_loop(0, n_own, tile_body, 0)
    for oslot in range(OUT_SLOTS):
        @pl.when(n_own > oslot)
        def _():
            out_copy(tile0, oslot).wait()

    @pl.when(expert == N_EXPERTS - 1)
    def _():
        obuf_ref[...] = jnp.zeros(obuf_ref.shape, F32)

        def fill(t, carry):
            out_copy(t, 0).start()
            out_copy(t, 0).wait()
            return carry

        lax.fori_loop(n_tiles, n_tiles_max, fill, 0)


def _expert_call(l, plan, hs_c, hs_l, w1, w3, w2):
    tables = plan["expert_tables"]
    n_tiles_max = tables[0].shape[0]
    n_ctx_tiles = hs_c.shape[0] // (PAIR_BLOCK * ROW_SLABS)

    def weight(shape):
        return pl.BlockSpec((1, 1) + shape, lambda e, *_: (l, e, 0, 0))

    return pl.pallas_call(
        functools.partial(_expert_kernel, n_ctx_tiles, n_tiles_max),
        grid_spec=pltpu.PrefetchScalarGridSpec(
            num_scalar_prefetch=N_EXPERT_TABLES,
            grid=(N_EXPERTS,),
            in_specs=[pl.BlockSpec(memory_space=pl.ANY), pl.BlockSpec(memory_space=pl.ANY),
                      weight((D_MODEL, D_EXPERT)), weight((D_MODEL, D_EXPERT)), weight((D_EXPERT, D_MODEL))],
            out_specs=pl.BlockSpec(memory_space=pl.ANY),
            scratch_shapes=[pltpu.VMEM((GATHER_SLOTS * ROW_BLOCK * ROW_SLABS, LANES), F32),
                            pltpu.VMEM((OUT_SLOTS * ROW_BLOCK * ROW_SLABS, LANES), F32),
                            pltpu.VMEM((D_MODEL, D_EXPERT), BF16), pltpu.VMEM((D_MODEL, D_EXPERT), BF16),
                            pltpu.VMEM((D_EXPERT, D_MODEL), BF16),
                            pltpu.SemaphoreType.DMA((GATHER_SLOTS,)), pltpu.SemaphoreType.DMA((OUT_SLOTS,))],
        ),
        out_shape=jax.ShapeDtypeStruct((n_tiles_max * ROW_BLOCK * ROW_SLABS, LANES), F32),
        compiler_params=pltpu.CompilerParams(
            dimension_semantics=("arbitrary",), vmem_limit_bytes=SMALL_KERNEL_VMEM_LIMIT),
        name="moe_experts",
    )(*tables, hs_c, hs_l, w1, w3, w2)


N_COMBINE_TABLES = 4
COMBINE_TILES_PER_STEP = 2


def _combine_kernel(n_ctx_steps, cnt_ref, cpre_ref, lofs_ref, starts_ref, x_c_ref, x_l_ref, slot_c_ref, slot_l_ref,
                    wt_c_ref, wt_l_ref, mod_ref, ys_ref, xo_c_ref, xo_l_ref, buf_ref, sem):
    step = pl.program_id(0)
    n_tiles = pl.num_programs(0) * COMBINE_TILES_PER_STEP

    def collect(t, act):
        slot = t % GATHER_SLOTS

        def body(e, carry):
            k = t * N_EXPERTS + e
            src, dst = starts_ref[e] + cpre_ref[k], slot * PAIR_BLOCK + lofs_ref[k]
            _segment_copies(cnt_ref[k], lambda done, size: pltpu.make_async_copy(
                _row_span(ys_ref, src + done, size), _row_span(buf_ref, dst + done, size), sem.at[slot]), act)
            return carry

        lax.fori_loop(0, N_EXPERTS, body, 0)

    @pl.when(step == 0)
    def _():
        for t in range(GATHER_AHEAD):
            collect(t, _start)

    gate = mod_ref[0, 0, 5:6, :]

    def finish(part, first_row, x_ref, slot_ref, wt_ref, xo_ref):
        tokens = pl.ds(part * ROW_BLOCK, ROW_BLOCK)
        slots, wts = slot_ref[:, tokens].astype(F32), wt_ref[:, tokens]
        slot1, slot2 = slots[0:1], slots[1:2]
        weight_of_row = jnp.sum(_slot_one_hot(slot1, slot2, wts[0:1], wts[1:2]), axis=1, keepdims=True)
        gather_rows = _slot_one_hot(slot1, slot2, 1.0, 1.0).T.astype(BF16)
        for g in range(N_SLAB_PAIRS):
            cols = _slab_pair_cols(g)
            hi, lo = _split2(_load_slab_pair(buf_ref, first_row, PAIR_BLOCK, g) * weight_of_row)
            y = _dot(gather_rows, hi) + _dot(gather_rows, lo)
            xo_ref[tokens, cols] = x_ref[tokens, cols] + gate[:, cols] * y

    for part in range(COMBINE_TILES_PER_STEP):
        j = step * COMBINE_TILES_PER_STEP + part

        @pl.when(j + GATHER_AHEAD < n_tiles)
        def _():
            collect(j + GATHER_AHEAD, _start)

        slot = j % GATHER_SLOTS
        for piece in range(PAIR_BLOCK // ROW_BLOCK):
            pltpu.make_async_copy(
                _row_span(ys_ref, piece * ROW_BLOCK, ROW_BLOCK),
                _row_span(buf_ref, slot * PAIR_BLOCK + piece * ROW_BLOCK, ROW_BLOCK), sem.at[slot]).wait()
        @pl.when(step < n_ctx_steps)
        def _():
            finish(part, slot * PAIR_BLOCK, x_c_ref, slot_c_ref, wt_c_ref, xo_c_ref)

        @pl.when(step >= n_ctx_steps)
        def _():
            finish(part, slot * PAIR_BLOCK, x_l_ref, slot_l_ref, wt_l_ref, xo_l_ref)


def _combine_call(l, plan, x_c, x_l, slot_c, slot_l, wt_c, wt_l, mods_all, mod_row_of_tile, ys):
    step_rows = COMBINE_TILES_PER_STEP * ROW_BLOCK
    n_steps = (x_c.shape[0] + x_l.shape[0]) // step_rows
    n_ctx_steps = x_c.shape[0] // step_rows
    first, second = _two_streams(n_ctx_steps)

    def lanes(index_map):
        return lambda i, *_: index_map(i)[::-1]

    return pl.pallas_call(
        functools.partial(_combine_kernel, n_ctx_steps),
        grid_spec=pltpu.PrefetchScalarGridSpec(
            num_scalar_prefetch=N_COMBINE_TABLES,
            grid=(n_steps,),
            in_specs=[pl.BlockSpec((step_rows, D_MODEL), first),
                      pl.BlockSpec((step_rows, D_MODEL), second),
                      pl.BlockSpec((2, step_rows), lanes(first)),
                      pl.BlockSpec((2, step_rows), lanes(second)),
                      pl.BlockSpec((2, step_rows), lanes(first)),
                      pl.BlockSpec((2, step_rows), lanes(second)),
                      pl.BlockSpec((1, 1, 6, D_MODEL),
                                   lambda i, *_: (l, mod_row_of_tile(i * COMBINE_TILES_PER_STEP), 0, 0)),
                      pl.BlockSpec(memory_space=pl.ANY)],
            out_specs=[pl.BlockSpec((step_rows, D_MODEL), first),
                       pl.BlockSpec((step_rows, D_MODEL), second)],
            scratch_shapes=[pltpu.VMEM((GATHER_SLOTS * PAIR_BLOCK * ROW_SLABS, LANES), F32),
                            pltpu.SemaphoreType.DMA((GATHER_SLOTS,))],
        ),
        out_shape=[jax.ShapeDtypeStruct(x_c.shape, F32), jax.ShapeDtypeStruct(x_l.shape, F32)],
        compiler_params=pltpu.CompilerParams(
            dimension_semantics=("arbitrary",), vmem_limit_bytes=SMALL_KERNEL_VMEM_LIMIT),
        name="moe_combine",
    )(*plan["combine_tables"], x_c, x_l, slot_c, slot_l, wt_c, wt_l, mods_all, ys)


def _moe_plan(cnt):
    n_tok_tiles = cnt.shape[0]
    n_tiles = n_tok_tiles * PAIR_BLOCK // ROW_BLOCK + N_EXPERTS
    lofs = jnp.cumsum(cnt, axis=1) - cnt
    cpre = jnp.cumsum(cnt, axis=0) - cnt
    counts = jnp.sum(cnt, axis=0)
    padded = (counts + ROW_BLOCK - 1) // ROW_BLOCK * ROW_BLOCK
    ends = jnp.cumsum(padded)
    starts = ends - padded
    tile_start = jnp.arange(n_tiles, dtype=jnp.int32) * ROW_BLOCK
    tile_expert = jnp.minimum(
        jnp.sum((tile_start[:, None] >= ends[None, :]).astype(jnp.int32), axis=1), N_EXPERTS - 1)
    hot = tile_expert[:, None] == jnp.arange(N_EXPERTS, dtype=jnp.int32)[None, :]
    first = tile_start - jnp.sum(jnp.where(hot, starts[None, :], 0), axis=1)
    rows = jnp.clip(jnp.sum(jnp.where(hot, counts[None, :], 0), axis=1) - first, 0, ROW_BLOCK)
    seg_first = jnp.sum(jnp.where(hot[:, None, :], cpre[None, :, :], 0), axis=2)
    seg_rows = jnp.sum(jnp.where(hot[:, None, :], cnt[None, :, :], 0), axis=2)
    overlap = (seg_first < (first + rows)[:, None]) & (seg_first + seg_rows > first[:, None])
    j = jnp.arange(n_tok_tiles, dtype=jnp.int32)[None, :]
    jlo = jnp.min(jnp.where(overlap, j, n_tok_tiles), axis=1)
    jhi = jnp.max(jnp.where(overlap, j + 1, 0), axis=1)
    i32 = lambda a: a.astype(jnp.int32).reshape(-1)
    tile0 = jnp.concatenate([starts, ends[-1:]]) // ROW_BLOCK
    return {
        "expert_tables": tuple(i32(a) for a in (tile_expert, first, rows, jlo, jhi, cpre, cnt, lofs,
                                                tile0, padded // ROW_BLOCK)),
        "combine_tables": tuple(i32(a) for a in (cnt, cpre, lofs, starts)),
    }


def _rope_tables(n_tok):
    n_rows = n_tok // GRID_W
    pos_r = jnp.repeat(jnp.arange(n_rows), GRID_W)
    pos_c = jnp.tile(jnp.arange(GRID_W), n_rows)
    half = DIFF_QK // 2
    nf = half // 2
    freqs = ROPE_BASE ** (-jnp.arange(nf, dtype=F32) / nf)

    def tables(pos):
        ang = pos.astype(F32)[:, None] * freqs
        cos, sin = jnp.cos(ang), jnp.sin(ang)
        return jnp.concatenate([cos, cos], axis=-1), jnp.concatenate([-sin, sin], axis=-1)

    cos_r, sin_r = tables(pos_r)
    cos_c, sin_c = tables(pos_c)
    cos = jnp.concatenate([cos_r, cos_c], axis=-1)
    sin = jnp.concatenate([sin_r, sin_c], axis=-1)
    return jnp.concatenate([cos, cos], axis=-1), jnp.concatenate([sin, sin], axis=-1)


def _mixer_weights(w_in, w_out, sgu_w, sgu_b, q_norm_g, k_norm_g, diff_lambda, diff_norm_g, gla_w2, gla_b,
                   gla_norm_g, norm1_g, norm2_g, router_w, router_bias):
    w_in_pad = jnp.concatenate(
        [w_in.astype(BF16), jnp.zeros(w_in.shape[:2] + (D_PROJ_PAD - w_in.shape[2],), BF16)], axis=-1)
    w2cat = jnp.zeros((DEPTH, LANES, 2 * W_GLA), F32)
    w2cat = w2cat.at[:, 0:GLA_RANK, 0:W_GLA].set(gla_w2[:, 0]).at[:, GLA_RANK:2 * GLA_RANK, W_GLA:].set(gla_w2[:, 1])
    return (
        norm1_g[:, None, :], norm2_g[:, None, :], w_in_pad, w_out.astype(BF16),
        sgu_w.astype(BF16), jnp.repeat(sgu_b.transpose(0, 2, 1), SGU_GROUP_W, axis=2),
        jnp.tile(q_norm_g, (1, W_QK // DIFF_QK))[:, None, :], jnp.tile(k_norm_g, (1, W_QK // DIFF_QK))[:, None, :],
        diff_lambda, diff_norm_g[:, None, :],
        w2cat.astype(BF16), gla_b.reshape(DEPTH, 1, 2 * W_GLA), jnp.tile(gla_norm_g, (1, GLA_HEADS))[:, None, :],
        router_w.T, router_bias[:, None],
    )


def kernel(x_prompt, x_sample, cache_k, cache_v, state_gla, c, c_ctx, w_in, w_out, sgu_w, sgu_b, q_norm_g, k_norm_g,
           diff_lambda, diff_norm_g, gla_w2, gla_b, gla_norm_g, norm1_g, norm2_g, ada_w, ada_b, router_w, router_bias,
           moe_w1, moe_w3, moe_w2):
    n_ctx_seq, ctx_len, _ = x_prompt.shape
    n_lat_seq, lat_len, _ = x_sample.shape
    n_ctx_tok = n_ctx_seq * ctx_len
    n_lat_tok = n_lat_seq * lat_len
    ctx_tiles = n_ctx_tok // ROW_BLOCK
    lat_tiles_per_seq = lat_len // ROW_BLOCK

    n_cond = 1 + n_lat_seq
    cond_t = jnp.zeros((D_MODEL, SUBLANES), F32).at[:, 0].set(c_ctx).at[:, 1:n_cond].set(c.T)
    mods_all = _adaln_call(cond_t, n_cond, ada_w, ada_b)[:, :n_cond].reshape(DEPTH, n_cond, 6, D_MODEL)
    weights = _mixer_weights(w_in, w_out, sgu_w, sgu_b, q_norm_g, k_norm_g, diff_lambda, diff_norm_g, gla_w2, gla_b,
                             gla_norm_g, norm1_g, norm2_g, router_w, router_bias)

    ck_all = cache_k.transpose(0, 1, 2, 4, 3, 5).reshape(cache_k.shape[:3] + (cache_k.shape[4], DIFF_V))
    st_all = jnp.einsum('bldhkv,hg->bldhvgk', state_gla, jnp.eye(GLA_HEADS, dtype=F32)).reshape(
        n_lat_seq, DEPTH, 2, W_GLA, W_GLA)
    cos, sin = _rope_tables(lat_len)
    extras = (ck_all, cache_v, st_all, cos, sin)

    def mod_row_of_tile(i):
        return jnp.where(i < ctx_tiles, 0, 1 + (i - ctx_tiles) // lat_tiles_per_seq)

    x_c = x_prompt.reshape(n_ctx_tok, D_MODEL)
    x_l = x_sample.reshape(n_lat_tok, D_MODEL)
    cache_bufs = ()
    for l in range(DEPTH):
        ctx_par = 1 if l == 0 else CTX_SEQS_PER_STEP
        x1_c, hs_c, slot_c, wt_c, cnt_c, *cache_bufs = _mixer_call(
            l, ctx_len, ctx_par, False, x_c, mods_all, weights, None, tuple(cache_bufs))
        x1_l, hs_l, slot_l, wt_l, cnt_l = _mixer_call(l, lat_len, 1, True, x_l, mods_all, weights, extras, ())
        plan = _moe_plan(jnp.concatenate([cnt_c[:, :, 0], cnt_l[:, :, 0]], axis=0))
        ys = _expert_call(l, plan, hs_c, hs_l, moe_w1, moe_w3, moe_w2)
        x_c, x_l = _combine_call(l, plan, x1_c, x1_l, slot_c, slot_l, wt_c, wt_l, mods_all, mod_row_of_tile, ys)

    new_k, new_v, new_s = cache_bufs
    return (x_c.reshape(x_prompt.shape), x_l.reshape(x_sample.shape), new_k, new_v, new_s)
```

```python
import functools
import math

import jax
import jax.numpy as jnp
from jax import lax
from jax.experimental import pallas as pl
from jax.experimental.pallas import tpu as pltpu

F32 = jnp.float32
BF16 = jnp.bfloat16

D_MODEL = 1024
DEPTH = 4
GRID_W = 64
SGU_GROUPS = 4
SGU_GROUP_W = 64
SGU_W = SGU_GROUPS * SGU_GROUP_W
SGU_CHUNK = 128
DIFF_HEADS = 4
DIFF_QK = 64
DIFF_V = 2 * DIFF_QK
ROPE_BASE = 10000.0
GLA_HEADS = 4
GLA_DK = 64
GLA_DV = 64
GLA_RANK = 16
GLA_GATE_NORM = 16.0
GLA_CHUNK = 64
N_EXPERTS = 16
N_GROUPS = 4
EXPERTS_PER_GROUP = N_EXPERTS // N_GROUPS
D_EXPERT = 512
EPS = 1e-6

LANES = 128
SUBLANES = 8
MXU_DIM = 256
V7X_VMEM_BYTES = 64 * 1024 * 1024
MIXER_VMEM_LIMIT = V7X_VMEM_BYTES * 7 // 8
SMALL_KERNEL_VMEM_LIMIT = V7X_VMEM_BYTES * 5 // 8

ROW_BLOCK = MXU_DIM
ROW_SLABS = D_MODEL // LANES

W_QK = DIFF_HEADS * 2 * DIFF_QK
W_GLA = GLA_HEADS * GLA_DK
C_AU, C_AV = 0, SGU_W
C_BQ = C_AV + SGU_W
C_BK, C_BV = C_BQ + W_QK, C_BQ + 2 * W_QK
C_CQ = C_BV + DIFF_HEADS * DIFF_V
C_CK, C_CV, C_CR, C_LR = C_CQ + W_GLA, C_CQ + 2 * W_GLA, C_CQ + 3 * W_GLA, C_CQ + 4 * W_GLA
D_PROJ_MAIN = C_LR
D_PROJ_PAD = D_PROJ_MAIN + LANES
M_A, M_B, M_C = 0, SGU_W, SGU_W + DIFF_HEADS * DIFF_V


def _split2(x):
    hi = x.astype(BF16)
    lo = (x - hi.astype(F32)).astype(BF16)
    return hi, lo


def _split3(x):
    hi = x.astype(BF16)
    r = x - hi.astype(F32)
    mid = r.astype(BF16)
    lo = (r - mid.astype(F32)).astype(BF16)
    return hi, mid, lo


def _dot(a, b):
    return jnp.dot(a, b, preferred_element_type=F32)


def _dot_nt(a, b):
    return lax.dot_general(a, b, (((1,), (1,)), ((), ())), preferred_element_type=F32)


def _dot_tn(a, b):
    return lax.dot_general(a, b, (((0,), (0,)), ((), ())), preferred_element_type=F32)


def _iota(shape, dim):
    return lax.broadcasted_iota(jnp.int32, shape, dim)


def _block_ones(width, block):
    r = _iota((width, width), 0) // block
    c = _iota((width, width), 1) // block
    return (r == c)


def _group_sum(z, block):
    width = z.shape[-1]
    outs = []
    for s in range(0, width, MXU_DIM):
        w = min(MXU_DIM, width - s)
        ones = _block_ones(w, block).astype(BF16)
        hi, lo = _split2(z[:, s:s + w])
        outs.append(_dot(hi, ones) + _dot(lo, ones))
    return outs[0] if len(outs) == 1 else jnp.concatenate(outs, axis=-1)


def _group_rms(z, block):
    ms = _group_sum(z * z, block) * (1.0 / block)
    return z * lax.rsqrt(ms + EPS)


def _row_rms(z):
    return z * lax.rsqrt(jnp.mean(z * z, axis=-1, keepdims=True) + EPS)


def _log_sigmoid(x):
    return jnp.minimum(x, 0.0) - jnp.log(1.0 + jnp.exp(-jnp.abs(x)))


ADA_COLS = 1536


def _adaln_kernel(n_cond, cond_t_ref, w_ref, b_ref, o_ref):
    sc = jax.nn.silu(cond_t_ref[...])
    w = w_ref[0]
    rows = [jnp.sum(sc[:, r:r + 1] * w, axis=0, keepdims=True) + b_ref[0] for r in range(n_cond)]
    o_ref[0] = jnp.concatenate(rows + [jnp.zeros((SUBLANES - n_cond, w.shape[1]), F32)], axis=0)


def _adaln_call(cond_t, n_cond, ada_w, ada_b):
    n_col = 6 * D_MODEL // ADA_COLS
    return pl.pallas_call(
        functools.partial(_adaln_kernel, n_cond),
        grid=(DEPTH, n_col),
        in_specs=[
            pl.BlockSpec((D_MODEL, SUBLANES), lambda l, j: (0, 0)),
            pl.BlockSpec((1, D_MODEL, ADA_COLS), lambda l, j: (l, 0, j)),
            pl.BlockSpec((1, 1, ADA_COLS), lambda l, j: (l, 0, j)),
        ],
        out_specs=pl.BlockSpec((1, SUBLANES, ADA_COLS), lambda l, j: (l, 0, j)),
        out_shape=jax.ShapeDtypeStruct((DEPTH, SUBLANES, 6 * D_MODEL), F32),
        compiler_params=pltpu.CompilerParams(
            dimension_semantics=("arbitrary", "arbitrary"), vmem_limit_bytes=SMALL_KERNEL_VMEM_LIMIT),
        name="adaln",
    )(cond_t, ada_w, ada_b.reshape(DEPTH, 1, 6 * D_MODEL))


def _route(hn, rwt_ref, rb_ref):
    h_hi, h_lo = _split2(hn)
    rw = rwt_ref[...]
    rw_hi = rw.astype(BF16)
    rw_lo = (rw - rw_hi.astype(F32)).astype(BF16)
    logits = _dot_nt(rw_hi, h_hi) + _dot_nt(rw_hi, h_lo) + _dot_nt(rw_lo, h_hi)
    aff = jax.nn.sigmoid(logits)
    sel = aff + rb_ref[...]
    n_tok = sel.shape[1]

    def top2_sum(a, b, c, d):
        hi1, lo1 = jnp.maximum(a, b), jnp.minimum(a, b)
        hi2, lo2 = jnp.maximum(c, d), jnp.minimum(c, d)
        return jnp.maximum(hi1, hi2) + jnp.maximum(jnp.minimum(hi1, hi2), jnp.maximum(lo1, lo2))

    scores = []
    for g in range(N_GROUPS):
        rows = [sel[EXPERTS_PER_GROUP * g + j:EXPERTS_PER_GROUP * g + j + 1, :] for j in range(EXPERTS_PER_GROUP)]
        scores.append(top2_sum(*rows))
    best = jnp.zeros((1, n_tok), jnp.int32)
    best_score = scores[0]
    for g in range(1, N_GROUPS):
        upd = scores[g] > best_score
        best = jnp.where(upd, g, best)
        best_score = jnp.where(upd, scores[g], best_score)

    eid_i = _iota((N_EXPERTS, n_tok), 0)
    eid = eid_i.astype(F32)
    neg = jnp.float32(-jnp.inf)
    msel = jnp.where(eid_i // EXPERTS_PER_GROUP == best, sel, neg)
    m1 = jnp.max(msel, axis=0, keepdims=True)
    idx1 = jnp.min(jnp.where(msel == m1, eid, float(N_EXPERTS)), axis=0, keepdims=True)
    msel2 = jnp.where(eid == idx1, neg, msel)
    m2 = jnp.max(msel2, axis=0, keepdims=True)
    idx2 = jnp.min(jnp.where(msel2 == m2, eid, float(N_EXPERTS)), axis=0, keepdims=True)
    w1 = jnp.sum(jnp.where(eid == idx1, aff, 0.0), axis=0, keepdims=True)
    w2 = jnp.sum(jnp.where(eid == idx2, aff, 0.0), axis=0, keepdims=True)
    wsum = w1 + w2
    return idx1.astype(jnp.int32), idx2.astype(jnp.int32), w1 / wsum, w2 / wsum


def _local_slots(idx1, idx2):
    n_tok = idx1.shape[1]
    eid = _iota((N_EXPERTS, n_tok), 0)
    hot1, hot2 = eid == idx1, eid == idx2
    hot = jnp.where(hot1, 1.0, jnp.where(hot2, 1.0, 0.0))
    earlier = jnp.where(_iota((n_tok, n_tok), 0) < _iota((n_tok, n_tok), 1), 1.0, 0.0).astype(BF16)
    before_in_expert = _dot(hot.astype(BF16), earlier)
    counts = jnp.sum(hot, axis=1, keepdims=True)
    lower = jnp.where(_iota((N_EXPERTS, N_EXPERTS), 1) < _iota((N_EXPERTS, N_EXPERTS), 0), 1.0, 0.0).astype(BF16)
    first_slot = _dot(lower, jnp.broadcast_to(counts, (N_EXPERTS, LANES)).astype(BF16))[:, 0:1]
    slot = before_in_expert + first_slot
    slot1 = jnp.sum(jnp.where(hot1, slot, 0.0), axis=0, keepdims=True)
    slot2 = jnp.sum(jnp.where(hot2, slot, 0.0), axis=0, keepdims=True)
    return slot1, slot2, counts


def _slot_one_hot(slot1, slot2, v1, v2):
    n_tok = slot1.shape[1]
    row = _iota((2 * n_tok, n_tok), 0).astype(F32)
    return jnp.where(row == slot1, v1, jnp.where(row == slot2, v2, 0.0))


N_MIXER_WEIGHTS = 15
CTX_SEQS_PER_STEP = 2
MAX_INLINE_BLOCKS = 2


def _mixer_kernel(n_tok, n_par, latent, n_alias, lam_init, *refs):
    it = iter(refs)
    x_ref, mod_ref = next(it), next(it)
    (n1_ref, n2_ref, win_ref, wout_ref, sw_ref, sb_ref, qg_ref, kg_ref, dl_ref, dg_ref,
     w2c_ref, gb_ref, gg_ref, rwt_ref, rb_ref) = (next(it) for _ in range(N_MIXER_WEIGHTS))
    if latent:
        ck_ref, cv_ref, st0_ref, cos_ref, sin_ref = (next(it) for _ in range(5))
    for _ in range(n_alias):
        next(it)
    xo_ref, hs_ref, slot_ref, wt_ref, cnt_ref = (next(it) for _ in range(5))
    if not latent:
        ko_ref, vo_ref, so_ref = (next(it) for _ in range(3))
    proj_ref, mix_ref, q_ref, k_ref, v_ref = (next(it) for _ in range(5))
    gq_ref, gke_ref, gv_ref, gr_ref, dec_ref, go_ref, st_ref = (next(it) for _ in range(7))

    n_blk = n_tok // ROW_BLOCK
    n_ctx = k_ref.shape[0] - n_par * n_tok
    n_keys = n_ctx + n_tok
    mod = mod_ref[0, 0]

    by_block = latent

    def whole_sequence(fn):
        def run():
            fn()

        if by_block:
            pl.when(pl.program_id(1) == 0)(run)
        else:
            run()

    def blocks(body):
        if n_par * n_blk <= MAX_INLINE_BLOCKS:
            for r in range(n_par * n_blk):
                body(r)
        else:
            def step(r, carry):
                body(r)
                return carry
            whole_sequence(lambda: lax.fori_loop(0, n_par * n_blk, step, 0))

    def aligned(start, size):
        return pl.ds(start if isinstance(start, int) else pl.multiple_of(start, size), size)

    def block_rows(r, offset=0):
        return aligned(offset + r * ROW_BLOCK, ROW_BLOCK)

    if not latent:
        for ref in (ko_ref, vo_ref, so_ref):
            for q in range(n_par):
                for other in range(1, ref.shape[1]):
                    ref[q, other] = jnp.zeros(ref.shape[2:], F32)

    lane_group = _iota((SGU_CHUNK, SGU_W), 1) // SGU_GROUP_W
    blk_r = _iota((ROW_BLOCK, ROW_BLOCK), 0)
    blk_c = _iota((ROW_BLOCK, ROW_BLOCK), 1)
    same_chunk = (blk_r // GLA_CHUNK) == (blk_c // GLA_CHUNK)
    tri = (jnp.where(same_chunk & (blk_c <= blk_r), 1.0, 0.0).astype(BF16),
           jnp.where(same_chunk & (blk_c >= blk_r), 1.0, 0.0).astype(BF16))
    chunks_per_blk = ROW_BLOCK // GLA_CHUNK
    head_of_lane = _iota((GLA_CHUNK, W_GLA), 1) // GLA_DK
    stack_r = _iota((GLA_HEADS * GLA_CHUNK, GLA_CHUNK), 0) % GLA_CHUNK
    stack_c = _iota((GLA_HEADS * GLA_CHUNK, GLA_CHUNK), 1)
    causal = (stack_c <= stack_r, stack_c >= stack_r)

    if latent:
        def cached_context():
            for h in range(DIFF_HEADS):
                k_ref[0:n_ctx, h * DIFF_V:(h + 1) * DIFF_V] = ck_ref[0, 0, h].astype(BF16)
                v_ref[0:n_ctx, h * DIFF_V:(h + 1) * DIFF_V] = cv_ref[0, 0, h].astype(BF16)
            st_ref[0] = st0_ref[0, 0]

        whole_sequence(cached_context)
        pair_lo = (_iota((ROW_BLOCK, W_QK), 1) % (DIFF_QK // 2)) < (DIFF_QK // 4)

        def rope(z, rows):
            cos = jnp.concatenate([cos_ref[rows, :]] * DIFF_HEADS, axis=-1)
            sin = jnp.concatenate([sin_ref[rows, :]] * DIFF_HEADS, axis=-1)
            shift = DIFF_QK // 4
            swapped = jnp.where(pair_lo, pltpu.roll(z, W_QK - shift, 1), pltpu.roll(z, shift, 1))
            return z * cos + swapped * sin

    def modulated_input(r):
        h = _row_rms(x_ref[block_rows(r), :]) * n1_ref[0]
        return (h * (1.0 + mod[1:2, :]) + mod[0:1, :]).astype(BF16)

    def spatial_gating(r):
        for c in range(ROW_BLOCK // SGU_CHUNK):
            local = slice(c * SGU_CHUNK, (c + 1) * SGU_CHUNK)
            u = jax.nn.gelu(proj_ref[local, C_AU:C_AU + SGU_W])
            v = _group_rms(jax.nn.gelu(proj_ref[local, C_AV:C_AV + SGU_W]), SGU_GROUP_W).astype(BF16)
            s = sb_ref[0]
            for g in range(SGU_GROUPS):
                s = s + jnp.where(lane_group == g, _dot(sw_ref[0, g], v), 0.0)
            mix_ref[aligned(r * ROW_BLOCK + c * SGU_CHUNK, SGU_CHUNK), M_A:M_A + SGU_W] = (u * s).astype(BF16)

    def attention_operands(r):
        rows = block_rows(r)
        key_rows = block_rows(r, n_ctx)
        seq, seq_rows = r // n_blk, block_rows(r % n_blk)
        qn = _group_rms(proj_ref[:, C_BQ:C_BQ + W_QK], DIFF_QK) * qg_ref[0]
        kn = _group_rms(proj_ref[:, C_BK:C_BK + W_QK], DIFF_QK) * kg_ref[0]
        vv = proj_ref[:, C_BV:C_BV + W_QK]
        if latent:
            qn, kn = rope(qn, rows), rope(kn, rows)
        else:
            for h in range(DIFF_HEADS):
                for i in range(2):
                    lo = h * DIFF_V + i * DIFF_QK
                    ko_ref[seq, 0, h, i, seq_rows, :] = kn[:, lo:lo + DIFF_QK]
                vo_ref[seq, 0, h, seq_rows, :] = vv[:, h * DIFF_V:(h + 1) * DIFF_V]
        q_ref[rows, :] = (qn * (DIFF_QK ** -0.5)).astype(BF16)
        k_ref[key_rows, :] = kn.astype(BF16)
        v_ref[key_rows, :] = vv.astype(BF16)

    def gla_operands(r):
        rows = block_rows(r)
        gpre = _dot(proj_ref[:, C_LR:C_LR + LANES].astype(BF16), w2c_ref[0]) + gb_ref[0]
        gate = _log_sigmoid(gpre) * (1.0 / GLA_GATE_NORM)
        gq = proj_ref[:, C_CQ:C_CQ + W_GLA] * (GLA_DK ** -0.5)
        gk = proj_ref[:, C_CK:C_CK + W_GLA]
        gv = proj_ref[:, C_CV:C_CV + W_GLA].astype(BF16)
        gv_ref[rows, :] = gv
        gr_ref[rows, :] = proj_ref[:, C_CR:C_CR + W_GLA]
        for d in range(2):
            g = gate[:, d * W_GLA:(d + 1) * W_GLA]
            b = sum(_dot(tri[d], p) for p in _split3(g))
            last = GLA_CHUNK - 1 if d == 0 else 0
            b_last = jnp.concatenate(
                [jnp.broadcast_to(b[c * GLA_CHUNK + last:c * GLA_CHUNK + last + 1, :], (GLA_CHUNK, W_GLA))
                 for c in range(chunks_per_blk)], axis=0)
            q_dec = (gq * jnp.exp(b)).astype(BF16)
            k_inv = (gk * jnp.exp(-b)).astype(BF16)
            gq_ref[d, rows, :] = q_dec
            gke_ref[d, rows, :] = (gk * jnp.exp(b_last - b)).astype(BF16)
            for c in range(chunks_per_blk):
                row = c * GLA_CHUNK + last
                dec_ref[d, r * chunks_per_blk + c] = jnp.exp(b[row:row + 1, :])
                chunk = slice(c * GLA_CHUNK, (c + 1) * GLA_CHUNK)
                qd = q_dec[chunk]
                q_stack = jnp.concatenate(
                    [jnp.where(head_of_lane == h, qd, jnp.zeros_like(qd)) for h in range(GLA_HEADS)], axis=0)
                attn = jnp.where(causal[d], _dot_nt(q_stack, k_inv[chunk]), 0.0)
                spread = _dot(attn.astype(BF16), gv[chunk])
                o = jnp.zeros((GLA_CHUNK, W_GLA), F32)
                for h in range(GLA_HEADS):
                    o = o + jnp.where(head_of_lane == h, spread[h * GLA_CHUNK:(h + 1) * GLA_CHUNK, :], 0.0)
                go_ref[d, aligned(r * ROW_BLOCK + c * GLA_CHUNK, GLA_CHUNK), :] = o

    def project_and_split(r):
        proj_ref[...] = _dot(modulated_input(r), win_ref[0])
        spatial_gating(r)
        attention_operands(r)
        gla_operands(r)

    blocks(project_and_split)

    dl = dl_ref[0]
    lam = (jnp.exp(jnp.sum(dl[0:1] * dl[1:2], axis=-1, keepdims=True))
           - jnp.exp(jnp.sum(dl[2:3] * dl[3:4], axis=-1, keepdims=True)) + lam_init)
    sub0 = (_iota((ROW_BLOCK, DIFF_V), 1) < DIFF_QK)

    def softmax(s):
        e = jnp.exp(s - jnp.max(s, axis=-1, keepdims=True))
        return e, jnp.sum(e, axis=-1, keepdims=True)

    def attn_block(r):
        rows = block_rows(r)
        keys = aligned((r // n_blk) * n_keys, n_keys)
        for h in range(DIFF_HEADS):
            cols = slice(h * DIFF_V, (h + 1) * DIFF_V)
            qh = q_ref[rows, cols]
            kh = k_ref[keys, cols]
            e0, z0 = softmax(_dot_nt(jnp.where(sub0, qh, jnp.zeros_like(qh)), kh))
            e1, z1 = softmax(_dot_nt(jnp.where(sub0, jnp.zeros_like(qh), qh), kh))
            w = e0 / z0 - lam * (e1 / z1)
            o = _dot(w.astype(BF16), v_ref[keys, cols])
            o = _row_rms(o) * dg_ref[0] * (1.0 - lam_init)
            mix_ref[rows, M_B + h * DIFF_V:M_B + (h + 1) * DIFF_V] = o.astype(BF16)

    blocks(attn_block)

    if not latent:
        st_ref[...] = jnp.zeros(st_ref.shape, F32)

    n_chunk = n_tok // GLA_CHUNK
    st_diag = (_iota((W_GLA, W_GLA), 0) // GLA_DV) == (_iota((W_GLA, W_GLA), 1) // GLA_DK)

    def gla_step(c, carry):
        for seq in range(n_par):
            for d in range(2):
                cc = seq * n_chunk + (c if d == 0 else n_chunk - 1 - c)
                rows = pl.ds(pl.multiple_of(cc * GLA_CHUNK, GLA_CHUNK), GLA_CHUNK)
                st = st_ref[seq, d]
                go_ref[d, rows, :] = go_ref[d, rows, :] + _dot_nt(gq_ref[d, rows, :], st.astype(BF16))
                upd = _dot_tn(gv_ref[rows, :], gke_ref[d, rows, :])
                st_ref[seq, d] = dec_ref[d, cc] * st + jnp.where(st_diag, upd, 0.0)
        return carry

    whole_sequence(lambda: lax.fori_loop(0, n_chunk, gla_step, 0))

    if not latent:
        for seq in range(n_par):
            for d in range(2):
                s_full = st_ref[seq, d].T
                for h in range(GLA_HEADS):
                    so_ref[seq, 0, d, h] = s_full[h * GLA_DK:(h + 1) * GLA_DK, h * GLA_DV:(h + 1) * GLA_DV]

    def finish_block(r, out_r):
        rows, out_rows = block_rows(r), block_rows(out_r)
        oc = _group_rms(go_ref[0, rows, :] + go_ref[1, rows, :], GLA_DV) * gg_ref[0]
        oc = oc * jax.nn.silu(gr_ref[rows, :])
        mix_ref[rows, M_C:M_C + W_GLA] = oc.astype(BF16)
        x1 = x_ref[rows, :] + mod[2:3, :] * _dot(mix_ref[rows, :], wout_ref[0])
        xo_ref[out_rows, :] = x1
        hn = _row_rms(x1) * n2_ref[0]
        hn = hn * (1.0 + mod[4:5, :]) + mod[3:4, :]
        idx1, idx2, w1, w2 = _route(hn, rwt_ref, rb_ref)
        slot1, slot2, counts = _local_slots(idx1, idx2)
        perm = _slot_one_hot(slot1, slot2, 1.0, 1.0).astype(BF16)
        _to_row_slabs(hs_ref, 2 * out_r * ROW_BLOCK, _dot(perm, hn.astype(BF16)))
        slot_ref[:, out_rows] = jnp.concatenate([slot1, slot2], axis=0).astype(jnp.int32)
        wt_ref[:, out_rows] = jnp.concatenate([w1, w2], axis=0)
        cnt_ref[out_r] = jnp.broadcast_to(counts, (N_EXPERTS, LANES)).astype(jnp.int32)

    if by_block:
        finish_block(pl.program_id(1), 0)
    else:
        for r in range(n_par * n_blk):
            finish_block(r, r)


def _mixer_call(l, n_tok, n_par, latent, x, mods_all, weights, extras, cache_bufs):
    n_seq = x.shape[0] // n_tok
    n_all = x.shape[0]
    assert n_seq % n_par == 0 and not (latent and n_par > 1)
    n_step_tok = n_par * n_tok
    n_keys = n_step_tok + (extras[0].shape[3] if latent else 0)
    n_chunk = n_step_tok // GLA_CHUNK
    lam_init = 0.8 - 0.6 * math.exp(-0.3 * l)

    single = pl.Buffered(1)
    seq_mode = single if latent else None

    def layer(arr):
        tail = arr.shape[1:]
        return pl.BlockSpec((1,) + tail, lambda s, *_r, _n=len(tail): (l,) + (0,) * _n, pipeline_mode=single)

    def const(arr):
        return pl.BlockSpec(arr.shape, lambda s, *_r, _n=arr.ndim: (0,) * _n, pipeline_mode=single)

    def tok_spec(width):
        return pl.BlockSpec((n_step_tok, width), lambda s, *_r: (s, 0), pipeline_mode=seq_mode)

    mod_row = (lambda s: 1 + s) if latent else (lambda s: 0)
    in_specs = [tok_spec(D_MODEL),
                pl.BlockSpec((1, 1, 6, D_MODEL), lambda s, *_r: (l, mod_row(s), 0, 0))]
    in_specs += [layer(w) for w in weights[:N_MIXER_WEIGHTS - 2]] + [const(w) for w in weights[-2:]]
    operands = [x, mods_all] + list(weights)
    if latent:
        ck, cv, st0, cos, sin = extras
        in_specs += [
            pl.BlockSpec((1, 1) + ck.shape[2:], lambda s, *_r: (s, l, 0, 0, 0)),
            pl.BlockSpec((1, 1) + cv.shape[2:], lambda s, *_r: (s, l, 0, 0, 0)),
            pl.BlockSpec((1, 1) + st0.shape[2:], lambda s, *_r: (s, l, 0, 0, 0)),
            const(cos), const(sin),
        ]
        operands += [ck, cv, st0, cos, sin]
    n_in = len(operands)
    in_specs += [pl.BlockSpec(memory_space=pl.ANY)] * len(cache_bufs)
    operands += list(cache_bufs)

    tiles_per_step = n_step_tok // ROW_BLOCK
    out_shape = [
        jax.ShapeDtypeStruct((n_all, D_MODEL), F32),
        jax.ShapeDtypeStruct((2 * n_all * ROW_SLABS, LANES), F32),
        jax.ShapeDtypeStruct((2, n_all), jnp.int32),
        jax.ShapeDtypeStruct((2, n_all), F32),
        jax.ShapeDtypeStruct((n_all // ROW_BLOCK, N_EXPERTS, LANES), jnp.int32),
    ]
    if latent:
        grid = (n_seq, tiles_per_step)
        out_tok, out_tiles = ROW_BLOCK, 1
        at = lambda s, r: s * tiles_per_step + r
    else:
        grid = (n_seq // n_par,)
        out_tok, out_tiles = n_step_tok, tiles_per_step
        at = lambda s: s
    out_specs = [
        pl.BlockSpec((out_tok, D_MODEL), lambda *g: (at(*g), 0)),
        pl.BlockSpec((2 * out_tok * ROW_SLABS, LANES), lambda *g: (at(*g), 0)),
        pl.BlockSpec((2, out_tok), lambda *g: (0, at(*g))),
        pl.BlockSpec((2, out_tok), lambda *g: (0, at(*g))),
        pl.BlockSpec((out_tiles, N_EXPERTS, LANES), lambda *g: (at(*g), 0, 0)),
    ]
    n_shared_out = len(out_shape)
    aliases = {}
    if not latent:
        out_shape += [
            jax.ShapeDtypeStruct((n_seq, DEPTH, DIFF_HEADS, 2, n_tok, DIFF_QK), F32),
            jax.ShapeDtypeStruct((n_seq, DEPTH, DIFF_HEADS, n_tok, DIFF_V), F32),
            jax.ShapeDtypeStruct((n_seq, DEPTH, 2, GLA_HEADS, GLA_DK, GLA_DV), F32),
        ]
        n_lay, lay = (1, l) if cache_bufs else (DEPTH, 0)
        out_specs += [
            pl.BlockSpec((n_par, n_lay, DIFF_HEADS, 2, n_tok, DIFF_QK), lambda s: (s, lay, 0, 0, 0, 0)),
            pl.BlockSpec((n_par, n_lay, DIFF_HEADS, n_tok, DIFF_V), lambda s: (s, lay, 0, 0, 0)),
            pl.BlockSpec((n_par, n_lay, 2, GLA_HEADS, GLA_DK, GLA_DV), lambda s: (s, lay, 0, 0, 0, 0)),
        ]
        aliases = {n_in + j: n_shared_out + j for j in range(len(cache_bufs))}
    scratch = [
        pltpu.VMEM((ROW_BLOCK, D_PROJ_PAD), F32),
        pltpu.VMEM((n_step_tok, D_MODEL), BF16),
        pltpu.VMEM((n_step_tok, W_QK), BF16),
        pltpu.VMEM((n_keys, W_QK), BF16),
        pltpu.VMEM((n_keys, W_QK), BF16),
        pltpu.VMEM((2, n_step_tok, W_GLA), BF16),
        pltpu.VMEM((2, n_step_tok, W_GLA), BF16),
        pltpu.VMEM((n_step_tok, W_GLA), BF16),
        pltpu.VMEM((n_step_tok, W_GLA), F32),
        pltpu.VMEM((2, n_chunk, 1, W_GLA), F32),
        pltpu.VMEM((2, n_step_tok, W_GLA), F32),
        pltpu.VMEM((n_par, 2, W_GLA, W_GLA), F32),
    ]
    return pl.pallas_call(
        functools.partial(_mixer_kernel, n_tok, n_par, latent, len(cache_bufs), lam_init),
        grid=grid,
        in_specs=in_specs,
        out_specs=out_specs,
        out_shape=out_shape,
        scratch_shapes=scratch,
        input_output_aliases=aliases,
        compiler_params=pltpu.CompilerParams(
            dimension_semantics=("arbitrary",) * len(grid), vmem_limit_bytes=MIXER_VMEM_LIMIT),
        name="mixer_latent" if latent else "mixer_context",
    )(*operands)


PAIR_BLOCK = 2 * ROW_BLOCK
COPY_SIZES = tuple(ROW_BLOCK >> k for k in range(ROW_BLOCK.bit_length()))
LARGE_COPY = 64
GATHER_AHEAD = 2
GATHER_SLOTS = GATHER_AHEAD + 1


def _segment_copies(n_rows, make_copy, act):
    def copy_if_set(size):
        @pl.when((n_rows & size) != 0)
        def _():
            act(make_copy(n_rows & (-2 * size), size), COPY_SIZES.index(size) % 2)

    n_large = COPY_SIZES.index(LARGE_COPY) + 1

    @pl.when(n_rows >= LARGE_COPY)
    def _():
        for size in COPY_SIZES[:n_large]:
            copy_if_set(size)

    for size in COPY_SIZES[n_large:]:
        copy_if_set(size)


def _start(copy, priority):
    copy.start(priority=priority)


def _wait(copy, priority):
    del priority
    copy.wait()


def _slab_rows(first_row, n_rows, slab):
    return pl.ds(first_row * ROW_SLABS + slab, n_rows, stride=ROW_SLABS)


def _to_row_slabs(ref, first_row, value):
    for s in range(ROW_SLABS):
        ref[_slab_rows(first_row, value.shape[0], s), :] = value[:, s * LANES:(s + 1) * LANES]


def _from_row_slabs(ref, first_row, n_rows):
    return jnp.concatenate([ref[_slab_rows(first_row, n_rows, s), :] for s in range(ROW_SLABS)], axis=-1)


SLAB_PAIR_W = 2 * LANES
N_SLAB_PAIRS = ROW_SLABS // 2


def _slab_pair_cols(g):
    return slice(g * SLAB_PAIR_W, (g + 1) * SLAB_PAIR_W)


def _load_slab_pair(ref, first_row, n_rows, g):
    return jnp.concatenate([ref[_slab_rows(first_row, n_rows, s), :] for s in (2 * g, 2 * g + 1)], axis=-1)


def _row_span(ref, first_row, n_rows):
    return ref.at[pl.ds(pl.multiple_of(first_row * ROW_SLABS, ROW_SLABS), n_rows * ROW_SLABS)]


def _two_streams(n_first_tiles):
    def first(i, *_):
        return (jnp.minimum(i, n_first_tiles - 1), 0)

    def second(i, *_):
        return (jnp.maximum(i - n_first_tiles, 0), 0)

    return first, second


N_EXPERT_TABLES = 10
OUT_SLOTS = 2


def _expert_kernel(n_ctx_tiles, n_tiles_max, te_ref, first_ref, rows_ref, jlo_ref, jhi_ref, cpre_ref, cnt_ref,
                   lofs_ref, tile0_ref, ntile_ref, hs_c_ref, hs_l_ref, w1_ref, w3_ref, w2_ref, ys_ref,
                   xbuf_ref, obuf_ref, w1b_ref, w3b_ref, w2b_ref, sem, out_sem):
    expert = pl.program_id(0)
    n_tiles = tile0_ref[N_EXPERTS]

    def gather(t, act):
        slot = t % GATHER_SLOTS
        e, first = te_ref[t], first_ref[t]
        last = first + rows_ref[t]

        def segment_of(hs_ref, first_tile):
            def body(j, carry):
                k = j * N_EXPERTS + e
                seg_first = cpre_ref[k]
                lo = jnp.maximum(seg_first, first)
                n = jnp.maximum(jnp.minimum(seg_first + cnt_ref[k], last) - lo, 0)
                src = (j - first_tile) * PAIR_BLOCK + lofs_ref[k] + (lo - seg_first)
                dst = slot * ROW_BLOCK + lo - first
                _segment_copies(n, lambda done, size: pltpu.make_async_copy(
                    _row_span(hs_ref, src + done, size), _row_span(xbuf_ref, dst + done, size), sem.at[slot]), act)
                return carry
            return body

        jlo, jhi = jlo_ref[t], jhi_ref[t]
        lax.fori_loop(jnp.minimum(jlo, n_ctx_tiles), jnp.minimum(jhi, n_ctx_tiles), segment_of(hs_c_ref, 0), 0)
        lax.fori_loop(jnp.maximum(jlo, n_ctx_tiles), jnp.maximum(jhi, n_ctx_tiles),
                      segment_of(hs_l_ref, n_ctx_tiles), 0)

    def out_copy(t, oslot):
        return pltpu.make_async_copy(
            _row_span(obuf_ref, oslot * ROW_BLOCK, ROW_BLOCK), _row_span(ys_ref, t * ROW_BLOCK, ROW_BLOCK),
            out_sem.at[oslot])

    @pl.when(expert == 0)
    def _():
        xbuf_ref[...] = jnp.zeros(xbuf_ref.shape, F32)
        for t in range(GATHER_AHEAD):
            gather(t, _start)

    w1b_ref[...] = w1_ref[0, 0].astype(BF16)
    w3b_ref[...] = w3_ref[0, 0].astype(BF16)
    w2b_ref[...] = w2_ref[0, 0].astype(BF16)
    tile0, n_own = tile0_ref[expert], ntile_ref[expert]

    def tile_body(k, carry):
        t = tile0 + k
        slot, oslot = t % GATHER_SLOTS, k % OUT_SLOTS

        @pl.when(t + GATHER_AHEAD < n_tiles)
        def _():
            gather(t + GATHER_AHEAD, _start)

        n_rows = rows_ref[t]
        _segment_copies(n_rows, lambda done, size: pltpu.make_async_copy(
            _row_span(hs_c_ref, done, size), _row_span(xbuf_ref, slot * ROW_BLOCK + done, size), sem.at[slot]), _wait)

        @pl.when(k >= OUT_SLOTS)
        def _():
            out_copy(t, oslot).wait()

        live = _iota((ROW_BLOCK, D_MODEL), 0) < n_rows
        x = jnp.where(live, _from_row_slabs(xbuf_ref, slot * ROW_BLOCK, ROW_BLOCK), 0.0).astype(BF16)
        hid = jax.nn.silu(_dot(x, w1b_ref[...])) * _dot(x, w3b_ref[...])
        _to_row_slabs(obuf_ref, oslot * ROW_BLOCK, _dot(hid.astype(BF16), w2b_ref[...]))
        out_copy(t, oslot).start()
        return carry

    lax.fori_loop(0, n_own, tile_body, 0)
    for oslot in range(OUT_SLOTS):
        @pl.when(n_own > oslot)
        def _():
            out_copy(tile0, oslot).wait()

    @pl.when(expert == N_EXPERTS - 1)
    def _():
        obuf_ref[...] = jnp.zeros(obuf_ref.shape, F32)

        def fill(t, carry):
            out_copy(t, 0).start()
            out_copy(t, 0).wait()
            return carry

        lax.fori_loop(n_tiles, n_tiles_max, fill, 0)


def _expert_call(l, plan, hs_c, hs_l, w1, w3, w2):
    tables = plan["expert_tables"]
    n_tiles_max = tables[0].shape[0]
    n_ctx_tiles = hs_c.shape[0] // (PAIR_BLOCK * ROW_SLABS)

    def weight(shape):
        return pl.BlockSpec((1, 1) + shape, lambda e, *_: (l, e, 0, 0))

    return pl.pallas_call(
        functools.partial(_expert_kernel, n_ctx_tiles, n_tiles_max),
        grid_spec=pltpu.PrefetchScalarGridSpec(
            num_scalar_prefetch=N_EXPERT_TABLES,
            grid=(N_EXPERTS,),
            in_specs=[pl.BlockSpec(memory_space=pl.ANY), pl.BlockSpec(memory_space=pl.ANY),
                      weight((D_MODEL, D_EXPERT)), weight((D_MODEL, D_EXPERT)), weight((D_EXPERT, D_MODEL))],
            out_specs=pl.BlockSpec(memory_space=pl.ANY),
            scratch_shapes=[pltpu.VMEM((GATHER_SLOTS * ROW_BLOCK * ROW_SLABS, LANES), F32),
                            pltpu.VMEM((OUT_SLOTS * ROW_BLOCK * ROW_SLABS, LANES), F32),
                            pltpu.VMEM((D_MODEL, D_EXPERT), BF16), pltpu.VMEM((D_MODEL, D_EXPERT), BF16),
                            pltpu.VMEM((D_EXPERT, D_MODEL), BF16),
                            pltpu.SemaphoreType.DMA((GATHER_SLOTS,)), pltpu.SemaphoreType.DMA((OUT_SLOTS,))],
        ),
        out_shape=jax.ShapeDtypeStruct((n_tiles_max * ROW_BLOCK * ROW_SLABS, LANES), F32),
        compiler_params=pltpu.CompilerParams(
            dimension_semantics=("arbitrary",), vmem_limit_bytes=SMALL_KERNEL_VMEM_LIMIT),
        name="moe_experts",
    )(*tables, hs_c, hs_l, w1, w3, w2)


N_COMBINE_TABLES = 4
COMBINE_TILES_PER_STEP = 2


def _combine_kernel(n_ctx_steps, cnt_ref, cpre_ref, lofs_ref, starts_ref, x_c_ref, x_l_ref, slot_c_ref, slot_l_ref,
                    wt_c_ref, wt_l_ref, mod_ref, ys_ref, xo_c_ref, xo_l_ref, buf_ref, sem):
    step = pl.program_id(0)
    n_tiles = pl.num_programs(0) * COMBINE_TILES_PER_STEP

    def collect(t, act):
        slot = t % GATHER_SLOTS

        def body(e, carry):
            k = t * N_EXPERTS + e
            src, dst = starts_ref[e] + cpre_ref[k], slot * PAIR_BLOCK + lofs_ref[k]
            _segment_copies(cnt_ref[k], lambda done, size: pltpu.make_async_copy(
                _row_span(ys_ref, src + done, size), _row_span(buf_ref, dst + done, size), sem.at[slot]), act)
            return carry

        lax.fori_loop(0, N_EXPERTS, body, 0)

    @pl.when(step == 0)
    def _():
        for t in range(GATHER_AHEAD):
            collect(t, _start)

    gate = mod_ref[0, 0, 5:6, :]

    def finish(part, first_row, x_ref, slot_ref, wt_ref, xo_ref):
        tokens = pl.ds(part * ROW_BLOCK, ROW_BLOCK)
        slots, wts = slot_ref[:, tokens].astype(F32), wt_ref[:, tokens]
        slot1, slot2 = slots[0:1], slots[1:2]
        weight_of_row = jnp.sum(_slot_one_hot(slot1, slot2, wts[0:1], wts[1:2]), axis=1, keepdims=True)
        gather_rows = _slot_one_hot(slot1, slot2, 1.0, 1.0).T.astype(BF16)
        for g in range(N_SLAB_PAIRS):
            cols = _slab_pair_cols(g)
            hi, lo = _split2(_load_slab_pair(buf_ref, first_row, PAIR_BLOCK, g) * weight_of_row)
            y = _dot(gather_rows, hi) + _dot(gather_rows, lo)
            xo_ref[tokens, cols] = x_ref[tokens, cols] + gate[:, cols] * y

    for part in range(COMBINE_TILES_PER_STEP):
        j = step * COMBINE_TILES_PER_STEP + part

        @pl.when(j + GATHER_AHEAD < n_tiles)
        def _():
            collect(j + GATHER_AHEAD, _start)

        slot = j % GATHER_SLOTS
        for piece in range(PAIR_BLOCK // ROW_BLOCK):
            pltpu.make_async_copy(
                _row_span(ys_ref, piece * ROW_BLOCK, ROW_BLOCK),
                _row_span(buf_ref, slot * PAIR_BLOCK + piece * ROW_BLOCK, ROW_BLOCK), sem.at[slot]).wait()
        @pl.when(step < n_ctx_steps)
        def _():
            finish(part, slot * PAIR_BLOCK, x_c_ref, slot_c_ref, wt_c_ref, xo_c_ref)

        @pl.when(step >= n_ctx_steps)
        def _():
            finish(part, slot * PAIR_BLOCK, x_l_ref, slot_l_ref, wt_l_ref, xo_l_ref)


def _combine_call(l, plan, x_c, x_l, slot_c, slot_l, wt_c, wt_l, mods_all, mod_row_of_tile, ys):
    step_rows = COMBINE_TILES_PER_STEP * ROW_BLOCK
    n_steps = (x_c.shape[0] + x_l.shape[0]) // step_rows
    n_ctx_steps = x_c.shape[0] // step_rows
    first, second = _two_streams(n_ctx_steps)

    def lanes(index_map):
        return lambda i, *_: index_map(i)[::-1]

    return pl.pallas_call(
        functools.partial(_combine_kernel, n_ctx_steps),
        grid_spec=pltpu.PrefetchScalarGridSpec(
            num_scalar_prefetch=N_COMBINE_TABLES,
            grid=(n_steps,),
            in_specs=[pl.BlockSpec((step_rows, D_MODEL), first),
                      pl.BlockSpec((step_rows, D_MODEL), second),
                      pl.BlockSpec((2, step_rows), lanes(first)),
                      pl.BlockSpec((2, step_rows), lanes(second)),
                      pl.BlockSpec((2, step_rows), lanes(first)),
                      pl.BlockSpec((2, step_rows), lanes(second)),
                      pl.BlockSpec((1, 1, 6, D_MODEL),
                                   lambda i, *_: (l, mod_row_of_tile(i * COMBINE_TILES_PER_STEP), 0, 0)),
                      pl.BlockSpec(memory_space=pl.ANY)],
            out_specs=[pl.BlockSpec((step_rows, D_MODEL), first),
                       pl.BlockSpec((step_rows, D_MODEL), second)],
            scratch_shapes=[pltpu.VMEM((GATHER_SLOTS * PAIR_BLOCK * ROW_SLABS, LANES), F32),
                            pltpu.SemaphoreType.DMA((GATHER_SLOTS,))],
        ),
        out_shape=[jax.ShapeDtypeStruct(x_c.shape, F32), jax.ShapeDtypeStruct(x_l.shape, F32)],
        compiler_params=pltpu.CompilerParams(
            dimension_semantics=("arbitrary",), vmem_limit_bytes=SMALL_KERNEL_VMEM_LIMIT),
        name="moe_combine",
    )(*plan["combine_tables"], x_c, x_l, slot_c, slot_l, wt_c, wt_l, mods_all, ys)


def _moe_plan(cnt):
    n_tok_tiles = cnt.shape[0]
    n_tiles = n_tok_tiles * PAIR_BLOCK // ROW_BLOCK + N_EXPERTS
    lofs = jnp.cumsum(cnt, axis=1) - cnt
    cpre = jnp.cumsum(cnt, axis=0) - cnt
    counts = jnp.sum(cnt, axis=0)
    padded = (counts + ROW_BLOCK - 1) // ROW_BLOCK * ROW_BLOCK
    ends = jnp.cumsum(padded)
    starts = ends - padded
    tile_start = jnp.arange(n_tiles, dtype=jnp.int32) * ROW_BLOCK
    tile_expert = jnp.minimum(
        jnp.sum((tile_start[:, None] >= ends[None, :]).astype(jnp.int32), axis=1), N_EXPERTS - 1)
    hot = tile_expert[:, None] == jnp.arange(N_EXPERTS, dtype=jnp.int32)[None, :]
    first = tile_start - jnp.sum(jnp.where(hot, starts[None, :], 0), axis=1)
    rows = jnp.clip(jnp.sum(jnp.where(hot, counts[None, :], 0), axis=1) - first, 0, ROW_BLOCK)
    seg_first = jnp.sum(jnp.where(hot[:, None, :], cpre[None, :, :], 0), axis=2)
    seg_rows = jnp.sum(jnp.where(hot[:, None, :], cnt[None, :, :], 0), axis=2)
    overlap = (seg_first < (first + rows)[:, None]) & (seg_first + seg_rows > first[:, None])
    j = jnp.arange(n_tok_tiles, dtype=jnp.int32)[None, :]
    jlo = jnp.min(jnp.where(overlap, j, n_tok_tiles), axis=1)
    jhi = jnp.max(jnp.where(overlap, j + 1, 0), axis=1)
    i32 = lambda a: a.astype(jnp.int32).reshape(-1)
    tile0 = jnp.concatenate([starts, ends[-1:]]) // ROW_BLOCK
    return {
        "expert_tables": tuple(i32(a) for a in (tile_expert, first, rows, jlo, jhi, cpre, cnt, lofs,
                                                tile0, padded // ROW_BLOCK)),
        "combine_tables": tuple(i32(a) for a in (cnt, cpre, lofs, starts)),
    }


def _rope_tables(n_tok):
    n_rows = n_tok // GRID_W
    pos_r = jnp.repeat(jnp.arange(n_rows), GRID_W)
    pos_c = jnp.tile(jnp.arange(GRID_W), n_rows)
    half = DIFF_QK // 2
    nf = half // 2
    freqs = ROPE_BASE ** (-jnp.arange(nf, dtype=F32) / nf)

    def tables(pos):
        ang = pos.astype(F32)[:, None] * freqs
        cos, sin = jnp.cos(ang), jnp.sin(ang)
        return jnp.concatenate([cos, cos], axis=-1), jnp.concatenate([-sin, sin], axis=-1)

    cos_r, sin_r = tables(pos_r)
    cos_c, sin_c = tables(pos_c)
    cos = jnp.concatenate([cos_r, cos_c], axis=-1)
    sin = jnp.concatenate([sin_r, sin_c], axis=-1)
    return jnp.concatenate([cos, cos], axis=-1), jnp.concatenate([sin, sin], axis=-1)


def _mixer_weights(w_in, w_out, sgu_w, sgu_b, q_norm_g, k_norm_g, diff_lambda, diff_norm_g, gla_w2, gla_b,
                   gla_norm_g, norm1_g, norm2_g, router_w, router_bias):
    w_in_pad = jnp.concatenate(
        [w_in.astype(BF16), jnp.zeros(w_in.shape[:2] + (D_PROJ_PAD - w_in.shape[2],), BF16)], axis=-1)
    w2cat = jnp.zeros((DEPTH, LANES, 2 * W_GLA), F32)
    w2cat = w2cat.at[:, 0:GLA_RANK, 0:W_GLA].set(gla_w2[:, 0]).at[:, GLA_RANK:2 * GLA_RANK, W_GLA:].set(gla_w2[:, 1])
    return (
        norm1_g[:, None, :], norm2_g[:, None, :], w_in_pad, w_out.astype(BF16),
        sgu_w.astype(BF16), jnp.repeat(sgu_b.transpose(0, 2, 1), SGU_GROUP_W, axis=2),
        jnp.tile(q_norm_g, (1, W_QK // DIFF_QK))[:, None, :], jnp.tile(k_norm_g, (1, W_QK // DIFF_QK))[:, None, :],
        diff_lambda, diff_norm_g[:, None, :],
        w2cat.astype(BF16), gla_b.reshape(DEPTH, 1, 2 * W_GLA), jnp.tile(gla_norm_g, (1, GLA_HEADS))[:, None, :],
        router_w.T, router_bias[:, None],
    )


def kernel(x_prompt, x_sample, cache_k, cache_v, state_gla, c, c_ctx, w_in, w_out, sgu_w, sgu_b, q_norm_g, k_norm_g,
           diff_lambda, diff_norm_g, gla_w2, gla_b, gla_norm_g, norm1_g, norm2_g, ada_w, ada_b, router_w, router_bias,
           moe_w1, moe_w3, moe_w2):
    n_ctx_seq, ctx_len, _ = x_prompt.shape
    n_lat_seq, lat_len, _ = x_sample.shape
    n_ctx_tok = n_ctx_seq * ctx_len
    n_lat_tok = n_lat_seq * lat_len
    ctx_tiles = n_ctx_tok // ROW_BLOCK
    lat_tiles_per_seq = lat_len // ROW_BLOCK

    n_cond = 1 + n_lat_seq
    cond_t = jnp.zeros((D_MODEL, SUBLANES), F32).at[:, 0].set(c_ctx).at[:, 1:n_cond].set(c.T)
    mods_all = _adaln_call(cond_t, n_cond, ada_w, ada_b)[:, :n_cond].reshape(DEPTH, n_cond, 6, D_MODEL)
    weights = _mixer_weights(w_in, w_out, sgu_w, sgu_b, q_norm_g, k_norm_g, diff_lambda, diff_norm_g, gla_w2, gla_b,
                             gla_norm_g, norm1_g, norm2_g, router_w, router_bias)

    ck_all = cache_k.transpose(0, 1, 2, 4, 3, 5).reshape(cache_k.shape[:3] + (cache_k.shape[4], DIFF_V))
    st_all = jnp.einsum('bldhkv,hg->bldhvgk', state_gla, jnp.eye(GLA_HEADS, dtype=F32)).reshape(
        n_lat_seq, DEPTH, 2, W_GLA, W_GLA)
    cos, sin = _rope_tables(lat_len)
    extras = (ck_all, cache_v, st_all, cos, sin)

    def mod_row_of_tile(i):
        return jnp.where(i < ctx_tiles, 0, 1 + (i - ctx_tiles) // lat_tiles_per_seq)

    x_c = x_prompt.reshape(n_ctx_tok, D_MODEL)
    x_l = x_sample.reshape(n_lat_tok, D_MODEL)
    cache_bufs = ()
    for l in range(DEPTH):
        ctx_par = 1 if l == 0 else CTX_SEQS_PER_STEP
        x1_c, hs_c, slot_c, wt_c, cnt_c, *cache_bufs = _mixer_call(
            l, ctx_len, ctx_par, False, x_c, mods_all, weights, None, tuple(cache_bufs))
        x1_l, hs_l, slot_l, wt_l, cnt_l = _mixer_call(l, lat_len, 1, True, x_l, mods_all, weights, extras, ())
        plan = _moe_plan(jnp.concatenate([cnt_c[:, :, 0], cnt_l[:, :, 0]], axis=0))
        ys = _expert_call(l, plan, hs_c, hs_l, moe_w1, moe_w3, moe_w2)
        x_c, x_l = _combine_call(l, plan, x1_c, x1_l, slot_c, slot_l, wt_c, wt_l, mods_all, mod_row_of_tile, ys)

    new_k, new_v, new_s = cache_bufs
    return (x_c.reshape(x_prompt.shape), x_l.reshape(x_sample.shape), new_k, new_v, new_s)
```

```python
import functools
import math

import jax
import jax.numpy as jnp
from jax import lax
from jax.experimental import pallas as pl
from jax.experimental.pallas import tpu as pltpu

F32 = jnp.float32
BF16 = jnp.bfloat16

D_MODEL = 1024
DEPTH = 4
GRID_W = 64
SGU_GROUPS = 4
SGU_GROUP_W = 64
SGU_W = SGU_GROUPS * SGU_GROUP_W
SGU_CHUNK = 128
DIFF_HEADS = 4
DIFF_QK = 64
DIFF_V = 2 * DIFF_QK
ROPE_BASE = 10000.0
GLA_HEADS = 4
GLA_DK = 64
GLA_DV = 64
GLA_RANK = 16
GLA_GATE_NORM = 16.0
GLA_CHUNK = 64
N_EXPERTS = 16
N_GROUPS = 4
EXPERTS_PER_GROUP = N_EXPERTS // N_GROUPS
D_EXPERT = 512
EPS = 1e-6

LANES = 128
SUBLANES = 8
MXU_DIM = 256
V7X_VMEM_BYTES = 64 * 1024 * 1024
MIXER_VMEM_LIMIT = V7X_VMEM_BYTES * 7 // 8
SMALL_KERNEL_VMEM_LIMIT = V7X_VMEM_BYTES * 5 // 8

ROW_BLOCK = MXU_DIM
ROW_SLABS = D_MODEL // LANES

W_QK = DIFF_HEADS * 2 * DIFF_QK
W_GLA = GLA_HEADS * GLA_DK
C_AU, C_AV = 0, SGU_W
C_BQ = C_AV + SGU_W
C_BK, C_BV = C_BQ + W_QK, C_BQ + 2 * W_QK
C_CQ = C_BV + DIFF_HEADS * DIFF_V
C_CK, C_CV, C_CR, C_LR = C_CQ + W_GLA, C_CQ + 2 * W_GLA, C_CQ + 3 * W_GLA, C_CQ + 4 * W_GLA
D_PROJ_MAIN = C_LR
D_PROJ_PAD = D_PROJ_MAIN + LANES
M_A, M_B, M_C = 0, SGU_W, SGU_W + DIFF_HEADS * DIFF_V


def _split2(x):
    hi = x.astype(BF16)
    lo = (x - hi.astype(F32)).astype(BF16)
    return hi, lo


def _split3(x):
    hi = x.astype(BF16)
    r = x - hi.astype(F32)
    mid = r.astype(BF16)
    lo = (r - mid.astype(F32)).astype(BF16)
    return hi, mid, lo


def _dot(a, b):
    return jnp.dot(a, b, preferred_element_type=F32)


def _dot_nt(a, b):
    return lax.dot_general(a, b, (((1,), (1,)), ((), ())), preferred_element_type=F32)


def _dot_tn(a, b):
    return lax.dot_general(a, b, (((0,), (0,)), ((), ())), preferred_element_type=F32)


def _iota(shape, dim):
    return lax.broadcasted_iota(jnp.int32, shape, dim)


def _block_ones(width, block):
    r = _iota((width, width), 0) // block
    c = _iota((width, width), 1) // block
    return (r == c)


def _group_sum(z, block):
    width = z.shape[-1]
    outs = []
    for s in range(0, width, MXU_DIM):
        w = min(MXU_DIM, width - s)
        ones = _block_ones(w, block).astype(BF16)
        hi, lo = _split2(z[:, s:s + w])
        outs.append(_dot(hi, ones) + _dot(lo, ones))
    return outs[0] if len(outs) == 1 else jnp.concatenate(outs, axis=-1)


def _group_rms(z, block):
    ms = _group_sum(z * z, block) * (1.0 / block)
    return z * lax.rsqrt(ms + EPS)


def _row_rms(z):
    return z * lax.rsqrt(jnp.mean(z * z, axis=-1, keepdims=True) + EPS)


def _log_sigmoid(x):
    return jnp.minimum(x, 0.0) - jnp.log(1.0 + jnp.exp(-jnp.abs(x)))


ADA_COLS = 1536


def _adaln_kernel(n_cond, cond_t_ref, w_ref, b_ref, o_ref):
    sc = jax.nn.silu(cond_t_ref[...])
    w = w_ref[0]
    rows = [jnp.sum(sc[:, r:r + 1] * w, axis=0, keepdims=True) + b_ref[0] for r in range(n_cond)]
    o_ref[0] = jnp.concatenate(rows + [jnp.zeros((SUBLANES - n_cond, w.shape[1]), F32)], axis=0)


def _adaln_call(cond_t, n_cond, ada_w, ada_b):
    n_col = 6 * D_MODEL // ADA_COLS
    return pl.pallas_call(
        functools.partial(_adaln_kernel, n_cond),
        grid=(DEPTH, n_col),
        in_specs=[
            pl.BlockSpec((D_MODEL, SUBLANES), lambda l, j: (0, 0)),
            pl.BlockSpec((1, D_MODEL, ADA_COLS), lambda l, j: (l, 0, j)),
            pl.BlockSpec((1, 1, ADA_COLS), lambda l, j: (l, 0, j)),
        ],
        out_specs=pl.BlockSpec((1, SUBLANES, ADA_COLS), lambda l, j: (l, 0, j)),
        out_shape=jax.ShapeDtypeStruct((DEPTH, SUBLANES, 6 * D_MODEL), F32),
        compiler_params=pltpu.CompilerParams(
            dimension_semantics=("arbitrary", "arbitrary"), vmem_limit_bytes=SMALL_KERNEL_VMEM_LIMIT),
        name="adaln",
    )(cond_t, ada_w, ada_b.reshape(DEPTH, 1, 6 * D_MODEL))


def _route(hn, rwt_ref, rb_ref):
    h_hi, h_lo = _split2(hn)
    rw = rwt_ref[...]
    rw_hi = rw.astype(BF16)
    rw_lo = (rw - rw_hi.astype(F32)).astype(BF16)
    logits = _dot_nt(rw_hi, h_hi) + _dot_nt(rw_hi, h_lo) + _dot_nt(rw_lo, h_hi)
    aff = jax.nn.sigmoid(logits)
    sel = aff + rb_ref[...]
    n_tok = sel.shape[1]

    def top2_sum(a, b, c, d):
        hi1, lo1 = jnp.maximum(a, b), jnp.minimum(a, b)
        hi2, lo2 = jnp.maximum(c, d), jnp.minimum(c, d)
        return jnp.maximum(hi1, hi2) + jnp.maximum(jnp.minimum(hi1, hi2), jnp.maximum(lo1, lo2))

    scores = []
    for g in range(N_GROUPS):
        rows = [sel[EXPERTS_PER_GROUP * g + j:EXPERTS_PER_GROUP * g + j + 1, :] for j in range(EXPERTS_PER_GROUP)]
        scores.append(top2_sum(*rows))
    best = jnp.zeros((1, n_tok), jnp.int32)
    best_score = scores[0]
    for g in range(1, N_GROUPS):
        upd = scores[g] > best_score
        best = jnp.where(upd, g, best)
        best_score = jnp.where(upd, scores[g], best_score)

    eid_i = _iota((N_EXPERTS, n_tok), 0)
    eid = eid_i.astype(F32)
    neg = jnp.float32(-jnp.inf)
    msel = jnp.where(eid_i // EXPERTS_PER_GROUP == best, sel, neg)
    m1 = jnp.max(msel, axis=0, keepdims=True)
    idx1 = jnp.min(jnp.where(msel == m1, eid, float(N_EXPERTS)), axis=0, keepdims=True)
    msel2 = jnp.where(eid == idx1, neg, msel)
    m2 = jnp.max(msel2, axis=0, keepdims=True)
    idx2 = jnp.min(jnp.where(msel2 == m2, eid, float(N_EXPERTS)), axis=0, keepdims=True)
    w1 = jnp.sum(jnp.where(eid == idx1, aff, 0.0), axis=0, keepdims=True)
    w2 = jnp.sum(jnp.where(eid == idx2, aff, 0.0), axis=0, keepdims=True)
    wsum = w1 + w2
    return idx1.astype(jnp.int32), idx2.astype(jnp.int32), w1 / wsum, w2 / wsum


def _local_slots(idx1, idx2):
    n_tok = idx1.shape[1]
    eid = _iota((N_EXPERTS, n_tok), 0)
    hot1, hot2 = eid == idx1, eid == idx2
    hot = jnp.where(hot1, 1.0, jnp.where(hot2, 1.0, 0.0))
    earlier = jnp.where(_iota((n_tok, n_tok), 0) < _iota((n_tok, n_tok), 1), 1.0, 0.0).astype(BF16)
    before_in_expert = _dot(hot.astype(BF16), earlier)
    counts = jnp.sum(hot, axis=1, keepdims=True)
    lower = jnp.where(_iota((N_EXPERTS, N_EXPERTS), 1) < _iota((N_EXPERTS, N_EXPERTS), 0), 1.0, 0.0).astype(BF16)
    first_slot = _dot(lower, jnp.broadcast_to(counts, (N_EXPERTS, LANES)).astype(BF16))[:, 0:1]
    slot = before_in_expert + first_slot
    slot1 = jnp.sum(jnp.where(hot1, slot, 0.0), axis=0, keepdims=True)
    slot2 = jnp.sum(jnp.where(hot2, slot, 0.0), axis=0, keepdims=True)
    return slot1, slot2, counts


def _slot_one_hot(slot1, slot2, v1, v2):
    n_tok = slot1.shape[1]
    row = _iota((2 * n_tok, n_tok), 0).astype(F32)
    return jnp.where(row == slot1, v1, jnp.where(row == slot2, v2, 0.0))


N_MIXER_WEIGHTS = 15
CTX_SEQS_PER_STEP = 2
MAX_INLINE_BLOCKS = 2


def _mixer_kernel(n_tok, n_par, latent, n_alias, lam_init, *refs):
    it = iter(refs)
    x_ref, mod_ref = next(it), next(it)
    (n1_ref, n2_ref, win_ref, wout_ref, sw_ref, sb_ref, qg_ref, kg_ref, dl_ref, dg_ref,
     w2c_ref, gb_ref, gg_ref, rwt_ref, rb_ref) = (next(it) for _ in range(N_MIXER_WEIGHTS))
    if latent:
        ck_ref, cv_ref, st0_ref, cos_ref, sin_ref = (next(it) for _ in range(5))
    for _ in range(n_alias):
        next(it)
    xo_ref, hs_ref, slot_ref, wt_ref, cnt_ref = (next(it) for _ in range(5))
    if not latent:
        ko_ref, vo_ref, so_ref = (next(it) for _ in range(3))
    proj_ref, mix_ref, q_ref, k_ref, v_ref = (next(it) for _ in range(5))
    gq_ref, gke_ref, gv_ref, gr_ref, dec_ref, go_ref, st_ref = (next(it) for _ in range(7))

    n_blk = n_tok // ROW_BLOCK
    n_ctx = k_ref.shape[0] - n_par * n_tok
    n_keys = n_ctx + n_tok
    mod = mod_ref[0, 0]

    by_block = latent

    def whole_sequence(fn):
        def run():
            fn()

        if by_block:
            pl.when(pl.program_id(1) == 0)(run)
        else:
            run()

    def blocks(body):
        if n_par * n_blk <= MAX_INLINE_BLOCKS:
            for r in range(n_par * n_blk):
                body(r)
        else:
            def step(r, carry):
                body(r)
                return carry
            whole_sequence(lambda: lax.fori_loop(0, n_par * n_blk, step, 0))

    def aligned(start, size):
        return pl.ds(start if isinstance(start, int) else pl.multiple_of(start, size), size)

    def block_rows(r, offset=0):
        return aligned(offset + r * ROW_BLOCK, ROW_BLOCK)

    if not latent:
        for ref in (ko_ref, vo_ref, so_ref):
            for q in range(n_par):
                for other in range(1, ref.shape[1]):
                    ref[q, other] = jnp.zeros(ref.shape[2:], F32)

    lane_group = _iota((SGU_CHUNK, SGU_W), 1) // SGU_GROUP_W
    blk_r = _iota((ROW_BLOCK, ROW_BLOCK), 0)
    blk_c = _iota((ROW_BLOCK, ROW_BLOCK), 1)
    same_chunk = (blk_r // GLA_CHUNK) == (blk_c // GLA_CHUNK)
    tri = (jnp.where(same_chunk & (blk_c <= blk_r), 1.0, 0.0).astype(BF16),
           jnp.where(same_chunk & (blk_c >= blk_r), 1.0, 0.0).astype(BF16))
    chunks_per_blk = ROW_BLOCK // GLA_CHUNK
    head_of_lane = _iota((GLA_CHUNK, W_GLA), 1) // GLA_DK
    stack_r = _iota((GLA_HEADS * GLA_CHUNK, GLA_CHUNK), 0) % GLA_CHUNK
    stack_c = _iota((GLA_HEADS * GLA_CHUNK, GLA_CHUNK), 1)
    causal = (stack_c <= stack_r, stack_c >= stack_r)

    if latent:
        def cached_context():
            for h in range(DIFF_HEADS):
                k_ref[0:n_ctx, h * DIFF_V:(h + 1) * DIFF_V] = ck_ref[0, 0, h].astype(BF16)
                v_ref[0:n_ctx, h * DIFF_V:(h + 1) * DIFF_V] = cv_ref[0, 0, h].astype(BF16)
            st_ref[0] = st0_ref[0, 0]

        whole_sequence(cached_context)
        pair_lo = (_iota((ROW_BLOCK, W_QK), 1) % (DIFF_QK // 2)) < (DIFF_QK // 4)

        def rope(z, rows):
            cos = jnp.concatenate([cos_ref[rows, :]] * DIFF_HEADS, axis=-1)
            sin = jnp.concatenate([sin_ref[rows, :]] * DIFF_HEADS, axis=-1)
            shift = DIFF_QK // 4
            swapped = jnp.where(pair_lo, pltpu.roll(z, W_QK - shift, 1), pltpu.roll(z, shift, 1))
            return z * cos + swapped * sin

    def modulated_input(r):
        h = _row_rms(x_ref[block_rows(r), :]) * n1_ref[0]
        return (h * (1.0 + mod[1:2, :]) + mod[0:1, :]).astype(BF16)

    def spatial_gating(r):
        for c in range(ROW_BLOCK // SGU_CHUNK):
            local = slice(c * SGU_CHUNK, (c + 1) * SGU_CHUNK)
            u = jax.nn.gelu(proj_ref[local, C_AU:C_AU + SGU_W])
            v = _group_rms(jax.nn.gelu(proj_ref[local, C_AV:C_AV + SGU_W]), SGU_GROUP_W).astype(BF16)
            s = sb_ref[0]
            for g in range(SGU_GROUPS):
                s = s + jnp.where(lane_group == g, _dot(sw_ref[0, g], v), 0.0)
            mix_ref[aligned(r * ROW_BLOCK + c * SGU_CHUNK, SGU_CHUNK), M_A:M_A + SGU_W] = (u * s).astype(BF16)

    def attention_operands(r):
        rows = block_rows(r)
        key_rows = block_rows(r, n_ctx)
        seq, seq_rows = r // n_blk, block_rows(r % n_blk)
        qn = _group_rms(proj_ref[:, C_BQ:C_BQ + W_QK], DIFF_QK) * qg_ref[0]
        kn = _group_rms(proj_ref[:, C_BK:C_BK + W_QK], DIFF_QK) * kg_ref[0]
        vv = proj_ref[:, C_BV:C_BV + W_QK]
        if latent:
            qn, kn = rope(qn, rows), rope(kn, rows)
        else:
            for h in range(DIFF_HEADS):
                for i in range(2):
                    lo = h * DIFF_V + i * DIFF_QK
                    ko_ref[seq, 0, h, i, seq_rows, :] = kn[:, lo:lo + DIFF_QK]
                vo_ref[seq, 0, h, seq_rows, :] = vv[:, h * DIFF_V:(h + 1) * DIFF_V]
        q_ref[rows, :] = (qn * (DIFF_QK ** -0.5)).astype(BF16)
        k_ref[key_rows, :] = kn.astype(BF16)
        v_ref[key_rows, :] = vv.astype(BF16)

    def gla_operands(r):
        rows = block_rows(r)
        gpre = _dot(proj_ref[:, C_LR:C_LR + LANES].astype(BF16), w2c_ref[0]) + gb_ref[0]
        gate = _log_sigmoid(gpre) * (1.0 / GLA_GATE_NORM)
        gq = proj_ref[:, C_CQ:C_CQ + W_GLA] * (GLA_DK ** -0.5)
        gk = proj_ref[:, C_CK:C_CK + W_GLA]
        gv = proj_ref[:, C_CV:C_CV + W_GLA].astype(BF16)
        gv_ref[rows, :] = gv
        gr_ref[rows, :] = proj_ref[:, C_CR:C_CR + W_GLA]
        for d in range(2):
            g = gate[:, d * W_GLA:(d + 1) * W_GLA]
            b = sum(_dot(tri[d], p) for p in _split3(g))
            last = GLA_CHUNK - 1 if d == 0 else 0
            b_last = jnp.concatenate(
                [jnp.broadcast_to(b[c * GLA_CHUNK + last:c * GLA_CHUNK + last + 1, :], (GLA_CHUNK, W_GLA))
                 for c in range(chunks_per_blk)], axis=0)
            q_dec = (gq * jnp.exp(b)).astype(BF16)
            k_inv = (gk * jnp.exp(-b)).astype(BF16)
            gq_ref[d, rows, :] = q_dec
            gke_ref[d, rows, :] = (gk * jnp.exp(b_last - b)).astype(BF16)
            for c in range(chunks_per_blk):
                row = c * GLA_CHUNK + last
                dec_ref[d, r * chunks_per_blk + c] = jnp.exp(b[row:row + 1, :])
                chunk = slice(c * GLA_CHUNK, (c + 1) * GLA_CHUNK)
                qd = q_dec[chunk]
                q_stack = jnp.concatenate(
                    [jnp.where(head_of_lane == h, qd, jnp.zeros_like(qd)) for h in range(GLA_HEADS)], axis=0)
                attn = jnp.where(causal[d], _dot_nt(q_stack, k_inv[chunk]), 0.0)
                spread = _dot(attn.astype(BF16), gv[chunk])
                o = jnp.zeros((GLA_CHUNK, W_GLA), F32)
                for h in range(GLA_HEADS):
                    o = o + jnp.where(head_of_lane == h, spread[h * GLA_CHUNK:(h + 1) * GLA_CHUNK, :], 0.0)
                go_ref[d, aligned(r * ROW_BLOCK + c * GLA_CHUNK, GLA_CHUNK), :] = o

    def project_and_split(r):
        proj_ref[...] = _dot(modulated_input(r), win_ref[0])
        spatial_gating(r)
        attention_operands(r)
        gla_operands(r)

    blocks(project_and_split)

    dl = dl_ref[0]
    lam = (jnp.exp(jnp.sum(dl[0:1] * dl[1:2], axis=-1, keepdims=True))
           - jnp.exp(jnp.sum(dl[2:3] * dl[3:4], axis=-1, keepdims=True)) + lam_init)
    sub0 = (_iota((ROW_BLOCK, DIFF_V), 1) < DIFF_QK)

    def softmax(s):
        e = jnp.exp(s - jnp.max(s, axis=-1, keepdims=True))
        return e, jnp.sum(e, axis=-1, keepdims=True)

    def attn_block(r):
        rows = block_rows(r)
        keys = aligned((r // n_blk) * n_keys, n_keys)
        for h in range(DIFF_HEADS):
            cols = slice(h * DIFF_V, (h + 1) * DIFF_V)
            qh = q_ref[rows, cols]
            kh = k_ref[keys, cols]
            e0, z0 = softmax(_dot_nt(jnp.where(sub0, qh, jnp.zeros_like(qh)), kh))
            e1, z1 = softmax(_dot_nt(jnp.where(sub0, jnp.zeros_like(qh), qh), kh))
            w = e0 / z0 - lam * (e1 / z1)
            o = _dot(w.astype(BF16), v_ref[keys, cols])
            o = _row_rms(o) * dg_ref[0] * (1.0 - lam_init)
            mix_ref[rows, M_B + h * DIFF_V:M_B + (h + 1) * DIFF_V] = o.astype(BF16)

    blocks(attn_block)

    if not latent:
        st_ref[...] = jnp.zeros(st_ref.shape, F32)

    n_chunk = n_tok // GLA_CHUNK
    st_diag = (_iota((W_GLA, W_GLA), 0) // GLA_DV) == (_iota((W_GLA, W_GLA), 1) // GLA_DK)

    def gla_step(c, carry):
        for seq in range(n_par):
            for d in range(2):
                cc = seq * n_chunk + (c if d == 0 else n_chunk - 1 - c)
                rows = pl.ds(pl.multiple_of(cc * GLA_CHUNK, GLA_CHUNK), GLA_CHUNK)
                st = st_ref[seq, d]
                go_ref[d, rows, :] = go_ref[d, rows, :] + _dot_nt(gq_ref[d, rows, :], st.astype(BF16))
                upd = _dot_tn(gv_ref[rows, :], gke_ref[d, rows, :])
                st_ref[seq, d] = dec_ref[d, cc] * st + jnp.where(st_diag, upd, 0.0)
        return carry

    whole_sequence(lambda: lax.fori_loop(0, n_chunk, gla_step, 0))

    if not latent:
        for seq in range(n_par):
            for d in range(2):
                s_full = st_ref[seq, d].T
                for h in range(GLA_HEADS):
                    so_ref[seq, 0, d, h] = s_full[h * GLA_DK:(h + 1) * GLA_DK, h * GLA_DV:(h + 1) * GLA_DV]

    def finish_block(r, out_r):
        rows, out_rows = block_rows(r), block_rows(out_r)
        oc = _group_rms(go_ref[0, rows, :] + go_ref[1, rows, :], GLA_DV) * gg_ref[0]
        oc = oc * jax.nn.silu(gr_ref[rows, :])
        mix_ref[rows, M_C:M_C + W_GLA] = oc.astype(BF16)
        x1 = x_ref[rows, :] + mod[2:3, :] * _dot(mix_ref[rows, :], wout_ref[0])
        xo_ref[out_rows, :] = x1
        hn = _row_rms(x1) * n2_ref[0]
        hn = hn * (1.0 + mod[4:5, :]) + mod[3:4, :]
        idx1, idx2, w1, w2 = _route(hn, rwt_ref, rb_ref)
        slot1, slot2, counts = _local_slots(idx1, idx2)
        perm = _slot_one_hot(slot1, slot2, 1.0, 1.0).astype(BF16)
        _to_row_slabs(hs_ref, 2 * out_r * ROW_BLOCK, _dot(perm, hn.astype(BF16)))
        slot_ref[:, out_rows] = jnp.concatenate([slot1, slot2], axis=0).astype(jnp.int32)
        wt_ref[:, out_rows] = jnp.concatenate([w1, w2], axis=0)
        cnt_ref[out_r] = jnp.broadcast_to(counts, (N_EXPERTS, LANES)).astype(jnp.int32)

    if by_block:
        finish_block(pl.program_id(1), 0)
    else:
        for r in range(n_par * n_blk):
            finish_block(r, r)


def _mixer_call(l, n_tok, n_par, latent, x, mods_all, weights, extras, cache_bufs):
    n_seq = x.shape[0] // n_tok
    n_all = x.shape[0]
    assert n_seq % n_par == 0 and not (latent and n_par > 1)
    n_step_tok = n_par * n_tok
    n_keys = n_step_tok + (extras[0].shape[3] if latent else 0)
    n_chunk = n_step_tok // GLA_CHUNK
    lam_init = 0.8 - 0.6 * math.exp(-0.3 * l)

    single = pl.Buffered(1)
    seq_mode = single if latent else None

    def layer(arr):
        tail = arr.shape[1:]
        return pl.BlockSpec((1,) + tail, lambda s, *_r, _n=len(tail): (l,) + (0,) * _n, pipeline_mode=single)

    def const(arr):
        return pl.BlockSpec(arr.shape, lambda s, *_r, _n=arr.ndim: (0,) * _n, pipeline_mode=single)

    def tok_spec(width):
        return pl.BlockSpec((n_step_tok, width), lambda s, *_r: (s, 0), pipeline_mode=seq_mode)

    mod_row = (lambda s: 1 + s) if latent else (lambda s: 0)
    in_specs = [tok_spec(D_MODEL),
                pl.BlockSpec((1, 1, 6, D_MODEL), lambda s, *_r: (l, mod_row(s), 0, 0))]
    in_specs += [layer(w) for w in weights[:N_MIXER_WEIGHTS - 2]] + [const(w) for w in weights[-2:]]
    operands = [x, mods_all] + list(weights)
    if latent:
        ck, cv, st0, cos, sin = extras
        in_specs += [
            pl.BlockSpec((1, 1) + ck.shape[2:], lambda s, *_r: (s, l, 0, 0, 0)),
            pl.BlockSpec((1, 1) + cv.shape[2:], lambda s, *_r: (s, l, 0, 0, 0)),
            pl.BlockSpec((1, 1) + st0.shape[2:], lambda s, *_r: (s, l, 0, 0, 0)),
            const(cos), const(sin),
        ]
        operands += [ck, cv, st0, cos, sin]
    n_in = len(operands)
    in_specs += [pl.BlockSpec(memory_space=pl.ANY)] * len(cache_bufs)
    operands += list(cache_bufs)

    tiles_per_step = n_step_tok // ROW_BLOCK
    out_shape = [
        jax.ShapeDtypeStruct((n_all, D_MODEL), F32),
        jax.ShapeDtypeStruct((2 * n_all * ROW_SLABS, LANES), F32),
        jax.ShapeDtypeStruct((2, n_all), jnp.int32),
        jax.ShapeDtypeStruct((2, n_all), F32),
        jax.ShapeDtypeStruct((n_all // ROW_BLOCK, N_EXPERTS, LANES), jnp.int32),
    ]
    if latent:
        grid = (n_seq, tiles_per_step)
        out_tok, out_tiles = ROW_BLOCK, 1
        at = lambda s, r: s * tiles_per_step + r
    else:
        grid = (n_seq // n_par,)
        out_tok, out_tiles = n_step_tok, tiles_per_step
        at = lambda s: s
    out_specs = [
        pl.BlockSpec((out_tok, D_MODEL), lambda *g: (at(*g), 0)),
        pl.BlockSpec((2 * out_tok * ROW_SLABS, LANES), lambda *g: (at(*g), 0)),
        pl.BlockSpec((2, out_tok), lambda *g: (0, at(*g))),
        pl.BlockSpec((2, out_tok), lambda *g: (0, at(*g))),
        pl.BlockSpec((out_tiles, N_EXPERTS, LANES), lambda *g: (at(*g), 0, 0)),
    ]
    n_shared_out = len(out_shape)
    aliases = {}
    if not latent:
        out_shape += [
            jax.ShapeDtypeStruct((n_seq, DEPTH, DIFF_HEADS, 2, n_tok, DIFF_QK), F32),
            jax.ShapeDtypeStruct((n_seq, DEPTH, DIFF_HEADS, n_tok, DIFF_V), F32),
            jax.ShapeDtypeStruct((n_seq, DEPTH, 2, GLA_HEADS, GLA_DK, GLA_DV), F32),
        ]
        n_lay, lay = (1, l) if cache_bufs else (DEPTH, 0)
        out_specs += [
            pl.BlockSpec((n_par, n_lay, DIFF_HEADS, 2, n_tok, DIFF_QK), lambda s: (s, lay, 0, 0, 0, 0)),
            pl.BlockSpec((n_par, n_lay, DIFF_HEADS, n_tok, DIFF_V), lambda s: (s, lay, 0, 0, 0)),
            pl.BlockSpec((n_par, n_lay, 2, GLA_HEADS, GLA_DK, GLA_DV), lambda s: (s, lay, 0, 0, 0, 0)),
        ]
        aliases = {n_in + j: n_shared_out + j for j in range(len(cache_bufs))}
    scratch = [
        pltpu.VMEM((ROW_BLOCK, D_PROJ_PAD), F32),
        pltpu.VMEM((n_step_tok, D_MODEL), BF16),
        pltpu.VMEM((n_step_tok, W_QK), BF16),
        pltpu.VMEM((n_keys, W_QK), BF16),
        pltpu.VMEM((n_keys, W_QK), BF16),
        pltpu.VMEM((2, n_step_tok, W_GLA), BF16),
        pltpu.VMEM((2, n_step_tok, W_GLA), BF16),
        pltpu.VMEM((n_step_tok, W_GLA), BF16),
        pltpu.VMEM((n_step_tok, W_GLA), F32),
        pltpu.VMEM((2, n_chunk, 1, W_GLA), F32),
        pltpu.VMEM((2, n_step_tok, W_GLA), F32),
        pltpu.VMEM((n_par, 2, W_GLA, W_GLA), F32),
    ]
    return pl.pallas_call(
        functools.partial(_mixer_kernel, n_tok, n_par, latent, len(cache_bufs), lam_init),
        grid=grid,
        in_specs=in_specs,
        out_specs=out_specs,
        out_shape=out_shape,
        scratch_shapes=scratch,
        input_output_aliases=aliases,
        compiler_params=pltpu.CompilerParams(
            dimension_semantics=("arbitrary",) * len(grid), vmem_limit_bytes=MIXER_VMEM_LIMIT),
        name="mixer_latent" if latent else "mixer_context",
    )(*operands)


PAIR_BLOCK = 2 * ROW_BLOCK
COPY_SIZES = tuple(ROW_BLOCK >> k for k in range(ROW_BLOCK.bit_length()))
LARGE_COPY = 64
GATHER_AHEAD = 2
GATHER_SLOTS = GATHER_AHEAD + 1


def _segment_copies(n_rows, make_copy, act):
    def copy_if_set(size):
        @pl.when((n_rows & size) != 0)
        def _():
            act(make_copy(n_rows & (-2 * size), size))

    n_large = COPY_SIZES.index(LARGE_COPY) + 1

    @pl.when(n_rows >= LARGE_COPY)
    def _():
        for size in COPY_SIZES[:n_large]:
            copy_if_set(size)

    for size in COPY_SIZES[n_large:]:
        copy_if_set(size)


def _start(copy):
    copy.start()


def _wait(copy):
    copy.wait()


def _slab_rows(first_row, n_rows, slab):
    return pl.ds(first_row * ROW_SLABS + slab, n_rows, stride=ROW_SLABS)


def _to_row_slabs(ref, first_row, value):
    for s in range(ROW_SLABS):
        ref[_slab_rows(first_row, value.shape[0], s), :] = value[:, s * LANES:(s + 1) * LANES]


def _from_row_slabs(ref, first_row, n_rows):
    return jnp.concatenate([ref[_slab_rows(first_row, n_rows, s), :] for s in range(ROW_SLABS)], axis=-1)


SLAB_PAIR_W = 2 * LANES
N_SLAB_PAIRS = ROW_SLABS // 2


def _slab_pair_cols(g):
    return slice(g * SLAB_PAIR_W, (g + 1) * SLAB_PAIR_W)


def _load_slab_pair(ref, first_row, n_rows, g):
    return jnp.concatenate([ref[_slab_rows(first_row, n_rows, s), :] for s in (2 * g, 2 * g + 1)], axis=-1)


def _row_span(ref, first_row, n_rows):
    return ref.at[pl.ds(pl.multiple_of(first_row * ROW_SLABS, ROW_SLABS), n_rows * ROW_SLABS)]


def _two_streams(n_first_tiles):
    def first(i, *_):
        return (jnp.minimum(i, n_first_tiles - 1), 0)

    def second(i, *_):
        return (jnp.maximum(i - n_first_tiles, 0), 0)

    return first, second


N_EXPERT_TABLES = 10
OUT_SLOTS = 2


def _expert_kernel(n_ctx_tiles, n_tiles_max, te_ref, first_ref, rows_ref, jlo_ref, jhi_ref, cpre_ref, cnt_ref,
                   lofs_ref, tile0_ref, ntile_ref, hs_c_ref, hs_l_ref, w1_ref, w3_ref, w2_ref, ys_ref,
                   xbuf_ref, obuf_ref, w1b_ref, w3b_ref, w2b_ref, sem, out_sem):
    expert = pl.program_id(0)
    n_tiles = tile0_ref[N_EXPERTS]

    def gather(t, act):
        slot = t % GATHER_SLOTS
        e, first = te_ref[t], first_ref[t]
        last = first + rows_ref[t]

        def segment_of(hs_ref, first_tile):
            def body(j, carry):
                k = j * N_EXPERTS + e
                seg_first = cpre_ref[k]
                lo = jnp.maximum(seg_first, first)
                n = jnp.maximum(jnp.minimum(seg_first + cnt_ref[k], last) - lo, 0)
                src = (j - first_tile) * PAIR_BLOCK + lofs_ref[k] + (lo - seg_first)
                dst = slot * ROW_BLOCK + lo - first
                _segment_copies(n, lambda done, size: pltpu.make_async_copy(
                    _row_span(hs_ref, src + done, size), _row_span(xbuf_ref, dst + done, size), sem.at[slot]), act)
                return carry
            return body

        jlo, jhi = jlo_ref[t], jhi_ref[t]
        lax.fori_loop(jnp.minimum(jlo, n_ctx_tiles), jnp.minimum(jhi, n_ctx_tiles), segment_of(hs_c_ref, 0), 0)
        lax.fori_loop(jnp.maximum(jlo, n_ctx_tiles), jnp.maximum(jhi, n_ctx_tiles),
                      segment_of(hs_l_ref, n_ctx_tiles), 0)

    def out_copy(t, oslot):
        return pltpu.make_async_copy(
            _row_span(obuf_ref, oslot * ROW_BLOCK, ROW_BLOCK), _row_span(ys_ref, t * ROW_BLOCK, ROW_BLOCK),
            out_sem.at[oslot])

    @pl.when(expert == 0)
    def _():
        xbuf_ref[...] = jnp.zeros(xbuf_ref.shape, F32)
        for t in range(GATHER_AHEAD):
            gather(t, _start)

    w1b_ref[...] = w1_ref[0, 0].astype(BF16)
    w3b_ref[...] = w3_ref[0, 0].astype(BF16)
    w2b_ref[...] = w2_ref[0, 0].astype(BF16)
    tile0, n_own = tile0_ref[expert], ntile_ref[expert]

    def tile_body(k, carry):
        t = tile0 + k
        slot, oslot = t % GATHER_SLOTS, t % OUT_SLOTS

        @pl.when(t + GATHER_AHEAD < n_tiles)
        def _():
            gather(t + GATHER_AHEAD, _start)

        n_rows = rows_ref[t]
        _segment_copies(n_rows, lambda done, size: pltpu.make_async_copy(
            _row_span(hs_c_ref, done, size), _row_span(xbuf_ref, slot * ROW_BLOCK + done, size), sem.at[slot]), _wait)

        @pl.when(t >= OUT_SLOTS)
        def _():
            out_copy(t, oslot).wait()

        live = _iota((ROW_BLOCK, D_MODEL), 0) < n_rows
        x = jnp.where(live, _from_row_slabs(xbuf_ref, slot * ROW_BLOCK, ROW_BLOCK), 0.0).astype(BF16)
        hid = jax.nn.silu(_dot(x, w1b_ref[...])) * _dot(x, w3b_ref[...])
        _to_row_slabs(obuf_ref, oslot * ROW_BLOCK, _dot(hid.astype(BF16), w2b_ref[...]))
        out_copy(t, oslot).start()
        return carry

    lax.fori_loop(0, n_own, tile_body, 0)

    @pl.when(expert == N_EXPERTS - 1)
    def _():
        for oslot in range(OUT_SLOTS):
            @pl.when(n_tiles > oslot)
            def _():
                out_copy(0, oslot).wait()
        obuf_ref[...] = jnp.zeros(obuf_ref.shape, F32)

        def fill(t, carry):
            out_copy(t, 0).start()
            out_copy(t, 0).wait()
            return carry

        lax.fori_loop(n_tiles, n_tiles_max, fill, 0)


def _expert_call(l, plan, hs_c, hs_l, w1, w3, w2):
    tables = plan["expert_tables"]
    n_tiles_max = tables[0].shape[0]
    n_ctx_tiles = hs_c.shape[0] // (PAIR_BLOCK * ROW_SLABS)

    def weight(shape):
        return pl.BlockSpec((1, 1) + shape, lambda e, *_: (l, e, 0, 0))

    return pl.pallas_call(
        functools.partial(_expert_kernel, n_ctx_tiles, n_tiles_max),
        grid_spec=pltpu.PrefetchScalarGridSpec(
            num_scalar_prefetch=N_EXPERT_TABLES,
            grid=(N_EXPERTS,),
            in_specs=[pl.BlockSpec(memory_space=pl.ANY), pl.BlockSpec(memory_space=pl.ANY),
                      weight((D_MODEL, D_EXPERT)), weight((D_MODEL, D_EXPERT)), weight((D_EXPERT, D_MODEL))],
            out_specs=pl.BlockSpec(memory_space=pl.ANY),
            scratch_shapes=[pltpu.VMEM((GATHER_SLOTS * ROW_BLOCK * ROW_SLABS, LANES), F32),
                            pltpu.VMEM((OUT_SLOTS * ROW_BLOCK * ROW_SLABS, LANES), F32),
                            pltpu.VMEM((D_MODEL, D_EXPERT), BF16), pltpu.VMEM((D_MODEL, D_EXPERT), BF16),
                            pltpu.VMEM((D_EXPERT, D_MODEL), BF16),
                            pltpu.SemaphoreType.DMA((GATHER_SLOTS,)), pltpu.SemaphoreType.DMA((OUT_SLOTS,))],
        ),
        out_shape=jax.ShapeDtypeStruct((n_tiles_max * ROW_BLOCK * ROW_SLABS, LANES), F32),
        compiler_params=pltpu.CompilerParams(
            dimension_semantics=("arbitrary",), vmem_limit_bytes=SMALL_KERNEL_VMEM_LIMIT),
        name="moe_experts",
    )(*tables, hs_c, hs_l, w1, w3, w2)


N_COMBINE_TABLES = 4
COMBINE_TILES_PER_STEP = 2


def _combine_kernel(n_ctx_steps, cnt_ref, cpre_ref, lofs_ref, starts_ref, x_c_ref, x_l_ref, slot_c_ref, slot_l_ref,
                    wt_c_ref, wt_l_ref, mod_ref, ys_ref, xo_c_ref, xo_l_ref, buf_ref, sem):
    step = pl.program_id(0)
    n_tiles = pl.num_programs(0) * COMBINE_TILES_PER_STEP

    def collect(t, act):
        slot = t % GATHER_SLOTS

        def body(e, carry):
            k = t * N_EXPERTS + e
            src, dst = starts_ref[e] + cpre_ref[k], slot * PAIR_BLOCK + lofs_ref[k]
            _segment_copies(cnt_ref[k], lambda done, size: pltpu.make_async_copy(
                _row_span(ys_ref, src + done, size), _row_span(buf_ref, dst + done, size), sem.at[slot]), act)
            return carry

        lax.fori_loop(0, N_EXPERTS, body, 0)

    @pl.when(step == 0)
    def _():
        for t in range(GATHER_AHEAD):
            collect(t, _start)

    gate = mod_ref[0, 0, 5:6, :]

    def finish(part, first_row, x_ref, slot_ref, wt_ref, xo_ref):
        tokens = pl.ds(part * ROW_BLOCK, ROW_BLOCK)
        slots, wts = slot_ref[:, tokens].astype(F32), wt_ref[:, tokens]
        slot1, slot2 = slots[0:1], slots[1:2]
        weight_of_row = jnp.sum(_slot_one_hot(slot1, slot2, wts[0:1], wts[1:2]), axis=1, keepdims=True)
        gather_rows = _slot_one_hot(slot1, slot2, 1.0, 1.0).T.astype(BF16)
        for g in range(N_SLAB_PAIRS):
            cols = _slab_pair_cols(g)
            hi, lo = _split2(_load_slab_pair(buf_ref, first_row, PAIR_BLOCK, g) * weight_of_row)
            y = _dot(gather_rows, hi) + _dot(gather_rows, lo)
            xo_ref[tokens, cols] = x_ref[tokens, cols] + gate[:, cols] * y

    for part in range(COMBINE_TILES_PER_STEP):
        j = step * COMBINE_TILES_PER_STEP + part

        @pl.when(j + GATHER_AHEAD < n_tiles)
        def _():
            collect(j + GATHER_AHEAD, _start)

        slot = j % GATHER_SLOTS
        for piece in range(PAIR_BLOCK // ROW_BLOCK):
            pltpu.make_async_copy(
                _row_span(ys_ref, piece * ROW_BLOCK, ROW_BLOCK),
                _row_span(buf_ref, slot * PAIR_BLOCK + piece * ROW_BLOCK, ROW_BLOCK), sem.at[slot]).wait()
        @pl.when(step < n_ctx_steps)
        def _():
            finish(part, slot * PAIR_BLOCK, x_c_ref, slot_c_ref, wt_c_ref, xo_c_ref)

        @pl.when(step >= n_ctx_steps)
        def _():
            finish(part, slot * PAIR_BLOCK, x_l_ref, slot_l_ref, wt_l_ref, xo_l_ref)


def _combine_call(l, plan, x_c, x_l, slot_c, slot_l, wt_c, wt_l, mods_all, mod_row_of_tile, ys):
    step_rows = COMBINE_TILES_PER_STEP * ROW_BLOCK
    n_steps = (x_c.shape[0] + x_l.shape[0]) // step_rows
    n_ctx_steps = x_c.shape[0] // step_rows
    first, second = _two_streams(n_ctx_steps)

    def lanes(index_map):
        return lambda i, *_: index_map(i)[::-1]

    return pl.pallas_call(
        functools.partial(_combine_kernel, n_ctx_steps),
        grid_spec=pltpu.PrefetchScalarGridSpec(
            num_scalar_prefetch=N_COMBINE_TABLES,
            grid=(n_steps,),
            in_specs=[pl.BlockSpec((step_rows, D_MODEL), first),
                      pl.BlockSpec((step_rows, D_MODEL), second),
                      pl.BlockSpec((2, step_rows), lanes(first)),
                      pl.BlockSpec((2, step_rows), lanes(second)),
                      pl.BlockSpec((2, step_rows), lanes(first)),
                      pl.BlockSpec((2, step_rows), lanes(second)),
                      pl.BlockSpec((1, 1, 6, D_MODEL),
                                   lambda i, *_: (l, mod_row_of_tile(i * COMBINE_TILES_PER_STEP), 0, 0)),
                      pl.BlockSpec(memory_space=pl.ANY)],
            out_specs=[pl.BlockSpec((step_rows, D_MODEL), first),
                       pl.BlockSpec((step_rows, D_MODEL), second)],
            scratch_shapes=[pltpu.VMEM((GATHER_SLOTS * PAIR_BLOCK * ROW_SLABS, LANES), F32),
                            pltpu.SemaphoreType.DMA((GATHER_SLOTS,))],
        ),
        out_shape=[jax.ShapeDtypeStruct(x_c.shape, F32), jax.ShapeDtypeStruct(x_l.shape, F32)],
        compiler_params=pltpu.CompilerParams(
            dimension_semantics=("arbitrary",), vmem_limit_bytes=SMALL_KERNEL_VMEM_LIMIT),
        name="moe_combine",
    )(*plan["combine_tables"], x_c, x_l, slot_c, slot_l, wt_c, wt_l, mods_all, ys)


def _moe_plan(cnt):
    n_tok_tiles = cnt.shape[0]
    n_tiles = n_tok_tiles * PAIR_BLOCK // ROW_BLOCK + N_EXPERTS
    lofs = jnp.cumsum(cnt, axis=1) - cnt
    cpre = jnp.cumsum(cnt, axis=0) - cnt
    counts = jnp.sum(cnt, axis=0)
    padded = (counts + ROW_BLOCK - 1) // ROW_BLOCK * ROW_BLOCK
    ends = jnp.cumsum(padded)
    starts = ends - padded
    tile_start = jnp.arange(n_tiles, dtype=jnp.int32) * ROW_BLOCK
    tile_expert = jnp.minimum(
        jnp.sum((tile_start[:, None] >= ends[None, :]).astype(jnp.int32), axis=1), N_EXPERTS - 1)
    hot = tile_expert[:, None] == jnp.arange(N_EXPERTS, dtype=jnp.int32)[None, :]
    first = tile_start - jnp.sum(jnp.where(hot, starts[None, :], 0), axis=1)
    rows = jnp.clip(jnp.sum(jnp.where(hot, counts[None, :], 0), axis=1) - first, 0, ROW_BLOCK)
    seg_first = jnp.sum(jnp.where(hot[:, None, :], cpre[None, :, :], 0), axis=2)
    seg_rows = jnp.sum(jnp.where(hot[:, None, :], cnt[None, :, :], 0), axis=2)
    overlap = (seg_first < (first + rows)[:, None]) & (seg_first + seg_rows > first[:, None])
    j = jnp.arange(n_tok_tiles, dtype=jnp.int32)[None, :]
    jlo = jnp.min(jnp.where(overlap, j, n_tok_tiles), axis=1)
    jhi = jnp.max(jnp.where(overlap, j + 1, 0), axis=1)
    i32 = lambda a: a.astype(jnp.int32).reshape(-1)
    tile0 = jnp.concatenate([starts, ends[-1:]]) // ROW_BLOCK
    return {
        "expert_tables": tuple(i32(a) for a in (tile_expert, first, rows, jlo, jhi, cpre, cnt, lofs,
                                                tile0, padded // ROW_BLOCK)),
        "combine_tables": tuple(i32(a) for a in (cnt, cpre, lofs, starts)),
    }


def _rope_tables(n_tok):
    n_rows = n_tok // GRID_W
    pos_r = jnp.repeat(jnp.arange(n_rows), GRID_W)
    pos_c = jnp.tile(jnp.arange(GRID_W), n_rows)
    half = DIFF_QK // 2
    nf = half // 2
    freqs = ROPE_BASE ** (-jnp.arange(nf, dtype=F32) / nf)

    def tables(pos):
        ang = pos.astype(F32)[:, None] * freqs
        cos, sin = jnp.cos(ang), jnp.sin(ang)
        return jnp.concatenate([cos, cos], axis=-1), jnp.concatenate([-sin, sin], axis=-1)

    cos_r, sin_r = tables(pos_r)
    cos_c, sin_c = tables(pos_c)
    cos = jnp.concatenate([cos_r, cos_c], axis=-1)
    sin = jnp.concatenate([sin_r, sin_c], axis=-1)
    return jnp.concatenate([cos, cos], axis=-1), jnp.concatenate([sin, sin], axis=-1)


def _mixer_weights(w_in, w_out, sgu_w, sgu_b, q_norm_g, k_norm_g, diff_lambda, diff_norm_g, gla_w2, gla_b,
                   gla_norm_g, norm1_g, norm2_g, router_w, router_bias):
    w_in_pad = jnp.concatenate(
        [w_in.astype(BF16), jnp.zeros(w_in.shape[:2] + (D_PROJ_PAD - w_in.shape[2],), BF16)], axis=-1)
    w2cat = jnp.zeros((DEPTH, LANES, 2 * W_GLA), F32)
    w2cat = w2cat.at[:, 0:GLA_RANK, 0:W_GLA].set(gla_w2[:, 0]).at[:, GLA_RANK:2 * GLA_RANK, W_GLA:].set(gla_w2[:, 1])
    return (
        norm1_g[:, None, :], norm2_g[:, None, :], w_in_pad, w_out.astype(BF16),
        sgu_w.astype(BF16), jnp.repeat(sgu_b.transpose(0, 2, 1), SGU_GROUP_W, axis=2),
        jnp.tile(q_norm_g, (1, W_QK // DIFF_QK))[:, None, :], jnp.tile(k_norm_g, (1, W_QK // DIFF_QK))[:, None, :],
        diff_lambda, diff_norm_g[:, None, :],
        w2cat.astype(BF16), gla_b.reshape(DEPTH, 1, 2 * W_GLA), jnp.tile(gla_norm_g, (1, GLA_HEADS))[:, None, :],
        router_w.T, router_bias[:, None],
    )


def kernel(x_prompt, x_sample, cache_k, cache_v, state_gla, c, c_ctx, w_in, w_out, sgu_w, sgu_b, q_norm_g, k_norm_g,
           diff_lambda, diff_norm_g, gla_w2, gla_b, gla_norm_g, norm1_g, norm2_g, ada_w, ada_b, router_w, router_bias,
           moe_w1, moe_w3, moe_w2):
    n_ctx_seq, ctx_len, _ = x_prompt.shape
    n_lat_seq, lat_len, _ = x_sample.shape
    n_ctx_tok = n_ctx_seq * ctx_len
    n_lat_tok = n_lat_seq * lat_len
    ctx_tiles = n_ctx_tok // ROW_BLOCK
    lat_tiles_per_seq = lat_len // ROW_BLOCK

    n_cond = 1 + n_lat_seq
    cond_t = jnp.zeros((D_MODEL, SUBLANES), F32).at[:, 0].set(c_ctx).at[:, 1:n_cond].set(c.T)
    mods_all = _adaln_call(cond_t, n_cond, ada_w, ada_b)[:, :n_cond].reshape(DEPTH, n_cond, 6, D_MODEL)
    weights = _mixer_weights(w_in, w_out, sgu_w, sgu_b, q_norm_g, k_norm_g, diff_lambda, diff_norm_g, gla_w2, gla_b,
                             gla_norm_g, norm1_g, norm2_g, router_w, router_bias)

    ck_all = cache_k.transpose(0, 1, 2, 4, 3, 5).reshape(cache_k.shape[:3] + (cache_k.shape[4], DIFF_V))
    st_all = jnp.einsum('bldhkv,hg->bldhvgk', state_gla, jnp.eye(GLA_HEADS, dtype=F32)).reshape(
        n_lat_seq, DEPTH, 2, W_GLA, W_GLA)
    cos, sin = _rope_tables(lat_len)
    extras = (ck_all, cache_v, st_all, cos, sin)

    def mod_row_of_tile(i):
        return jnp.where(i < ctx_tiles, 0, 1 + (i - ctx_tiles) // lat_tiles_per_seq)

    x_c = x_prompt.reshape(n_ctx_tok, D_MODEL)
    x_l = x_sample.reshape(n_lat_tok, D_MODEL)
    cache_bufs = ()
    for l in range(DEPTH):
        ctx_par = 1 if l == 0 else CTX_SEQS_PER_STEP
        x1_c, hs_c, slot_c, wt_c, cnt_c, *cache_bufs = _mixer_call(
            l, ctx_len, ctx_par, False, x_c, mods_all, weights, None, tuple(cache_bufs))
        x1_l, hs_l, slot_l, wt_l, cnt_l = _mixer_call(l, lat_len, 1, True, x_l, mods_all, weights, extras, ())
        plan = _moe_plan(jnp.concatenate([cnt_c[:, :, 0], cnt_l[:, :, 0]], axis=0))
        ys = _expert_call(l, plan, hs_c, hs_l, moe_w1, moe_w3, moe_w2)
        x_c, x_l = _combine_call(l, plan, x1_c, x1_l, slot_c, slot_l, wt_c, wt_l, mods_all, mod_row_of_tile, ys)

    new_k, new_v, new_s = cache_bufs
    return (x_c.reshape(x_prompt.shape), x_l.reshape(x_sample.shape), new_k, new_v, new_s)
```

```python
import functools
import math

import jax
import jax.numpy as jnp
from jax import lax
from jax.experimental import pallas as pl
from jax.experimental.pallas import tpu as pltpu

F32 = jnp.float32
BF16 = jnp.bfloat16

D_MODEL = 1024
DEPTH = 4
GRID_W = 64
SGU_GROUPS = 4
SGU_GROUP_W = 64
SGU_W = SGU_GROUPS * SGU_GROUP_W
SGU_CHUNK = 128
DIFF_HEADS = 4
DIFF_QK = 64
DIFF_V = 2 * DIFF_QK
ROPE_BASE = 10000.0
GLA_HEADS = 4
GLA_DK = 64
GLA_DV = 64
GLA_RANK = 16
GLA_GATE_NORM = 16.0
GLA_CHUNK = 64
N_EXPERTS = 16
N_GROUPS = 4
EXPERTS_PER_GROUP = N_EXPERTS // N_GROUPS
D_EXPERT = 512
EPS = 1e-6

LANES = 128
SUBLANES = 8
MXU_DIM = 256
V7X_VMEM_BYTES = 64 * 1024 * 1024
MIXER_VMEM_LIMIT = V7X_VMEM_BYTES * 7 // 8
SMALL_KERNEL_VMEM_LIMIT = V7X_VMEM_BYTES * 5 // 8

ROW_BLOCK = MXU_DIM
ROW_SLABS = D_MODEL // LANES

W_QK = DIFF_HEADS * 2 * DIFF_QK
W_GLA = GLA_HEADS * GLA_DK
C_AU, C_AV = 0, SGU_W
C_BQ = C_AV + SGU_W
C_BK, C_BV = C_BQ + W_QK, C_BQ + 2 * W_QK
C_CQ = C_BV + DIFF_HEADS * DIFF_V
C_CK, C_CV, C_CR, C_LR = C_CQ + W_GLA, C_CQ + 2 * W_GLA, C_CQ + 3 * W_GLA, C_CQ + 4 * W_GLA
D_PROJ_MAIN = C_LR
D_PROJ_PAD = D_PROJ_MAIN + LANES
M_A, M_B, M_C = 0, SGU_W, SGU_W + DIFF_HEADS * DIFF_V


def _split2(x):
    hi = x.astype(BF16)
    lo = (x - hi.astype(F32)).astype(BF16)
    return hi, lo


def _split3(x):
    hi = x.astype(BF16)
    r = x - hi.astype(F32)
    mid = r.astype(BF16)
    lo = (r - mid.astype(F32)).astype(BF16)
    return hi, mid, lo


def _dot(a, b):
    return jnp.dot(a, b, preferred_element_type=F32)


def _dot_nt(a, b):
    return lax.dot_general(a, b, (((1,), (1,)), ((), ())), preferred_element_type=F32)


def _dot_tn(a, b):
    return lax.dot_general(a, b, (((0,), (0,)), ((), ())), preferred_element_type=F32)


def _iota(shape, dim):
    return lax.broadcasted_iota(jnp.int32, shape, dim)


def _block_ones(width, block):
    r = _iota((width, width), 0) // block
    c = _iota((width, width), 1) // block
    return (r == c)


def _group_sum(z, block):
    width = z.shape[-1]
    outs = []
    for s in range(0, width, MXU_DIM):
        w = min(MXU_DIM, width - s)
        ones = _block_ones(w, block).astype(BF16)
        hi, lo = _split2(z[:, s:s + w])
        outs.append(_dot(hi, ones) + _dot(lo, ones))
    return outs[0] if len(outs) == 1 else jnp.concatenate(outs, axis=-1)


def _group_rms(z, block):
    ms = _group_sum(z * z, block) * (1.0 / block)
    return z * lax.rsqrt(ms + EPS)


def _row_rms(z):
    return z * lax.rsqrt(jnp.mean(z * z, axis=-1, keepdims=True) + EPS)


def _log_sigmoid(x):
    return jnp.minimum(x, 0.0) - jnp.log(1.0 + jnp.exp(-jnp.abs(x)))


ADA_COLS = 1536


def _adaln_kernel(n_cond, cond_t_ref, w_ref, b_ref, o_ref):
    sc = jax.nn.silu(cond_t_ref[...])
    w = w_ref[0]
    rows = [jnp.sum(sc[:, r:r + 1] * w, axis=0, keepdims=True) + b_ref[0] for r in range(n_cond)]
    o_ref[0] = jnp.concatenate(rows + [jnp.zeros((SUBLANES - n_cond, w.shape[1]), F32)], axis=0)


def _adaln_call(cond_t, n_cond, ada_w, ada_b):
    n_col = 6 * D_MODEL // ADA_COLS
    return pl.pallas_call(
        functools.partial(_adaln_kernel, n_cond),
        grid=(DEPTH, n_col),
        in_specs=[
            pl.BlockSpec((D_MODEL, SUBLANES), lambda l, j: (0, 0)),
            pl.BlockSpec((1, D_MODEL, ADA_COLS), lambda l, j: (l, 0, j)),
            pl.BlockSpec((1, 1, ADA_COLS), lambda l, j: (l, 0, j)),
        ],
        out_specs=pl.BlockSpec((1, SUBLANES, ADA_COLS), lambda l, j: (l, 0, j)),
        out_shape=jax.ShapeDtypeStruct((DEPTH, SUBLANES, 6 * D_MODEL), F32),
        compiler_params=pltpu.CompilerParams(
            dimension_semantics=("arbitrary", "arbitrary"), vmem_limit_bytes=SMALL_KERNEL_VMEM_LIMIT),
        name="adaln",
    )(cond_t, ada_w, ada_b.reshape(DEPTH, 1, 6 * D_MODEL))


def _route(hn, rwt_ref, rb_ref):
    h_hi, h_lo = _split2(hn)
    rw = rwt_ref[...]
    rw_hi = rw.astype(BF16)
    rw_lo = (rw - rw_hi.astype(F32)).astype(BF16)
    logits = _dot_nt(rw_hi, h_hi) + _dot_nt(rw_hi, h_lo) + _dot_nt(rw_lo, h_hi)
    aff = jax.nn.sigmoid(logits)
    sel = aff + rb_ref[...]
    n_tok = sel.shape[1]

    def top2_sum(a, b, c, d):
        hi1, lo1 = jnp.maximum(a, b), jnp.minimum(a, b)
        hi2, lo2 = jnp.maximum(c, d), jnp.minimum(c, d)
        return jnp.maximum(hi1, hi2) + jnp.maximum(jnp.minimum(hi1, hi2), jnp.maximum(lo1, lo2))

    scores = []
    for g in range(N_GROUPS):
        rows = [sel[EXPERTS_PER_GROUP * g + j:EXPERTS_PER_GROUP * g + j + 1, :] for j in range(EXPERTS_PER_GROUP)]
        scores.append(top2_sum(*rows))
    best = jnp.zeros((1, n_tok), jnp.int32)
    best_score = scores[0]
    for g in range(1, N_GROUPS):
        upd = scores[g] > best_score
        best = jnp.where(upd, g, best)
        best_score = jnp.where(upd, scores[g], best_score)

    eid_i = _iota((N_EXPERTS, n_tok), 0)
    eid = eid_i.astype(F32)
    neg = jnp.float32(-jnp.inf)
    msel = jnp.where(eid_i // EXPERTS_PER_GROUP == best, sel, neg)
    m1 = jnp.max(msel, axis=0, keepdims=True)
    idx1 = jnp.min(jnp.where(msel == m1, eid, float(N_EXPERTS)), axis=0, keepdims=True)
    msel2 = jnp.where(eid == idx1, neg, msel)
    m2 = jnp.max(msel2, axis=0, keepdims=True)
    idx2 = jnp.min(jnp.where(msel2 == m2, eid, float(N_EXPERTS)), axis=0, keepdims=True)
    w1 = jnp.sum(jnp.where(eid == idx1, aff, 0.0), axis=0, keepdims=True)
    w2 = jnp.sum(jnp.where(eid == idx2, aff, 0.0), axis=0, keepdims=True)
    wsum = w1 + w2
    return idx1.astype(jnp.int32), idx2.astype(jnp.int32), w1 / wsum, w2 / wsum


def _local_slots(idx1, idx2):
    n_tok = idx1.shape[1]
    eid = _iota((N_EXPERTS, n_tok), 0)
    hot1, hot2 = eid == idx1, eid == idx2
    hot = jnp.where(hot1, 1.0, jnp.where(hot2, 1.0, 0.0))
    earlier = jnp.where(_iota((n_tok, n_tok), 0) < _iota((n_tok, n_tok), 1), 1.0, 0.0).astype(BF16)
    before_in_expert = _dot(hot.astype(BF16), earlier)
    counts = jnp.sum(hot, axis=1, keepdims=True)
    lower = jnp.where(_iota((N_EXPERTS, N_EXPERTS), 1) < _iota((N_EXPERTS, N_EXPERTS), 0), 1.0, 0.0).astype(BF16)
    first_slot = _dot(lower, jnp.broadcast_to(counts, (N_EXPERTS, LANES)).astype(BF16))[:, 0:1]
    slot = before_in_expert + first_slot
    slot1 = jnp.sum(jnp.where(hot1, slot, 0.0), axis=0, keepdims=True)
    slot2 = jnp.sum(jnp.where(hot2, slot, 0.0), axis=0, keepdims=True)
    return slot1, slot2, counts


def _slot_one_hot(slot1, slot2, v1, v2):
    n_tok = slot1.shape[1]
    row = _iota((2 * n_tok, n_tok), 0).astype(F32)
    return jnp.where(row == slot1, v1, jnp.where(row == slot2, v2, 0.0))


N_MIXER_WEIGHTS = 15
CTX_SEQS_PER_STEP = 2
MAX_INLINE_BLOCKS = 2


def _mixer_kernel(n_tok, n_par, latent, n_alias, lam_init, *refs):
    it = iter(refs)
    x_ref, mod_ref = next(it), next(it)
    (n1_ref, n2_ref, win_ref, wout_ref, sw_ref, sb_ref, qg_ref, kg_ref, dl_ref, dg_ref,
     w2c_ref, gb_ref, gg_ref, rwt_ref, rb_ref) = (next(it) for _ in range(N_MIXER_WEIGHTS))
    if latent:
        ck_ref, cv_ref, st0_ref, cos_ref, sin_ref = (next(it) for _ in range(5))
    for _ in range(n_alias):
        next(it)
    xo_ref, hs_ref, slot_ref, wt_ref, cnt_ref = (next(it) for _ in range(5))
    if not latent:
        ko_ref, vo_ref, so_ref = (next(it) for _ in range(3))
    proj_ref, mix_ref, q_ref, k_ref, v_ref = (next(it) for _ in range(5))
    gq_ref, gke_ref, gv_ref, gr_ref, dec_ref, go_ref, st_ref = (next(it) for _ in range(7))

    n_blk = n_tok // ROW_BLOCK
    n_ctx = k_ref.shape[0] - n_par * n_tok
    n_keys = n_ctx + n_tok
    mod = mod_ref[0, 0]

    by_block = latent

    def whole_sequence(fn):
        def run():
            fn()

        if by_block:
            pl.when(pl.program_id(1) == 0)(run)
        else:
            run()

    def blocks(body):
        if n_par * n_blk <= MAX_INLINE_BLOCKS:
            for r in range(n_par * n_blk):
                body(r)
        else:
            def step(r, carry):
                body(r)
                return carry
            whole_sequence(lambda: lax.fori_loop(0, n_par * n_blk, step, 0))

    def aligned(start, size):
        return pl.ds(start if isinstance(start, int) else pl.multiple_of(start, size), size)

    def block_rows(r, offset=0):
        return aligned(offset + r * ROW_BLOCK, ROW_BLOCK)

    if not latent:
        for ref in (ko_ref, vo_ref, so_ref):
            for q in range(n_par):
                for other in range(1, ref.shape[1]):
                    ref[q, other] = jnp.zeros(ref.shape[2:], F32)

    lane_group = _iota((SGU_CHUNK, SGU_W), 1) // SGU_GROUP_W
    blk_r = _iota((ROW_BLOCK, ROW_BLOCK), 0)
    blk_c = _iota((ROW_BLOCK, ROW_BLOCK), 1)
    same_chunk = (blk_r // GLA_CHUNK) == (blk_c // GLA_CHUNK)
    tri = (jnp.where(same_chunk & (blk_c <= blk_r), 1.0, 0.0).astype(BF16),
           jnp.where(same_chunk & (blk_c >= blk_r), 1.0, 0.0).astype(BF16))
    chunks_per_blk = ROW_BLOCK // GLA_CHUNK
    head_of_lane = _iota((GLA_CHUNK, W_GLA), 1) // GLA_DK
    stack_r = _iota((GLA_HEADS * GLA_CHUNK, GLA_CHUNK), 0) % GLA_CHUNK
    stack_c = _iota((GLA_HEADS * GLA_CHUNK, GLA_CHUNK), 1)
    causal = (stack_c <= stack_r, stack_c >= stack_r)

    if latent:
        def cached_context():
            for h in range(DIFF_HEADS):
                k_ref[0:n_ctx, h * DIFF_V:(h + 1) * DIFF_V] = ck_ref[0, 0, h].astype(BF16)
                v_ref[0:n_ctx, h * DIFF_V:(h + 1) * DIFF_V] = cv_ref[0, 0, h].astype(BF16)
            st_ref[0] = st0_ref[0, 0]

        whole_sequence(cached_context)
        pair_lo = (_iota((ROW_BLOCK, W_QK), 1) % (DIFF_QK // 2)) < (DIFF_QK // 4)

        def rope(z, rows):
            cos = jnp.concatenate([cos_ref[rows, :]] * DIFF_HEADS, axis=-1)
            sin = jnp.concatenate([sin_ref[rows, :]] * DIFF_HEADS, axis=-1)
            shift = DIFF_QK // 4
            swapped = jnp.where(pair_lo, pltpu.roll(z, W_QK - shift, 1), pltpu.roll(z, shift, 1))
            return z * cos + swapped * sin

    def modulated_input(r):
        h = _row_rms(x_ref[block_rows(r), :]) * n1_ref[0]
        return (h * (1.0 + mod[1:2, :]) + mod[0:1, :]).astype(BF16)

    def spatial_gating(r):
        for c in range(ROW_BLOCK // SGU_CHUNK):
            local = slice(c * SGU_CHUNK, (c + 1) * SGU_CHUNK)
            u = jax.nn.gelu(proj_ref[local, C_AU:C_AU + SGU_W])
            v = _group_rms(jax.nn.gelu(proj_ref[local, C_AV:C_AV + SGU_W]), SGU_GROUP_W).astype(BF16)
            s = sb_ref[0]
            for g in range(SGU_GROUPS):
                s = s + jnp.where(lane_group == g, _dot(sw_ref[0, g], v), 0.0)
            mix_ref[aligned(r * ROW_BLOCK + c * SGU_CHUNK, SGU_CHUNK), M_A:M_A + SGU_W] = (u * s).astype(BF16)

    def attention_operands(r):
        rows = block_rows(r)
        key_rows = block_rows(r, n_ctx)
        seq, seq_rows = r // n_blk, block_rows(r % n_blk)
        qn = _group_rms(proj_ref[:, C_BQ:C_BQ + W_QK], DIFF_QK) * qg_ref[0]
        kn = _group_rms(proj_ref[:, C_BK:C_BK + W_QK], DIFF_QK) * kg_ref[0]
        vv = proj_ref[:, C_BV:C_BV + W_QK]
        if latent:
            qn, kn = rope(qn, rows), rope(kn, rows)
        else:
            for h in range(DIFF_HEADS):
                for i in range(2):
                    lo = h * DIFF_V + i * DIFF_QK
                    ko_ref[seq, 0, h, i, seq_rows, :] = kn[:, lo:lo + DIFF_QK]
                vo_ref[seq, 0, h, seq_rows, :] = vv[:, h * DIFF_V:(h + 1) * DIFF_V]
        q_ref[rows, :] = (qn * (DIFF_QK ** -0.5)).astype(BF16)
        k_ref[key_rows, :] = kn.astype(BF16)
        v_ref[key_rows, :] = vv.astype(BF16)

    def gla_operands(r):
        rows = block_rows(r)
        gpre = _dot(proj_ref[:, C_LR:C_LR + LANES].astype(BF16), w2c_ref[0]) + gb_ref[0]
        gate = _log_sigmoid(gpre) * (1.0 / GLA_GATE_NORM)
        gq = proj_ref[:, C_CQ:C_CQ + W_GLA] * (GLA_DK ** -0.5)
        gk = proj_ref[:, C_CK:C_CK + W_GLA]
        gv = proj_ref[:, C_CV:C_CV + W_GLA].astype(BF16)
        gv_ref[rows, :] = gv
        gr_ref[rows, :] = proj_ref[:, C_CR:C_CR + W_GLA]
        for d in range(2):
            g = gate[:, d * W_GLA:(d + 1) * W_GLA]
            b = sum(_dot(tri[d], p) for p in _split3(g))
            last = GLA_CHUNK - 1 if d == 0 else 0
            b_last = jnp.concatenate(
                [jnp.broadcast_to(b[c * GLA_CHUNK + last:c * GLA_CHUNK + last + 1, :], (GLA_CHUNK, W_GLA))
                 for c in range(chunks_per_blk)], axis=0)
            q_dec = (gq * jnp.exp(b)).astype(BF16)
            k_inv = (gk * jnp.exp(-b)).astype(BF16)
            gq_ref[d, rows, :] = q_dec
            gke_ref[d, rows, :] = (gk * jnp.exp(b_last - b)).astype(BF16)
            for c in range(chunks_per_blk):
                row = c * GLA_CHUNK + last
                dec_ref[d, r * chunks_per_blk + c] = jnp.exp(b[row:row + 1, :])
                chunk = slice(c * GLA_CHUNK, (c + 1) * GLA_CHUNK)
                qd = q_dec[chunk]
                q_stack = jnp.concatenate(
                    [jnp.where(head_of_lane == h, qd, jnp.zeros_like(qd)) for h in range(GLA_HEADS)], axis=0)
                attn = jnp.where(causal[d], _dot_nt(q_stack, k_inv[chunk]), 0.0)
                spread = _dot(attn.astype(BF16), gv[chunk])
                o = jnp.zeros((GLA_CHUNK, W_GLA), F32)
                for h in range(GLA_HEADS):
                    o = o + jnp.where(head_of_lane == h, spread[h * GLA_CHUNK:(h + 1) * GLA_CHUNK, :], 0.0)
                go_ref[d, aligned(r * ROW_BLOCK + c * GLA_CHUNK, GLA_CHUNK), :] = o

    def project_and_split(r):
        proj_ref[...] = _dot(modulated_input(r), win_ref[0])
        spatial_gating(r)
        attention_operands(r)
        gla_operands(r)

    blocks(project_and_split)

    dl = dl_ref[0]
    lam = (jnp.exp(jnp.sum(dl[0:1] * dl[1:2], axis=-1, keepdims=True))
           - jnp.exp(jnp.sum(dl[2:3] * dl[3:4], axis=-1, keepdims=True)) + lam_init)
    sub0 = (_iota((ROW_BLOCK, DIFF_V), 1) < DIFF_QK)

    def softmax(s):
        e = jnp.exp(s - jnp.max(s, axis=-1, keepdims=True))
        return e, jnp.sum(e, axis=-1, keepdims=True)

    def attn_block(r):
        rows = block_rows(r)
        keys = aligned((r // n_blk) * n_keys, n_keys)
        for h in range(DIFF_HEADS):
            cols = slice(h * DIFF_V, (h + 1) * DIFF_V)
            qh = q_ref[rows, cols]
            kh = k_ref[keys, cols]
            e0, z0 = softmax(_dot_nt(jnp.where(sub0, qh, jnp.zeros_like(qh)), kh))
            e1, z1 = softmax(_dot_nt(jnp.where(sub0, jnp.zeros_like(qh), qh), kh))
            w = e0 / z0 - lam * (e1 / z1)
            o = _dot(w.astype(BF16), v_ref[keys, cols])
            o = _row_rms(o) * dg_ref[0] * (1.0 - lam_init)
            mix_ref[rows, M_B + h * DIFF_V:M_B + (h + 1) * DIFF_V] = o.astype(BF16)

    blocks(attn_block)

    if not latent:
        st_ref[...] = jnp.zeros(st_ref.shape, F32)

    n_chunk = n_tok // GLA_CHUNK
    st_diag = (_iota((W_GLA, W_GLA), 0) // GLA_DV) == (_iota((W_GLA, W_GLA), 1) // GLA_DK)

    def gla_step(c, carry):
        for seq in range(n_par):
            for d in range(2):
                cc = seq * n_chunk + (c if d == 0 else n_chunk - 1 - c)
                rows = pl.ds(pl.multiple_of(cc * GLA_CHUNK, GLA_CHUNK), GLA_CHUNK)
                st = st_ref[seq, d]
                go_ref[d, rows, :] = go_ref[d, rows, :] + _dot_nt(gq_ref[d, rows, :], st.astype(BF16))
                upd = _dot_tn(gv_ref[rows, :], gke_ref[d, rows, :])
                st_ref[seq, d] = dec_ref[d, cc] * st + jnp.where(st_diag, upd, 0.0)
        return carry

    whole_sequence(lambda: lax.fori_loop(0, n_chunk, gla_step, 0))

    if not latent:
        for seq in range(n_par):
            for d in range(2):
                s_full = st_ref[seq, d].T
                for h in range(GLA_HEADS):
                    so_ref[seq, 0, d, h] = s_full[h * GLA_DK:(h + 1) * GLA_DK, h * GLA_DV:(h + 1) * GLA_DV]

    def finish_block(r, out_r):
        rows, out_rows = block_rows(r), block_rows(out_r)
        oc = _group_rms(go_ref[0, rows, :] + go_ref[1, rows, :], GLA_DV) * gg_ref[0]
        oc = oc * jax.nn.silu(gr_ref[rows, :])
        mix_ref[rows, M_C:M_C + W_GLA] = oc.astype(BF16)
        x1 = x_ref[rows, :] + mod[2:3, :] * _dot(mix_ref[rows, :], wout_ref[0])
        xo_ref[out_rows, :] = x1
        hn = _row_rms(x1) * n2_ref[0]
        hn = hn * (1.0 + mod[4:5, :]) + mod[3:4, :]
        idx1, idx2, w1, w2 = _route(hn, rwt_ref, rb_ref)
        slot1, slot2, counts = _local_slots(idx1, idx2)
        perm = _slot_one_hot(slot1, slot2, 1.0, 1.0).astype(BF16)
        _to_row_slabs(hs_ref, 2 * out_r * ROW_BLOCK, _dot(perm, hn.astype(BF16)))
        slot_ref[:, out_rows] = jnp.concatenate([slot1, slot2], axis=0).astype(jnp.int32)
        wt_ref[:, out_rows] = jnp.concatenate([w1, w2], axis=0)
        cnt_ref[out_r] = jnp.broadcast_to(counts, (N_EXPERTS, LANES)).astype(jnp.int32)

    if by_block:
        finish_block(pl.program_id(1), 0)
    else:
        for r in range(n_par * n_blk):
            finish_block(r, r)


def _mixer_call(l, n_tok, n_par, latent, x, mods_all, weights, extras, cache_bufs):
    n_seq = x.shape[0] // n_tok
    n_all = x.shape[0]
    assert n_seq % n_par == 0 and not (latent and n_par > 1)
    n_step_tok = n_par * n_tok
    n_keys = n_step_tok + (extras[0].shape[3] if latent else 0)
    n_chunk = n_step_tok // GLA_CHUNK
    lam_init = 0.8 - 0.6 * math.exp(-0.3 * l)

    single = pl.Buffered(1)
    seq_mode = single if latent else None

    def layer(arr):
        tail = arr.shape[1:]
        return pl.BlockSpec((1,) + tail, lambda s, *_r, _n=len(tail): (l,) + (0,) * _n, pipeline_mode=single)

    def const(arr):
        return pl.BlockSpec(arr.shape, lambda s, *_r, _n=arr.ndim: (0,) * _n, pipeline_mode=single)

    def tok_spec(width):
        return pl.BlockSpec((n_step_tok, width), lambda s, *_r: (s, 0), pipeline_mode=seq_mode)

    mod_row = (lambda s: 1 + s) if latent else (lambda s: 0)
    in_specs = [tok_spec(D_MODEL),
                pl.BlockSpec((1, 1, 6, D_MODEL), lambda s, *_r: (l, mod_row(s), 0, 0))]
    in_specs += [layer(w) for w in weights[:N_MIXER_WEIGHTS - 2]] + [const(w) for w in weights[-2:]]
    operands = [x, mods_all] + list(weights)
    if latent:
        ck, cv, st0, cos, sin = extras
        in_specs += [
            pl.BlockSpec((1, 1) + ck.shape[2:], lambda s, *_r: (s, l, 0, 0, 0)),
            pl.BlockSpec((1, 1) + cv.shape[2:], lambda s, *_r: (s, l, 0, 0, 0)),
            pl.BlockSpec((1, 1) + st0.shape[2:], lambda s, *_r: (s, l, 0, 0, 0)),
            const(cos), const(sin),
        ]
        operands += [ck, cv, st0, cos, sin]
    n_in = len(operands)
    in_specs += [pl.BlockSpec(memory_space=pl.ANY)] * len(cache_bufs)
    operands += list(cache_bufs)

    tiles_per_step = n_step_tok // ROW_BLOCK
    out_shape = [
        jax.ShapeDtypeStruct((n_all, D_MODEL), F32),
        jax.ShapeDtypeStruct((2 * n_all * ROW_SLABS, LANES), F32),
        jax.ShapeDtypeStruct((2, n_all), jnp.int32),
        jax.ShapeDtypeStruct((2, n_all), F32),
        jax.ShapeDtypeStruct((n_all // ROW_BLOCK, N_EXPERTS, LANES), jnp.int32),
    ]
    if latent:
        grid = (n_seq, tiles_per_step)
        out_tok, out_tiles = ROW_BLOCK, 1
        at = lambda s, r: s * tiles_per_step + r
    else:
        grid = (n_seq // n_par,)
        out_tok, out_tiles = n_step_tok, tiles_per_step
        at = lambda s: s
    out_specs = [
        pl.BlockSpec((out_tok, D_MODEL), lambda *g: (at(*g), 0)),
        pl.BlockSpec((2 * out_tok * ROW_SLABS, LANES), lambda *g: (at(*g), 0)),
        pl.BlockSpec((2, out_tok), lambda *g: (0, at(*g))),
        pl.BlockSpec((2, out_tok), lambda *g: (0, at(*g))),
        pl.BlockSpec((out_tiles, N_EXPERTS, LANES), lambda *g: (at(*g), 0, 0)),
    ]
    n_shared_out = len(out_shape)
    aliases = {}
    if not latent:
        out_shape += [
            jax.ShapeDtypeStruct((n_seq, DEPTH, DIFF_HEADS, 2, n_tok, DIFF_QK), F32),
            jax.ShapeDtypeStruct((n_seq, DEPTH, DIFF_HEADS, n_tok, DIFF_V), F32),
            jax.ShapeDtypeStruct((n_seq, DEPTH, 2, GLA_HEADS, GLA_DK, GLA_DV), F32),
        ]
        n_lay, lay = (1, l) if cache_bufs else (DEPTH, 0)
        out_specs += [
            pl.BlockSpec((n_par, n_lay, DIFF_HEADS, 2, n_tok, DIFF_QK), lambda s: (s, lay, 0, 0, 0, 0)),
            pl.BlockSpec((n_par, n_lay, DIFF_HEADS, n_tok, DIFF_V), lambda s: (s, lay, 0, 0, 0)),
            pl.BlockSpec((n_par, n_lay, 2, GLA_HEADS, GLA_DK, GLA_DV), lambda s: (s, lay, 0, 0, 0, 0)),
        ]
        aliases = {n_in + j: n_shared_out + j for j in range(len(cache_bufs))}
    scratch = [
        pltpu.VMEM((ROW_BLOCK, D_PROJ_PAD), F32),
        pltpu.VMEM((n_step_tok, D_MODEL), BF16),
        pltpu.VMEM((n_step_tok, W_QK), BF16),
        pltpu.VMEM((n_keys, W_QK), BF16),
        pltpu.VMEM((n_keys, W_QK), BF16),
        pltpu.VMEM((2, n_step_tok, W_GLA), BF16),
        pltpu.VMEM((2, n_step_tok, W_GLA), BF16),
        pltpu.VMEM((n_step_tok, W_GLA), BF16),
        pltpu.VMEM((n_step_tok, W_GLA), F32),
        pltpu.VMEM((2, n_chunk, 1, W_GLA), F32),
        pltpu.VMEM((2, n_step_tok, W_GLA), F32),
        pltpu.VMEM((n_par, 2, W_GLA, W_GLA), F32),
    ]
    return pl.pallas_call(
        functools.partial(_mixer_kernel, n_tok, n_par, latent, len(cache_bufs), lam_init),
        grid=grid,
        in_specs=in_specs,
        out_specs=out_specs,
        out_shape=out_shape,
        scratch_shapes=scratch,
        input_output_aliases=aliases,
        compiler_params=pltpu.CompilerParams(
            dimension_semantics=("arbitrary",) * len(grid), vmem_limit_bytes=MIXER_VMEM_LIMIT),
        name="mixer_latent" if latent else "mixer_context",
    )(*operands)


PAIR_BLOCK = 2 * ROW_BLOCK
COPY_SIZES = tuple(ROW_BLOCK >> k for k in range(ROW_BLOCK.bit_length()))
LARGE_COPY = 64
GATHER_AHEAD = 2
GATHER_SLOTS = GATHER_AHEAD + 1


def _segment_copies(n_rows, make_copy, act):
    def copy_if_set(size):
        @pl.when((n_rows & size) != 0)
        def _():
            act(make_copy(n_rows & (-2 * size), size))

    n_large = COPY_SIZES.index(LARGE_COPY) + 1

    @pl.when(n_rows >= LARGE_COPY)
    def _():
        for size in COPY_SIZES[:n_large]:
            copy_if_set(size)

    for size in COPY_SIZES[n_large:]:
        copy_if_set(size)


def _start(copy):
    copy.start()


def _wait(copy):
    copy.wait()


def _slab_rows(first_row, n_rows, slab):
    return pl.ds(first_row * ROW_SLABS + slab, n_rows, stride=ROW_SLABS)


def _to_row_slabs(ref, first_row, value):
    for s in range(ROW_SLABS):
        ref[_slab_rows(first_row, value.shape[0], s), :] = value[:, s * LANES:(s + 1) * LANES]


def _from_row_slabs(ref, first_row, n_rows):
    return jnp.concatenate([ref[_slab_rows(first_row, n_rows, s), :] for s in range(ROW_SLABS)], axis=-1)


SLAB_PAIR_W = 2 * LANES
N_SLAB_PAIRS = ROW_SLABS // 2


def _slab_pair_cols(g):
    return slice(g * SLAB_PAIR_W, (g + 1) * SLAB_PAIR_W)


def _load_slab_pair(ref, first_row, n_rows, g):
    return jnp.concatenate([ref[_slab_rows(first_row, n_rows, s), :] for s in (2 * g, 2 * g + 1)], axis=-1)


def _row_span(ref, first_row, n_rows):
    return ref.at[pl.ds(pl.multiple_of(first_row * ROW_SLABS, ROW_SLABS), n_rows * ROW_SLABS)]


def _two_streams(n_first_tiles):
    def first(i, *_):
        return (jnp.minimum(i, n_first_tiles - 1), 0)

    def second(i, *_):
        return (jnp.maximum(i - n_first_tiles, 0), 0)

    return first, second


N_EXPERT_TABLES = 10
OUT_SLOTS = 4


def _expert_kernel(n_ctx_tiles, n_tiles_max, te_ref, first_ref, rows_ref, jlo_ref, jhi_ref, cpre_ref, cnt_ref,
                   lofs_ref, tile0_ref, ntile_ref, hs_c_ref, hs_l_ref, w1_ref, w3_ref, w2_ref, ys_ref,
                   xbuf_ref, obuf_ref, w1b_ref, w3b_ref, w2b_ref, sem, out_sem):
    expert = pl.program_id(0)
    n_tiles = tile0_ref[N_EXPERTS]

    def gather(t, act):
        slot = t % GATHER_SLOTS
        e, first = te_ref[t], first_ref[t]
        last = first + rows_ref[t]

        def segment_of(hs_ref, first_tile):
            def body(j, carry):
                k = j * N_EXPERTS + e
                seg_first = cpre_ref[k]
                lo = jnp.maximum(seg_first, first)
                n = jnp.maximum(jnp.minimum(seg_first + cnt_ref[k], last) - lo, 0)
                src = (j - first_tile) * PAIR_BLOCK + lofs_ref[k] + (lo - seg_first)
                dst = slot * ROW_BLOCK + lo - first
                _segment_copies(n, lambda done, size: pltpu.make_async_copy(
                    _row_span(hs_ref, src + done, size), _row_span(xbuf_ref, dst + done, size), sem.at[slot]), act)
                return carry
            return body

        jlo, jhi = jlo_ref[t], jhi_ref[t]
        lax.fori_loop(jnp.minimum(jlo, n_ctx_tiles), jnp.minimum(jhi, n_ctx_tiles), segment_of(hs_c_ref, 0), 0)
        lax.fori_loop(jnp.maximum(jlo, n_ctx_tiles), jnp.maximum(jhi, n_ctx_tiles),
                      segment_of(hs_l_ref, n_ctx_tiles), 0)

    def out_copy(t, oslot):
        return pltpu.make_async_copy(
            _row_span(obuf_ref, oslot * ROW_BLOCK, ROW_BLOCK), _row_span(ys_ref, t * ROW_BLOCK, ROW_BLOCK),
            out_sem.at[oslot])

    @pl.when(expert == 0)
    def _():
        xbuf_ref[...] = jnp.zeros(xbuf_ref.shape, F32)
        for t in range(GATHER_AHEAD):
            gather(t, _start)

    w1b_ref[...] = w1_ref[0, 0].astype(BF16)
    w3b_ref[...] = w3_ref[0, 0].astype(BF16)
    w2b_ref[...] = w2_ref[0, 0].astype(BF16)
    tile0, n_own = tile0_ref[expert], ntile_ref[expert]

    def tile_body(k, carry):
        t = tile0 + k
        slot, oslot = t % GATHER_SLOTS, t % OUT_SLOTS

        @pl.when(t + GATHER_AHEAD < n_tiles)
        def _():
            gather(t + GATHER_AHEAD, _start)

        n_rows = rows_ref[t]
        _segment_copies(n_rows, lambda done, size: pltpu.make_async_copy(
            _row_span(hs_c_ref, done, size), _row_span(xbuf_ref, slot * ROW_BLOCK + done, size), sem.at[slot]), _wait)

        @pl.when(t >= OUT_SLOTS)
        def _():
            out_copy(t, oslot).wait()

        live = _iota((ROW_BLOCK, D_MODEL), 0) < n_rows
        x = jnp.where(live, _from_row_slabs(xbuf_ref, slot * ROW_BLOCK, ROW_BLOCK), 0.0).astype(BF16)
        hid = jax.nn.silu(_dot(x, w1b_ref[...])) * _dot(x, w3b_ref[...])
        _to_row_slabs(obuf_ref, oslot * ROW_BLOCK, _dot(hid.astype(BF16), w2b_ref[...]))
        out_copy(t, oslot).start()
        return carry

    lax.fori_loop(0, n_own, tile_body, 0)

    @pl.when(expert == N_EXPERTS - 1)
    def _():
        for oslot in range(OUT_SLOTS):
            @pl.when(n_tiles > oslot)
            def _():
                out_copy(0, oslot).wait()
        obuf_ref[...] = jnp.zeros(obuf_ref.shape, F32)

        def fill(t, carry):
            out_copy(t, 0).start()
            out_copy(t, 0).wait()
            return carry

        lax.fori_loop(n_tiles, n_tiles_max, fill, 0)


def _expert_call(l, plan, hs_c, hs_l, w1, w3, w2):
    tables = plan["expert_tables"]
    n_tiles_max = tables[0].shape[0]
    n_ctx_tiles = hs_c.shape[0] // (PAIR_BLOCK * ROW_SLABS)

    def weight(shape):
        return pl.BlockSpec((1, 1) + shape, lambda e, *_: (l, e, 0, 0))

    return pl.pallas_call(
        functools.partial(_expert_kernel, n_ctx_tiles, n_tiles_max),
        grid_spec=pltpu.PrefetchScalarGridSpec(
            num_scalar_prefetch=N_EXPERT_TABLES,
            grid=(N_EXPERTS,),
            in_specs=[pl.BlockSpec(memory_space=pl.ANY), pl.BlockSpec(memory_space=pl.ANY),
                      weight((D_MODEL, D_EXPERT)), weight((D_MODEL, D_EXPERT)), weight((D_EXPERT, D_MODEL))],
            out_specs=pl.BlockSpec(memory_space=pl.ANY),
            scratch_shapes=[pltpu.VMEM((GATHER_SLOTS * ROW_BLOCK * ROW_SLABS, LANES), F32),
                            pltpu.VMEM((OUT_SLOTS * ROW_BLOCK * ROW_SLABS, LANES), F32),
                            pltpu.VMEM((D_MODEL, D_EXPERT), BF16), pltpu.VMEM((D_MODEL, D_EXPERT), BF16),
                            pltpu.VMEM((D_EXPERT, D_MODEL), BF16),
                            pltpu.SemaphoreType.DMA((GATHER_SLOTS,)), pltpu.SemaphoreType.DMA((OUT_SLOTS,))],
        ),
        out_shape=jax.ShapeDtypeStruct((n_tiles_max * ROW_BLOCK * ROW_SLABS, LANES), F32),
        compiler_params=pltpu.CompilerParams(
            dimension_semantics=("arbitrary",), vmem_limit_bytes=SMALL_KERNEL_VMEM_LIMIT),
        name="moe_experts",
    )(*tables, hs_c, hs_l, w1, w3, w2)


N_COMBINE_TABLES = 4
COMBINE_TILES_PER_STEP = 2


def _combine_kernel(n_ctx_steps, cnt_ref, cpre_ref, lofs_ref, starts_ref, x_c_ref, x_l_ref, slot_c_ref, slot_l_ref,
                    wt_c_ref, wt_l_ref, mod_ref, ys_ref, xo_c_ref, xo_l_ref, buf_ref, sem):
    step = pl.program_id(0)
    n_tiles = pl.num_programs(0) * COMBINE_TILES_PER_STEP

    def collect(t, act):
        slot = t % GATHER_SLOTS

        def body(e, carry):
            k = t * N_EXPERTS + e
            src, dst = starts_ref[e] + cpre_ref[k], slot * PAIR_BLOCK + lofs_ref[k]
            _segment_copies(cnt_ref[k], lambda done, size: pltpu.make_async_copy(
                _row_span(ys_ref, src + done, size), _row_span(buf_ref, dst + done, size), sem.at[slot]), act)
            return carry

        lax.fori_loop(0, N_EXPERTS, body, 0)

    @pl.when(step == 0)
    def _():
        for t in range(GATHER_AHEAD):
            collect(t, _start)

    gate = mod_ref[0, 0, 5:6, :]

    def finish(part, first_row, x_ref, slot_ref, wt_ref, xo_ref):
        tokens = pl.ds(part * ROW_BLOCK, ROW_BLOCK)
        slots, wts = slot_ref[:, tokens].astype(F32), wt_ref[:, tokens]
        slot1, slot2 = slots[0:1], slots[1:2]
        weight_of_row = jnp.sum(_slot_one_hot(slot1, slot2, wts[0:1], wts[1:2]), axis=1, keepdims=True)
        gather_rows = _slot_one_hot(slot1, slot2, 1.0, 1.0).T.astype(BF16)
        for g in range(N_SLAB_PAIRS):
            cols = _slab_pair_cols(g)
            hi, lo = _split2(_load_slab_pair(buf_ref, first_row, PAIR_BLOCK, g) * weight_of_row)
            y = _dot(gather_rows, hi) + _dot(gather_rows, lo)
            xo_ref[tokens, cols] = x_ref[tokens, cols] + gate[:, cols] * y

    for part in range(COMBINE_TILES_PER_STEP):
        j = step * COMBINE_TILES_PER_STEP + part

        @pl.when(j + GATHER_AHEAD < n_tiles)
        def _():
            collect(j + GATHER_AHEAD, _start)

        slot = j % GATHER_SLOTS
        for piece in range(PAIR_BLOCK // ROW_BLOCK):
            pltpu.make_async_copy(
                _row_span(ys_ref, piece * ROW_BLOCK, ROW_BLOCK),
                _row_span(buf_ref, slot * PAIR_BLOCK + piece * ROW_BLOCK, ROW_BLOCK), sem.at[slot]).wait()
        @pl.when(step < n_ctx_steps)
        def _():
            finish(part, slot * PAIR_BLOCK, x_c_ref, slot_c_ref, wt_c_ref, xo_c_ref)

        @pl.when(step >= n_ctx_steps)
        def _():
            finish(part, slot * PAIR_BLOCK, x_l_ref, slot_l_ref, wt_l_ref, xo_l_ref)


def _combine_call(l, plan, x_c, x_l, slot_c, slot_l, wt_c, wt_l, mods_all, mod_row_of_tile, ys):
    step_rows = COMBINE_TILES_PER_STEP * ROW_BLOCK
    n_steps = (x_c.shape[0] + x_l.shape[0]) // step_rows
    n_ctx_steps = x_c.shape[0] // step_rows
    first, second = _two_streams(n_ctx_steps)

    def lanes(index_map):
        return lambda i, *_: index_map(i)[::-1]

    return pl.pallas_call(
        functools.partial(_combine_kernel, n_ctx_steps),
        grid_spec=pltpu.PrefetchScalarGridSpec(
            num_scalar_prefetch=N_COMBINE_TABLES,
            grid=(n_steps,),
            in_specs=[pl.BlockSpec((step_rows, D_MODEL), first),
                      pl.BlockSpec((step_rows, D_MODEL), second),
                      pl.BlockSpec((2, step_rows), lanes(first)),
                      pl.BlockSpec((2, step_rows), lanes(second)),
                      pl.BlockSpec((2, step_rows), lanes(first)),
                      pl.BlockSpec((2, step_rows), lanes(second)),
                      pl.BlockSpec((1, 1, 6, D_MODEL),
                                   lambda i, *_: (l, mod_row_of_tile(i * COMBINE_TILES_PER_STEP), 0, 0)),
                      pl.BlockSpec(memory_space=pl.ANY)],
            out_specs=[pl.BlockSpec((step_rows, D_MODEL), first),
                       pl.BlockSpec((step_rows, D_MODEL), second)],
            scratch_shapes=[pltpu.VMEM((GATHER_SLOTS * PAIR_BLOCK * ROW_SLABS, LANES), F32),
                            pltpu.SemaphoreType.DMA((GATHER_SLOTS,))],
        ),
        out_shape=[jax.ShapeDtypeStruct(x_c.shape, F32), jax.ShapeDtypeStruct(x_l.shape, F32)],
        compiler_params=pltpu.CompilerParams(
            dimension_semantics=("arbitrary",), vmem_limit_bytes=SMALL_KERNEL_VMEM_LIMIT),
        name="moe_combine",
    )(*plan["combine_tables"], x_c, x_l, slot_c, slot_l, wt_c, wt_l, mods_all, ys)


def _moe_plan(cnt):
    n_tok_tiles = cnt.shape[0]
    n_tiles = n_tok_tiles * PAIR_BLOCK // ROW_BLOCK + N_EXPERTS
    lofs = jnp.cumsum(cnt, axis=1) - cnt
    cpre = jnp.cumsum(cnt, axis=0) - cnt
    counts = jnp.sum(cnt, axis=0)
    padded = (counts + ROW_BLOCK - 1) // ROW_BLOCK * ROW_BLOCK
    ends = jnp.cumsum(padded)
    starts = ends - padded
    tile_start = jnp.arange(n_tiles, dtype=jnp.int32) * ROW_BLOCK
    tile_expert = jnp.minimum(
        jnp.sum((tile_start[:, None] >= ends[None, :]).astype(jnp.int32), axis=1), N_EXPERTS - 1)
    hot = tile_expert[:, None] == jnp.arange(N_EXPERTS, dtype=jnp.int32)[None, :]
    first = tile_start - jnp.sum(jnp.where(hot, starts[None, :], 0), axis=1)
    rows = jnp.clip(jnp.sum(jnp.where(hot, counts[None, :], 0), axis=1) - first, 0, ROW_BLOCK)
    seg_first = jnp.sum(jnp.where(hot[:, None, :], cpre[None, :, :], 0), axis=2)
    seg_rows = jnp.sum(jnp.where(hot[:, None, :], cnt[None, :, :], 0), axis=2)
    overlap = (seg_first < (first + rows)[:, None]) & (seg_first + seg_rows > first[:, None])
    j = jnp.arange(n_tok_tiles, dtype=jnp.int32)[None, :]
    jlo = jnp.min(jnp.where(overlap, j, n_tok_tiles), axis=1)
    jhi = jnp.max(jnp.where(overlap, j + 1, 0), axis=1)
    i32 = lambda a: a.astype(jnp.int32).reshape(-1)
    tile0 = jnp.concatenate([starts, ends[-1:]]) // ROW_BLOCK
    return {
        "expert_tables": tuple(i32(a) for a in (tile_expert, first, rows, jlo, jhi, cpre, cnt, lofs,
                                                tile0, padded // ROW_BLOCK)),
        "combine_tables": tuple(i32(a) for a in (cnt, cpre, lofs, starts)),
    }


def _rope_tables(n_tok):
    n_rows = n_tok // GRID_W
    pos_r = jnp.repeat(jnp.arange(n_rows), GRID_W)
    pos_c = jnp.tile(jnp.arange(GRID_W), n_rows)
    half = DIFF_QK // 2
    nf = half // 2
    freqs = ROPE_BASE ** (-jnp.arange(nf, dtype=F32) / nf)

    def tables(pos):
        ang = pos.astype(F32)[:, None] * freqs
        cos, sin = jnp.cos(ang), jnp.sin(ang)
        return jnp.concatenate([cos, cos], axis=-1), jnp.concatenate([-sin, sin], axis=-1)

    cos_r, sin_r = tables(pos_r)
    cos_c, sin_c = tables(pos_c)
    cos = jnp.concatenate([cos_r, cos_c], axis=-1)
    sin = jnp.concatenate([sin_r, sin_c], axis=-1)
    return jnp.concatenate([cos, cos], axis=-1), jnp.concatenate([sin, sin], axis=-1)


def _mixer_weights(w_in, w_out, sgu_w, sgu_b, q_norm_g, k_norm_g, diff_lambda, diff_norm_g, gla_w2, gla_b,
                   gla_norm_g, norm1_g, norm2_g, router_w, router_bias):
    w_in_pad = jnp.concatenate(
        [w_in.astype(BF16), jnp.zeros(w_in.shape[:2] + (D_PROJ_PAD - w_in.shape[2],), BF16)], axis=-1)
    w2cat = jnp.zeros((DEPTH, LANES, 2 * W_GLA), F32)
    w2cat = w2cat.at[:, 0:GLA_RANK, 0:W_GLA].set(gla_w2[:, 0]).at[:, GLA_RANK:2 * GLA_RANK, W_GLA:].set(gla_w2[:, 1])
    return (
        norm1_g[:, None, :], norm2_g[:, None, :], w_in_pad, w_out.astype(BF16),
        sgu_w.astype(BF16), jnp.repeat(sgu_b.transpose(0, 2, 1), SGU_GROUP_W, axis=2),
        jnp.tile(q_norm_g, (1, W_QK // DIFF_QK))[:, None, :], jnp.tile(k_norm_g, (1, W_QK // DIFF_QK))[:, None, :],
        diff_lambda, diff_norm_g[:, None, :],
        w2cat.astype(BF16), gla_b.reshape(DEPTH, 1, 2 * W_GLA), jnp.tile(gla_norm_g, (1, GLA_HEADS))[:, None, :],
        router_w.T, router_bias[:, None],
    )


def kernel(x_prompt, x_sample, cache_k, cache_v, state_gla, c, c_ctx, w_in, w_out, sgu_w, sgu_b, q_norm_g, k_norm_g,
           diff_lambda, diff_norm_g, gla_w2, gla_b, gla_norm_g, norm1_g, norm2_g, ada_w, ada_b, router_w, router_bias,
           moe_w1, moe_w3, moe_w2):
    n_ctx_seq, ctx_len, _ = x_prompt.shape
    n_lat_seq, lat_len, _ = x_sample.shape
    n_ctx_tok = n_ctx_seq * ctx_len
    n_lat_tok = n_lat_seq * lat_len
    ctx_tiles = n_ctx_tok // ROW_BLOCK
    lat_tiles_per_seq = lat_len // ROW_BLOCK

    n_cond = 1 + n_lat_seq
    cond_t = jnp.zeros((D_MODEL, SUBLANES), F32).at[:, 0].set(c_ctx).at[:, 1:n_cond].set(c.T)
    mods_all = _adaln_call(cond_t, n_cond, ada_w, ada_b)[:, :n_cond].reshape(DEPTH, n_cond, 6, D_MODEL)
    weights = _mixer_weights(w_in, w_out, sgu_w, sgu_b, q_norm_g, k_norm_g, diff_lambda, diff_norm_g, gla_w2, gla_b,
                             gla_norm_g, norm1_g, norm2_g, router_w, router_bias)

    ck_all = cache_k.transpose(0, 1, 2, 4, 3, 5).reshape(cache_k.shape[:3] + (cache_k.shape[4], DIFF_V))
    st_all = jnp.einsum('bldhkv,hg->bldhvgk', state_gla, jnp.eye(GLA_HEADS, dtype=F32)).reshape(
        n_lat_seq, DEPTH, 2, W_GLA, W_GLA)
    cos, sin = _rope_tables(lat_len)
    extras = (ck_all, cache_v, st_all, cos, sin)

    def mod_row_of_tile(i):
        return jnp.where(i < ctx_tiles, 0, 1 + (i - ctx_tiles) // lat_tiles_per_seq)

    x_c = x_prompt.reshape(n_ctx_tok, D_MODEL)
    x_l = x_sample.reshape(n_lat_tok, D_MODEL)
    cache_bufs = ()
    for l in range(DEPTH):
        ctx_par = 1 if l == 0 else CTX_SEQS_PER_STEP
        x1_c, hs_c, slot_c, wt_c, cnt_c, *cache_bufs = _mixer_call(
            l, ctx_len, ctx_par, False, x_c, mods_all, weights, None, tuple(cache_bufs))
        x1_l, hs_l, slot_l, wt_l, cnt_l = _mixer_call(l, lat_len, 1, True, x_l, mods_all, weights, extras, ())
        plan = _moe_plan(jnp.concatenate([cnt_c[:, :, 0], cnt_l[:, :, 0]], axis=0))
        ys = _expert_call(l, plan, hs_c, hs_l, moe_w1, moe_w3, moe_w2)
        x_c, x_l = _combine_call(l, plan, x1_c, x1_l, slot_c, slot_l, wt_c, wt_l, mods_all, mod_row_of_tile, ys)

    new_k, new_v, new_s = cache_bufs
    return (x_c.reshape(x_prompt.shape), x_l.reshape(x_sample.shape), new_k, new_v, new_s)
```

```python
import functools
import math

import jax
import jax.numpy as jnp
from jax import lax
from jax.experimental import pallas as pl
from jax.experimental.pallas import tpu as pltpu

F32 = jnp.float32
BF16 = jnp.bfloat16

D_MODEL = 1024
DEPTH = 4
GRID_W = 64
SGU_GROUPS = 4
SGU_GROUP_W = 64
SGU_W = SGU_GROUPS * SGU_GROUP_W
SGU_CHUNK = 128
DIFF_HEADS = 4
DIFF_QK = 64
DIFF_V = 2 * DIFF_QK
ROPE_BASE = 10000.0
GLA_HEADS = 4
GLA_DK = 64
GLA_DV = 64
GLA_RANK = 16
GLA_GATE_NORM = 16.0
GLA_CHUNK = 64
N_EXPERTS = 16
N_GROUPS = 4
EXPERTS_PER_GROUP = N_EXPERTS // N_GROUPS
D_EXPERT = 512
EPS = 1e-6

LANES = 128
SUBLANES = 8
MXU_DIM = 256
V7X_VMEM_BYTES = 64 * 1024 * 1024
MIXER_VMEM_LIMIT = V7X_VMEM_BYTES * 7 // 8
SMALL_KERNEL_VMEM_LIMIT = V7X_VMEM_BYTES * 5 // 8

ROW_BLOCK = MXU_DIM
ROW_SLABS = D_MODEL // LANES

W_QK = DIFF_HEADS * 2 * DIFF_QK
W_GLA = GLA_HEADS * GLA_DK
C_AU, C_AV = 0, SGU_W
C_BQ = C_AV + SGU_W
C_BK, C_BV = C_BQ + W_QK, C_BQ + 2 * W_QK
C_CQ = C_BV + DIFF_HEADS * DIFF_V
C_CK, C_CV, C_CR, C_LR = C_CQ + W_GLA, C_CQ + 2 * W_GLA, C_CQ + 3 * W_GLA, C_CQ + 4 * W_GLA
D_PROJ_MAIN = C_LR
D_PROJ_PAD = D_PROJ_MAIN + LANES
M_A, M_B, M_C = 0, SGU_W, SGU_W + DIFF_HEADS * DIFF_V


def _split2(x):
    hi = x.astype(BF16)
    lo = (x - hi.astype(F32)).astype(BF16)
    return hi, lo


def _split3(x):
    hi = x.astype(BF16)
    r = x - hi.astype(F32)
    mid = r.astype(BF16)
    lo = (r - mid.astype(F32)).astype(BF16)
    return hi, mid, lo


def _dot(a, b):
    return jnp.dot(a, b, preferred_element_type=F32)


def _dot_nt(a, b):
    return lax.dot_general(a, b, (((1,), (1,)), ((), ())), preferred_element_type=F32)


def _dot_tn(a, b):
    return lax.dot_general(a, b, (((0,), (0,)), ((), ())), preferred_element_type=F32)


def _iota(shape, dim):
    return lax.broadcasted_iota(jnp.int32, shape, dim)


def _block_ones(width, block):
    r = _iota((width, width), 0) // block
    c = _iota((width, width), 1) // block
    return (r == c)


def _group_sum(z, block):
    width = z.shape[-1]
    outs = []
    for s in range(0, width, MXU_DIM):
        w = min(MXU_DIM, width - s)
        ones = _block_ones(w, block).astype(BF16)
        hi, lo = _split2(z[:, s:s + w])
        outs.append(_dot(hi, ones) + _dot(lo, ones))
    return outs[0] if len(outs) == 1 else jnp.concatenate(outs, axis=-1)


def _group_rms(z, block):
    ms = _group_sum(z * z, block) * (1.0 / block)
    return z * lax.rsqrt(ms + EPS)


def _row_rms(z):
    return z * lax.rsqrt(jnp.mean(z * z, axis=-1, keepdims=True) + EPS)


def _log_sigmoid(x):
    return jnp.minimum(x, 0.0) - jnp.log(1.0 + jnp.exp(-jnp.abs(x)))


ADA_COLS = 1536


def _adaln_kernel(n_cond, cond_t_ref, w_ref, b_ref, o_ref):
    sc = jax.nn.silu(cond_t_ref[...])
    w = w_ref[0]
    rows = [jnp.sum(sc[:, r:r + 1] * w, axis=0, keepdims=True) + b_ref[0] for r in range(n_cond)]
    o_ref[0] = jnp.concatenate(rows + [jnp.zeros((SUBLANES - n_cond, w.shape[1]), F32)], axis=0)


def _adaln_call(cond_t, n_cond, ada_w, ada_b):
    n_col = 6 * D_MODEL // ADA_COLS
    return pl.pallas_call(
        functools.partial(_adaln_kernel, n_cond),
        grid=(DEPTH, n_col),
        in_specs=[
            pl.BlockSpec((D_MODEL, SUBLANES), lambda l, j: (0, 0)),
            pl.BlockSpec((1, D_MODEL, ADA_COLS), lambda l, j: (l, 0, j)),
            pl.BlockSpec((1, 1, ADA_COLS), lambda l, j: (l, 0, j)),
        ],
        out_specs=pl.BlockSpec((1, SUBLANES, ADA_COLS), lambda l, j: (l, 0, j)),
        out_shape=jax.ShapeDtypeStruct((DEPTH, SUBLANES, 6 * D_MODEL), F32),
        compiler_params=pltpu.CompilerParams(
            dimension_semantics=("arbitrary", "arbitrary"), vmem_limit_bytes=SMALL_KERNEL_VMEM_LIMIT),
        name="adaln",
    )(cond_t, ada_w, ada_b.reshape(DEPTH, 1, 6 * D_MODEL))


def _route(hn, rwt_ref, rb_ref):
    h_hi, h_lo = _split2(hn)
    rw = rwt_ref[...]
    rw_hi = rw.astype(BF16)
    rw_lo = (rw - rw_hi.astype(F32)).astype(BF16)
    logits = _dot_nt(rw_hi, h_hi) + _dot_nt(rw_hi, h_lo) + _dot_nt(rw_lo, h_hi)
    aff = jax.nn.sigmoid(logits)
    sel = aff + rb_ref[...]
    n_tok = sel.shape[1]

    def top2_sum(a, b, c, d):
        hi1, lo1 = jnp.maximum(a, b), jnp.minimum(a, b)
        hi2, lo2 = jnp.maximum(c, d), jnp.minimum(c, d)
        return jnp.maximum(hi1, hi2) + jnp.maximum(jnp.minimum(hi1, hi2), jnp.maximum(lo1, lo2))

    scores = []
    for g in range(N_GROUPS):
        rows = [sel[EXPERTS_PER_GROUP * g + j:EXPERTS_PER_GROUP * g + j + 1, :] for j in range(EXPERTS_PER_GROUP)]
        scores.append(top2_sum(*rows))
    best = jnp.zeros((1, n_tok), jnp.int32)
    best_score = scores[0]
    for g in range(1, N_GROUPS):
        upd = scores[g] > best_score
        best = jnp.where(upd, g, best)
        best_score = jnp.where(upd, scores[g], best_score)

    eid_i = _iota((N_EXPERTS, n_tok), 0)
    eid = eid_i.astype(F32)
    neg = jnp.float32(-jnp.inf)
    msel = jnp.where(eid_i // EXPERTS_PER_GROUP == best, sel, neg)
    m1 = jnp.max(msel, axis=0, keepdims=True)
    idx1 = jnp.min(jnp.where(msel == m1, eid, float(N_EXPERTS)), axis=0, keepdims=True)
    msel2 = jnp.where(eid == idx1, neg, msel)
    m2 = jnp.max(msel2, axis=0, keepdims=True)
    idx2 = jnp.min(jnp.where(msel2 == m2, eid, float(N_EXPERTS)), axis=0, keepdims=True)
    w1 = jnp.sum(jnp.where(eid == idx1, aff, 0.0), axis=0, keepdims=True)
    w2 = jnp.sum(jnp.where(eid == idx2, aff, 0.0), axis=0, keepdims=True)
    wsum = w1 + w2
    return idx1.astype(jnp.int32), idx2.astype(jnp.int32), w1 / wsum, w2 / wsum


def _local_slots(idx1, idx2):
    n_tok = idx1.shape[1]
    eid = _iota((N_EXPERTS, n_tok), 0)
    hot1, hot2 = eid == idx1, eid == idx2
    hot = jnp.where(hot1, 1.0, jnp.where(hot2, 1.0, 0.0))
    earlier = jnp.where(_iota((n_tok, n_tok), 0) < _iota((n_tok, n_tok), 1), 1.0, 0.0).astype(BF16)
    before_in_expert = _dot(hot.astype(BF16), earlier)
    counts = jnp.sum(hot, axis=1, keepdims=True)
    lower = jnp.where(_iota((N_EXPERTS, N_EXPERTS), 1) < _iota((N_EXPERTS, N_EXPERTS), 0), 1.0, 0.0).astype(BF16)
    first_slot = _dot(lower, jnp.broadcast_to(counts, (N_EXPERTS, LANES)).astype(BF16))[:, 0:1]
    slot = before_in_expert + first_slot
    slot1 = jnp.sum(jnp.where(hot1, slot, 0.0), axis=0, keepdims=True)
    slot2 = jnp.sum(jnp.where(hot2, slot, 0.0), axis=0, keepdims=True)
    return slot1, slot2, counts


def _slot_one_hot(slot1, slot2, v1, v2):
    n_tok = slot1.shape[1]
    row = _iota((2 * n_tok, n_tok), 0).astype(F32)
    return jnp.where(row == slot1, v1, jnp.where(row == slot2, v2, 0.0))


N_MIXER_WEIGHTS = 15
CTX_SEQS_PER_STEP = 2
MAX_INLINE_BLOCKS = 2


def _mixer_kernel(n_tok, n_par, latent, n_alias, lam_init, *refs):
    it = iter(refs)
    x_ref, mod_ref = next(it), next(it)
    (n1_ref, n2_ref, win_ref, wout_ref, sw_ref, sb_ref, qg_ref, kg_ref, dl_ref, dg_ref,
     w2c_ref, gb_ref, gg_ref, rwt_ref, rb_ref) = (next(it) for _ in range(N_MIXER_WEIGHTS))
    if latent:
        ck_ref, cv_ref, st0_ref, cos_ref, sin_ref = (next(it) for _ in range(5))
    for _ in range(n_alias):
        next(it)
    xo_ref, hs_ref, slot_ref, wt_ref, cnt_ref = (next(it) for _ in range(5))
    if not latent:
        ko_ref, vo_ref, so_ref = (next(it) for _ in range(3))
    proj_ref, mix_ref, q_ref, k_ref, v_ref = (next(it) for _ in range(5))
    gq_ref, gke_ref, gv_ref, gr_ref, dec_ref, go_ref, st_ref = (next(it) for _ in range(7))

    n_blk = n_tok // ROW_BLOCK
    n_ctx = k_ref.shape[0] - n_par * n_tok
    n_keys = n_ctx + n_tok
    mod = mod_ref[0, 0]

    by_block = latent

    def whole_sequence(fn):
        def run():
            fn()

        if by_block:
            pl.when(pl.program_id(1) == 0)(run)
        else:
            run()

    def blocks(body):
        if n_par * n_blk <= MAX_INLINE_BLOCKS:
            for r in range(n_par * n_blk):
                body(r)
        else:
            def step(r, carry):
                body(r)
                return carry
            whole_sequence(lambda: lax.fori_loop(0, n_par * n_blk, step, 0))

    def aligned(start, size):
        return pl.ds(start if isinstance(start, int) else pl.multiple_of(start, size), size)

    def block_rows(r, offset=0):
        return aligned(offset + r * ROW_BLOCK, ROW_BLOCK)

    if not latent:
        for ref in (ko_ref, vo_ref, so_ref):
            for q in range(n_par):
                for other in range(1, ref.shape[1]):
                    ref[q, other] = jnp.zeros(ref.shape[2:], F32)

    lane_group = _iota((SGU_CHUNK, SGU_W), 1) // SGU_GROUP_W
    blk_r = _iota((ROW_BLOCK, ROW_BLOCK), 0)
    blk_c = _iota((ROW_BLOCK, ROW_BLOCK), 1)
    same_chunk = (blk_r // GLA_CHUNK) == (blk_c // GLA_CHUNK)
    tri = (jnp.where(same_chunk & (blk_c <= blk_r), 1.0, 0.0).astype(BF16),
           jnp.where(same_chunk & (blk_c >= blk_r), 1.0, 0.0).astype(BF16))
    chunks_per_blk = ROW_BLOCK // GLA_CHUNK
    head_of_lane = _iota((GLA_CHUNK, W_GLA), 1) // GLA_DK
    stack_r = _iota((GLA_HEADS * GLA_CHUNK, GLA_CHUNK), 0) % GLA_CHUNK
    stack_c = _iota((GLA_HEADS * GLA_CHUNK, GLA_CHUNK), 1)
    causal = (stack_c <= stack_r, stack_c >= stack_r)

    if latent:
        def cached_context():
            for h in range(DIFF_HEADS):
                k_ref[0:n_ctx, h * DIFF_V:(h + 1) * DIFF_V] = ck_ref[0, 0, h].astype(BF16)
                v_ref[0:n_ctx, h * DIFF_V:(h + 1) * DIFF_V] = cv_ref[0, 0, h].astype(BF16)
            st_ref[0] = st0_ref[0, 0]

        whole_sequence(cached_context)
        pair_lo = (_iota((ROW_BLOCK, W_QK), 1) % (DIFF_QK // 2)) < (DIFF_QK // 4)

        def rope(z, rows):
            cos = jnp.concatenate([cos_ref[rows, :]] * DIFF_HEADS, axis=-1)
            sin = jnp.concatenate([sin_ref[rows, :]] * DIFF_HEADS, axis=-1)
            shift = DIFF_QK // 4
            swapped = jnp.where(pair_lo, pltpu.roll(z, W_QK - shift, 1), pltpu.roll(z, shift, 1))
            return z * cos + swapped * sin

    def modulated_input(r):
        h = _row_rms(x_ref[block_rows(r), :]) * n1_ref[0]
        return (h * (1.0 + mod[1:2, :]) + mod[0:1, :]).astype(BF16)

    def spatial_gating(r):
        for c in range(ROW_BLOCK // SGU_CHUNK):
            local = slice(c * SGU_CHUNK, (c + 1) * SGU_CHUNK)
            u = jax.nn.gelu(proj_ref[local, C_AU:C_AU + SGU_W])
            v = _group_rms(jax.nn.gelu(proj_ref[local, C_AV:C_AV + SGU_W]), SGU_GROUP_W).astype(BF16)
            s = sb_ref[0]
            for g in range(SGU_GROUPS):
                s = s + jnp.where(lane_group == g, _dot(sw_ref[0, g], v), 0.0)
            mix_ref[aligned(r * ROW_BLOCK + c * SGU_CHUNK, SGU_CHUNK), M_A:M_A + SGU_W] = (u * s).astype(BF16)

    def attention_operands(r):
        rows = block_rows(r)
        key_rows = block_rows(r, n_ctx)
        seq, seq_rows = r // n_blk, block_rows(r % n_blk)
        qn = _group_rms(proj_ref[:, C_BQ:C_BQ + W_QK], DIFF_QK) * qg_ref[0]
        kn = _group_rms(proj_ref[:, C_BK:C_BK + W_QK], DIFF_QK) * kg_ref[0]
        vv = proj_ref[:, C_BV:C_BV + W_QK]
        if latent:
            qn, kn = rope(qn, rows), rope(kn, rows)
        else:
            for h in range(DIFF_HEADS):
                for i in range(2):
                    lo = h * DIFF_V + i * DIFF_QK
                    ko_ref[seq, 0, h, i, seq_rows, :] = kn[:, lo:lo + DIFF_QK]
                vo_ref[seq, 0, h, seq_rows, :] = vv[:, h * DIFF_V:(h + 1) * DIFF_V]
        q_ref[rows, :] = (qn * (DIFF_QK ** -0.5)).astype(BF16)
        k_ref[key_rows, :] = kn.astype(BF16)
        v_ref[key_rows, :] = vv.astype(BF16)

    def gla_operands(r):
        rows = block_rows(r)
        gpre = _dot(proj_ref[:, C_LR:C_LR + LANES].astype(BF16), w2c_ref[0]) + gb_ref[0]
        gate = _log_sigmoid(gpre) * (1.0 / GLA_GATE_NORM)
        gq = proj_ref[:, C_CQ:C_CQ + W_GLA] * (GLA_DK ** -0.5)
        gk = proj_ref[:, C_CK:C_CK + W_GLA]
        gv = proj_ref[:, C_CV:C_CV + W_GLA].astype(BF16)
        gv_ref[rows, :] = gv
        gr_ref[rows, :] = proj_ref[:, C_CR:C_CR + W_GLA]
        for d in range(2):
            g = gate[:, d * W_GLA:(d + 1) * W_GLA]
            b = sum(_dot(tri[d], p) for p in _split3(g))
            last = GLA_CHUNK - 1 if d == 0 else 0
            b_last = jnp.concatenate(
                [jnp.broadcast_to(b[c * GLA_CHUNK + last:c * GLA_CHUNK + last + 1, :], (GLA_CHUNK, W_GLA))
                 for c in range(chunks_per_blk)], axis=0)
            q_dec = (gq * jnp.exp(b)).astype(BF16)
            k_inv = (gk * jnp.exp(-b)).astype(BF16)
            gq_ref[d, rows, :] = q_dec
            gke_ref[d, rows, :] = (gk * jnp.exp(b_last - b)).astype(BF16)
            for c in range(chunks_per_blk):
                row = c * GLA_CHUNK + last
                dec_ref[d, r * chunks_per_blk + c] = jnp.exp(b[row:row + 1, :])
                chunk = slice(c * GLA_CHUNK, (c + 1) * GLA_CHUNK)
                qd = q_dec[chunk]
                q_stack = jnp.concatenate(
                    [jnp.where(head_of_lane == h, qd, jnp.zeros_like(qd)) for h in range(GLA_HEADS)], axis=0)
                attn = jnp.where(causal[d], _dot_nt(q_stack, k_inv[chunk]), 0.0)
                spread = _dot(attn.astype(BF16), gv[chunk])
                o = jnp.zeros((GLA_CHUNK, W_GLA), F32)
                for h in range(GLA_HEADS):
                    o = o + jnp.where(head_of_lane == h, spread[h * GLA_CHUNK:(h + 1) * GLA_CHUNK, :], 0.0)
                go_ref[d, aligned(r * ROW_BLOCK + c * GLA_CHUNK, GLA_CHUNK), :] = o

    def project_and_split(r):
        proj_ref[...] = _dot(modulated_input(r), win_ref[0])
        spatial_gating(r)
        attention_operands(r)
        gla_operands(r)

    blocks(project_and_split)

    dl = dl_ref[0]
    lam = (jnp.exp(jnp.sum(dl[0:1] * dl[1:2], axis=-1, keepdims=True))
           - jnp.exp(jnp.sum(dl[2:3] * dl[3:4], axis=-1, keepdims=True)) + lam_init)
    sub0 = (_iota((ROW_BLOCK, DIFF_V), 1) < DIFF_QK)

    def softmax(s):
        e = jnp.exp(s - jnp.max(s, axis=-1, keepdims=True))
        return e, jnp.sum(e, axis=-1, keepdims=True)

    def attn_block(r):
        rows = block_rows(r)
        keys = aligned((r // n_blk) * n_keys, n_keys)
        for h in range(DIFF_HEADS):
            cols = slice(h * DIFF_V, (h + 1) * DIFF_V)
            qh = q_ref[rows, cols]
            kh = k_ref[keys, cols]
            e0, z0 = softmax(_dot_nt(jnp.where(sub0, qh, jnp.zeros_like(qh)), kh))
            e1, z1 = softmax(_dot_nt(jnp.where(sub0, jnp.zeros_like(qh), qh), kh))
            w = e0 / z0 - lam * (e1 / z1)
            o = _dot(w.astype(BF16), v_ref[keys, cols])
            o = _row_rms(o) * dg_ref[0] * (1.0 - lam_init)
            mix_ref[rows, M_B + h * DIFF_V:M_B + (h + 1) * DIFF_V] = o.astype(BF16)

    blocks(attn_block)

    if not latent:
        st_ref[...] = jnp.zeros(st_ref.shape, F32)

    n_chunk = n_tok // GLA_CHUNK
    st_diag = (_iota((W_GLA, W_GLA), 0) // GLA_DV) == (_iota((W_GLA, W_GLA), 1) // GLA_DK)

    def gla_step(c, carry):
        for seq in range(n_par):
            for d in range(2):
                cc = seq * n_chunk + (c if d == 0 else n_chunk - 1 - c)
                rows = pl.ds(pl.multiple_of(cc * GLA_CHUNK, GLA_CHUNK), GLA_CHUNK)
                st = st_ref[seq, d]
                go_ref[d, rows, :] = go_ref[d, rows, :] + _dot_nt(gq_ref[d, rows, :], st.astype(BF16))
                upd = _dot_tn(gv_ref[rows, :], gke_ref[d, rows, :])
                st_ref[seq, d] = dec_ref[d, cc] * st + jnp.where(st_diag, upd, 0.0)
        return carry

    whole_sequence(lambda: lax.fori_loop(0, n_chunk, gla_step, 0))

    if not latent:
        for seq in range(n_par):
            for d in range(2):
                s_full = st_ref[seq, d].T
                for h in range(GLA_HEADS):
                    so_ref[seq, 0, d, h] = s_full[h * GLA_DK:(h + 1) * GLA_DK, h * GLA_DV:(h + 1) * GLA_DV]

    def finish_block(r, out_r):
        rows, out_rows = block_rows(r), block_rows(out_r)
        oc = _group_rms(go_ref[0, rows, :] + go_ref[1, rows, :], GLA_DV) * gg_ref[0]
        oc = oc * jax.nn.silu(gr_ref[rows, :])
        mix_ref[rows, M_C:M_C + W_GLA] = oc.astype(BF16)
        x1 = x_ref[rows, :] + mod[2:3, :] * _dot(mix_ref[rows, :], wout_ref[0])
        xo_ref[out_rows, :] = x1
        hn = _row_rms(x1) * n2_ref[0]
        hn = hn * (1.0 + mod[4:5, :]) + mod[3:4, :]
        idx1, idx2, w1, w2 = _route(hn, rwt_ref, rb_ref)
        slot1, slot2, counts = _local_slots(idx1, idx2)
        perm = _slot_one_hot(slot1, slot2, 1.0, 1.0).astype(BF16)
        _to_row_slabs(hs_ref, 2 * out_r * ROW_BLOCK, _dot(perm, hn.astype(BF16)))
        slot_ref[:, out_rows] = jnp.concatenate([slot1, slot2], axis=0).astype(jnp.int32)
        wt_ref[:, out_rows] = jnp.concatenate([w1, w2], axis=0)
        cnt_ref[out_r] = jnp.broadcast_to(counts, (N_EXPERTS, LANES)).astype(jnp.int32)

    if by_block:
        finish_block(pl.program_id(1), 0)
    else:
        for r in range(n_par * n_blk):
            finish_block(r, r)


def _mixer_call(l, n_tok, n_par, latent, x, mods_all, weights, extras, cache_bufs):
    n_seq = x.shape[0] // n_tok
    n_all = x.shape[0]
    assert n_seq % n_par == 0 and not (latent and n_par > 1)
    n_step_tok = n_par * n_tok
    n_keys = n_step_tok + (extras[0].shape[3] if latent else 0)
    n_chunk = n_step_tok // GLA_CHUNK
    lam_init = 0.8 - 0.6 * math.exp(-0.3 * l)

    single = pl.Buffered(1)

    def layer(arr):
        tail = arr.shape[1:]
        return pl.BlockSpec((1,) + tail, lambda s, *_r, _n=len(tail): (l,) + (0,) * _n, pipeline_mode=single)

    def const(arr):
        return pl.BlockSpec(arr.shape, lambda s, *_r, _n=arr.ndim: (0,) * _n, pipeline_mode=single)

    def tok_spec(width):
        return pl.BlockSpec((n_step_tok, width), lambda s, *_r: (s, 0))

    mod_row = (lambda s: 1 + s) if latent else (lambda s: 0)
    in_specs = [tok_spec(D_MODEL),
                pl.BlockSpec((1, 1, 6, D_MODEL), lambda s, *_r: (l, mod_row(s), 0, 0))]
    in_specs += [layer(w) for w in weights[:N_MIXER_WEIGHTS - 2]] + [const(w) for w in weights[-2:]]
    operands = [x, mods_all] + list(weights)
    if latent:
        ck, cv, st0, cos, sin = extras
        in_specs += [
            pl.BlockSpec((1, 1) + ck.shape[2:], lambda s, *_r: (s, l, 0, 0, 0)),
            pl.BlockSpec((1, 1) + cv.shape[2:], lambda s, *_r: (s, l, 0, 0, 0)),
            pl.BlockSpec((1, 1) + st0.shape[2:], lambda s, *_r: (s, l, 0, 0, 0)),
            const(cos), const(sin),
        ]
        operands += [ck, cv, st0, cos, sin]
    n_in = len(operands)
    in_specs += [pl.BlockSpec(memory_space=pl.ANY)] * len(cache_bufs)
    operands += list(cache_bufs)

    tiles_per_step = n_step_tok // ROW_BLOCK
    out_shape = [
        jax.ShapeDtypeStruct((n_all, D_MODEL), F32),
        jax.ShapeDtypeStruct((2 * n_all * ROW_SLABS, LANES), F32),
        jax.ShapeDtypeStruct((2, n_all), jnp.int32),
        jax.ShapeDtypeStruct((2, n_all), F32),
        jax.ShapeDtypeStruct((n_all // ROW_BLOCK, N_EXPERTS, LANES), jnp.int32),
    ]
    if latent:
        grid = (n_seq, tiles_per_step)
        out_tok, out_tiles = ROW_BLOCK, 1
        at = lambda s, r: s * tiles_per_step + r
    else:
        grid = (n_seq // n_par,)
        out_tok, out_tiles = n_step_tok, tiles_per_step
        at = lambda s: s
    out_specs = [
        pl.BlockSpec((out_tok, D_MODEL), lambda *g: (at(*g), 0)),
        pl.BlockSpec((2 * out_tok * ROW_SLABS, LANES), lambda *g: (at(*g), 0)),
        pl.BlockSpec((2, out_tok), lambda *g: (0, at(*g))),
        pl.BlockSpec((2, out_tok), lambda *g: (0, at(*g))),
        pl.BlockSpec((out_tiles, N_EXPERTS, LANES), lambda *g: (at(*g), 0, 0)),
    ]
    n_shared_out = len(out_shape)
    aliases = {}
    if not latent:
        out_shape += [
            jax.ShapeDtypeStruct((n_seq, DEPTH, DIFF_HEADS, 2, n_tok, DIFF_QK), F32),
            jax.ShapeDtypeStruct((n_seq, DEPTH, DIFF_HEADS, n_tok, DIFF_V), F32),
            jax.ShapeDtypeStruct((n_seq, DEPTH, 2, GLA_HEADS, GLA_DK, GLA_DV), F32),
        ]
        n_lay, lay = (1, l) if cache_bufs else (DEPTH, 0)
        out_specs += [
            pl.BlockSpec((n_par, n_lay, DIFF_HEADS, 2, n_tok, DIFF_QK), lambda s: (s, lay, 0, 0, 0, 0)),
            pl.BlockSpec((n_par, n_lay, DIFF_HEADS, n_tok, DIFF_V), lambda s: (s, lay, 0, 0, 0)),
            pl.BlockSpec((n_par, n_lay, 2, GLA_HEADS, GLA_DK, GLA_DV), lambda s: (s, lay, 0, 0, 0, 0)),
        ]
        aliases = {n_in + j: n_shared_out + j for j in range(len(cache_bufs))}
    scratch = [
        pltpu.VMEM((ROW_BLOCK, D_PROJ_PAD), F32),
        pltpu.VMEM((n_step_tok, D_MODEL), BF16),
        pltpu.VMEM((n_step_tok, W_QK), BF16),
        pltpu.VMEM((n_keys, W_QK), BF16),
        pltpu.VMEM((n_keys, W_QK), BF16),
        pltpu.VMEM((2, n_step_tok, W_GLA), BF16),
        pltpu.VMEM((2, n_step_tok, W_GLA), BF16),
        pltpu.VMEM((n_step_tok, W_GLA), BF16),
        pltpu.VMEM((n_step_tok, W_GLA), F32),
        pltpu.VMEM((2, n_chunk, 1, W_GLA), F32),
        pltpu.VMEM((2, n_step_tok, W_GLA), F32),
        pltpu.VMEM((n_par, 2, W_GLA, W_GLA), F32),
    ]
    return pl.pallas_call(
        functools.partial(_mixer_kernel, n_tok, n_par, latent, len(cache_bufs), lam_init),
        grid=grid,
        in_specs=in_specs,
        out_specs=out_specs,
        out_shape=out_shape,
        scratch_shapes=scratch,
        input_output_aliases=aliases,
        compiler_params=pltpu.CompilerParams(
            dimension_semantics=("arbitrary",) * len(grid), vmem_limit_bytes=MIXER_VMEM_LIMIT),
        name="mixer_latent" if latent else "mixer_context",
    )(*operands)


PAIR_BLOCK = 2 * ROW_BLOCK
COPY_SIZES = tuple(ROW_BLOCK >> k for k in range(ROW_BLOCK.bit_length()))
LARGE_COPY = 64
GATHER_AHEAD = 2
GATHER_SLOTS = GATHER_AHEAD + 1


def _segment_copies(n_rows, make_copy, act):
    def copy_if_set(size):
        @pl.when((n_rows & size) != 0)
        def _():
            act(make_copy(n_rows & (-2 * size), size))

    n_large = COPY_SIZES.index(LARGE_COPY) + 1

    @pl.when(n_rows >= LARGE_COPY)
    def _():
        for size in COPY_SIZES[:n_large]:
            copy_if_set(size)

    for size in COPY_SIZES[n_large:]:
        copy_if_set(size)


def _start(copy):
    copy.start()


def _wait(copy):
    copy.wait()


def _slab_rows(first_row, n_rows, slab):
    return pl.ds(first_row * ROW_SLABS + slab, n_rows, stride=ROW_SLABS)


def _to_row_slabs(ref, first_row, value):
    for s in range(ROW_SLABS):
        ref[_slab_rows(first_row, value.shape[0], s), :] = value[:, s * LANES:(s + 1) * LANES]


def _from_row_slabs(ref, first_row, n_rows):
    return jnp.concatenate([ref[_slab_rows(first_row, n_rows, s), :] for s in range(ROW_SLABS)], axis=-1)


SLAB_PAIR_W = 2 * LANES
N_SLAB_PAIRS = ROW_SLABS // 2


def _slab_pair_cols(g):
    return slice(g * SLAB_PAIR_W, (g + 1) * SLAB_PAIR_W)


def _load_slab_pair(ref, first_row, n_rows, g):
    return jnp.concatenate([ref[_slab_rows(first_row, n_rows, s), :] for s in (2 * g, 2 * g + 1)], axis=-1)


def _row_span(ref, first_row, n_rows):
    return ref.at[pl.ds(pl.multiple_of(first_row * ROW_SLABS, ROW_SLABS), n_rows * ROW_SLABS)]


def _two_streams(n_first_tiles):
    def first(i, *_):
        return (jnp.minimum(i, n_first_tiles - 1), 0)

    def second(i, *_):
        return (jnp.maximum(i - n_first_tiles, 0), 0)

    return first, second


N_EXPERT_TABLES = 10
OUT_SLOTS = 2


def _expert_kernel(n_ctx_tiles, n_tiles_max, te_ref, first_ref, rows_ref, jlo_ref, jhi_ref, cpre_ref, cnt_ref,
                   lofs_ref, tile0_ref, ntile_ref, hs_c_ref, hs_l_ref, w1_ref, w3_ref, w2_ref, ys_ref,
                   xbuf_ref, obuf_ref, w1b_ref, w3b_ref, w2b_ref, sem, out_sem):
    expert = pl.program_id(0)
    n_tiles = tile0_ref[N_EXPERTS]

    def gather(t, act):
        slot = t % GATHER_SLOTS
        e, first = te_ref[t], first_ref[t]
        last = first + rows_ref[t]

        def segment_of(hs_ref, first_tile):
            def body(j, carry):
                k = j * N_EXPERTS + e
                seg_first = cpre_ref[k]
                lo = jnp.maximum(seg_first, first)
                n = jnp.maximum(jnp.minimum(seg_first + cnt_ref[k], last) - lo, 0)
                src = (j - first_tile) * PAIR_BLOCK + lofs_ref[k] + (lo - seg_first)
                dst = slot * ROW_BLOCK + lo - first
                _segment_copies(n, lambda done, size: pltpu.make_async_copy(
                    _row_span(hs_ref, src + done, size), _row_span(xbuf_ref, dst + done, size), sem.at[slot]), act)
                return carry
            return body

        jlo, jhi = jlo_ref[t], jhi_ref[t]
        lax.fori_loop(jnp.minimum(jlo, n_ctx_tiles), jnp.minimum(jhi, n_ctx_tiles), segment_of(hs_c_ref, 0), 0)
        lax.fori_loop(jnp.maximum(jlo, n_ctx_tiles), jnp.maximum(jhi, n_ctx_tiles),
                      segment_of(hs_l_ref, n_ctx_tiles), 0)

    def out_copy(t, oslot):
        return pltpu.make_async_copy(
            _row_span(obuf_ref, oslot * ROW_BLOCK, ROW_BLOCK), _row_span(ys_ref, t * ROW_BLOCK, ROW_BLOCK),
            out_sem.at[oslot])

    @pl.when(expert == 0)
    def _():
        xbuf_ref[...] = jnp.zeros(xbuf_ref.shape, F32)
        for t in range(GATHER_AHEAD):
            gather(t, _start)

    w1b_ref[...] = w1_ref[0, 0].astype(BF16)
    w3b_ref[...] = w3_ref[0, 0].astype(BF16)
    w2b_ref[...] = w2_ref[0, 0].astype(BF16)
    tile0, n_own = tile0_ref[expert], ntile_ref[expert]

    def tile_body(k, carry):
        t = tile0 + k
        slot, oslot = t % GATHER_SLOTS, t % OUT_SLOTS

        @pl.when(t + GATHER_AHEAD < n_tiles)
        def _():
            gather(t + GATHER_AHEAD, _start)

        n_rows = rows_ref[t]
        _segment_copies(n_rows, lambda done, size: pltpu.make_async_copy(
            _row_span(hs_c_ref, done, size), _row_span(xbuf_ref, slot * ROW_BLOCK + done, size), sem.at[slot]), _wait)

        @pl.when(t >= OUT_SLOTS)
        def _():
            out_copy(t, oslot).wait()

        live = _iota((ROW_BLOCK, D_MODEL), 0) < n_rows
        x = jnp.where(live, _from_row_slabs(xbuf_ref, slot * ROW_BLOCK, ROW_BLOCK), 0.0).astype(BF16)
        hid = jax.nn.silu(_dot(x, w1b_ref[...])) * _dot(x, w3b_ref[...])
        _to_row_slabs(obuf_ref, oslot * ROW_BLOCK, _dot(hid.astype(BF16), w2b_ref[...]))
        out_copy(t, oslot).start()
        return carry

    lax.fori_loop(0, n_own, tile_body, 0)

    @pl.when(expert == N_EXPERTS - 1)
    def _():
        for oslot in range(OUT_SLOTS):
            @pl.when(n_tiles > oslot)
            def _():
                out_copy(0, oslot).wait()
        obuf_ref[...] = jnp.zeros(obuf_ref.shape, F32)

        def fill(t, carry):
            out_copy(t, 0).start()
            out_copy(t, 0).wait()
            return carry

        lax.fori_loop(n_tiles, n_tiles_max, fill, 0)


def _expert_call(l, plan, hs_c, hs_l, w1, w3, w2):
    tables = plan["expert_tables"]
    n_tiles_max = tables[0].shape[0]
    n_ctx_tiles = hs_c.shape[0] // (PAIR_BLOCK * ROW_SLABS)

    def weight(shape):
        return pl.BlockSpec((1, 1) + shape, lambda e, *_: (l, e, 0, 0))

    return pl.pallas_call(
        functools.partial(_expert_kernel, n_ctx_tiles, n_tiles_max),
        grid_spec=pltpu.PrefetchScalarGridSpec(
            num_scalar_prefetch=N_EXPERT_TABLES,
            grid=(N_EXPERTS,),
            in_specs=[pl.BlockSpec(memory_space=pl.ANY), pl.BlockSpec(memory_space=pl.ANY),
                      weight((D_MODEL, D_EXPERT)), weight((D_MODEL, D_EXPERT)), weight((D_EXPERT, D_MODEL))],
            out_specs=pl.BlockSpec(memory_space=pl.ANY),
            scratch_shapes=[pltpu.VMEM((GATHER_SLOTS * ROW_BLOCK * ROW_SLABS, LANES), F32),
                            pltpu.VMEM((OUT_SLOTS * ROW_BLOCK * ROW_SLABS, LANES), F32),
                            pltpu.VMEM((D_MODEL, D_EXPERT), BF16), pltpu.VMEM((D_MODEL, D_EXPERT), BF16),
                            pltpu.VMEM((D_EXPERT, D_MODEL), BF16),
                            pltpu.SemaphoreType.DMA((GATHER_SLOTS,)), pltpu.SemaphoreType.DMA((OUT_SLOTS,))],
        ),
        out_shape=jax.ShapeDtypeStruct((n_tiles_max * ROW_BLOCK * ROW_SLABS, LANES), F32),
        compiler_params=pltpu.CompilerParams(
            dimension_semantics=("arbitrary",), vmem_limit_bytes=SMALL_KERNEL_VMEM_LIMIT),
        name="moe_experts",
    )(*tables, hs_c, hs_l, w1, w3, w2)


N_COMBINE_TABLES = 4
COMBINE_TILES_PER_STEP = 2


def _combine_kernel(n_ctx_steps, cnt_ref, cpre_ref, lofs_ref, starts_ref, x_c_ref, x_l_ref, slot_c_ref, slot_l_ref,
                    wt_c_ref, wt_l_ref, mod_ref, ys_ref, xo_c_ref, xo_l_ref, buf_ref, sem):
    step = pl.program_id(0)
    n_tiles = pl.num_programs(0) * COMBINE_TILES_PER_STEP

    def collect(t, act):
        slot = t % GATHER_SLOTS

        def body(e, carry):
            k = t * N_EXPERTS + e
            src, dst = starts_ref[e] + cpre_ref[k], slot * PAIR_BLOCK + lofs_ref[k]
            _segment_copies(cnt_ref[k], lambda done, size: pltpu.make_async_copy(
                _row_span(ys_ref, src + done, size), _row_span(buf_ref, dst + done, size), sem.at[slot]), act)
            return carry

        lax.fori_loop(0, N_EXPERTS, body, 0)

    @pl.when(step == 0)
    def _():
        for t in range(GATHER_AHEAD):
            collect(t, _start)

    gate = mod_ref[0, 0, 5:6, :]

    def finish(part, first_row, x_ref, slot_ref, wt_ref, xo_ref):
        tokens = pl.ds(part * ROW_BLOCK, ROW_BLOCK)
        slots, wts = slot_ref[:, tokens].astype(F32), wt_ref[:, tokens]
        slot1, slot2 = slots[0:1], slots[1:2]
        weight_of_row = jnp.sum(_slot_one_hot(slot1, slot2, wts[0:1], wts[1:2]), axis=1, keepdims=True)
        gather_rows = _slot_one_hot(slot1, slot2, 1.0, 1.0).T.astype(BF16)
        for g in range(N_SLAB_PAIRS):
            cols = _slab_pair_cols(g)
            hi, lo = _split2(_load_slab_pair(buf_ref, first_row, PAIR_BLOCK, g) * weight_of_row)
            y = _dot(gather_rows, hi) + _dot(gather_rows, lo)
            xo_ref[tokens, cols] = x_ref[tokens, cols] + gate[:, cols] * y

    for part in range(COMBINE_TILES_PER_STEP):
        j = step * COMBINE_TILES_PER_STEP + part

        @pl.when(j + GATHER_AHEAD < n_tiles)
        def _():
            collect(j + GATHER_AHEAD, _start)

        slot = j % GATHER_SLOTS
        for piece in range(PAIR_BLOCK // ROW_BLOCK):
            pltpu.make_async_copy(
                _row_span(ys_ref, piece * ROW_BLOCK, ROW_BLOCK),
                _row_span(buf_ref, slot * PAIR_BLOCK + piece * ROW_BLOCK, ROW_BLOCK), sem.at[slot]).wait()
        @pl.when(step < n_ctx_steps)
        def _():
            finish(part, slot * PAIR_BLOCK, x_c_ref, slot_c_ref, wt_c_ref, xo_c_ref)

        @pl.when(step >= n_ctx_steps)
        def _():
            finish(part, slot * PAIR_BLOCK, x_l_ref, slot_l_ref, wt_l_ref, xo_l_ref)


def _combine_call(l, plan, x_c, x_l, slot_c, slot_l, wt_c, wt_l, mods_all, mod_row_of_tile, ys):
    step_rows = COMBINE_TILES_PER_STEP * ROW_BLOCK
    n_steps = (x_c.shape[0] + x_l.shape[0]) // step_rows
    n_ctx_steps = x_c.shape[0] // step_rows
    first, second = _two_streams(n_ctx_steps)

    def lanes(index_map):
        return lambda i, *_: index_map(i)[::-1]

    return pl.pallas_call(
        functools.partial(_combine_kernel, n_ctx_steps),
        grid_spec=pltpu.PrefetchScalarGridSpec(
            num_scalar_prefetch=N_COMBINE_TABLES,
            grid=(n_steps,),
            in_specs=[pl.BlockSpec((step_rows, D_MODEL), first),
                      pl.BlockSpec((step_rows, D_MODEL), second),
                      pl.BlockSpec((2, step_rows), lanes(first)),
                      pl.BlockSpec((2, step_rows), lanes(second)),
                      pl.BlockSpec((2, step_rows), lanes(first)),
                      pl.BlockSpec((2, step_rows), lanes(second)),
                      pl.BlockSpec((1, 1, 6, D_MODEL),
                                   lambda i, *_: (l, mod_row_of_tile(i * COMBINE_TILES_PER_STEP), 0, 0)),
                      pl.BlockSpec(memory_space=pl.ANY)],
            out_specs=[pl.BlockSpec((step_rows, D_MODEL), first),
                       pl.BlockSpec((step_rows, D_MODEL), second)],
            scratch_shapes=[pltpu.VMEM((GATHER_SLOTS * PAIR_BLOCK * ROW_SLABS, LANES), F32),
                            pltpu.SemaphoreType.DMA((GATHER_SLOTS,))],
        ),
        out_shape=[jax.ShapeDtypeStruct(x_c.shape, F32), jax.ShapeDtypeStruct(x_l.shape, F32)],
        compiler_params=pltpu.CompilerParams(
            dimension_semantics=("arbitrary",), vmem_limit_bytes=SMALL_KERNEL_VMEM_LIMIT),
        name="moe_combine",
    )(*plan["combine_tables"], x_c, x_l, slot_c, slot_l, wt_c, wt_l, mods_all, ys)


def _moe_plan(cnt):
    n_tok_tiles = cnt.shape[0]
    n_tiles = n_tok_tiles * PAIR_BLOCK // ROW_BLOCK + N_EXPERTS
    lofs = jnp.cumsum(cnt, axis=1) - cnt
    cpre = jnp.cumsum(cnt, axis=0) - cnt
    counts = jnp.sum(cnt, axis=0)
    padded = (counts + ROW_BLOCK - 1) // ROW_BLOCK * ROW_BLOCK
    ends = jnp.cumsum(padded)
    starts = ends - padded
    tile_start = jnp.arange(n_tiles, dtype=jnp.int32) * ROW_BLOCK
    tile_expert = jnp.minimum(
        jnp.sum((tile_start[:, None] >= ends[None, :]).astype(jnp.int32), axis=1), N_EXPERTS - 1)
    hot = tile_expert[:, None] == jnp.arange(N_EXPERTS, dtype=jnp.int32)[None, :]
    first = tile_start - jnp.sum(jnp.where(hot, starts[None, :], 0), axis=1)
    rows = jnp.clip(jnp.sum(jnp.where(hot, counts[None, :], 0), axis=1) - first, 0, ROW_BLOCK)
    seg_first = jnp.sum(jnp.where(hot[:, None, :], cpre[None, :, :], 0), axis=2)
    seg_rows = jnp.sum(jnp.where(hot[:, None, :], cnt[None, :, :], 0), axis=2)
    overlap = (seg_first < (first + rows)[:, None]) & (seg_first + seg_rows > first[:, None])
    j = jnp.arange(n_tok_tiles, dtype=jnp.int32)[None, :]
    jlo = jnp.min(jnp.where(overlap, j, n_tok_tiles), axis=1)
    jhi = jnp.max(jnp.where(overlap, j + 1, 0), axis=1)
    i32 = lambda a: a.astype(jnp.int32).reshape(-1)
    tile0 = jnp.concatenate([starts, ends[-1:]]) // ROW_BLOCK
    return {
        "expert_tables": tuple(i32(a) for a in (tile_expert, first, rows, jlo, jhi, cpre, cnt, lofs,
                                                tile0, padded // ROW_BLOCK)),
        "combine_tables": tuple(i32(a) for a in (cnt, cpre, lofs, starts)),
    }


def _rope_tables(n_tok):
    n_rows = n_tok // GRID_W
    pos_r = jnp.repeat(jnp.arange(n_rows), GRID_W)
    pos_c = jnp.tile(jnp.arange(GRID_W), n_rows)
    half = DIFF_QK // 2
    nf = half // 2
    freqs = ROPE_BASE ** (-jnp.arange(nf, dtype=F32) / nf)

    def tables(pos):
        ang = pos.astype(F32)[:, None] * freqs
        cos, sin = jnp.cos(ang), jnp.sin(ang)
        return jnp.concatenate([cos, cos], axis=-1), jnp.concatenate([-sin, sin], axis=-1)

    cos_r, sin_r = tables(pos_r)
    cos_c, sin_c = tables(pos_c)
    cos = jnp.concatenate([cos_r, cos_c], axis=-1)
    sin = jnp.concatenate([sin_r, sin_c], axis=-1)
    return jnp.concatenate([cos, cos], axis=-1), jnp.concatenate([sin, sin], axis=-1)


def _mixer_weights(w_in, w_out, sgu_w, sgu_b, q_norm_g, k_norm_g, diff_lambda, diff_norm_g, gla_w2, gla_b,
                   gla_norm_g, norm1_g, norm2_g, router_w, router_bias):
    w_in_pad = jnp.concatenate(
        [w_in.astype(BF16), jnp.zeros(w_in.shape[:2] + (D_PROJ_PAD - w_in.shape[2],), BF16)], axis=-1)
    w2cat = jnp.zeros((DEPTH, LANES, 2 * W_GLA), F32)
    w2cat = w2cat.at[:, 0:GLA_RANK, 0:W_GLA].set(gla_w2[:, 0]).at[:, GLA_RANK:2 * GLA_RANK, W_GLA:].set(gla_w2[:, 1])
    return (
        norm1_g[:, None, :], norm2_g[:, None, :], w_in_pad, w_out.astype(BF16),
        sgu_w.astype(BF16), jnp.repeat(sgu_b.transpose(0, 2, 1), SGU_GROUP_W, axis=2),
        jnp.tile(q_norm_g, (1, W_QK // DIFF_QK))[:, None, :], jnp.tile(k_norm_g, (1, W_QK // DIFF_QK))[:, None, :],
        diff_lambda, diff_norm_g[:, None, :],
        w2cat.astype(BF16), gla_b.reshape(DEPTH, 1, 2 * W_GLA), jnp.tile(gla_norm_g, (1, GLA_HEADS))[:, None, :],
        router_w.T, router_bias[:, None],
    )


def kernel(x_prompt, x_sample, cache_k, cache_v, state_gla, c, c_ctx, w_in, w_out, sgu_w, sgu_b, q_norm_g, k_norm_g,
           diff_lambda, diff_norm_g, gla_w2, gla_b, gla_norm_g, norm1_g, norm2_g, ada_w, ada_b, router_w, router_bias,
           moe_w1, moe_w3, moe_w2):
    n_ctx_seq, ctx_len, _ = x_prompt.shape
    n_lat_seq, lat_len, _ = x_sample.shape
    n_ctx_tok = n_ctx_seq * ctx_len
    n_lat_tok = n_lat_seq * lat_len
    ctx_tiles = n_ctx_tok // ROW_BLOCK
    lat_tiles_per_seq = lat_len // ROW_BLOCK

    n_cond = 1 + n_lat_seq
    cond_t = jnp.zeros((D_MODEL, SUBLANES), F32).at[:, 0].set(c_ctx).at[:, 1:n_cond].set(c.T)
    mods_all = _adaln_call(cond_t, n_cond, ada_w, ada_b)[:, :n_cond].reshape(DEPTH, n_cond, 6, D_MODEL)
    weights = _mixer_weights(w_in, w_out, sgu_w, sgu_b, q_norm_g, k_norm_g, diff_lambda, diff_norm_g, gla_w2, gla_b,
                             gla_norm_g, norm1_g, norm2_g, router_w, router_bias)

    ck_all = cache_k.transpose(0, 1, 2, 4, 3, 5).reshape(cache_k.shape[:3] + (cache_k.shape[4], DIFF_V))
    st_all = jnp.einsum('bldhkv,hg->bldhvgk', state_gla, jnp.eye(GLA_HEADS, dtype=F32)).reshape(
        n_lat_seq, DEPTH, 2, W_GLA, W_GLA)
    cos, sin = _rope_tables(lat_len)
    extras = (ck_all, cache_v, st_all, cos, sin)

    def mod_row_of_tile(i):
        return jnp.where(i < ctx_tiles, 0, 1 + (i - ctx_tiles) // lat_tiles_per_seq)

    x_c = x_prompt.reshape(n_ctx_tok, D_MODEL)
    x_l = x_sample.reshape(n_lat_tok, D_MODEL)
    cache_bufs = ()
    for l in range(DEPTH):
        ctx_par = 1 if l == 0 else CTX_SEQS_PER_STEP
        x1_c, hs_c, slot_c, wt_c, cnt_c, *cache_bufs = _mixer_call(
            l, ctx_len, ctx_par, False, x_c, mods_all, weights, None, tuple(cache_bufs))
        x1_l, hs_l, slot_l, wt_l, cnt_l = _mixer_call(l, lat_len, 1, True, x_l, mods_all, weights, extras, ())
        plan = _moe_plan(jnp.concatenate([cnt_c[:, :, 0], cnt_l[:, :, 0]], axis=0))
        ys = _expert_call(l, plan, hs_c, hs_l, moe_w1, moe_w3, moe_w2)
        x_c, x_l = _combine_call(l, plan, x1_c, x1_l, slot_c, slot_l, wt_c, wt_l, mods_all, mod_row_of_tile, ys)

    new_k, new_v, new_s = cache_bufs
    return (x_c.reshape(x_prompt.shape), x_l.reshape(x_sample.shape), new_k, new_v, new_s)
```

```python
import functools
import math

import jax
import jax.numpy as jnp
from jax import lax
from jax.experimental import pallas as pl
from jax.experimental.pallas import tpu as pltpu

F32 = jnp.float32
BF16 = jnp.bfloat16

D_MODEL = 1024
DEPTH = 4
GRID_W = 64
SGU_GROUPS = 4
SGU_GROUP_W = 64
SGU_W = SGU_GROUPS * SGU_GROUP_W
SGU_CHUNK = 128
DIFF_HEADS = 4
DIFF_QK = 64
DIFF_V = 2 * DIFF_QK
ROPE_BASE = 10000.0
GLA_HEADS = 4
GLA_DK = 64
GLA_DV = 64
GLA_RANK = 16
GLA_GATE_NORM = 16.0
GLA_CHUNK = 64
N_EXPERTS = 16
N_GROUPS = 4
EXPERTS_PER_GROUP = N_EXPERTS // N_GROUPS
D_EXPERT = 512
EPS = 1e-6

LANES = 128
SUBLANES = 8
MXU_DIM = 256
V7X_VMEM_BYTES = 64 * 1024 * 1024
MIXER_VMEM_LIMIT = V7X_VMEM_BYTES * 7 // 8
SMALL_KERNEL_VMEM_LIMIT = V7X_VMEM_BYTES * 5 // 8

ROW_BLOCK = MXU_DIM
ROW_SLABS = D_MODEL // LANES

W_QK = DIFF_HEADS * 2 * DIFF_QK
W_GLA = GLA_HEADS * GLA_DK
C_AU, C_AV = 0, SGU_W
C_BQ = C_AV + SGU_W
C_BK, C_BV = C_BQ + W_QK, C_BQ + 2 * W_QK
C_CQ = C_BV + DIFF_HEADS * DIFF_V
C_CK, C_CV, C_CR, C_LR = C_CQ + W_GLA, C_CQ + 2 * W_GLA, C_CQ + 3 * W_GLA, C_CQ + 4 * W_GLA
D_PROJ_MAIN = C_LR
D_PROJ_PAD = D_PROJ_MAIN + LANES
M_A, M_B, M_C = 0, SGU_W, SGU_W + DIFF_HEADS * DIFF_V


def _split2(x):
    hi = x.astype(BF16)
    lo = (x - hi.astype(F32)).astype(BF16)
    return hi, lo


def _split3(x):
    hi = x.astype(BF16)
    r = x - hi.astype(F32)
    mid = r.astype(BF16)
    lo = (r - mid.astype(F32)).astype(BF16)
    return hi, mid, lo


def _dot(a, b):
    return jnp.dot(a, b, preferred_element_type=F32)


def _dot_nt(a, b):
    return lax.dot_general(a, b, (((1,), (1,)), ((), ())), preferred_element_type=F32)


def _dot_tn(a, b):
    return lax.dot_general(a, b, (((0,), (0,)), ((), ())), preferred_element_type=F32)


def _iota(shape, dim):
    return lax.broadcasted_iota(jnp.int32, shape, dim)


def _block_ones(width, block):
    r = _iota((width, width), 0) // block
    c = _iota((width, width), 1) // block
    return (r == c)


def _group_sum(z, block):
    width = z.shape[-1]
    outs = []
    for s in range(0, width, MXU_DIM):
        w = min(MXU_DIM, width - s)
        ones = _block_ones(w, block).astype(BF16)
        hi, lo = _split2(z[:, s:s + w])
        outs.append(_dot(hi, ones) + _dot(lo, ones))
    return outs[0] if len(outs) == 1 else jnp.concatenate(outs, axis=-1)


def _group_rms(z, block):
    ms = _group_sum(z * z, block) * (1.0 / block)
    return z * lax.rsqrt(ms + EPS)


def _row_rms(z):
    return z * lax.rsqrt(jnp.mean(z * z, axis=-1, keepdims=True) + EPS)


def _log_sigmoid(x):
    return jnp.minimum(x, 0.0) - jnp.log(1.0 + jnp.exp(-jnp.abs(x)))


ADA_COLS = 1536


def _adaln_kernel(n_cond, cond_t_ref, w_ref, b_ref, o_ref):
    sc = jax.nn.silu(cond_t_ref[...])
    w = w_ref[0]
    rows = [jnp.sum(sc[:, r:r + 1] * w, axis=0, keepdims=True) + b_ref[0] for r in range(n_cond)]
    o_ref[0] = jnp.concatenate(rows + [jnp.zeros((SUBLANES - n_cond, w.shape[1]), F32)], axis=0)


def _adaln_call(cond_t, n_cond, ada_w, ada_b):
    n_col = 6 * D_MODEL // ADA_COLS
    return pl.pallas_call(
        functools.partial(_adaln_kernel, n_cond),
        grid=(DEPTH, n_col),
        in_specs=[
            pl.BlockSpec((D_MODEL, SUBLANES), lambda l, j: (0, 0)),
            pl.BlockSpec((1, D_MODEL, ADA_COLS), lambda l, j: (l, 0, j)),
            pl.BlockSpec((1, 1, ADA_COLS), lambda l, j: (l, 0, j)),
        ],
        out_specs=pl.BlockSpec((1, SUBLANES, ADA_COLS), lambda l, j: (l, 0, j)),
        out_shape=jax.ShapeDtypeStruct((DEPTH, SUBLANES, 6 * D_MODEL), F32),
        compiler_params=pltpu.CompilerParams(
            dimension_semantics=("arbitrary", "arbitrary"), vmem_limit_bytes=SMALL_KERNEL_VMEM_LIMIT),
        name="adaln",
    )(cond_t, ada_w, ada_b.reshape(DEPTH, 1, 6 * D_MODEL))


def _route(hn, rwt_ref, rb_ref):
    h_hi, h_lo = _split2(hn)
    rw = rwt_ref[...]
    rw_hi = rw.astype(BF16)
    rw_lo = (rw - rw_hi.astype(F32)).astype(BF16)
    logits = _dot_nt(rw_hi, h_hi) + _dot_nt(rw_hi, h_lo) + _dot_nt(rw_lo, h_hi)
    aff = jax.nn.sigmoid(logits)
    sel = aff + rb_ref[...]
    n_tok = sel.shape[1]

    def top2_sum(a, b, c, d):
        hi1, lo1 = jnp.maximum(a, b), jnp.minimum(a, b)
        hi2, lo2 = jnp.maximum(c, d), jnp.minimum(c, d)
        return jnp.maximum(hi1, hi2) + jnp.maximum(jnp.minimum(hi1, hi2), jnp.maximum(lo1, lo2))

    scores = []
    for g in range(N_GROUPS):
        rows = [sel[EXPERTS_PER_GROUP * g + j:EXPERTS_PER_GROUP * g + j + 1, :] for j in range(EXPERTS_PER_GROUP)]
        scores.append(top2_sum(*rows))
    best = jnp.zeros((1, n_tok), jnp.int32)
    best_score = scores[0]
    for g in range(1, N_GROUPS):
        upd = scores[g] > best_score
        best = jnp.where(upd, g, best)
        best_score = jnp.where(upd, scores[g], best_score)

    eid_i = _iota((N_EXPERTS, n_tok), 0)
    eid = eid_i.astype(F32)
    neg = jnp.float32(-jnp.inf)
    msel = jnp.where(eid_i // EXPERTS_PER_GROUP == best, sel, neg)
    m1 = jnp.max(msel, axis=0, keepdims=True)
    idx1 = jnp.min(jnp.where(msel == m1, eid, float(N_EXPERTS)), axis=0, keepdims=True)
    msel2 = jnp.where(eid == idx1, neg, msel)
    m2 = jnp.max(msel2, axis=0, keepdims=True)
    idx2 = jnp.min(jnp.where(msel2 == m2, eid, float(N_EXPERTS)), axis=0, keepdims=True)
    w1 = jnp.sum(jnp.where(eid == idx1, aff, 0.0), axis=0, keepdims=True)
    w2 = jnp.sum(jnp.where(eid == idx2, aff, 0.0), axis=0, keepdims=True)
    wsum = w1 + w2
    return idx1.astype(jnp.int32), idx2.astype(jnp.int32), w1 / wsum, w2 / wsum


def _local_slots(idx1, idx2):
    n_tok = idx1.shape[1]
    eid = _iota((N_EXPERTS, n_tok), 0)
    hot1, hot2 = eid == idx1, eid == idx2
    hot = jnp.where(hot1, 1.0, jnp.where(hot2, 1.0, 0.0))
    earlier = jnp.where(_iota((n_tok, n_tok), 0) < _iota((n_tok, n_tok), 1), 1.0, 0.0).astype(BF16)
    before_in_expert = _dot(hot.astype(BF16), earlier)
    counts = jnp.sum(hot, axis=1, keepdims=True)
    lower = jnp.where(_iota((N_EXPERTS, N_EXPERTS), 1) < _iota((N_EXPERTS, N_EXPERTS), 0), 1.0, 0.0).astype(BF16)
    first_slot = _dot(lower, jnp.broadcast_to(counts, (N_EXPERTS, LANES)).astype(BF16))[:, 0:1]
    slot = before_in_expert + first_slot
    slot1 = jnp.sum(jnp.where(hot1, slot, 0.0), axis=0, keepdims=True)
    slot2 = jnp.sum(jnp.where(hot2, slot, 0.0), axis=0, keepdims=True)
    return slot1, slot2, counts


def _slot_one_hot(slot1, slot2, v1, v2):
    n_tok = slot1.shape[1]
    row = _iota((2 * n_tok, n_tok), 0).astype(F32)
    return jnp.where(row == slot1, v1, jnp.where(row == slot2, v2, 0.0))


N_MIXER_WEIGHTS = 15
CTX_SEQS_PER_STEP = 2
MAX_INLINE_BLOCKS = 2


def _mixer_kernel(n_tok, n_par, latent, n_alias, lam_init, *refs):
    it = iter(refs)
    x_ref, mod_ref = next(it), next(it)
    (n1_ref, n2_ref, win_ref, wout_ref, sw_ref, sb_ref, qg_ref, kg_ref, dl_ref, dg_ref,
     w2c_ref, gb_ref, gg_ref, rwt_ref, rb_ref) = (next(it) for _ in range(N_MIXER_WEIGHTS))
    if latent:
        ck_ref, cv_ref, st0_ref, cos_ref, sin_ref = (next(it) for _ in range(5))
    for _ in range(n_alias):
        next(it)
    xo_ref, hs_ref, slot_ref, wt_ref, cnt_ref = (next(it) for _ in range(5))
    if not latent:
        ko_ref, vo_ref, so_ref = (next(it) for _ in range(3))
    proj_ref, mix_ref, q_ref, k_ref, v_ref = (next(it) for _ in range(5))
    gq_ref, gke_ref, gv_ref, gr_ref, dec_ref, go_ref, st_ref = (next(it) for _ in range(7))

    n_blk = n_tok // ROW_BLOCK
    n_ctx = k_ref.shape[0] - n_par * n_tok
    n_keys = n_ctx + n_tok
    mod = mod_ref[0, 0]

    by_block = latent

    def whole_sequence(fn):
        def run():
            fn()

        if by_block:
            pl.when(pl.program_id(1) == 0)(run)
        else:
            run()

    def blocks(body):
        if n_par * n_blk <= MAX_INLINE_BLOCKS:
            for r in range(n_par * n_blk):
                body(r)
        else:
            def step(r, carry):
                body(r)
                return carry
            whole_sequence(lambda: lax.fori_loop(0, n_par * n_blk, step, 0))

    def aligned(start, size):
        return pl.ds(start if isinstance(start, int) else pl.multiple_of(start, size), size)

    def block_rows(r, offset=0):
        return aligned(offset + r * ROW_BLOCK, ROW_BLOCK)

    if not latent:
        for ref in (ko_ref, vo_ref, so_ref):
            for q in range(n_par):
                for other in range(1, ref.shape[1]):
                    ref[q, other] = jnp.zeros(ref.shape[2:], F32)

    lane_group = _iota((SGU_CHUNK, SGU_W), 1) // SGU_GROUP_W
    blk_r = _iota((ROW_BLOCK, ROW_BLOCK), 0)
    blk_c = _iota((ROW_BLOCK, ROW_BLOCK), 1)
    same_chunk = (blk_r // GLA_CHUNK) == (blk_c // GLA_CHUNK)
    tri = (jnp.where(same_chunk & (blk_c <= blk_r), 1.0, 0.0).astype(BF16),
           jnp.where(same_chunk & (blk_c >= blk_r), 1.0, 0.0).astype(BF16))
    chunks_per_blk = ROW_BLOCK // GLA_CHUNK
    head_of_lane = _iota((GLA_CHUNK, W_GLA), 1) // GLA_DK
    stack_r = _iota((GLA_HEADS * GLA_CHUNK, GLA_CHUNK), 0) % GLA_CHUNK
    stack_c = _iota((GLA_HEADS * GLA_CHUNK, GLA_CHUNK), 1)
    causal = (stack_c <= stack_r, stack_c >= stack_r)

    if latent:
        def cached_context():
            for h in range(DIFF_HEADS):
                for i in range(2):
                    lo = h * DIFF_V + i * DIFF_QK
                    k_ref[0:n_ctx, lo:lo + DIFF_QK] = ck_ref[0, 0, h, i].astype(BF16)
                v_ref[0:n_ctx, h * DIFF_V:(h + 1) * DIFF_V] = cv_ref[0, 0, h].astype(BF16)
            st_ref[0] = st0_ref[0, 0]

        whole_sequence(cached_context)
        pair_lo = (_iota((ROW_BLOCK, W_QK), 1) % (DIFF_QK // 2)) < (DIFF_QK // 4)

        def rope(z, rows):
            cos = jnp.concatenate([cos_ref[rows, :]] * DIFF_HEADS, axis=-1)
            sin = jnp.concatenate([sin_ref[rows, :]] * DIFF_HEADS, axis=-1)
            shift = DIFF_QK // 4
            swapped = jnp.where(pair_lo, pltpu.roll(z, W_QK - shift, 1), pltpu.roll(z, shift, 1))
            return z * cos + swapped * sin

    def modulated_input(r):
        h = _row_rms(x_ref[block_rows(r), :]) * n1_ref[0]
        return (h * (1.0 + mod[1:2, :]) + mod[0:1, :]).astype(BF16)

    def spatial_gating(r):
        for c in range(ROW_BLOCK // SGU_CHUNK):
            local = slice(c * SGU_CHUNK, (c + 1) * SGU_CHUNK)
            u = jax.nn.gelu(proj_ref[local, C_AU:C_AU + SGU_W])
            v = _group_rms(jax.nn.gelu(proj_ref[local, C_AV:C_AV + SGU_W]), SGU_GROUP_W).astype(BF16)
            s = sb_ref[0]
            for g in range(SGU_GROUPS):
                s = s + jnp.where(lane_group == g, _dot(sw_ref[0, g], v), 0.0)
            mix_ref[aligned(r * ROW_BLOCK + c * SGU_CHUNK, SGU_CHUNK), M_A:M_A + SGU_W] = (u * s).astype(BF16)

    def attention_operands(r):
        rows = block_rows(r)
        key_rows = block_rows(r, n_ctx)
        seq, seq_rows = r // n_blk, block_rows(r % n_blk)
        qn = _group_rms(proj_ref[:, C_BQ:C_BQ + W_QK], DIFF_QK) * qg_ref[0]
        kn = _group_rms(proj_ref[:, C_BK:C_BK + W_QK], DIFF_QK) * kg_ref[0]
        vv = proj_ref[:, C_BV:C_BV + W_QK]
        if latent:
            qn, kn = rope(qn, rows), rope(kn, rows)
        else:
            for h in range(DIFF_HEADS):
                for i in range(2):
                    lo = h * DIFF_V + i * DIFF_QK
                    ko_ref[seq, 0, h, i, seq_rows, :] = kn[:, lo:lo + DIFF_QK]
                vo_ref[seq, 0, h, seq_rows, :] = vv[:, h * DIFF_V:(h + 1) * DIFF_V]
        q_ref[rows, :] = (qn * (DIFF_QK ** -0.5)).astype(BF16)
        k_ref[key_rows, :] = kn.astype(BF16)
        v_ref[key_rows, :] = vv.astype(BF16)

    def gla_operands(r):
        rows = block_rows(r)
        gpre = _dot(proj_ref[:, C_LR:C_LR + LANES].astype(BF16), w2c_ref[0]) + gb_ref[0]
        gate = _log_sigmoid(gpre) * (1.0 / GLA_GATE_NORM)
        gq = proj_ref[:, C_CQ:C_CQ + W_GLA] * (GLA_DK ** -0.5)
        gk = proj_ref[:, C_CK:C_CK + W_GLA]
        gv = proj_ref[:, C_CV:C_CV + W_GLA].astype(BF16)
        gv_ref[rows, :] = gv
        gr_ref[rows, :] = proj_ref[:, C_CR:C_CR + W_GLA]
        for d in range(2):
            g = gate[:, d * W_GLA:(d + 1) * W_GLA]
            b = sum(_dot(tri[d], p) for p in _split3(g))
            last = GLA_CHUNK - 1 if d == 0 else 0
            b_last = jnp.concatenate(
                [jnp.broadcast_to(b[c * GLA_CHUNK + last:c * GLA_CHUNK + last + 1, :], (GLA_CHUNK, W_GLA))
                 for c in range(chunks_per_blk)], axis=0)
            q_dec = (gq * jnp.exp(b)).astype(BF16)
            k_inv = (gk * jnp.exp(-b)).astype(BF16)
            gq_ref[d, rows, :] = q_dec
            gke_ref[d, rows, :] = (gk * jnp.exp(b_last - b)).astype(BF16)
            for c in range(chunks_per_blk):
                row = c * GLA_CHUNK + last
                dec_ref[d, r * chunks_per_blk + c] = jnp.exp(b[row:row + 1, :])
                chunk = slice(c * GLA_CHUNK, (c + 1) * GLA_CHUNK)
                qd = q_dec[chunk]
                q_stack = jnp.concatenate(
                    [jnp.where(head_of_lane == h, qd, jnp.zeros_like(qd)) for h in range(GLA_HEADS)], axis=0)
                attn = jnp.where(causal[d], _dot_nt(q_stack, k_inv[chunk]), 0.0)
                spread = _dot(attn.astype(BF16), gv[chunk])
                o = jnp.zeros((GLA_CHUNK, W_GLA), F32)
                for h in range(GLA_HEADS):
                    o = o + jnp.where(head_of_lane == h, spread[h * GLA_CHUNK:(h + 1) * GLA_CHUNK, :], 0.0)
                go_ref[d, aligned(r * ROW_BLOCK + c * GLA_CHUNK, GLA_CHUNK), :] = o

    def project_and_split(r):
        proj_ref[...] = _dot(modulated_input(r), win_ref[0])
        spatial_gating(r)
        attention_operands(r)
        gla_operands(r)

    blocks(project_and_split)

    dl = dl_ref[0]
    lam = (jnp.exp(jnp.sum(dl[0:1] * dl[1:2], axis=-1, keepdims=True))
           - jnp.exp(jnp.sum(dl[2:3] * dl[3:4], axis=-1, keepdims=True)) + lam_init)
    sub0 = (_iota((ROW_BLOCK, DIFF_V), 1) < DIFF_QK)

    def softmax(s):
        e = jnp.exp(s - jnp.max(s, axis=-1, keepdims=True))
        return e, jnp.sum(e, axis=-1, keepdims=True)

    def attn_block(r):
        rows = block_rows(r)
        keys = aligned((r // n_blk) * n_keys, n_keys)
        for h in range(DIFF_HEADS):
            cols = slice(h * DIFF_V, (h + 1) * DIFF_V)
            qh = q_ref[rows, cols]
            kh = k_ref[keys, cols]
            e0, z0 = softmax(_dot_nt(jnp.where(sub0, qh, jnp.zeros_like(qh)), kh))
            e1, z1 = softmax(_dot_nt(jnp.where(sub0, jnp.zeros_like(qh), qh), kh))
            w = e0 / z0 - lam * (e1 / z1)
            o = _dot(w.astype(BF16), v_ref[keys, cols])
            o = _row_rms(o) * dg_ref[0] * (1.0 - lam_init)
            mix_ref[rows, M_B + h * DIFF_V:M_B + (h + 1) * DIFF_V] = o.astype(BF16)

    blocks(attn_block)

    if not latent:
        st_ref[...] = jnp.zeros(st_ref.shape, F32)

    n_chunk = n_tok // GLA_CHUNK
    st_diag = (_iota((W_GLA, W_GLA), 0) // GLA_DV) == (_iota((W_GLA, W_GLA), 1) // GLA_DK)

    def gla_step(c, carry):
        for seq in range(n_par):
            for d in range(2):
                cc = seq * n_chunk + (c if d == 0 else n_chunk - 1 - c)
                rows = pl.ds(pl.multiple_of(cc * GLA_CHUNK, GLA_CHUNK), GLA_CHUNK)
                st = st_ref[seq, d]
                go_ref[d, rows, :] = go_ref[d, rows, :] + _dot_nt(gq_ref[d, rows, :], st.astype(BF16))
                upd = _dot_tn(gv_ref[rows, :], gke_ref[d, rows, :])
                st_ref[seq, d] = dec_ref[d, cc] * st + jnp.where(st_diag, upd, 0.0)
        return carry

    whole_sequence(lambda: lax.fori_loop(0, n_chunk, gla_step, 0))

    if not latent:
        for seq in range(n_par):
            for d in range(2):
                s_full = st_ref[seq, d].T
                for h in range(GLA_HEADS):
                    so_ref[seq, 0, d, h] = s_full[h * GLA_DK:(h + 1) * GLA_DK, h * GLA_DV:(h + 1) * GLA_DV]

    def finish_block(r, out_r):
        rows, out_rows = block_rows(r), block_rows(out_r)
        oc = _group_rms(go_ref[0, rows, :] + go_ref[1, rows, :], GLA_DV) * gg_ref[0]
        oc = oc * jax.nn.silu(gr_ref[rows, :])
        mix_ref[rows, M_C:M_C + W_GLA] = oc.astype(BF16)
        x1 = x_ref[rows, :] + mod[2:3, :] * _dot(mix_ref[rows, :], wout_ref[0])
        xo_ref[out_rows, :] = x1
        hn = _row_rms(x1) * n2_ref[0]
        hn = hn * (1.0 + mod[4:5, :]) + mod[3:4, :]
        idx1, idx2, w1, w2 = _route(hn, rwt_ref, rb_ref)
        slot1, slot2, counts = _local_slots(idx1, idx2)
        perm = _slot_one_hot(slot1, slot2, 1.0, 1.0).astype(BF16)
        _to_row_slabs(hs_ref, 2 * out_r * ROW_BLOCK, _dot(perm, hn.astype(BF16)))
        slot_ref[:, out_rows] = jnp.concatenate([slot1, slot2], axis=0).astype(jnp.int32)
        wt_ref[:, out_rows] = jnp.concatenate([w1, w2], axis=0)
        cnt_ref[out_r] = jnp.broadcast_to(counts, (N_EXPERTS, LANES)).astype(jnp.int32)

    if by_block:
        finish_block(pl.program_id(1), 0)
    else:
        for r in range(n_par * n_blk):
            finish_block(r, r)


def _mixer_call(l, n_tok, n_par, latent, x, mods_all, weights, extras, cache_bufs):
    n_seq = x.shape[0] // n_tok
    n_all = x.shape[0]
    assert n_seq % n_par == 0 and not (latent and n_par > 1)
    n_step_tok = n_par * n_tok
    n_keys = n_step_tok + (extras[1].shape[3] if latent else 0)
    n_chunk = n_step_tok // GLA_CHUNK
    lam_init = 0.8 - 0.6 * math.exp(-0.3 * l)

    single = pl.Buffered(1)

    def layer(arr):
        tail = arr.shape[1:]
        return pl.BlockSpec((1,) + tail, lambda s, *_r, _n=len(tail): (l,) + (0,) * _n, pipeline_mode=single)

    def const(arr):
        return pl.BlockSpec(arr.shape, lambda s, *_r, _n=arr.ndim: (0,) * _n, pipeline_mode=single)

    def tok_spec(width):
        return pl.BlockSpec((n_step_tok, width), lambda s, *_r: (s, 0))

    mod_row = (lambda s: 1 + s) if latent else (lambda s: 0)
    in_specs = [tok_spec(D_MODEL),
                pl.BlockSpec((1, 1, 6, D_MODEL), lambda s, *_r: (l, mod_row(s), 0, 0))]
    in_specs += [layer(w) for w in weights[:N_MIXER_WEIGHTS - 2]] + [const(w) for w in weights[-2:]]
    operands = [x, mods_all] + list(weights)
    if latent:
        ck, cv, st0, cos, sin = extras
        in_specs += [
            pl.BlockSpec((1, 1) + ck.shape[2:], lambda s, *_r: (s, l, 0, 0, 0, 0)),
            pl.BlockSpec((1, 1) + cv.shape[2:], lambda s, *_r: (s, l, 0, 0, 0)),
            pl.BlockSpec((1, 1) + st0.shape[2:], lambda s, *_r: (s, l, 0, 0, 0)),
            const(cos), const(sin),
        ]
        operands += [ck, cv, st0, cos, sin]
    n_in = len(operands)
    in_specs += [pl.BlockSpec(memory_space=pl.ANY)] * len(cache_bufs)
    operands += list(cache_bufs)

    tiles_per_step = n_step_tok // ROW_BLOCK
    out_shape = [
        jax.ShapeDtypeStruct((n_all, D_MODEL), F32),
        jax.ShapeDtypeStruct((2 * n_all * ROW_SLABS, LANES), F32),
        jax.ShapeDtypeStruct((2, n_all), jnp.int32),
        jax.ShapeDtypeStruct((2, n_all), F32),
        jax.ShapeDtypeStruct((n_all // ROW_BLOCK, N_EXPERTS, LANES), jnp.int32),
    ]
    if latent:
        grid = (n_seq, tiles_per_step)
        out_tok, out_tiles = ROW_BLOCK, 1
        at = lambda s, r: s * tiles_per_step + r
    else:
        grid = (n_seq // n_par,)
        out_tok, out_tiles = n_step_tok, tiles_per_step
        at = lambda s: s
    out_specs = [
        pl.BlockSpec((out_tok, D_MODEL), lambda *g: (at(*g), 0)),
        pl.BlockSpec((2 * out_tok * ROW_SLABS, LANES), lambda *g: (at(*g), 0)),
        pl.BlockSpec((2, out_tok), lambda *g: (0, at(*g))),
        pl.BlockSpec((2, out_tok), lambda *g: (0, at(*g))),
        pl.BlockSpec((out_tiles, N_EXPERTS, LANES), lambda *g: (at(*g), 0, 0)),
    ]
    n_shared_out = len(out_shape)
    aliases = {}
    if not latent:
        out_shape += [
            jax.ShapeDtypeStruct((n_seq, DEPTH, DIFF_HEADS, 2, n_tok, DIFF_QK), F32),
            jax.ShapeDtypeStruct((n_seq, DEPTH, DIFF_HEADS, n_tok, DIFF_V), F32),
            jax.ShapeDtypeStruct((n_seq, DEPTH, 2, GLA_HEADS, GLA_DK, GLA_DV), F32),
        ]
        n_lay, lay = (1, l) if cache_bufs else (DEPTH, 0)
        out_specs += [
            pl.BlockSpec((n_par, n_lay, DIFF_HEADS, 2, n_tok, DIFF_QK), lambda s: (s, lay, 0, 0, 0, 0)),
            pl.BlockSpec((n_par, n_lay, DIFF_HEADS, n_tok, DIFF_V), lambda s: (s, lay, 0, 0, 0)),
            pl.BlockSpec((n_par, n_lay, 2, GLA_HEADS, GLA_DK, GLA_DV), lambda s: (s, lay, 0, 0, 0, 0)),
        ]
        aliases = {n_in + j: n_shared_out + j for j in range(len(cache_bufs))}
    scratch = [
        pltpu.VMEM((ROW_BLOCK, D_PROJ_PAD), F32),
        pltpu.VMEM((n_step_tok, D_MODEL), BF16),
        pltpu.VMEM((n_step_tok, W_QK), BF16),
        pltpu.VMEM((n_keys, W_QK), BF16),
        pltpu.VMEM((n_keys, W_QK), BF16),
        pltpu.VMEM((2, n_step_tok, W_GLA), BF16),
        pltpu.VMEM((2, n_step_tok, W_GLA), BF16),
        pltpu.VMEM((n_step_tok, W_GLA), BF16),
        pltpu.VMEM((n_step_tok, W_GLA), F32),
        pltpu.VMEM((2, n_chunk, 1, W_GLA), F32),
        pltpu.VMEM((2, n_step_tok, W_GLA), F32),
        pltpu.VMEM((n_par, 2, W_GLA, W_GLA), F32),
    ]
    return pl.pallas_call(
        functools.partial(_mixer_kernel, n_tok, n_par, latent, len(cache_bufs), lam_init),
        grid=grid,
        in_specs=in_specs,
        out_specs=out_specs,
        out_shape=out_shape,
        scratch_shapes=scratch,
        input_output_aliases=aliases,
        compiler_params=pltpu.CompilerParams(
            dimension_semantics=("arbitrary",) * len(grid), vmem_limit_bytes=MIXER_VMEM_LIMIT),
        name="mixer_latent" if latent else "mixer_context",
    )(*operands)


PAIR_BLOCK = 2 * ROW_BLOCK
COPY_SIZES = tuple(ROW_BLOCK >> k for k in range(ROW_BLOCK.bit_length()))
LARGE_COPY = 64
GATHER_AHEAD = 2
GATHER_SLOTS = GATHER_AHEAD + 1


def _segment_copies(n_rows, make_copy, act):
    def copy_if_set(size):
        @pl.when((n_rows & size) != 0)
        def _():
            act(make_copy(n_rows & (-2 * size), size))

    n_large = COPY_SIZES.index(LARGE_COPY) + 1

    @pl.when(n_rows >= LARGE_COPY)
    def _():
        for size in COPY_SIZES[:n_large]:
            copy_if_set(size)

    for size in COPY_SIZES[n_large:]:
        copy_if_set(size)


def _start(copy):
    copy.start()


def _wait(copy):
    copy.wait()


def _slab_rows(first_row, n_rows, slab):
    return pl.ds(first_row * ROW_SLABS + slab, n_rows, stride=ROW_SLABS)


def _to_row_slabs(ref, first_row, value):
    for s in range(ROW_SLABS):
        ref[_slab_rows(first_row, value.shape[0], s), :] = value[:, s * LANES:(s + 1) * LANES]


def _from_row_slabs(ref, first_row, n_rows):
    return jnp.concatenate([ref[_slab_rows(first_row, n_rows, s), :] for s in range(ROW_SLABS)], axis=-1)


SLAB_PAIR_W = 2 * LANES
N_SLAB_PAIRS = ROW_SLABS // 2


def _slab_pair_cols(g):
    return slice(g * SLAB_PAIR_W, (g + 1) * SLAB_PAIR_W)


def _load_slab_pair(ref, first_row, n_rows, g):
    return jnp.concatenate([ref[_slab_rows(first_row, n_rows, s), :] for s in (2 * g, 2 * g + 1)], axis=-1)


def _row_span(ref, first_row, n_rows):
    return ref.at[pl.ds(pl.multiple_of(first_row * ROW_SLABS, ROW_SLABS), n_rows * ROW_SLABS)]


def _two_streams(n_first_tiles):
    def first(i, *_):
        return (jnp.minimum(i, n_first_tiles - 1), 0)

    def second(i, *_):
        return (jnp.maximum(i - n_first_tiles, 0), 0)

    return first, second


N_EXPERT_TABLES = 10
OUT_SLOTS = 2


def _expert_kernel(n_ctx_tiles, n_tiles_max, te_ref, first_ref, rows_ref, jlo_ref, jhi_ref, cpre_ref, cnt_ref,
                   lofs_ref, tile0_ref, ntile_ref, hs_c_ref, hs_l_ref, w1_ref, w3_ref, w2_ref, ys_ref,
                   xbuf_ref, obuf_ref, w1b_ref, w3b_ref, w2b_ref, sem, out_sem):
    expert = pl.program_id(0)
    n_tiles = tile0_ref[N_EXPERTS]

    def gather(t, act):
        slot = t % GATHER_SLOTS
        e, first = te_ref[t], first_ref[t]
        last = first + rows_ref[t]

        def segment_of(hs_ref, first_tile):
            def body(j, carry):
                k = j * N_EXPERTS + e
                seg_first = cpre_ref[k]
                lo = jnp.maximum(seg_first, first)
                n = jnp.maximum(jnp.minimum(seg_first + cnt_ref[k], last) - lo, 0)
                src = (j - first_tile) * PAIR_BLOCK + lofs_ref[k] + (lo - seg_first)
                dst = slot * ROW_BLOCK + lo - first
                _segment_copies(n, lambda done, size: pltpu.make_async_copy(
                    _row_span(hs_ref, src + done, size), _row_span(xbuf_ref, dst + done, size), sem.at[slot]), act)
                return carry
            return body

        jlo, jhi = jlo_ref[t], jhi_ref[t]
        lax.fori_loop(jnp.minimum(jlo, n_ctx_tiles), jnp.minimum(jhi, n_ctx_tiles), segment_of(hs_c_ref, 0), 0)
        lax.fori_loop(jnp.maximum(jlo, n_ctx_tiles), jnp.maximum(jhi, n_ctx_tiles),
                      segment_of(hs_l_ref, n_ctx_tiles), 0)

    def out_copy(t, oslot):
        return pltpu.make_async_copy(
            _row_span(obuf_ref, oslot * ROW_BLOCK, ROW_BLOCK), _row_span(ys_ref, t * ROW_BLOCK, ROW_BLOCK),
            out_sem.at[oslot])

    @pl.when(expert == 0)
    def _():
        xbuf_ref[...] = jnp.zeros(xbuf_ref.shape, F32)
        for t in range(GATHER_AHEAD):
            gather(t, _start)

    w1b_ref[...] = w1_ref[0, 0].astype(BF16)
    w3b_ref[...] = w3_ref[0, 0].astype(BF16)
    w2b_ref[...] = w2_ref[0, 0].astype(BF16)
    tile0, n_own = tile0_ref[expert], ntile_ref[expert]

    def tile_body(k, carry):
        t = tile0 + k
        slot, oslot = t % GATHER_SLOTS, t % OUT_SLOTS

        @pl.when(t + GATHER_AHEAD < n_tiles)
        def _():
            gather(t + GATHER_AHEAD, _start)

        n_rows = rows_ref[t]
        _segment_copies(n_rows, lambda done, size: pltpu.make_async_copy(
            _row_span(hs_c_ref, done, size), _row_span(xbuf_ref, slot * ROW_BLOCK + done, size), sem.at[slot]), _wait)

        @pl.when(t >= OUT_SLOTS)
        def _():
            out_copy(t, oslot).wait()

        live = _iota((ROW_BLOCK, D_MODEL), 0) < n_rows
        x = jnp.where(live, _from_row_slabs(xbuf_ref, slot * ROW_BLOCK, ROW_BLOCK), 0.0).astype(BF16)
        hid = jax.nn.silu(_dot(x, w1b_ref[...])) * _dot(x, w3b_ref[...])
        _to_row_slabs(obuf_ref, oslot * ROW_BLOCK, _dot(hid.astype(BF16), w2b_ref[...]))
        out_copy(t, oslot).start()
        return carry

    lax.fori_loop(0, n_own, tile_body, 0)

    @pl.when(expert == N_EXPERTS - 1)
    def _():
        for oslot in range(OUT_SLOTS):
            @pl.when(n_tiles > oslot)
            def _():
                out_copy(0, oslot).wait()
        obuf_ref[...] = jnp.zeros(obuf_ref.shape, F32)

        def fill(t, carry):
            out_copy(t, 0).start()
            out_copy(t, 0).wait()
            return carry

        lax.fori_loop(n_tiles, n_tiles_max, fill, 0)


def _expert_call(l, plan, hs_c, hs_l, w1, w3, w2):
    tables = plan["expert_tables"]
    n_tiles_max = tables[0].shape[0]
    n_ctx_tiles = hs_c.shape[0] // (PAIR_BLOCK * ROW_SLABS)

    def weight(shape):
        return pl.BlockSpec((1, 1) + shape, lambda e, *_: (l, e, 0, 0))

    return pl.pallas_call(
        functools.partial(_expert_kernel, n_ctx_tiles, n_tiles_max),
        grid_spec=pltpu.PrefetchScalarGridSpec(
            num_scalar_prefetch=N_EXPERT_TABLES,
            grid=(N_EXPERTS,),
            in_specs=[pl.BlockSpec(memory_space=pl.ANY), pl.BlockSpec(memory_space=pl.ANY),
                      weight((D_MODEL, D_EXPERT)), weight((D_MODEL, D_EXPERT)), weight((D_EXPERT, D_MODEL))],
            out_specs=pl.BlockSpec(memory_space=pl.ANY),
            scratch_shapes=[pltpu.VMEM((GATHER_SLOTS * ROW_BLOCK * ROW_SLABS, LANES), F32),
                            pltpu.VMEM((OUT_SLOTS * ROW_BLOCK * ROW_SLABS, LANES), F32),
                            pltpu.VMEM((D_MODEL, D_EXPERT), BF16), pltpu.VMEM((D_MODEL, D_EXPERT), BF16),
                            pltpu.VMEM((D_EXPERT, D_MODEL), BF16),
                            pltpu.SemaphoreType.DMA((GATHER_SLOTS,)), pltpu.SemaphoreType.DMA((OUT_SLOTS,))],
        ),
        out_shape=jax.ShapeDtypeStruct((n_tiles_max * ROW_BLOCK * ROW_SLABS, LANES), F32),
        compiler_params=pltpu.CompilerParams(
            dimension_semantics=("arbitrary",), vmem_limit_bytes=SMALL_KERNEL_VMEM_LIMIT),
        name="moe_experts",
    )(*tables, hs_c, hs_l, w1, w3, w2)


N_COMBINE_TABLES = 4
COMBINE_TILES_PER_STEP = 2


def _combine_kernel(n_ctx_steps, cnt_ref, cpre_ref, lofs_ref, starts_ref, x_c_ref, x_l_ref, slot_c_ref, slot_l_ref,
                    wt_c_ref, wt_l_ref, mod_ref, ys_ref, xo_c_ref, xo_l_ref, buf_ref, sem):
    step = pl.program_id(0)
    n_tiles = pl.num_programs(0) * COMBINE_TILES_PER_STEP

    def collect(t, act):
        slot = t % GATHER_SLOTS

        def body(e, carry):
            k = t * N_EXPERTS + e
            src, dst = starts_ref[e] + cpre_ref[k], slot * PAIR_BLOCK + lofs_ref[k]
            _segment_copies(cnt_ref[k], lambda done, size: pltpu.make_async_copy(
                _row_span(ys_ref, src + done, size), _row_span(buf_ref, dst + done, size), sem.at[slot]), act)
            return carry

        lax.fori_loop(0, N_EXPERTS, body, 0)

    @pl.when(step == 0)
    def _():
        for t in range(GATHER_AHEAD):
            collect(t, _start)

    gate = mod_ref[0, 0, 5:6, :]

    def finish(part, first_row, x_ref, slot_ref, wt_ref, xo_ref):
        tokens = pl.ds(part * ROW_BLOCK, ROW_BLOCK)
        slots, wts = slot_ref[:, tokens].astype(F32), wt_ref[:, tokens]
        slot1, slot2 = slots[0:1], slots[1:2]
        weight_of_row = jnp.sum(_slot_one_hot(slot1, slot2, wts[0:1], wts[1:2]), axis=1, keepdims=True)
        gather_rows = _slot_one_hot(slot1, slot2, 1.0, 1.0).T.astype(BF16)
        for g in range(N_SLAB_PAIRS):
            cols = _slab_pair_cols(g)
            hi, lo = _split2(_load_slab_pair(buf_ref, first_row, PAIR_BLOCK, g) * weight_of_row)
            y = _dot(gather_rows, hi) + _dot(gather_rows, lo)
            xo_ref[tokens, cols] = x_ref[tokens, cols] + gate[:, cols] * y

    for part in range(COMBINE_TILES_PER_STEP):
        j = step * COMBINE_TILES_PER_STEP + part

        @pl.when(j + GATHER_AHEAD < n_tiles)
        def _():
            collect(j + GATHER_AHEAD, _start)

        slot = j % GATHER_SLOTS
        for piece in range(PAIR_BLOCK // ROW_BLOCK):
            pltpu.make_async_copy(
                _row_span(ys_ref, piece * ROW_BLOCK, ROW_BLOCK),
                _row_span(buf_ref, slot * PAIR_BLOCK + piece * ROW_BLOCK, ROW_BLOCK), sem.at[slot]).wait()
        @pl.when(step < n_ctx_steps)
        def _():
            finish(part, slot * PAIR_BLOCK, x_c_ref, slot_c_ref, wt_c_ref, xo_c_ref)

        @pl.when(step >= n_ctx_steps)
        def _():
            finish(part, slot * PAIR_BLOCK, x_l_ref, slot_l_ref, wt_l_ref, xo_l_ref)


def _combine_call(l, plan, x_c, x_l, slot_c, slot_l, wt_c, wt_l, mods_all, mod_row_of_tile, ys):
    step_rows = COMBINE_TILES_PER_STEP * ROW_BLOCK
    n_steps = (x_c.shape[0] + x_l.shape[0]) // step_rows
    n_ctx_steps = x_c.shape[0] // step_rows
    first, second = _two_streams(n_ctx_steps)

    def lanes(index_map):
        return lambda i, *_: index_map(i)[::-1]

    return pl.pallas_call(
        functools.partial(_combine_kernel, n_ctx_steps),
        grid_spec=pltpu.PrefetchScalarGridSpec(
            num_scalar_prefetch=N_COMBINE_TABLES,
            grid=(n_steps,),
            in_specs=[pl.BlockSpec((step_rows, D_MODEL), first),
                      pl.BlockSpec((step_rows, D_MODEL), second),
                      pl.BlockSpec((2, step_rows), lanes(first)),
                      pl.BlockSpec((2, step_rows), lanes(second)),
                      pl.BlockSpec((2, step_rows), lanes(first)),
                      pl.BlockSpec((2, step_rows), lanes(second)),
                      pl.BlockSpec((1, 1, 6, D_MODEL),
                                   lambda i, *_: (l, mod_row_of_tile(i * COMBINE_TILES_PER_STEP), 0, 0)),
                      pl.BlockSpec(memory_space=pl.ANY)],
            out_specs=[pl.BlockSpec((step_rows, D_MODEL), first),
                       pl.BlockSpec((step_rows, D_MODEL), second)],
            scratch_shapes=[pltpu.VMEM((GATHER_SLOTS * PAIR_BLOCK * ROW_SLABS, LANES), F32),
                            pltpu.SemaphoreType.DMA((GATHER_SLOTS,))],
        ),
        out_shape=[jax.ShapeDtypeStruct(x_c.shape, F32), jax.ShapeDtypeStruct(x_l.shape, F32)],
        compiler_params=pltpu.CompilerParams(
            dimension_semantics=("arbitrary",), vmem_limit_bytes=SMALL_KERNEL_VMEM_LIMIT),
        name="moe_combine",
    )(*plan["combine_tables"], x_c, x_l, slot_c, slot_l, wt_c, wt_l, mods_all, ys)


def _moe_plan(cnt):
    n_tok_tiles = cnt.shape[0]
    n_tiles = n_tok_tiles * PAIR_BLOCK // ROW_BLOCK + N_EXPERTS
    lofs = jnp.cumsum(cnt, axis=1) - cnt
    cpre = jnp.cumsum(cnt, axis=0) - cnt
    counts = jnp.sum(cnt, axis=0)
    padded = (counts + ROW_BLOCK - 1) // ROW_BLOCK * ROW_BLOCK
    ends = jnp.cumsum(padded)
    starts = ends - padded
    tile_start = jnp.arange(n_tiles, dtype=jnp.int32) * ROW_BLOCK
    tile_expert = jnp.minimum(
        jnp.sum((tile_start[:, None] >= ends[None, :]).astype(jnp.int32), axis=1), N_EXPERTS - 1)
    hot = tile_expert[:, None] == jnp.arange(N_EXPERTS, dtype=jnp.int32)[None, :]
    first = tile_start - jnp.sum(jnp.where(hot, starts[None, :], 0), axis=1)
    rows = jnp.clip(jnp.sum(jnp.where(hot, counts[None, :], 0), axis=1) - first, 0, ROW_BLOCK)
    seg_first = jnp.sum(jnp.where(hot[:, None, :], cpre[None, :, :], 0), axis=2)
    seg_rows = jnp.sum(jnp.where(hot[:, None, :], cnt[None, :, :], 0), axis=2)
    overlap = (seg_first < (first + rows)[:, None]) & (seg_first + seg_rows > first[:, None])
    j = jnp.arange(n_tok_tiles, dtype=jnp.int32)[None, :]
    jlo = jnp.min(jnp.where(overlap, j, n_tok_tiles), axis=1)
    jhi = jnp.max(jnp.where(overlap, j + 1, 0), axis=1)
    i32 = lambda a: a.astype(jnp.int32).reshape(-1)
    tile0 = jnp.concatenate([starts, ends[-1:]]) // ROW_BLOCK
    return {
        "expert_tables": tuple(i32(a) for a in (tile_expert, first, rows, jlo, jhi, cpre, cnt, lofs,
                                                tile0, padded // ROW_BLOCK)),
        "combine_tables": tuple(i32(a) for a in (cnt, cpre, lofs, starts)),
    }


def _rope_tables(n_tok):
    n_rows = n_tok // GRID_W
    pos_r = jnp.repeat(jnp.arange(n_rows), GRID_W)
    pos_c = jnp.tile(jnp.arange(GRID_W), n_rows)
    half = DIFF_QK // 2
    nf = half // 2
    freqs = ROPE_BASE ** (-jnp.arange(nf, dtype=F32) / nf)

    def tables(pos):
        ang = pos.astype(F32)[:, None] * freqs
        cos, sin = jnp.cos(ang), jnp.sin(ang)
        return jnp.concatenate([cos, cos], axis=-1), jnp.concatenate([-sin, sin], axis=-1)

    cos_r, sin_r = tables(pos_r)
    cos_c, sin_c = tables(pos_c)
    cos = jnp.concatenate([cos_r, cos_c], axis=-1)
    sin = jnp.concatenate([sin_r, sin_c], axis=-1)
    return jnp.concatenate([cos, cos], axis=-1), jnp.concatenate([sin, sin], axis=-1)


CAST_ROWS = 256


def _cast_pad_kernel(w_ref, o_ref):
    n_in, n_out = w_ref.shape[2], o_ref.shape[2]
    o_ref[0, :, :n_in] = w_ref[0].astype(BF16)
    if n_out > n_in:
        o_ref[0, :, n_in:] = jnp.zeros((o_ref.shape[1], n_out - n_in), BF16)


def _cast_pad_call(w, width):
    depth, rows, n = w.shape
    return pl.pallas_call(
        _cast_pad_kernel,
        grid=(depth, rows // CAST_ROWS),
        in_specs=[pl.BlockSpec((1, CAST_ROWS, n), lambda l, i: (l, i, 0))],
        out_specs=pl.BlockSpec((1, CAST_ROWS, width), lambda l, i: (l, i, 0)),
        out_shape=jax.ShapeDtypeStruct((depth, rows, width), BF16),
        compiler_params=pltpu.CompilerParams(dimension_semantics=("arbitrary", "arbitrary")),
        name="cast_pad",
    )(w)


def _mixer_weights(w_in, w_out, sgu_w, sgu_b, q_norm_g, k_norm_g, diff_lambda, diff_norm_g, gla_w2, gla_b,
                   gla_norm_g, norm1_g, norm2_g, router_w, router_bias):
    w_in_pad = _cast_pad_call(w_in, D_PROJ_PAD)
    w2cat = jnp.zeros((DEPTH, LANES, 2 * W_GLA), F32)
    w2cat = w2cat.at[:, 0:GLA_RANK, 0:W_GLA].set(gla_w2[:, 0]).at[:, GLA_RANK:2 * GLA_RANK, W_GLA:].set(gla_w2[:, 1])
    return (
        norm1_g[:, None, :], norm2_g[:, None, :], w_in_pad, _cast_pad_call(w_out, D_MODEL),
        sgu_w.astype(BF16), jnp.repeat(sgu_b.transpose(0, 2, 1), SGU_GROUP_W, axis=2),
        jnp.tile(q_norm_g, (1, W_QK // DIFF_QK))[:, None, :], jnp.tile(k_norm_g, (1, W_QK // DIFF_QK))[:, None, :],
        diff_lambda, diff_norm_g[:, None, :],
        w2cat.astype(BF16), gla_b.reshape(DEPTH, 1, 2 * W_GLA), jnp.tile(gla_norm_g, (1, GLA_HEADS))[:, None, :],
        router_w.T, router_bias[:, None],
    )


def kernel(x_prompt, x_sample, cache_k, cache_v, state_gla, c, c_ctx, w_in, w_out, sgu_w, sgu_b, q_norm_g, k_norm_g,
           diff_lambda, diff_norm_g, gla_w2, gla_b, gla_norm_g, norm1_g, norm2_g, ada_w, ada_b, router_w, router_bias,
           moe_w1, moe_w3, moe_w2):
    n_ctx_seq, ctx_len, _ = x_prompt.shape
    n_lat_seq, lat_len, _ = x_sample.shape
    n_ctx_tok = n_ctx_seq * ctx_len
    n_lat_tok = n_lat_seq * lat_len
    ctx_tiles = n_ctx_tok // ROW_BLOCK
    lat_tiles_per_seq = lat_len // ROW_BLOCK

    n_cond = 1 + n_lat_seq
    cond_t = jnp.zeros((D_MODEL, SUBLANES), F32).at[:, 0].set(c_ctx).at[:, 1:n_cond].set(c.T)
    mods_all = _adaln_call(cond_t, n_cond, ada_w, ada_b)[:, :n_cond].reshape(DEPTH, n_cond, 6, D_MODEL)
    weights = _mixer_weights(w_in, w_out, sgu_w, sgu_b, q_norm_g, k_norm_g, diff_lambda, diff_norm_g, gla_w2, gla_b,
                             gla_norm_g, norm1_g, norm2_g, router_w, router_bias)

    st_all = jnp.einsum('bldhkv,hg->bldhvgk', state_gla, jnp.eye(GLA_HEADS, dtype=F32)).reshape(
        n_lat_seq, DEPTH, 2, W_GLA, W_GLA)
    cos, sin = _rope_tables(lat_len)
    extras = (cache_k, cache_v, st_all, cos, sin)

    def mod_row_of_tile(i):
        return jnp.where(i < ctx_tiles, 0, 1 + (i - ctx_tiles) // lat_tiles_per_seq)

    x_c = x_prompt.reshape(n_ctx_tok, D_MODEL)
    x_l = x_sample.reshape(n_lat_tok, D_MODEL)
    cache_bufs = ()
    for l in range(DEPTH):
        ctx_par = 1 if l == 0 else CTX_SEQS_PER_STEP
        x1_c, hs_c, slot_c, wt_c, cnt_c, *cache_bufs = _mixer_call(
            l, ctx_len, ctx_par, False, x_c, mods_all, weights, None, tuple(cache_bufs))
        x1_l, hs_l, slot_l, wt_l, cnt_l = _mixer_call(l, lat_len, 1, True, x_l, mods_all, weights, extras, ())
        plan = _moe_plan(jnp.concatenate([cnt_c[:, :, 0], cnt_l[:, :, 0]], axis=0))
        ys = _expert_call(l, plan, hs_c, hs_l, moe_w1, moe_w3, moe_w2)
        x_c, x_l = _combine_call(l, plan, x1_c, x1_l, slot_c, slot_l, wt_c, wt_l, mods_all, mod_row_of_tile, ys)

    new_k, new_v, new_s = cache_bufs
    return (x_c.reshape(x_prompt.shape), x_l.reshape(x_sample.shape), new_k, new_v, new_s)
```

```python
import functools
import math

import jax
import jax.numpy as jnp
from jax import lax
from jax.experimental import pallas as pl
from jax.experimental.pallas import tpu as pltpu

F32 = jnp.float32
BF16 = jnp.bfloat16

D_MODEL = 1024
DEPTH = 4
GRID_W = 64
SGU_GROUPS = 4
SGU_GROUP_W = 64
SGU_W = SGU_GROUPS * SGU_GROUP_W
SGU_CHUNK = 128
DIFF_HEADS = 4
DIFF_QK = 64
DIFF_V = 2 * DIFF_QK
ROPE_BASE = 10000.0
GLA_HEADS = 4
GLA_DK = 64
GLA_DV = 64
GLA_RANK = 16
GLA_GATE_NORM = 16.0
GLA_CHUNK = 64
N_EXPERTS = 16
N_GROUPS = 4
EXPERTS_PER_GROUP = N_EXPERTS // N_GROUPS
D_EXPERT = 512
EPS = 1e-6

LANES = 128
SUBLANES = 8
MXU_DIM = 256
V7X_VMEM_BYTES = 64 * 1024 * 1024
MIXER_VMEM_LIMIT = V7X_VMEM_BYTES * 7 // 8
SMALL_KERNEL_VMEM_LIMIT = V7X_VMEM_BYTES * 5 // 8

ROW_BLOCK = MXU_DIM
ROW_SLABS = D_MODEL // LANES

W_QK = DIFF_HEADS * 2 * DIFF_QK
W_GLA = GLA_HEADS * GLA_DK
C_AU, C_AV = 0, SGU_W
C_BQ = C_AV + SGU_W
C_BK, C_BV = C_BQ + W_QK, C_BQ + 2 * W_QK
C_CQ = C_BV + DIFF_HEADS * DIFF_V
C_CK, C_CV, C_CR, C_LR = C_CQ + W_GLA, C_CQ + 2 * W_GLA, C_CQ + 3 * W_GLA, C_CQ + 4 * W_GLA
D_PROJ_MAIN = C_LR
D_PROJ_PAD = D_PROJ_MAIN + LANES
M_A, M_B, M_C = 0, SGU_W, SGU_W + DIFF_HEADS * DIFF_V


def _split2(x):
    hi = x.astype(BF16)
    lo = (x - hi.astype(F32)).astype(BF16)
    return hi, lo


def _split3(x):
    hi = x.astype(BF16)
    r = x - hi.astype(F32)
    mid = r.astype(BF16)
    lo = (r - mid.astype(F32)).astype(BF16)
    return hi, mid, lo


def _dot(a, b):
    return jnp.dot(a, b, preferred_element_type=F32)


def _dot_nt(a, b):
    return lax.dot_general(a, b, (((1,), (1,)), ((), ())), preferred_element_type=F32)


def _dot_tn(a, b):
    return lax.dot_general(a, b, (((0,), (0,)), ((), ())), preferred_element_type=F32)


def _iota(shape, dim):
    return lax.broadcasted_iota(jnp.int32, shape, dim)


def _block_ones(width, block):
    r = _iota((width, width), 0) // block
    c = _iota((width, width), 1) // block
    return (r == c)


def _group_sum(z, block):
    width = z.shape[-1]
    outs = []
    for s in range(0, width, MXU_DIM):
        w = min(MXU_DIM, width - s)
        ones = _block_ones(w, block).astype(BF16)
        hi, lo = _split2(z[:, s:s + w])
        outs.append(_dot(hi, ones) + _dot(lo, ones))
    return outs[0] if len(outs) == 1 else jnp.concatenate(outs, axis=-1)


def _group_rms(z, block):
    ms = _group_sum(z * z, block) * (1.0 / block)
    return z * lax.rsqrt(ms + EPS)


def _row_rms(z):
    return z * lax.rsqrt(jnp.mean(z * z, axis=-1, keepdims=True) + EPS)


def _log_sigmoid(x):
    return jnp.minimum(x, 0.0) - jnp.log(1.0 + jnp.exp(-jnp.abs(x)))


ADA_COLS = 1536


def _adaln_kernel(n_cond, cond_t_ref, w_ref, b_ref, o_ref):
    sc = jax.nn.silu(cond_t_ref[...])
    w = w_ref[0]
    rows = [jnp.sum(sc[:, r:r + 1] * w, axis=0, keepdims=True) + b_ref[0] for r in range(n_cond)]
    o_ref[0] = jnp.concatenate(rows + [jnp.zeros((SUBLANES - n_cond, w.shape[1]), F32)], axis=0)


def _adaln_call(cond_t, n_cond, ada_w, ada_b):
    n_col = 6 * D_MODEL // ADA_COLS
    return pl.pallas_call(
        functools.partial(_adaln_kernel, n_cond),
        grid=(DEPTH, n_col),
        in_specs=[
            pl.BlockSpec((D_MODEL, SUBLANES), lambda l, j: (0, 0)),
            pl.BlockSpec((1, D_MODEL, ADA_COLS), lambda l, j: (l, 0, j)),
            pl.BlockSpec((1, 1, ADA_COLS), lambda l, j: (l, 0, j)),
        ],
        out_specs=pl.BlockSpec((1, SUBLANES, ADA_COLS), lambda l, j: (l, 0, j)),
        out_shape=jax.ShapeDtypeStruct((DEPTH, SUBLANES, 6 * D_MODEL), F32),
        compiler_params=pltpu.CompilerParams(
            dimension_semantics=("arbitrary", "arbitrary"), vmem_limit_bytes=SMALL_KERNEL_VMEM_LIMIT),
        name="adaln",
    )(cond_t, ada_w, ada_b.reshape(DEPTH, 1, 6 * D_MODEL))


def _route(hn, rwt_ref, rb_ref):
    h_hi, h_lo = _split2(hn)
    rw = rwt_ref[...]
    rw_hi = rw.astype(BF16)
    rw_lo = (rw - rw_hi.astype(F32)).astype(BF16)
    logits = _dot_nt(rw_hi, h_hi) + _dot_nt(rw_hi, h_lo) + _dot_nt(rw_lo, h_hi)
    aff = jax.nn.sigmoid(logits)
    sel = aff + rb_ref[...]
    n_tok = sel.shape[1]

    def top2_sum(a, b, c, d):
        hi1, lo1 = jnp.maximum(a, b), jnp.minimum(a, b)
        hi2, lo2 = jnp.maximum(c, d), jnp.minimum(c, d)
        return jnp.maximum(hi1, hi2) + jnp.maximum(jnp.minimum(hi1, hi2), jnp.maximum(lo1, lo2))

    scores = []
    for g in range(N_GROUPS):
        rows = [sel[EXPERTS_PER_GROUP * g + j:EXPERTS_PER_GROUP * g + j + 1, :] for j in range(EXPERTS_PER_GROUP)]
        scores.append(top2_sum(*rows))
    best = jnp.zeros((1, n_tok), jnp.int32)
    best_score = scores[0]
    for g in range(1, N_GROUPS):
        upd = scores[g] > best_score
        best = jnp.where(upd, g, best)
        best_score = jnp.where(upd, scores[g], best_score)

    eid_i = _iota((N_EXPERTS, n_tok), 0)
    eid = eid_i.astype(F32)
    neg = jnp.float32(-jnp.inf)
    msel = jnp.where(eid_i // EXPERTS_PER_GROUP == best, sel, neg)
    m1 = jnp.max(msel, axis=0, keepdims=True)
    idx1 = jnp.min(jnp.where(msel == m1, eid, float(N_EXPERTS)), axis=0, keepdims=True)
    msel2 = jnp.where(eid == idx1, neg, msel)
    m2 = jnp.max(msel2, axis=0, keepdims=True)
    idx2 = jnp.min(jnp.where(msel2 == m2, eid, float(N_EXPERTS)), axis=0, keepdims=True)
    w1 = jnp.sum(jnp.where(eid == idx1, aff, 0.0), axis=0, keepdims=True)
    w2 = jnp.sum(jnp.where(eid == idx2, aff, 0.0), axis=0, keepdims=True)
    wsum = w1 + w2
    return idx1.astype(jnp.int32), idx2.astype(jnp.int32), w1 / wsum, w2 / wsum


def _local_slots(idx1, idx2):
    n_tok = idx1.shape[1]
    eid = _iota((N_EXPERTS, n_tok), 0)
    hot1, hot2 = eid == idx1, eid == idx2
    hot = jnp.where(hot1, 1.0, jnp.where(hot2, 1.0, 0.0))
    earlier = jnp.where(_iota((n_tok, n_tok), 0) < _iota((n_tok, n_tok), 1), 1.0, 0.0).astype(BF16)
    before_in_expert = _dot(hot.astype(BF16), earlier)
    counts = jnp.sum(hot, axis=1, keepdims=True)
    lower = jnp.where(_iota((N_EXPERTS, N_EXPERTS), 1) < _iota((N_EXPERTS, N_EXPERTS), 0), 1.0, 0.0).astype(BF16)
    first_slot = _dot(lower, jnp.broadcast_to(counts, (N_EXPERTS, LANES)).astype(BF16))[:, 0:1]
    slot = before_in_expert + first_slot
    slot1 = jnp.sum(jnp.where(hot1, slot, 0.0), axis=0, keepdims=True)
    slot2 = jnp.sum(jnp.where(hot2, slot, 0.0), axis=0, keepdims=True)
    return slot1, slot2, counts


def _slot_one_hot(slot1, slot2, v1, v2):
    n_tok = slot1.shape[1]
    row = _iota((2 * n_tok, n_tok), 0).astype(F32)
    return jnp.where(row == slot1, v1, jnp.where(row == slot2, v2, 0.0))


N_MIXER_WEIGHTS = 15
CTX_SEQS_PER_STEP = 2
MAX_INLINE_BLOCKS = 2


def _mixer_kernel(n_tok, n_par, latent, n_alias, lam_init, *refs):
    it = iter(refs)
    x_ref, mod_ref = next(it), next(it)
    (n1_ref, n2_ref, win_ref, wout_ref, sw_ref, sb_ref, qg_ref, kg_ref, dl_ref, dg_ref,
     w2c_ref, gb_ref, gg_ref, rwt_ref, rb_ref) = (next(it) for _ in range(N_MIXER_WEIGHTS))
    if latent:
        ck_ref, cv_ref, st0_ref, cos_ref, sin_ref = (next(it) for _ in range(5))
    for _ in range(n_alias):
        next(it)
    xo_ref, hs_ref, slot_ref, wt_ref, cnt_ref = (next(it) for _ in range(5))
    if not latent:
        ko_ref, vo_ref, so_ref = (next(it) for _ in range(3))
    proj_ref, mix_ref, q_ref, k_ref, v_ref = (next(it) for _ in range(5))
    gq_ref, gke_ref, gv_ref, gr_ref, dec_ref, go_ref, st_ref = (next(it) for _ in range(7))

    n_blk = n_tok // ROW_BLOCK
    n_ctx = k_ref.shape[0] - n_par * n_tok
    n_keys = n_ctx + n_tok
    mod = mod_ref[0, 0]

    by_block = latent

    def whole_sequence(fn):
        def run():
            fn()

        if by_block:
            pl.when(pl.program_id(1) == 0)(run)
        else:
            run()

    def blocks(body):
        if n_par * n_blk <= MAX_INLINE_BLOCKS:
            for r in range(n_par * n_blk):
                body(r)
        else:
            def step(r, carry):
                body(r)
                return carry
            whole_sequence(lambda: lax.fori_loop(0, n_par * n_blk, step, 0))

    def aligned(start, size):
        return pl.ds(start if isinstance(start, int) else pl.multiple_of(start, size), size)

    def block_rows(r, offset=0):
        return aligned(offset + r * ROW_BLOCK, ROW_BLOCK)

    if not latent:
        for ref in (ko_ref, vo_ref, so_ref):
            for q in range(n_par):
                for other in range(1, ref.shape[1]):
                    ref[q, other] = jnp.zeros(ref.shape[2:], F32)

    lane_group = _iota((SGU_CHUNK, SGU_W), 1) // SGU_GROUP_W
    blk_r = _iota((ROW_BLOCK, ROW_BLOCK), 0)
    blk_c = _iota((ROW_BLOCK, ROW_BLOCK), 1)
    same_chunk = (blk_r // GLA_CHUNK) == (blk_c // GLA_CHUNK)
    tri = (jnp.where(same_chunk & (blk_c <= blk_r), 1.0, 0.0).astype(BF16),
           jnp.where(same_chunk & (blk_c >= blk_r), 1.0, 0.0).astype(BF16))
    chunks_per_blk = ROW_BLOCK // GLA_CHUNK
    head_of_lane = _iota((GLA_CHUNK, W_GLA), 1) // GLA_DK
    stack_r = _iota((GLA_HEADS * GLA_CHUNK, GLA_CHUNK), 0) % GLA_CHUNK
    stack_c = _iota((GLA_HEADS * GLA_CHUNK, GLA_CHUNK), 1)
    causal = (stack_c <= stack_r, stack_c >= stack_r)

    if latent:
        def cached_context():
            for h in range(DIFF_HEADS):
                k_ref[0:n_ctx, h * DIFF_V:(h + 1) * DIFF_V] = ck_ref[0, 0, h].astype(BF16)
                v_ref[0:n_ctx, h * DIFF_V:(h + 1) * DIFF_V] = cv_ref[0, 0, h].astype(BF16)
            st_ref[0] = st0_ref[0, 0]

        whole_sequence(cached_context)
        pair_lo = (_iota((ROW_BLOCK, W_QK), 1) % (DIFF_QK // 2)) < (DIFF_QK // 4)

        def rope(z, rows):
            cos = jnp.concatenate([cos_ref[rows, :]] * DIFF_HEADS, axis=-1)
            sin = jnp.concatenate([sin_ref[rows, :]] * DIFF_HEADS, axis=-1)
            shift = DIFF_QK // 4
            swapped = jnp.where(pair_lo, pltpu.roll(z, W_QK - shift, 1), pltpu.roll(z, shift, 1))
            return z * cos + swapped * sin

    def modulated_input(r):
        h = _row_rms(x_ref[block_rows(r), :]) * n1_ref[0]
        return (h * (1.0 + mod[1:2, :]) + mod[0:1, :]).astype(BF16)

    def spatial_gating(r):
        for c in range(ROW_BLOCK // SGU_CHUNK):
            local = slice(c * SGU_CHUNK, (c + 1) * SGU_CHUNK)
            u = jax.nn.gelu(proj_ref[local, C_AU:C_AU + SGU_W])
            v = _group_rms(jax.nn.gelu(proj_ref[local, C_AV:C_AV + SGU_W]), SGU_GROUP_W).astype(BF16)
            s = sb_ref[0]
            for g in range(SGU_GROUPS):
                s = s + jnp.where(lane_group == g, _dot(sw_ref[0, g], v), 0.0)
            mix_ref[aligned(r * ROW_BLOCK + c * SGU_CHUNK, SGU_CHUNK), M_A:M_A + SGU_W] = (u * s).astype(BF16)

    def attention_operands(r):
        rows = block_rows(r)
        key_rows = block_rows(r, n_ctx)
        seq, seq_rows = r // n_blk, block_rows(r % n_blk)
        qn = _group_rms(proj_ref[:, C_BQ:C_BQ + W_QK], DIFF_QK) * qg_ref[0]
        kn = _group_rms(proj_ref[:, C_BK:C_BK + W_QK], DIFF_QK) * kg_ref[0]
        vv = proj_ref[:, C_BV:C_BV + W_QK]
        if latent:
            qn, kn = rope(qn, rows), rope(kn, rows)
        else:
            for h in range(DIFF_HEADS):
                for i in range(2):
                    lo = h * DIFF_V + i * DIFF_QK
                    ko_ref[seq, 0, h, i, seq_rows, :] = kn[:, lo:lo + DIFF_QK]
                vo_ref[seq, 0, h, seq_rows, :] = vv[:, h * DIFF_V:(h + 1) * DIFF_V]
        q_ref[rows, :] = (qn * (DIFF_QK ** -0.5)).astype(BF16)
        k_ref[key_rows, :] = kn.astype(BF16)
        v_ref[key_rows, :] = vv.astype(BF16)

    def gla_operands(r):
        rows = block_rows(r)
        gpre = _dot(proj_ref[:, C_LR:C_LR + LANES].astype(BF16), w2c_ref[0]) + gb_ref[0]
        gate = _log_sigmoid(gpre) * (1.0 / GLA_GATE_NORM)
        gq = proj_ref[:, C_CQ:C_CQ + W_GLA] * (GLA_DK ** -0.5)
        gk = proj_ref[:, C_CK:C_CK + W_GLA]
        gv = proj_ref[:, C_CV:C_CV + W_GLA].astype(BF16)
        gv_ref[rows, :] = gv
        gr_ref[rows, :] = proj_ref[:, C_CR:C_CR + W_GLA]
        for d in range(2):
            g = gate[:, d * W_GLA:(d + 1) * W_GLA]
            b = sum(_dot(tri[d], p) for p in _split3(g))
            last = GLA_CHUNK - 1 if d == 0 else 0
            b_last = jnp.concatenate(
                [jnp.broadcast_to(b[c * GLA_CHUNK + last:c * GLA_CHUNK + last + 1, :], (GLA_CHUNK, W_GLA))
                 for c in range(chunks_per_blk)], axis=0)
            q_dec = (gq * jnp.exp(b)).astype(BF16)
            k_inv = (gk * jnp.exp(-b)).astype(BF16)
            gq_ref[d, rows, :] = q_dec
            gke_ref[d, rows, :] = (gk * jnp.exp(b_last - b)).astype(BF16)
            for c in range(chunks_per_blk):
                row = c * GLA_CHUNK + last
                dec_ref[d, r * chunks_per_blk + c] = jnp.exp(b[row:row + 1, :])
                chunk = slice(c * GLA_CHUNK, (c + 1) * GLA_CHUNK)
                qd = q_dec[chunk]
                q_stack = jnp.concatenate(
                    [jnp.where(head_of_lane == h, qd, jnp.zeros_like(qd)) for h in range(GLA_HEADS)], axis=0)
                attn = jnp.where(causal[d], _dot_nt(q_stack, k_inv[chunk]), 0.0)
                spread = _dot(attn.astype(BF16), gv[chunk])
                o = jnp.zeros((GLA_CHUNK, W_GLA), F32)
                for h in range(GLA_HEADS):
                    o = o + jnp.where(head_of_lane == h, spread[h * GLA_CHUNK:(h + 1) * GLA_CHUNK, :], 0.0)
                go_ref[d, aligned(r * ROW_BLOCK + c * GLA_CHUNK, GLA_CHUNK), :] = o

    def project_and_split(r):
        proj_ref[...] = _dot(modulated_input(r), win_ref[0])
        spatial_gating(r)
        attention_operands(r)
        gla_operands(r)

    blocks(project_and_split)

    dl = dl_ref[0]
    lam = (jnp.exp(jnp.sum(dl[0:1] * dl[1:2], axis=-1, keepdims=True))
           - jnp.exp(jnp.sum(dl[2:3] * dl[3:4], axis=-1, keepdims=True)) + lam_init)
    sub0 = (_iota((ROW_BLOCK, DIFF_V), 1) < DIFF_QK)

    def softmax(s):
        e = jnp.exp(s - jnp.max(s, axis=-1, keepdims=True))
        return e, jnp.sum(e, axis=-1, keepdims=True)

    def attn_block(r):
        rows = block_rows(r)
        keys = aligned((r // n_blk) * n_keys, n_keys)
        for h in range(DIFF_HEADS):
            cols = slice(h * DIFF_V, (h + 1) * DIFF_V)
            qh = q_ref[rows, cols]
            kh = k_ref[keys, cols]
            e0, z0 = softmax(_dot_nt(jnp.where(sub0, qh, jnp.zeros_like(qh)), kh))
            e1, z1 = softmax(_dot_nt(jnp.where(sub0, jnp.zeros_like(qh), qh), kh))
            w = e0 / z0 - lam * (e1 / z1)
            o = _dot(w.astype(BF16), v_ref[keys, cols])
            o = _row_rms(o) * dg_ref[0] * (1.0 - lam_init)
            mix_ref[rows, M_B + h * DIFF_V:M_B + (h + 1) * DIFF_V] = o.astype(BF16)

    blocks(attn_block)

    if not latent:
        st_ref[...] = jnp.zeros(st_ref.shape, F32)

    n_chunk = n_tok // GLA_CHUNK
    st_diag = (_iota((W_GLA, W_GLA), 0) // GLA_DV) == (_iota((W_GLA, W_GLA), 1) // GLA_DK)

    def gla_step(c, carry):
        for seq in range(n_par):
            for d in range(2):
                cc = seq * n_chunk + (c if d == 0 else n_chunk - 1 - c)
                rows = pl.ds(pl.multiple_of(cc * GLA_CHUNK, GLA_CHUNK), GLA_CHUNK)
                st = st_ref[seq, d]
                go_ref[d, rows, :] = go_ref[d, rows, :] + _dot_nt(gq_ref[d, rows, :], st.astype(BF16))
                upd = _dot_tn(gv_ref[rows, :], gke_ref[d, rows, :])
                st_ref[seq, d] = dec_ref[d, cc] * st + jnp.where(st_diag, upd, 0.0)
        return carry

    whole_sequence(lambda: lax.fori_loop(0, n_chunk, gla_step, 0))

    if not latent:
        for seq in range(n_par):
            for d in range(2):
                s_full = st_ref[seq, d].T
                for h in range(GLA_HEADS):
                    so_ref[seq, 0, d, h] = s_full[h * GLA_DK:(h + 1) * GLA_DK, h * GLA_DV:(h + 1) * GLA_DV]

    def finish_block(r, out_r):
        rows, out_rows = block_rows(r), block_rows(out_r)
        oc = _group_rms(go_ref[0, rows, :] + go_ref[1, rows, :], GLA_DV) * gg_ref[0]
        oc = oc * jax.nn.silu(gr_ref[rows, :])
        mix_ref[rows, M_C:M_C + W_GLA] = oc.astype(BF16)
        x1 = x_ref[rows, :] + mod[2:3, :] * _dot(mix_ref[rows, :], wout_ref[0])
        xo_ref[out_rows, :] = x1
        hn = _row_rms(x1) * n2_ref[0]
        hn = hn * (1.0 + mod[4:5, :]) + mod[3:4, :]
        idx1, idx2, w1, w2 = _route(hn, rwt_ref, rb_ref)
        slot1, slot2, counts = _local_slots(idx1, idx2)
        perm = _slot_one_hot(slot1, slot2, 1.0, 1.0).astype(BF16)
        _to_row_slabs(hs_ref, 2 * out_r * ROW_BLOCK, _dot(perm, hn.astype(BF16)))
        slot_ref[:, out_rows] = jnp.concatenate([slot1, slot2], axis=0).astype(jnp.int32)
        wt_ref[:, out_rows] = jnp.concatenate([w1, w2], axis=0)
        cnt_ref[out_r] = jnp.broadcast_to(counts, (N_EXPERTS, LANES)).astype(jnp.int32)

    if by_block:
        finish_block(pl.program_id(1), 0)
    else:
        for r in range(n_par * n_blk):
            finish_block(r, r)


def _mixer_call(l, n_tok, n_par, latent, x, mods_all, weights, extras, cache_bufs):
    n_seq = x.shape[0] // n_tok
    n_all = x.shape[0]
    assert n_seq % n_par == 0 and not (latent and n_par > 1)
    n_step_tok = n_par * n_tok
    n_keys = n_step_tok + (extras[0].shape[3] if latent else 0)
    n_chunk = n_step_tok // GLA_CHUNK
    lam_init = 0.8 - 0.6 * math.exp(-0.3 * l)

    single = pl.Buffered(1)

    def layer(arr):
        tail = arr.shape[1:]
        return pl.BlockSpec((1,) + tail, lambda s, *_r, _n=len(tail): (l,) + (0,) * _n, pipeline_mode=single)

    def const(arr):
        return pl.BlockSpec(arr.shape, lambda s, *_r, _n=arr.ndim: (0,) * _n, pipeline_mode=single)

    def tok_spec(width):
        return pl.BlockSpec((n_step_tok, width), lambda s, *_r: (s, 0))

    mod_row = (lambda s: 1 + s) if latent else (lambda s: 0)
    in_specs = [tok_spec(D_MODEL),
                pl.BlockSpec((1, 1, 6, D_MODEL), lambda s, *_r: (l, mod_row(s), 0, 0))]
    in_specs += [layer(w) for w in weights[:N_MIXER_WEIGHTS - 2]] + [const(w) for w in weights[-2:]]
    operands = [x, mods_all] + list(weights)
    if latent:
        ck, cv, st0, cos, sin = extras
        in_specs += [
            pl.BlockSpec((1, 1) + ck.shape[2:], lambda s, *_r: (s, l, 0, 0, 0)),
            pl.BlockSpec((1, 1) + cv.shape[2:], lambda s, *_r: (s, l, 0, 0, 0)),
            pl.BlockSpec((1, 1) + st0.shape[2:], lambda s, *_r: (s, l, 0, 0, 0)),
            const(cos), const(sin),
        ]
        operands += [ck, cv, st0, cos, sin]
    n_in = len(operands)
    in_specs += [pl.BlockSpec(memory_space=pl.ANY)] * len(cache_bufs)
    operands += list(cache_bufs)

    tiles_per_step = n_step_tok // ROW_BLOCK
    out_shape = [
        jax.ShapeDtypeStruct((n_all, D_MODEL), F32),
        jax.ShapeDtypeStruct((2 * n_all * ROW_SLABS, LANES), F32),
        jax.ShapeDtypeStruct((2, n_all), jnp.int32),
        jax.ShapeDtypeStruct((2, n_all), F32),
        jax.ShapeDtypeStruct((n_all // ROW_BLOCK, N_EXPERTS, LANES), jnp.int32),
    ]
    if latent:
        grid = (n_seq, tiles_per_step)
        out_tok, out_tiles = ROW_BLOCK, 1
        at = lambda s, r: s * tiles_per_step + r
    else:
        grid = (n_seq // n_par,)
        out_tok, out_tiles = n_step_tok, tiles_per_step
        at = lambda s: s
    out_specs = [
        pl.BlockSpec((out_tok, D_MODEL), lambda *g: (at(*g), 0)),
        pl.BlockSpec((2 * out_tok * ROW_SLABS, LANES), lambda *g: (at(*g), 0)),
        pl.BlockSpec((2, out_tok), lambda *g: (0, at(*g))),
        pl.BlockSpec((2, out_tok), lambda *g: (0, at(*g))),
        pl.BlockSpec((out_tiles, N_EXPERTS, LANES), lambda *g: (at(*g), 0, 0)),
    ]
    n_shared_out = len(out_shape)
    aliases = {}
    if not latent:
        out_shape += [
            jax.ShapeDtypeStruct((n_seq, DEPTH, DIFF_HEADS, 2, n_tok, DIFF_QK), F32),
            jax.ShapeDtypeStruct((n_seq, DEPTH, DIFF_HEADS, n_tok, DIFF_V), F32),
            jax.ShapeDtypeStruct((n_seq, DEPTH, 2, GLA_HEADS, GLA_DK, GLA_DV), F32),
        ]
        n_lay, lay = (1, l) if cache_bufs else (DEPTH, 0)
        out_specs += [
            pl.BlockSpec((n_par, n_lay, DIFF_HEADS, 2, n_tok, DIFF_QK), lambda s: (s, lay, 0, 0, 0, 0)),
            pl.BlockSpec((n_par, n_lay, DIFF_HEADS, n_tok, DIFF_V), lambda s: (s, lay, 0, 0, 0)),
            pl.BlockSpec((n_par, n_lay, 2, GLA_HEADS, GLA_DK, GLA_DV), lambda s: (s, lay, 0, 0, 0, 0)),
        ]
        aliases = {n_in + j: n_shared_out + j for j in range(len(cache_bufs))}
    scratch = [
        pltpu.VMEM((ROW_BLOCK, D_PROJ_PAD), F32),
        pltpu.VMEM((n_step_tok, D_MODEL), BF16),
        pltpu.VMEM((n_step_tok, W_QK), BF16),
        pltpu.VMEM((n_keys, W_QK), BF16),
        pltpu.VMEM((n_keys, W_QK), BF16),
        pltpu.VMEM((2, n_step_tok, W_GLA), BF16),
        pltpu.VMEM((2, n_step_tok, W_GLA), BF16),
        pltpu.VMEM((n_step_tok, W_GLA), BF16),
        pltpu.VMEM((n_step_tok, W_GLA), F32),
        pltpu.VMEM((2, n_chunk, 1, W_GLA), F32),
        pltpu.VMEM((2, n_step_tok, W_GLA), F32),
        pltpu.VMEM((n_par, 2, W_GLA, W_GLA), F32),
    ]
    return pl.pallas_call(
        functools.partial(_mixer_kernel, n_tok, n_par, latent, len(cache_bufs), lam_init),
        grid=grid,
        in_specs=in_specs,
        out_specs=out_specs,
        out_shape=out_shape,
        scratch_shapes=scratch,
        input_output_aliases=aliases,
        compiler_params=pltpu.CompilerParams(
            dimension_semantics=("arbitrary",) * len(grid), vmem_limit_bytes=MIXER_VMEM_LIMIT),
        name="mixer_latent" if latent else "mixer_context",
    )(*operands)


PAIR_BLOCK = 2 * ROW_BLOCK
COPY_SIZES = tuple(ROW_BLOCK >> k for k in range(ROW_BLOCK.bit_length()))
LARGE_COPY = 64
GATHER_AHEAD = 2
GATHER_SLOTS = GATHER_AHEAD + 1


def _segment_copies(n_rows, make_copy, act):
    def copy_if_set(size):
        @pl.when((n_rows & size) != 0)
        def _():
            act(make_copy(n_rows & (-2 * size), size))

    n_large = COPY_SIZES.index(LARGE_COPY) + 1

    @pl.when(n_rows >= LARGE_COPY)
    def _():
        for size in COPY_SIZES[:n_large]:
            copy_if_set(size)

    for size in COPY_SIZES[n_large:]:
        copy_if_set(size)


def _start(copy):
    copy.start()


def _wait(copy):
    copy.wait()


def _slab_rows(first_row, n_rows, slab):
    return pl.ds(first_row * ROW_SLABS + slab, n_rows, stride=ROW_SLABS)


def _to_row_slabs(ref, first_row, value):
    for s in range(ROW_SLABS):
        ref[_slab_rows(first_row, value.shape[0], s), :] = value[:, s * LANES:(s + 1) * LANES]


def _from_row_slabs(ref, first_row, n_rows):
    return jnp.concatenate([ref[_slab_rows(first_row, n_rows, s), :] for s in range(ROW_SLABS)], axis=-1)


SLAB_PAIR_W = 2 * LANES
N_SLAB_PAIRS = ROW_SLABS // 2


def _slab_pair_cols(g):
    return slice(g * SLAB_PAIR_W, (g + 1) * SLAB_PAIR_W)


def _load_slab_pair(ref, first_row, n_rows, g):
    return jnp.concatenate([ref[_slab_rows(first_row, n_rows, s), :] for s in (2 * g, 2 * g + 1)], axis=-1)


def _row_span(ref, first_row, n_rows):
    return ref.at[pl.ds(pl.multiple_of(first_row * ROW_SLABS, ROW_SLABS), n_rows * ROW_SLABS)]


def _two_streams(n_first_tiles):
    def first(i, *_):
        return (jnp.minimum(i, n_first_tiles - 1), 0)

    def second(i, *_):
        return (jnp.maximum(i - n_first_tiles, 0), 0)

    return first, second


N_EXPERT_TABLES = 10
OUT_SLOTS = 2


def _expert_kernel(n_ctx_tiles, n_tiles_max, te_ref, first_ref, rows_ref, jlo_ref, jhi_ref, cpre_ref, cnt_ref,
                   lofs_ref, tile0_ref, ntile_ref, hs_c_ref, hs_l_ref, w1_ref, w3_ref, w2_ref, ys_ref,
                   xbuf_ref, obuf_ref, w1b_ref, w3b_ref, w2b_ref, sem, out_sem):
    expert = pl.program_id(0)
    n_tiles = tile0_ref[N_EXPERTS]

    def gather(t, act):
        slot = t % GATHER_SLOTS
        e, first = te_ref[t], first_ref[t]
        last = first + rows_ref[t]

        def segment_of(hs_ref, first_tile):
            def body(j, carry):
                k = j * N_EXPERTS + e
                seg_first = cpre_ref[k]
                lo = jnp.maximum(seg_first, first)
                n = jnp.maximum(jnp.minimum(seg_first + cnt_ref[k], last) - lo, 0)
                src = (j - first_tile) * PAIR_BLOCK + lofs_ref[k] + (lo - seg_first)
                dst = slot * ROW_BLOCK + lo - first
                _segment_copies(n, lambda done, size: pltpu.make_async_copy(
                    _row_span(hs_ref, src + done, size), _row_span(xbuf_ref, dst + done, size), sem.at[slot]), act)
                return carry
            return body

        jlo, jhi = jlo_ref[t], jhi_ref[t]
        lax.fori_loop(jnp.minimum(jlo, n_ctx_tiles), jnp.minimum(jhi, n_ctx_tiles), segment_of(hs_c_ref, 0), 0)
        lax.fori_loop(jnp.maximum(jlo, n_ctx_tiles), jnp.maximum(jhi, n_ctx_tiles),
                      segment_of(hs_l_ref, n_ctx_tiles), 0)

    def out_copy(t, oslot):
        return pltpu.make_async_copy(
            _row_span(obuf_ref, oslot * ROW_BLOCK, ROW_BLOCK), _row_span(ys_ref, t * ROW_BLOCK, ROW_BLOCK),
            out_sem.at[oslot])

    @pl.when(expert == 0)
    def _():
        xbuf_ref[...] = jnp.zeros(xbuf_ref.shape, F32)
        for t in range(GATHER_AHEAD):
            gather(t, _start)

    w1b_ref[...] = w1_ref[0, 0].astype(BF16)
    w3b_ref[...] = w3_ref[0, 0].astype(BF16)
    w2b_ref[...] = w2_ref[0, 0].astype(BF16)
    tile0, n_own = tile0_ref[expert], ntile_ref[expert]

    def tile_body(k, carry):
        t = tile0 + k
        slot, oslot = t % GATHER_SLOTS, t % OUT_SLOTS

        @pl.when(t + GATHER_AHEAD < n_tiles)
        def _():
            gather(t + GATHER_AHEAD, _start)

        n_rows = rows_ref[t]
        _segment_copies(n_rows, lambda done, size: pltpu.make_async_copy(
            _row_span(hs_c_ref, done, size), _row_span(xbuf_ref, slot * ROW_BLOCK + done, size), sem.at[slot]), _wait)

        @pl.when(t >= OUT_SLOTS)
        def _():
            out_copy(t, oslot).wait()

        live = _iota((ROW_BLOCK, D_MODEL), 0) < n_rows
        x = jnp.where(live, _from_row_slabs(xbuf_ref, slot * ROW_BLOCK, ROW_BLOCK), 0.0).astype(BF16)
        hid = jax.nn.silu(_dot(x, w1b_ref[...])) * _dot(x, w3b_ref[...])
        _to_row_slabs(obuf_ref, oslot * ROW_BLOCK, _dot(hid.astype(BF16), w2b_ref[...]))
        out_copy(t, oslot).start()
        return carry

    lax.fori_loop(0, n_own, tile_body, 0)

    @pl.when(expert == N_EXPERTS - 1)
    def _():
        for oslot in range(OUT_SLOTS):
            @pl.when(n_tiles > oslot)
            def _():
                out_copy(0, oslot).wait()
        obuf_ref[...] = jnp.zeros(obuf_ref.shape, F32)

        def fill(t, carry):
            out_copy(t, 0).start()
            out_copy(t, 0).wait()
            return carry

        lax.fori_loop(n_tiles, n_tiles_max, fill, 0)


def _expert_call(l, plan, hs_c, hs_l, w1, w3, w2):
    tables = plan["expert_tables"]
    n_tiles_max = tables[0].shape[0]
    n_ctx_tiles = hs_c.shape[0] // (PAIR_BLOCK * ROW_SLABS)

    def weight(shape):
        return pl.BlockSpec((1, 1) + shape, lambda e, *_: (l, e, 0, 0))

    return pl.pallas_call(
        functools.partial(_expert_kernel, n_ctx_tiles, n_tiles_max),
        grid_spec=pltpu.PrefetchScalarGridSpec(
            num_scalar_prefetch=N_EXPERT_TABLES,
            grid=(N_EXPERTS,),
            in_specs=[pl.BlockSpec(memory_space=pl.ANY), pl.BlockSpec(memory_space=pl.ANY),
                      weight((D_MODEL, D_EXPERT)), weight((D_MODEL, D_EXPERT)), weight((D_EXPERT, D_MODEL))],
            out_specs=pl.BlockSpec(memory_space=pl.ANY),
            scratch_shapes=[pltpu.VMEM((GATHER_SLOTS * ROW_BLOCK * ROW_SLABS, LANES), F32),
                            pltpu.VMEM((OUT_SLOTS * ROW_BLOCK * ROW_SLABS, LANES), F32),
                            pltpu.VMEM((D_MODEL, D_EXPERT), BF16), pltpu.VMEM((D_MODEL, D_EXPERT), BF16),
                            pltpu.VMEM((D_EXPERT, D_MODEL), BF16),
                            pltpu.SemaphoreType.DMA((GATHER_SLOTS,)), pltpu.SemaphoreType.DMA((OUT_SLOTS,))],
        ),
        out_shape=jax.ShapeDtypeStruct((n_tiles_max * ROW_BLOCK * ROW_SLABS, LANES), F32),
        compiler_params=pltpu.CompilerParams(
            dimension_semantics=("arbitrary",), vmem_limit_bytes=SMALL_KERNEL_VMEM_LIMIT),
        name="moe_experts",
    )(*tables, hs_c, hs_l, w1, w3, w2)


N_COMBINE_TABLES = 4
COMBINE_TILES_PER_STEP = 2


def _combine_kernel(n_ctx_steps, cnt_ref, cpre_ref, lofs_ref, starts_ref, x_c_ref, x_l_ref, slot_c_ref, slot_l_ref,
                    wt_c_ref, wt_l_ref, mod_ref, ys_ref, xo_c_ref, xo_l_ref, buf_ref, sem):
    step = pl.program_id(0)
    n_tiles = pl.num_programs(0) * COMBINE_TILES_PER_STEP

    def collect(t, act):
        slot = t % GATHER_SLOTS

        def body(e, carry):
            k = t * N_EXPERTS + e
            src, dst = starts_ref[e] + cpre_ref[k], slot * PAIR_BLOCK + lofs_ref[k]
            _segment_copies(cnt_ref[k], lambda done, size: pltpu.make_async_copy(
                _row_span(ys_ref, src + done, size), _row_span(buf_ref, dst + done, size), sem.at[slot]), act)
            return carry

        lax.fori_loop(0, N_EXPERTS, body, 0)

    @pl.when(step == 0)
    def _():
        for t in range(GATHER_AHEAD):
            collect(t, _start)

    gate = mod_ref[0, 0, 5:6, :]

    def finish(part, first_row, x_ref, slot_ref, wt_ref, xo_ref):
        tokens = pl.ds(part * ROW_BLOCK, ROW_BLOCK)
        slots, wts = slot_ref[:, tokens].astype(F32), wt_ref[:, tokens]
        slot1, slot2 = slots[0:1], slots[1:2]
        weight_of_row = jnp.sum(_slot_one_hot(slot1, slot2, wts[0:1], wts[1:2]), axis=1, keepdims=True)
        gather_rows = _slot_one_hot(slot1, slot2, 1.0, 1.0).T.astype(BF16)
        for g in range(N_SLAB_PAIRS):
            cols = _slab_pair_cols(g)
            hi, lo = _split2(_load_slab_pair(buf_ref, first_row, PAIR_BLOCK, g) * weight_of_row)
            y = _dot(gather_rows, hi) + _dot(gather_rows, lo)
            xo_ref[tokens, cols] = x_ref[tokens, cols] + gate[:, cols] * y

    for part in range(COMBINE_TILES_PER_STEP):
        j = step * COMBINE_TILES_PER_STEP + part

        @pl.when(j + GATHER_AHEAD < n_tiles)
        def _():
            collect(j + GATHER_AHEAD, _start)

        slot = j % GATHER_SLOTS
        for piece in range(PAIR_BLOCK // ROW_BLOCK):
            pltpu.make_async_copy(
                _row_span(ys_ref, piece * ROW_BLOCK, ROW_BLOCK),
                _row_span(buf_ref, slot * PAIR_BLOCK + piece * ROW_BLOCK, ROW_BLOCK), sem.at[slot]).wait()
        @pl.when(step < n_ctx_steps)
        def _():
            finish(part, slot * PAIR_BLOCK, x_c_ref, slot_c_ref, wt_c_ref, xo_c_ref)

        @pl.when(step >= n_ctx_steps)
        def _():
            finish(part, slot * PAIR_BLOCK, x_l_ref, slot_l_ref, wt_l_ref, xo_l_ref)


def _combine_call(l, plan, x_c, x_l, slot_c, slot_l, wt_c, wt_l, mods_all, mod_row_of_tile, ys):
    step_rows = COMBINE_TILES_PER_STEP * ROW_BLOCK
    n_steps = (x_c.shape[0] + x_l.shape[0]) // step_rows
    n_ctx_steps = x_c.shape[0] // step_rows
    first, second = _two_streams(n_ctx_steps)

    def lanes(index_map):
        return lambda i, *_: index_map(i)[::-1]

    return pl.pallas_call(
        functools.partial(_combine_kernel, n_ctx_steps),
        grid_spec=pltpu.PrefetchScalarGridSpec(
            num_scalar_prefetch=N_COMBINE_TABLES,
            grid=(n_steps,),
            in_specs=[pl.BlockSpec((step_rows, D_MODEL), first),
                      pl.BlockSpec((step_rows, D_MODEL), second),
                      pl.BlockSpec((2, step_rows), lanes(first)),
                      pl.BlockSpec((2, step_rows), lanes(second)),
                      pl.BlockSpec((2, step_rows), lanes(first)),
                      pl.BlockSpec((2, step_rows), lanes(second)),
                      pl.BlockSpec((1, 1, 6, D_MODEL),
                                   lambda i, *_: (l, mod_row_of_tile(i * COMBINE_TILES_PER_STEP), 0, 0)),
                      pl.BlockSpec(memory_space=pl.ANY)],
            out_specs=[pl.BlockSpec((step_rows, D_MODEL), first),
                       pl.BlockSpec((step_rows, D_MODEL), second)],
            scratch_shapes=[pltpu.VMEM((GATHER_SLOTS * PAIR_BLOCK * ROW_SLABS, LANES), F32),
                            pltpu.SemaphoreType.DMA((GATHER_SLOTS,))],
        ),
        out_shape=[jax.ShapeDtypeStruct(x_c.shape, F32), jax.ShapeDtypeStruct(x_l.shape, F32)],
        compiler_params=pltpu.CompilerParams(
            dimension_semantics=("arbitrary",), vmem_limit_bytes=SMALL_KERNEL_VMEM_LIMIT),
        name="moe_combine",
    )(*plan["combine_tables"], x_c, x_l, slot_c, slot_l, wt_c, wt_l, mods_all, ys)


def _moe_plan(cnt):
    n_tok_tiles = cnt.shape[0]
    n_tiles = n_tok_tiles * PAIR_BLOCK // ROW_BLOCK + N_EXPERTS
    lofs = jnp.cumsum(cnt, axis=1) - cnt
    cpre = jnp.cumsum(cnt, axis=0) - cnt
    counts = jnp.sum(cnt, axis=0)
    padded = (counts + ROW_BLOCK - 1) // ROW_BLOCK * ROW_BLOCK
    ends = jnp.cumsum(padded)
    starts = ends - padded
    tile_start = jnp.arange(n_tiles, dtype=jnp.int32) * ROW_BLOCK
    tile_expert = jnp.minimum(
        jnp.sum((tile_start[:, None] >= ends[None, :]).astype(jnp.int32), axis=1), N_EXPERTS - 1)
    hot = tile_expert[:, None] == jnp.arange(N_EXPERTS, dtype=jnp.int32)[None, :]
    first = tile_start - jnp.sum(jnp.where(hot, starts[None, :], 0), axis=1)
    rows = jnp.clip(jnp.sum(jnp.where(hot, counts[None, :], 0), axis=1) - first, 0, ROW_BLOCK)
    seg_first = jnp.sum(jnp.where(hot[:, None, :], cpre[None, :, :], 0), axis=2)
    seg_rows = jnp.sum(jnp.where(hot[:, None, :], cnt[None, :, :], 0), axis=2)
    overlap = (seg_first < (first + rows)[:, None]) & (seg_first + seg_rows > first[:, None])
    j = jnp.arange(n_tok_tiles, dtype=jnp.int32)[None, :]
    jlo = jnp.min(jnp.where(overlap, j, n_tok_tiles), axis=1)
    jhi = jnp.max(jnp.where(overlap, j + 1, 0), axis=1)
    i32 = lambda a: a.astype(jnp.int32).reshape(-1)
    tile0 = jnp.concatenate([starts, ends[-1:]]) // ROW_BLOCK
    return {
        "expert_tables": tuple(i32(a) for a in (tile_expert, first, rows, jlo, jhi, cpre, cnt, lofs,
                                                tile0, padded // ROW_BLOCK)),
        "combine_tables": tuple(i32(a) for a in (cnt, cpre, lofs, starts)),
    }


def _rope_tables(n_tok):
    n_rows = n_tok // GRID_W
    pos_r = jnp.repeat(jnp.arange(n_rows), GRID_W)
    pos_c = jnp.tile(jnp.arange(GRID_W), n_rows)
    half = DIFF_QK // 2
    nf = half // 2
    freqs = ROPE_BASE ** (-jnp.arange(nf, dtype=F32) / nf)

    def tables(pos):
        ang = pos.astype(F32)[:, None] * freqs
        cos, sin = jnp.cos(ang), jnp.sin(ang)
        return jnp.concatenate([cos, cos], axis=-1), jnp.concatenate([-sin, sin], axis=-1)

    cos_r, sin_r = tables(pos_r)
    cos_c, sin_c = tables(pos_c)
    cos = jnp.concatenate([cos_r, cos_c], axis=-1)
    sin = jnp.concatenate([sin_r, sin_c], axis=-1)
    return jnp.concatenate([cos, cos], axis=-1), jnp.concatenate([sin, sin], axis=-1)


def _mixer_weights(w_in, w_out, sgu_w, sgu_b, q_norm_g, k_norm_g, diff_lambda, diff_norm_g, gla_w2, gla_b,
                   gla_norm_g, norm1_g, norm2_g, router_w, router_bias):
    w_in_pad = jnp.concatenate(
        [w_in.astype(BF16), jnp.zeros(w_in.shape[:2] + (D_PROJ_PAD - w_in.shape[2],), BF16)], axis=-1)
    w2cat = jnp.zeros((DEPTH, LANES, 2 * W_GLA), F32)
    w2cat = w2cat.at[:, 0:GLA_RANK, 0:W_GLA].set(gla_w2[:, 0]).at[:, GLA_RANK:2 * GLA_RANK, W_GLA:].set(gla_w2[:, 1])
    return (
        norm1_g[:, None, :], norm2_g[:, None, :], w_in_pad, w_out.astype(BF16),
        sgu_w.astype(BF16), jnp.repeat(sgu_b.transpose(0, 2, 1), SGU_GROUP_W, axis=2),
        jnp.tile(q_norm_g, (1, W_QK // DIFF_QK))[:, None, :], jnp.tile(k_norm_g, (1, W_QK // DIFF_QK))[:, None, :],
        diff_lambda, diff_norm_g[:, None, :],
        w2cat.astype(BF16), gla_b.reshape(DEPTH, 1, 2 * W_GLA), jnp.tile(gla_norm_g, (1, GLA_HEADS))[:, None, :],
        router_w.T, router_bias[:, None],
    )


def kernel(x_prompt, x_sample, cache_k, cache_v, state_gla, c, c_ctx, w_in, w_out, sgu_w, sgu_b, q_norm_g, k_norm_g,
           diff_lambda, diff_norm_g, gla_w2, gla_b, gla_norm_g, norm1_g, norm2_g, ada_w, ada_b, router_w, router_bias,
           moe_w1, moe_w3, moe_w2):
    n_ctx_seq, ctx_len, _ = x_prompt.shape
    n_lat_seq, lat_len, _ = x_sample.shape
    n_ctx_tok = n_ctx_seq * ctx_len
    n_lat_tok = n_lat_seq * lat_len
    ctx_tiles = n_ctx_tok // ROW_BLOCK
    lat_tiles_per_seq = lat_len // ROW_BLOCK

    n_cond = 1 + n_lat_seq
    cond_t = jnp.zeros((D_MODEL, SUBLANES), F32).at[:, 0].set(c_ctx).at[:, 1:n_cond].set(c.T)
    mods_all = _adaln_call(cond_t, n_cond, ada_w, ada_b)[:, :n_cond].reshape(DEPTH, n_cond, 6, D_MODEL)
    weights = _mixer_weights(w_in, w_out, sgu_w, sgu_b, q_norm_g, k_norm_g, diff_lambda, diff_norm_g, gla_w2, gla_b,
                             gla_norm_g, norm1_g, norm2_g, router_w, router_bias)

    ck_all = cache_k.transpose(0, 1, 2, 4, 3, 5).reshape(cache_k.shape[:3] + (cache_k.shape[4], DIFF_V))
    st_all = jnp.einsum('bldhkv,hg->bldhvgk', state_gla, jnp.eye(GLA_HEADS, dtype=F32)).reshape(
        n_lat_seq, DEPTH, 2, W_GLA, W_GLA)
    cos, sin = _rope_tables(lat_len)
    extras = (ck_all, cache_v, st_all, cos, sin)

    def mod_row_of_tile(i):
        return jnp.where(i < ctx_tiles, 0, 1 + (i - ctx_tiles) // lat_tiles_per_seq)

    x_c = x_prompt.reshape(n_ctx_tok, D_MODEL)
    x_l = x_sample.reshape(n_lat_tok, D_MODEL)
    cache_bufs = ()
    for l in range(DEPTH):
        ctx_par = 1 if l == 0 else CTX_SEQS_PER_STEP
        x1_c, hs_c, slot_c, wt_c, cnt_c, *cache_bufs = _mixer_call(
            l, ctx_len, ctx_par, False, x_c, mods_all, weights, None, tuple(cache_bufs))
        x1_l, hs_l, slot_l, wt_l, cnt_l = _mixer_call(l, lat_len, 1, True, x_l, mods_all, weights, extras, ())
        plan = _moe_plan(jnp.concatenate([cnt_c[:, :, 0], cnt_l[:, :, 0]], axis=0))
        ys = _expert_call(l, plan, hs_c, hs_l, moe_w1, moe_w3, moe_w2)
        x_c, x_l = _combine_call(l, plan, x1_c, x1_l, slot_c, slot_l, wt_c, wt_l, mods_all, mod_row_of_tile, ys)

    new_k, new_v, new_s = cache_bufs
    return (x_c.reshape(x_prompt.shape), x_l.reshape(x_sample.shape), new_k, new_v, new_s)
```

```python
import functools
import math

import jax
import jax.numpy as jnp
from jax import lax
from jax.experimental import pallas as pl
from jax.experimental.pallas import tpu as pltpu

F32 = jnp.float32
BF16 = jnp.bfloat16

D_MODEL = 1024
DEPTH = 4
GRID_W = 64
SGU_GROUPS = 4
SGU_GROUP_W = 64
SGU_W = SGU_GROUPS * SGU_GROUP_W
SGU_CHUNK = 128
DIFF_HEADS = 4
DIFF_QK = 64
DIFF_V = 2 * DIFF_QK
ROPE_BASE = 10000.0
GLA_HEADS = 4
GLA_DK = 64
GLA_DV = 64
GLA_RANK = 16
GLA_GATE_NORM = 16.0
GLA_CHUNK = 64
N_EXPERTS = 16
N_GROUPS = 4
EXPERTS_PER_GROUP = N_EXPERTS // N_GROUPS
D_EXPERT = 512
EPS = 1e-6

LANES = 128
SUBLANES = 8
MXU_DIM = 256
V7X_VMEM_BYTES = 64 * 1024 * 1024
MIXER_VMEM_LIMIT = V7X_VMEM_BYTES * 7 // 8
SMALL_KERNEL_VMEM_LIMIT = V7X_VMEM_BYTES * 5 // 8

ROW_BLOCK = MXU_DIM
ROW_SLABS = D_MODEL // LANES

W_QK = DIFF_HEADS * 2 * DIFF_QK
W_GLA = GLA_HEADS * GLA_DK
C_AU, C_AV = 0, SGU_W
C_BQ = C_AV + SGU_W
C_BK, C_BV = C_BQ + W_QK, C_BQ + 2 * W_QK
C_CQ = C_BV + DIFF_HEADS * DIFF_V
C_CK, C_CV, C_CR, C_LR = C_CQ + W_GLA, C_CQ + 2 * W_GLA, C_CQ + 3 * W_GLA, C_CQ + 4 * W_GLA
D_PROJ_MAIN = C_LR
D_PROJ_PAD = D_PROJ_MAIN + LANES
M_A, M_B, M_C = 0, SGU_W, SGU_W + DIFF_HEADS * DIFF_V


def _split2(x):
    hi = x.astype(BF16)
    lo = (x - hi.astype(F32)).astype(BF16)
    return hi, lo


def _split3(x):
    hi = x.astype(BF16)
    r = x - hi.astype(F32)
    mid = r.astype(BF16)
    lo = (r - mid.astype(F32)).astype(BF16)
    return hi, mid, lo


def _dot(a, b):
    return jnp.dot(a, b, preferred_element_type=F32)


def _dot_nt(a, b):
    return lax.dot_general(a, b, (((1,), (1,)), ((), ())), preferred_element_type=F32)


def _dot_tn(a, b):
    return lax.dot_general(a, b, (((0,), (0,)), ((), ())), preferred_element_type=F32)


def _iota(shape, dim):
    return lax.broadcasted_iota(jnp.int32, shape, dim)


def _block_ones(width, block):
    r = _iota((width, width), 0) // block
    c = _iota((width, width), 1) // block
    return (r == c)


def _group_sum(z, block):
    width = z.shape[-1]
    outs = []
    for s in range(0, width, MXU_DIM):
        w = min(MXU_DIM, width - s)
        ones = _block_ones(w, block).astype(BF16)
        hi, lo = _split2(z[:, s:s + w])
        outs.append(_dot(hi, ones) + _dot(lo, ones))
    return outs[0] if len(outs) == 1 else jnp.concatenate(outs, axis=-1)


def _group_rms(z, block):
    ms = _group_sum(z * z, block) * (1.0 / block)
    return z * lax.rsqrt(ms + EPS)


def _row_rms(z):
    return z * lax.rsqrt(jnp.mean(z * z, axis=-1, keepdims=True) + EPS)


def _log_sigmoid(x):
    return jnp.minimum(x, 0.0) - jnp.log(1.0 + jnp.exp(-jnp.abs(x)))


ADA_COLS = 1536


def _adaln_kernel(n_cond, cond_t_ref, w_ref, b_ref, o_ref):
    sc = jax.nn.silu(cond_t_ref[...])
    w = w_ref[0]
    rows = [jnp.sum(sc[:, r:r + 1] * w, axis=0, keepdims=True) + b_ref[0] for r in range(n_cond)]
    o_ref[0] = jnp.concatenate(rows + [jnp.zeros((SUBLANES - n_cond, w.shape[1]), F32)], axis=0)


def _adaln_call(cond_t, n_cond, ada_w, ada_b):
    n_col = 6 * D_MODEL // ADA_COLS
    return pl.pallas_call(
        functools.partial(_adaln_kernel, n_cond),
        grid=(DEPTH, n_col),
        in_specs=[
            pl.BlockSpec((D_MODEL, SUBLANES), lambda l, j: (0, 0)),
            pl.BlockSpec((1, D_MODEL, ADA_COLS), lambda l, j: (l, 0, j)),
            pl.BlockSpec((1, 1, ADA_COLS), lambda l, j: (l, 0, j)),
        ],
        out_specs=pl.BlockSpec((1, SUBLANES, ADA_COLS), lambda l, j: (l, 0, j)),
        out_shape=jax.ShapeDtypeStruct((DEPTH, SUBLANES, 6 * D_MODEL), F32),
        compiler_params=pltpu.CompilerParams(
            dimension_semantics=("arbitrary", "arbitrary"), vmem_limit_bytes=SMALL_KERNEL_VMEM_LIMIT),
        name="adaln",
    )(cond_t, ada_w, ada_b.reshape(DEPTH, 1, 6 * D_MODEL))


def _route(hn, rwt_ref, rb_ref):
    h_hi, h_lo = _split2(hn)
    rw = rwt_ref[...]
    rw_hi = rw.astype(BF16)
    rw_lo = (rw - rw_hi.astype(F32)).astype(BF16)
    logits = _dot_nt(rw_hi, h_hi) + _dot_nt(rw_hi, h_lo) + _dot_nt(rw_lo, h_hi)
    aff = jax.nn.sigmoid(logits)
    sel = aff + rb_ref[...]
    n_tok = sel.shape[1]

    def top2_sum(a, b, c, d):
        hi1, lo1 = jnp.maximum(a, b), jnp.minimum(a, b)
        hi2, lo2 = jnp.maximum(c, d), jnp.minimum(c, d)
        return jnp.maximum(hi1, hi2) + jnp.maximum(jnp.minimum(hi1, hi2), jnp.maximum(lo1, lo2))

    scores = []
    for g in range(N_GROUPS):
        rows = [sel[EXPERTS_PER_GROUP * g + j:EXPERTS_PER_GROUP * g + j + 1, :] for j in range(EXPERTS_PER_GROUP)]
        scores.append(top2_sum(*rows))
    best = jnp.zeros((1, n_tok), jnp.int32)
    best_score = scores[0]
    for g in range(1, N_GROUPS):
        upd = scores[g] > best_score
        best = jnp.where(upd, g, best)
        best_score = jnp.where(upd, scores[g], best_score)

    eid_i = _iota((N_EXPERTS, n_tok), 0)
    eid = eid_i.astype(F32)
    neg = jnp.float32(-jnp.inf)
    msel = jnp.where(eid_i // EXPERTS_PER_GROUP == best, sel, neg)
    m1 = jnp.max(msel, axis=0, keepdims=True)
    idx1 = jnp.min(jnp.where(msel == m1, eid, float(N_EXPERTS)), axis=0, keepdims=True)
    msel2 = jnp.where(eid == idx1, neg, msel)
    m2 = jnp.max(msel2, axis=0, keepdims=True)
    idx2 = jnp.min(jnp.where(msel2 == m2, eid, float(N_EXPERTS)), axis=0, keepdims=True)
    w1 = jnp.sum(jnp.where(eid == idx1, aff, 0.0), axis=0, keepdims=True)
    w2 = jnp.sum(jnp.where(eid == idx2, aff, 0.0), axis=0, keepdims=True)
    wsum = w1 + w2
    return idx1.astype(jnp.int32), idx2.astype(jnp.int32), w1 / wsum, w2 / wsum


def _local_slots(idx1, idx2):
    n_tok = idx1.shape[1]
    eid = _iota((N_EXPERTS, n_tok), 0)
    hot1, hot2 = eid == idx1, eid == idx2
    hot = jnp.where(hot1, 1.0, jnp.where(hot2, 1.0, 0.0))
    earlier = jnp.where(_iota((n_tok, n_tok), 0) < _iota((n_tok, n_tok), 1), 1.0, 0.0).astype(BF16)
    before_in_expert = _dot(hot.astype(BF16), earlier)
    counts = jnp.sum(hot, axis=1, keepdims=True)
    lower = jnp.where(_iota((N_EXPERTS, N_EXPERTS), 1) < _iota((N_EXPERTS, N_EXPERTS), 0), 1.0, 0.0).astype(BF16)
    first_slot = _dot(lower, jnp.broadcast_to(counts, (N_EXPERTS, LANES)).astype(BF16))[:, 0:1]
    slot = before_in_expert + first_slot
    slot1 = jnp.sum(jnp.where(hot1, slot, 0.0), axis=0, keepdims=True)
    slot2 = jnp.sum(jnp.where(hot2, slot, 0.0), axis=0, keepdims=True)
    return slot1, slot2, counts


def _slot_one_hot(slot1, slot2, v1, v2):
    n_tok = slot1.shape[1]
    row = _iota((2 * n_tok, n_tok), 0).astype(F32)
    return jnp.where(row == slot1, v1, jnp.where(row == slot2, v2, 0.0))


N_MIXER_WEIGHTS = 15
CTX_SEQS_PER_STEP = 2
MAX_INLINE_BLOCKS = 2


def _mixer_kernel(n_tok, n_par, latent, n_alias, lam_init, *refs):
    it = iter(refs)
    x_ref, mod_ref = next(it), next(it)
    (n1_ref, n2_ref, win_ref, wout_ref, sw_ref, sb_ref, qg_ref, kg_ref, dl_ref, dg_ref,
     w2c_ref, gb_ref, gg_ref, rwt_ref, rb_ref) = (next(it) for _ in range(N_MIXER_WEIGHTS))
    if latent:
        ck_ref, cv_ref, st0_ref, cos_ref, sin_ref = (next(it) for _ in range(5))
    for _ in range(n_alias):
        next(it)
    xo_ref, hs_ref, slot_ref, wt_ref, cnt_ref = (next(it) for _ in range(5))
    if not latent:
        ko_ref, vo_ref, so_ref = (next(it) for _ in range(3))
    proj_ref, mix_ref, q_ref, k_ref, v_ref = (next(it) for _ in range(5))
    gq_ref, gke_ref, gv_ref, gr_ref, dec_ref, go_ref, st_ref = (next(it) for _ in range(7))

    n_blk = n_tok // ROW_BLOCK
    n_ctx = k_ref.shape[0] - n_par * n_tok
    n_keys = n_ctx + n_tok
    mod = mod_ref[0, 0]

    by_block = latent

    def whole_sequence(fn):
        def run():
            fn()

        if by_block:
            pl.when(pl.program_id(1) == 0)(run)
        else:
            run()

    def blocks(body):
        if n_par * n_blk <= MAX_INLINE_BLOCKS:
            for r in range(n_par * n_blk):
                body(r)
        else:
            def step(r, carry):
                body(r)
                return carry
            whole_sequence(lambda: lax.fori_loop(0, n_par * n_blk, step, 0))

    def aligned(start, size):
        return pl.ds(start if isinstance(start, int) else pl.multiple_of(start, size), size)

    def block_rows(r, offset=0):
        return aligned(offset + r * ROW_BLOCK, ROW_BLOCK)

    if not latent:
        for ref in (ko_ref, vo_ref, so_ref):
            for q in range(n_par):
                for other in range(1, ref.shape[1]):
                    ref[q, other] = jnp.zeros(ref.shape[2:], F32)

    lane_group = _iota((SGU_CHUNK, SGU_W), 1) // SGU_GROUP_W
    blk_r = _iota((ROW_BLOCK, ROW_BLOCK), 0)
    blk_c = _iota((ROW_BLOCK, ROW_BLOCK), 1)
    same_chunk = (blk_r // GLA_CHUNK) == (blk_c // GLA_CHUNK)
    tri = (jnp.where(same_chunk & (blk_c <= blk_r), 1.0, 0.0).astype(BF16),
           jnp.where(same_chunk & (blk_c >= blk_r), 1.0, 0.0).astype(BF16))
    chunks_per_blk = ROW_BLOCK // GLA_CHUNK
    head_of_lane = _iota((GLA_CHUNK, W_GLA), 1) // GLA_DK
    stack_r = _iota((GLA_HEADS * GLA_CHUNK, GLA_CHUNK), 0) % GLA_CHUNK
    stack_c = _iota((GLA_HEADS * GLA_CHUNK, GLA_CHUNK), 1)
    causal = (stack_c <= stack_r, stack_c >= stack_r)

    if latent:
        def cached_context():
            for h in range(DIFF_HEADS):
                k_ref[0:n_ctx, h * DIFF_V:(h + 1) * DIFF_V] = ck_ref[0, 0, h].astype(BF16)
                v_ref[0:n_ctx, h * DIFF_V:(h + 1) * DIFF_V] = cv_ref[0, 0, h].astype(BF16)
            st_ref[0] = st0_ref[0, 0]

        whole_sequence(cached_context)
        pair_lo = (_iota((ROW_BLOCK, W_QK), 1) % (DIFF_QK // 2)) < (DIFF_QK // 4)

        def rope(z, rows):
            cos = jnp.concatenate([cos_ref[rows, :]] * DIFF_HEADS, axis=-1)
            sin = jnp.concatenate([sin_ref[rows, :]] * DIFF_HEADS, axis=-1)
            shift = DIFF_QK // 4
            swapped = jnp.where(pair_lo, pltpu.roll(z, W_QK - shift, 1), pltpu.roll(z, shift, 1))
            return z * cos + swapped * sin

    def modulated_input(r):
        h = _row_rms(x_ref[block_rows(r), :]) * n1_ref[0]
        return (h * (1.0 + mod[1:2, :]) + mod[0:1, :]).astype(BF16)

    def spatial_gating(r):
        for c in range(ROW_BLOCK // SGU_CHUNK):
            local = slice(c * SGU_CHUNK, (c + 1) * SGU_CHUNK)
            u = jax.nn.gelu(proj_ref[local, C_AU:C_AU + SGU_W])
            v = _group_rms(jax.nn.gelu(proj_ref[local, C_AV:C_AV + SGU_W]), SGU_GROUP_W).astype(BF16)
            s = sb_ref[0]
            for g in range(SGU_GROUPS):
                s = s + jnp.where(lane_group == g, _dot(sw_ref[0, g], v), 0.0)
            mix_ref[aligned(r * ROW_BLOCK + c * SGU_CHUNK, SGU_CHUNK), M_A:M_A + SGU_W] = (u * s).astype(BF16)

    def attention_operands(r):
        rows = block_rows(r)
        key_rows = block_rows(r, n_ctx)
        seq, seq_rows = r // n_blk, block_rows(r % n_blk)
        qn = _group_rms(proj_ref[:, C_BQ:C_BQ + W_QK], DIFF_QK) * qg_ref[0]
        kn = _group_rms(proj_ref[:, C_BK:C_BK + W_QK], DIFF_QK) * kg_ref[0]
        vv = proj_ref[:, C_BV:C_BV + W_QK]
        if latent:
            qn, kn = rope(qn, rows), rope(kn, rows)
        else:
            for h in range(DIFF_HEADS):
                for i in range(2):
                    lo = h * DIFF_V + i * DIFF_QK
                    ko_ref[seq, 0, h, i, seq_rows, :] = kn[:, lo:lo + DIFF_QK]
                vo_ref[seq, 0, h, seq_rows, :] = vv[:, h * DIFF_V:(h + 1) * DIFF_V]
        q_ref[rows, :] = (qn * (DIFF_QK ** -0.5)).astype(BF16)
        k_ref[key_rows, :] = kn.astype(BF16)
        v_ref[key_rows, :] = vv.astype(BF16)

    def gla_operands(r):
        rows = block_rows(r)
        gpre = _dot(proj_ref[:, C_LR:C_LR + LANES].astype(BF16), w2c_ref[0]) + gb_ref[0]
        gate = _log_sigmoid(gpre) * (1.0 / GLA_GATE_NORM)
        gq = proj_ref[:, C_CQ:C_CQ + W_GLA] * (GLA_DK ** -0.5)
        gk = proj_ref[:, C_CK:C_CK + W_GLA]
        gv = proj_ref[:, C_CV:C_CV + W_GLA].astype(BF16)
        gv_ref[rows, :] = gv
        gr_ref[rows, :] = proj_ref[:, C_CR:C_CR + W_GLA]
        for d in range(2):
            g = gate[:, d * W_GLA:(d + 1) * W_GLA]
            b = sum(_dot(tri[d], p) for p in _split3(g))
            last = GLA_CHUNK - 1 if d == 0 else 0
            b_last = jnp.concatenate(
                [jnp.broadcast_to(b[c * GLA_CHUNK + last:c * GLA_CHUNK + last + 1, :], (GLA_CHUNK, W_GLA))
                 for c in range(chunks_per_blk)], axis=0)
            q_dec = (gq * jnp.exp(b)).astype(BF16)
            k_inv = (gk * jnp.exp(-b)).astype(BF16)
            gq_ref[d, rows, :] = q_dec
            gke_ref[d, rows, :] = (gk * jnp.exp(b_last - b)).astype(BF16)
            for c in range(chunks_per_blk):
                row = c * GLA_CHUNK + last
                dec_ref[d, r * chunks_per_blk + c] = jnp.exp(b[row:row + 1, :])
                chunk = slice(c * GLA_CHUNK, (c + 1) * GLA_CHUNK)
                qd = q_dec[chunk]
                q_stack = jnp.concatenate(
                    [jnp.where(head_of_lane == h, qd, jnp.zeros_like(qd)) for h in range(GLA_HEADS)], axis=0)
                attn = jnp.where(causal[d], _dot_nt(q_stack, k_inv[chunk]), 0.0)
                spread = _dot(attn.astype(BF16), gv[chunk])
                o = jnp.zeros((GLA_CHUNK, W_GLA), F32)
                for h in range(GLA_HEADS):
                    o = o + jnp.where(head_of_lane == h, spread[h * GLA_CHUNK:(h + 1) * GLA_CHUNK, :], 0.0)
                go_ref[d, aligned(r * ROW_BLOCK + c * GLA_CHUNK, GLA_CHUNK), :] = o

    def project_and_split(r):
        proj_ref[...] = _dot(modulated_input(r), win_ref[0])
        spatial_gating(r)
        attention_operands(r)
        gla_operands(r)

    blocks(project_and_split)

    dl = dl_ref[0]
    lam = (jnp.exp(jnp.sum(dl[0:1] * dl[1:2], axis=-1, keepdims=True))
           - jnp.exp(jnp.sum(dl[2:3] * dl[3:4], axis=-1, keepdims=True)) + lam_init)
    sub0 = (_iota((ROW_BLOCK, DIFF_V), 1) < DIFF_QK)

    def softmax(s):
        e = jnp.exp(s - jnp.max(s, axis=-1, keepdims=True))
        return e, jnp.sum(e, axis=-1, keepdims=True)

    def attn_block(r):
        rows = block_rows(r)
        keys = aligned((r // n_blk) * n_keys, n_keys)
        for h in range(DIFF_HEADS):
            cols = slice(h * DIFF_V, (h + 1) * DIFF_V)
            qh = q_ref[rows, cols]
            kh = k_ref[keys, cols]
            e0, z0 = softmax(_dot_nt(jnp.where(sub0, qh, jnp.zeros_like(qh)), kh))
            e1, z1 = softmax(_dot_nt(jnp.where(sub0, jnp.zeros_like(qh), qh), kh))
            w = e0 / z0 - lam * (e1 / z1)
            o = _dot(w.astype(BF16), v_ref[keys, cols])
            o = _row_rms(o) * dg_ref[0] * (1.0 - lam_init)
            mix_ref[rows, M_B + h * DIFF_V:M_B + (h + 1) * DIFF_V] = o.astype(BF16)

    blocks(attn_block)

    if not latent:
        st_ref[...] = jnp.zeros(st_ref.shape, F32)

    n_chunk = n_tok // GLA_CHUNK
    st_diag = (_iota((W_GLA, W_GLA), 0) // GLA_DV) == (_iota((W_GLA, W_GLA), 1) // GLA_DK)

    def gla_step(c, carry):
        for seq in range(n_par):
            for d in range(2):
                cc = seq * n_chunk + (c if d == 0 else n_chunk - 1 - c)
                rows = pl.ds(pl.multiple_of(cc * GLA_CHUNK, GLA_CHUNK), GLA_CHUNK)
                st = st_ref[seq, d]
                go_ref[d, rows, :] = go_ref[d, rows, :] + _dot_nt(gq_ref[d, rows, :], st.astype(BF16))
                upd = _dot_tn(gv_ref[rows, :], gke_ref[d, rows, :])
                st_ref[seq, d] = dec_ref[d, cc] * st + jnp.where(st_diag, upd, 0.0)
        return carry

    whole_sequence(lambda: lax.fori_loop(0, n_chunk, gla_step, 0))

    if not latent:
        for seq in range(n_par):
            for d in range(2):
                s_full = st_ref[seq, d].T
                for h in range(GLA_HEADS):
                    so_ref[seq, 0, d, h] = s_full[h * GLA_DK:(h + 1) * GLA_DK, h * GLA_DV:(h + 1) * GLA_DV]

    def finish_block(r, out_r):
        rows, out_rows = block_rows(r), block_rows(out_r)
        oc = _group_rms(go_ref[0, rows, :] + go_ref[1, rows, :], GLA_DV) * gg_ref[0]
        oc = oc * jax.nn.silu(gr_ref[rows, :])
        mix_ref[rows, M_C:M_C + W_GLA] = oc.astype(BF16)
        x1 = x_ref[rows, :] + mod[2:3, :] * _dot(mix_ref[rows, :], wout_ref[0])
        xo_ref[out_rows, :] = x1
        hn = _row_rms(x1) * n2_ref[0]
        hn = hn * (1.0 + mod[4:5, :]) + mod[3:4, :]
        idx1, idx2, w1, w2 = _route(hn, rwt_ref, rb_ref)
        slot1, slot2, counts = _local_slots(idx1, idx2)
        perm = _slot_one_hot(slot1, slot2, 1.0, 1.0).astype(BF16)
        _to_row_slabs(hs_ref, 2 * out_r * ROW_BLOCK, _dot(perm, hn.astype(BF16)))
        slot_ref[:, out_rows] = jnp.concatenate([slot1, slot2], axis=0).astype(jnp.int32)
        wt_ref[:, out_rows] = jnp.concatenate([w1, w2], axis=0)
        cnt_ref[out_r] = jnp.broadcast_to(counts, (N_EXPERTS, LANES)).astype(jnp.int32)

    if by_block:
        finish_block(pl.program_id(1), 0)
    else:
        for r in range(n_par * n_blk):
            finish_block(r, r)


def _mixer_call(l, n_tok, n_par, latent, x, mods_all, weights, extras, cache_bufs):
    n_seq = x.shape[0] // n_tok
    n_all = x.shape[0]
    assert n_seq % n_par == 0 and not (latent and n_par > 1)
    n_step_tok = n_par * n_tok
    n_keys = n_step_tok + (extras[0].shape[3] if latent else 0)
    n_chunk = n_step_tok // GLA_CHUNK
    lam_init = 0.8 - 0.6 * math.exp(-0.3 * l)

    single = pl.Buffered(1)

    def layer(arr):
        tail = arr.shape[1:]
        return pl.BlockSpec((1,) + tail, lambda s, *_r, _n=len(tail): (l,) + (0,) * _n, pipeline_mode=single)

    def const(arr):
        return pl.BlockSpec(arr.shape, lambda s, *_r, _n=arr.ndim: (0,) * _n, pipeline_mode=single)

    def tok_spec(width):
        return pl.BlockSpec((n_step_tok, width), lambda s, *_r: (s, 0))

    mod_row = (lambda s: 1 + s) if latent else (lambda s: 0)
    in_specs = [tok_spec(D_MODEL),
                pl.BlockSpec((1, 1, 6, D_MODEL), lambda s, *_r: (l, mod_row(s), 0, 0))]
    in_specs += [layer(w) for w in weights[:N_MIXER_WEIGHTS - 2]] + [const(w) for w in weights[-2:]]
    operands = [x, mods_all] + list(weights)
    if latent:
        ck, cv, st0, cos, sin = extras
        in_specs += [
            pl.BlockSpec((1, 1) + ck.shape[2:], lambda s, *_r: (s, l, 0, 0, 0)),
            pl.BlockSpec((1, 1) + cv.shape[2:], lambda s, *_r: (s, l, 0, 0, 0)),
            pl.BlockSpec((1, 1) + st0.shape[2:], lambda s, *_r: (s, l, 0, 0, 0)),
            const(cos), const(sin),
        ]
        operands += [ck, cv, st0, cos, sin]
    n_in = len(operands)
    in_specs += [pl.BlockSpec(memory_space=pl.ANY)] * len(cache_bufs)
    operands += list(cache_bufs)

    tiles_per_step = n_step_tok // ROW_BLOCK
    out_shape = [
        jax.ShapeDtypeStruct((n_all, D_MODEL), F32),
        jax.ShapeDtypeStruct((2 * n_all * ROW_SLABS, LANES), F32),
        jax.ShapeDtypeStruct((2, n_all), jnp.int32),
        jax.ShapeDtypeStruct((2, n_all), F32),
        jax.ShapeDtypeStruct((n_all // ROW_BLOCK, N_EXPERTS, LANES), jnp.int32),
    ]
    if latent:
        grid = (n_seq, tiles_per_step)
        out_tok, out_tiles = ROW_BLOCK, 1
        at = lambda s, r: s * tiles_per_step + r
    else:
        grid = (n_seq // n_par,)
        out_tok, out_tiles = n_step_tok, tiles_per_step
        at = lambda s: s
    out_specs = [
        pl.BlockSpec((out_tok, D_MODEL), lambda *g: (at(*g), 0)),
        pl.BlockSpec((2 * out_tok * ROW_SLABS, LANES), lambda *g: (at(*g), 0)),
        pl.BlockSpec((2, out_tok), lambda *g: (0, at(*g))),
        pl.BlockSpec((2, out_tok), lambda *g: (0, at(*g))),
        pl.BlockSpec((out_tiles, N_EXPERTS, LANES), lambda *g: (at(*g), 0, 0)),
    ]
    n_shared_out = len(out_shape)
    aliases = {}
    if not latent:
        out_shape += [
            jax.ShapeDtypeStruct((n_seq, DEPTH, DIFF_HEADS, 2, n_tok, DIFF_QK), F32),
            jax.ShapeDtypeStruct((n_seq, DEPTH, DIFF_HEADS, n_tok, DIFF_V), F32),
            jax.ShapeDtypeStruct((n_seq, DEPTH, 2, GLA_HEADS, GLA_DK, GLA_DV), F32),
        ]
        n_lay, lay = (1, l) if cache_bufs else (DEPTH, 0)
        out_specs += [
            pl.BlockSpec((n_par, n_lay, DIFF_HEADS, 2, n_tok, DIFF_QK), lambda s: (s, lay, 0, 0, 0, 0)),
            pl.BlockSpec((n_par, n_lay, DIFF_HEADS, n_tok, DIFF_V), lambda s: (s, lay, 0, 0, 0)),
            pl.BlockSpec((n_par, n_lay, 2, GLA_HEADS, GLA_DK, GLA_DV), lambda s: (s, lay, 0, 0, 0, 0)),
        ]
        aliases = {n_in + j: n_shared_out + j for j in range(len(cache_bufs))}
    scratch = [
        pltpu.VMEM((ROW_BLOCK, D_PROJ_PAD), F32),
        pltpu.VMEM((n_step_tok, D_MODEL), BF16),
        pltpu.VMEM((n_step_tok, W_QK), BF16),
        pltpu.VMEM((n_keys, W_QK), BF16),
        pltpu.VMEM((n_keys, W_QK), BF16),
        pltpu.VMEM((2, n_step_tok, W_GLA), BF16),
        pltpu.VMEM((2, n_step_tok, W_GLA), BF16),
        pltpu.VMEM((n_step_tok, W_GLA), BF16),
        pltpu.VMEM((n_step_tok, W_GLA), F32),
        pltpu.VMEM((2, n_chunk, 1, W_GLA), F32),
        pltpu.VMEM((2, n_step_tok, W_GLA), F32),
        pltpu.VMEM((n_par, 2, W_GLA, W_GLA), F32),
    ]
    return pl.pallas_call(
        functools.partial(_mixer_kernel, n_tok, n_par, latent, len(cache_bufs), lam_init),
        grid=grid,
        in_specs=in_specs,
        out_specs=out_specs,
        out_shape=out_shape,
        scratch_shapes=scratch,
        input_output_aliases=aliases,
        compiler_params=pltpu.CompilerParams(
            dimension_semantics=("arbitrary",) * len(grid), vmem_limit_bytes=MIXER_VMEM_LIMIT),
        name="mixer_latent" if latent else "mixer_context",
    )(*operands)


PAIR_BLOCK = 2 * ROW_BLOCK
COPY_SIZES = tuple(ROW_BLOCK >> k for k in range(ROW_BLOCK.bit_length()))
LARGE_COPY = 64
GATHER_AHEAD = 3
GATHER_SLOTS = GATHER_AHEAD + 1


def _segment_copies(n_rows, make_copy, act):
    def copy_if_set(size):
        @pl.when((n_rows & size) != 0)
        def _():
            act(make_copy(n_rows & (-2 * size), size))

    n_large = COPY_SIZES.index(LARGE_COPY) + 1

    @pl.when(n_rows >= LARGE_COPY)
    def _():
        for size in COPY_SIZES[:n_large]:
            copy_if_set(size)

    for size in COPY_SIZES[n_large:]:
        copy_if_set(size)


def _start(copy):
    copy.start()


def _wait(copy):
    copy.wait()


def _slab_rows(first_row, n_rows, slab):
    return pl.ds(first_row * ROW_SLABS + slab, n_rows, stride=ROW_SLABS)


def _to_row_slabs(ref, first_row, value):
    for s in range(ROW_SLABS):
        ref[_slab_rows(first_row, value.shape[0], s), :] = value[:, s * LANES:(s + 1) * LANES]


def _from_row_slabs(ref, first_row, n_rows):
    return jnp.concatenate([ref[_slab_rows(first_row, n_rows, s), :] for s in range(ROW_SLABS)], axis=-1)


SLAB_PAIR_W = 2 * LANES
N_SLAB_PAIRS = ROW_SLABS // 2


def _slab_pair_cols(g):
    return slice(g * SLAB_PAIR_W, (g + 1) * SLAB_PAIR_W)


def _load_slab_pair(ref, first_row, n_rows, g):
    return jnp.concatenate([ref[_slab_rows(first_row, n_rows, s), :] for s in (2 * g, 2 * g + 1)], axis=-1)


def _row_span(ref, first_row, n_rows):
    return ref.at[pl.ds(pl.multiple_of(first_row * ROW_SLABS, ROW_SLABS), n_rows * ROW_SLABS)]


def _two_streams(n_first_tiles):
    def first(i, *_):
        return (jnp.minimum(i, n_first_tiles - 1), 0)

    def second(i, *_):
        return (jnp.maximum(i - n_first_tiles, 0), 0)

    return first, second


N_EXPERT_TABLES = 10
OUT_SLOTS = 2


def _expert_kernel(n_ctx_tiles, n_tiles_max, te_ref, first_ref, rows_ref, jlo_ref, jhi_ref, cpre_ref, cnt_ref,
                   lofs_ref, tile0_ref, ntile_ref, hs_c_ref, hs_l_ref, w1_ref, w3_ref, w2_ref, ys_ref,
                   xbuf_ref, obuf_ref, w1b_ref, w3b_ref, w2b_ref, sem, out_sem):
    expert = pl.program_id(0)
    n_tiles = tile0_ref[N_EXPERTS]

    def gather(t, act):
        slot = t % GATHER_SLOTS
        e, first = te_ref[t], first_ref[t]
        last = first + rows_ref[t]

        def segment_of(hs_ref, first_tile):
            def body(j, carry):
                k = j * N_EXPERTS + e
                seg_first = cpre_ref[k]
                lo = jnp.maximum(seg_first, first)
                n = jnp.maximum(jnp.minimum(seg_first + cnt_ref[k], last) - lo, 0)
                src = (j - first_tile) * PAIR_BLOCK + lofs_ref[k] + (lo - seg_first)
                dst = slot * ROW_BLOCK + lo - first
                _segment_copies(n, lambda done, size: pltpu.make_async_copy(
                    _row_span(hs_ref, src + done, size), _row_span(xbuf_ref, dst + done, size), sem.at[slot]), act)
                return carry
            return body

        jlo, jhi = jlo_ref[t], jhi_ref[t]
        lax.fori_loop(jnp.minimum(jlo, n_ctx_tiles), jnp.minimum(jhi, n_ctx_tiles), segment_of(hs_c_ref, 0), 0)
        lax.fori_loop(jnp.maximum(jlo, n_ctx_tiles), jnp.maximum(jhi, n_ctx_tiles),
                      segment_of(hs_l_ref, n_ctx_tiles), 0)

    def out_copy(t, oslot):
        return pltpu.make_async_copy(
            _row_span(obuf_ref, oslot * ROW_BLOCK, ROW_BLOCK), _row_span(ys_ref, t * ROW_BLOCK, ROW_BLOCK),
            out_sem.at[oslot])

    @pl.when(expert == 0)
    def _():
        xbuf_ref[...] = jnp.zeros(xbuf_ref.shape, F32)
        for t in range(GATHER_AHEAD):
            gather(t, _start)

    w1b_ref[...] = w1_ref[0, 0].astype(BF16)
    w3b_ref[...] = w3_ref[0, 0].astype(BF16)
    w2b_ref[...] = w2_ref[0, 0].astype(BF16)
    tile0, n_own = tile0_ref[expert], ntile_ref[expert]

    def tile_body(k, carry):
        t = tile0 + k
        slot, oslot = t % GATHER_SLOTS, t % OUT_SLOTS

        @pl.when(t + GATHER_AHEAD < n_tiles)
        def _():
            gather(t + GATHER_AHEAD, _start)

        n_rows = rows_ref[t]
        _segment_copies(n_rows, lambda done, size: pltpu.make_async_copy(
            _row_span(hs_c_ref, done, size), _row_span(xbuf_ref, slot * ROW_BLOCK + done, size), sem.at[slot]), _wait)

        @pl.when(t >= OUT_SLOTS)
        def _():
            out_copy(t, oslot).wait()

        live = _iota((ROW_BLOCK, D_MODEL), 0) < n_rows
        x = jnp.where(live, _from_row_slabs(xbuf_ref, slot * ROW_BLOCK, ROW_BLOCK), 0.0).astype(BF16)
        hid = jax.nn.silu(_dot(x, w1b_ref[...])) * _dot(x, w3b_ref[...])
        _to_row_slabs(obuf_ref, oslot * ROW_BLOCK, _dot(hid.astype(BF16), w2b_ref[...]))
        out_copy(t, oslot).start()
        return carry

    lax.fori_loop(0, n_own, tile_body, 0)

    @pl.when(expert == N_EXPERTS - 1)
    def _():
        for oslot in range(OUT_SLOTS):
            @pl.when(n_tiles > oslot)
            def _():
                out_copy(0, oslot).wait()
        obuf_ref[...] = jnp.zeros(obuf_ref.shape, F32)

        def fill(t, carry):
            out_copy(t, 0).start()
            out_copy(t, 0).wait()
            return carry

        lax.fori_loop(n_tiles, n_tiles_max, fill, 0)


def _expert_call(l, plan, hs_c, hs_l, w1, w3, w2):
    tables = plan["expert_tables"]
    n_tiles_max = tables[0].shape[0]
    n_ctx_tiles = hs_c.shape[0] // (PAIR_BLOCK * ROW_SLABS)

    def weight(shape):
        return pl.BlockSpec((1, 1) + shape, lambda e, *_: (l, e, 0, 0))

    return pl.pallas_call(
        functools.partial(_expert_kernel, n_ctx_tiles, n_tiles_max),
        grid_spec=pltpu.PrefetchScalarGridSpec(
            num_scalar_prefetch=N_EXPERT_TABLES,
            grid=(N_EXPERTS,),
            in_specs=[pl.BlockSpec(memory_space=pl.ANY), pl.BlockSpec(memory_space=pl.ANY),
                      weight((D_MODEL, D_EXPERT)), weight((D_MODEL, D_EXPERT)), weight((D_EXPERT, D_MODEL))],
            out_specs=pl.BlockSpec(memory_space=pl.ANY),
            scratch_shapes=[pltpu.VMEM((GATHER_SLOTS * ROW_BLOCK * ROW_SLABS, LANES), F32),
                            pltpu.VMEM((OUT_SLOTS * ROW_BLOCK * ROW_SLABS, LANES), F32),
                            pltpu.VMEM((D_MODEL, D_EXPERT), BF16), pltpu.VMEM((D_MODEL, D_EXPERT), BF16),
                            pltpu.VMEM((D_EXPERT, D_MODEL), BF16),
                            pltpu.SemaphoreType.DMA((GATHER_SLOTS,)), pltpu.SemaphoreType.DMA((OUT_SLOTS,))],
        ),
        out_shape=jax.ShapeDtypeStruct((n_tiles_max * ROW_BLOCK * ROW_SLABS, LANES), F32),
        compiler_params=pltpu.CompilerParams(
            dimension_semantics=("arbitrary",), vmem_limit_bytes=SMALL_KERNEL_VMEM_LIMIT),
        name="moe_experts",
    )(*tables, hs_c, hs_l, w1, w3, w2)


N_COMBINE_TABLES = 4
COMBINE_TILES_PER_STEP = 2


def _combine_kernel(n_ctx_steps, cnt_ref, cpre_ref, lofs_ref, starts_ref, x_c_ref, x_l_ref, slot_c_ref, slot_l_ref,
                    wt_c_ref, wt_l_ref, mod_ref, ys_ref, xo_c_ref, xo_l_ref, buf_ref, sem):
    step = pl.program_id(0)
    n_tiles = pl.num_programs(0) * COMBINE_TILES_PER_STEP

    def collect(t, act):
        slot = t % GATHER_SLOTS

        def body(e, carry):
            k = t * N_EXPERTS + e
            src, dst = starts_ref[e] + cpre_ref[k], slot * PAIR_BLOCK + lofs_ref[k]
            _segment_copies(cnt_ref[k], lambda done, size: pltpu.make_async_copy(
                _row_span(ys_ref, src + done, size), _row_span(buf_ref, dst + done, size), sem.at[slot]), act)
            return carry

        lax.fori_loop(0, N_EXPERTS, body, 0)

    @pl.when(step == 0)
    def _():
        for t in range(GATHER_AHEAD):
            collect(t, _start)

    gate = mod_ref[0, 0, 5:6, :]

    def finish(part, first_row, x_ref, slot_ref, wt_ref, xo_ref):
        tokens = pl.ds(part * ROW_BLOCK, ROW_BLOCK)
        slots, wts = slot_ref[:, tokens].astype(F32), wt_ref[:, tokens]
        slot1, slot2 = slots[0:1], slots[1:2]
        weight_of_row = jnp.sum(_slot_one_hot(slot1, slot2, wts[0:1], wts[1:2]), axis=1, keepdims=True)
        gather_rows = _slot_one_hot(slot1, slot2, 1.0, 1.0).T.astype(BF16)
        for g in range(N_SLAB_PAIRS):
            cols = _slab_pair_cols(g)
            hi, lo = _split2(_load_slab_pair(buf_ref, first_row, PAIR_BLOCK, g) * weight_of_row)
            y = _dot(gather_rows, hi) + _dot(gather_rows, lo)
            xo_ref[tokens, cols] = x_ref[tokens, cols] + gate[:, cols] * y

    for part in range(COMBINE_TILES_PER_STEP):
        j = step * COMBINE_TILES_PER_STEP + part

        @pl.when(j + GATHER_AHEAD < n_tiles)
        def _():
            collect(j + GATHER_AHEAD, _start)

        slot = j % GATHER_SLOTS
        for piece in range(PAIR_BLOCK // ROW_BLOCK):
            pltpu.make_async_copy(
                _row_span(ys_ref, piece * ROW_BLOCK, ROW_BLOCK),
                _row_span(buf_ref, slot * PAIR_BLOCK + piece * ROW_BLOCK, ROW_BLOCK), sem.at[slot]).wait()
        @pl.when(step < n_ctx_steps)
        def _():
            finish(part, slot * PAIR_BLOCK, x_c_ref, slot_c_ref, wt_c_ref, xo_c_ref)

        @pl.when(step >= n_ctx_steps)
        def _():
            finish(part, slot * PAIR_BLOCK, x_l_ref, slot_l_ref, wt_l_ref, xo_l_ref)


def _combine_call(l, plan, x_c, x_l, slot_c, slot_l, wt_c, wt_l, mods_all, mod_row_of_tile, ys):
    step_rows = COMBINE_TILES_PER_STEP * ROW_BLOCK
    n_steps = (x_c.shape[0] + x_l.shape[0]) // step_rows
    n_ctx_steps = x_c.shape[0] // step_rows
    first, second = _two_streams(n_ctx_steps)

    def lanes(index_map):
        return lambda i, *_: index_map(i)[::-1]

    return pl.pallas_call(
        functools.partial(_combine_kernel, n_ctx_steps),
        grid_spec=pltpu.PrefetchScalarGridSpec(
            num_scalar_prefetch=N_COMBINE_TABLES,
            grid=(n_steps,),
            in_specs=[pl.BlockSpec((step_rows, D_MODEL), first),
                      pl.BlockSpec((step_rows, D_MODEL), second),
                      pl.BlockSpec((2, step_rows), lanes(first)),
                      pl.BlockSpec((2, step_rows), lanes(second)),
                      pl.BlockSpec((2, step_rows), lanes(first)),
                      pl.BlockSpec((2, step_rows), lanes(second)),
                      pl.BlockSpec((1, 1, 6, D_MODEL),
                                   lambda i, *_: (l, mod_row_of_tile(i * COMBINE_TILES_PER_STEP), 0, 0)),
                      pl.BlockSpec(memory_space=pl.ANY)],
            out_specs=[pl.BlockSpec((step_rows, D_MODEL), first),
                       pl.BlockSpec((step_rows, D_MODEL), second)],
            scratch_shapes=[pltpu.VMEM((GATHER_SLOTS * PAIR_BLOCK * ROW_SLABS, LANES), F32),
                            pltpu.SemaphoreType.DMA((GATHER_SLOTS,))],
        ),
        out_shape=[jax.ShapeDtypeStruct(x_c.shape, F32), jax.ShapeDtypeStruct(x_l.shape, F32)],
        compiler_params=pltpu.CompilerParams(
            dimension_semantics=("arbitrary",), vmem_limit_bytes=SMALL_KERNEL_VMEM_LIMIT),
        name="moe_combine",
    )(*plan["combine_tables"], x_c, x_l, slot_c, slot_l, wt_c, wt_l, mods_all, ys)


def _moe_plan(cnt):
    n_tok_tiles = cnt.shape[0]
    n_tiles = n_tok_tiles * PAIR_BLOCK // ROW_BLOCK + N_EXPERTS
    lofs = jnp.cumsum(cnt, axis=1) - cnt
    cpre = jnp.cumsum(cnt, axis=0) - cnt
    counts = jnp.sum(cnt, axis=0)
    padded = (counts + ROW_BLOCK - 1) // ROW_BLOCK * ROW_BLOCK
    ends = jnp.cumsum(padded)
    starts = ends - padded
    tile_start = jnp.arange(n_tiles, dtype=jnp.int32) * ROW_BLOCK
    tile_expert = jnp.minimum(
        jnp.sum((tile_start[:, None] >= ends[None, :]).astype(jnp.int32), axis=1), N_EXPERTS - 1)
    hot = tile_expert[:, None] == jnp.arange(N_EXPERTS, dtype=jnp.int32)[None, :]
    first = tile_start - jnp.sum(jnp.where(hot, starts[None, :], 0), axis=1)
    rows = jnp.clip(jnp.sum(jnp.where(hot, counts[None, :], 0), axis=1) - first, 0, ROW_BLOCK)
    seg_first = jnp.sum(jnp.where(hot[:, None, :], cpre[None, :, :], 0), axis=2)
    seg_rows = jnp.sum(jnp.where(hot[:, None, :], cnt[None, :, :], 0), axis=2)
    overlap = (seg_first < (first + rows)[:, None]) & (seg_first + seg_rows > first[:, None])
    j = jnp.arange(n_tok_tiles, dtype=jnp.int32)[None, :]
    jlo = jnp.min(jnp.where(overlap, j, n_tok_tiles), axis=1)
    jhi = jnp.max(jnp.where(overlap, j + 1, 0), axis=1)
    i32 = lambda a: a.astype(jnp.int32).reshape(-1)
    tile0 = jnp.concatenate([starts, ends[-1:]]) // ROW_BLOCK
    return {
        "expert_tables": tuple(i32(a) for a in (tile_expert, first, rows, jlo, jhi, cpre, cnt, lofs,
                                                tile0, padded // ROW_BLOCK)),
        "combine_tables": tuple(i32(a) for a in (cnt, cpre, lofs, starts)),
    }


def _rope_tables(n_tok):
    n_rows = n_tok // GRID_W
    pos_r = jnp.repeat(jnp.arange(n_rows), GRID_W)
    pos_c = jnp.tile(jnp.arange(GRID_W), n_rows)
    half = DIFF_QK // 2
    nf = half // 2
    freqs = ROPE_BASE ** (-jnp.arange(nf, dtype=F32) / nf)

    def tables(pos):
        ang = pos.astype(F32)[:, None] * freqs
        cos, sin = jnp.cos(ang), jnp.sin(ang)
        return jnp.concatenate([cos, cos], axis=-1), jnp.concatenate([-sin, sin], axis=-1)

    cos_r, sin_r = tables(pos_r)
    cos_c, sin_c = tables(pos_c)
    cos = jnp.concatenate([cos_r, cos_c], axis=-1)
    sin = jnp.concatenate([sin_r, sin_c], axis=-1)
    return jnp.concatenate([cos, cos], axis=-1), jnp.concatenate([sin, sin], axis=-1)


def _mixer_weights(w_in, w_out, sgu_w, sgu_b, q_norm_g, k_norm_g, diff_lambda, diff_norm_g, gla_w2, gla_b,
                   gla_norm_g, norm1_g, norm2_g, router_w, router_bias):
    w_in_pad = jnp.concatenate(
        [w_in.astype(BF16), jnp.zeros(w_in.shape[:2] + (D_PROJ_PAD - w_in.shape[2],), BF16)], axis=-1)
    w2cat = jnp.zeros((DEPTH, LANES, 2 * W_GLA), F32)
    w2cat = w2cat.at[:, 0:GLA_RANK, 0:W_GLA].set(gla_w2[:, 0]).at[:, GLA_RANK:2 * GLA_RANK, W_GLA:].set(gla_w2[:, 1])
    return (
        norm1_g[:, None, :], norm2_g[:, None, :], w_in_pad, w_out.astype(BF16),
        sgu_w.astype(BF16), jnp.repeat(sgu_b.transpose(0, 2, 1), SGU_GROUP_W, axis=2),
        jnp.tile(q_norm_g, (1, W_QK // DIFF_QK))[:, None, :], jnp.tile(k_norm_g, (1, W_QK // DIFF_QK))[:, None, :],
        diff_lambda, diff_norm_g[:, None, :],
        w2cat.astype(BF16), gla_b.reshape(DEPTH, 1, 2 * W_GLA), jnp.tile(gla_norm_g, (1, GLA_HEADS))[:, None, :],
        router_w.T, router_bias[:, None],
    )


def kernel(x_prompt, x_sample, cache_k, cache_v, state_gla, c, c_ctx, w_in, w_out, sgu_w, sgu_b, q_norm_g, k_norm_g,
           diff_lambda, diff_norm_g, gla_w2, gla_b, gla_norm_g, norm1_g, norm2_g, ada_w, ada_b, router_w, router_bias,
           moe_w1, moe_w3, moe_w2):
    n_ctx_seq, ctx_len, _ = x_prompt.shape
    n_lat_seq, lat_len, _ = x_sample.shape
    n_ctx_tok = n_ctx_seq * ctx_len
    n_lat_tok = n_lat_seq * lat_len
    ctx_tiles = n_ctx_tok // ROW_BLOCK
    lat_tiles_per_seq = lat_len // ROW_BLOCK

    n_cond = 1 + n_lat_seq
    cond_t = jnp.zeros((D_MODEL, SUBLANES), F32).at[:, 0].set(c_ctx).at[:, 1:n_cond].set(c.T)
    mods_all = _adaln_call(cond_t, n_cond, ada_w, ada_b)[:, :n_cond].reshape(DEPTH, n_cond, 6, D_MODEL)
    weights = _mixer_weights(w_in, w_out, sgu_w, sgu_b, q_norm_g, k_norm_g, diff_lambda, diff_norm_g, gla_w2, gla_b,
                             gla_norm_g, norm1_g, norm2_g, router_w, router_bias)

    ck_all = cache_k.transpose(0, 1, 2, 4, 3, 5).reshape(cache_k.shape[:3] + (cache_k.shape[4], DIFF_V))
    st_all = jnp.einsum('bldhkv,hg->bldhvgk', state_gla, jnp.eye(GLA_HEADS, dtype=F32)).reshape(
        n_lat_seq, DEPTH, 2, W_GLA, W_GLA)
    cos, sin = _rope_tables(lat_len)
    extras = (ck_all, cache_v, st_all, cos, sin)

    def mod_row_of_tile(i):
        return jnp.where(i < ctx_tiles, 0, 1 + (i - ctx_tiles) // lat_tiles_per_seq)

    x_c = x_prompt.reshape(n_ctx_tok, D_MODEL)
    x_l = x_sample.reshape(n_lat_tok, D_MODEL)
    cache_bufs = ()
    for l in range(DEPTH):
        ctx_par = 1 if l == 0 else CTX_SEQS_PER_STEP
        x1_c, hs_c, slot_c, wt_c, cnt_c, *cache_bufs = _mixer_call(
            l, ctx_len, ctx_par, False, x_c, mods_all, weights, None, tuple(cache_bufs))
        x1_l, hs_l, slot_l, wt_l, cnt_l = _mixer_call(l, lat_len, 1, True, x_l, mods_all, weights, extras, ())
        plan = _moe_plan(jnp.concatenate([cnt_c[:, :, 0], cnt_l[:, :, 0]], axis=0))
        ys = _expert_call(l, plan, hs_c, hs_l, moe_w1, moe_w3, moe_w2)
        x_c, x_l = _combine_call(l, plan, x1_c, x1_l, slot_c, slot_l, wt_c, wt_l, mods_all, mod_row_of_tile, ys)

    new_k, new_v, new_s = cache_bufs
    return (x_c.reshape(x_prompt.shape), x_l.reshape(x_sample.shape), new_k, new_v, new_s)
```

```python
import functools
import math

import jax
import jax.numpy as jnp
from jax import lax
from jax.experimental import pallas as pl
from jax.experimental.pallas import tpu as pltpu

F32 = jnp.float32
BF16 = jnp.bfloat16

D_MODEL = 1024
DEPTH = 4
GRID_W = 64
SGU_GROUPS = 4
SGU_GROUP_W = 64
SGU_W = SGU_GROUPS * SGU_GROUP_W
SGU_CHUNK = 128
DIFF_HEADS = 4
DIFF_QK = 64
DIFF_V = 2 * DIFF_QK
ROPE_BASE = 10000.0
GLA_HEADS = 4
GLA_DK = 64
GLA_DV = 64
GLA_RANK = 16
GLA_GATE_NORM = 16.0
GLA_CHUNK = 64
N_EXPERTS = 16
N_GROUPS = 4
EXPERTS_PER_GROUP = N_EXPERTS // N_GROUPS
D_EXPERT = 512
EPS = 1e-6

LANES = 128
SUBLANES = 8
MXU_DIM = 256
V7X_VMEM_BYTES = 64 * 1024 * 1024
MIXER_VMEM_LIMIT = V7X_VMEM_BYTES * 7 // 8
SMALL_KERNEL_VMEM_LIMIT = V7X_VMEM_BYTES * 5 // 8

ROW_BLOCK = MXU_DIM
ROW_SLABS = D_MODEL // LANES

W_QK = DIFF_HEADS * 2 * DIFF_QK
W_GLA = GLA_HEADS * GLA_DK
C_AU, C_AV = 0, SGU_W
C_BQ = C_AV + SGU_W
C_BK, C_BV = C_BQ + W_QK, C_BQ + 2 * W_QK
C_CQ = C_BV + DIFF_HEADS * DIFF_V
C_CK, C_CV, C_CR, C_LR = C_CQ + W_GLA, C_CQ + 2 * W_GLA, C_CQ + 3 * W_GLA, C_CQ + 4 * W_GLA
D_PROJ_MAIN = C_LR
D_PROJ_PAD = D_PROJ_MAIN + LANES
M_A, M_B, M_C = 0, SGU_W, SGU_W + DIFF_HEADS * DIFF_V


def _split2(x):
    hi = x.astype(BF16)
    lo = (x - hi.astype(F32)).astype(BF16)
    return hi, lo


def _split3(x):
    hi = x.astype(BF16)
    r = x - hi.astype(F32)
    mid = r.astype(BF16)
    lo = (r - mid.astype(F32)).astype(BF16)
    return hi, mid, lo


def _dot(a, b):
    return jnp.dot(a, b, preferred_element_type=F32)


def _dot_nt(a, b):
    return lax.dot_general(a, b, (((1,), (1,)), ((), ())), preferred_element_type=F32)


def _dot_tn(a, b):
    return lax.dot_general(a, b, (((0,), (0,)), ((), ())), preferred_element_type=F32)


def _iota(shape, dim):
    return lax.broadcasted_iota(jnp.int32, shape, dim)


def _block_ones(width, block):
    r = _iota((width, width), 0) // block
    c = _iota((width, width), 1) // block
    return (r == c)


def _group_sum(z, block):
    width = z.shape[-1]
    outs = []
    for s in range(0, width, MXU_DIM):
        w = min(MXU_DIM, width - s)
        ones = _block_ones(w, block).astype(BF16)
        hi, lo = _split2(z[:, s:s + w])
        outs.append(_dot(hi, ones) + _dot(lo, ones))
    return outs[0] if len(outs) == 1 else jnp.concatenate(outs, axis=-1)


def _group_rms(z, block):
    ms = _group_sum(z * z, block) * (1.0 / block)
    return z * lax.rsqrt(ms + EPS)


def _row_rms(z):
    return z * lax.rsqrt(jnp.mean(z * z, axis=-1, keepdims=True) + EPS)


def _log_sigmoid(x):
    return jnp.minimum(x, 0.0) - jnp.log(1.0 + jnp.exp(-jnp.abs(x)))


ADA_COLS = 1536


def _adaln_kernel(n_cond, cond_t_ref, w_ref, b_ref, o_ref):
    sc = jax.nn.silu(cond_t_ref[...])
    w = w_ref[0]
    rows = [jnp.sum(sc[:, r:r + 1] * w, axis=0, keepdims=True) + b_ref[0] for r in range(n_cond)]
    o_ref[0] = jnp.concatenate(rows + [jnp.zeros((SUBLANES - n_cond, w.shape[1]), F32)], axis=0)


def _adaln_call(cond_t, n_cond, ada_w, ada_b):
    n_col = 6 * D_MODEL // ADA_COLS
    return pl.pallas_call(
        functools.partial(_adaln_kernel, n_cond),
        grid=(DEPTH, n_col),
        in_specs=[
            pl.BlockSpec((D_MODEL, SUBLANES), lambda l, j: (0, 0)),
            pl.BlockSpec((1, D_MODEL, ADA_COLS), lambda l, j: (l, 0, j)),
            pl.BlockSpec((1, 1, ADA_COLS), lambda l, j: (l, 0, j)),
        ],
        out_specs=pl.BlockSpec((1, SUBLANES, ADA_COLS), lambda l, j: (l, 0, j)),
        out_shape=jax.ShapeDtypeStruct((DEPTH, SUBLANES, 6 * D_MODEL), F32),
        compiler_params=pltpu.CompilerParams(
            dimension_semantics=("arbitrary", "arbitrary"), vmem_limit_bytes=SMALL_KERNEL_VMEM_LIMIT),
        name="adaln",
    )(cond_t, ada_w, ada_b.reshape(DEPTH, 1, 6 * D_MODEL))


def _route(hn, rwt_ref, rb_ref):
    h_hi, h_lo = _split2(hn)
    rw = rwt_ref[...]
    rw_hi = rw.astype(BF16)
    rw_lo = (rw - rw_hi.astype(F32)).astype(BF16)
    logits = _dot_nt(rw_hi, h_hi) + _dot_nt(rw_hi, h_lo) + _dot_nt(rw_lo, h_hi)
    aff = jax.nn.sigmoid(logits)
    sel = aff + rb_ref[...]
    n_tok = sel.shape[1]

    def top2_sum(a, b, c, d):
        hi1, lo1 = jnp.maximum(a, b), jnp.minimum(a, b)
        hi2, lo2 = jnp.maximum(c, d), jnp.minimum(c, d)
        return jnp.maximum(hi1, hi2) + jnp.maximum(jnp.minimum(hi1, hi2), jnp.maximum(lo1, lo2))

    scores = []
    for g in range(N_GROUPS):
        rows = [sel[EXPERTS_PER_GROUP * g + j:EXPERTS_PER_GROUP * g + j + 1, :] for j in range(EXPERTS_PER_GROUP)]
        scores.append(top2_sum(*rows))
    best = jnp.zeros((1, n_tok), jnp.int32)
    best_score = scores[0]
    for g in range(1, N_GROUPS):
        upd = scores[g] > best_score
        best = jnp.where(upd, g, best)
        best_score = jnp.where(upd, scores[g], best_score)

    eid_i = _iota((N_EXPERTS, n_tok), 0)
    eid = eid_i.astype(F32)
    neg = jnp.float32(-jnp.inf)
    msel = jnp.where(eid_i // EXPERTS_PER_GROUP == best, sel, neg)
    m1 = jnp.max(msel, axis=0, keepdims=True)
    idx1 = jnp.min(jnp.where(msel == m1, eid, float(N_EXPERTS)), axis=0, keepdims=True)
    msel2 = jnp.where(eid == idx1, neg, msel)
    m2 = jnp.max(msel2, axis=0, keepdims=True)
    idx2 = jnp.min(jnp.where(msel2 == m2, eid, float(N_EXPERTS)), axis=0, keepdims=True)
    w1 = jnp.sum(jnp.where(eid == idx1, aff, 0.0), axis=0, keepdims=True)
    w2 = jnp.sum(jnp.where(eid == idx2, aff, 0.0), axis=0, keepdims=True)
    wsum = w1 + w2
    return idx1.astype(jnp.int32), idx2.astype(jnp.int32), w1 / wsum, w2 / wsum


def _local_slots(idx1, idx2):
    n_tok = idx1.shape[1]
    eid = _iota((N_EXPERTS, n_tok), 0)
    hot1, hot2 = eid == idx1, eid == idx2
    hot = jnp.where(hot1, 1.0, jnp.where(hot2, 1.0, 0.0))
    earlier = jnp.where(_iota((n_tok, n_tok), 0) < _iota((n_tok, n_tok), 1), 1.0, 0.0).astype(BF16)
    before_in_expert = _dot(hot.astype(BF16), earlier)
    counts = jnp.sum(hot, axis=1, keepdims=True)
    lower = jnp.where(_iota((N_EXPERTS, N_EXPERTS), 1) < _iota((N_EXPERTS, N_EXPERTS), 0), 1.0, 0.0).astype(BF16)
    first_slot = _dot(lower, jnp.broadcast_to(counts, (N_EXPERTS, LANES)).astype(BF16))[:, 0:1]
    slot = before_in_expert + first_slot
    slot1 = jnp.sum(jnp.where(hot1, slot, 0.0), axis=0, keepdims=True)
    slot2 = jnp.sum(jnp.where(hot2, slot, 0.0), axis=0, keepdims=True)
    return slot1, slot2, counts


def _slot_one_hot(slot1, slot2, v1, v2):
    n_tok = slot1.shape[1]
    row = _iota((2 * n_tok, n_tok), 0).astype(F32)
    return jnp.where(row == slot1, v1, jnp.where(row == slot2, v2, 0.0))


N_MIXER_WEIGHTS = 15
CTX_SEQS_PER_STEP = 2
MAX_INLINE_BLOCKS = 2


def _mixer_kernel(n_tok, n_par, latent, n_alias, lam_init, *refs):
    it = iter(refs)
    x_ref, mod_ref = next(it), next(it)
    (n1_ref, n2_ref, win_ref, wout_ref, sw_ref, sb_ref, qg_ref, kg_ref, dl_ref, dg_ref,
     w2c_ref, gb_ref, gg_ref, rwt_ref, rb_ref) = (next(it) for _ in range(N_MIXER_WEIGHTS))
    if latent:
        ck_ref, cv_ref, st0_ref, cos_ref, sin_ref = (next(it) for _ in range(5))
    for _ in range(n_alias):
        next(it)
    xo_ref, hs_ref, slot_ref, wt_ref, cnt_ref = (next(it) for _ in range(5))
    if not latent:
        ko_ref, vo_ref, so_ref = (next(it) for _ in range(3))
    proj_ref, mix_ref, q_ref, k_ref, v_ref = (next(it) for _ in range(5))
    gq_ref, gke_ref, gv_ref, gr_ref, dec_ref, go_ref, st_ref = (next(it) for _ in range(7))

    n_blk = n_tok // ROW_BLOCK
    n_ctx = k_ref.shape[0] - n_par * n_tok
    n_keys = n_ctx + n_tok
    mod = mod_ref[0, 0]

    by_block = latent

    def whole_sequence(fn):
        def run():
            fn()

        if by_block:
            pl.when(pl.program_id(1) == 0)(run)
        else:
            run()

    def blocks(body):
        if n_par * n_blk <= MAX_INLINE_BLOCKS:
            for r in range(n_par * n_blk):
                body(r)
        else:
            def step(r, carry):
                body(r)
                return carry
            whole_sequence(lambda: lax.fori_loop(0, n_par * n_blk, step, 0))

    def aligned(start, size):
        return pl.ds(start if isinstance(start, int) else pl.multiple_of(start, size), size)

    def block_rows(r, offset=0):
        return aligned(offset + r * ROW_BLOCK, ROW_BLOCK)

    if not latent:
        for ref in (ko_ref, vo_ref, so_ref):
            for q in range(n_par):
                for other in range(1, ref.shape[1]):
                    ref[q, other] = jnp.zeros(ref.shape[2:], F32)

    lane_group = _iota((SGU_CHUNK, SGU_W), 1) // SGU_GROUP_W
    blk_r = _iota((ROW_BLOCK, ROW_BLOCK), 0)
    blk_c = _iota((ROW_BLOCK, ROW_BLOCK), 1)
    same_chunk = (blk_r // GLA_CHUNK) == (blk_c // GLA_CHUNK)
    tri = (jnp.where(same_chunk & (blk_c <= blk_r), 1.0, 0.0).astype(BF16),
           jnp.where(same_chunk & (blk_c >= blk_r), 1.0, 0.0).astype(BF16))
    chunks_per_blk = ROW_BLOCK // GLA_CHUNK
    head_of_lane = _iota((GLA_CHUNK, W_GLA), 1) // GLA_DK
    stack_r = _iota((GLA_HEADS * GLA_CHUNK, GLA_CHUNK), 0) % GLA_CHUNK
    stack_c = _iota((GLA_HEADS * GLA_CHUNK, GLA_CHUNK), 1)
    causal = (stack_c <= stack_r, stack_c >= stack_r)

    if latent:
        def cached_context():
            for h in range(DIFF_HEADS):
                k_ref[0:n_ctx, h * DIFF_V:(h + 1) * DIFF_V] = ck_ref[0, 0, h].astype(BF16)
                v_ref[0:n_ctx, h * DIFF_V:(h + 1) * DIFF_V] = cv_ref[0, 0, h].astype(BF16)
            st_ref[0] = st0_ref[0, 0]

        whole_sequence(cached_context)
        pair_lo = (_iota((ROW_BLOCK, W_QK), 1) % (DIFF_QK // 2)) < (DIFF_QK // 4)

        def rope(z, rows):
            cos = jnp.concatenate([cos_ref[rows, :]] * DIFF_HEADS, axis=-1)
            sin = jnp.concatenate([sin_ref[rows, :]] * DIFF_HEADS, axis=-1)
            shift = DIFF_QK // 4
            swapped = jnp.where(pair_lo, pltpu.roll(z, W_QK - shift, 1), pltpu.roll(z, shift, 1))
            return z * cos + swapped * sin

    def modulated_input(r):
        h = _row_rms(x_ref[block_rows(r), :]) * n1_ref[0]
        return (h * (1.0 + mod[1:2, :]) + mod[0:1, :]).astype(BF16)

    def spatial_gating(r):
        for c in range(ROW_BLOCK // SGU_CHUNK):
            local = slice(c * SGU_CHUNK, (c + 1) * SGU_CHUNK)
            u = jax.nn.gelu(proj_ref[local, C_AU:C_AU + SGU_W])
            v = _group_rms(jax.nn.gelu(proj_ref[local, C_AV:C_AV + SGU_W]), SGU_GROUP_W).astype(BF16)
            s = sb_ref[0]
            for g in range(SGU_GROUPS):
                s = s + jnp.where(lane_group == g, _dot(sw_ref[0, g], v), 0.0)
            mix_ref[aligned(r * ROW_BLOCK + c * SGU_CHUNK, SGU_CHUNK), M_A:M_A + SGU_W] = (u * s).astype(BF16)

    def attention_operands(r):
        rows = block_rows(r)
        key_rows = block_rows(r, n_ctx)
        seq, seq_rows = r // n_blk, block_rows(r % n_blk)
        qn = _group_rms(proj_ref[:, C_BQ:C_BQ + W_QK], DIFF_QK) * qg_ref[0]
        kn = _group_rms(proj_ref[:, C_BK:C_BK + W_QK], DIFF_QK) * kg_ref[0]
        vv = proj_ref[:, C_BV:C_BV + W_QK]
        if latent:
            qn, kn = rope(qn, rows), rope(kn, rows)
        else:
            for h in range(DIFF_HEADS):
                for i in range(2):
                    lo = h * DIFF_V + i * DIFF_QK
                    ko_ref[seq, 0, h, i, seq_rows, :] = kn[:, lo:lo + DIFF_QK]
                vo_ref[seq, 0, h, seq_rows, :] = vv[:, h * DIFF_V:(h + 1) * DIFF_V]
        q_ref[rows, :] = (qn * (DIFF_QK ** -0.5)).astype(BF16)
        k_ref[key_rows, :] = kn.astype(BF16)
        v_ref[key_rows, :] = vv.astype(BF16)

    def gla_operands(r):
        rows = block_rows(r)
        gpre = _dot(proj_ref[:, C_LR:C_LR + LANES].astype(BF16), w2c_ref[0]) + gb_ref[0]
        gate = _log_sigmoid(gpre) * (1.0 / GLA_GATE_NORM)
        gq = proj_ref[:, C_CQ:C_CQ + W_GLA] * (GLA_DK ** -0.5)
        gk = proj_ref[:, C_CK:C_CK + W_GLA]
        gv = proj_ref[:, C_CV:C_CV + W_GLA].astype(BF16)
        gv_ref[rows, :] = gv
        gr_ref[rows, :] = proj_ref[:, C_CR:C_CR + W_GLA]
        for d in range(2):
            g = gate[:, d * W_GLA:(d + 1) * W_GLA]
            b = sum(_dot(tri[d], p) for p in _split3(g))
            last = GLA_CHUNK - 1 if d == 0 else 0
            b_last = jnp.concatenate(
                [jnp.broadcast_to(b[c * GLA_CHUNK + last:c * GLA_CHUNK + last + 1, :], (GLA_CHUNK, W_GLA))
                 for c in range(chunks_per_blk)], axis=0)
            q_dec = (gq * jnp.exp(b)).astype(BF16)
            k_inv = (gk * jnp.exp(-b)).astype(BF16)
            gq_ref[d, rows, :] = q_dec
            gke_ref[d, rows, :] = (gk * jnp.exp(b_last - b)).astype(BF16)
            for c in range(chunks_per_blk):
                row = c * GLA_CHUNK + last
                dec_ref[d, r * chunks_per_blk + c] = jnp.exp(b[row:row + 1, :])
                chunk = slice(c * GLA_CHUNK, (c + 1) * GLA_CHUNK)
                qd = q_dec[chunk]
                q_stack = jnp.concatenate(
                    [jnp.where(head_of_lane == h, qd, jnp.zeros_like(qd)) for h in range(GLA_HEADS)], axis=0)
                attn = jnp.where(causal[d], _dot_nt(q_stack, k_inv[chunk]), 0.0)
                spread = _dot(attn.astype(BF16), gv[chunk])
                o = jnp.zeros((GLA_CHUNK, W_GLA), F32)
                for h in range(GLA_HEADS):
                    o = o + jnp.where(head_of_lane == h, spread[h * GLA_CHUNK:(h + 1) * GLA_CHUNK, :], 0.0)
                go_ref[d, aligned(r * ROW_BLOCK + c * GLA_CHUNK, GLA_CHUNK), :] = o

    def project_and_split(r):
        proj_ref[...] = _dot(modulated_input(r), win_ref[0])
        spatial_gating(r)
        attention_operands(r)
        gla_operands(r)

    blocks(project_and_split)

    dl = dl_ref[0]
    lam = (jnp.exp(jnp.sum(dl[0:1] * dl[1:2], axis=-1, keepdims=True))
           - jnp.exp(jnp.sum(dl[2:3] * dl[3:4], axis=-1, keepdims=True)) + lam_init)
    sub0 = (_iota((ROW_BLOCK, DIFF_V), 1) < DIFF_QK)

    def softmax(s):
        e = jnp.exp(s - jnp.max(s, axis=-1, keepdims=True))
        return e, jnp.sum(e, axis=-1, keepdims=True)

    def attn_block(r):
        rows = block_rows(r)
        keys = aligned((r // n_blk) * n_keys, n_keys)
        for h in range(DIFF_HEADS):
            cols = slice(h * DIFF_V, (h + 1) * DIFF_V)
            qh = q_ref[rows, cols]
            kh = k_ref[keys, cols]
            e0, z0 = softmax(_dot_nt(jnp.where(sub0, qh, jnp.zeros_like(qh)), kh))
            e1, z1 = softmax(_dot_nt(jnp.where(sub0, jnp.zeros_like(qh), qh), kh))
            w = e0 / z0 - lam * (e1 / z1)
            o = _dot(w.astype(BF16), v_ref[keys, cols])
            o = _row_rms(o) * dg_ref[0] * (1.0 - lam_init)
            mix_ref[rows, M_B + h * DIFF_V:M_B + (h + 1) * DIFF_V] = o.astype(BF16)

    blocks(attn_block)

    if not latent:
        st_ref[...] = jnp.zeros(st_ref.shape, F32)

    n_chunk = n_tok // GLA_CHUNK
    st_diag = (_iota((W_GLA, W_GLA), 0) // GLA_DV) == (_iota((W_GLA, W_GLA), 1) // GLA_DK)

    def gla_step(c, carry):
        for seq in range(n_par):
            for d in range(2):
                cc = seq * n_chunk + (c if d == 0 else n_chunk - 1 - c)
                rows = pl.ds(pl.multiple_of(cc * GLA_CHUNK, GLA_CHUNK), GLA_CHUNK)
                st = st_ref[seq, d]
                go_ref[d, rows, :] = go_ref[d, rows, :] + _dot_nt(gq_ref[d, rows, :], st.astype(BF16))
                upd = _dot_tn(gv_ref[rows, :], gke_ref[d, rows, :])
                st_ref[seq, d] = dec_ref[d, cc] * st + jnp.where(st_diag, upd, 0.0)
        return carry

    whole_sequence(lambda: lax.fori_loop(0, n_chunk, gla_step, 0))

    if not latent:
        for seq in range(n_par):
            for d in range(2):
                s_full = st_ref[seq, d].T
                for h in range(GLA_HEADS):
                    so_ref[seq, 0, d, h] = s_full[h * GLA_DK:(h + 1) * GLA_DK, h * GLA_DV:(h + 1) * GLA_DV]

    def finish_block(r, out_r):
        rows, out_rows = block_rows(r), block_rows(out_r)
        oc = _group_rms(go_ref[0, rows, :] + go_ref[1, rows, :], GLA_DV) * gg_ref[0]
        oc = oc * jax.nn.silu(gr_ref[rows, :])
        mix_ref[rows, M_C:M_C + W_GLA] = oc.astype(BF16)
        x1 = x_ref[rows, :] + mod[2:3, :] * _dot(mix_ref[rows, :], wout_ref[0])
        xo_ref[out_rows, :] = x1
        hn = _row_rms(x1) * n2_ref[0]
        hn = hn * (1.0 + mod[4:5, :]) + mod[3:4, :]
        idx1, idx2, w1, w2 = _route(hn, rwt_ref, rb_ref)
        slot1, slot2, counts = _local_slots(idx1, idx2)
        perm = _slot_one_hot(slot1, slot2, 1.0, 1.0).astype(BF16)
        _to_row_slabs(hs_ref, 2 * out_r * ROW_BLOCK, _dot(perm, hn.astype(BF16)))
        slot_ref[:, out_rows] = jnp.concatenate([slot1, slot2], axis=0).astype(jnp.int32)
        wt_ref[:, out_rows] = jnp.concatenate([w1, w2], axis=0)
        cnt_ref[out_r] = jnp.broadcast_to(counts, (N_EXPERTS, LANES)).astype(jnp.int32)

    if by_block:
        finish_block(pl.program_id(1), 0)
    else:
        for r in range(n_par * n_blk):
            finish_block(r, r)


def _mixer_call(l, n_tok, n_par, latent, x, mods_all, weights, extras, cache_bufs):
    n_seq = x.shape[0] // n_tok
    n_all = x.shape[0]
    assert n_seq % n_par == 0 and not (latent and n_par > 1)
    n_step_tok = n_par * n_tok
    n_keys = n_step_tok + (extras[0].shape[3] if latent else 0)
    n_chunk = n_step_tok // GLA_CHUNK
    lam_init = 0.8 - 0.6 * math.exp(-0.3 * l)

    single = pl.Buffered(1)

    def layer(arr):
        tail = arr.shape[1:]
        return pl.BlockSpec((1,) + tail, lambda s, *_r, _n=len(tail): (l,) + (0,) * _n, pipeline_mode=single)

    def const(arr):
        return pl.BlockSpec(arr.shape, lambda s, *_r, _n=arr.ndim: (0,) * _n, pipeline_mode=single)

    def tok_spec(width):
        return pl.BlockSpec((n_step_tok, width), lambda s, *_r: (s, 0))

    mod_row = (lambda s: 1 + s) if latent else (lambda s: 0)
    in_specs = [tok_spec(D_MODEL),
                pl.BlockSpec((1, 1, 6, D_MODEL), lambda s, *_r: (l, mod_row(s), 0, 0))]
    in_specs += [layer(w) for w in weights[:N_MIXER_WEIGHTS - 2]] + [const(w) for w in weights[-2:]]
    operands = [x, mods_all] + list(weights)
    if latent:
        ck, cv, st0, cos, sin = extras
        in_specs += [
            pl.BlockSpec((1, 1) + ck.shape[2:], lambda s, *_r: (s, l, 0, 0, 0)),
            pl.BlockSpec((1, 1) + cv.shape[2:], lambda s, *_r: (s, l, 0, 0, 0)),
            pl.BlockSpec((1, 1) + st0.shape[2:], lambda s, *_r: (s, l, 0, 0, 0)),
            const(cos), const(sin),
        ]
        operands += [ck, cv, st0, cos, sin]
    n_in = len(operands)
    in_specs += [pl.BlockSpec(memory_space=pl.ANY)] * len(cache_bufs)
    operands += list(cache_bufs)

    tiles_per_step = n_step_tok // ROW_BLOCK
    out_shape = [
        jax.ShapeDtypeStruct((n_all, D_MODEL), F32),
        jax.ShapeDtypeStruct((2 * n_all * ROW_SLABS, LANES), F32),
        jax.ShapeDtypeStruct((2, n_all), jnp.int32),
        jax.ShapeDtypeStruct((2, n_all), F32),
        jax.ShapeDtypeStruct((n_all // ROW_BLOCK, N_EXPERTS, LANES), jnp.int32),
    ]
    if latent:
        grid = (n_seq, tiles_per_step)
        out_tok, out_tiles = ROW_BLOCK, 1
        at = lambda s, r: s * tiles_per_step + r
    else:
        grid = (n_seq // n_par,)
        out_tok, out_tiles = n_step_tok, tiles_per_step
        at = lambda s: s
    out_specs = [
        pl.BlockSpec((out_tok, D_MODEL), lambda *g: (at(*g), 0)),
        pl.BlockSpec((2 * out_tok * ROW_SLABS, LANES), lambda *g: (at(*g), 0)),
        pl.BlockSpec((2, out_tok), lambda *g: (0, at(*g))),
        pl.BlockSpec((2, out_tok), lambda *g: (0, at(*g))),
        pl.BlockSpec((out_tiles, N_EXPERTS, LANES), lambda *g: (at(*g), 0, 0)),
    ]
    n_shared_out = len(out_shape)
    aliases = {}
    if not latent:
        out_shape += [
            jax.ShapeDtypeStruct((n_seq, DEPTH, DIFF_HEADS, 2, n_tok, DIFF_QK), F32),
            jax.ShapeDtypeStruct((n_seq, DEPTH, DIFF_HEADS, n_tok, DIFF_V), F32),
            jax.ShapeDtypeStruct((n_seq, DEPTH, 2, GLA_HEADS, GLA_DK, GLA_DV), F32),
        ]
        n_lay, lay = (1, l) if cache_bufs else (DEPTH, 0)
        out_specs += [
            pl.BlockSpec((n_par, n_lay, DIFF_HEADS, 2, n_tok, DIFF_QK), lambda s: (s, lay, 0, 0, 0, 0)),
            pl.BlockSpec((n_par, n_lay, DIFF_HEADS, n_tok, DIFF_V), lambda s: (s, lay, 0, 0, 0)),
            pl.BlockSpec((n_par, n_lay, 2, GLA_HEADS, GLA_DK, GLA_DV), lambda s: (s, lay, 0, 0, 0, 0)),
        ]
        aliases = {n_in + j: n_shared_out + j for j in range(len(cache_bufs))}
    scratch = [
        pltpu.VMEM((ROW_BLOCK, D_PROJ_PAD), F32),
        pltpu.VMEM((n_step_tok, D_MODEL), BF16),
        pltpu.VMEM((n_step_tok, W_QK), BF16),
        pltpu.VMEM((n_keys, W_QK), BF16),
        pltpu.VMEM((n_keys, W_QK), BF16),
        pltpu.VMEM((2, n_step_tok, W_GLA), BF16),
        pltpu.VMEM((2, n_step_tok, W_GLA), BF16),
        pltpu.VMEM((n_step_tok, W_GLA), BF16),
        pltpu.VMEM((n_step_tok, W_GLA), F32),
        pltpu.VMEM((2, n_chunk, 1, W_GLA), F32),
        pltpu.VMEM((2, n_step_tok, W_GLA), F32),
        pltpu.VMEM((n_par, 2, W_GLA, W_GLA), F32),
    ]
    return pl.pallas_call(
        functools.partial(_mixer_kernel, n_tok, n_par, latent, len(cache_bufs), lam_init),
        grid=grid,
        in_specs=in_specs,
        out_specs=out_specs,
        out_shape=out_shape,
        scratch_shapes=scratch,
        input_output_aliases=aliases,
        compiler_params=pltpu.CompilerParams(
            dimension_semantics=("arbitrary",) * len(grid), vmem_limit_bytes=MIXER_VMEM_LIMIT),
        name="mixer_latent" if latent else "mixer_context",
    )(*operands)


PAIR_BLOCK = 2 * ROW_BLOCK
COPY_SIZES = tuple(ROW_BLOCK >> k for k in range(ROW_BLOCK.bit_length()))
LARGE_COPY = 64
GATHER_AHEAD = 4
GATHER_SLOTS = GATHER_AHEAD + 1


def _segment_copies(n_rows, make_copy, act):
    def copy_if_set(size):
        @pl.when((n_rows & size) != 0)
        def _():
            act(make_copy(n_rows & (-2 * size), size))

    n_large = COPY_SIZES.index(LARGE_COPY) + 1

    @pl.when(n_rows >= LARGE_COPY)
    def _():
        for size in COPY_SIZES[:n_large]:
            copy_if_set(size)

    for size in COPY_SIZES[n_large:]:
        copy_if_set(size)


def _start(copy):
    copy.start()


def _wait(copy):
    copy.wait()


def _slab_rows(first_row, n_rows, slab):
    return pl.ds(first_row * ROW_SLABS + slab, n_rows, stride=ROW_SLABS)


def _to_row_slabs(ref, first_row, value):
    for s in range(ROW_SLABS):
        ref[_slab_rows(first_row, value.shape[0], s), :] = value[:, s * LANES:(s + 1) * LANES]


def _from_row_slabs(ref, first_row, n_rows):
    return jnp.concatenate([ref[_slab_rows(first_row, n_rows, s), :] for s in range(ROW_SLABS)], axis=-1)


SLAB_PAIR_W = 2 * LANES
N_SLAB_PAIRS = ROW_SLABS // 2


def _slab_pair_cols(g):
    return slice(g * SLAB_PAIR_W, (g + 1) * SLAB_PAIR_W)


def _load_slab_pair(ref, first_row, n_rows, g):
    return jnp.concatenate([ref[_slab_rows(first_row, n_rows, s), :] for s in (2 * g, 2 * g + 1)], axis=-1)


def _row_span(ref, first_row, n_rows):
    return ref.at[pl.ds(pl.multiple_of(first_row * ROW_SLABS, ROW_SLABS), n_rows * ROW_SLABS)]


def _two_streams(n_first_tiles):
    def first(i, *_):
        return (jnp.minimum(i, n_first_tiles - 1), 0)

    def second(i, *_):
        return (jnp.maximum(i - n_first_tiles, 0), 0)

    return first, second


N_EXPERT_TABLES = 10
OUT_SLOTS = 2


def _expert_kernel(n_ctx_tiles, n_tiles_max, te_ref, first_ref, rows_ref, jlo_ref, jhi_ref, cpre_ref, cnt_ref,
                   lofs_ref, tile0_ref, ntile_ref, hs_c_ref, hs_l_ref, w1_ref, w3_ref, w2_ref, ys_ref,
                   xbuf_ref, obuf_ref, w1b_ref, w3b_ref, w2b_ref, sem, out_sem):
    expert = pl.program_id(0)
    n_tiles = tile0_ref[N_EXPERTS]

    def gather(t, act):
        slot = t % GATHER_SLOTS
        e, first = te_ref[t], first_ref[t]
        last = first + rows_ref[t]

        def segment_of(hs_ref, first_tile):
            def body(j, carry):
                k = j * N_EXPERTS + e
                seg_first = cpre_ref[k]
                lo = jnp.maximum(seg_first, first)
                n = jnp.maximum(jnp.minimum(seg_first + cnt_ref[k], last) - lo, 0)
                src = (j - first_tile) * PAIR_BLOCK + lofs_ref[k] + (lo - seg_first)
                dst = slot * ROW_BLOCK + lo - first
                _segment_copies(n, lambda done, size: pltpu.make_async_copy(
                    _row_span(hs_ref, src + done, size), _row_span(xbuf_ref, dst + done, size), sem.at[slot]), act)
                return carry
            return body

        jlo, jhi = jlo_ref[t], jhi_ref[t]
        lax.fori_loop(jnp.minimum(jlo, n_ctx_tiles), jnp.minimum(jhi, n_ctx_tiles), segment_of(hs_c_ref, 0), 0)
        lax.fori_loop(jnp.maximum(jlo, n_ctx_tiles), jnp.maximum(jhi, n_ctx_tiles),
                      segment_of(hs_l_ref, n_ctx_tiles), 0)

    def out_copy(t, oslot):
        return pltpu.make_async_copy(
            _row_span(obuf_ref, oslot * ROW_BLOCK, ROW_BLOCK), _row_span(ys_ref, t * ROW_BLOCK, ROW_BLOCK),
            out_sem.at[oslot])

    @pl.when(expert == 0)
    def _():
        xbuf_ref[...] = jnp.zeros(xbuf_ref.shape, F32)
        for t in range(GATHER_AHEAD):
            gather(t, _start)

    w1b_ref[...] = w1_ref[0, 0].astype(BF16)
    w3b_ref[...] = w3_ref[0, 0].astype(BF16)
    w2b_ref[...] = w2_ref[0, 0].astype(BF16)
    tile0, n_own = tile0_ref[expert], ntile_ref[expert]

    def tile_body(k, carry):
        t = tile0 + k
        slot, oslot = t % GATHER_SLOTS, t % OUT_SLOTS

        @pl.when(t + GATHER_AHEAD < n_tiles)
        def _():
            gather(t + GATHER_AHEAD, _start)

        n_rows = rows_ref[t]
        _segment_copies(n_rows, lambda done, size: pltpu.make_async_copy(
            _row_span(hs_c_ref, done, size), _row_span(xbuf_ref, slot * ROW_BLOCK + done, size), sem.at[slot]), _wait)

        @pl.when(t >= OUT_SLOTS)
        def _():
            out_copy(t, oslot).wait()

        live = _iota((ROW_BLOCK, D_MODEL), 0) < n_rows
        x = jnp.where(live, _from_row_slabs(xbuf_ref, slot * ROW_BLOCK, ROW_BLOCK), 0.0).astype(BF16)
        hid = jax.nn.silu(_dot(x, w1b_ref[...])) * _dot(x, w3b_ref[...])
        _to_row_slabs(obuf_ref, oslot * ROW_BLOCK, _dot(hid.astype(BF16), w2b_ref[...]))
        out_copy(t, oslot).start()
        return carry

    lax.fori_loop(0, n_own, tile_body, 0)

    @pl.when(expert == N_EXPERTS - 1)
    def _():
        for oslot in range(OUT_SLOTS):
            @pl.when(n_tiles > oslot)
            def _():
                out_copy(0, oslot).wait()
        obuf_ref[...] = jnp.zeros(obuf_ref.shape, F32)

        def fill(t, carry):
            out_copy(t, 0).start()
            out_copy(t, 0).wait()
            return carry

        lax.fori_loop(n_tiles, n_tiles_max, fill, 0)


def _expert_call(l, plan, hs_c, hs_l, w1, w3, w2):
    tables = plan["expert_tables"]
    n_tiles_max = tables[0].shape[0]
    n_ctx_tiles = hs_c.shape[0] // (PAIR_BLOCK * ROW_SLABS)

    def weight(shape):
        return pl.BlockSpec((1, 1) + shape, lambda e, *_: (l, e, 0, 0))

    return pl.pallas_call(
        functools.partial(_expert_kernel, n_ctx_tiles, n_tiles_max),
        grid_spec=pltpu.PrefetchScalarGridSpec(
            num_scalar_prefetch=N_EXPERT_TABLES,
            grid=(N_EXPERTS,),
            in_specs=[pl.BlockSpec(memory_space=pl.ANY), pl.BlockSpec(memory_space=pl.ANY),
                      weight((D_MODEL, D_EXPERT)), weight((D_MODEL, D_EXPERT)), weight((D_EXPERT, D_MODEL))],
            out_specs=pl.BlockSpec(memory_space=pl.ANY),
            scratch_shapes=[pltpu.VMEM((GATHER_SLOTS * ROW_BLOCK * ROW_SLABS, LANES), F32),
                            pltpu.VMEM((OUT_SLOTS * ROW_BLOCK * ROW_SLABS, LANES), F32),
                            pltpu.VMEM((D_MODEL, D_EXPERT), BF16), pltpu.VMEM((D_MODEL, D_EXPERT), BF16),
                            pltpu.VMEM((D_EXPERT, D_MODEL), BF16),
                            pltpu.SemaphoreType.DMA((GATHER_SLOTS,)), pltpu.SemaphoreType.DMA((OUT_SLOTS,))],
        ),
        out_shape=jax.ShapeDtypeStruct((n_tiles_max * ROW_BLOCK * ROW_SLABS, LANES), F32),
        compiler_params=pltpu.CompilerParams(
            dimension_semantics=("arbitrary",), vmem_limit_bytes=SMALL_KERNEL_VMEM_LIMIT),
        name="moe_experts",
    )(*tables, hs_c, hs_l, w1, w3, w2)


N_COMBINE_TABLES = 4
COMBINE_TILES_PER_STEP = 2


def _combine_kernel(n_ctx_steps, cnt_ref, cpre_ref, lofs_ref, starts_ref, x_c_ref, x_l_ref, slot_c_ref, slot_l_ref,
                    wt_c_ref, wt_l_ref, mod_ref, ys_ref, xo_c_ref, xo_l_ref, buf_ref, sem):
    step = pl.program_id(0)
    n_tiles = pl.num_programs(0) * COMBINE_TILES_PER_STEP

    def collect(t, act):
        slot = t % GATHER_SLOTS

        def body(e, carry):
            k = t * N_EXPERTS + e
            src, dst = starts_ref[e] + cpre_ref[k], slot * PAIR_BLOCK + lofs_ref[k]
            _segment_copies(cnt_ref[k], lambda done, size: pltpu.make_async_copy(
                _row_span(ys_ref, src + done, size), _row_span(buf_ref, dst + done, size), sem.at[slot]), act)
            return carry

        lax.fori_loop(0, N_EXPERTS, body, 0)

    @pl.when(step == 0)
    def _():
        for t in range(GATHER_AHEAD):
            collect(t, _start)

    gate = mod_ref[0, 0, 5:6, :]

    def finish(part, first_row, x_ref, slot_ref, wt_ref, xo_ref):
        tokens = pl.ds(part * ROW_BLOCK, ROW_BLOCK)
        slots, wts = slot_ref[:, tokens].astype(F32), wt_ref[:, tokens]
        slot1, slot2 = slots[0:1], slots[1:2]
        weight_of_row = jnp.sum(_slot_one_hot(slot1, slot2, wts[0:1], wts[1:2]), axis=1, keepdims=True)
        gather_rows = _slot_one_hot(slot1, slot2, 1.0, 1.0).T.astype(BF16)
        for g in range(N_SLAB_PAIRS):
            cols = _slab_pair_cols(g)
            hi, lo = _split2(_load_slab_pair(buf_ref, first_row, PAIR_BLOCK, g) * weight_of_row)
            y = _dot(gather_rows, hi) + _dot(gather_rows, lo)
            xo_ref[tokens, cols] = x_ref[tokens, cols] + gate[:, cols] * y

    for part in range(COMBINE_TILES_PER_STEP):
        j = step * COMBINE_TILES_PER_STEP + part

        @pl.when(j + GATHER_AHEAD < n_tiles)
        def _():
            collect(j + GATHER_AHEAD, _start)

        slot = j % GATHER_SLOTS
        for piece in range(PAIR_BLOCK // ROW_BLOCK):
            pltpu.make_async_copy(
                _row_span(ys_ref, piece * ROW_BLOCK, ROW_BLOCK),
                _row_span(buf_ref, slot * PAIR_BLOCK + piece * ROW_BLOCK, ROW_BLOCK), sem.at[slot]).wait()
        @pl.when(step < n_ctx_steps)
        def _():
            finish(part, slot * PAIR_BLOCK, x_c_ref, slot_c_ref, wt_c_ref, xo_c_ref)

        @pl.when(step >= n_ctx_steps)
        def _():
            finish(part, slot * PAIR_BLOCK, x_l_ref, slot_l_ref, wt_l_ref, xo_l_ref)


def _combine_call(l, plan, x_c, x_l, slot_c, slot_l, wt_c, wt_l, mods_all, mod_row_of_tile, ys):
    step_rows = COMBINE_TILES_PER_STEP * ROW_BLOCK
    n_steps = (x_c.shape[0] + x_l.shape[0]) // step_rows
    n_ctx_steps = x_c.shape[0] // step_rows
    first, second = _two_streams(n_ctx_steps)

    def lanes(index_map):
        return lambda i, *_: index_map(i)[::-1]

    return pl.pallas_call(
        functools.partial(_combine_kernel, n_ctx_steps),
        grid_spec=pltpu.PrefetchScalarGridSpec(
            num_scalar_prefetch=N_COMBINE_TABLES,
            grid=(n_steps,),
            in_specs=[pl.BlockSpec((step_rows, D_MODEL), first),
                      pl.BlockSpec((step_rows, D_MODEL), second),
                      pl.BlockSpec((2, step_rows), lanes(first)),
                      pl.BlockSpec((2, step_rows), lanes(second)),
                      pl.BlockSpec((2, step_rows), lanes(first)),
                      pl.BlockSpec((2, step_rows), lanes(second)),
                      pl.BlockSpec((1, 1, 6, D_MODEL),
                                   lambda i, *_: (l, mod_row_of_tile(i * COMBINE_TILES_PER_STEP), 0, 0)),
                      pl.BlockSpec(memory_space=pl.ANY)],
            out_specs=[pl.BlockSpec((step_rows, D_MODEL), first),
                       pl.BlockSpec((step_rows, D_MODEL), second)],
            scratch_shapes=[pltpu.VMEM((GATHER_SLOTS * PAIR_BLOCK * ROW_SLABS, LANES), F32),
                            pltpu.SemaphoreType.DMA((GATHER_SLOTS,))],
        ),
        out_shape=[jax.ShapeDtypeStruct(x_c.shape, F32), jax.ShapeDtypeStruct(x_l.shape, F32)],
        compiler_params=pltpu.CompilerParams(
            dimension_semantics=("arbitrary",), vmem_limit_bytes=SMALL_KERNEL_VMEM_LIMIT),
        name="moe_combine",
    )(*plan["combine_tables"], x_c, x_l, slot_c, slot_l, wt_c, wt_l, mods_all, ys)


def _moe_plan(cnt):
    n_tok_tiles = cnt.shape[0]
    n_tiles = n_tok_tiles * PAIR_BLOCK // ROW_BLOCK + N_EXPERTS
    lofs = jnp.cumsum(cnt, axis=1) - cnt
    cpre = jnp.cumsum(cnt, axis=0) - cnt
    counts = jnp.sum(cnt, axis=0)
    padded = (counts + ROW_BLOCK - 1) // ROW_BLOCK * ROW_BLOCK
    ends = jnp.cumsum(padded)
    starts = ends - padded
    tile_start = jnp.arange(n_tiles, dtype=jnp.int32) * ROW_BLOCK
    tile_expert = jnp.minimum(
        jnp.sum((tile_start[:, None] >= ends[None, :]).astype(jnp.int32), axis=1), N_EXPERTS - 1)
    hot = tile_expert[:, None] == jnp.arange(N_EXPERTS, dtype=jnp.int32)[None, :]
    first = tile_start - jnp.sum(jnp.where(hot, starts[None, :], 0), axis=1)
    rows = jnp.clip(jnp.sum(jnp.where(hot, counts[None, :], 0), axis=1) - first, 0, ROW_BLOCK)
    seg_first = jnp.sum(jnp.where(hot[:, None, :], cpre[None, :, :], 0), axis=2)
    seg_rows = jnp.sum(jnp.where(hot[:, None, :], cnt[None, :, :], 0), axis=2)
    overlap = (seg_first < (first + rows)[:, None]) & (seg_first + seg_rows > first[:, None])
    j = jnp.arange(n_tok_tiles, dtype=jnp.int32)[None, :]
    jlo = jnp.min(jnp.where(overlap, j, n_tok_tiles), axis=1)
    jhi = jnp.max(jnp.where(overlap, j + 1, 0), axis=1)
    i32 = lambda a: a.astype(jnp.int32).reshape(-1)
    tile0 = jnp.concatenate([starts, ends[-1:]]) // ROW_BLOCK
    return {
        "expert_tables": tuple(i32(a) for a in (tile_expert, first, rows, jlo, jhi, cpre, cnt, lofs,
                                                tile0, padded // ROW_BLOCK)),
        "combine_tables": tuple(i32(a) for a in (cnt, cpre, lofs, starts)),
    }


def _rope_tables(n_tok):
    n_rows = n_tok // GRID_W
    pos_r = jnp.repeat(jnp.arange(n_rows), GRID_W)
    pos_c = jnp.tile(jnp.arange(GRID_W), n_rows)
    half = DIFF_QK // 2
    nf = half // 2
    freqs = ROPE_BASE ** (-jnp.arange(nf, dtype=F32) / nf)

    def tables(pos):
        ang = pos.astype(F32)[:, None] * freqs
        cos, sin = jnp.cos(ang), jnp.sin(ang)
        return jnp.concatenate([cos, cos], axis=-1), jnp.concatenate([-sin, sin], axis=-1)

    cos_r, sin_r = tables(pos_r)
    cos_c, sin_c = tables(pos_c)
    cos = jnp.concatenate([cos_r, cos_c], axis=-1)
    sin = jnp.concatenate([sin_r, sin_c], axis=-1)
    return jnp.concatenate([cos, cos], axis=-1), jnp.concatenate([sin, sin], axis=-1)


def _mixer_weights(w_in, w_out, sgu_w, sgu_b, q_norm_g, k_norm_g, diff_lambda, diff_norm_g, gla_w2, gla_b,
                   gla_norm_g, norm1_g, norm2_g, router_w, router_bias):
    w_in_pad = jnp.concatenate(
        [w_in.astype(BF16), jnp.zeros(w_in.shape[:2] + (D_PROJ_PAD - w_in.shape[2],), BF16)], axis=-1)
    w2cat = jnp.zeros((DEPTH, LANES, 2 * W_GLA), F32)
    w2cat = w2cat.at[:, 0:GLA_RANK, 0:W_GLA].set(gla_w2[:, 0]).at[:, GLA_RANK:2 * GLA_RANK, W_GLA:].set(gla_w2[:, 1])
    return (
        norm1_g[:, None, :], norm2_g[:, None, :], w_in_pad, w_out.astype(BF16),
        sgu_w.astype(BF16), jnp.repeat(sgu_b.transpose(0, 2, 1), SGU_GROUP_W, axis=2),
        jnp.tile(q_norm_g, (1, W_QK // DIFF_QK))[:, None, :], jnp.tile(k_norm_g, (1, W_QK // DIFF_QK))[:, None, :],
        diff_lambda, diff_norm_g[:, None, :],
        w2cat.astype(BF16), gla_b.reshape(DEPTH, 1, 2 * W_GLA), jnp.tile(gla_norm_g, (1, GLA_HEADS))[:, None, :],
        router_w.T, router_bias[:, None],
    )


def kernel(x_prompt, x_sample, cache_k, cache_v, state_gla, c, c_ctx, w_in, w_out, sgu_w, sgu_b, q_norm_g, k_norm_g,
           diff_lambda, diff_norm_g, gla_w2, gla_b, gla_norm_g, norm1_g, norm2_g, ada_w, ada_b, router_w, router_bias,
           moe_w1, moe_w3, moe_w2):
    n_ctx_seq, ctx_len, _ = x_prompt.shape
    n_lat_seq, lat_len, _ = x_sample.shape
    n_ctx_tok = n_ctx_seq * ctx_len
    n_lat_tok = n_lat_seq * lat_len
    ctx_tiles = n_ctx_tok // ROW_BLOCK
    lat_tiles_per_seq = lat_len // ROW_BLOCK

    n_cond = 1 + n_lat_seq
    cond_t = jnp.zeros((D_MODEL, SUBLANES), F32).at[:, 0].set(c_ctx).at[:, 1:n_cond].set(c.T)
    mods_all = _adaln_call(cond_t, n_cond, ada_w, ada_b)[:, :n_cond].reshape(DEPTH, n_cond, 6, D_MODEL)
    weights = _mixer_weights(w_in, w_out, sgu_w, sgu_b, q_norm_g, k_norm_g, diff_lambda, diff_norm_g, gla_w2, gla_b,
                             gla_norm_g, norm1_g, norm2_g, router_w, router_bias)

    ck_all = cache_k.transpose(0, 1, 2, 4, 3, 5).reshape(cache_k.shape[:3] + (cache_k.shape[4], DIFF_V))
    st_all = jnp.einsum('bldhkv,hg->bldhvgk', state_gla, jnp.eye(GLA_HEADS, dtype=F32)).reshape(
        n_lat_seq, DEPTH, 2, W_GLA, W_GLA)
    cos, sin = _rope_tables(lat_len)
    extras = (ck_all, cache_v, st_all, cos, sin)

    def mod_row_of_tile(i):
        return jnp.where(i < ctx_tiles, 0, 1 + (i - ctx_tiles) // lat_tiles_per_seq)

    x_c = x_prompt.reshape(n_ctx_tok, D_MODEL)
    x_l = x_sample.reshape(n_lat_tok, D_MODEL)
    cache_bufs = ()
    for l in range(DEPTH):
        ctx_par = 1 if l == 0 else CTX_SEQS_PER_STEP
        x1_c, hs_c, slot_c, wt_c, cnt_c, *cache_bufs = _mixer_call(
            l, ctx_len, ctx_par, False, x_c, mods_all, weights, None, tuple(cache_bufs))
        x1_l, hs_l, slot_l, wt_l, cnt_l = _mixer_call(l, lat_len, 1, True, x_l, mods_all, weights, extras, ())
        plan = _moe_plan(jnp.concatenate([cnt_c[:, :, 0], cnt_l[:, :, 0]], axis=0))
        ys = _expert_call(l, plan, hs_c, hs_l, moe_w1, moe_w3, moe_w2)
        x_c, x_l = _combine_call(l, plan, x1_c, x1_l, slot_c, slot_l, wt_c, wt_l, mods_all, mod_row_of_tile, ys)

    new_k, new_v, new_s = cache_bufs
    return (x_c.reshape(x_prompt.shape), x_l.reshape(x_sample.shape), new_k, new_v, new_s)
```
